```python
import math
import jax, jax.numpy as jnp
from jax import lax
import numpy as np


D_MODEL = 1024
BATCH = 8
SEQ = 4096
DEPTH = 4

N_MIXERS = 4
HEAD_DIM = 64
RMS_EPS = 1e-6
D_FF = 4 * D_MODEL
NEG_INF = -1e30

MOBA_HEADS = D_MODEL // HEAD_DIM
MOBA_BLOCK = 256
MOBA_TOPK = 3
MOBA_Q_CHUNK = 16

DIL_PAIRS = ((128, 1), (512, 4), (2048, 16))
DIL_HEADS_PER_GROUP = 8
DIL_BAND = 128

MLA_HEADS = 16
MLA_Q_RANK = 768
MLA_KV_RANK = 256
MLA_NOPE = 64
MLA_ROPE = 32
MLA_V = 64
ROPE_THETA = 10000.0
MLA_Q_BLOCK = 128

SWA_Q_HEADS = 16
SWA_KV_HEADS = 2
SWA_WINDOW = 128
SWA_BLOCK = 128

kernel_name = 'hybrid_moba_dilated_mla_swa_trunk'


def rmsnorm(x, g):
    xf = x.astype(jnp.float32)
    y = xf * lax.rsqrt(jnp.mean(xf * xf, axis=-1, keepdims=True) + RMS_EPS)
    return (y * g.astype(jnp.float32)).astype(x.dtype)


def alibi_slopes(n_heads):
    return 2.0 ** (-8.0 * jnp.arange(1, n_heads + 1, dtype=jnp.float32) / n_heads)


def heads_first(t, n_heads):
    B, S, _ = t.shape
    return t.reshape(B, S, n_heads, -1).transpose(0, 2, 1, 3)


def rope(x, pos):
    R = x.shape[-1]
    inv = ROPE_THETA ** (-jnp.arange(0, R, 2, dtype=jnp.float32) / R)
    ang = pos.astype(jnp.float32)[:, None] * inv[None, :]
    cos, sin = jnp.cos(ang)[:, None, :], jnp.sin(ang)[:, None, :]
    x1, x2 = jnp.split(x.astype(jnp.float32), 2, axis=-1)
    return jnp.concatenate([x1 * cos - x2 * sin, x1 * sin + x2 * cos], axis=-1).astype(x.dtype)


def moba_attention(h, w_qkv, w_o):
    B, S, _ = h.shape
    H, Dh, BLK, CQ = MOBA_HEADS, HEAD_DIM, MOBA_BLOCK, MOBA_Q_CHUNK
    nb = -(-S // BLK)
    Sp = nb * BLK
    pad = ((0, 0), (0, 0), (0, Sp - S), (0, 0))
    q, k, v = (jnp.pad(heads_first(t, H), pad) for t in jnp.split(h @ w_qkv, 3, axis=-1))
    scale = Dh ** -0.5
    slopes = alibi_slopes(H)[:, None, None]
    k_mean = jnp.mean(k.reshape(B, H, nb, BLK, Dh).astype(jnp.float32), axis=3)
    gate = jnp.einsum('bhsd,bhnd->bhsn', q.astype(jnp.float32), k_mean)
    q_block = jnp.arange(Sp) // BLK
    fully_past = jnp.arange(nb)[None, :] < q_block[:, None]
    gate = jnp.where(fully_past, gate, NEG_INF)
    n_sel = min(MOBA_TOPK, nb)
    _, sel = lax.top_k(gate, n_sel)
    sel_valid = sel < q_block[:, None]
    k_blocks = k.reshape(B, H, nb, BLK, Dh)
    v_blocks = v.reshape(B, H, nb, BLK, Dh)
    b_ix = jnp.arange(B)[:, None, None, None]
    h_ix = jnp.arange(H)[None, :, None, None]
    offs = jnp.arange(BLK)
    nc = Sp // CQ

    def to_chunks(t):
        return jnp.moveaxis(t.reshape(B, H, nc, CQ, *t.shape[3:]), 2, 0)

    def chunk(args):
        c, q_c, sel_c, valid_c = args
        t = c * CQ + jnp.arange(CQ)
        own = (c * CQ) // BLK
        kg = k_blocks[b_ix, h_ix, sel_c].reshape(B, H, CQ, n_sel * BLK, Dh)
        vg = v_blocks[b_ix, h_ix, sel_c].reshape(B, H, CQ, n_sel * BLK, Dh)
        s_g = (sel_c[..., None] * BLK + offs).reshape(B, H, CQ, n_sel * BLK)
        m_g = jnp.broadcast_to(valid_c[..., None], (B, H, CQ, n_sel, BLK)).reshape(B, H, CQ, n_sel * BLK)
        l_g = (jnp.einsum('bhqd,bhqkd->bhqk', q_c, kg).astype(jnp.float32) * scale
               - slopes * (t[:, None] - s_g).astype(jnp.float32))
        l_g = jnp.where(m_g, l_g, NEG_INF)
        ko = lax.dynamic_slice_in_dim(k, own * BLK, BLK, axis=2)
        vo = lax.dynamic_slice_in_dim(v, own * BLK, BLK, axis=2)
        d_o = t[:, None] - (own * BLK + offs)[None, :]
        l_o = (jnp.einsum('bhqd,bhkd->bhqk', q_c, ko).astype(jnp.float32) * scale
               - slopes * d_o.astype(jnp.float32))
        l_o = jnp.where(d_o >= 0, l_o, NEG_INF)
        p = jax.nn.softmax(jnp.concatenate([l_g, l_o], axis=-1), axis=-1).astype(v.dtype)
        return (jnp.einsum('bhqk,bhqkd->bhqd', p[..., :n_sel * BLK], vg)
                + jnp.einsum('bhqk,bhkd->bhqd', p[..., n_sel * BLK:], vo))

    o = lax.map(chunk, (jnp.arange(nc), to_chunks(q), to_chunks(sel), to_chunks(sel_valid)))
    o = jnp.moveaxis(o, 0, 2).reshape(B, H, Sp, Dh)[:, :, :S]
    return o.transpose(0, 2, 1, 3).reshape(B, S, H * Dh) @ w_o


def dilated_group(q, k, v, window, dil, slopes):
    B, S, Hg, Dh = q.shape
    n_pts = window // dil
    Sd = S // dil
    nblk = -(-Sd // DIL_BAND)
    Sdp = nblk * DIL_BAND
    scale = Dh ** -0.5

    def to_sub(t):
        t = t.reshape(B, Sd, dil, Hg, Dh).transpose(0, 2, 3, 1, 4)
        t = jnp.pad(t, ((0, 0), (0, 0), (0, 0), (0, Sdp - Sd), (0, 0)))
        return t.reshape(B, dil, Hg, nblk, DIL_BAND, Dh)

    def band(t):
        prev = jnp.pad(t[:, :, :, :-1], ((0, 0), (0, 0), (0, 0), (1, 0), (0, 0), (0, 0)))
        return jnp.concatenate([prev, t], axis=4)

    qs = to_sub(q)
    kb, vb = band(to_sub(k)), band(to_sub(v))
    logits = jnp.einsum('brhnqd,brhnkd->brhnqk', qs, kb).astype(jnp.float32) * scale
    qi = jnp.arange(DIL_BAND)[:, None]
    ki = jnp.arange(2 * DIL_BAND)[None, :]
    diff = qi + DIL_BAND - ki
    key_idx = jnp.arange(nblk)[:, None, None] * DIL_BAND - DIL_BAND + ki
    mask = (diff >= 0) & (diff <= n_pts) & (key_idx >= 0)
    logits = logits - slopes[:, None, None, None] * (dil * diff).astype(jnp.float32)
    logits = jnp.where(mask, logits, NEG_INF)
    m = jnp.max(logits, axis=-1, keepdims=True)
    e = jnp.exp(logits - m)
    den = jnp.sum(e, axis=-1, keepdims=True)
    o = jnp.einsum('brhnqk,brhnkd->brhnqd', e.astype(v.dtype), vb).astype(jnp.float32) / den
    lse = (m + jnp.log(den))[..., 0]
    o = o.reshape(B, dil, Hg, Sdp, Dh)[:, :, :, :Sd].transpose(0, 3, 1, 2, 4).reshape(B, S, Hg, Dh)
    lse = lse.reshape(B, dil, Hg, Sdp)[..., :Sd].transpose(0, 3, 1, 2).reshape(B, S, Hg)
    return o, lse


def dilated_attention(h, w_qkv, w_o):
    B, S, _ = h.shape
    G, Hg, Dh = len(DIL_PAIRS), DIL_HEADS_PER_GROUP, HEAD_DIM
    qkv = (h @ w_qkv).reshape(B, S, G, 3, Hg, Dh)
    slopes = alibi_slopes(G * Hg).reshape(G, Hg)
    outs, lses = [], []
    for g, (window, dil) in enumerate(DIL_PAIRS):
        o_g, lse_g = dilated_group(qkv[:, :, g, 0], qkv[:, :, g, 1], qkv[:, :, g, 2], window, dil, slopes[g])
        outs.append(o_g)
        lses.append(lse_g)
    w = jax.nn.softmax(jnp.stack(lses, axis=0), axis=0)[..., None]
    merged = jnp.sum(w * jnp.stack(outs, axis=0), axis=0).astype(h.dtype)
    return merged.reshape(B, S, Hg * Dh) @ w_o


def causal_block_attention(q, k, v, scale):
    B, H, S, Dk = q.shape
    QB = MLA_Q_BLOCK
    nqb = S // QB
    qb = jnp.moveaxis(q.reshape(B, H, nqb, QB, Dk), 2, 0)
    s_pos = jnp.arange(S)

    def blk(args):
        n, q_n = args
        t = n * QB + jnp.arange(QB)
        lg = jnp.einsum('bhqd,bhkd->bhqk', q_n, k).astype(jnp.float32) * scale
        lg = jnp.where(s_pos[None, :] <= t[:, None], lg, NEG_INF)
        p = jax.nn.softmax(lg, axis=-1).astype(v.dtype)
        return jnp.einsum('bhqk,bhkd->bhqd', p, v)

    o = lax.map(blk, (jnp.arange(nqb), qb))
    return jnp.moveaxis(o, 0, 2).reshape(B, H, S, v.shape[-1])


def mla_attention(h, w_dkv, q_norm, w_uq, kv_norm, w_ukv, w_o):
    B, S, _ = h.shape
    H = MLA_HEADS
    c_q, c_kv, k_rope = jnp.split(h @ w_dkv, [MLA_Q_RANK, MLA_Q_RANK + MLA_KV_RANK], axis=-1)
    q = (rmsnorm(c_q, q_norm) @ w_uq).reshape(B, S, H, MLA_NOPE + MLA_ROPE)
    kv = (rmsnorm(c_kv, kv_norm) @ w_ukv).reshape(B, S, H, MLA_NOPE + MLA_V)
    pos = jnp.arange(S)
    q = jnp.concatenate([q[..., :MLA_NOPE], rope(q[..., MLA_NOPE:], pos)], axis=-1)
    k_r = jnp.broadcast_to(rope(k_rope[:, :, None, :], pos), (B, S, H, MLA_ROPE))
    k = jnp.concatenate([kv[..., :MLA_NOPE], k_r], axis=-1)
    v = kv[..., MLA_NOPE:]
    o = causal_block_attention(q.transpose(0, 2, 1, 3), k.transpose(0, 2, 1, 3), v.transpose(0, 2, 1, 3),
                               (MLA_NOPE + MLA_ROPE) ** -0.5)
    return o.transpose(0, 2, 1, 3).reshape(B, S, H * MLA_V) @ w_o


def swa_sink_attention(h, w_qkv, sinks, w_o):
    B, S, _ = h.shape
    Hq, Hkv, Dh, BLK = SWA_Q_HEADS, SWA_KV_HEADS, HEAD_DIM, SWA_BLOCK
    G = Hq // Hkv
    nb = S // BLK
    q, k, v = jnp.split(h @ w_qkv, [Hq * Dh, (Hq + Hkv) * Dh], axis=-1)
    q = q.reshape(B, nb, BLK, Hkv, G, Dh).transpose(0, 3, 4, 1, 2, 5)

    def band(t):
        t = t.reshape(B, nb, BLK, Hkv, Dh).transpose(0, 3, 1, 2, 4)
        prev = jnp.pad(t[:, :, :-1], ((0, 0), (0, 0), (1, 0), (0, 0), (0, 0)))
        return jnp.concatenate([prev, t], axis=3)

    kb, vb = band(k), band(v)
    logits = jnp.einsum('bkgnqd,bkntd->bkgnqt', q, kb).astype(jnp.float32) * Dh ** -0.5
    qi = jnp.arange(BLK)[:, None]
    ki = jnp.arange(2 * BLK)[None, :]
    diff = qi + BLK - ki
    key_idx = jnp.arange(nb)[:, None, None] * BLK - BLK + ki
    mask = (diff >= 0) & (diff < SWA_WINDOW) & (key_idx >= 0)
    slopes = alibi_slopes(Hq).reshape(Hkv, G)[:, :, None, None, None]
    logits = jnp.where(mask, logits - slopes * diff.astype(jnp.float32), NEG_INF)
    sink = jnp.broadcast_to(sinks.astype(jnp.float32).reshape(Hkv, G, 1, 1, 1), logits.shape[:-1] + (1,))
    p = jax.nn.softmax(jnp.concatenate([logits, sink], axis=-1), axis=-1)[..., :-1].astype(v.dtype)
    o = jnp.einsum('bkgnqt,bkntd->bkgnqd', p, vb)
    o = o.transpose(0, 3, 4, 1, 2, 5).reshape(B, S, Hq * Dh)
    return o @ w_o


def squared_relu_mlp(h, w_up, w_down):
    return jnp.square(jax.nn.relu(h @ w_up)) @ w_down


def setup_inputs(seed: int = 0) -> dict:
    key = jax.random.key(seed)
    keys = jax.random.split(key, 64)
    counter = [0]

    def nk():
        counter[0] += 1
        return keys[counter[0] - 1]

    def w(shape):
        return jax.random.normal(nk(), shape, jnp.float32) * (shape[0] ** -0.5)

    def gain(n):
        return 1.0 + 0.02 * jax.random.normal(nk(), (n,), jnp.float32)

    p = {}
    p['x'] = jax.random.normal(nk(), (BATCH, SEQ, D_MODEL), jnp.float32)
    p['l0_attn_norm'] = gain(D_MODEL)
    p['l0_w_qkv'] = w((D_MODEL, 3 * MOBA_HEADS * HEAD_DIM))
    p['l0_w_o'] = w((MOBA_HEADS * HEAD_DIM, D_MODEL))
    p['l0_mlp_norm'] = gain(D_MODEL)
    p['l0_w_up'] = w((D_MODEL, D_FF))
    p['l0_w_down'] = w((D_FF, D_MODEL))
    p['l1_attn_norm'] = gain(D_MODEL)
    p['l1_w_qkv'] = w((D_MODEL, len(DIL_PAIRS) * 3 * DIL_HEADS_PER_GROUP * HEAD_DIM))
    p['l1_w_o'] = w((DIL_HEADS_PER_GROUP * HEAD_DIM, D_MODEL))
    p['l1_mlp_norm'] = gain(D_MODEL)
    p['l1_w_up'] = w((D_MODEL, D_FF))
    p['l1_w_down'] = w((D_FF, D_MODEL))
    p['l2_attn_norm'] = gain(D_MODEL)
    p['l2_w_dkv'] = w((D_MODEL, MLA_Q_RANK + MLA_KV_RANK + MLA_ROPE))
    p['l2_q_norm'] = gain(MLA_Q_RANK)
    p['l2_w_uq'] = w((MLA_Q_RANK, MLA_HEADS * (MLA_NOPE + MLA_ROPE)))
    p['l2_kv_norm'] = gain(MLA_KV_RANK)
    p['l2_w_ukv'] = w((MLA_KV_RANK, MLA_HEADS * (MLA_NOPE + MLA_V)))
    p['l2_w_o'] = w((MLA_HEADS * MLA_V, D_MODEL))
    p['l2_mlp_norm'] = gain(D_MODEL)
    p['l2_w_up'] = w((D_MODEL, D_FF))
    p['l2_w_down'] = w((D_FF, D_MODEL))
    p['l3_attn_norm'] = gain(D_MODEL)
    p['l3_w_qkv'] = w((D_MODEL, (SWA_Q_HEADS + 2 * SWA_KV_HEADS) * HEAD_DIM))
    p['l3_sinks'] = 0.5 * jax.random.normal(nk(), (SWA_Q_HEADS,), jnp.float32)
    p['l3_w_o'] = w((SWA_Q_HEADS * HEAD_DIM, D_MODEL))
    p['l3_mlp_norm'] = gain(D_MODEL)
    p['l3_w_up'] = w((D_MODEL, D_FF))
    p['l3_w_down'] = w((D_FF, D_MODEL))
    p['final_norm'] = gain(D_MODEL)
    return p


def reference(x,
              l0_attn_norm, l0_w_qkv, l0_w_o, l0_mlp_norm, l0_w_up, l0_w_down,
              l1_attn_norm, l1_w_qkv, l1_w_o, l1_mlp_norm, l1_w_up, l1_w_down,
              l2_attn_norm, l2_w_dkv, l2_q_norm, l2_w_uq, l2_kv_norm, l2_w_ukv, l2_w_o,
              l2_mlp_norm, l2_w_up, l2_w_down,
              l3_attn_norm, l3_w_qkv, l3_sinks, l3_w_o, l3_mlp_norm, l3_w_up, l3_w_down,
              final_norm):
    mixers = (moba_attention, dilated_attention, mla_attention, swa_sink_attention)
    mixer_params = ((l0_w_qkv, l0_w_o),
                    (l1_w_qkv, l1_w_o),
                    (l2_w_dkv, l2_q_norm, l2_w_uq, l2_kv_norm, l2_w_ukv, l2_w_o),
                    (l3_w_qkv, l3_sinks, l3_w_o))
    attn_norms = (l0_attn_norm, l1_attn_norm, l2_attn_norm, l3_attn_norm)
    mlp_norms = (l0_mlp_norm, l1_mlp_norm, l2_mlp_norm, l3_mlp_norm)
    w_ups = (l0_w_up, l1_w_up, l2_w_up, l3_w_up)
    w_downs = (l0_w_down, l1_w_down, l2_w_down, l3_w_down)
    h = x
    for i in range(DEPTH):
        mix = mixers[i % N_MIXERS]
        h = h + mix(rmsnorm(h, attn_norms[i]), *mixer_params[i])
        h = h + squared_relu_mlp(rmsnorm(h, mlp_norms[i]), w_ups[i], w_downs[i])
    return rmsnorm(h, final_norm)
```

```python
import functools

import jax
import jax.numpy as jnp
from jax import lax
from jax.experimental import pallas as pl
from jax.experimental.pallas import tpu as pltpu

F32 = jnp.float32
BF16 = jnp.bfloat16

D_MODEL = 1024
HEAD_DIM = 64
RMS_EPS = 1e-6
D_FF = 4 * D_MODEL
NEG_INF = -1e30

MOBA_HEADS = 16
MOBA_BLOCK = 256
MOBA_TOPK = 3

DIL_PAIRS = ((128, 1), (512, 4), (2048, 16))
DIL_HEADS_PER_GROUP = 8
DIL_BAND = 128

MLA_HEADS = 16
MLA_Q_RANK = 768
MLA_KV_RANK = 256
MLA_NOPE = 64
MLA_ROPE = 32
MLA_V = 64
ROPE_THETA = 10000.0

SWA_Q_HEADS = 16
SWA_KV_HEADS = 2
SWA_WINDOW = 128
SWA_BLOCK = 128

LANES = 128
ROW_TILE = 512
FF_TILE = 512
VMEM_LIMIT = 56 * 1024 * 1024


def _params(semantics):
    return pltpu.CompilerParams(dimension_semantics=semantics, vmem_limit_bytes=VMEM_LIMIT)


def _resident(shape):
    return pl.BlockSpec(shape, lambda *_: (0,) * len(shape), pipeline_mode=pl.Buffered(1))


def _alibi_slopes(n_heads):
    return 2.0 ** (-8.0 * jnp.arange(1, n_heads + 1, dtype=F32) / n_heads)


def _rms(x, g):
    return x * lax.rsqrt(jnp.mean(x * x, axis=-1, keepdims=True) + RMS_EPS) * g


def _dot(a, b):
    return jnp.dot(a, b, preferred_element_type=F32)


def _dot_nt(a, b):
    return lax.dot_general(a, b, (((1,), (1,)), ((), ())), preferred_element_type=F32)


def _norm_proj_kernel(h_ref, g_ref, w_ref, o_ref, *, col_tile):
    n = _rms(h_ref[...], g_ref[...]).astype(BF16)
    for c in range(o_ref.shape[1] // col_tile):
        cols = slice(c * col_tile, (c + 1) * col_tile)
        o_ref[:, cols] = _dot(n, w_ref[:, cols]).astype(o_ref.dtype)


def _norm_proj(h, g, w, col_tile=512):
    m, d = h.shape
    n_out = w.shape[1]
    return pl.pallas_call(
        functools.partial(_norm_proj_kernel, col_tile=col_tile),
        grid=(m // ROW_TILE,),
        in_specs=[pl.BlockSpec((ROW_TILE, d), lambda i: (i, 0)),
                  _resident((1, d)),
                  _resident((d, n_out))],
        out_specs=pl.BlockSpec((ROW_TILE, n_out), lambda i: (i, 0)),
        out_shape=jax.ShapeDtypeStruct((m, n_out), BF16),
        compiler_params=_params(("parallel",)),
        name="norm_proj",
    )(h, g.reshape(1, d), w)


def _mlp_tail(h1, g_ref, wup_ref, wdn_ref, gf_ref, out_ref):
    n = _rms(h1, g_ref[...]).astype(BF16)
    acc = jnp.zeros_like(h1)
    for c in range(wup_ref.shape[1] // FF_TILE):
        cols = slice(c * FF_TILE, (c + 1) * FF_TILE)
        u = jnp.square(jnp.maximum(_dot(n, wup_ref[:, cols]), 0.0)).astype(BF16)
        acc = acc + _dot(u, wdn_ref[cols, :])
    out = h1 + acc
    if gf_ref is not None:
        out = _rms(out, gf_ref[...])
    out_ref[...] = out


def _post_kernel(h_ref, a_ref, wo_ref, g_ref, wup_ref, wdn_ref, *rest):
    gf_ref, out_ref = (rest[0], rest[1]) if len(rest) == 2 else (None, rest[0])
    h1 = h_ref[...] + _dot(a_ref[...], wo_ref[...])
    _mlp_tail(h1, g_ref, wup_ref, wdn_ref, gf_ref, out_ref)


def _post_merge_kernel(h_ref, o0_ref, o1_ref, o2_ref, l0_ref, l1_ref, l2_ref,
                       wo_ref, g_ref, wup_ref, wdn_ref, out_ref):
    l0, l1, l2 = l0_ref[...], l1_ref[...], l2_ref[...]
    mx = jnp.maximum(jnp.maximum(l0, l1), l2)
    e0, e1, e2 = jnp.exp(l0 - mx), jnp.exp(l1 - mx), jnp.exp(l2 - mx)
    merged = (e0 * o0_ref[...] + e1 * o1_ref[...] + e2 * o2_ref[...]) / (e0 + e1 + e2)
    h1 = h_ref[...] + _dot(merged.astype(BF16), wo_ref[...])
    _mlp_tail(h1, g_ref, wup_ref, wdn_ref, None, out_ref)


def _post(h, attn_ins, w_o, g, w_up, w_down, g_final=None):
    m, d = h.shape
    row = lambda width: pl.BlockSpec((ROW_TILE, width), lambda i: (i, 0))
    in_specs = [row(d)] + [row(a.shape[1]) for a in attn_ins]
    in_specs += [_resident(w_o.shape), _resident((1, d)), _resident(w_up.shape), _resident(w_down.shape)]
    args = [h, *attn_ins, w_o, g.reshape(1, d), w_up, w_down]
    body = _post_kernel if len(attn_ins) == 1 else _post_merge_kernel
    if g_final is not None:
        in_specs.append(_resident((1, d)))
        args.append(g_final.reshape(1, d))
    return pl.pallas_call(
        body,
        grid=(m // ROW_TILE,),
        in_specs=in_specs,
        out_specs=row(d),
        out_shape=jax.ShapeDtypeStruct((m, d), F32),
        compiler_params=_params(("parallel",)),
        name="post_mlp",
    )(*args)


def _softmax_step(logits, v, m_ref, l_ref, acc_ref):
    m_prev = m_ref[...]
    m_new = jnp.maximum(m_prev, jnp.max(logits, axis=1, keepdims=True))
    alpha = jnp.exp(m_prev - m_new)
    p = jnp.exp(logits - jnp.tile(m_new, (1, logits.shape[1] // LANES)))
    l_ref[...] = alpha * l_ref[...] + jnp.sum(p, axis=1, keepdims=True)
    acc_ref[...] = alpha * acc_ref[...] + _dot(p.astype(BF16), v)
    m_ref[...] = m_new


def _head_half(x, hh):
    lane = lax.broadcasted_iota(jnp.int32, x.shape, 1)
    keep = (lane < HEAD_DIM) if hh == 0 else (lane >= HEAD_DIM)
    return jnp.where(keep, x, jnp.zeros_like(x))


def _pair_select(a0, a1):
    lane = lax.broadcasted_iota(jnp.int32, a0.shape, 1)
    return jnp.where(lane < HEAD_DIM, a0, a1)


def _moba_kernel(slopes_ref, q_ref, k_ref, v_ref, o_ref, kmean_ref, m_ref, l_ref, acc_ref, *, n_blocks):
    blk = MOBA_BLOCK
    hp = pl.program_id(1)
    i = pl.program_id(2)

    @pl.when(i == 0)
    def _():
        for j in range(n_blocks):
            kj = k_ref[0, j * blk:(j + 1) * blk, :].astype(F32)
            kmean_ref[j:j + 1, :] = jnp.mean(kj, axis=0, keepdims=True)

    q2 = q_ref[0]
    row = lax.broadcasted_iota(jnp.int32, (blk, blk), 0)
    col = lax.broadcasted_iota(jnp.int32, (blk, blk), 1)
    rel = (row - col).astype(F32)
    blk_lane = lax.broadcasted_iota(jnp.int32, (blk, n_blocks), 1)
    scale = HEAD_DIM ** -0.5
    outs = []
    for hh in range(2):
        slope = slopes_ref[2 * hp + hh]
        q = _head_half(q2, hh)
        kmean = _head_half(kmean_ref[...], hh)
        gate = lax.dot_general(q.astype(F32), kmean, (((1,), (1,)), ((), ())),
                               precision=lax.Precision.HIGHEST, preferred_element_type=F32)
        fully_past = blk_lane < i
        work = jnp.where(fully_past, gate, NEG_INF)
        chosen = jnp.zeros((blk, n_blocks), jnp.bool_)
        for _ in range(min(MOBA_TOPK, n_blocks)):
            best = jnp.max(work, axis=1, keepdims=True)
            first = jnp.min(jnp.where(work == best, blk_lane, n_blocks), axis=1, keepdims=True)
            pick = blk_lane == first
            chosen = jnp.logical_or(chosen, pick)
            work = jnp.where(pick, -jnp.inf, work)
        block_dist = ((i - blk_lane) * blk).astype(F32)
        row_term = jnp.where(jnp.logical_and(chosen, fully_past), slope * block_dist, -NEG_INF)
        alibi = slope * rel

        m_ref[...] = jnp.full(m_ref.shape, NEG_INF, F32)
        l_ref[...] = jnp.zeros(l_ref.shape, F32)
        acc_ref[...] = jnp.zeros(acc_ref.shape, F32)

        own = pl.multiple_of(i * blk, blk)
        s = _dot_nt(q, k_ref[0, pl.ds(own, blk), :])
        logits = jnp.where(rel >= 0, s * scale - alibi, NEG_INF)
        _softmax_step(logits, v_ref[0, pl.ds(own, blk), :], m_ref, l_ref, acc_ref)

        def past_block(j, carry):
            start = pl.multiple_of(j * blk, blk)
            s = _dot_nt(q, k_ref[0, pl.ds(start, blk), :])
            term = jnp.sum(jnp.where(blk_lane == j, row_term, 0.0), axis=1, keepdims=True)
            logits = s * scale - alibi - term
            _softmax_step(logits, v_ref[0, pl.ds(start, blk), :], m_ref, l_ref, acc_ref)
            return carry

        lax.fori_loop(0, i, past_block, 0)
        outs.append(acc_ref[...] / l_ref[...])
    o_ref[0] = _pair_select(outs[0], outs[1]).astype(o_ref.dtype)


def _moba_attention(qkv, b, s):
    n_blocks = s // MOBA_BLOCK
    n_pairs = MOBA_HEADS // 2
    return pl.pallas_call(
        functools.partial(_moba_kernel, n_blocks=n_blocks),
        grid=(b, n_pairs, n_blocks),
        in_specs=[pl.BlockSpec(memory_space=pltpu.SMEM),
                  pl.BlockSpec((1, MOBA_BLOCK, LANES), lambda bi, hp, i: (bi, i, hp)),
                  pl.BlockSpec((1, s, LANES), lambda bi, hp, i: (bi, 0, n_pairs + hp)),
                  pl.BlockSpec((1, s, LANES), lambda bi, hp, i: (bi, 0, 2 * n_pairs + hp))],
        out_specs=pl.BlockSpec((1, MOBA_BLOCK, LANES), lambda bi, hp, i: (bi, i, hp)),
        out_shape=jax.ShapeDtypeStruct((b, s, MOBA_HEADS * HEAD_DIM), BF16),
        scratch_shapes=[pltpu.VMEM((n_blocks, LANES), F32),
                        pltpu.VMEM((MOBA_BLOCK, LANES), F32),
                        pltpu.VMEM((MOBA_BLOCK, LANES), F32),
                        pltpu.VMEM((MOBA_BLOCK, LANES), F32)],
        compiler_params=_params(("parallel", "parallel", "arbitrary")),
        name="moba_attention",
    )(_alibi_slopes(MOBA_HEADS), qkv, qkv, qkv)


def _band_kernel_body(q_ref, kp_ref, kc_ref, vp_ref, vc_ref, n, heads, head_logits, finish):
    band = q_ref.shape[1]
    row = lax.broadcasted_iota(jnp.int32, (band, 2 * band), 0)
    col = lax.broadcasted_iota(jnp.int32, (band, 2 * band), 1)
    diff = row + band - col
    for h in heads:
        q = q_ref[0, :, h.q_cols]
        k = jnp.concatenate([kp_ref[0, :, h.kv_cols], kc_ref[0, :, h.kv_cols]], axis=0)
        v = jnp.concatenate([vp_ref[0, :, h.kv_cols], vc_ref[0, :, h.kv_cols]], axis=0)
        logits = head_logits(h, _dot_nt(q, k), diff, n)
        finish(h, logits, v)


class _Head:
    def __init__(self, index, q_cols, kv_cols):
        self.index, self.q_cols, self.kv_cols = index, q_cols, kv_cols


def _dil_kernel(slopes_ref, q_ref, kp_ref, kc_ref, vp_ref, vc_ref, o_ref, lse_ref, *, n_pts, dil, group):
    n = pl.program_id(2)
    scale = HEAD_DIM ** -0.5
    heads = [_Head(h, slice(h * HEAD_DIM, (h + 1) * HEAD_DIM), slice(h * HEAD_DIM, (h + 1) * HEAD_DIM))
             for h in range(DIL_HEADS_PER_GROUP)]

    def head_logits(h, s, diff, n):
        slope = slopes_ref[group * DIL_HEADS_PER_GROUP + h.index]
        band = s.shape[0]
        col = lax.broadcasted_iota(jnp.int32, s.shape, 1)
        mask = (diff >= 0) & (diff <= n_pts) & ((n > 0) | (col >= band))
        return jnp.where(mask, s * scale - slope * (dil * diff).astype(F32), NEG_INF)

    def finish(h, logits, v):
        m = jnp.max(logits, axis=1, keepdims=True)
        e = jnp.exp(logits - m)
        den = jnp.sum(e, axis=1, keepdims=True)
        o = _dot(e.astype(BF16), v) / den
        o_ref[0, :, h.q_cols] = o
        lse_ref[0, :, h.q_cols] = jnp.broadcast_to(m + jnp.log(den), o.shape)

    _band_kernel_body(q_ref, kp_ref, kc_ref, vp_ref, vc_ref, n, heads, head_logits, finish)


def _dilated_group(qkv, b, s, group, window, dil):
    width = DIL_HEADS_PER_GROUP * HEAD_DIM
    n_sub = s // dil
    assert n_sub % DIL_BAND == 0
    feat = qkv.shape[-1]
    tiles_per_pos = feat // width
    view = qkv.reshape(b, n_sub, dil * feat)

    def col(part):
        return lambda bi, r, n: (bi, n, r * tiles_per_pos + group * 3 + part)

    def col_prev(part):
        return lambda bi, r, n: (bi, jnp.maximum(n - 1, 0), r * tiles_per_pos + group * 3 + part)

    blk = (1, DIL_BAND, width)
    out_spec = pl.BlockSpec(blk, lambda bi, r, n: (bi, n, r))
    out_sds = jax.ShapeDtypeStruct((b, n_sub, dil * width), F32)
    o, lse = pl.pallas_call(
        functools.partial(_dil_kernel, n_pts=window // dil, dil=dil, group=group),
        grid=(b, dil, n_sub // DIL_BAND),
        in_specs=[pl.BlockSpec(memory_space=pltpu.SMEM),
                  pl.BlockSpec(blk, col(0)),
                  pl.BlockSpec(blk, col_prev(1)), pl.BlockSpec(blk, col(1)),
                  pl.BlockSpec(blk, col_prev(2)), pl.BlockSpec(blk, col(2))],
        out_specs=[out_spec, out_spec],
        out_shape=[out_sds, out_sds],
        compiler_params=_params(("parallel", "parallel", "arbitrary")),
        name=f"dilated_attention_g{group}",
    )(_alibi_slopes(len(DIL_PAIRS) * DIL_HEADS_PER_GROUP), view, view, view, view, view)
    return o.reshape(b * s, width), lse.reshape(b * s, width)


def _mla_proj_kernel(h_ref, g_ref, wd_ref, gq_ref, gkv_ref, wuq_ref, wuk_ref, wuv_ref,
                     cos_ref, sin_up_ref, sin_dn_ref, q_ref, k_ref, v_ref):
    n = _rms(h_ref[...], g_ref[...]).astype(BF16)
    c = _dot(n, wd_ref[...])
    nq = _rms(c[:, :MLA_Q_RANK], gq_ref[...]).astype(BF16)
    nkv = _rms(c[:, MLA_Q_RANK:MLA_Q_RANK + MLA_KV_RANK], gkv_ref[...]).astype(BF16)
    k_rope = c[:, MLA_Q_RANK + MLA_KV_RANK:]
    cos, sin_up, sin_dn = cos_ref[...], sin_up_ref[...], sin_dn_ref[...]
    half = MLA_ROPE // 2

    def rope(x):
        return x * cos + pltpu.roll(x, half, 1) * sin_up + pltpu.roll(x, LANES - half, 1) * sin_dn

    kr = rope(k_rope)
    for h in range(MLA_HEADS):
        cols = slice(h * LANES, (h + 1) * LANES)
        q_ref[:, cols] = rope(_dot(nq, wuq_ref[:, cols])).astype(BF16)
        k_ref[:, cols] = (_dot(nkv, wuk_ref[:, cols]) + kr).astype(BF16)
    v_ref[...] = _dot(nkv, wuv_ref[...]).astype(BF16)


def _mla_proj(h, g, w_dkv, q_norm, w_uq, kv_norm, w_ukv, s):
    m, d = h.shape
    qk = MLA_NOPE + MLA_ROPE
    wd = jnp.concatenate([w_dkv[:, :MLA_Q_RANK + MLA_KV_RANK],
                          jnp.zeros((d, MLA_NOPE), F32), w_dkv[:, MLA_Q_RANK + MLA_KV_RANK:],
                          jnp.zeros((d, LANES - qk), F32)], axis=1).astype(BF16)
    wuq = jnp.pad(w_uq.reshape(MLA_Q_RANK, MLA_HEADS, qk), ((0, 0), (0, 0), (0, LANES - qk)))
    wuq = wuq.reshape(MLA_Q_RANK, MLA_HEADS * LANES).astype(BF16)
    w_ukv = w_ukv.reshape(MLA_KV_RANK, MLA_HEADS, MLA_NOPE + MLA_V)
    wuk = jnp.pad(w_ukv[:, :, :MLA_NOPE], ((0, 0), (0, 0), (0, LANES - MLA_NOPE)))
    wuk = wuk.reshape(MLA_KV_RANK, MLA_HEADS * LANES).astype(BF16)
    wuv = w_ukv[:, :, MLA_NOPE:].reshape(MLA_KV_RANK, MLA_HEADS * MLA_V).astype(BF16)
    half = MLA_ROPE // 2
    inv = ROPE_THETA ** (-jnp.arange(0, MLA_ROPE, 2, dtype=F32) / MLA_ROPE)
    ang = jnp.arange(s).astype(F32)[:, None] * inv[None, :]
    cos, sin = jnp.cos(ang), jnp.sin(ang)
    zeros = lambda w: jnp.zeros((s, w), F32)
    cos_t = jnp.concatenate([jnp.ones((s, MLA_NOPE), F32), cos, cos, zeros(LANES - qk)], axis=1)
    sin_up = jnp.concatenate([zeros(MLA_NOPE + half), sin, zeros(LANES - qk)], axis=1)
    sin_dn = jnp.concatenate([zeros(MLA_NOPE), -sin, zeros(half + LANES - qk)], axis=1)

    tiles_per_seq = s // ROW_TILE
    row = lambda width: pl.BlockSpec((ROW_TILE, width), lambda i: (i, 0))
    table = pl.BlockSpec((ROW_TILE, LANES), lambda i: (i % tiles_per_seq, 0))
    return pl.pallas_call(
        _mla_proj_kernel,
        grid=(m // ROW_TILE,),
        in_specs=[row(d), _resident((1, d)), _resident(wd.shape),
                  _resident((1, MLA_Q_RANK)), _resident((1, MLA_KV_RANK)),
                  _resident(wuq.shape), _resident(wuk.shape), _resident(wuv.shape),
                  table, table, table],
        out_specs=[row(MLA_HEADS * LANES), row(MLA_HEADS * LANES), row(MLA_HEADS * MLA_V)],
        out_shape=[jax.ShapeDtypeStruct((m, MLA_HEADS * LANES), BF16),
                   jax.ShapeDtypeStruct((m, MLA_HEADS * LANES), BF16),
                   jax.ShapeDtypeStruct((m, MLA_HEADS * MLA_V), BF16)],
        compiler_params=_params(("parallel",)),
        name="mla_proj",
    )(h, g.reshape(1, d), wd, q_norm.reshape(1, -1), kv_norm.reshape(1, -1), wuq, wuk, wuv,
      cos_t, sin_up, sin_dn)


MLA_TILE = 256


def _mla_kernel(q_ref, k_ref, v_ref, o_ref, m_ref, l_ref, acc_ref):
    t = MLA_TILE
    i = pl.program_id(2)
    scale = (MLA_NOPE + MLA_ROPE) ** -0.5
    row = lax.broadcasted_iota(jnp.int32, (t, t), 0)
    col = lax.broadcasted_iota(jnp.int32, (t, t), 1)
    outs = []
    for hh in range(2):
        cols = slice(hh * LANES, (hh + 1) * LANES)
        q = q_ref[0, :, cols]
        m_ref[...] = jnp.full(m_ref.shape, NEG_INF, F32)
        l_ref[...] = jnp.zeros(l_ref.shape, F32)
        acc_ref[...] = jnp.zeros(acc_ref.shape, F32)

        own = pl.multiple_of(i * t, t)
        s = _dot_nt(q, k_ref[0, pl.ds(own, t), cols])
        _softmax_step(jnp.where(col <= row, s * scale, NEG_INF), v_ref[0, pl.ds(own, t), :],
                      m_ref, l_ref, acc_ref)

        def past_block(j, carry):
            start = pl.multiple_of(j * t, t)
            s = _dot_nt(q, k_ref[0, pl.ds(start, t), cols])
            _softmax_step(s * scale, v_ref[0, pl.ds(start, t), :], m_ref, l_ref, acc_ref)
            return carry

        lax.fori_loop(0, i, past_block, 0)
        outs.append(acc_ref[...] / l_ref[...])
    o_ref[0] = _pair_select(outs[0], outs[1]).astype(o_ref.dtype)


def _mla_attention(q, k, v, b, s):
    n_pairs = MLA_HEADS // 2
    pair = lambda bi, hp, i: (bi, 0, hp)
    return pl.pallas_call(
        _mla_kernel,
        grid=(b, n_pairs, s // MLA_TILE),
        in_specs=[pl.BlockSpec((1, MLA_TILE, 2 * LANES), lambda bi, hp, i: (bi, i, hp)),
                  pl.BlockSpec((1, s, 2 * LANES), pair),
                  pl.BlockSpec((1, s, LANES), pair)],
        out_specs=pl.BlockSpec((1, MLA_TILE, LANES), lambda bi, hp, i: (bi, i, hp)),
        out_shape=jax.ShapeDtypeStruct((b, s, MLA_HEADS * MLA_V), BF16),
        scratch_shapes=[pltpu.VMEM((MLA_TILE, LANES), F32)] * 3,
        compiler_params=_params(("parallel", "parallel", "arbitrary")),
        name="mla_attention",
    )(q, k, v)


def _swa_kernel(slopes_ref, sinks_ref, q_ref, kp_ref, kc_ref, vp_ref, vc_ref, o_ref):
    n = pl.program_id(1)
    scale = HEAD_DIM ** -0.5
    group = SWA_Q_HEADS // SWA_KV_HEADS
    heads = [_Head(h, slice(h * HEAD_DIM, (h + 1) * HEAD_DIM),
                   slice((h // group) * HEAD_DIM, (h // group + 1) * HEAD_DIM))
             for h in range(SWA_Q_HEADS)]

    def head_logits(h, s, diff, n):
        band = s.shape[0]
        col = lax.broadcasted_iota(jnp.int32, s.shape, 1)
        mask = (diff >= 0) & (diff < SWA_WINDOW) & ((n > 0) | (col >= band))
        return jnp.where(mask, s * scale - slopes_ref[h.index] * diff.astype(F32), NEG_INF)

    def finish(h, logits, v):
        m = jnp.maximum(jnp.max(logits, axis=1, keepdims=True), sinks_ref[h.index])
        e = jnp.exp(logits - m)
        den = jnp.sum(e, axis=1, keepdims=True) + jnp.exp(sinks_ref[h.index] - m)
        o_ref[0, :, h.q_cols] = (_dot(e.astype(BF16), v) / den).astype(o_ref.dtype)

    _band_kernel_body(q_ref, kp_ref, kc_ref, vp_ref, vc_ref, n, heads, head_logits, finish)


def _swa_attention(qkv, sinks, b, s):
    q_width = SWA_Q_HEADS * HEAD_DIM
    kv_width = SWA_KV_HEADS * HEAD_DIM
    k_tile = q_width // kv_width
    cur = lambda t: (lambda bi, n: (bi, n, t))
    prev = lambda t: (lambda bi, n: (bi, jnp.maximum(n - 1, 0), t))
    kv_blk = (1, SWA_BLOCK, kv_width)
    return pl.pallas_call(
        _swa_kernel,
        grid=(b, s // SWA_BLOCK),
        in_specs=[pl.BlockSpec(memory_space=pltpu.SMEM), pl.BlockSpec(memory_space=pltpu.SMEM),
                  pl.BlockSpec((1, SWA_BLOCK, q_width), cur(0)),
                  pl.BlockSpec(kv_blk, prev(k_tile)), pl.BlockSpec(kv_blk, cur(k_tile)),
                  pl.BlockSpec(kv_blk, prev(k_tile + 1)), pl.BlockSpec(kv_blk, cur(k_tile + 1))],
        out_specs=pl.BlockSpec((1, SWA_BLOCK, q_width), cur(0)),
        out_shape=jax.ShapeDtypeStruct((b, s, q_width), BF16),
        compiler_params=_params(("parallel", "arbitrary")),
        name="swa_attention",
    )(_alibi_slopes(SWA_Q_HEADS), sinks.astype(F32), qkv, qkv, qkv, qkv, qkv)


def kernel(x, l0_attn_norm, l0_w_qkv, l0_w_o, l0_mlp_norm, l0_w_up, l0_w_down, l1_attn_norm, l1_w_qkv, l1_w_o, l1_mlp_norm, l1_w_up, l1_w_down, l2_attn_norm, l2_w_dkv, l2_q_norm, l2_w_uq, l2_kv_norm, l2_w_ukv, l2_w_o, l2_mlp_norm, l2_w_up, l2_w_down, l3_attn_norm, l3_w_qkv, l3_sinks, l3_w_o, l3_mlp_norm, l3_w_up, l3_w_down, final_norm):
    b, s, d = x.shape
    bf = lambda w: w.astype(BF16)
    h = x.reshape(b * s, d)

    qkv = _norm_proj(h, l0_attn_norm, bf(l0_w_qkv))
    a = _moba_attention(qkv.reshape(b, s, -1), b, s).reshape(b * s, -1)
    h = _post(h, (a,), bf(l0_w_o), l0_mlp_norm, bf(l0_w_up), bf(l0_w_down))

    qkv = _norm_proj(h, l1_attn_norm, bf(l1_w_qkv)).reshape(b, s, -1)
    groups = [_dilated_group(qkv, b, s, g, window, dil) for g, (window, dil) in enumerate(DIL_PAIRS)]
    h = _post(h, tuple(o for o, _ in groups) + tuple(l for _, l in groups),
              bf(l1_w_o), l1_mlp_norm, bf(l1_w_up), bf(l1_w_down))

    q, k, v = _mla_proj(h, l2_attn_norm, l2_w_dkv, l2_q_norm, l2_w_uq, l2_kv_norm, l2_w_ukv, s)
    a = _mla_attention(q.reshape(b, s, -1), k.reshape(b, s, -1), v.reshape(b, s, -1), b, s).reshape(b * s, -1)
    h = _post(h, (a,), bf(l2_w_o), l2_mlp_norm, bf(l2_w_up), bf(l2_w_down))

    qkv = _norm_proj(h, l3_attn_norm, bf(l3_w_qkv), col_tile=256)
    a = _swa_attention(qkv.reshape(b, s, -1), l3_sinks, b, s).reshape(b * s, -1)
    h = _post(h, (a,), bf(l3_w_o), l3_mlp_norm, bf(l3_w_up), bf(l3_w_down), g_final=final_norm)
    return h.reshape(b, s, d)
```

```python
import functools

import jax
import jax.numpy as jnp
from jax import lax
from jax.experimental import pallas as pl
from jax.experimental.pallas import tpu as pltpu

F32 = jnp.float32
BF16 = jnp.bfloat16

D_MODEL = 1024
HEAD_DIM = 64
RMS_EPS = 1e-6
D_FF = 4 * D_MODEL
NEG_INF = -1e30

MOBA_HEADS = 16
MOBA_BLOCK = 256
MOBA_TOPK = 3

DIL_PAIRS = ((128, 1), (512, 4), (2048, 16))
DIL_HEADS_PER_GROUP = 8
DIL_BAND = 128

MLA_HEADS = 16
MLA_Q_RANK = 768
MLA_KV_RANK = 256
MLA_NOPE = 64
MLA_ROPE = 32
MLA_V = 64
ROPE_THETA = 10000.0
MLA_TILE = 256
MLA_HEADS_PER_STEP = 4
MOBA_HEADS_PER_STEP = 4

SWA_Q_HEADS = 16
SWA_KV_HEADS = 2
SWA_WINDOW = 128
SWA_BLOCK = 128

LANES = 128
ROW_TILE = 512
FF_TILE = 512
VMEM_LIMIT = 56 * 1024 * 1024
LOG2E = 1.4426950408889634


def _params(semantics):
    return pltpu.CompilerParams(dimension_semantics=semantics, vmem_limit_bytes=VMEM_LIMIT)


def _resident(shape):
    return pl.BlockSpec(shape, lambda *_: (0,) * len(shape), pipeline_mode=pl.Buffered(1))


def _alibi_slopes(n_heads):
    return 2.0 ** (-8.0 * jnp.arange(1, n_heads + 1, dtype=F32) / n_heads)


def _rms(x, g):
    return x * lax.rsqrt(jnp.mean(x * x, axis=-1, keepdims=True) + RMS_EPS) * g


def _dot(a, b):
    return jnp.dot(a, b, preferred_element_type=F32)


def _dot_nt(a, b):
    return lax.dot_general(a, b, (((1,), (1,)), ((), ())), preferred_element_type=F32)


def _norm_proj_kernel(h_ref, g_ref, w_ref, o_ref, *, col_tile):
    n = _rms(h_ref[...], g_ref[...]).astype(BF16)
    for c in range(o_ref.shape[1] // col_tile):
        cols = slice(c * col_tile, (c + 1) * col_tile)
        o_ref[:, cols] = _dot(n, w_ref[:, cols]).astype(o_ref.dtype)


def _norm_proj(h, g, w, col_tile=512):
    m, d = h.shape
    n_out = w.shape[1]
    return pl.pallas_call(
        functools.partial(_norm_proj_kernel, col_tile=col_tile),
        grid=(m // ROW_TILE,),
        in_specs=[pl.BlockSpec((ROW_TILE, d), lambda i: (i, 0)),
                  _resident((1, d)),
                  _resident((d, n_out))],
        out_specs=pl.BlockSpec((ROW_TILE, n_out), lambda i: (i, 0)),
        out_shape=jax.ShapeDtypeStruct((m, n_out), BF16),
        compiler_params=_params(("parallel",)),
        name="norm_proj",
    )(h, g.reshape(1, d), w)


def _mlp_tail(h1, g_ref, wup_ref, wdn_ref, gf_ref, out_ref):
    n = _rms(h1, g_ref[...]).astype(BF16)
    acc = jnp.zeros_like(h1)
    for c in range(wup_ref.shape[1] // FF_TILE):
        cols = slice(c * FF_TILE, (c + 1) * FF_TILE)
        u = jnp.square(jnp.maximum(_dot(n, wup_ref[:, cols]), 0.0)).astype(BF16)
        acc = acc + _dot(u, wdn_ref[cols, :])
    out = h1 + acc
    if gf_ref is not None:
        out = _rms(out, gf_ref[...])
    out_ref[...] = out


def _post_kernel(h_ref, a_ref, wo_ref, g_ref, wup_ref, wdn_ref, *rest):
    gf_ref, out_ref = (rest[0], rest[1]) if len(rest) == 2 else (None, rest[0])
    h1 = h_ref[...] + _dot(a_ref[...], wo_ref[...])
    _mlp_tail(h1, g_ref, wup_ref, wdn_ref, gf_ref, out_ref)


def _post_merge_kernel(h_ref, o0_ref, o1_ref, o2_ref, l0_ref, l1_ref, l2_ref,
                       wo_ref, g_ref, wup_ref, wdn_ref, out_ref):
    l0, l1, l2 = l0_ref[...], l1_ref[...], l2_ref[...]
    mx = jnp.maximum(jnp.maximum(l0, l1), l2)
    e0, e1, e2 = jnp.exp(l0 - mx), jnp.exp(l1 - mx), jnp.exp(l2 - mx)
    merged = (e0 * o0_ref[...] + e1 * o1_ref[...] + e2 * o2_ref[...]) / (e0 + e1 + e2)
    h1 = h_ref[...] + _dot(merged.astype(BF16), wo_ref[...])
    _mlp_tail(h1, g_ref, wup_ref, wdn_ref, None, out_ref)


def _post(h, attn_ins, w_o, g, w_up, w_down, g_final=None):
    m, d = h.shape
    row = lambda width: pl.BlockSpec((ROW_TILE, width), lambda i: (i, 0))
    in_specs = [row(d)] + [row(a.shape[1]) for a in attn_ins]
    in_specs += [_resident(w_o.shape), _resident((1, d)), _resident(w_up.shape), _resident(w_down.shape)]
    args = [h, *attn_ins, w_o, g.reshape(1, d), w_up, w_down]
    body = _post_kernel if len(attn_ins) == 1 else _post_merge_kernel
    if g_final is not None:
        in_specs.append(_resident((1, d)))
        args.append(g_final.reshape(1, d))
    return pl.pallas_call(
        body,
        grid=(m // ROW_TILE,),
        in_specs=in_specs,
        out_specs=row(d),
        out_shape=jax.ShapeDtypeStruct((m, d), F32),
        compiler_params=_params(("parallel",)),
        name="post_mlp",
    )(*args)


class _SoftmaxState:
    def __init__(self, m_ref, l_ref, acc_ref):
        self.m_ref, self.l_ref, self.acc_ref = m_ref, l_ref, acc_ref

    @staticmethod
    def scratch(n_heads, dv, tq):
        return [pltpu.VMEM((n_heads, 1, tq), F32), pltpu.VMEM((n_heads, 1, tq), F32),
                pltpu.VMEM((n_heads, dv, tq), F32)]

    def init(self):
        self.m_ref[...] = jnp.full(self.m_ref.shape, NEG_INF, F32)
        self.l_ref[...] = jnp.zeros(self.l_ref.shape, F32)
        self.acc_ref[...] = jnp.zeros(self.acc_ref.shape, F32)

    def step(self, hh, x, c, v_t, shift=None):
        m = self.m_ref[hh]
        m_cur = jnp.max(x, axis=0, keepdims=True)
        if shift is not None:
            m_cur = m_cur - shift
        m_new = jnp.maximum(m, m_cur)
        alpha = jnp.exp2((m - m_new) * c)
        p = jnp.exp2((x - (m_new if shift is None else m_new + shift)) * c)
        self.m_ref[hh] = m_new
        self.l_ref[hh] = alpha * self.l_ref[hh] + jnp.sum(p, axis=0, keepdims=True)
        self.acc_ref[hh] = alpha * self.acc_ref[hh] + _dot(v_t, p.astype(BF16))

    def normalized(self, hh):
        return self.acc_ref[hh] / self.l_ref[hh]


def _sweep_blocks(n_past, n_heads, score_fn, own_fn, consume_fn, buf_a, buf_b):
    def produce(buf, block):
        for hh in range(n_heads):
            buf[hh] = score_fn(hh, block)

    def consume(buf, block):
        for hh in range(n_heads):
            consume_fn(hh, block, buf[hh])

    produce(buf_b, n_past)
    produce(buf_a, 0)
    for hh in range(n_heads):
        own_fn(hh, buf_b[hh])

    def pair(p, carry):
        first = 2 * p
        produce(buf_b, first + 1)
        consume(buf_a, first)
        produce(buf_a, jnp.minimum(first + 2, n_past - 1))
        consume(buf_b, first + 1)
        return carry

    lax.fori_loop(0, n_past // 2, pair, 0)

    @pl.when(n_past % 2 == 1)
    def _():
        consume(buf_a, n_past - 1)


def _transpose_bf16(x):
    return x.astype(F32).T.astype(BF16)


def _keep_head_rows(x_t, half):
    r = lax.broadcasted_iota(jnp.int32, x_t.shape, 0)
    keep = (r < HEAD_DIM) if half == 0 else (r >= HEAD_DIM)
    return jnp.where(keep, x_t, jnp.zeros_like(x_t))


def _moba_kernel(slopes_ref, q_ref, k_ref, v_ref, o_ref, kmean_ref, vt_ref, term_ref, alibi_ref, qh_ref,
                 buf_a, buf_b, m_ref, l_ref, acc_ref, *, n_blocks):
    blk = MOBA_BLOCK
    n_heads = MOBA_HEADS_PER_STEP
    group = pl.program_id(1)
    i = pl.program_id(2)
    state = _SoftmaxState(m_ref, l_ref, acc_ref)

    @pl.when(i == 0)
    def _():
        for j in range(n_blocks):
            rows = slice(j * blk, (j + 1) * blk)
            kmean_ref[j:j + 1, :] = jnp.mean(k_ref[0, rows, :].astype(F32), axis=0, keepdims=True)
            vt_ref[:, rows] = _transpose_bf16(v_ref[0, rows, :])

    state.init()
    scale = HEAD_DIM ** -0.5
    c = scale * LOG2E
    q_t = q_ref[0].astype(F32).T
    key = lax.broadcasted_iota(jnp.int32, (blk, blk), 0)
    qry = lax.broadcasted_iota(jnp.int32, (blk, blk), 1)
    rel = (qry - key).astype(F32)
    blk_idx = lax.broadcasted_iota(jnp.int32, (n_blocks, blk), 0)
    fully_past = blk_idx < i
    block_dist = ((i - blk_idx) * blk).astype(F32)
    lane_tile = lambda hh: slice((hh // 2) * LANES, (hh // 2 + 1) * LANES)

    for hh in range(n_heads):
        slope_s = slopes_ref[n_heads * group + hh] / scale
        q_h = _keep_head_rows(q_t[lane_tile(hh), :], hh % 2)
        gate = lax.dot_general(kmean_ref[:, lane_tile(hh)], q_h, (((1,), (0,)), ((), ())),
                               precision=lax.Precision.HIGHEST, preferred_element_type=F32)
        work = jnp.where(fully_past, gate, NEG_INF)
        chosen = jnp.zeros((n_blocks, blk), jnp.bool_)
        for _ in range(min(MOBA_TOPK, n_blocks)):
            best = jnp.max(work, axis=0, keepdims=True)
            first = jnp.min(jnp.where(work == best, blk_idx, n_blocks), axis=0, keepdims=True)
            pick = blk_idx == first
            chosen = jnp.logical_or(chosen, pick)
            work = jnp.where(pick, -jnp.inf, work)
        term_ref[hh] = jnp.where(jnp.logical_and(chosen, fully_past), slope_s * block_dist, -NEG_INF)
        qh_ref[hh] = q_h.astype(BF16)
        alibi_ref[hh] = slope_s * rel

    def rows_of(block):
        return pl.ds(pl.multiple_of(block * blk, blk), blk)

    def scores(hh, block):
        return _dot(k_ref[0, rows_of(block), lane_tile(hh)], qh_ref[hh])

    def v_t(hh, block):
        return vt_ref[hh * HEAD_DIM:(hh + 1) * HEAD_DIM, rows_of(block)]

    def own_block(hh, s):
        state.step(hh, jnp.where(rel >= 0, s - alibi_ref[hh], NEG_INF), c, v_t(hh, i))

    def past_block(hh, block, s):
        state.step(hh, s - alibi_ref[hh], c, v_t(hh, block), shift=term_ref[hh, pl.ds(block, 1), :])

    _sweep_blocks(i, n_heads, scores, own_block, past_block, buf_a, buf_b)
    o_t = jnp.concatenate([state.normalized(hh) for hh in range(n_heads)], axis=0)
    o_ref[0] = o_t.T.astype(o_ref.dtype)


def _moba_attention(qkv, b, s):
    n_blocks = s // MOBA_BLOCK
    n_heads = MOBA_HEADS_PER_STEP
    width = n_heads * HEAD_DIM
    n_groups = MOBA_HEADS // n_heads
    score_buf = pltpu.VMEM((n_heads, MOBA_BLOCK, MOBA_BLOCK), F32)
    return pl.pallas_call(
        functools.partial(_moba_kernel, n_blocks=n_blocks),
        grid=(b, n_groups, n_blocks),
        in_specs=[pl.BlockSpec(memory_space=pltpu.SMEM),
                  pl.BlockSpec((1, MOBA_BLOCK, width), lambda bi, g, i: (bi, i, g)),
                  pl.BlockSpec((1, s, width), lambda bi, g, i: (bi, 0, n_groups + g)),
                  pl.BlockSpec((1, s, width), lambda bi, g, i: (bi, 0, 2 * n_groups + g))],
        out_specs=pl.BlockSpec((1, MOBA_BLOCK, width), lambda bi, g, i: (bi, i, g)),
        out_shape=jax.ShapeDtypeStruct((b, s, MOBA_HEADS * HEAD_DIM), BF16),
        scratch_shapes=[pltpu.VMEM((n_blocks, width), F32),
                        pltpu.VMEM((width, s), BF16),
                        pltpu.VMEM((n_heads, n_blocks, MOBA_BLOCK), F32),
                        pltpu.VMEM((n_heads, MOBA_BLOCK, MOBA_BLOCK), F32),
                        pltpu.VMEM((n_heads, LANES, MOBA_BLOCK), BF16),
                        score_buf, score_buf,
                        *_SoftmaxState.scratch(n_heads, HEAD_DIM, MOBA_BLOCK)],
        compiler_params=_params(("parallel", "parallel", "arbitrary")),
        name="moba_attention",
    )(_alibi_slopes(MOBA_HEADS), qkv, qkv, qkv)


def _band_kernel_body(q_ref, kp_ref, kc_ref, vp_ref, vc_ref, n, heads, head_logits, finish):
    band = q_ref.shape[1]
    row = lax.broadcasted_iota(jnp.int32, (band, 2 * band), 0)
    col = lax.broadcasted_iota(jnp.int32, (band, 2 * band), 1)
    diff = row + band - col
    for h in heads:
        q = q_ref[0, :, h.q_cols]
        k = jnp.concatenate([kp_ref[0, :, h.kv_cols], kc_ref[0, :, h.kv_cols]], axis=0)
        v = jnp.concatenate([vp_ref[0, :, h.kv_cols], vc_ref[0, :, h.kv_cols]], axis=0)
        logits = head_logits(h, _dot_nt(q, k), diff, n)
        finish(h, logits, v)


class _Head:
    def __init__(self, index, q_cols, kv_cols):
        self.index, self.q_cols, self.kv_cols = index, q_cols, kv_cols


def _dil_kernel(slopes_ref, q_ref, kp_ref, kc_ref, vp_ref, vc_ref, o_ref, lse_ref, *, n_pts, dil, group):
    n = pl.program_id(2)
    scale = HEAD_DIM ** -0.5
    heads = [_Head(h, slice(h * HEAD_DIM, (h + 1) * HEAD_DIM), slice(h * HEAD_DIM, (h + 1) * HEAD_DIM))
             for h in range(DIL_HEADS_PER_GROUP)]

    def head_logits(h, s, diff, n):
        slope = slopes_ref[group * DIL_HEADS_PER_GROUP + h.index]
        band = s.shape[0]
        col = lax.broadcasted_iota(jnp.int32, s.shape, 1)
        mask = (diff >= 0) & (diff <= n_pts) & ((n > 0) | (col >= band))
        return jnp.where(mask, s * scale - slope * (dil * diff).astype(F32), NEG_INF)

    def finish(h, logits, v):
        m = jnp.max(logits, axis=1, keepdims=True)
        e = jnp.exp(logits - m)
        den = jnp.sum(e, axis=1, keepdims=True)
        o = _dot(e.astype(BF16), v) / den
        o_ref[0, :, h.q_cols] = o
        lse_ref[0, :, h.q_cols] = jnp.broadcast_to(m + jnp.log(den), o.shape)

    _band_kernel_body(q_ref, kp_ref, kc_ref, vp_ref, vc_ref, n, heads, head_logits, finish)


def _dilated_group(qkv, b, s, group, window, dil):
    width = DIL_HEADS_PER_GROUP * HEAD_DIM
    n_sub = s // dil
    assert n_sub % DIL_BAND == 0
    feat = qkv.shape[-1]
    tiles_per_pos = feat // width
    view = qkv.reshape(b, n_sub, dil * feat)

    def col(part):
        return lambda bi, r, n: (bi, n, r * tiles_per_pos + group * 3 + part)

    def col_prev(part):
        return lambda bi, r, n: (bi, jnp.maximum(n - 1, 0), r * tiles_per_pos + group * 3 + part)

    blk = (1, DIL_BAND, width)
    out_spec = pl.BlockSpec(blk, lambda bi, r, n: (bi, n, r))
    out_sds = jax.ShapeDtypeStruct((b, n_sub, dil * width), F32)
    o, lse = pl.pallas_call(
        functools.partial(_dil_kernel, n_pts=window // dil, dil=dil, group=group),
        grid=(b, dil, n_sub // DIL_BAND),
        in_specs=[pl.BlockSpec(memory_space=pltpu.SMEM),
                  pl.BlockSpec(blk, col(0)),
                  pl.BlockSpec(blk, col_prev(1)), pl.BlockSpec(blk, col(1)),
                  pl.BlockSpec(blk, col_prev(2)), pl.BlockSpec(blk, col(2))],
        out_specs=[out_spec, out_spec],
        out_shape=[out_sds, out_sds],
        compiler_params=_params(("parallel", "parallel", "arbitrary")),
        name=f"dilated_attention_g{group}",
    )(_alibi_slopes(len(DIL_PAIRS) * DIL_HEADS_PER_GROUP), view, view, view, view, view)
    return o.reshape(b * s, width), lse.reshape(b * s, width)


def _mla_proj_kernel(h_ref, g_ref, wd_ref, gq_ref, gkv_ref, wuq_ref, wuk_ref, wuv_ref,
                     cos_ref, sin_up_ref, sin_dn_ref, q_ref, k_ref, v_ref):
    n = _rms(h_ref[...], g_ref[...]).astype(BF16)
    c = _dot(n, wd_ref[...])
    nq = _rms(c[:, :MLA_Q_RANK], gq_ref[...]).astype(BF16)
    nkv = _rms(c[:, MLA_Q_RANK:MLA_Q_RANK + MLA_KV_RANK], gkv_ref[...]).astype(BF16)
    k_rope = c[:, MLA_Q_RANK + MLA_KV_RANK:]
    cos, sin_up, sin_dn = cos_ref[...], sin_up_ref[...], sin_dn_ref[...]
    half = MLA_ROPE // 2

    def rope(x):
        return x * cos + pltpu.roll(x, half, 1) * sin_up + pltpu.roll(x, LANES - half, 1) * sin_dn

    kr = rope(k_rope)
    for h in range(MLA_HEADS):
        cols = slice(h * LANES, (h + 1) * LANES)
        q_ref[:, cols] = rope(_dot(nq, wuq_ref[:, cols])).astype(BF16)
        k_ref[:, cols] = (_dot(nkv, wuk_ref[:, cols]) + kr).astype(BF16)
    v_ref[...] = _dot(nkv, wuv_ref[...]).astype(BF16)


def _mla_proj(h, g, w_dkv, q_norm, w_uq, kv_norm, w_ukv, s):
    m, d = h.shape
    qk = MLA_NOPE + MLA_ROPE
    wd = jnp.concatenate([w_dkv[:, :MLA_Q_RANK + MLA_KV_RANK],
                          jnp.zeros((d, MLA_NOPE), F32), w_dkv[:, MLA_Q_RANK + MLA_KV_RANK:],
                          jnp.zeros((d, LANES - qk), F32)], axis=1).astype(BF16)
    wuq = jnp.pad(w_uq.reshape(MLA_Q_RANK, MLA_HEADS, qk), ((0, 0), (0, 0), (0, LANES - qk)))
    wuq = wuq.reshape(MLA_Q_RANK, MLA_HEADS * LANES).astype(BF16)
    w_ukv = w_ukv.reshape(MLA_KV_RANK, MLA_HEADS, MLA_NOPE + MLA_V)
    wuk = jnp.pad(w_ukv[:, :, :MLA_NOPE], ((0, 0), (0, 0), (0, LANES - MLA_NOPE)))
    wuk = wuk.reshape(MLA_KV_RANK, MLA_HEADS * LANES).astype(BF16)
    wuv = w_ukv[:, :, MLA_NOPE:].reshape(MLA_KV_RANK, MLA_HEADS * MLA_V).astype(BF16)
    half = MLA_ROPE // 2
    inv = ROPE_THETA ** (-jnp.arange(0, MLA_ROPE, 2, dtype=F32) / MLA_ROPE)
    ang = jnp.arange(s).astype(F32)[:, None] * inv[None, :]
    cos, sin = jnp.cos(ang), jnp.sin(ang)
    zeros = lambda w: jnp.zeros((s, w), F32)
    cos_t = jnp.concatenate([jnp.ones((s, MLA_NOPE), F32), cos, cos, zeros(LANES - qk)], axis=1)
    sin_up = jnp.concatenate([zeros(MLA_NOPE + half), sin, zeros(LANES - qk)], axis=1)
    sin_dn = jnp.concatenate([zeros(MLA_NOPE), -sin, zeros(half + LANES - qk)], axis=1)

    tiles_per_seq = s // ROW_TILE
    row = lambda width: pl.BlockSpec((ROW_TILE, width), lambda i: (i, 0))
    table = pl.BlockSpec((ROW_TILE, LANES), lambda i: (i % tiles_per_seq, 0))
    return pl.pallas_call(
        _mla_proj_kernel,
        grid=(m // ROW_TILE,),
        in_specs=[row(d), _resident((1, d)), _resident(wd.shape),
                  _resident((1, MLA_Q_RANK)), _resident((1, MLA_KV_RANK)),
                  _resident(wuq.shape), _resident(wuk.shape), _resident(wuv.shape),
                  table, table, table],
        out_specs=[row(MLA_HEADS * LANES), row(MLA_HEADS * LANES), row(MLA_HEADS * MLA_V)],
        out_shape=[jax.ShapeDtypeStruct((m, MLA_HEADS * LANES), BF16),
                   jax.ShapeDtypeStruct((m, MLA_HEADS * LANES), BF16),
                   jax.ShapeDtypeStruct((m, MLA_HEADS * MLA_V), BF16)],
        compiler_params=_params(("parallel",)),
        name="mla_proj",
    )(h, g.reshape(1, d), wd, q_norm.reshape(1, -1), kv_norm.reshape(1, -1), wuq, wuk, wuv,
      cos_t, sin_up, sin_dn)


def _mla_kernel(q_ref, k_ref, v_ref, o_ref, vt_ref, qh_ref, buf_a, buf_b, m_ref, l_ref, acc_ref, *, n_tiles):
    t = MLA_TILE
    n_heads = MLA_HEADS_PER_STEP
    i = pl.program_id(2)
    state = _SoftmaxState(m_ref, l_ref, acc_ref)

    @pl.when(i == 0)
    def _():
        for j in range(n_tiles):
            rows = slice(j * t, (j + 1) * t)
            vt_ref[:, rows] = _transpose_bf16(v_ref[0, rows, :])

    state.init()
    c = (MLA_NOPE + MLA_ROPE) ** -0.5 * LOG2E
    key = lax.broadcasted_iota(jnp.int32, (t, t), 0)
    qry = lax.broadcasted_iota(jnp.int32, (t, t), 1)
    lane_tile = lambda hh: slice(hh * LANES, (hh + 1) * LANES)
    for hh in range(n_heads):
        qh_ref[hh] = _transpose_bf16(q_ref[0, :, lane_tile(hh)])

    def rows_of(block):
        return pl.ds(pl.multiple_of(block * t, t), t)

    def scores(hh, block):
        return _dot(k_ref[0, rows_of(block), lane_tile(hh)], qh_ref[hh])

    def v_t(hh, block):
        return vt_ref[hh * MLA_V:(hh + 1) * MLA_V, rows_of(block)]

    def own_block(hh, s):
        state.step(hh, jnp.where(key <= qry, s, NEG_INF), c, v_t(hh, i))

    def past_block(hh, block, s):
        state.step(hh, s, c, v_t(hh, block))

    _sweep_blocks(i, n_heads, scores, own_block, past_block, buf_a, buf_b)
    o_t = jnp.concatenate([state.normalized(hh) for hh in range(n_heads)], axis=0)
    o_ref[0] = o_t.T.astype(o_ref.dtype)


def _mla_attention(q, k, v, b, s):
    n_heads = MLA_HEADS_PER_STEP
    n_groups = MLA_HEADS // n_heads
    whole_seq = lambda bi, g, i: (bi, 0, g)
    tile = lambda bi, g, i: (bi, i, g)
    score_buf = pltpu.VMEM((n_heads, MLA_TILE, MLA_TILE), F32)
    return pl.pallas_call(
        functools.partial(_mla_kernel, n_tiles=s // MLA_TILE),
        grid=(b, n_groups, s // MLA_TILE),
        in_specs=[pl.BlockSpec((1, MLA_TILE, n_heads * LANES), tile),
                  pl.BlockSpec((1, s, n_heads * LANES), whole_seq),
                  pl.BlockSpec((1, s, n_heads * MLA_V), whole_seq)],
        out_specs=pl.BlockSpec((1, MLA_TILE, n_heads * MLA_V), tile),
        out_shape=jax.ShapeDtypeStruct((b, s, MLA_HEADS * MLA_V), BF16),
        scratch_shapes=[pltpu.VMEM((n_heads * MLA_V, s), BF16),
                        pltpu.VMEM((n_heads, LANES, MLA_TILE), BF16),
                        score_buf, score_buf,
                        *_SoftmaxState.scratch(n_heads, MLA_V, MLA_TILE)],
        compiler_params=_params(("parallel", "parallel", "arbitrary")),
        name="mla_attention",
    )(q, k, v)


def _swa_kernel(slopes_ref, sinks_ref, q_ref, kp_ref, kc_ref, vp_ref, vc_ref, o_ref):
    n = pl.program_id(1)
    scale = HEAD_DIM ** -0.5
    group = SWA_Q_HEADS // SWA_KV_HEADS
    heads = [_Head(h, slice(h * HEAD_DIM, (h + 1) * HEAD_DIM),
                   slice((h // group) * HEAD_DIM, (h // group + 1) * HEAD_DIM))
             for h in range(SWA_Q_HEADS)]

    def head_logits(h, s, diff, n):
        band = s.shape[0]
        col = lax.broadcasted_iota(jnp.int32, s.shape, 1)
        mask = (diff >= 0) & (diff < SWA_WINDOW) & ((n > 0) | (col >= band))
        return jnp.where(mask, s * scale - slopes_ref[h.index] * diff.astype(F32), NEG_INF)

    def finish(h, logits, v):
        m = jnp.maximum(jnp.max(logits, axis=1, keepdims=True), sinks_ref[h.index])
        e = jnp.exp(logits - m)
        den = jnp.sum(e, axis=1, keepdims=True) + jnp.exp(sinks_ref[h.index] - m)
        o_ref[0, :, h.q_cols] = (_dot(e.astype(BF16), v) / den).astype(o_ref.dtype)

    _band_kernel_body(q_ref, kp_ref, kc_ref, vp_ref, vc_ref, n, heads, head_logits, finish)


def _swa_attention(qkv, sinks, b, s):
    q_width = SWA_Q_HEADS * HEAD_DIM
    kv_width = SWA_KV_HEADS * HEAD_DIM
    k_tile = q_width // kv_width
    cur = lambda t: (lambda bi, n: (bi, n, t))
    prev = lambda t: (lambda bi, n: (bi, jnp.maximum(n - 1, 0), t))
    kv_blk = (1, SWA_BLOCK, kv_width)
    return pl.pallas_call(
        _swa_kernel,
        grid=(b, s // SWA_BLOCK),
        in_specs=[pl.BlockSpec(memory_space=pltpu.SMEM), pl.BlockSpec(memory_space=pltpu.SMEM),
                  pl.BlockSpec((1, SWA_BLOCK, q_width), cur(0)),
                  pl.BlockSpec(kv_blk, prev(k_tile)), pl.BlockSpec(kv_blk, cur(k_tile)),
                  pl.BlockSpec(kv_blk, prev(k_tile + 1)), pl.BlockSpec(kv_blk, cur(k_tile + 1))],
        out_specs=pl.BlockSpec((1, SWA_BLOCK, q_width), cur(0)),
        out_shape=jax.ShapeDtypeStruct((b, s, q_width), BF16),
        compiler_params=_params(("parallel", "arbitrary")),
        name="swa_attention",
    )(_alibi_slopes(SWA_Q_HEADS), sinks.astype(F32), qkv, qkv, qkv, qkv, qkv)


def kernel(x, l0_attn_norm, l0_w_qkv, l0_w_o, l0_mlp_norm, l0_w_up, l0_w_down, l1_attn_norm, l1_w_qkv, l1_w_o, l1_mlp_norm, l1_w_up, l1_w_down, l2_attn_norm, l2_w_dkv, l2_q_norm, l2_w_uq, l2_kv_norm, l2_w_ukv, l2_w_o, l2_mlp_norm, l2_w_up, l2_w_down, l3_attn_norm, l3_w_qkv, l3_sinks, l3_w_o, l3_mlp_norm, l3_w_up, l3_w_down, final_norm):
    b, s, d = x.shape
    bf = lambda w: w.astype(BF16)
    h = x.reshape(b * s, d)

    qkv = _norm_proj(h, l0_attn_norm, bf(l0_w_qkv))
    a = _moba_attention(qkv.reshape(b, s, -1), b, s).reshape(b * s, -1)
    h = _post(h, (a,), bf(l0_w_o), l0_mlp_norm, bf(l0_w_up), bf(l0_w_down))

    qkv = _norm_proj(h, l1_attn_norm, bf(l1_w_qkv)).reshape(b, s, -1)
    groups = [_dilated_group(qkv, b, s, g, window, dil) for g, (window, dil) in enumerate(DIL_PAIRS)]
    h = _post(h, tuple(o for o, _ in groups) + tuple(l for _, l in groups),
              bf(l1_w_o), l1_mlp_norm, bf(l1_w_up), bf(l1_w_down))

    q, k, v = _mla_proj(h, l2_attn_norm, l2_w_dkv, l2_q_norm, l2_w_uq, l2_kv_norm, l2_w_ukv, s)
    a = _mla_attention(q.reshape(b, s, -1), k.reshape(b, s, -1), v.reshape(b, s, -1), b, s).reshape(b * s, -1)
    h = _post(h, (a,), bf(l2_w_o), l2_mlp_norm, bf(l2_w_up), bf(l2_w_down))

    qkv = _norm_proj(h, l3_attn_norm, bf(l3_w_qkv), col_tile=256)
    a = _swa_attention(qkv.reshape(b, s, -1), l3_sinks, b, s).reshape(b * s, -1)
    h = _post(h, (a,), bf(l3_w_o), l3_mlp_norm, bf(l3_w_up), bf(l3_w_down), g_final=final_norm)
    return h.reshape(b, s, d)
```

```python
import functools

import jax
import jax.numpy as jnp
from jax import lax
from jax.experimental import pallas as pl
from jax.experimental.pallas import tpu as pltpu

F32 = jnp.float32
BF16 = jnp.bfloat16

D_MODEL = 1024
HEAD_DIM = 64
RMS_EPS = 1e-6
D_FF = 4 * D_MODEL
NEG_INF = -1e30

MOBA_HEADS = 16
MOBA_BLOCK = 256
MOBA_TOPK = 3

DIL_PAIRS = ((128, 1), (512, 4), (2048, 16))
DIL_HEADS_PER_GROUP = 8
DIL_BAND = 128

MLA_HEADS = 16
MLA_Q_RANK = 768
MLA_KV_RANK = 256
MLA_NOPE = 64
MLA_ROPE = 32
MLA_V = 64
ROPE_THETA = 10000.0
MLA_TILE = 256
MLA_HEADS_PER_STEP = 4
MOBA_HEADS_PER_STEP = 4

SWA_Q_HEADS = 16
SWA_KV_HEADS = 2
SWA_WINDOW = 128
SWA_BLOCK = 128

LANES = 128
ROW_TILE = 512
FF_TILE = 512
VMEM_LIMIT = 56 * 1024 * 1024
LOG2E = 1.4426950408889634


def _params(semantics):
    return pltpu.CompilerParams(dimension_semantics=semantics, vmem_limit_bytes=VMEM_LIMIT)


def _resident(shape):
    return pl.BlockSpec(shape, lambda *_: (0,) * len(shape), pipeline_mode=pl.Buffered(1))


def _alibi_slopes(n_heads):
    return 2.0 ** (-8.0 * jnp.arange(1, n_heads + 1, dtype=F32) / n_heads)


def _rms(x, g):
    return x * lax.rsqrt(jnp.mean(x * x, axis=-1, keepdims=True) + RMS_EPS) * g


def _dot(a, b):
    return jnp.dot(a, b, preferred_element_type=F32)


def _dot_nt(a, b):
    return lax.dot_general(a, b, (((1,), (1,)), ((), ())), preferred_element_type=F32)


def _norm_proj_kernel(h_ref, g_ref, w_ref, o_ref, *, col_tile):
    n = _rms(h_ref[...], g_ref[...]).astype(BF16)
    for c in range(o_ref.shape[1] // col_tile):
        cols = slice(c * col_tile, (c + 1) * col_tile)
        o_ref[:, cols] = _dot(n, w_ref[:, cols]).astype(o_ref.dtype)


def _norm_proj(h, g, w, col_tile=512):
    m, d = h.shape
    n_out = w.shape[1]
    return pl.pallas_call(
        functools.partial(_norm_proj_kernel, col_tile=col_tile),
        grid=(m // ROW_TILE,),
        in_specs=[pl.BlockSpec((ROW_TILE, d), lambda i: (i, 0)),
                  _resident((1, d)),
                  _resident((d, n_out))],
        out_specs=pl.BlockSpec((ROW_TILE, n_out), lambda i: (i, 0)),
        out_shape=jax.ShapeDtypeStruct((m, n_out), BF16),
        compiler_params=_params(("parallel",)),
        name="norm_proj",
    )(h, g.reshape(1, d), w)


def _dil_proj_kernel(h_ref, g_ref, w_ref, o0_ref, o1_ref, o2_ref, stage_ref):
    n = _rms(h_ref[...], g_ref[...]).astype(BF16)
    rows = h_ref.shape[0]
    width = DIL_HEADS_PER_GROUP * HEAD_DIM
    for g, (o_ref, (_, dil)) in enumerate(zip((o0_ref, o1_ref, o2_ref), DIL_PAIRS)):
        for part in range(3):
            src = (3 * g + part) * width
            y = _dot(n, w_ref[:, src:src + width])
            if dil == 1:
                o_ref[:, part * width:(part + 1) * width] = y.astype(BF16)
                continue
            for t in range(width // LANES):
                stage_ref[g - 1, part, t] = y[:, t * LANES:(t + 1) * LANES]
            for r in range(dil):
                for t in range(width // LANES):
                    dst = (3 * r + part) * width + t * LANES
                    strided = stage_ref[g - 1, part, t, pl.ds(r, rows // dil, stride=dil), :]
                    o_ref[:, dst:dst + LANES] = strided.astype(BF16)


def _dil_proj(h, g, w):
    m, d = h.shape
    feat = 3 * DIL_HEADS_PER_GROUP * HEAD_DIM
    assert DIL_PAIRS[0][1] == 1
    view = lambda dil: (m // dil, dil * feat)
    view_block = lambda dil: pl.BlockSpec((ROW_TILE // dil, dil * feat), lambda i: (i, 0))
    return pl.pallas_call(
        _dil_proj_kernel,
        grid=(m // ROW_TILE,),
        in_specs=[pl.BlockSpec((ROW_TILE, d), lambda i: (i, 0)), _resident((1, d)), _resident(w.shape)],
        out_specs=[view_block(dil) for _, dil in DIL_PAIRS],
        out_shape=[jax.ShapeDtypeStruct(view(dil), BF16) for _, dil in DIL_PAIRS],
        scratch_shapes=[pltpu.VMEM((len(DIL_PAIRS) - 1, 3, feat // 3 // LANES, ROW_TILE, LANES), F32)],
        compiler_params=_params(("parallel",)),
        name="dil_proj",
    )(h, g.reshape(1, d), w)


def _mlp_tail(h1, g_ref, wup_ref, wdn_ref, gf_ref, out_ref):
    n = _rms(h1, g_ref[...]).astype(BF16)
    acc = jnp.zeros_like(h1)
    for c in range(wup_ref.shape[1] // FF_TILE):
        cols = slice(c * FF_TILE, (c + 1) * FF_TILE)
        u = jnp.square(jnp.maximum(_dot(n, wup_ref[:, cols]), 0.0)).astype(BF16)
        acc = acc + _dot(u, wdn_ref[cols, :])
    out = h1 + acc
    if gf_ref is not None:
        out = _rms(out, gf_ref[...])
    out_ref[...] = out


def _post_kernel(h_ref, a_ref, wo_ref, g_ref, wup_ref, wdn_ref, *rest):
    gf_ref, out_ref = (rest[0], rest[1]) if len(rest) == 2 else (None, rest[0])
    h1 = h_ref[...] + _dot(a_ref[...], wo_ref[...])
    _mlp_tail(h1, g_ref, wup_ref, wdn_ref, gf_ref, out_ref)


def _post_merge_kernel(h_ref, o0_ref, o1_ref, o2_ref, l0_ref, l1_ref, l2_ref,
                       wo_ref, g_ref, wup_ref, wdn_ref, out_ref, stage_ref):
    rows, width = o0_ref.shape

    def token_rows(x_ref, slot, dil):
        if dil == 1:
            return x_ref[...]
        for r in range(dil):
            for t in range(width // LANES):
                src = r * width + t * LANES
                stage_ref[slot, t, pl.ds(r, rows // dil, stride=dil), :] = x_ref[:, src:src + LANES]
        return jnp.concatenate([stage_ref[slot, t] for t in range(width // LANES)], axis=1)

    dils = [dil for _, dil in DIL_PAIRS]
    l0, l1, l2 = (token_rows(ref, slot, dil) for slot, (ref, dil) in enumerate(zip((l0_ref, l1_ref, l2_ref), dils)))
    o0, o1, o2 = (token_rows(ref, 3 + slot, dil) for slot, (ref, dil) in enumerate(zip((o0_ref, o1_ref, o2_ref), dils)))
    mx = jnp.maximum(jnp.maximum(l0, l1), l2)
    e0, e1, e2 = jnp.exp(l0 - mx), jnp.exp(l1 - mx), jnp.exp(l2 - mx)
    merged = (e0 * o0 + e1 * o1 + e2 * o2) / (e0 + e1 + e2)
    h1 = h_ref[...] + _dot(merged.astype(BF16), wo_ref[...])
    _mlp_tail(h1, g_ref, wup_ref, wdn_ref, None, out_ref)


def _post(h, attn_ins, w_o, g, w_up, w_down, g_final=None):
    m, d = h.shape
    row = lambda width: pl.BlockSpec((ROW_TILE, width), lambda i: (i, 0))
    merge = len(attn_ins) > 1
    in_specs = [row(d)] + [pl.BlockSpec((ROW_TILE * a.shape[0] // m, a.shape[1]), lambda i: (i, 0)) for a in attn_ins]
    in_specs += [_resident(w_o.shape), _resident((1, d)), _resident(w_up.shape), _resident(w_down.shape)]
    args = [h, *attn_ins, w_o, g.reshape(1, d), w_up, w_down]
    if g_final is not None:
        in_specs.append(_resident((1, d)))
        args.append(g_final.reshape(1, d))
    scratch = [pltpu.VMEM((len(attn_ins), w_o.shape[0] // LANES, ROW_TILE, LANES), F32)] if merge else []
    return pl.pallas_call(
        _post_merge_kernel if merge else _post_kernel,
        grid=(m // ROW_TILE,),
        in_specs=in_specs,
        out_specs=row(d),
        out_shape=jax.ShapeDtypeStruct((m, d), F32),
        scratch_shapes=scratch,
        compiler_params=_params(("parallel",)),
        name="post_mlp",
    )(*args)


class _SoftmaxState:
    def __init__(self, m_ref, l_ref, acc_ref):
        self.m_ref, self.l_ref, self.acc_ref = m_ref, l_ref, acc_ref

    @staticmethod
    def scratch(n_heads, dv, tq):
        return [pltpu.VMEM((n_heads, 1, tq), F32), pltpu.VMEM((n_heads, 1, tq), F32),
                pltpu.VMEM((n_heads, dv, tq), F32)]

    def init(self):
        self.m_ref[...] = jnp.full(self.m_ref.shape, NEG_INF, F32)
        self.l_ref[...] = jnp.zeros(self.l_ref.shape, F32)
        self.acc_ref[...] = jnp.zeros(self.acc_ref.shape, F32)

    def step(self, hh, x, c, v_t, shift=None):
        m = self.m_ref[hh]
        m_cur = jnp.max(x, axis=0, keepdims=True)
        if shift is not None:
            m_cur = m_cur - shift
        m_new = jnp.maximum(m, m_cur)
        alpha = jnp.exp2((m - m_new) * c)
        p = jnp.exp2((x - (m_new if shift is None else m_new + shift)) * c)
        self.m_ref[hh] = m_new
        self.l_ref[hh] = alpha * self.l_ref[hh] + jnp.sum(p, axis=0, keepdims=True)
        self.acc_ref[hh] = alpha * self.acc_ref[hh] + _dot(v_t, p.astype(BF16))

    def normalized(self, hh):
        return self.acc_ref[hh] / self.l_ref[hh]


def _sweep_blocks(n_past, n_heads, score_fn, own_fn, consume_fn, buf_a, buf_b):
    def produce(buf, block):
        for hh in range(n_heads):
            buf[hh] = score_fn(hh, block)

    def consume(buf, block):
        for hh in range(n_heads):
            consume_fn(hh, block, buf[hh])

    produce(buf_b, n_past)
    produce(buf_a, 0)
    for hh in range(n_heads):
        own_fn(hh, buf_b[hh])

    def pair(p, carry):
        first = 2 * p
        produce(buf_b, first + 1)
        consume(buf_a, first)
        produce(buf_a, jnp.minimum(first + 2, n_past - 1))
        consume(buf_b, first + 1)
        return carry

    lax.fori_loop(0, n_past // 2, pair, 0)

    @pl.when(n_past % 2 == 1)
    def _():
        consume(buf_a, n_past - 1)


def _transpose_bf16(x):
    return x.astype(F32).T.astype(BF16)


def _keep_head_rows(x_t, half):
    r = lax.broadcasted_iota(jnp.int32, x_t.shape, 0)
    keep = (r < HEAD_DIM) if half == 0 else (r >= HEAD_DIM)
    return jnp.where(keep, x_t, jnp.zeros_like(x_t))


def _moba_kernel(slopes_ref, q_ref, k_ref, v_ref, o_ref, kmean_ref, vt_ref, term_ref, alibi_ref, qh_ref,
                 buf_a, buf_b, m_ref, l_ref, acc_ref, *, n_blocks):
    blk = MOBA_BLOCK
    n_heads = MOBA_HEADS_PER_STEP
    group = pl.program_id(1)
    i = pl.program_id(2)
    state = _SoftmaxState(m_ref, l_ref, acc_ref)

    @pl.when(i == 0)
    def _():
        for j in range(n_blocks):
            rows = slice(j * blk, (j + 1) * blk)
            kmean_ref[j:j + 1, :] = jnp.mean(k_ref[0, rows, :].astype(F32), axis=0, keepdims=True)
            vt_ref[:, rows] = _transpose_bf16(v_ref[0, rows, :])

    state.init()
    scale = HEAD_DIM ** -0.5
    c = scale * LOG2E
    q_t = q_ref[0].astype(F32).T
    key = lax.broadcasted_iota(jnp.int32, (blk, blk), 0)
    qry = lax.broadcasted_iota(jnp.int32, (blk, blk), 1)
    rel = (qry - key).astype(F32)
    blk_idx = lax.broadcasted_iota(jnp.int32, (n_blocks, blk), 0)
    fully_past = blk_idx < i
    block_dist = ((i - blk_idx) * blk).astype(F32)
    lane_tile = lambda hh: slice((hh // 2) * LANES, (hh // 2 + 1) * LANES)

    for hh in range(n_heads):
        slope_s = slopes_ref[n_heads * group + hh] / scale
        q_h = _keep_head_rows(q_t[lane_tile(hh), :], hh % 2)
        gate = lax.dot_general(kmean_ref[:, lane_tile(hh)], q_h, (((1,), (0,)), ((), ())),
                               precision=lax.Precision.HIGHEST, preferred_element_type=F32)
        work = jnp.where(fully_past, gate, NEG_INF)
        chosen = jnp.zeros((n_blocks, blk), jnp.bool_)
        for _ in range(min(MOBA_TOPK, n_blocks)):
            best = jnp.max(work, axis=0, keepdims=True)
            first = jnp.min(jnp.where(work == best, blk_idx, n_blocks), axis=0, keepdims=True)
            pick = blk_idx == first
            chosen = jnp.logical_or(chosen, pick)
            work = jnp.where(pick, -jnp.inf, work)
        term_ref[hh] = jnp.where(jnp.logical_and(chosen, fully_past), slope_s * block_dist, -NEG_INF)
        qh_ref[hh] = q_h.astype(BF16)
        alibi_ref[hh] = slope_s * rel

    def rows_of(block):
        return pl.ds(pl.multiple_of(block * blk, blk), blk)

    def scores(hh, block):
        return _dot(k_ref[0, rows_of(block), lane_tile(hh)], qh_ref[hh])

    def v_t(hh, block):
        return vt_ref[hh * HEAD_DIM:(hh + 1) * HEAD_DIM, rows_of(block)]

    def own_block(hh, s):
        state.step(hh, jnp.where(rel >= 0, s - alibi_ref[hh], NEG_INF), c, v_t(hh, i))

    def past_block(hh, block, s):
        state.step(hh, s - alibi_ref[hh], c, v_t(hh, block), shift=term_ref[hh, pl.ds(block, 1), :])

    _sweep_blocks(i, n_heads, scores, own_block, past_block, buf_a, buf_b)
    o_t = jnp.concatenate([state.normalized(hh) for hh in range(n_heads)], axis=0)
    o_ref[0] = o_t.T.astype(o_ref.dtype)


def _moba_attention(qkv, b, s):
    n_blocks = s // MOBA_BLOCK
    n_heads = MOBA_HEADS_PER_STEP
    width = n_heads * HEAD_DIM
    n_groups = MOBA_HEADS // n_heads
    score_buf = pltpu.VMEM((n_heads, MOBA_BLOCK, MOBA_BLOCK), F32)
    return pl.pallas_call(
        functools.partial(_moba_kernel, n_blocks=n_blocks),
        grid=(b, n_groups, n_blocks),
        in_specs=[pl.BlockSpec(memory_space=pltpu.SMEM),
                  pl.BlockSpec((1, MOBA_BLOCK, width), lambda bi, g, i: (bi, i, g)),
                  pl.BlockSpec((1, s, width), lambda bi, g, i: (bi, 0, n_groups + g)),
                  pl.BlockSpec((1, s, width), lambda bi, g, i: (bi, 0, 2 * n_groups + g))],
        out_specs=pl.BlockSpec((1, MOBA_BLOCK, width), lambda bi, g, i: (bi, i, g)),
        out_shape=jax.ShapeDtypeStruct((b, s, MOBA_HEADS * HEAD_DIM), BF16),
        scratch_shapes=[pltpu.VMEM((n_blocks, width), F32),
                        pltpu.VMEM((width, s), BF16),
                        pltpu.VMEM((n_heads, n_blocks, MOBA_BLOCK), F32),
                        pltpu.VMEM((n_heads, MOBA_BLOCK, MOBA_BLOCK), F32),
                        pltpu.VMEM((n_heads, LANES, MOBA_BLOCK), BF16),
                        score_buf, score_buf,
                        *_SoftmaxState.scratch(n_heads, HEAD_DIM, MOBA_BLOCK)],
        compiler_params=_params(("parallel", "parallel", "arbitrary")),
        name="moba_attention",
    )(_alibi_slopes(MOBA_HEADS), qkv, qkv, qkv)


def _lane_half(x, half):
    lane = lax.broadcasted_iota(jnp.int32, x.shape, x.ndim - 1)
    keep = (lane < HEAD_DIM) if half == 0 else (lane >= HEAD_DIM)
    return jnp.where(keep, x, jnp.zeros_like(x))


def _band_attention(n_heads, q_tile, k_tile, v_tile, head_softmax, store_pair, s_ref):
    for h in range(n_heads):
        s_ref[h] = _dot_nt(_lane_half(q_tile(h), h % 2), k_tile(h))
    even = None
    for h in range(n_heads):
        p, den, extra = head_softmax(h, s_ref[h])
        o = _dot(p.astype(BF16), v_tile(h)) / den
        if h % 2 == 0:
            even = (o, extra)
            continue
        lane = lax.broadcasted_iota(jnp.int32, o.shape, 1)
        store_pair(h // 2, lane < HEAD_DIM, even, (o, extra))


def _band_geometry(band, n):
    row = lax.broadcasted_iota(jnp.int32, (band, 2 * band), 0)
    col = lax.broadcasted_iota(jnp.int32, (band, 2 * band), 1)
    diff = row + band - col
    in_sequence = (n > 0) | (col >= band)
    return diff, in_sequence


def _dil_kernel(slopes_ref, q_ref, kp_ref, kc_ref, vp_ref, vc_ref, o_ref, lse_ref, s_ref, *, n_pts, dil, group):
    n = pl.program_id(2)
    scale = HEAD_DIM ** -0.5
    diff, in_sequence = _band_geometry(DIL_BAND, n)
    mask = (diff >= 0) & (diff <= n_pts) & in_sequence
    dist = (dil * diff).astype(F32)
    tile = lambda h: slice((h // 2) * LANES, (h // 2 + 1) * LANES)

    def head_softmax(h, s):
        slope = slopes_ref[group * DIL_HEADS_PER_GROUP + h]
        logits = jnp.where(mask, s * scale - slope * dist, NEG_INF)
        m = jnp.max(logits, axis=1, keepdims=True)
        e = jnp.exp(logits - m)
        den = jnp.sum(e, axis=1, keepdims=True)
        return e, den, m + jnp.log(den)

    def store_pair(pair, first_half, even, odd):
        cols = slice(pair * LANES, (pair + 1) * LANES)
        o_ref[0, :, cols] = jnp.where(first_half, even[0], odd[0])
        lse_ref[0, :, cols] = jnp.where(first_half, even[1], odd[1])

    _band_attention(
        DIL_HEADS_PER_GROUP,
        lambda h: q_ref[0, :, tile(h)],
        lambda h: jnp.concatenate([kp_ref[0, :, tile(h)], kc_ref[0, :, tile(h)]], axis=0),
        lambda h: jnp.concatenate([vp_ref[0, :, tile(h)], vc_ref[0, :, tile(h)]], axis=0),
        head_softmax, store_pair, s_ref)


def _dilated_group(qkv, b, s, group, window, dil):
    width = DIL_HEADS_PER_GROUP * HEAD_DIM
    n_sub = s // dil
    assert n_sub % DIL_BAND == 0
    view = qkv.reshape(b, n_sub, dil * 3 * width)

    def col(part):
        return lambda bi, r, n: (bi, n, r * 3 + part)

    def col_prev(part):
        return lambda bi, r, n: (bi, jnp.maximum(n - 1, 0), r * 3 + part)

    blk = (1, DIL_BAND, width)
    out_spec = pl.BlockSpec(blk, lambda bi, r, n: (bi, n, r))
    out_sds = jax.ShapeDtypeStruct((b, n_sub, dil * width), F32)
    o, lse = pl.pallas_call(
        functools.partial(_dil_kernel, n_pts=window // dil, dil=dil, group=group),
        grid=(b, dil, n_sub // DIL_BAND),
        in_specs=[pl.BlockSpec(memory_space=pltpu.SMEM),
                  pl.BlockSpec(blk, col(0)),
                  pl.BlockSpec(blk, col_prev(1)), pl.BlockSpec(blk, col(1)),
                  pl.BlockSpec(blk, col_prev(2)), pl.BlockSpec(blk, col(2))],
        out_specs=[out_spec, out_spec],
        out_shape=[out_sds, out_sds],
        scratch_shapes=[pltpu.VMEM((DIL_HEADS_PER_GROUP, DIL_BAND, 2 * DIL_BAND), F32)],
        compiler_params=_params(("parallel", "parallel", "arbitrary")),
        name=f"dilated_attention_g{group}",
    )(_alibi_slopes(len(DIL_PAIRS) * DIL_HEADS_PER_GROUP), view, view, view, view, view)
    return o.reshape(b * n_sub, dil * width), lse.reshape(b * n_sub, dil * width)


def _mla_proj_kernel(h_ref, g_ref, wd_ref, gq_ref, gkv_ref, wuq_ref, wuk_ref, wuv_ref,
                     cos_ref, sin_up_ref, sin_dn_ref, q_ref, k_ref, v_ref):
    n = _rms(h_ref[...], g_ref[...]).astype(BF16)
    c = _dot(n, wd_ref[...])
    nq = _rms(c[:, :MLA_Q_RANK], gq_ref[...]).astype(BF16)
    nkv = _rms(c[:, MLA_Q_RANK:MLA_Q_RANK + MLA_KV_RANK], gkv_ref[...]).astype(BF16)
    k_rope = c[:, MLA_Q_RANK + MLA_KV_RANK:]
    cos, sin_up, sin_dn = cos_ref[...], sin_up_ref[...], sin_dn_ref[...]
    half = MLA_ROPE // 2

    def rope(x):
        return x * cos + pltpu.roll(x, half, 1) * sin_up + pltpu.roll(x, LANES - half, 1) * sin_dn

    kr = rope(k_rope)
    for h in range(MLA_HEADS):
        cols = slice(h * LANES, (h + 1) * LANES)
        q_ref[:, cols] = rope(_dot(nq, wuq_ref[:, cols])).astype(BF16)
        k_ref[:, cols] = (_dot(nkv, wuk_ref[:, cols]) + kr).astype(BF16)
    v_ref[...] = _dot(nkv, wuv_ref[...]).astype(BF16)


def _mla_proj(h, g, w_dkv, q_norm, w_uq, kv_norm, w_ukv, s):
    m, d = h.shape
    qk = MLA_NOPE + MLA_ROPE
    wd = jnp.concatenate([w_dkv[:, :MLA_Q_RANK + MLA_KV_RANK],
                          jnp.zeros((d, MLA_NOPE), F32), w_dkv[:, MLA_Q_RANK + MLA_KV_RANK:],
                          jnp.zeros((d, LANES - qk), F32)], axis=1).astype(BF16)
    wuq = jnp.pad(w_uq.reshape(MLA_Q_RANK, MLA_HEADS, qk), ((0, 0), (0, 0), (0, LANES - qk)))
    wuq = wuq.reshape(MLA_Q_RANK, MLA_HEADS * LANES).astype(BF16)
    w_ukv = w_ukv.reshape(MLA_KV_RANK, MLA_HEADS, MLA_NOPE + MLA_V)
    wuk = jnp.pad(w_ukv[:, :, :MLA_NOPE], ((0, 0), (0, 0), (0, LANES - MLA_NOPE)))
    wuk = wuk.reshape(MLA_KV_RANK, MLA_HEADS * LANES).astype(BF16)
    wuv = w_ukv[:, :, MLA_NOPE:].reshape(MLA_KV_RANK, MLA_HEADS * MLA_V).astype(BF16)
    half = MLA_ROPE // 2
    inv = ROPE_THETA ** (-jnp.arange(0, MLA_ROPE, 2, dtype=F32) / MLA_ROPE)
    ang = jnp.arange(s).astype(F32)[:, None] * inv[None, :]
    cos, sin = jnp.cos(ang), jnp.sin(ang)
    zeros = lambda w: jnp.zeros((s, w), F32)
    cos_t = jnp.concatenate([jnp.ones((s, MLA_NOPE), F32), cos, cos, zeros(LANES - qk)], axis=1)
    sin_up = jnp.concatenate([zeros(MLA_NOPE + half), sin, zeros(LANES - qk)], axis=1)
    sin_dn = jnp.concatenate([zeros(MLA_NOPE), -sin, zeros(half + LANES - qk)], axis=1)

    tiles_per_seq = s // ROW_TILE
    row = lambda width: pl.BlockSpec((ROW_TILE, width), lambda i: (i, 0))
    table = pl.BlockSpec((ROW_TILE, LANES), lambda i: (i % tiles_per_seq, 0))
    return pl.pallas_call(
        _mla_proj_kernel,
        grid=(m // ROW_TILE,),
        in_specs=[row(d), _resident((1, d)), _resident(wd.shape),
                  _resident((1, MLA_Q_RANK)), _resident((1, MLA_KV_RANK)),
                  _resident(wuq.shape), _resident(wuk.shape), _resident(wuv.shape),
                  table, table, table],
        out_specs=[row(MLA_HEADS * LANES), row(MLA_HEADS * LANES), row(MLA_HEADS * MLA_V)],
        out_shape=[jax.ShapeDtypeStruct((m, MLA_HEADS * LANES), BF16),
                   jax.ShapeDtypeStruct((m, MLA_HEADS * LANES), BF16),
                   jax.ShapeDtypeStruct((m, MLA_HEADS * MLA_V), BF16)],
        compiler_params=_params(("parallel",)),
        name="mla_proj",
    )(h, g.reshape(1, d), wd, q_norm.reshape(1, -1), kv_norm.reshape(1, -1), wuq, wuk, wuv,
      cos_t, sin_up, sin_dn)


def _mla_kernel(q_ref, k_ref, v_ref, o_ref, vt_ref, qh_ref, buf_a, buf_b, m_ref, l_ref, acc_ref, *, n_tiles):
    t = MLA_TILE
    n_heads = MLA_HEADS_PER_STEP
    i = pl.program_id(2)
    state = _SoftmaxState(m_ref, l_ref, acc_ref)

    @pl.when(i == 0)
    def _():
        for j in range(n_tiles):
            rows = slice(j * t, (j + 1) * t)
            vt_ref[:, rows] = _transpose_bf16(v_ref[0, rows, :])

    state.init()
    c = (MLA_NOPE + MLA_ROPE) ** -0.5 * LOG2E
    key = lax.broadcasted_iota(jnp.int32, (t, t), 0)
    qry = lax.broadcasted_iota(jnp.int32, (t, t), 1)
    lane_tile = lambda hh: slice(hh * LANES, (hh + 1) * LANES)
    for hh in range(n_heads):
        qh_ref[hh] = _transpose_bf16(q_ref[0, :, lane_tile(hh)])

    def rows_of(block):
        return pl.ds(pl.multiple_of(block * t, t), t)

    def scores(hh, block):
        return _dot(k_ref[0, rows_of(block), lane_tile(hh)], qh_ref[hh])

    def v_t(hh, block):
        return vt_ref[hh * MLA_V:(hh + 1) * MLA_V, rows_of(block)]

    def own_block(hh, s):
        state.step(hh, jnp.where(key <= qry, s, NEG_INF), c, v_t(hh, i))

    def past_block(hh, block, s):
        state.step(hh, s, c, v_t(hh, block))

    _sweep_blocks(i, n_heads, scores, own_block, past_block, buf_a, buf_b)
    o_t = jnp.concatenate([state.normalized(hh) for hh in range(n_heads)], axis=0)
    o_ref[0] = o_t.T.astype(o_ref.dtype)


def _mla_attention(q, k, v, b, s):
    n_heads = MLA_HEADS_PER_STEP
    n_groups = MLA_HEADS // n_heads
    whole_seq = lambda bi, g, i: (bi, 0, g)
    tile = lambda bi, g, i: (bi, i, g)
    score_buf = pltpu.VMEM((n_heads, MLA_TILE, MLA_TILE), F32)
    return pl.pallas_call(
        functools.partial(_mla_kernel, n_tiles=s // MLA_TILE),
        grid=(b, n_groups, s // MLA_TILE),
        in_specs=[pl.BlockSpec((1, MLA_TILE, n_heads * LANES), tile),
                  pl.BlockSpec((1, s, n_heads * LANES), whole_seq),
                  pl.BlockSpec((1, s, n_heads * MLA_V), whole_seq)],
        out_specs=pl.BlockSpec((1, MLA_TILE, n_heads * MLA_V), tile),
        out_shape=jax.ShapeDtypeStruct((b, s, MLA_HEADS * MLA_V), BF16),
        scratch_shapes=[pltpu.VMEM((n_heads * MLA_V, s), BF16),
                        pltpu.VMEM((n_heads, LANES, MLA_TILE), BF16),
                        score_buf, score_buf,
                        *_SoftmaxState.scratch(n_heads, MLA_V, MLA_TILE)],
        compiler_params=_params(("parallel", "parallel", "arbitrary")),
        name="mla_attention",
    )(q, k, v)


def _swa_kernel(slopes_ref, sinks_ref, q_ref, kp_ref, kc_ref, vp_ref, vc_ref, o_ref, s_ref):
    n = pl.program_id(1)
    scale = HEAD_DIM ** -0.5
    group = SWA_Q_HEADS // SWA_KV_HEADS
    diff, in_sequence = _band_geometry(SWA_BLOCK, n)
    mask = (diff >= 0) & (diff < SWA_WINDOW) & in_sequence
    dist = diff.astype(F32)

    def both_halves(prev_ref, cur_ref, kv_head):
        x = jnp.concatenate([prev_ref[0], cur_ref[0]], axis=0)
        part = x[:, kv_head * HEAD_DIM:(kv_head + 1) * HEAD_DIM]
        return jnp.concatenate([part, part], axis=1)

    k_tiles = [both_halves(kp_ref, kc_ref, kv) for kv in range(SWA_KV_HEADS)]
    v_tiles = [both_halves(vp_ref, vc_ref, kv) for kv in range(SWA_KV_HEADS)]

    def head_softmax(h, s):
        logits = jnp.where(mask, s * scale - slopes_ref[h] * dist, NEG_INF)
        m = jnp.maximum(jnp.max(logits, axis=1, keepdims=True), sinks_ref[h])
        e = jnp.exp(logits - m)
        return e, jnp.sum(e, axis=1, keepdims=True) + jnp.exp(sinks_ref[h] - m), None

    def store_pair(pair, first_half, even, odd):
        o_ref[0, :, pair * LANES:(pair + 1) * LANES] = jnp.where(first_half, even[0], odd[0]).astype(o_ref.dtype)

    _band_attention(
        SWA_Q_HEADS,
        lambda h: q_ref[0, :, (h // 2) * LANES:(h // 2 + 1) * LANES],
        lambda h: k_tiles[h // group],
        lambda h: v_tiles[h // group],
        head_softmax, store_pair, s_ref)


def _swa_attention(qkv, sinks, b, s):
    q_width = SWA_Q_HEADS * HEAD_DIM
    kv_width = SWA_KV_HEADS * HEAD_DIM
    k_tile = q_width // kv_width
    cur = lambda t: (lambda bi, n: (bi, n, t))
    prev = lambda t: (lambda bi, n: (bi, jnp.maximum(n - 1, 0), t))
    kv_blk = (1, SWA_BLOCK, kv_width)
    return pl.pallas_call(
        _swa_kernel,
        grid=(b, s // SWA_BLOCK),
        in_specs=[pl.BlockSpec(memory_space=pltpu.SMEM), pl.BlockSpec(memory_space=pltpu.SMEM),
                  pl.BlockSpec((1, SWA_BLOCK, q_width), cur(0)),
                  pl.BlockSpec(kv_blk, prev(k_tile)), pl.BlockSpec(kv_blk, cur(k_tile)),
                  pl.BlockSpec(kv_blk, prev(k_tile + 1)), pl.BlockSpec(kv_blk, cur(k_tile + 1))],
        out_specs=pl.BlockSpec((1, SWA_BLOCK, q_width), cur(0)),
        out_shape=jax.ShapeDtypeStruct((b, s, q_width), BF16),
        scratch_shapes=[pltpu.VMEM((SWA_Q_HEADS, SWA_BLOCK, 2 * SWA_BLOCK), F32)],
        compiler_params=_params(("parallel", "arbitrary")),
        name="swa_attention",
    )(_alibi_slopes(SWA_Q_HEADS), sinks.astype(F32), qkv, qkv, qkv, qkv, qkv)


def kernel(x, l0_attn_norm, l0_w_qkv, l0_w_o, l0_mlp_norm, l0_w_up, l0_w_down, l1_attn_norm, l1_w_qkv, l1_w_o, l1_mlp_norm, l1_w_up, l1_w_down, l2_attn_norm, l2_w_dkv, l2_q_norm, l2_w_uq, l2_kv_norm, l2_w_ukv, l2_w_o, l2_mlp_norm, l2_w_up, l2_w_down, l3_attn_norm, l3_w_qkv, l3_sinks, l3_w_o, l3_mlp_norm, l3_w_up, l3_w_down, final_norm):
    b, s, d = x.shape
    bf = lambda w: w.astype(BF16)
    h = x.reshape(b * s, d)

    qkv = _norm_proj(h, l0_attn_norm, bf(l0_w_qkv))
    a = _moba_attention(qkv.reshape(b, s, -1), b, s).reshape(b * s, -1)
    h = _post(h, (a,), bf(l0_w_o), l0_mlp_norm, bf(l0_w_up), bf(l0_w_down))

    qkvs = _dil_proj(h, l1_attn_norm, bf(l1_w_qkv))
    groups = [_dilated_group(qkvs[g], b, s, g, window, dil) for g, (window, dil) in enumerate(DIL_PAIRS)]
    h = _post(h, tuple(o for o, _ in groups) + tuple(l for _, l in groups),
              bf(l1_w_o), l1_mlp_norm, bf(l1_w_up), bf(l1_w_down))

    q, k, v = _mla_proj(h, l2_attn_norm, l2_w_dkv, l2_q_norm, l2_w_uq, l2_kv_norm, l2_w_ukv, s)
    a = _mla_attention(q.reshape(b, s, -1), k.reshape(b, s, -1), v.reshape(b, s, -1), b, s).reshape(b * s, -1)
    h = _post(h, (a,), bf(l2_w_o), l2_mlp_norm, bf(l2_w_up), bf(l2_w_down))

    qkv = _norm_proj(h, l3_attn_norm, bf(l3_w_qkv), col_tile=256)
    a = _swa_attention(qkv.reshape(b, s, -1), l3_sinks, b, s).reshape(b * s, -1)
    h = _post(h, (a,), bf(l3_w_o), l3_mlp_norm, bf(l3_w_up), bf(l3_w_down), g_final=final_norm)
    return h.reshape(b, s, d)
```

```python
import functools

import jax
import jax.numpy as jnp
from jax import lax
from jax.experimental import pallas as pl
from jax.experimental.pallas import tpu as pltpu

F32 = jnp.float32
BF16 = jnp.bfloat16

D_MODEL = 1024
HEAD_DIM = 64
RMS_EPS = 1e-6
D_FF = 4 * D_MODEL
NEG_INF = -1e30

MOBA_HEADS = 16
MOBA_BLOCK = 256
MOBA_TOPK = 3

DIL_PAIRS = ((128, 1), (512, 4), (2048, 16))
DIL_HEADS_PER_GROUP = 8
DIL_BAND = 128

MLA_HEADS = 16
MLA_Q_RANK = 768
MLA_KV_RANK = 256
MLA_NOPE = 64
MLA_ROPE = 32
MLA_V = 64
ROPE_THETA = 10000.0
MLA_TILE = 256
MLA_HEADS_PER_STEP = 4
MOBA_HEADS_PER_STEP = 4

SWA_Q_HEADS = 16
SWA_KV_HEADS = 2
SWA_WINDOW = 128
SWA_BLOCK = 128

LANES = 128
ROW_TILE = 512
FF_TILE = 512
VMEM_LIMIT = 56 * 1024 * 1024
LOG2E = 1.4426950408889634
MOBA_Q_SCALE = HEAD_DIM ** -0.5 * LOG2E
MLA_Q_SCALE = (MLA_NOPE + MLA_ROPE) ** -0.5 * LOG2E


def _params(semantics):
    return pltpu.CompilerParams(dimension_semantics=semantics, vmem_limit_bytes=VMEM_LIMIT)


def _resident(shape):
    return pl.BlockSpec(shape, lambda *_: (0,) * len(shape), pipeline_mode=pl.Buffered(1))


def _alibi_slopes(n_heads):
    return 2.0 ** (-8.0 * jnp.arange(1, n_heads + 1, dtype=F32) / n_heads)


def _rms(x, g):
    return x * lax.rsqrt(jnp.mean(x * x, axis=-1, keepdims=True) + RMS_EPS) * g


def _dot(a, b):
    return jnp.dot(a, b, preferred_element_type=F32)


def _dot_nt(a, b):
    return lax.dot_general(a, b, (((1,), (1,)), ((), ())), preferred_element_type=F32)


def _norm_proj_kernel(h_ref, g_ref, w_ref, o_ref, *, col_tile, scaled_cols, scale):
    n = _rms(h_ref[...], g_ref[...]).astype(BF16)
    for c in range(o_ref.shape[1] // col_tile):
        cols = slice(c * col_tile, (c + 1) * col_tile)
        y = _dot(n, w_ref[:, cols])
        if (c + 1) * col_tile <= scaled_cols:
            y = y * scale
        o_ref[:, cols] = y.astype(o_ref.dtype)


def _norm_proj(h, g, w, col_tile=512, scaled_cols=0, scale=1.0):
    m, d = h.shape
    n_out = w.shape[1]
    assert scaled_cols % col_tile == 0
    return pl.pallas_call(
        functools.partial(_norm_proj_kernel, col_tile=col_tile, scaled_cols=scaled_cols, scale=scale),
        grid=(m // ROW_TILE,),
        in_specs=[pl.BlockSpec((ROW_TILE, d), lambda i: (i, 0)),
                  _resident((1, d)),
                  _resident((d, n_out))],
        out_specs=pl.BlockSpec((ROW_TILE, n_out), lambda i: (i, 0)),
        out_shape=jax.ShapeDtypeStruct((m, n_out), BF16),
        compiler_params=_params(("parallel",)),
        name="norm_proj",
    )(h, g.reshape(1, d), w)


def _dil_proj_kernel(h_ref, g_ref, w_ref, o0_ref, o1_ref, o2_ref, stage_ref):
    n = _rms(h_ref[...], g_ref[...]).astype(BF16)
    rows = h_ref.shape[0]
    width = DIL_HEADS_PER_GROUP * HEAD_DIM
    for g, (o_ref, (_, dil)) in enumerate(zip((o0_ref, o1_ref, o2_ref), DIL_PAIRS)):
        for part in range(3):
            src = (3 * g + part) * width
            y = _dot(n, w_ref[:, src:src + width])
            if dil == 1:
                o_ref[:, part * width:(part + 1) * width] = y.astype(BF16)
                continue
            for t in range(width // LANES):
                stage_ref[g - 1, part, t] = y[:, t * LANES:(t + 1) * LANES]
            for r in range(dil):
                for t in range(width // LANES):
                    dst = (3 * r + part) * width + t * LANES
                    strided = stage_ref[g - 1, part, t, pl.ds(r, rows // dil, stride=dil), :]
                    o_ref[:, dst:dst + LANES] = strided.astype(BF16)


def _dil_proj(h, g, w):
    m, d = h.shape
    feat = 3 * DIL_HEADS_PER_GROUP * HEAD_DIM
    assert DIL_PAIRS[0][1] == 1
    view = lambda dil: (m // dil, dil * feat)
    view_block = lambda dil: pl.BlockSpec((ROW_TILE // dil, dil * feat), lambda i: (i, 0))
    return pl.pallas_call(
        _dil_proj_kernel,
        grid=(m // ROW_TILE,),
        in_specs=[pl.BlockSpec((ROW_TILE, d), lambda i: (i, 0)), _resident((1, d)), _resident(w.shape)],
        out_specs=[view_block(dil) for _, dil in DIL_PAIRS],
        out_shape=[jax.ShapeDtypeStruct(view(dil), BF16) for _, dil in DIL_PAIRS],
        scratch_shapes=[pltpu.VMEM((len(DIL_PAIRS) - 1, 3, feat // 3 // LANES, ROW_TILE, LANES), F32)],
        compiler_params=_params(("parallel",)),
        name="dil_proj",
    )(h, g.reshape(1, d), w)


def _mlp_tail(h1, g_ref, wup_ref, wdn_ref, gf_ref, out_ref):
    n = _rms(h1, g_ref[...]).astype(BF16)
    acc = jnp.zeros_like(h1)
    for c in range(wup_ref.shape[1] // FF_TILE):
        cols = slice(c * FF_TILE, (c + 1) * FF_TILE)
        u = jnp.square(jnp.maximum(_dot(n, wup_ref[:, cols]), 0.0)).astype(BF16)
        acc = acc + _dot(u, wdn_ref[cols, :])
    out = h1 + acc
    if gf_ref is not None:
        out = _rms(out, gf_ref[...])
    out_ref[...] = out


def _post_kernel(h_ref, a_ref, wo_ref, g_ref, wup_ref, wdn_ref, *rest):
    gf_ref, out_ref = (rest[0], rest[1]) if len(rest) == 2 else (None, rest[0])
    h1 = h_ref[...] + _dot(a_ref[...], wo_ref[...])
    _mlp_tail(h1, g_ref, wup_ref, wdn_ref, gf_ref, out_ref)


def _post_merge_kernel(h_ref, o0_ref, o1_ref, o2_ref, l0_ref, l1_ref, l2_ref,
                       wo_ref, g_ref, wup_ref, wdn_ref, out_ref, stage_ref):
    rows, width = o0_ref.shape

    def token_rows(x_ref, slot, dil):
        if dil == 1:
            return x_ref[...]
        for r in range(dil):
            for t in range(width // LANES):
                src = r * width + t * LANES
                stage_ref[slot, t, pl.ds(r, rows // dil, stride=dil), :] = x_ref[:, src:src + LANES]
        return jnp.concatenate([stage_ref[slot, t] for t in range(width // LANES)], axis=1)

    dils = [dil for _, dil in DIL_PAIRS]
    l0, l1, l2 = (token_rows(ref, slot, dil) for slot, (ref, dil) in enumerate(zip((l0_ref, l1_ref, l2_ref), dils)))
    o0, o1, o2 = (token_rows(ref, 3 + slot, dil) for slot, (ref, dil) in enumerate(zip((o0_ref, o1_ref, o2_ref), dils)))
    mx = jnp.maximum(jnp.maximum(l0, l1), l2)
    e0, e1, e2 = jnp.exp(l0 - mx), jnp.exp(l1 - mx), jnp.exp(l2 - mx)
    merged = (e0 * o0 + e1 * o1 + e2 * o2) / (e0 + e1 + e2)
    h1 = h_ref[...] + _dot(merged.astype(BF16), wo_ref[...])
    _mlp_tail(h1, g_ref, wup_ref, wdn_ref, None, out_ref)


def _post(h, attn_ins, w_o, g, w_up, w_down, g_final=None):
    m, d = h.shape
    row = lambda width: pl.BlockSpec((ROW_TILE, width), lambda i: (i, 0))
    merge = len(attn_ins) > 1
    in_specs = [row(d)] + [pl.BlockSpec((ROW_TILE * a.shape[0] // m, a.shape[1]), lambda i: (i, 0)) for a in attn_ins]
    in_specs += [_resident(w_o.shape), _resident((1, d)), _resident(w_up.shape), _resident(w_down.shape)]
    args = [h, *attn_ins, w_o, g.reshape(1, d), w_up, w_down]
    if g_final is not None:
        in_specs.append(_resident((1, d)))
        args.append(g_final.reshape(1, d))
    scratch = [pltpu.VMEM((len(attn_ins), w_o.shape[0] // LANES, ROW_TILE, LANES), F32)] if merge else []
    return pl.pallas_call(
        _post_merge_kernel if merge else _post_kernel,
        grid=(m // ROW_TILE,),
        in_specs=in_specs,
        out_specs=row(d),
        out_shape=jax.ShapeDtypeStruct((m, d), F32),
        scratch_shapes=scratch,
        compiler_params=_params(("parallel",)),
        name="post_mlp",
    )(*args)


ONES_ROWS = 16


class _SoftmaxState:
    def __init__(self, m_ref, acc_ref, dv):
        self.m_ref, self.acc_ref, self.dv = m_ref, acc_ref, dv

    @staticmethod
    def scratch(n_heads, dv, tq):
        return [pltpu.VMEM((n_heads, 1, tq), F32), pltpu.VMEM((n_heads, dv + ONES_ROWS, tq), F32)]

    @staticmethod
    def store_values_t(vt_ref, hh, cols, v_t):
        dv = v_t.shape[0]
        vt_ref[hh, :dv, cols] = v_t.astype(BF16)
        vt_ref[hh, dv:, cols] = jnp.ones((ONES_ROWS, v_t.shape[1]), BF16)

    def init(self):
        self.m_ref[...] = jnp.full(self.m_ref.shape, NEG_INF, F32)
        self.acc_ref[...] = jnp.zeros(self.acc_ref.shape, F32)

    def step(self, hh, x, v_t, shift=None):
        m = self.m_ref[hh]
        m_cur = jnp.max(x, axis=0, keepdims=True)
        if shift is not None:
            m_cur = m_cur - shift
        m_new = jnp.maximum(m, m_cur)
        alpha = jnp.exp2(m - m_new)
        p = jnp.exp2(x - (m_new if shift is None else m_new + shift))
        self.m_ref[hh] = m_new
        self.acc_ref[hh] = alpha * self.acc_ref[hh] + _dot(v_t, p.astype(BF16))

    def normalized(self, hh):
        return self.acc_ref[hh, :self.dv, :] / self.acc_ref[hh, self.dv:self.dv + 1, :]


def _sweep_blocks(n_past, n_heads, score_fn, own_fn, consume_fn, buf_a, buf_b):
    def produce(buf, block):
        for hh in range(n_heads):
            buf[hh] = score_fn(hh, block)

    def consume(buf, block):
        for hh in range(n_heads):
            consume_fn(hh, block, buf[hh])

    produce(buf_b, n_past)
    produce(buf_a, 0)
    for hh in range(n_heads):
        own_fn(hh, buf_b[hh])

    def pair(p, carry):
        first = 2 * p
        produce(buf_b, first + 1)
        consume(buf_a, first)
        produce(buf_a, jnp.minimum(first + 2, n_past - 1))
        consume(buf_b, first + 1)
        return carry

    lax.fori_loop(0, n_past // 2, pair, 0)

    @pl.when(n_past % 2 == 1)
    def _():
        consume(buf_a, n_past - 1)


def _transpose_bf16(x):
    return x.astype(F32).T.astype(BF16)


def _keep_head_rows(x_t, half):
    r = lax.broadcasted_iota(jnp.int32, x_t.shape, 0)
    keep = (r < HEAD_DIM) if half == 0 else (r >= HEAD_DIM)
    return jnp.where(keep, x_t, jnp.zeros_like(x_t))


def _moba_kernel(slopes_ref, q_ref, k_ref, v_ref, o_ref, kmean_ref, vt_ref, term_ref, alibi_ref, qh_ref,
                 buf_a, buf_b, m_ref, acc_ref, *, n_blocks):
    blk = MOBA_BLOCK
    n_heads = MOBA_HEADS_PER_STEP
    group = pl.program_id(1)
    i = pl.program_id(2)
    state = _SoftmaxState(m_ref, acc_ref, HEAD_DIM)

    @pl.when(i == 0)
    def _():
        for j in range(n_blocks):
            rows = slice(j * blk, (j + 1) * blk)
            kmean_ref[j:j + 1, :] = jnp.mean(k_ref[0, rows, :].astype(F32), axis=0, keepdims=True)
            v_t = v_ref[0, rows, :].astype(F32).T
            for hh in range(n_heads):
                state.store_values_t(vt_ref, hh, rows, v_t[hh * HEAD_DIM:(hh + 1) * HEAD_DIM, :])

    state.init()
    q_t = q_ref[0].astype(F32).T
    key = lax.broadcasted_iota(jnp.int32, (blk, blk), 0)
    qry = lax.broadcasted_iota(jnp.int32, (blk, blk), 1)
    rel = (qry - key).astype(F32)
    blk_idx = lax.broadcasted_iota(jnp.int32, (n_blocks, blk), 0)
    fully_past = blk_idx < i
    block_dist = ((i - blk_idx) * blk).astype(F32)
    lane_tile = lambda hh: slice((hh // 2) * LANES, (hh // 2 + 1) * LANES)

    for hh in range(n_heads):
        slope_s = slopes_ref[n_heads * group + hh] * LOG2E
        q_h = _keep_head_rows(q_t[lane_tile(hh), :], hh % 2)
        gate = lax.dot_general(kmean_ref[:, lane_tile(hh)], q_h, (((1,), (0,)), ((), ())),
                               precision=lax.Precision.HIGHEST, preferred_element_type=F32)
        work = jnp.where(fully_past, gate, NEG_INF * MOBA_Q_SCALE)
        chosen = jnp.zeros((n_blocks, blk), jnp.bool_)
        for _ in range(min(MOBA_TOPK, n_blocks)):
            best = jnp.max(work, axis=0, keepdims=True)
            first = jnp.min(jnp.where(work == best, blk_idx, n_blocks), axis=0, keepdims=True)
            pick = blk_idx == first
            chosen = jnp.logical_or(chosen, pick)
            work = jnp.where(pick, -jnp.inf, work)
        term_ref[hh] = jnp.where(jnp.logical_and(chosen, fully_past), slope_s * block_dist, -NEG_INF)
        qh_ref[hh] = q_h.astype(BF16)
        alibi_ref[hh] = slope_s * rel

    def rows_of(block):
        return pl.ds(pl.multiple_of(block * blk, blk), blk)

    def scores(hh, block):
        return _dot(k_ref[0, rows_of(block), lane_tile(hh)], qh_ref[hh])

    def v_t(hh, block):
        return vt_ref[hh, :, rows_of(block)]

    def own_block(hh, s):
        state.step(hh, jnp.where(rel >= 0, s - alibi_ref[hh], NEG_INF), v_t(hh, i))

    def past_block(hh, block, s):
        state.step(hh, s - alibi_ref[hh], v_t(hh, block), shift=term_ref[hh, pl.ds(block, 1), :])

    _sweep_blocks(i, n_heads, scores, own_block, past_block, buf_a, buf_b)
    o_t = jnp.concatenate([state.normalized(hh) for hh in range(n_heads)], axis=0)
    o_ref[0] = o_t.T.astype(o_ref.dtype)


def _moba_attention(qkv, b, s):
    n_blocks = s // MOBA_BLOCK
    n_heads = MOBA_HEADS_PER_STEP
    width = n_heads * HEAD_DIM
    n_groups = MOBA_HEADS // n_heads
    score_buf = pltpu.VMEM((n_heads, MOBA_BLOCK, MOBA_BLOCK), F32)
    return pl.pallas_call(
        functools.partial(_moba_kernel, n_blocks=n_blocks),
        grid=(b, n_groups, n_blocks),
        in_specs=[pl.BlockSpec(memory_space=pltpu.SMEM),
                  pl.BlockSpec((1, MOBA_BLOCK, width), lambda bi, g, i: (bi, i, g)),
                  pl.BlockSpec((1, s, width), lambda bi, g, i: (bi, 0, n_groups + g)),
                  pl.BlockSpec((1, s, width), lambda bi, g, i: (bi, 0, 2 * n_groups + g))],
        out_specs=pl.BlockSpec((1, MOBA_BLOCK, width), lambda bi, g, i: (bi, i, g)),
        out_shape=jax.ShapeDtypeStruct((b, s, MOBA_HEADS * HEAD_DIM), BF16),
        scratch_shapes=[pltpu.VMEM((n_blocks, width), F32),
                        pltpu.VMEM((n_heads, HEAD_DIM + ONES_ROWS, s), BF16),
                        pltpu.VMEM((n_heads, n_blocks, MOBA_BLOCK), F32),
                        pltpu.VMEM((n_heads, MOBA_BLOCK, MOBA_BLOCK), F32),
                        pltpu.VMEM((n_heads, LANES, MOBA_BLOCK), BF16),
                        score_buf, score_buf,
                        *_SoftmaxState.scratch(n_heads, HEAD_DIM, MOBA_BLOCK)],
        compiler_params=_params(("parallel", "parallel", "arbitrary")),
        name="moba_attention",
    )(_alibi_slopes(MOBA_HEADS), qkv, qkv, qkv)


def _lane_half(x, half):
    lane = lax.broadcasted_iota(jnp.int32, x.shape, x.ndim - 1)
    keep = (lane < HEAD_DIM) if half == 0 else (lane >= HEAD_DIM)
    return jnp.where(keep, x, jnp.zeros_like(x))


def _band_attention(n_heads, q_tile, k_tile, v_tile, head_softmax, store_pair, s_ref):
    for h in range(n_heads):
        s_ref[h] = _dot_nt(_lane_half(q_tile(h), h % 2), k_tile(h))
    even = None
    for h in range(n_heads):
        p, den, extra = head_softmax(h, s_ref[h])
        o = _dot(p.astype(BF16), v_tile(h)) / den
        if h % 2 == 0:
            even = (o, extra)
            continue
        lane = lax.broadcasted_iota(jnp.int32, o.shape, 1)
        store_pair(h // 2, lane < HEAD_DIM, even, (o, extra))


def _band_geometry(band, n):
    row = lax.broadcasted_iota(jnp.int32, (band, 2 * band), 0)
    col = lax.broadcasted_iota(jnp.int32, (band, 2 * band), 1)
    diff = row + band - col
    in_sequence = (n > 0) | (col >= band)
    return diff, in_sequence


def _dil_kernel(slopes_ref, q_ref, kp_ref, kc_ref, vp_ref, vc_ref, o_ref, lse_ref, s_ref, *, n_pts, dil, group):
    n = pl.program_id(2)
    scale = HEAD_DIM ** -0.5
    diff, in_sequence = _band_geometry(DIL_BAND, n)
    mask = (diff >= 0) & (diff <= n_pts) & in_sequence
    dist = (dil * diff).astype(F32)
    tile = lambda h: slice((h // 2) * LANES, (h // 2 + 1) * LANES)

    def head_softmax(h, s):
        slope = slopes_ref[group * DIL_HEADS_PER_GROUP + h]
        logits = jnp.where(mask, s * scale - slope * dist, NEG_INF)
        m = jnp.max(logits, axis=1, keepdims=True)
        e = jnp.exp(logits - m)
        den = jnp.sum(e, axis=1, keepdims=True)
        return e, den, m + jnp.log(den)

    def store_pair(pair, first_half, even, odd):
        cols = slice(pair * LANES, (pair + 1) * LANES)
        o_ref[0, :, cols] = jnp.where(first_half, even[0], odd[0])
        lse_ref[0, :, cols] = jnp.where(first_half, even[1], odd[1])

    _band_attention(
        DIL_HEADS_PER_GROUP,
        lambda h: q_ref[0, :, tile(h)],
        lambda h: jnp.concatenate([kp_ref[0, :, tile(h)], kc_ref[0, :, tile(h)]], axis=0),
        lambda h: jnp.concatenate([vp_ref[0, :, tile(h)], vc_ref[0, :, tile(h)]], axis=0),
        head_softmax, store_pair, s_ref)


def _dilated_group(qkv, b, s, group, window, dil):
    width = DIL_HEADS_PER_GROUP * HEAD_DIM
    n_sub = s // dil
    assert n_sub % DIL_BAND == 0
    view = qkv.reshape(b, n_sub, dil * 3 * width)

    def col(part):
        return lambda bi, r, n: (bi, n, r * 3 + part)

    def col_prev(part):
        return lambda bi, r, n: (bi, jnp.maximum(n - 1, 0), r * 3 + part)

    blk = (1, DIL_BAND, width)
    out_spec = pl.BlockSpec(blk, lambda bi, r, n: (bi, n, r))
    out_sds = jax.ShapeDtypeStruct((b, n_sub, dil * width), F32)
    o, lse = pl.pallas_call(
        functools.partial(_dil_kernel, n_pts=window // dil, dil=dil, group=group),
        grid=(b, dil, n_sub // DIL_BAND),
        in_specs=[pl.BlockSpec(memory_space=pltpu.SMEM),
                  pl.BlockSpec(blk, col(0)),
                  pl.BlockSpec(blk, col_prev(1)), pl.BlockSpec(blk, col(1)),
                  pl.BlockSpec(blk, col_prev(2)), pl.BlockSpec(blk, col(2))],
        out_specs=[out_spec, out_spec],
        out_shape=[out_sds, out_sds],
        scratch_shapes=[pltpu.VMEM((DIL_HEADS_PER_GROUP, DIL_BAND, 2 * DIL_BAND), F32)],
        compiler_params=_params(("parallel", "parallel", "arbitrary")),
        name=f"dilated_attention_g{group}",
    )(_alibi_slopes(len(DIL_PAIRS) * DIL_HEADS_PER_GROUP), view, view, view, view, view)
    return o.reshape(b * n_sub, dil * width), lse.reshape(b * n_sub, dil * width)


def _mla_proj_kernel(h_ref, g_ref, wd_ref, gq_ref, gkv_ref, wuq_ref, wuk_ref, wuv_ref,
                     cos_ref, sin_up_ref, sin_dn_ref, q_ref, k_ref, v_ref):
    n = _rms(h_ref[...], g_ref[...]).astype(BF16)
    c = _dot(n, wd_ref[...])
    nq = _rms(c[:, :MLA_Q_RANK], gq_ref[...]).astype(BF16)
    nkv = _rms(c[:, MLA_Q_RANK:MLA_Q_RANK + MLA_KV_RANK], gkv_ref[...]).astype(BF16)
    k_rope = c[:, MLA_Q_RANK + MLA_KV_RANK:]
    cos, sin_up, sin_dn = cos_ref[...], sin_up_ref[...], sin_dn_ref[...]
    half = MLA_ROPE // 2

    def rope(x):
        return x * cos + pltpu.roll(x, half, 1) * sin_up + pltpu.roll(x, LANES - half, 1) * sin_dn

    kr = rope(k_rope)
    for h in range(MLA_HEADS):
        cols = slice(h * LANES, (h + 1) * LANES)
        q_ref[:, cols] = (rope(_dot(nq, wuq_ref[:, cols])) * MLA_Q_SCALE).astype(BF16)
        k_ref[:, cols] = (_dot(nkv, wuk_ref[:, cols]) + kr).astype(BF16)
    v_ref[...] = _dot(nkv, wuv_ref[...]).astype(BF16)


def _mla_proj(h, g, w_dkv, q_norm, w_uq, kv_norm, w_ukv, s):
    m, d = h.shape
    qk = MLA_NOPE + MLA_ROPE
    wd = jnp.concatenate([w_dkv[:, :MLA_Q_RANK + MLA_KV_RANK],
                          jnp.zeros((d, MLA_NOPE), F32), w_dkv[:, MLA_Q_RANK + MLA_KV_RANK:],
                          jnp.zeros((d, LANES - qk), F32)], axis=1).astype(BF16)
    wuq = jnp.pad(w_uq.reshape(MLA_Q_RANK, MLA_HEADS, qk), ((0, 0), (0, 0), (0, LANES - qk)))
    wuq = wuq.reshape(MLA_Q_RANK, MLA_HEADS * LANES).astype(BF16)
    w_ukv = w_ukv.reshape(MLA_KV_RANK, MLA_HEADS, MLA_NOPE + MLA_V)
    wuk = jnp.pad(w_ukv[:, :, :MLA_NOPE], ((0, 0), (0, 0), (0, LANES - MLA_NOPE)))
    wuk = wuk.reshape(MLA_KV_RANK, MLA_HEADS * LANES).astype(BF16)
    wuv = w_ukv[:, :, MLA_NOPE:].reshape(MLA_KV_RANK, MLA_HEADS * MLA_V).astype(BF16)
    half = MLA_ROPE // 2
    inv = ROPE_THETA ** (-jnp.arange(0, MLA_ROPE, 2, dtype=F32) / MLA_ROPE)
    ang = jnp.arange(s).astype(F32)[:, None] * inv[None, :]
    cos, sin = jnp.cos(ang), jnp.sin(ang)
    zeros = lambda w: jnp.zeros((s, w), F32)
    cos_t = jnp.concatenate([jnp.ones((s, MLA_NOPE), F32), cos, cos, zeros(LANES - qk)], axis=1)
    sin_up = jnp.concatenate([zeros(MLA_NOPE + half), sin, zeros(LANES - qk)], axis=1)
    sin_dn = jnp.concatenate([zeros(MLA_NOPE), -sin, zeros(half + LANES - qk)], axis=1)

    tiles_per_seq = s // ROW_TILE
    row = lambda width: pl.BlockSpec((ROW_TILE, width), lambda i: (i, 0))
    table = pl.BlockSpec((ROW_TILE, LANES), lambda i: (i % tiles_per_seq, 0))
    return pl.pallas_call(
        _mla_proj_kernel,
        grid=(m // ROW_TILE,),
        in_specs=[row(d), _resident((1, d)), _resident(wd.shape),
                  _resident((1, MLA_Q_RANK)), _resident((1, MLA_KV_RANK)),
                  _resident(wuq.shape), _resident(wuk.shape), _resident(wuv.shape),
                  table, table, table],
        out_specs=[row(MLA_HEADS * LANES), row(MLA_HEADS * LANES), row(MLA_HEADS * MLA_V)],
        out_shape=[jax.ShapeDtypeStruct((m, MLA_HEADS * LANES), BF16),
                   jax.ShapeDtypeStruct((m, MLA_HEADS * LANES), BF16),
                   jax.ShapeDtypeStruct((m, MLA_HEADS * MLA_V), BF16)],
        compiler_params=_params(("parallel",)),
        name="mla_proj",
    )(h, g.reshape(1, d), wd, q_norm.reshape(1, -1), kv_norm.reshape(1, -1), wuq, wuk, wuv,
      cos_t, sin_up, sin_dn)


def _mla_kernel(q_ref, k_ref, v_ref, o_ref, vt_ref, qh_ref, buf_a, buf_b, m_ref, acc_ref, *, n_tiles):
    t = MLA_TILE
    n_heads = MLA_HEADS_PER_STEP
    i = pl.program_id(2)
    state = _SoftmaxState(m_ref, acc_ref, MLA_V)

    @pl.when(i == 0)
    def _():
        for j in range(n_tiles):
            rows = slice(j * t, (j + 1) * t)
            v_t = v_ref[0, rows, :].astype(F32).T
            for hh in range(n_heads):
                state.store_values_t(vt_ref, hh, rows, v_t[hh * MLA_V:(hh + 1) * MLA_V, :])

    state.init()
    key = lax.broadcasted_iota(jnp.int32, (t, t), 0)
    qry = lax.broadcasted_iota(jnp.int32, (t, t), 1)
    lane_tile = lambda hh: slice(hh * LANES, (hh + 1) * LANES)
    for hh in range(n_heads):
        qh_ref[hh] = _transpose_bf16(q_ref[0, :, lane_tile(hh)])

    def rows_of(block):
        return pl.ds(pl.multiple_of(block * t, t), t)

    def scores(hh, block):
        return _dot(k_ref[0, rows_of(block), lane_tile(hh)], qh_ref[hh])

    def v_t(hh, block):
        return vt_ref[hh, :, rows_of(block)]

    def own_block(hh, s):
        state.step(hh, jnp.where(key <= qry, s, NEG_INF), v_t(hh, i))

    def past_block(hh, block, s):
        state.step(hh, s, v_t(hh, block))

    _sweep_blocks(i, n_heads, scores, own_block, past_block, buf_a, buf_b)
    o_t = jnp.concatenate([state.normalized(hh) for hh in range(n_heads)], axis=0)
    o_ref[0] = o_t.T.astype(o_ref.dtype)


def _mla_attention(q, k, v, b, s):
    n_heads = MLA_HEADS_PER_STEP
    n_groups = MLA_HEADS // n_heads
    whole_seq = lambda bi, g, i: (bi, 0, g)
    tile = lambda bi, g, i: (bi, i, g)
    score_buf = pltpu.VMEM((n_heads, MLA_TILE, MLA_TILE), F32)
    return pl.pallas_call(
        functools.partial(_mla_kernel, n_tiles=s // MLA_TILE),
        grid=(b, n_groups, s // MLA_TILE),
        in_specs=[pl.BlockSpec((1, MLA_TILE, n_heads * LANES), tile),
                  pl.BlockSpec((1, s, n_heads * LANES), whole_seq),
                  pl.BlockSpec((1, s, n_heads * MLA_V), whole_seq)],
        out_specs=pl.BlockSpec((1, MLA_TILE, n_heads * MLA_V), tile),
        out_shape=jax.ShapeDtypeStruct((b, s, MLA_HEADS * MLA_V), BF16),
        scratch_shapes=[pltpu.VMEM((n_heads, MLA_V + ONES_ROWS, s), BF16),
                        pltpu.VMEM((n_heads, LANES, MLA_TILE), BF16),
                        score_buf, score_buf,
                        *_SoftmaxState.scratch(n_heads, MLA_V, MLA_TILE)],
        compiler_params=_params(("parallel", "parallel", "arbitrary")),
        name="mla_attention",
    )(q, k, v)


def _swa_kernel(slopes_ref, sinks_ref, q_ref, kp_ref, kc_ref, vp_ref, vc_ref, o_ref, s_ref):
    n = pl.program_id(1)
    scale = HEAD_DIM ** -0.5
    group = SWA_Q_HEADS // SWA_KV_HEADS
    diff, in_sequence = _band_geometry(SWA_BLOCK, n)
    mask = (diff >= 0) & (diff < SWA_WINDOW) & in_sequence
    dist = diff.astype(F32)

    def both_halves(prev_ref, cur_ref, kv_head):
        x = jnp.concatenate([prev_ref[0], cur_ref[0]], axis=0)
        part = x[:, kv_head * HEAD_DIM:(kv_head + 1) * HEAD_DIM]
        return jnp.concatenate([part, part], axis=1)

    k_tiles = [both_halves(kp_ref, kc_ref, kv) for kv in range(SWA_KV_HEADS)]
    v_tiles = [both_halves(vp_ref, vc_ref, kv) for kv in range(SWA_KV_HEADS)]

    def head_softmax(h, s):
        logits = jnp.where(mask, s * scale - slopes_ref[h] * dist, NEG_INF)
        m = jnp.maximum(jnp.max(logits, axis=1, keepdims=True), sinks_ref[h])
        e = jnp.exp(logits - m)
        return e, jnp.sum(e, axis=1, keepdims=True) + jnp.exp(sinks_ref[h] - m), None

    def store_pair(pair, first_half, even, odd):
        o_ref[0, :, pair * LANES:(pair + 1) * LANES] = jnp.where(first_half, even[0], odd[0]).astype(o_ref.dtype)

    _band_attention(
        SWA_Q_HEADS,
        lambda h: q_ref[0, :, (h // 2) * LANES:(h // 2 + 1) * LANES],
        lambda h: k_tiles[h // group],
        lambda h: v_tiles[h // group],
        head_softmax, store_pair, s_ref)


def _swa_attention(qkv, sinks, b, s):
    q_width = SWA_Q_HEADS * HEAD_DIM
    kv_width = SWA_KV_HEADS * HEAD_DIM
    k_tile = q_width // kv_width
    cur = lambda t: (lambda bi, n: (bi, n, t))
    prev = lambda t: (lambda bi, n: (bi, jnp.maximum(n - 1, 0), t))
    kv_blk = (1, SWA_BLOCK, kv_width)
    return pl.pallas_call(
        _swa_kernel,
        grid=(b, s // SWA_BLOCK),
        in_specs=[pl.BlockSpec(memory_space=pltpu.SMEM), pl.BlockSpec(memory_space=pltpu.SMEM),
                  pl.BlockSpec((1, SWA_BLOCK, q_width), cur(0)),
                  pl.BlockSpec(kv_blk, prev(k_tile)), pl.BlockSpec(kv_blk, cur(k_tile)),
                  pl.BlockSpec(kv_blk, prev(k_tile + 1)), pl.BlockSpec(kv_blk, cur(k_tile + 1))],
        out_specs=pl.BlockSpec((1, SWA_BLOCK, q_width), cur(0)),
        out_shape=jax.ShapeDtypeStruct((b, s, q_width), BF16),
        scratch_shapes=[pltpu.VMEM((SWA_Q_HEADS, SWA_BLOCK, 2 * SWA_BLOCK), F32)],
        compiler_params=_params(("parallel", "arbitrary")),
        name="swa_attention",
    )(_alibi_slopes(SWA_Q_HEADS), sinks.astype(F32), qkv, qkv, qkv, qkv, qkv)


def kernel(x, l0_attn_norm, l0_w_qkv, l0_w_o, l0_mlp_norm, l0_w_up, l0_w_down, l1_attn_norm, l1_w_qkv, l1_w_o, l1_mlp_norm, l1_w_up, l1_w_down, l2_attn_norm, l2_w_dkv, l2_q_norm, l2_w_uq, l2_kv_norm, l2_w_ukv, l2_w_o, l2_mlp_norm, l2_w_up, l2_w_down, l3_attn_norm, l3_w_qkv, l3_sinks, l3_w_o, l3_mlp_norm, l3_w_up, l3_w_down, final_norm):
    b, s, d = x.shape
    bf = lambda w: w.astype(BF16)
    h = x.reshape(b * s, d)

    qkv = _norm_proj(h, l0_attn_norm, bf(l0_w_qkv), scaled_cols=MOBA_HEADS * HEAD_DIM, scale=MOBA_Q_SCALE)
    a = _moba_attention(qkv.reshape(b, s, -1), b, s).reshape(b * s, -1)
    h = _post(h, (a,), bf(l0_w_o), l0_mlp_norm, bf(l0_w_up), bf(l0_w_down))

    qkvs = _dil_proj(h, l1_attn_norm, bf(l1_w_qkv))
    groups = [_dilated_group(qkvs[g], b, s, g, window, dil) for g, (window, dil) in enumerate(DIL_PAIRS)]
    h = _post(h, tuple(o for o, _ in groups) + tuple(l for _, l in groups),
              bf(l1_w_o), l1_mlp_norm, bf(l1_w_up), bf(l1_w_down))

    q, k, v = _mla_proj(h, l2_attn_norm, l2_w_dkv, l2_q_norm, l2_w_uq, l2_kv_norm, l2_w_ukv, s)
    a = _mla_attention(q.reshape(b, s, -1), k.reshape(b, s, -1), v.reshape(b, s, -1), b, s).reshape(b * s, -1)
    h = _post(h, (a,), bf(l2_w_o), l2_mlp_norm, bf(l2_w_up), bf(l2_w_down))

    qkv = _norm_proj(h, l3_attn_norm, bf(l3_w_qkv), col_tile=256)
    a = _swa_attention(qkv.reshape(b, s, -1), l3_sinks, b, s).reshape(b * s, -1)
    h = _post(h, (a,), bf(l3_w_o), l3_mlp_norm, bf(l3_w_up), bf(l3_w_down), g_final=final_norm)
    return h.reshape(b, s, d)
```

```python
import functools

import jax
import jax.numpy as jnp
from jax import lax
from jax.experimental import pallas as pl
from jax.experimental.pallas import tpu as pltpu

F32 = jnp.float32
BF16 = jnp.bfloat16

D_MODEL = 1024
HEAD_DIM = 64
RMS_EPS = 1e-6
D_FF = 4 * D_MODEL
NEG_INF = -1e30

MOBA_HEADS = 16
MOBA_BLOCK = 256
MOBA_TOPK = 3

DIL_PAIRS = ((128, 1), (512, 4), (2048, 16))
DIL_HEADS_PER_GROUP = 8
DIL_BAND = 128

MLA_HEADS = 16
MLA_Q_RANK = 768
MLA_KV_RANK = 256
MLA_NOPE = 64
MLA_ROPE = 32
MLA_V = 64
ROPE_THETA = 10000.0
MLA_TILE = 256
MLA_HEADS_PER_STEP = 4
MOBA_HEADS_PER_STEP = 4

SWA_Q_HEADS = 16
SWA_KV_HEADS = 2
SWA_WINDOW = 128
SWA_BLOCK = 128

LANES = 128
ROW_TILE = 512
FF_TILE = 512
VMEM_LIMIT = 56 * 1024 * 1024
LOG2E = 1.4426950408889634
MOBA_Q_SCALE = HEAD_DIM ** -0.5 * LOG2E
MLA_Q_SCALE = (MLA_NOPE + MLA_ROPE) ** -0.5 * LOG2E


def _params(semantics):
    return pltpu.CompilerParams(dimension_semantics=semantics, vmem_limit_bytes=VMEM_LIMIT)


def _resident(shape):
    return pl.BlockSpec(shape, lambda *_: (0,) * len(shape), pipeline_mode=pl.Buffered(1))


def _alibi_slopes(n_heads):
    return 2.0 ** (-8.0 * jnp.arange(1, n_heads + 1, dtype=F32) / n_heads)


def _rms(x, g):
    return x * lax.rsqrt(jnp.mean(x * x, axis=-1, keepdims=True) + RMS_EPS) * g


def _dot(a, b):
    return jnp.dot(a, b, preferred_element_type=F32)


def _dot_nt(a, b):
    return lax.dot_general(a, b, (((1,), (1,)), ((), ())), preferred_element_type=F32)


def _norm_proj_kernel(h_ref, g_ref, w_ref, o_ref, *, col_tile, scaled_cols, scale):
    n = _rms(h_ref[...], g_ref[...]).astype(BF16)
    for c in range(o_ref.shape[1] // col_tile):
        cols = slice(c * col_tile, (c + 1) * col_tile)
        y = _dot(n, w_ref[:, cols])
        if (c + 1) * col_tile <= scaled_cols:
            y = y * scale
        o_ref[:, cols] = y.astype(o_ref.dtype)


def _norm_proj(h, g, w, col_tile=512, scaled_cols=0, scale=1.0):
    m, d = h.shape
    n_out = w.shape[1]
    assert scaled_cols % col_tile == 0
    return pl.pallas_call(
        functools.partial(_norm_proj_kernel, col_tile=col_tile, scaled_cols=scaled_cols, scale=scale),
        grid=(m // ROW_TILE,),
        in_specs=[pl.BlockSpec((ROW_TILE, d), lambda i: (i, 0)),
                  _resident((1, d)),
                  _resident((d, n_out))],
        out_specs=pl.BlockSpec((ROW_TILE, n_out), lambda i: (i, 0)),
        out_shape=jax.ShapeDtypeStruct((m, n_out), BF16),
        compiler_params=_params(("parallel",)),
        name="norm_proj",
    )(h, g.reshape(1, d), w)


def _dil_proj_kernel(h_ref, g_ref, w_ref, o0_ref, o1_ref, o2_ref, stage_ref):
    n = _rms(h_ref[...], g_ref[...]).astype(BF16)
    rows = h_ref.shape[0]
    width = DIL_HEADS_PER_GROUP * HEAD_DIM
    for g, (o_ref, (_, dil)) in enumerate(zip((o0_ref, o1_ref, o2_ref), DIL_PAIRS)):
        for part in range(3):
            src = (3 * g + part) * width
            y = _dot(n, w_ref[:, src:src + width])
            if dil == 1:
                o_ref[:, part * width:(part + 1) * width] = y.astype(BF16)
                continue
            for t in range(width // LANES):
                stage_ref[g - 1, part, t] = y[:, t * LANES:(t + 1) * LANES]
            for r in range(dil):
                for t in range(width // LANES):
                    dst = (3 * r + part) * width + t * LANES
                    strided = stage_ref[g - 1, part, t, pl.ds(r, rows // dil, stride=dil), :]
                    o_ref[:, dst:dst + LANES] = strided.astype(BF16)


def _dil_proj(h, g, w):
    m, d = h.shape
    feat = 3 * DIL_HEADS_PER_GROUP * HEAD_DIM
    assert DIL_PAIRS[0][1] == 1
    view = lambda dil: (m // dil, dil * feat)
    view_block = lambda dil: pl.BlockSpec((ROW_TILE // dil, dil * feat), lambda i: (i, 0))
    return pl.pallas_call(
        _dil_proj_kernel,
        grid=(m // ROW_TILE,),
        in_specs=[pl.BlockSpec((ROW_TILE, d), lambda i: (i, 0)), _resident((1, d)), _resident(w.shape)],
        out_specs=[view_block(dil) for _, dil in DIL_PAIRS],
        out_shape=[jax.ShapeDtypeStruct(view(dil), BF16) for _, dil in DIL_PAIRS],
        scratch_shapes=[pltpu.VMEM((len(DIL_PAIRS) - 1, 3, feat // 3 // LANES, ROW_TILE, LANES), F32)],
        compiler_params=_params(("parallel",)),
        name="dil_proj",
    )(h, g.reshape(1, d), w)


def _mlp_tail(h1, g_ref, wup_ref, wdn_ref, gf_ref, out_ref):
    n = _rms(h1, g_ref[...]).astype(BF16)
    acc = jnp.zeros_like(h1)
    for c in range(wup_ref.shape[1] // FF_TILE):
        cols = slice(c * FF_TILE, (c + 1) * FF_TILE)
        u = jnp.square(jnp.maximum(_dot(n, wup_ref[:, cols]), 0.0)).astype(BF16)
        acc = acc + _dot(u, wdn_ref[cols, :])
    out = h1 + acc
    if gf_ref is not None:
        out = _rms(out, gf_ref[...])
    out_ref[...] = out


def _post_kernel(h_ref, a_ref, wo_ref, g_ref, wup_ref, wdn_ref, *rest):
    gf_ref, out_ref = (rest[0], rest[1]) if len(rest) == 2 else (None, rest[0])
    h1 = h_ref[...] + _dot(a_ref[...], wo_ref[...])
    _mlp_tail(h1, g_ref, wup_ref, wdn_ref, gf_ref, out_ref)


def _post_merge_kernel(h_ref, o0_ref, o1_ref, o2_ref, l0_ref, l1_ref, l2_ref,
                       wo_ref, g_ref, wup_ref, wdn_ref, out_ref, stage_ref):
    rows, width = o0_ref.shape

    def token_rows(x_ref, slot, dil):
        if dil == 1:
            return x_ref[...]
        for r in range(dil):
            for t in range(width // LANES):
                src = r * width + t * LANES
                stage_ref[slot, t, pl.ds(r, rows // dil, stride=dil), :] = x_ref[:, src:src + LANES]
        return jnp.concatenate([stage_ref[slot, t] for t in range(width // LANES)], axis=1)

    dils = [dil for _, dil in DIL_PAIRS]
    l0, l1, l2 = (token_rows(ref, slot, dil) for slot, (ref, dil) in enumerate(zip((l0_ref, l1_ref, l2_ref), dils)))
    o0, o1, o2 = (token_rows(ref, 3 + slot, dil) for slot, (ref, dil) in enumerate(zip((o0_ref, o1_ref, o2_ref), dils)))
    mx = jnp.maximum(jnp.maximum(l0, l1), l2)
    e0, e1, e2 = jnp.exp(l0 - mx), jnp.exp(l1 - mx), jnp.exp(l2 - mx)
    merged = (e0 * o0 + e1 * o1 + e2 * o2) / (e0 + e1 + e2)
    h1 = h_ref[...] + _dot(merged.astype(BF16), wo_ref[...])
    _mlp_tail(h1, g_ref, wup_ref, wdn_ref, None, out_ref)


def _post(h, attn_ins, w_o, g, w_up, w_down, g_final=None):
    m, d = h.shape
    row = lambda width: pl.BlockSpec((ROW_TILE, width), lambda i: (i, 0))
    merge = len(attn_ins) > 1
    in_specs = [row(d)] + [pl.BlockSpec((ROW_TILE * a.shape[0] // m, a.shape[1]), lambda i: (i, 0)) for a in attn_ins]
    in_specs += [_resident(w_o.shape), _resident((1, d)), _resident(w_up.shape), _resident(w_down.shape)]
    args = [h, *attn_ins, w_o, g.reshape(1, d), w_up, w_down]
    if g_final is not None:
        in_specs.append(_resident((1, d)))
        args.append(g_final.reshape(1, d))
    scratch = [pltpu.VMEM((len(attn_ins), w_o.shape[0] // LANES, ROW_TILE, LANES), F32)] if merge else []
    return pl.pallas_call(
        _post_merge_kernel if merge else _post_kernel,
        grid=(m // ROW_TILE,),
        in_specs=in_specs,
        out_specs=row(d),
        out_shape=jax.ShapeDtypeStruct((m, d), F32),
        scratch_shapes=scratch,
        compiler_params=_params(("parallel",)),
        name="post_mlp",
    )(*args)


ONES_ROWS = 16


class _SoftmaxState:
    def __init__(self, m_ref, acc_ref, dv):
        self.m_ref, self.acc_ref, self.dv = m_ref, acc_ref, dv

    @staticmethod
    def scratch(n_heads, dv, tq):
        return [pltpu.VMEM((n_heads, 1, tq), F32), pltpu.VMEM((n_heads, dv + ONES_ROWS, tq), F32)]

    @staticmethod
    def store_values_t(vt_ref, hh, cols, v_t):
        dv = v_t.shape[0]
        vt_ref[hh, :dv, cols] = v_t.astype(BF16)
        vt_ref[hh, dv:, cols] = jnp.ones((ONES_ROWS, v_t.shape[1]), BF16)

    def init(self):
        self.m_ref[...] = jnp.full(self.m_ref.shape, NEG_INF, F32)
        self.acc_ref[...] = jnp.zeros(self.acc_ref.shape, F32)

    def step(self, hh, x, v_t, shift=None):
        m = self.m_ref[hh]
        m_cur = jnp.max(x, axis=0, keepdims=True)
        if shift is not None:
            m_cur = m_cur - shift
        m_new = jnp.maximum(m, m_cur)
        alpha = jnp.exp2(m - m_new)
        p = jnp.exp2(x - (m_new if shift is None else m_new + shift))
        self.m_ref[hh] = m_new
        self.acc_ref[hh] = alpha * self.acc_ref[hh] + _dot(v_t, p.astype(BF16))

    def normalized(self, hh):
        return self.acc_ref[hh, :self.dv, :] / self.acc_ref[hh, self.dv:self.dv + 1, :]


def _sweep_blocks(n_past, n_heads, state, score_fn, values_fn, own_logits, past_logits, s_a, s_b):
    def produce(buf, block):
        for hh in range(n_heads):
            buf[hh] = score_fn(hh, block)

    def consume(buf, logits_fn, block):
        for hh in range(n_heads):
            x, shift = logits_fn(hh, buf[hh])
            state.step(hh, x, values_fn(hh, block), shift)

    produce(s_b, n_past)
    produce(s_a, 0)
    consume(s_b, own_logits, n_past)

    def two_blocks(first):
        produce(s_b, first + 1)
        consume(s_a, functools.partial(past_logits, block=first), first)
        produce(s_a, jnp.minimum(first + 2, n_past - 1))
        consume(s_b, functools.partial(past_logits, block=first + 1), first + 1)

    def quad(t, carry):
        two_blocks(4 * t)
        two_blocks(4 * t + 2)
        return carry

    def pair(t, carry):
        two_blocks(4 * (n_past // 4) + 2 * t)
        return carry

    lax.fori_loop(0, n_past // 4, quad, 0)
    lax.fori_loop(0, (n_past % 4) // 2, pair, 0)

    @pl.when(n_past % 2 == 1)
    def _():
        consume(s_a, functools.partial(past_logits, block=n_past - 1), n_past - 1)


def _transpose_bf16(x):
    return x.astype(F32).T.astype(BF16)


def _keep_head_rows(x_t, half):
    r = lax.broadcasted_iota(jnp.int32, x_t.shape, 0)
    keep = (r < HEAD_DIM) if half == 0 else (r >= HEAD_DIM)
    return jnp.where(keep, x_t, jnp.zeros_like(x_t))


def _moba_kernel(slopes_ref, q_ref, k_ref, v_ref, o_ref, kmean_ref, vt_ref, term_ref, alibi_ref, qh_ref,
                 s_a, s_b, m_ref, acc_ref, *, n_blocks):
    blk = MOBA_BLOCK
    n_heads = MOBA_HEADS_PER_STEP
    group = pl.program_id(1)
    i = pl.program_id(2)
    state = _SoftmaxState(m_ref, acc_ref, HEAD_DIM)

    @pl.when(i == 0)
    def _():
        for j in range(n_blocks):
            rows = slice(j * blk, (j + 1) * blk)
            kmean_ref[j:j + 1, :] = jnp.mean(k_ref[0, rows, :].astype(F32), axis=0, keepdims=True)
            v_t = v_ref[0, rows, :].astype(F32).T
            for hh in range(n_heads):
                state.store_values_t(vt_ref, hh, rows, v_t[hh * HEAD_DIM:(hh + 1) * HEAD_DIM, :])

    state.init()
    q_t = q_ref[0].astype(F32).T
    key = lax.broadcasted_iota(jnp.int32, (blk, blk), 0)
    qry = lax.broadcasted_iota(jnp.int32, (blk, blk), 1)
    rel = (qry - key).astype(F32)
    blk_idx = lax.broadcasted_iota(jnp.int32, (n_blocks, blk), 0)
    fully_past = blk_idx < i
    block_dist = ((i - blk_idx) * blk).astype(F32)
    lane_tile = lambda hh: slice((hh // 2) * LANES, (hh // 2 + 1) * LANES)

    for hh in range(n_heads):
        slope_s = slopes_ref[n_heads * group + hh] * LOG2E
        q_h = _keep_head_rows(q_t[lane_tile(hh), :], hh % 2)
        gate = lax.dot_general(kmean_ref[:, lane_tile(hh)], q_h, (((1,), (0,)), ((), ())),
                               precision=lax.Precision.HIGHEST, preferred_element_type=F32)
        work = jnp.where(fully_past, gate, NEG_INF * MOBA_Q_SCALE)
        chosen = jnp.zeros((n_blocks, blk), jnp.bool_)
        for _ in range(min(MOBA_TOPK, n_blocks)):
            best = jnp.max(work, axis=0, keepdims=True)
            first = jnp.min(jnp.where(work == best, blk_idx, n_blocks), axis=0, keepdims=True)
            pick = blk_idx == first
            chosen = jnp.logical_or(chosen, pick)
            work = jnp.where(pick, -jnp.inf, work)
        term_ref[hh] = jnp.where(jnp.logical_and(chosen, fully_past), slope_s * block_dist, -NEG_INF)
        qh_ref[hh] = q_h.astype(BF16)
        alibi_ref[hh] = slope_s * rel

    def rows_of(block):
        return pl.ds(pl.multiple_of(block * blk, blk), blk)

    def scores(hh, block):
        return _dot(k_ref[0, rows_of(block), lane_tile(hh)], qh_ref[hh])

    def v_t(hh, block):
        return vt_ref[hh, :, rows_of(block)]

    def own_logits(hh, s):
        return jnp.where(rel >= 0, s - alibi_ref[hh], NEG_INF), None

    def past_logits(hh, s, block):
        return s - alibi_ref[hh], term_ref[hh, pl.ds(block, 1), :]

    _sweep_blocks(i, n_heads, state, scores, v_t, own_logits, past_logits, s_a, s_b)
    o_t = jnp.concatenate([state.normalized(hh) for hh in range(n_heads)], axis=0)
    o_ref[0] = o_t.T.astype(o_ref.dtype)


def _moba_attention(qkv, b, s):
    n_blocks = s // MOBA_BLOCK
    n_heads = MOBA_HEADS_PER_STEP
    width = n_heads * HEAD_DIM
    n_groups = MOBA_HEADS // n_heads
    score_buf = pltpu.VMEM((n_heads, MOBA_BLOCK, MOBA_BLOCK), F32)
    return pl.pallas_call(
        functools.partial(_moba_kernel, n_blocks=n_blocks),
        grid=(b, n_groups, n_blocks),
        in_specs=[pl.BlockSpec(memory_space=pltpu.SMEM),
                  pl.BlockSpec((1, MOBA_BLOCK, width), lambda bi, g, i: (bi, i, g)),
                  pl.BlockSpec((1, s, width), lambda bi, g, i: (bi, 0, n_groups + g)),
                  pl.BlockSpec((1, s, width), lambda bi, g, i: (bi, 0, 2 * n_groups + g))],
        out_specs=pl.BlockSpec((1, MOBA_BLOCK, width), lambda bi, g, i: (bi, i, g)),
        out_shape=jax.ShapeDtypeStruct((b, s, MOBA_HEADS * HEAD_DIM), BF16),
        scratch_shapes=[pltpu.VMEM((n_blocks, width), F32),
                        pltpu.VMEM((n_heads, HEAD_DIM + ONES_ROWS, s), BF16),
                        pltpu.VMEM((n_heads, n_blocks, MOBA_BLOCK), F32),
                        pltpu.VMEM((n_heads, MOBA_BLOCK, MOBA_BLOCK), F32),
                        pltpu.VMEM((n_heads, LANES, MOBA_BLOCK), BF16),
                        score_buf, score_buf,
                        *_SoftmaxState.scratch(n_heads, HEAD_DIM, MOBA_BLOCK)],
        compiler_params=_params(("parallel", "parallel", "arbitrary")),
        name="moba_attention",
    )(_alibi_slopes(MOBA_HEADS), qkv, qkv, qkv)


def _lane_half(x, half):
    lane = lax.broadcasted_iota(jnp.int32, x.shape, x.ndim - 1)
    keep = (lane < HEAD_DIM) if half == 0 else (lane >= HEAD_DIM)
    return jnp.where(keep, x, jnp.zeros_like(x))


def _band_attention(n_heads, q_tile, k_tile, v_tile, head_softmax, store_pair, s_ref):
    for h in range(n_heads):
        s_ref[h] = _dot_nt(_lane_half(q_tile(h), h % 2), k_tile(h))
    even = None
    for h in range(n_heads):
        p, den, extra = head_softmax(h, s_ref[h])
        o = _dot(p.astype(BF16), v_tile(h)) / den
        if h % 2 == 0:
            even = (o, extra)
            continue
        lane = lax.broadcasted_iota(jnp.int32, o.shape, 1)
        store_pair(h // 2, lane < HEAD_DIM, even, (o, extra))


def _band_geometry(band, n):
    row = lax.broadcasted_iota(jnp.int32, (band, 2 * band), 0)
    col = lax.broadcasted_iota(jnp.int32, (band, 2 * band), 1)
    diff = row + band - col
    in_sequence = (n > 0) | (col >= band)
    return diff, in_sequence


def _dil_kernel(slopes_ref, q_ref, kp_ref, kc_ref, vp_ref, vc_ref, o_ref, lse_ref, s_ref, *, n_pts, dil, group):
    n = pl.program_id(2)
    scale = HEAD_DIM ** -0.5
    diff, in_sequence = _band_geometry(DIL_BAND, n)
    mask = (diff >= 0) & (diff <= n_pts) & in_sequence
    dist = (dil * diff).astype(F32)
    tile = lambda h: slice((h // 2) * LANES, (h // 2 + 1) * LANES)

    def head_softmax(h, s):
        slope = slopes_ref[group * DIL_HEADS_PER_GROUP + h]
        logits = jnp.where(mask, s * scale - slope * dist, NEG_INF)
        m = jnp.max(logits, axis=1, keepdims=True)
        e = jnp.exp(logits - m)
        den = jnp.sum(e, axis=1, keepdims=True)
        return e, den, m + jnp.log(den)

    def store_pair(pair, first_half, even, odd):
        cols = slice(pair * LANES, (pair + 1) * LANES)
        o_ref[0, :, cols] = jnp.where(first_half, even[0], odd[0])
        lse_ref[0, :, cols] = jnp.where(first_half, even[1], odd[1])

    _band_attention(
        DIL_HEADS_PER_GROUP,
        lambda h: q_ref[0, :, tile(h)],
        lambda h: jnp.concatenate([kp_ref[0, :, tile(h)], kc_ref[0, :, tile(h)]], axis=0),
        lambda h: jnp.concatenate([vp_ref[0, :, tile(h)], vc_ref[0, :, tile(h)]], axis=0),
        head_softmax, store_pair, s_ref)


def _dilated_group(qkv, b, s, group, window, dil):
    width = DIL_HEADS_PER_GROUP * HEAD_DIM
    n_sub = s // dil
    assert n_sub % DIL_BAND == 0
    view = qkv.reshape(b, n_sub, dil * 3 * width)

    def col(part):
        return lambda bi, r, n: (bi, n, r * 3 + part)

    def col_prev(part):
        return lambda bi, r, n: (bi, jnp.maximum(n - 1, 0), r * 3 + part)

    blk = (1, DIL_BAND, width)
    out_spec = pl.BlockSpec(blk, lambda bi, r, n: (bi, n, r))
    out_sds = jax.ShapeDtypeStruct((b, n_sub, dil * width), F32)
    o, lse = pl.pallas_call(
        functools.partial(_dil_kernel, n_pts=window // dil, dil=dil, group=group),
        grid=(b, dil, n_sub // DIL_BAND),
        in_specs=[pl.BlockSpec(memory_space=pltpu.SMEM),
                  pl.BlockSpec(blk, col(0)),
                  pl.BlockSpec(blk, col_prev(1)), pl.BlockSpec(blk, col(1)),
                  pl.BlockSpec(blk, col_prev(2)), pl.BlockSpec(blk, col(2))],
        out_specs=[out_spec, out_spec],
        out_shape=[out_sds, out_sds],
        scratch_shapes=[pltpu.VMEM((DIL_HEADS_PER_GROUP, DIL_BAND, 2 * DIL_BAND), F32)],
        compiler_params=_params(("parallel", "parallel", "arbitrary")),
        name=f"dilated_attention_g{group}",
    )(_alibi_slopes(len(DIL_PAIRS) * DIL_HEADS_PER_GROUP), view, view, view, view, view)
    return o.reshape(b * n_sub, dil * width), lse.reshape(b * n_sub, dil * width)


def _mla_proj_kernel(h_ref, g_ref, wd_ref, gq_ref, gkv_ref, wuq_ref, wuk_ref, wuv_ref,
                     cos_ref, sin_up_ref, sin_dn_ref, q_ref, k_ref, v_ref):
    n = _rms(h_ref[...], g_ref[...]).astype(BF16)
    c = _dot(n, wd_ref[...])
    nq = _rms(c[:, :MLA_Q_RANK], gq_ref[...]).astype(BF16)
    nkv = _rms(c[:, MLA_Q_RANK:MLA_Q_RANK + MLA_KV_RANK], gkv_ref[...]).astype(BF16)
    k_rope = c[:, MLA_Q_RANK + MLA_KV_RANK:]
    cos, sin_up, sin_dn = cos_ref[...], sin_up_ref[...], sin_dn_ref[...]
    half = MLA_ROPE // 2

    def rope(x):
        return x * cos + pltpu.roll(x, half, 1) * sin_up + pltpu.roll(x, LANES - half, 1) * sin_dn

    kr = rope(k_rope)
    for h in range(MLA_HEADS):
        cols = slice(h * LANES, (h + 1) * LANES)
        q_ref[:, cols] = (rope(_dot(nq, wuq_ref[:, cols])) * MLA_Q_SCALE).astype(BF16)
        k_ref[:, cols] = (_dot(nkv, wuk_ref[:, cols]) + kr).astype(BF16)
    v_ref[...] = _dot(nkv, wuv_ref[...]).astype(BF16)


def _mla_proj(h, g, w_dkv, q_norm, w_uq, kv_norm, w_ukv, s):
    m, d = h.shape
    qk = MLA_NOPE + MLA_ROPE
    wd = jnp.concatenate([w_dkv[:, :MLA_Q_RANK + MLA_KV_RANK],
                          jnp.zeros((d, MLA_NOPE), F32), w_dkv[:, MLA_Q_RANK + MLA_KV_RANK:],
                          jnp.zeros((d, LANES - qk), F32)], axis=1).astype(BF16)
    wuq = jnp.pad(w_uq.reshape(MLA_Q_RANK, MLA_HEADS, qk), ((0, 0), (0, 0), (0, LANES - qk)))
    wuq = wuq.reshape(MLA_Q_RANK, MLA_HEADS * LANES).astype(BF16)
    w_ukv = w_ukv.reshape(MLA_KV_RANK, MLA_HEADS, MLA_NOPE + MLA_V)
    wuk = jnp.pad(w_ukv[:, :, :MLA_NOPE], ((0, 0), (0, 0), (0, LANES - MLA_NOPE)))
    wuk = wuk.reshape(MLA_KV_RANK, MLA_HEADS * LANES).astype(BF16)
    wuv = w_ukv[:, :, MLA_NOPE:].reshape(MLA_KV_RANK, MLA_HEADS * MLA_V).astype(BF16)
    half = MLA_ROPE // 2
    inv = ROPE_THETA ** (-jnp.arange(0, MLA_ROPE, 2, dtype=F32) / MLA_ROPE)
    ang = jnp.arange(s).astype(F32)[:, None] * inv[None, :]
    cos, sin = jnp.cos(ang), jnp.sin(ang)
    zeros = lambda w: jnp.zeros((s, w), F32)
    cos_t = jnp.concatenate([jnp.ones((s, MLA_NOPE), F32), cos, cos, zeros(LANES - qk)], axis=1)
    sin_up = jnp.concatenate([zeros(MLA_NOPE + half), sin, zeros(LANES - qk)], axis=1)
    sin_dn = jnp.concatenate([zeros(MLA_NOPE), -sin, zeros(half + LANES - qk)], axis=1)

    tiles_per_seq = s // ROW_TILE
    row = lambda width: pl.BlockSpec((ROW_TILE, width), lambda i: (i, 0))
    table = pl.BlockSpec((ROW_TILE, LANES), lambda i: (i % tiles_per_seq, 0))
    return pl.pallas_call(
        _mla_proj_kernel,
        grid=(m // ROW_TILE,),
        in_specs=[row(d), _resident((1, d)), _resident(wd.shape),
                  _resident((1, MLA_Q_RANK)), _resident((1, MLA_KV_RANK)),
                  _resident(wuq.shape), _resident(wuk.shape), _resident(wuv.shape),
                  table, table, table],
        out_specs=[row(MLA_HEADS * LANES), row(MLA_HEADS * LANES), row(MLA_HEADS * MLA_V)],
        out_shape=[jax.ShapeDtypeStruct((m, MLA_HEADS * LANES), BF16),
                   jax.ShapeDtypeStruct((m, MLA_HEADS * LANES), BF16),
                   jax.ShapeDtypeStruct((m, MLA_HEADS * MLA_V), BF16)],
        compiler_params=_params(("parallel",)),
        name="mla_proj",
    )(h, g.reshape(1, d), wd, q_norm.reshape(1, -1), kv_norm.reshape(1, -1), wuq, wuk, wuv,
      cos_t, sin_up, sin_dn)


def _mla_kernel(q_ref, k_ref, v_ref, o_ref, vt_ref, qh_ref, s_a, s_b, m_ref, acc_ref, *, n_tiles):
    t = MLA_TILE
    n_heads = MLA_HEADS_PER_STEP
    i = pl.program_id(2)
    state = _SoftmaxState(m_ref, acc_ref, MLA_V)

    @pl.when(i == 0)
    def _():
        for j in range(n_tiles):
            rows = slice(j * t, (j + 1) * t)
            v_t = v_ref[0, rows, :].astype(F32).T
            for hh in range(n_heads):
                state.store_values_t(vt_ref, hh, rows, v_t[hh * MLA_V:(hh + 1) * MLA_V, :])

    state.init()
    key = lax.broadcasted_iota(jnp.int32, (t, t), 0)
    qry = lax.broadcasted_iota(jnp.int32, (t, t), 1)
    lane_tile = lambda hh: slice(hh * LANES, (hh + 1) * LANES)
    for hh in range(n_heads):
        qh_ref[hh] = _transpose_bf16(q_ref[0, :, lane_tile(hh)])

    def rows_of(block):
        return pl.ds(pl.multiple_of(block * t, t), t)

    def scores(hh, block):
        return _dot(k_ref[0, rows_of(block), lane_tile(hh)], qh_ref[hh])

    def v_t(hh, block):
        return vt_ref[hh, :, rows_of(block)]

    def own_logits(hh, s):
        return jnp.where(key <= qry, s, NEG_INF), None

    def past_logits(hh, s, block):
        return s, None

    _sweep_blocks(i, n_heads, state, scores, v_t, own_logits, past_logits, s_a, s_b)
    o_t = jnp.concatenate([state.normalized(hh) for hh in range(n_heads)], axis=0)
    o_ref[0] = o_t.T.astype(o_ref.dtype)


def _mla_attention(q, k, v, b, s):
    n_heads = MLA_HEADS_PER_STEP
    n_groups = MLA_HEADS // n_heads
    whole_seq = lambda bi, g, i: (bi, 0, g)
    tile = lambda bi, g, i: (bi, i, g)
    score_buf = pltpu.VMEM((n_heads, MLA_TILE, MLA_TILE), F32)
    return pl.pallas_call(
        functools.partial(_mla_kernel, n_tiles=s // MLA_TILE),
        grid=(b, n_groups, s // MLA_TILE),
        in_specs=[pl.BlockSpec((1, MLA_TILE, n_heads * LANES), tile),
                  pl.BlockSpec((1, s, n_heads * LANES), whole_seq),
                  pl.BlockSpec((1, s, n_heads * MLA_V), whole_seq)],
        out_specs=pl.BlockSpec((1, MLA_TILE, n_heads * MLA_V), tile),
        out_shape=jax.ShapeDtypeStruct((b, s, MLA_HEADS * MLA_V), BF16),
        scratch_shapes=[pltpu.VMEM((n_heads, MLA_V + ONES_ROWS, s), BF16),
                        pltpu.VMEM((n_heads, LANES, MLA_TILE), BF16),
                        score_buf, score_buf,
                        *_SoftmaxState.scratch(n_heads, MLA_V, MLA_TILE)],
        compiler_params=_params(("parallel", "parallel", "arbitrary")),
        name="mla_attention",
    )(q, k, v)


def _swa_kernel(slopes_ref, sinks_ref, q_ref, kp_ref, kc_ref, vp_ref, vc_ref, o_ref, s_ref):
    n = pl.program_id(1)
    scale = HEAD_DIM ** -0.5
    group = SWA_Q_HEADS // SWA_KV_HEADS
    diff, in_sequence = _band_geometry(SWA_BLOCK, n)
    mask = (diff >= 0) & (diff < SWA_WINDOW) & in_sequence
    dist = diff.astype(F32)

    def both_halves(prev_ref, cur_ref, kv_head):
        x = jnp.concatenate([prev_ref[0], cur_ref[0]], axis=0)
        part = x[:, kv_head * HEAD_DIM:(kv_head + 1) * HEAD_DIM]
        return jnp.concatenate([part, part], axis=1)

    k_tiles = [both_halves(kp_ref, kc_ref, kv) for kv in range(SWA_KV_HEADS)]
    v_tiles = [both_halves(vp_ref, vc_ref, kv) for kv in range(SWA_KV_HEADS)]

    def head_softmax(h, s):
        logits = jnp.where(mask, s * scale - slopes_ref[h] * dist, NEG_INF)
        m = jnp.maximum(jnp.max(logits, axis=1, keepdims=True), sinks_ref[h])
        e = jnp.exp(logits - m)
        return e, jnp.sum(e, axis=1, keepdims=True) + jnp.exp(sinks_ref[h] - m), None

    def store_pair(pair, first_half, even, odd):
        o_ref[0, :, pair * LANES:(pair + 1) * LANES] = jnp.where(first_half, even[0], odd[0]).astype(o_ref.dtype)

    _band_attention(
        SWA_Q_HEADS,
        lambda h: q_ref[0, :, (h // 2) * LANES:(h // 2 + 1) * LANES],
        lambda h: k_tiles[h // group],
        lambda h: v_tiles[h // group],
        head_softmax, store_pair, s_ref)


def _swa_attention(qkv, sinks, b, s):
    q_width = SWA_Q_HEADS * HEAD_DIM
    kv_width = SWA_KV_HEADS * HEAD_DIM
    k_tile = q_width // kv_width
    cur = lambda t: (lambda bi, n: (bi, n, t))
    prev = lambda t: (lambda bi, n: (bi, jnp.maximum(n - 1, 0), t))
    kv_blk = (1, SWA_BLOCK, kv_width)
    return pl.pallas_call(
        _swa_kernel,
        grid=(b, s // SWA_BLOCK),
        in_specs=[pl.BlockSpec(memory_space=pltpu.SMEM), pl.BlockSpec(memory_space=pltpu.SMEM),
                  pl.BlockSpec((1, SWA_BLOCK, q_width), cur(0)),
                  pl.BlockSpec(kv_blk, prev(k_tile)), pl.BlockSpec(kv_blk, cur(k_tile)),
                  pl.BlockSpec(kv_blk, prev(k_tile + 1)), pl.BlockSpec(kv_blk, cur(k_tile + 1))],
        out_specs=pl.BlockSpec((1, SWA_BLOCK, q_width), cur(0)),
        out_shape=jax.ShapeDtypeStruct((b, s, q_width), BF16),
        scratch_shapes=[pltpu.VMEM((SWA_Q_HEADS, SWA_BLOCK, 2 * SWA_BLOCK), F32)],
        compiler_params=_params(("parallel", "arbitrary")),
        name="swa_attention",
    )(_alibi_slopes(SWA_Q_HEADS), sinks.astype(F32), qkv, qkv, qkv, qkv, qkv)


def kernel(x, l0_attn_norm, l0_w_qkv, l0_w_o, l0_mlp_norm, l0_w_up, l0_w_down, l1_attn_norm, l1_w_qkv, l1_w_o, l1_mlp_norm, l1_w_up, l1_w_down, l2_attn_norm, l2_w_dkv, l2_q_norm, l2_w_uq, l2_kv_norm, l2_w_ukv, l2_w_o, l2_mlp_norm, l2_w_up, l2_w_down, l3_attn_norm, l3_w_qkv, l3_sinks, l3_w_o, l3_mlp_norm, l3_w_up, l3_w_down, final_norm):
    b, s, d = x.shape
    bf = lambda w: w.astype(BF16)
    h = x.reshape(b * s, d)

    qkv = _norm_proj(h, l0_attn_norm, bf(l0_w_qkv), scaled_cols=MOBA_HEADS * HEAD_DIM, scale=MOBA_Q_SCALE)
    a = _moba_attention(qkv.reshape(b, s, -1), b, s).reshape(b * s, -1)
    h = _post(h, (a,), bf(l0_w_o), l0_mlp_norm, bf(l0_w_up), bf(l0_w_down))

    qkvs = _dil_proj(h, l1_attn_norm, bf(l1_w_qkv))
    groups = [_dilated_group(qkvs[g], b, s, g, window, dil) for g, (window, dil) in enumerate(DIL_PAIRS)]
    h = _post(h, tuple(o for o, _ in groups) + tuple(l for _, l in groups),
              bf(l1_w_o), l1_mlp_norm, bf(l1_w_up), bf(l1_w_down))

    q, k, v = _mla_proj(h, l2_attn_norm, l2_w_dkv, l2_q_norm, l2_w_uq, l2_kv_norm, l2_w_ukv, s)
    a = _mla_attention(q.reshape(b, s, -1), k.reshape(b, s, -1), v.reshape(b, s, -1), b, s).reshape(b * s, -1)
    h = _post(h, (a,), bf(l2_w_o), l2_mlp_norm, bf(l2_w_up), bf(l2_w_down))

    qkv = _norm_proj(h, l3_attn_norm, bf(l3_w_qkv), col_tile=256)
    a = _swa_attention(qkv.reshape(b, s, -1), l3_sinks, b, s).reshape(b * s, -1)
    h = _post(h, (a,), bf(l3_w_o), l3_mlp_norm, bf(l3_w_up), bf(l3_w_down), g_final=final_norm)
    return h.reshape(b, s, d)
```

```python
import functools

import jax
import jax.numpy as jnp
from jax import lax
from jax.experimental import pallas as pl
from jax.experimental.pallas import tpu as pltpu

F32 = jnp.float32
BF16 = jnp.bfloat16

D_MODEL = 1024
HEAD_DIM = 64
RMS_EPS = 1e-6
D_FF = 4 * D_MODEL
NEG_INF = -1e30

MOBA_HEADS = 16
MOBA_BLOCK = 256
MOBA_TOPK = 3

DIL_PAIRS = ((128, 1), (512, 4), (2048, 16))
DIL_HEADS_PER_GROUP = 8
DIL_BAND = 128

MLA_HEADS = 16
MLA_Q_RANK = 768
MLA_KV_RANK = 256
MLA_NOPE = 64
MLA_ROPE = 32
MLA_V = 64
ROPE_THETA = 10000.0
MLA_TILE = 256
MLA_HEADS_PER_STEP = 4
MOBA_HEADS_PER_STEP = 4

SWA_Q_HEADS = 16
SWA_KV_HEADS = 2
SWA_WINDOW = 128
SWA_BLOCK = 128

LANES = 128
ROW_TILE = 512
FF_TILE = 512
VMEM_LIMIT = 56 * 1024 * 1024
LOG2E = 1.4426950408889634
MOBA_Q_SCALE = HEAD_DIM ** -0.5 * LOG2E
MLA_Q_SCALE = (MLA_NOPE + MLA_ROPE) ** -0.5 * LOG2E


def _params(semantics):
    return pltpu.CompilerParams(dimension_semantics=semantics, vmem_limit_bytes=VMEM_LIMIT)


def _resident(shape):
    return pl.BlockSpec(shape, lambda *_: (0,) * len(shape), pipeline_mode=pl.Buffered(1))


def _alibi_slopes(n_heads):
    return 2.0 ** (-8.0 * jnp.arange(1, n_heads + 1, dtype=F32) / n_heads)


def _rms(x, g):
    return x * lax.rsqrt(jnp.mean(x * x, axis=-1, keepdims=True) + RMS_EPS) * g


def _dot(a, b):
    return jnp.dot(a, b, preferred_element_type=F32)


def _dot_nt(a, b):
    return lax.dot_general(a, b, (((1,), (1,)), ((), ())), preferred_element_type=F32)


def _norm_proj_kernel(h_ref, g_ref, w_ref, o_ref, *, col_tile, scaled_cols, scale):
    n = _rms(h_ref[...], g_ref[...]).astype(BF16)
    for c in range(o_ref.shape[1] // col_tile):
        cols = slice(c * col_tile, (c + 1) * col_tile)
        y = _dot(n, w_ref[:, cols])
        if (c + 1) * col_tile <= scaled_cols:
            y = y * scale
        o_ref[:, cols] = y.astype(o_ref.dtype)


def _norm_proj(h, g, w, col_tile=512, scaled_cols=0, scale=1.0):
    m, d = h.shape
    n_out = w.shape[1]
    assert scaled_cols % col_tile == 0
    return pl.pallas_call(
        functools.partial(_norm_proj_kernel, col_tile=col_tile, scaled_cols=scaled_cols, scale=scale),
        grid=(m // ROW_TILE,),
        in_specs=[pl.BlockSpec((ROW_TILE, d), lambda i: (i, 0)),
                  _resident((1, d)),
                  _resident((d, n_out))],
        out_specs=pl.BlockSpec((ROW_TILE, n_out), lambda i: (i, 0)),
        out_shape=jax.ShapeDtypeStruct((m, n_out), BF16),
        compiler_params=_params(("parallel",)),
        name="norm_proj",
    )(h, g.reshape(1, d), w)


def _dil_proj_kernel(h_ref, g_ref, w_ref, o0_ref, o1_ref, o2_ref, stage_ref):
    n = _rms(h_ref[...], g_ref[...]).astype(BF16)
    rows = h_ref.shape[0]
    width = DIL_HEADS_PER_GROUP * HEAD_DIM
    for g, (o_ref, (_, dil)) in enumerate(zip((o0_ref, o1_ref, o2_ref), DIL_PAIRS)):
        for part in range(3):
            src = (3 * g + part) * width
            y = _dot(n, w_ref[:, src:src + width])
            if dil == 1:
                o_ref[:, part * width:(part + 1) * width] = y.astype(BF16)
                continue
            for t in range(width // LANES):
                stage_ref[g - 1, part, t] = y[:, t * LANES:(t + 1) * LANES]
            for r in range(dil):
                for t in range(width // LANES):
                    dst = (3 * r + part) * width + t * LANES
                    strided = stage_ref[g - 1, part, t, pl.ds(r, rows // dil, stride=dil), :]
                    o_ref[:, dst:dst + LANES] = strided.astype(BF16)


def _dil_proj(h, g, w):
    m, d = h.shape
    feat = 3 * DIL_HEADS_PER_GROUP * HEAD_DIM
    assert DIL_PAIRS[0][1] == 1
    view = lambda dil: (m // dil, dil * feat)
    view_block = lambda dil: pl.BlockSpec((ROW_TILE // dil, dil * feat), lambda i: (i, 0))
    return pl.pallas_call(
        _dil_proj_kernel,
        grid=(m // ROW_TILE,),
        in_specs=[pl.BlockSpec((ROW_TILE, d), lambda i: (i, 0)), _resident((1, d)), _resident(w.shape)],
        out_specs=[view_block(dil) for _, dil in DIL_PAIRS],
        out_shape=[jax.ShapeDtypeStruct(view(dil), BF16) for _, dil in DIL_PAIRS],
        scratch_shapes=[pltpu.VMEM((len(DIL_PAIRS) - 1, 3, feat // 3 // LANES, ROW_TILE, LANES), F32)],
        compiler_params=_params(("parallel",)),
        name="dil_proj",
    )(h, g.reshape(1, d), w)


def _mlp_tail(h1, g_ref, wup_ref, wdn_ref, gf_ref, out_ref):
    n = _rms(h1, g_ref[...]).astype(BF16)
    acc = jnp.zeros_like(h1)
    for c in range(wup_ref.shape[1] // FF_TILE):
        cols = slice(c * FF_TILE, (c + 1) * FF_TILE)
        u = jnp.square(jnp.maximum(_dot(n, wup_ref[:, cols]), 0.0)).astype(BF16)
        acc = acc + _dot(u, wdn_ref[cols, :])
    out = h1 + acc
    if gf_ref is not None:
        out = _rms(out, gf_ref[...])
    out_ref[...] = out


def _post_kernel(h_ref, a_ref, wo_ref, g_ref, wup_ref, wdn_ref, *rest):
    gf_ref, out_ref = (rest[0], rest[1]) if len(rest) == 2 else (None, rest[0])
    h1 = h_ref[...] + _dot(a_ref[...], wo_ref[...])
    _mlp_tail(h1, g_ref, wup_ref, wdn_ref, gf_ref, out_ref)


def _post_merge_kernel(h_ref, o0_ref, o1_ref, o2_ref, l0_ref, l1_ref, l2_ref,
                       wo_ref, g_ref, wup_ref, wdn_ref, out_ref, stage_ref):
    rows, width = o0_ref.shape

    def token_rows(x_ref, slot, dil):
        if dil == 1:
            return x_ref[...]
        for r in range(dil):
            for t in range(width // LANES):
                src = r * width + t * LANES
                stage_ref[slot, t, pl.ds(r, rows // dil, stride=dil), :] = x_ref[:, src:src + LANES]
        return jnp.concatenate([stage_ref[slot, t] for t in range(width // LANES)], axis=1)

    dils = [dil for _, dil in DIL_PAIRS]
    l0, l1, l2 = (token_rows(ref, slot, dil) for slot, (ref, dil) in enumerate(zip((l0_ref, l1_ref, l2_ref), dils)))
    o0, o1, o2 = (token_rows(ref, 3 + slot, dil) for slot, (ref, dil) in enumerate(zip((o0_ref, o1_ref, o2_ref), dils)))
    mx = jnp.maximum(jnp.maximum(l0, l1), l2)
    e0, e1, e2 = jnp.exp(l0 - mx), jnp.exp(l1 - mx), jnp.exp(l2 - mx)
    merged = (e0 * o0 + e1 * o1 + e2 * o2) / (e0 + e1 + e2)
    h1 = h_ref[...] + _dot(merged.astype(BF16), wo_ref[...])
    _mlp_tail(h1, g_ref, wup_ref, wdn_ref, None, out_ref)


def _post(h, attn_ins, w_o, g, w_up, w_down, g_final=None):
    m, d = h.shape
    row = lambda width: pl.BlockSpec((ROW_TILE, width), lambda i: (i, 0))
    merge = len(attn_ins) > 1
    in_specs = [row(d)] + [pl.BlockSpec((ROW_TILE * a.shape[0] // m, a.shape[1]), lambda i: (i, 0)) for a in attn_ins]
    in_specs += [_resident(w_o.shape), _resident((1, d)), _resident(w_up.shape), _resident(w_down.shape)]
    args = [h, *attn_ins, w_o, g.reshape(1, d), w_up, w_down]
    if g_final is not None:
        in_specs.append(_resident((1, d)))
        args.append(g_final.reshape(1, d))
    scratch = [pltpu.VMEM((len(attn_ins), w_o.shape[0] // LANES, ROW_TILE, LANES), F32)] if merge else []
    return pl.pallas_call(
        _post_merge_kernel if merge else _post_kernel,
        grid=(m // ROW_TILE,),
        in_specs=in_specs,
        out_specs=row(d),
        out_shape=jax.ShapeDtypeStruct((m, d), F32),
        scratch_shapes=scratch,
        compiler_params=_params(("parallel",)),
        name="post_mlp",
    )(*args)


ONES_ROWS = 16


class _SoftmaxState:
    def __init__(self, m_ref, acc_ref, dv):
        self.m_ref, self.acc_ref, self.dv = m_ref, acc_ref, dv

    @staticmethod
    def scratch(n_heads, dv, tq):
        return [pltpu.VMEM((n_heads, 1, tq), F32), pltpu.VMEM((n_heads, dv + ONES_ROWS, tq), F32)]

    @staticmethod
    def store_values_t(vt_ref, hh, cols, v_t):
        dv = v_t.shape[0]
        vt_ref[hh, :dv, cols] = v_t.astype(BF16)
        vt_ref[hh, dv:, cols] = jnp.ones((ONES_ROWS, v_t.shape[1]), BF16)

    def init(self):
        self.m_ref[...] = jnp.full(self.m_ref.shape, NEG_INF, F32)
        self.acc_ref[...] = jnp.zeros(self.acc_ref.shape, F32)

    def step(self, hh, x, v_t, shift=None):
        m = self.m_ref[hh]
        m_cur = jnp.max(x, axis=0, keepdims=True)
        if shift is not None:
            m_cur = m_cur - shift
        m_new = jnp.maximum(m, m_cur)
        alpha = jnp.exp2(m - m_new)
        p = jnp.exp2(x - (m_new if shift is None else m_new + shift))
        self.m_ref[hh] = m_new
        self.acc_ref[hh] = alpha * self.acc_ref[hh] + _dot(v_t, p.astype(BF16))

    def normalized(self, hh):
        return self.acc_ref[hh, :self.dv, :] / self.acc_ref[hh, self.dv:self.dv + 1, :]


def _sweep_blocks(n_past, n_heads, state, score_fn, values_fn, own_logits, past_logits, s_a, s_b):
    def produce(buf, block):
        for hh in range(n_heads):
            buf[hh] = score_fn(hh, block)

    def consume(buf, logits_fn, block):
        for hh in range(n_heads):
            x, shift = logits_fn(hh, buf[hh])
            state.step(hh, x, values_fn(hh, block), shift)

    produce(s_b, n_past)
    produce(s_a, 0)
    consume(s_b, own_logits, n_past)

    def two_blocks(first):
        produce(s_b, first + 1)
        consume(s_a, functools.partial(past_logits, block=first), first)
        produce(s_a, jnp.minimum(first + 2, n_past - 1))
        consume(s_b, functools.partial(past_logits, block=first + 1), first + 1)

    def quad(t, carry):
        two_blocks(4 * t)
        two_blocks(4 * t + 2)
        return carry

    def pair(t, carry):
        two_blocks(4 * (n_past // 4) + 2 * t)
        return carry

    lax.fori_loop(0, n_past // 4, quad, 0)
    lax.fori_loop(0, (n_past % 4) // 2, pair, 0)

    @pl.when(n_past % 2 == 1)
    def _():
        consume(s_a, functools.partial(past_logits, block=n_past - 1), n_past - 1)


def _transpose_bf16(x):
    return x.astype(F32).T.astype(BF16)


def _keep_head_rows(x_t, half):
    r = lax.broadcasted_iota(jnp.int32, x_t.shape, 0)
    keep = (r < HEAD_DIM) if half == 0 else (r >= HEAD_DIM)
    return jnp.where(keep, x_t, jnp.zeros_like(x_t))


def _moba_kernel(slopes_ref, q_ref, k_ref, v_ref, o_ref, kmean_ref, vt_ref, term_ref, alibi_ref, qh_ref,
                 s_a, s_b, m_ref, acc_ref, *, n_blocks):
    blk = MOBA_BLOCK
    n_heads = MOBA_HEADS_PER_STEP
    group = pl.program_id(1)
    i = pl.program_id(2)
    state = _SoftmaxState(m_ref, acc_ref, HEAD_DIM)

    @pl.when(i == 0)
    def _():
        for j in range(n_blocks):
            rows = slice(j * blk, (j + 1) * blk)
            kmean_ref[j:j + 1, :] = jnp.mean(k_ref[0, rows, :].astype(F32), axis=0, keepdims=True)
            v_t = v_ref[0, rows, :].astype(F32).T
            for hh in range(n_heads):
                state.store_values_t(vt_ref, hh, rows, v_t[hh * HEAD_DIM:(hh + 1) * HEAD_DIM, :])

    state.init()
    q_t = q_ref[0].astype(F32).T
    key = lax.broadcasted_iota(jnp.int32, (blk, blk), 0)
    qry = lax.broadcasted_iota(jnp.int32, (blk, blk), 1)
    rel = (qry - key).astype(F32)
    blk_idx = lax.broadcasted_iota(jnp.int32, (n_blocks, blk), 0)
    fully_past = blk_idx < i
    block_dist = ((i - blk_idx) * blk).astype(F32)
    lane_tile = lambda hh: slice((hh // 2) * LANES, (hh // 2 + 1) * LANES)

    for hh in range(n_heads):
        slope_s = slopes_ref[n_heads * group + hh] * LOG2E
        q_h = _keep_head_rows(q_t[lane_tile(hh), :], hh % 2)
        gate = lax.dot_general(kmean_ref[:, lane_tile(hh)], q_h, (((1,), (0,)), ((), ())),
                               precision=lax.Precision.HIGHEST, preferred_element_type=F32)
        work = jnp.where(fully_past, gate, NEG_INF * MOBA_Q_SCALE)
        chosen = jnp.zeros((n_blocks, blk), jnp.bool_)
        for _ in range(min(MOBA_TOPK, n_blocks)):
            best = jnp.max(work, axis=0, keepdims=True)
            first = jnp.min(jnp.where(work == best, blk_idx, n_blocks), axis=0, keepdims=True)
            pick = blk_idx == first
            chosen = jnp.logical_or(chosen, pick)
            work = jnp.where(pick, -jnp.inf, work)
        term_ref[hh] = jnp.where(jnp.logical_and(chosen, fully_past), slope_s * block_dist, -NEG_INF)
        qh_ref[hh] = q_h.astype(BF16)
        alibi_ref[hh] = slope_s * rel

    def rows_of(block):
        return pl.ds(pl.multiple_of(block * blk, blk), blk)

    def scores(hh, block):
        return _dot(k_ref[0, rows_of(block), lane_tile(hh)], qh_ref[hh])

    def v_t(hh, block):
        return vt_ref[hh, :, rows_of(block)]

    def own_logits(hh, s):
        return jnp.where(rel >= 0, s - alibi_ref[hh], NEG_INF), None

    def past_logits(hh, s, block):
        return s - alibi_ref[hh], term_ref[hh, pl.ds(block, 1), :]

    _sweep_blocks(i, n_heads, state, scores, v_t, own_logits, past_logits, s_a, s_b)
    o_t = jnp.concatenate([state.normalized(hh) for hh in range(n_heads)], axis=0)
    o_ref[0] = o_t.T.astype(o_ref.dtype)


def _moba_attention(qkv, b, s):
    n_blocks = s // MOBA_BLOCK
    n_heads = MOBA_HEADS_PER_STEP
    width = n_heads * HEAD_DIM
    n_groups = MOBA_HEADS // n_heads
    score_buf = pltpu.VMEM((n_heads, MOBA_BLOCK, MOBA_BLOCK), F32)
    return pl.pallas_call(
        functools.partial(_moba_kernel, n_blocks=n_blocks),
        grid=(b, n_groups, n_blocks),
        in_specs=[pl.BlockSpec(memory_space=pltpu.SMEM),
                  pl.BlockSpec((1, MOBA_BLOCK, width), lambda bi, g, i: (bi, i, g)),
                  pl.BlockSpec((1, s, width), lambda bi, g, i: (bi, 0, n_groups + g)),
                  pl.BlockSpec((1, s, width), lambda bi, g, i: (bi, 0, 2 * n_groups + g))],
        out_specs=pl.BlockSpec((1, MOBA_BLOCK, width), lambda bi, g, i: (bi, i, g)),
        out_shape=jax.ShapeDtypeStruct((b, s, MOBA_HEADS * HEAD_DIM), BF16),
        scratch_shapes=[pltpu.VMEM((n_blocks, width), F32),
                        pltpu.VMEM((n_heads, HEAD_DIM + ONES_ROWS, s), BF16),
                        pltpu.VMEM((n_heads, n_blocks, MOBA_BLOCK), F32),
                        pltpu.VMEM((n_heads, MOBA_BLOCK, MOBA_BLOCK), F32),
                        pltpu.VMEM((n_heads, LANES, MOBA_BLOCK), BF16),
                        score_buf, score_buf,
                        *_SoftmaxState.scratch(n_heads, HEAD_DIM, MOBA_BLOCK)],
        compiler_params=_params(("parallel", "parallel", "arbitrary")),
        name="moba_attention",
    )(_alibi_slopes(MOBA_HEADS), qkv, qkv, qkv)


def _lane_half(x, half):
    lane = lax.broadcasted_iota(jnp.int32, x.shape, x.ndim - 1)
    keep = (lane < HEAD_DIM) if half == 0 else (lane >= HEAD_DIM)
    return jnp.where(keep, x, jnp.zeros_like(x))


BANDS_PER_STEP = 2


def _band_attention(n_heads, q_tile, k_tile, v_tile, head_softmax, store_pair, s_ref):
    chains = [(band, h) for band in range(BANDS_PER_STEP) for h in range(n_heads)]
    for c, (band, h) in enumerate(chains):
        s_ref[c] = _dot_nt(_lane_half(q_tile(h, band), h % 2), k_tile(h, band))
    even = None
    for c, (band, h) in enumerate(chains):
        p, den, extra = head_softmax(h, band, s_ref[c])
        o = _dot(p.astype(BF16), v_tile(h, band)) / den
        if h % 2 == 0:
            even = (o, extra)
            continue
        lane = lax.broadcasted_iota(jnp.int32, o.shape, 1)
        store_pair(h // 2, band, lane < HEAD_DIM, even, (o, extra))


def _band_rows(band, size):
    return slice(band * size, (band + 1) * size)


def _band_keys(prev_ref, cur_ref, band, size, cols):
    if band == 0:
        return jnp.concatenate([prev_ref[0, :, cols], cur_ref[0, :size, cols]], axis=0)
    return cur_ref[0, (band - 1) * size:(band + 1) * size, cols]


def _band_geometry(size, n):
    row = lax.broadcasted_iota(jnp.int32, (size, 2 * size), 0)
    col = lax.broadcasted_iota(jnp.int32, (size, 2 * size), 1)
    diff = row + size - col
    in_sequence = (n > 0) | (col >= size)
    return diff, in_sequence


def _dil_kernel(slopes_ref, q_ref, kp_ref, kc_ref, vp_ref, vc_ref, o_ref, lse_ref, s_ref, *, n_pts, dil, group):
    n = pl.program_id(2)
    scale = HEAD_DIM ** -0.5
    diff, in_sequence = _band_geometry(DIL_BAND, n)
    window = (diff >= 0) & (diff <= n_pts)
    masks = [window & in_sequence] + [window] * (BANDS_PER_STEP - 1)
    dist = (dil * diff).astype(F32)
    tile = lambda h: slice((h // 2) * LANES, (h // 2 + 1) * LANES)

    def head_softmax(h, band, s):
        slope = slopes_ref[group * DIL_HEADS_PER_GROUP + h]
        logits = jnp.where(masks[band], s * scale - slope * dist, NEG_INF)
        m = jnp.max(logits, axis=1, keepdims=True)
        e = jnp.exp(logits - m)
        den = jnp.sum(e, axis=1, keepdims=True)
        return e, den, m + jnp.log(den)

    def store_pair(pair, band, first_half, even, odd):
        rows, cols = _band_rows(band, DIL_BAND), slice(pair * LANES, (pair + 1) * LANES)
        o_ref[0, rows, cols] = jnp.where(first_half, even[0], odd[0])
        lse_ref[0, rows, cols] = jnp.where(first_half, even[1], odd[1])

    _band_attention(
        DIL_HEADS_PER_GROUP,
        lambda h, band: q_ref[0, _band_rows(band, DIL_BAND), tile(h)],
        lambda h, band: _band_keys(kp_ref, kc_ref, band, DIL_BAND, tile(h)),
        lambda h, band: _band_keys(vp_ref, vc_ref, band, DIL_BAND, tile(h)),
        head_softmax, store_pair, s_ref)


def _dilated_group(qkv, b, s, group, window, dil):
    width = DIL_HEADS_PER_GROUP * HEAD_DIM
    n_sub = s // dil
    step_rows = BANDS_PER_STEP * DIL_BAND
    assert n_sub % step_rows == 0
    view = qkv.reshape(b, n_sub, dil * 3 * width)

    def col(part):
        return lambda bi, r, n: (bi, n, r * 3 + part)

    def col_prev(part):
        return lambda bi, r, n: (bi, jnp.maximum(n * BANDS_PER_STEP - 1, 0), r * 3 + part)

    blk, prev_blk = (1, step_rows, width), (1, DIL_BAND, width)
    out_spec = pl.BlockSpec(blk, lambda bi, r, n: (bi, n, r))
    out_sds = jax.ShapeDtypeStruct((b, n_sub, dil * width), F32)
    o, lse = pl.pallas_call(
        functools.partial(_dil_kernel, n_pts=window // dil, dil=dil, group=group),
        grid=(b, dil, n_sub // step_rows),
        in_specs=[pl.BlockSpec(memory_space=pltpu.SMEM),
                  pl.BlockSpec(blk, col(0)),
                  pl.BlockSpec(prev_blk, col_prev(1)), pl.BlockSpec(blk, col(1)),
                  pl.BlockSpec(prev_blk, col_prev(2)), pl.BlockSpec(blk, col(2))],
        out_specs=[out_spec, out_spec],
        out_shape=[out_sds, out_sds],
        scratch_shapes=[pltpu.VMEM((BANDS_PER_STEP * DIL_HEADS_PER_GROUP, DIL_BAND, 2 * DIL_BAND), F32)],
        compiler_params=_params(("parallel", "parallel", "arbitrary")),
        name=f"dilated_attention_g{group}",
    )(_alibi_slopes(len(DIL_PAIRS) * DIL_HEADS_PER_GROUP), view, view, view, view, view)
    return o.reshape(b * n_sub, dil * width), lse.reshape(b * n_sub, dil * width)


def _mla_proj_kernel(h_ref, g_ref, wd_ref, gq_ref, gkv_ref, wuq_ref, wuk_ref, wuv_ref,
                     cos_ref, sin_up_ref, sin_dn_ref, q_ref, k_ref, v_ref):
    n = _rms(h_ref[...], g_ref[...]).astype(BF16)
    c = _dot(n, wd_ref[...])
    nq = _rms(c[:, :MLA_Q_RANK], gq_ref[...]).astype(BF16)
    nkv = _rms(c[:, MLA_Q_RANK:MLA_Q_RANK + MLA_KV_RANK], gkv_ref[...]).astype(BF16)
    k_rope = c[:, MLA_Q_RANK + MLA_KV_RANK:]
    cos, sin_up, sin_dn = cos_ref[...], sin_up_ref[...], sin_dn_ref[...]
    half = MLA_ROPE // 2

    def rope(x):
        return x * cos + pltpu.roll(x, half, 1) * sin_up + pltpu.roll(x, LANES - half, 1) * sin_dn

    kr = rope(k_rope)
    for h in range(MLA_HEADS):
        cols = slice(h * LANES, (h + 1) * LANES)
        q_ref[:, cols] = (rope(_dot(nq, wuq_ref[:, cols])) * MLA_Q_SCALE).astype(BF16)
        k_ref[:, cols] = (_dot(nkv, wuk_ref[:, cols]) + kr).astype(BF16)
    v_ref[...] = _dot(nkv, wuv_ref[...]).astype(BF16)


def _mla_proj(h, g, w_dkv, q_norm, w_uq, kv_norm, w_ukv, s):
    m, d = h.shape
    qk = MLA_NOPE + MLA_ROPE
    wd = jnp.concatenate([w_dkv[:, :MLA_Q_RANK + MLA_KV_RANK],
                          jnp.zeros((d, MLA_NOPE), F32), w_dkv[:, MLA_Q_RANK + MLA_KV_RANK:],
                          jnp.zeros((d, LANES - qk), F32)], axis=1).astype(BF16)
    wuq = jnp.pad(w_uq.reshape(MLA_Q_RANK, MLA_HEADS, qk), ((0, 0), (0, 0), (0, LANES - qk)))
    wuq = wuq.reshape(MLA_Q_RANK, MLA_HEADS * LANES).astype(BF16)
    w_ukv = w_ukv.reshape(MLA_KV_RANK, MLA_HEADS, MLA_NOPE + MLA_V)
    wuk = jnp.pad(w_ukv[:, :, :MLA_NOPE], ((0, 0), (0, 0), (0, LANES - MLA_NOPE)))
    wuk = wuk.reshape(MLA_KV_RANK, MLA_HEADS * LANES).astype(BF16)
    wuv = w_ukv[:, :, MLA_NOPE:].reshape(MLA_KV_RANK, MLA_HEADS * MLA_V).astype(BF16)
    half = MLA_ROPE // 2
    inv = ROPE_THETA ** (-jnp.arange(0, MLA_ROPE, 2, dtype=F32) / MLA_ROPE)
    ang = jnp.arange(s).astype(F32)[:, None] * inv[None, :]
    cos, sin = jnp.cos(ang), jnp.sin(ang)
    zeros = lambda w: jnp.zeros((s, w), F32)
    cos_t = jnp.concatenate([jnp.ones((s, MLA_NOPE), F32), cos, cos, zeros(LANES - qk)], axis=1)
    sin_up = jnp.concatenate([zeros(MLA_NOPE + half), sin, zeros(LANES - qk)], axis=1)
    sin_dn = jnp.concatenate([zeros(MLA_NOPE), -sin, zeros(half + LANES - qk)], axis=1)

    tiles_per_seq = s // ROW_TILE
    row = lambda width: pl.BlockSpec((ROW_TILE, width), lambda i: (i, 0))
    table = pl.BlockSpec((ROW_TILE, LANES), lambda i: (i % tiles_per_seq, 0))
    return pl.pallas_call(
        _mla_proj_kernel,
        grid=(m // ROW_TILE,),
        in_specs=[row(d), _resident((1, d)), _resident(wd.shape),
                  _resident((1, MLA_Q_RANK)), _resident((1, MLA_KV_RANK)),
                  _resident(wuq.shape), _resident(wuk.shape), _resident(wuv.shape),
                  table, table, table],
        out_specs=[row(MLA_HEADS * LANES), row(MLA_HEADS * LANES), row(MLA_HEADS * MLA_V)],
        out_shape=[jax.ShapeDtypeStruct((m, MLA_HEADS * LANES), BF16),
                   jax.ShapeDtypeStruct((m, MLA_HEADS * LANES), BF16),
                   jax.ShapeDtypeStruct((m, MLA_HEADS * MLA_V), BF16)],
        compiler_params=_params(("parallel",)),
        name="mla_proj",
    )(h, g.reshape(1, d), wd, q_norm.reshape(1, -1), kv_norm.reshape(1, -1), wuq, wuk, wuv,
      cos_t, sin_up, sin_dn)


def _mla_kernel(q_ref, k_ref, v_ref, o_ref, vt_ref, qh_ref, s_a, s_b, m_ref, acc_ref, *, n_tiles):
    t = MLA_TILE
    n_heads = MLA_HEADS_PER_STEP
    i = pl.program_id(2)
    state = _SoftmaxState(m_ref, acc_ref, MLA_V)

    @pl.when(i == 0)
    def _():
        for j in range(n_tiles):
            rows = slice(j * t, (j + 1) * t)
            v_t = v_ref[0, rows, :].astype(F32).T
            for hh in range(n_heads):
                state.store_values_t(vt_ref, hh, rows, v_t[hh * MLA_V:(hh + 1) * MLA_V, :])

    state.init()
    key = lax.broadcasted_iota(jnp.int32, (t, t), 0)
    qry = lax.broadcasted_iota(jnp.int32, (t, t), 1)
    lane_tile = lambda hh: slice(hh * LANES, (hh + 1) * LANES)
    for hh in range(n_heads):
        qh_ref[hh] = _transpose_bf16(q_ref[0, :, lane_tile(hh)])

    def rows_of(block):
        return pl.ds(pl.multiple_of(block * t, t), t)

    def scores(hh, block):
        return _dot(k_ref[0, rows_of(block), lane_tile(hh)], qh_ref[hh])

    def v_t(hh, block):
        return vt_ref[hh, :, rows_of(block)]

    def own_logits(hh, s):
        return jnp.where(key <= qry, s, NEG_INF), None

    def past_logits(hh, s, block):
        return s, None

    _sweep_blocks(i, n_heads, state, scores, v_t, own_logits, past_logits, s_a, s_b)
    o_t = jnp.concatenate([state.normalized(hh) for hh in range(n_heads)], axis=0)
    o_ref[0] = o_t.T.astype(o_ref.dtype)


def _mla_attention(q, k, v, b, s):
    n_heads = MLA_HEADS_PER_STEP
    n_groups = MLA_HEADS // n_heads
    whole_seq = lambda bi, g, i: (bi, 0, g)
    tile = lambda bi, g, i: (bi, i, g)
    score_buf = pltpu.VMEM((n_heads, MLA_TILE, MLA_TILE), F32)
    return pl.pallas_call(
        functools.partial(_mla_kernel, n_tiles=s // MLA_TILE),
        grid=(b, n_groups, s // MLA_TILE),
        in_specs=[pl.BlockSpec((1, MLA_TILE, n_heads * LANES), tile),
                  pl.BlockSpec((1, s, n_heads * LANES), whole_seq),
                  pl.BlockSpec((1, s, n_heads * MLA_V), whole_seq)],
        out_specs=pl.BlockSpec((1, MLA_TILE, n_heads * MLA_V), tile),
        out_shape=jax.ShapeDtypeStruct((b, s, MLA_HEADS * MLA_V), BF16),
        scratch_shapes=[pltpu.VMEM((n_heads, MLA_V + ONES_ROWS, s), BF16),
                        pltpu.VMEM((n_heads, LANES, MLA_TILE), BF16),
                        score_buf, score_buf,
                        *_SoftmaxState.scratch(n_heads, MLA_V, MLA_TILE)],
        compiler_params=_params(("parallel", "parallel", "arbitrary")),
        name="mla_attention",
    )(q, k, v)


def _swa_kernel(slopes_ref, sinks_ref, q_ref, kp_ref, kc_ref, vp_ref, vc_ref, o_ref, s_ref):
    n = pl.program_id(1)
    scale = HEAD_DIM ** -0.5
    group = SWA_Q_HEADS // SWA_KV_HEADS
    diff, in_sequence = _band_geometry(SWA_BLOCK, n)
    window = (diff >= 0) & (diff < SWA_WINDOW)
    masks = [window & in_sequence] + [window] * (BANDS_PER_STEP - 1)
    dist = diff.astype(F32)

    def both_halves(prev_ref, cur_ref, band, kv_head):
        x = _band_keys(prev_ref, cur_ref, band, SWA_BLOCK, slice(None))
        part = x[:, kv_head * HEAD_DIM:(kv_head + 1) * HEAD_DIM]
        return jnp.concatenate([part, part], axis=1)

    bands_kv = [(band, kv) for band in range(BANDS_PER_STEP) for kv in range(SWA_KV_HEADS)]
    k_tiles = {bk: both_halves(kp_ref, kc_ref, *bk) for bk in bands_kv}
    v_tiles = {bk: both_halves(vp_ref, vc_ref, *bk) for bk in bands_kv}

    def head_softmax(h, band, s):
        logits = jnp.where(masks[band], s * scale - slopes_ref[h] * dist, NEG_INF)
        m = jnp.maximum(jnp.max(logits, axis=1, keepdims=True), sinks_ref[h])
        e = jnp.exp(logits - m)
        return e, jnp.sum(e, axis=1, keepdims=True) + jnp.exp(sinks_ref[h] - m), None

    def store_pair(pair, band, first_half, even, odd):
        o = jnp.where(first_half, even[0], odd[0]).astype(o_ref.dtype)
        o_ref[0, _band_rows(band, SWA_BLOCK), pair * LANES:(pair + 1) * LANES] = o

    _band_attention(
        SWA_Q_HEADS,
        lambda h, band: q_ref[0, _band_rows(band, SWA_BLOCK), (h // 2) * LANES:(h // 2 + 1) * LANES],
        lambda h, band: k_tiles[band, h // group],
        lambda h, band: v_tiles[band, h // group],
        head_softmax, store_pair, s_ref)


def _swa_attention(qkv, sinks, b, s):
    q_width = SWA_Q_HEADS * HEAD_DIM
    kv_width = SWA_KV_HEADS * HEAD_DIM
    k_tile = q_width // kv_width
    step_rows = BANDS_PER_STEP * SWA_BLOCK
    assert s % step_rows == 0
    cur = lambda t: (lambda bi, n: (bi, n, t))
    prev = lambda t: (lambda bi, n: (bi, jnp.maximum(n * BANDS_PER_STEP - 1, 0), t))
    kv_blk, kv_prev_blk = (1, step_rows, kv_width), (1, SWA_BLOCK, kv_width)
    return pl.pallas_call(
        _swa_kernel,
        grid=(b, s // step_rows),
        in_specs=[pl.BlockSpec(memory_space=pltpu.SMEM), pl.BlockSpec(memory_space=pltpu.SMEM),
                  pl.BlockSpec((1, step_rows, q_width), cur(0)),
                  pl.BlockSpec(kv_prev_blk, prev(k_tile)), pl.BlockSpec(kv_blk, cur(k_tile)),
                  pl.BlockSpec(kv_prev_blk, prev(k_tile + 1)), pl.BlockSpec(kv_blk, cur(k_tile + 1))],
        out_specs=pl.BlockSpec((1, step_rows, q_width), cur(0)),
        out_shape=jax.ShapeDtypeStruct((b, s, q_width), BF16),
        scratch_shapes=[pltpu.VMEM((BANDS_PER_STEP * SWA_Q_HEADS, SWA_BLOCK, 2 * SWA_BLOCK), F32)],
        compiler_params=_params(("parallel", "arbitrary")),
        name="swa_attention",
    )(_alibi_slopes(SWA_Q_HEADS), sinks.astype(F32), qkv, qkv, qkv, qkv, qkv)


def kernel(x, l0_attn_norm, l0_w_qkv, l0_w_o, l0_mlp_norm, l0_w_up, l0_w_down, l1_attn_norm, l1_w_qkv, l1_w_o, l1_mlp_norm, l1_w_up, l1_w_down, l2_attn_norm, l2_w_dkv, l2_q_norm, l2_w_uq, l2_kv_norm, l2_w_ukv, l2_w_o, l2_mlp_norm, l2_w_up, l2_w_down, l3_attn_norm, l3_w_qkv, l3_sinks, l3_w_o, l3_mlp_norm, l3_w_up, l3_w_down, final_norm):
    b, s, d = x.shape
    bf = lambda w: w.astype(BF16)
    h = x.reshape(b * s, d)

    qkv = _norm_proj(h, l0_attn_norm, bf(l0_w_qkv), scaled_cols=MOBA_HEADS * HEAD_DIM, scale=MOBA_Q_SCALE)
    a = _moba_attention(qkv.reshape(b, s, -1), b, s).reshape(b * s, -1)
    h = _post(h, (a,), bf(l0_w_o), l0_mlp_norm, bf(l0_w_up), bf(l0_w_down))

    qkvs = _dil_proj(h, l1_attn_norm, bf(l1_w_qkv))
    groups = [_dilated_group(qkvs[g], b, s, g, window, dil) for g, (window, dil) in enumerate(DIL_PAIRS)]
    h = _post(h, tuple(o for o, _ in groups) + tuple(l for _, l in groups),
              bf(l1_w_o), l1_mlp_norm, bf(l1_w_up), bf(l1_w_down))

    q, k, v = _mla_proj(h, l2_attn_norm, l2_w_dkv, l2_q_norm, l2_w_uq, l2_kv_norm, l2_w_ukv, s)
    a = _mla_attention(q.reshape(b, s, -1), k.reshape(b, s, -1), v.reshape(b, s, -1), b, s).reshape(b * s, -1)
    h = _post(h, (a,), bf(l2_w_o), l2_mlp_norm, bf(l2_w_up), bf(l2_w_down))

    qkv = _norm_proj(h, l3_attn_norm, bf(l3_w_qkv), col_tile=256)
    a = _swa_attention(qkv.reshape(b, s, -1), l3_sinks, b, s).reshape(b * s, -1)
    h = _post(h, (a,), bf(l3_w_o), l3_mlp_norm, bf(l3_w_up), bf(l3_w_down), g_final=final_norm)
    return h.reshape(b, s, d)
```

```python
import functools

import jax
import jax.numpy as jnp
from jax import lax
from jax.experimental import pallas as pl
from jax.experimental.pallas import tpu as pltpu

F32 = jnp.float32
BF16 = jnp.bfloat16

D_MODEL = 1024
HEAD_DIM = 64
RMS_EPS = 1e-6
D_FF = 4 * D_MODEL
NEG_INF = -1e30

MOBA_HEADS = 16
MOBA_BLOCK = 256
MOBA_TOPK = 3

DIL_PAIRS = ((128, 1), (512, 4), (2048, 16))
DIL_HEADS_PER_GROUP = 8
DIL_BAND = 128

MLA_HEADS = 16
MLA_Q_RANK = 768
MLA_KV_RANK = 256
MLA_NOPE = 64
MLA_ROPE = 32
MLA_V = 64
ROPE_THETA = 10000.0
MLA_TILE = 256
MLA_HEADS_PER_STEP = 4
MOBA_HEADS_PER_STEP = 4

SWA_Q_HEADS = 16
SWA_KV_HEADS = 2
SWA_WINDOW = 128
SWA_BLOCK = 128

LANES = 128
ROW_TILE = 512
FF_TILE = 512
VMEM_LIMIT = 56 * 1024 * 1024
LOG2E = 1.4426950408889634
HEAD_Q_SCALE = HEAD_DIM ** -0.5 * LOG2E
MLA_Q_SCALE = (MLA_NOPE + MLA_ROPE) ** -0.5 * LOG2E


def _params(semantics):
    return pltpu.CompilerParams(dimension_semantics=semantics, vmem_limit_bytes=VMEM_LIMIT)


def _resident(shape):
    return pl.BlockSpec(shape, lambda *_: (0,) * len(shape), pipeline_mode=pl.Buffered(1))


def _alibi_slopes(n_heads):
    return 2.0 ** (-8.0 * jnp.arange(1, n_heads + 1, dtype=F32) / n_heads)


def _rms(x, g):
    return x * lax.rsqrt(jnp.mean(x * x, axis=-1, keepdims=True) + RMS_EPS) * g


def _dot(a, b):
    return jnp.dot(a, b, preferred_element_type=F32)


def _dot_nt(a, b):
    return lax.dot_general(a, b, (((1,), (1,)), ((), ())), preferred_element_type=F32)


def _norm_proj_kernel(h_ref, g_ref, w_ref, o_ref, *, col_tile, scaled_cols, scale):
    n = _rms(h_ref[...], g_ref[...]).astype(BF16)
    for c in range(o_ref.shape[1] // col_tile):
        cols = slice(c * col_tile, (c + 1) * col_tile)
        y = _dot(n, w_ref[:, cols])
        if (c + 1) * col_tile <= scaled_cols:
            y = y * scale
        o_ref[:, cols] = y.astype(o_ref.dtype)


def _norm_proj(h, g, w, col_tile=512, scaled_cols=0, scale=1.0):
    m, d = h.shape
    n_out = w.shape[1]
    assert scaled_cols % col_tile == 0
    return pl.pallas_call(
        functools.partial(_norm_proj_kernel, col_tile=col_tile, scaled_cols=scaled_cols, scale=scale),
        grid=(m // ROW_TILE,),
        in_specs=[pl.BlockSpec((ROW_TILE, d), lambda i: (i, 0)),
                  _resident((1, d)),
                  _resident((d, n_out))],
        out_specs=pl.BlockSpec((ROW_TILE, n_out), lambda i: (i, 0)),
        out_shape=jax.ShapeDtypeStruct((m, n_out), BF16),
        compiler_params=_params(("parallel",)),
        name="norm_proj",
    )(h, g.reshape(1, d), w)


def _dil_proj_kernel(h_ref, g_ref, w_ref, o0_ref, o1_ref, o2_ref, stage_ref):
    n = _rms(h_ref[...], g_ref[...]).astype(BF16)
    rows = h_ref.shape[0]
    width = DIL_HEADS_PER_GROUP * HEAD_DIM
    for g, (o_ref, (_, dil)) in enumerate(zip((o0_ref, o1_ref, o2_ref), DIL_PAIRS)):
        for part in range(3):
            src = (3 * g + part) * width
            y = _dot(n, w_ref[:, src:src + width])
            if part == 0:
                y = y * HEAD_Q_SCALE
            if dil == 1:
                o_ref[:, part * width:(part + 1) * width] = y.astype(BF16)
                continue
            for t in range(width // LANES):
                stage_ref[g - 1, part, t] = y[:, t * LANES:(t + 1) * LANES]
            for r in range(dil):
                for t in range(width // LANES):
                    dst = (3 * r + part) * width + t * LANES
                    strided = stage_ref[g - 1, part, t, pl.ds(r, rows // dil, stride=dil), :]
                    o_ref[:, dst:dst + LANES] = strided.astype(BF16)


def _dil_proj(h, g, w):
    m, d = h.shape
    feat = 3 * DIL_HEADS_PER_GROUP * HEAD_DIM
    assert DIL_PAIRS[0][1] == 1
    view = lambda dil: (m // dil, dil * feat)
    view_block = lambda dil: pl.BlockSpec((ROW_TILE // dil, dil * feat), lambda i: (i, 0))
    return pl.pallas_call(
        _dil_proj_kernel,
        grid=(m // ROW_TILE,),
        in_specs=[pl.BlockSpec((ROW_TILE, d), lambda i: (i, 0)), _resident((1, d)), _resident(w.shape)],
        out_specs=[view_block(dil) for _, dil in DIL_PAIRS],
        out_shape=[jax.ShapeDtypeStruct(view(dil), BF16) for _, dil in DIL_PAIRS],
        scratch_shapes=[pltpu.VMEM((len(DIL_PAIRS) - 1, 3, feat // 3 // LANES, ROW_TILE, LANES), F32)],
        compiler_params=_params(("parallel",)),
        name="dil_proj",
    )(h, g.reshape(1, d), w)


def _mlp_tail(h1, g_ref, wup_ref, wdn_ref, gf_ref, out_ref):
    n = _rms(h1, g_ref[...]).astype(BF16)
    acc = jnp.zeros_like(h1)
    for c in range(wup_ref.shape[1] // FF_TILE):
        cols = slice(c * FF_TILE, (c + 1) * FF_TILE)
        u = jnp.square(jnp.maximum(_dot(n, wup_ref[:, cols]), 0.0)).astype(BF16)
        acc = acc + _dot(u, wdn_ref[cols, :])
    out = h1 + acc
    if gf_ref is not None:
        out = _rms(out, gf_ref[...])
    out_ref[...] = out


def _post_kernel(h_ref, a_ref, wo_ref, g_ref, wup_ref, wdn_ref, *rest):
    gf_ref, out_ref = (rest[0], rest[1]) if len(rest) == 2 else (None, rest[0])
    h1 = h_ref[...] + _dot(a_ref[...], wo_ref[...])
    _mlp_tail(h1, g_ref, wup_ref, wdn_ref, gf_ref, out_ref)


def _post_merge_kernel(h_ref, o0_ref, o1_ref, o2_ref, l0_ref, l1_ref, l2_ref,
                       wo_ref, g_ref, wup_ref, wdn_ref, out_ref, stage_ref):
    rows, width = o0_ref.shape

    def token_rows(x_ref, slot, dil):
        if dil == 1:
            return x_ref[...]
        for r in range(dil):
            for t in range(width // LANES):
                src = r * width + t * LANES
                stage_ref[slot, t, pl.ds(r, rows // dil, stride=dil), :] = x_ref[:, src:src + LANES]
        return jnp.concatenate([stage_ref[slot, t] for t in range(width // LANES)], axis=1)

    dils = [dil for _, dil in DIL_PAIRS]
    l0, l1, l2 = (token_rows(ref, slot, dil) for slot, (ref, dil) in enumerate(zip((l0_ref, l1_ref, l2_ref), dils)))
    o0, o1, o2 = (token_rows(ref, 3 + slot, dil) for slot, (ref, dil) in enumerate(zip((o0_ref, o1_ref, o2_ref), dils)))
    mx = jnp.maximum(jnp.maximum(l0, l1), l2)
    e0, e1, e2 = jnp.exp(l0 - mx), jnp.exp(l1 - mx), jnp.exp(l2 - mx)
    merged = (e0 * o0 + e1 * o1 + e2 * o2) / (e0 + e1 + e2)
    h1 = h_ref[...] + _dot(merged.astype(BF16), wo_ref[...])
    _mlp_tail(h1, g_ref, wup_ref, wdn_ref, None, out_ref)


def _post(h, attn_ins, w_o, g, w_up, w_down, g_final=None):
    m, d = h.shape
    row = lambda width: pl.BlockSpec((ROW_TILE, width), lambda i: (i, 0))
    merge = len(attn_ins) > 1
    in_specs = [row(d)] + [pl.BlockSpec((ROW_TILE * a.shape[0] // m, a.shape[1]), lambda i: (i, 0)) for a in attn_ins]
    in_specs += [_resident(w_o.shape), _resident((1, d)), _resident(w_up.shape), _resident(w_down.shape)]
    args = [h, *attn_ins, w_o, g.reshape(1, d), w_up, w_down]
    if g_final is not None:
        in_specs.append(_resident((1, d)))
        args.append(g_final.reshape(1, d))
    scratch = [pltpu.VMEM((len(attn_ins), w_o.shape[0] // LANES, ROW_TILE, LANES), F32)] if merge else []
    return pl.pallas_call(
        _post_merge_kernel if merge else _post_kernel,
        grid=(m // ROW_TILE,),
        in_specs=in_specs,
        out_specs=row(d),
        out_shape=jax.ShapeDtypeStruct((m, d), F32),
        scratch_shapes=scratch,
        compiler_params=_params(("parallel",)),
        name="post_mlp",
    )(*args)


ONES_ROWS = 16


class _SoftmaxState:
    def __init__(self, m_ref, acc_ref, dv):
        self.m_ref, self.acc_ref, self.dv = m_ref, acc_ref, dv

    @staticmethod
    def scratch(n_heads, dv, tq):
        return [pltpu.VMEM((n_heads, 1, tq), F32), pltpu.VMEM((n_heads, dv + ONES_ROWS, tq), F32)]

    @staticmethod
    def store_values_t(vt_ref, hh, cols, v_t):
        dv = v_t.shape[0]
        vt_ref[hh, :dv, cols] = v_t.astype(BF16)
        vt_ref[hh, dv:, cols] = jnp.ones((ONES_ROWS, v_t.shape[1]), BF16)

    def init(self):
        self.m_ref[...] = jnp.full(self.m_ref.shape, NEG_INF, F32)
        self.acc_ref[...] = jnp.zeros(self.acc_ref.shape, F32)

    def step(self, hh, x, v_t, shift=None):
        m = self.m_ref[hh]
        m_cur = jnp.max(x, axis=0, keepdims=True)
        if shift is not None:
            m_cur = m_cur - shift
        m_new = jnp.maximum(m, m_cur)
        alpha = jnp.exp2(m - m_new)
        p = jnp.exp2(x - (m_new if shift is None else m_new + shift))
        self.m_ref[hh] = m_new
        self.acc_ref[hh] = alpha * self.acc_ref[hh] + _dot(v_t, p.astype(BF16))

    def normalized(self, hh):
        return self.acc_ref[hh, :self.dv, :] / self.acc_ref[hh, self.dv:self.dv + 1, :]


def _sweep_blocks(n_past, n_heads, state, score_fn, values_fn, own_logits, past_logits, s_a, s_b):
    def produce(buf, block):
        for hh in range(n_heads):
            buf[hh] = score_fn(hh, block)

    def consume(buf, logits_fn, block):
        for hh in range(n_heads):
            x, shift = logits_fn(hh, buf[hh])
            state.step(hh, x, values_fn(hh, block), shift)

    produce(s_b, n_past)
    produce(s_a, 0)
    consume(s_b, own_logits, n_past)

    def two_blocks(first):
        produce(s_b, first + 1)
        consume(s_a, functools.partial(past_logits, block=first), first)
        produce(s_a, jnp.minimum(first + 2, n_past - 1))
        consume(s_b, functools.partial(past_logits, block=first + 1), first + 1)

    def quad(t, carry):
        two_blocks(4 * t)
        two_blocks(4 * t + 2)
        return carry

    def pair(t, carry):
        two_blocks(4 * (n_past // 4) + 2 * t)
        return carry

    lax.fori_loop(0, n_past // 4, quad, 0)
    lax.fori_loop(0, (n_past % 4) // 2, pair, 0)

    @pl.when(n_past % 2 == 1)
    def _():
        consume(s_a, functools.partial(past_logits, block=n_past - 1), n_past - 1)


def _transpose_bf16(x):
    return x.astype(F32).T.astype(BF16)


def _keep_head_rows(x_t, half):
    r = lax.broadcasted_iota(jnp.int32, x_t.shape, 0)
    keep = (r < HEAD_DIM) if half == 0 else (r >= HEAD_DIM)
    return jnp.where(keep, x_t, jnp.zeros_like(x_t))


def _moba_kernel(slopes_ref, q_ref, k_ref, v_ref, o_ref, kmean_ref, vt_ref, term_ref, alibi_ref, qh_ref,
                 s_a, s_b, m_ref, acc_ref, *, n_blocks):
    blk = MOBA_BLOCK
    n_heads = MOBA_HEADS_PER_STEP
    group = pl.program_id(1)
    i = pl.program_id(2)
    state = _SoftmaxState(m_ref, acc_ref, HEAD_DIM)

    @pl.when(i == 0)
    def _():
        for j in range(n_blocks):
            rows = slice(j * blk, (j + 1) * blk)
            kmean_ref[j:j + 1, :] = jnp.mean(k_ref[0, rows, :].astype(F32), axis=0, keepdims=True)
            v_t = v_ref[0, rows, :].astype(F32).T
            for hh in range(n_heads):
                state.store_values_t(vt_ref, hh, rows, v_t[hh * HEAD_DIM:(hh + 1) * HEAD_DIM, :])

    state.init()
    q_t = q_ref[0].astype(F32).T
    key = lax.broadcasted_iota(jnp.int32, (blk, blk), 0)
    qry = lax.broadcasted_iota(jnp.int32, (blk, blk), 1)
    rel = (qry - key).astype(F32)
    blk_idx = lax.broadcasted_iota(jnp.int32, (n_blocks, blk), 0)
    fully_past = blk_idx < i
    block_dist = ((i - blk_idx) * blk).astype(F32)
    lane_tile = lambda hh: slice((hh // 2) * LANES, (hh // 2 + 1) * LANES)

    for hh in range(n_heads):
        slope_s = slopes_ref[n_heads * group + hh] * LOG2E
        q_h = _keep_head_rows(q_t[lane_tile(hh), :], hh % 2)
        gate = lax.dot_general(kmean_ref[:, lane_tile(hh)], q_h, (((1,), (0,)), ((), ())),
                               precision=lax.Precision.HIGHEST, preferred_element_type=F32)
        work = jnp.where(fully_past, gate, NEG_INF * HEAD_Q_SCALE)
        chosen = jnp.zeros((n_blocks, blk), jnp.bool_)
        for _ in range(min(MOBA_TOPK, n_blocks)):
            best = jnp.max(work, axis=0, keepdims=True)
            first = jnp.min(jnp.where(work == best, blk_idx, n_blocks), axis=0, keepdims=True)
            pick = blk_idx == first
            chosen = jnp.logical_or(chosen, pick)
            work = jnp.where(pick, -jnp.inf, work)
        term_ref[hh] = jnp.where(jnp.logical_and(chosen, fully_past), slope_s * block_dist, -NEG_INF)
        qh_ref[hh] = q_h.astype(BF16)
        alibi_ref[hh] = slope_s * rel

    def rows_of(block):
        return pl.ds(pl.multiple_of(block * blk, blk), blk)

    def scores(hh, block):
        return _dot(k_ref[0, rows_of(block), lane_tile(hh)], qh_ref[hh])

    def v_t(hh, block):
        return vt_ref[hh, :, rows_of(block)]

    def own_logits(hh, s):
        return jnp.where(rel >= 0, s - alibi_ref[hh], NEG_INF), None

    def past_logits(hh, s, block):
        return s - alibi_ref[hh], term_ref[hh, pl.ds(block, 1), :]

    _sweep_blocks(i, n_heads, state, scores, v_t, own_logits, past_logits, s_a, s_b)
    o_t = jnp.concatenate([state.normalized(hh) for hh in range(n_heads)], axis=0)
    o_ref[0] = o_t.T.astype(o_ref.dtype)


def _moba_attention(qkv, b, s):
    n_blocks = s // MOBA_BLOCK
    n_heads = MOBA_HEADS_PER_STEP
    width = n_heads * HEAD_DIM
    n_groups = MOBA_HEADS // n_heads
    score_buf = pltpu.VMEM((n_heads, MOBA_BLOCK, MOBA_BLOCK), F32)
    return pl.pallas_call(
        functools.partial(_moba_kernel, n_blocks=n_blocks),
        grid=(b, n_groups, n_blocks),
        in_specs=[pl.BlockSpec(memory_space=pltpu.SMEM),
                  pl.BlockSpec((1, MOBA_BLOCK, width), lambda bi, g, i: (bi, i, g)),
                  pl.BlockSpec((1, s, width), lambda bi, g, i: (bi, 0, n_groups + g)),
                  pl.BlockSpec((1, s, width), lambda bi, g, i: (bi, 0, 2 * n_groups + g))],
        out_specs=pl.BlockSpec((1, MOBA_BLOCK, width), lambda bi, g, i: (bi, i, g)),
        out_shape=jax.ShapeDtypeStruct((b, s, MOBA_HEADS * HEAD_DIM), BF16),
        scratch_shapes=[pltpu.VMEM((n_blocks, width), F32),
                        pltpu.VMEM((n_heads, HEAD_DIM + ONES_ROWS, s), BF16),
                        pltpu.VMEM((n_heads, n_blocks, MOBA_BLOCK), F32),
                        pltpu.VMEM((n_heads, MOBA_BLOCK, MOBA_BLOCK), F32),
                        pltpu.VMEM((n_heads, LANES, MOBA_BLOCK), BF16),
                        score_buf, score_buf,
                        *_SoftmaxState.scratch(n_heads, HEAD_DIM, MOBA_BLOCK)],
        compiler_params=_params(("parallel", "parallel", "arbitrary")),
        name="moba_attention",
    )(_alibi_slopes(MOBA_HEADS), qkv, qkv, qkv)


def _lane_half(x, half):
    lane = lax.broadcasted_iota(jnp.int32, x.shape, x.ndim - 1)
    keep = (lane < HEAD_DIM) if half == 0 else (lane >= HEAD_DIM)
    return jnp.where(keep, x, jnp.zeros_like(x))


BANDS_PER_STEP = 2


def _band_attention(n_heads, q_tile, k_tile, v_tile, logits_fn, floor_fn, store_pair, s_ref, with_lse):
    chains = [(band, h) for band in range(BANDS_PER_STEP) for h in range(n_heads)]
    for c, (band, h) in enumerate(chains):
        s_ref[c] = _dot_nt(k_tile(h, band), _lane_half(q_tile(h, band), h % 2))
    even = None
    for c, (band, h) in enumerate(chains):
        x = logits_fn(h, band, s_ref[c])
        m = jnp.max(x, axis=0, keepdims=True)
        floor = floor_fn(h)
        if floor is not None:
            m = jnp.maximum(m, floor)
        p = jnp.exp2(x - m)
        den = jnp.sum(p, axis=0, keepdims=True)
        if floor is not None:
            den = den + jnp.exp2(floor - m)
        o_t = lax.dot_general(v_tile(h, band), p.astype(BF16), (((0,), (0,)), ((), ())),
                              preferred_element_type=F32)
        half = slice((h % 2) * HEAD_DIM, (h % 2 + 1) * HEAD_DIM)
        o_t = o_t[half, :] / den
        lse_t = jnp.broadcast_to((m + jnp.log2(den)) * (1.0 / LOG2E), o_t.shape) if with_lse else None
        if h % 2 == 0:
            even = (o_t, lse_t)
            continue
        store_pair(h // 2, band, jnp.concatenate([even[0], o_t], axis=0).T,
                   jnp.concatenate([even[1], lse_t], axis=0).T if with_lse else None)


def _band_rows(band, size):
    return slice(band * size, (band + 1) * size)


def _band_keys(prev_ref, cur_ref, band, size, cols):
    if band == 0:
        return jnp.concatenate([prev_ref[0, :, cols], cur_ref[0, :size, cols]], axis=0)
    return cur_ref[0, (band - 1) * size:(band + 1) * size, cols]


def _band_geometry(size, n):
    key = lax.broadcasted_iota(jnp.int32, (2 * size, size), 0)
    qry = lax.broadcasted_iota(jnp.int32, (2 * size, size), 1)
    diff = qry + size - key
    in_sequence = (n > 0) | (key >= size)
    return diff, in_sequence


def _dil_kernel(slopes_ref, q_ref, kp_ref, kc_ref, vp_ref, vc_ref, o_ref, lse_ref, s_ref, *, n_pts, dil, group):
    n = pl.program_id(2)
    diff, in_sequence = _band_geometry(DIL_BAND, n)
    window = (diff >= 0) & (diff <= n_pts)
    masks = [window & in_sequence] + [window] * (BANDS_PER_STEP - 1)
    dist = (dil * diff).astype(F32) * LOG2E
    tile = lambda h: slice((h // 2) * LANES, (h // 2 + 1) * LANES)

    def logits(h, band, s):
        slope = slopes_ref[group * DIL_HEADS_PER_GROUP + h]
        return jnp.where(masks[band], s - slope * dist, NEG_INF)

    def store_pair(pair, band, o, lse):
        rows, cols = _band_rows(band, DIL_BAND), slice(pair * LANES, (pair + 1) * LANES)
        o_ref[0, rows, cols] = o
        lse_ref[0, rows, cols] = lse

    _band_attention(
        DIL_HEADS_PER_GROUP,
        lambda h, band: q_ref[0, _band_rows(band, DIL_BAND), tile(h)],
        lambda h, band: _band_keys(kp_ref, kc_ref, band, DIL_BAND, tile(h)),
        lambda h, band: _band_keys(vp_ref, vc_ref, band, DIL_BAND, tile(h)),
        logits, lambda h: None, store_pair, s_ref, with_lse=True)


def _dilated_group(qkv, b, s, group, window, dil):
    width = DIL_HEADS_PER_GROUP * HEAD_DIM
    n_sub = s // dil
    step_rows = BANDS_PER_STEP * DIL_BAND
    assert n_sub % step_rows == 0
    view = qkv.reshape(b, n_sub, dil * 3 * width)

    def col(part):
        return lambda bi, r, n: (bi, n, r * 3 + part)

    def col_prev(part):
        return lambda bi, r, n: (bi, jnp.maximum(n * BANDS_PER_STEP - 1, 0), r * 3 + part)

    blk, prev_blk = (1, step_rows, width), (1, DIL_BAND, width)
    out_spec = pl.BlockSpec(blk, lambda bi, r, n: (bi, n, r))
    out_sds = jax.ShapeDtypeStruct((b, n_sub, dil * width), F32)
    o, lse = pl.pallas_call(
        functools.partial(_dil_kernel, n_pts=window // dil, dil=dil, group=group),
        grid=(b, dil, n_sub // step_rows),
        in_specs=[pl.BlockSpec(memory_space=pltpu.SMEM),
                  pl.BlockSpec(blk, col(0)),
                  pl.BlockSpec(prev_blk, col_prev(1)), pl.BlockSpec(blk, col(1)),
                  pl.BlockSpec(prev_blk, col_prev(2)), pl.BlockSpec(blk, col(2))],
        out_specs=[out_spec, out_spec],
        out_shape=[out_sds, out_sds],
        scratch_shapes=[pltpu.VMEM((BANDS_PER_STEP * DIL_HEADS_PER_GROUP, 2 * DIL_BAND, DIL_BAND), F32)],
        compiler_params=_params(("parallel", "parallel", "arbitrary")),
        name=f"dilated_attention_g{group}",
    )(_alibi_slopes(len(DIL_PAIRS) * DIL_HEADS_PER_GROUP), view, view, view, view, view)
    return o.reshape(b * n_sub, dil * width), lse.reshape(b * n_sub, dil * width)


def _mla_proj_kernel(h_ref, g_ref, wd_ref, gq_ref, gkv_ref, wuq_ref, wuk_ref, wuv_ref,
                     cos_ref, sin_up_ref, sin_dn_ref, q_ref, k_ref, v_ref):
    n = _rms(h_ref[...], g_ref[...]).astype(BF16)
    c = _dot(n, wd_ref[...])
    nq = _rms(c[:, :MLA_Q_RANK], gq_ref[...]).astype(BF16)
    nkv = _rms(c[:, MLA_Q_RANK:MLA_Q_RANK + MLA_KV_RANK], gkv_ref[...]).astype(BF16)
    k_rope = c[:, MLA_Q_RANK + MLA_KV_RANK:]
    cos, sin_up, sin_dn = cos_ref[...], sin_up_ref[...], sin_dn_ref[...]
    half = MLA_ROPE // 2

    def rope(x):
        return x * cos + pltpu.roll(x, half, 1) * sin_up + pltpu.roll(x, LANES - half, 1) * sin_dn

    kr = rope(k_rope)
    for h in range(MLA_HEADS):
        cols = slice(h * LANES, (h + 1) * LANES)
        q_ref[:, cols] = (rope(_dot(nq, wuq_ref[:, cols])) * MLA_Q_SCALE).astype(BF16)
        k_ref[:, cols] = (_dot(nkv, wuk_ref[:, cols]) + kr).astype(BF16)
    v_ref[...] = _dot(nkv, wuv_ref[...]).astype(BF16)


def _mla_proj(h, g, w_dkv, q_norm, w_uq, kv_norm, w_ukv, s):
    m, d = h.shape
    qk = MLA_NOPE + MLA_ROPE
    wd = jnp.concatenate([w_dkv[:, :MLA_Q_RANK + MLA_KV_RANK],
                          jnp.zeros((d, MLA_NOPE), F32), w_dkv[:, MLA_Q_RANK + MLA_KV_RANK:],
                          jnp.zeros((d, LANES - qk), F32)], axis=1).astype(BF16)
    wuq = jnp.pad(w_uq.reshape(MLA_Q_RANK, MLA_HEADS, qk), ((0, 0), (0, 0), (0, LANES - qk)))
    wuq = wuq.reshape(MLA_Q_RANK, MLA_HEADS * LANES).astype(BF16)
    w_ukv = w_ukv.reshape(MLA_KV_RANK, MLA_HEADS, MLA_NOPE + MLA_V)
    wuk = jnp.pad(w_ukv[:, :, :MLA_NOPE], ((0, 0), (0, 0), (0, LANES - MLA_NOPE)))
    wuk = wuk.reshape(MLA_KV_RANK, MLA_HEADS * LANES).astype(BF16)
    wuv = w_ukv[:, :, MLA_NOPE:].reshape(MLA_KV_RANK, MLA_HEADS * MLA_V).astype(BF16)
    half = MLA_ROPE // 2
    inv = ROPE_THETA ** (-jnp.arange(0, MLA_ROPE, 2, dtype=F32) / MLA_ROPE)
    ang = jnp.arange(s).astype(F32)[:, None] * inv[None, :]
    cos, sin = jnp.cos(ang), jnp.sin(ang)
    zeros = lambda w: jnp.zeros((s, w), F32)
    cos_t = jnp.concatenate([jnp.ones((s, MLA_NOPE), F32), cos, cos, zeros(LANES - qk)], axis=1)
    sin_up = jnp.concatenate([zeros(MLA_NOPE + half), sin, zeros(LANES - qk)], axis=1)
    sin_dn = jnp.concatenate([zeros(MLA_NOPE), -sin, zeros(half + LANES - qk)], axis=1)

    tiles_per_seq = s // ROW_TILE
    row = lambda width: pl.BlockSpec((ROW_TILE, width), lambda i: (i, 0))
    table = pl.BlockSpec((ROW_TILE, LANES), lambda i: (i % tiles_per_seq, 0))
    return pl.pallas_call(
        _mla_proj_kernel,
        grid=(m // ROW_TILE,),
        in_specs=[row(d), _resident((1, d)), _resident(wd.shape),
                  _resident((1, MLA_Q_RANK)), _resident((1, MLA_KV_RANK)),
                  _resident(wuq.shape), _resident(wuk.shape), _resident(wuv.shape),
                  table, table, table],
        out_specs=[row(MLA_HEADS * LANES), row(MLA_HEADS * LANES), row(MLA_HEADS * MLA_V)],
        out_shape=[jax.ShapeDtypeStruct((m, MLA_HEADS * LANES), BF16),
                   jax.ShapeDtypeStruct((m, MLA_HEADS * LANES), BF16),
                   jax.ShapeDtypeStruct((m, MLA_HEADS * MLA_V), BF16)],
        compiler_params=_params(("parallel",)),
        name="mla_proj",
    )(h, g.reshape(1, d), wd, q_norm.reshape(1, -1), kv_norm.reshape(1, -1), wuq, wuk, wuv,
      cos_t, sin_up, sin_dn)


def _mla_kernel(q_ref, k_ref, v_ref, o_ref, vt_ref, qh_ref, s_a, s_b, m_ref, acc_ref, *, n_tiles):
    t = MLA_TILE
    n_heads = MLA_HEADS_PER_STEP
    i = pl.program_id(2)
    state = _SoftmaxState(m_ref, acc_ref, MLA_V)

    @pl.when(i == 0)
    def _():
        for j in range(n_tiles):
            rows = slice(j * t, (j + 1) * t)
            v_t = v_ref[0, rows, :].astype(F32).T
            for hh in range(n_heads):
                state.store_values_t(vt_ref, hh, rows, v_t[hh * MLA_V:(hh + 1) * MLA_V, :])

    state.init()
    key = lax.broadcasted_iota(jnp.int32, (t, t), 0)
    qry = lax.broadcasted_iota(jnp.int32, (t, t), 1)
    lane_tile = lambda hh: slice(hh * LANES, (hh + 1) * LANES)
    for hh in range(n_heads):
        qh_ref[hh] = _transpose_bf16(q_ref[0, :, lane_tile(hh)])

    def rows_of(block):
        return pl.ds(pl.multiple_of(block * t, t), t)

    def scores(hh, block):
        return _dot(k_ref[0, rows_of(block), lane_tile(hh)], qh_ref[hh])

    def v_t(hh, block):
        return vt_ref[hh, :, rows_of(block)]

    def own_logits(hh, s):
        return jnp.where(key <= qry, s, NEG_INF), None

    def past_logits(hh, s, block):
        return s, None

    _sweep_blocks(i, n_heads, state, scores, v_t, own_logits, past_logits, s_a, s_b)
    o_t = jnp.concatenate([state.normalized(hh) for hh in range(n_heads)], axis=0)
    o_ref[0] = o_t.T.astype(o_ref.dtype)


def _mla_attention(q, k, v, b, s):
    n_heads = MLA_HEADS_PER_STEP
    n_groups = MLA_HEADS // n_heads
    whole_seq = lambda bi, g, i: (bi, 0, g)
    tile = lambda bi, g, i: (bi, i, g)
    score_buf = pltpu.VMEM((n_heads, MLA_TILE, MLA_TILE), F32)
    return pl.pallas_call(
        functools.partial(_mla_kernel, n_tiles=s // MLA_TILE),
        grid=(b, n_groups, s // MLA_TILE),
        in_specs=[pl.BlockSpec((1, MLA_TILE, n_heads * LANES), tile),
                  pl.BlockSpec((1, s, n_heads * LANES), whole_seq),
                  pl.BlockSpec((1, s, n_heads * MLA_V), whole_seq)],
        out_specs=pl.BlockSpec((1, MLA_TILE, n_heads * MLA_V), tile),
        out_shape=jax.ShapeDtypeStruct((b, s, MLA_HEADS * MLA_V), BF16),
        scratch_shapes=[pltpu.VMEM((n_heads, MLA_V + ONES_ROWS, s), BF16),
                        pltpu.VMEM((n_heads, LANES, MLA_TILE), BF16),
                        score_buf, score_buf,
                        *_SoftmaxState.scratch(n_heads, MLA_V, MLA_TILE)],
        compiler_params=_params(("parallel", "parallel", "arbitrary")),
        name="mla_attention",
    )(q, k, v)


def _swa_kernel(slopes_ref, sinks_ref, q_ref, kp_ref, kc_ref, vp_ref, vc_ref, o_ref, s_ref):
    n = pl.program_id(1)
    group = SWA_Q_HEADS // SWA_KV_HEADS
    diff, in_sequence = _band_geometry(SWA_BLOCK, n)
    window = (diff >= 0) & (diff < SWA_WINDOW)
    masks = [window & in_sequence] + [window] * (BANDS_PER_STEP - 1)
    dist = diff.astype(F32) * LOG2E

    def both_halves(prev_ref, cur_ref, band, kv_head):
        x = _band_keys(prev_ref, cur_ref, band, SWA_BLOCK, slice(None))
        part = x[:, kv_head * HEAD_DIM:(kv_head + 1) * HEAD_DIM]
        return jnp.concatenate([part, part], axis=1)

    bands_kv = [(band, kv) for band in range(BANDS_PER_STEP) for kv in range(SWA_KV_HEADS)]
    k_tiles = {bk: both_halves(kp_ref, kc_ref, *bk) for bk in bands_kv}
    v_tiles = {bk: both_halves(vp_ref, vc_ref, *bk) for bk in bands_kv}

    def logits(h, band, s):
        return jnp.where(masks[band], s - slopes_ref[h] * dist, NEG_INF)

    def sink(h):
        return sinks_ref[h] * LOG2E

    def store_pair(pair, band, o, lse):
        o_ref[0, _band_rows(band, SWA_BLOCK), pair * LANES:(pair + 1) * LANES] = o.astype(o_ref.dtype)

    _band_attention(
        SWA_Q_HEADS,
        lambda h, band: q_ref[0, _band_rows(band, SWA_BLOCK), (h // 2) * LANES:(h // 2 + 1) * LANES],
        lambda h, band: k_tiles[band, h // group],
        lambda h, band: v_tiles[band, h // group],
        logits, sink, store_pair, s_ref, with_lse=False)


def _swa_attention(qkv, sinks, b, s):
    q_width = SWA_Q_HEADS * HEAD_DIM
    kv_width = SWA_KV_HEADS * HEAD_DIM
    k_tile = q_width // kv_width
    step_rows = BANDS_PER_STEP * SWA_BLOCK
    assert s % step_rows == 0
    cur = lambda t: (lambda bi, n: (bi, n, t))
    prev = lambda t: (lambda bi, n: (bi, jnp.maximum(n * BANDS_PER_STEP - 1, 0), t))
    kv_blk, kv_prev_blk = (1, step_rows, kv_width), (1, SWA_BLOCK, kv_width)
    return pl.pallas_call(
        _swa_kernel,
        grid=(b, s // step_rows),
        in_specs=[pl.BlockSpec(memory_space=pltpu.SMEM), pl.BlockSpec(memory_space=pltpu.SMEM),
                  pl.BlockSpec((1, step_rows, q_width), cur(0)),
                  pl.BlockSpec(kv_prev_blk, prev(k_tile)), pl.BlockSpec(kv_blk, cur(k_tile)),
                  pl.BlockSpec(kv_prev_blk, prev(k_tile + 1)), pl.BlockSpec(kv_blk, cur(k_tile + 1))],
        out_specs=pl.BlockSpec((1, step_rows, q_width), cur(0)),
        out_shape=jax.ShapeDtypeStruct((b, s, q_width), BF16),
        scratch_shapes=[pltpu.VMEM((BANDS_PER_STEP * SWA_Q_HEADS, 2 * SWA_BLOCK, SWA_BLOCK), F32)],
        compiler_params=_params(("parallel", "arbitrary")),
        name="swa_attention",
    )(_alibi_slopes(SWA_Q_HEADS), sinks.astype(F32), qkv, qkv, qkv, qkv, qkv)


def kernel(x, l0_attn_norm, l0_w_qkv, l0_w_o, l0_mlp_norm, l0_w_up, l0_w_down, l1_attn_norm, l1_w_qkv, l1_w_o, l1_mlp_norm, l1_w_up, l1_w_down, l2_attn_norm, l2_w_dkv, l2_q_norm, l2_w_uq, l2_kv_norm, l2_w_ukv, l2_w_o, l2_mlp_norm, l2_w_up, l2_w_down, l3_attn_norm, l3_w_qkv, l3_sinks, l3_w_o, l3_mlp_norm, l3_w_up, l3_w_down, final_norm):
    b, s, d = x.shape
    bf = lambda w: w.astype(BF16)
    h = x.reshape(b * s, d)

    qkv = _norm_proj(h, l0_attn_norm, bf(l0_w_qkv), scaled_cols=MOBA_HEADS * HEAD_DIM, scale=HEAD_Q_SCALE)
    a = _moba_attention(qkv.reshape(b, s, -1), b, s).reshape(b * s, -1)
    h = _post(h, (a,), bf(l0_w_o), l0_mlp_norm, bf(l0_w_up), bf(l0_w_down))

    qkvs = _dil_proj(h, l1_attn_norm, bf(l1_w_qkv))
    groups = [_dilated_group(qkvs[g], b, s, g, window, dil) for g, (window, dil) in enumerate(DIL_PAIRS)]
    h = _post(h, tuple(o for o, _ in groups) + tuple(l for _, l in groups),
              bf(l1_w_o), l1_mlp_norm, bf(l1_w_up), bf(l1_w_down))

    q, k, v = _mla_proj(h, l2_attn_norm, l2_w_dkv, l2_q_norm, l2_w_uq, l2_kv_norm, l2_w_ukv, s)
    a = _mla_attention(q.reshape(b, s, -1), k.reshape(b, s, -1), v.reshape(b, s, -1), b, s).reshape(b * s, -1)
    h = _post(h, (a,), bf(l2_w_o), l2_mlp_norm, bf(l2_w_up), bf(l2_w_down))

    qkv = _norm_proj(h, l3_attn_norm, bf(l3_w_qkv), col_tile=256, scaled_cols=SWA_Q_HEADS * HEAD_DIM,
                     scale=HEAD_Q_SCALE)
    a = _swa_attention(qkv.reshape(b, s, -1), l3_sinks, b, s).reshape(b * s, -1)
    h = _post(h, (a,), bf(l3_w_o), l3_mlp_norm, bf(l3_w_up), bf(l3_w_down), g_final=final_norm)
    return h.reshape(b, s, d)
```

```python
import functools

import jax
import jax.numpy as jnp
from jax import lax
from jax.experimental import pallas as pl
from jax.experimental.pallas import tpu as pltpu

F32 = jnp.float32
BF16 = jnp.bfloat16

D_MODEL = 1024
HEAD_DIM = 64
RMS_EPS = 1e-6
D_FF = 4 * D_MODEL
NEG_INF = -1e30

MOBA_HEADS = 16
MOBA_BLOCK = 256
MOBA_TOPK = 3

DIL_PAIRS = ((128, 1), (512, 4), (2048, 16))
DIL_HEADS_PER_GROUP = 8
DIL_BAND = 128

MLA_HEADS = 16
MLA_Q_RANK = 768
MLA_KV_RANK = 256
MLA_NOPE = 64
MLA_ROPE = 32
MLA_V = 64
ROPE_THETA = 10000.0
MLA_TILE = 256
MLA_HEADS_PER_STEP = 4
MOBA_HEADS_PER_STEP = 4

SWA_Q_HEADS = 16
SWA_KV_HEADS = 2
SWA_WINDOW = 128
SWA_BLOCK = 128

LANES = 128
ROW_TILE = 512
FF_TILE = 512
VMEM_LIMIT = 56 * 1024 * 1024
LOG2E = 1.4426950408889634
HEAD_Q_SCALE = HEAD_DIM ** -0.5 * LOG2E
MLA_Q_SCALE = (MLA_NOPE + MLA_ROPE) ** -0.5 * LOG2E


def _params(semantics):
    return pltpu.CompilerParams(dimension_semantics=semantics, vmem_limit_bytes=VMEM_LIMIT)


def _resident(shape):
    return pl.BlockSpec(shape, lambda *_: (0,) * len(shape), pipeline_mode=pl.Buffered(1))


def _alibi_slopes(n_heads):
    return 2.0 ** (-8.0 * jnp.arange(1, n_heads + 1, dtype=F32) / n_heads)


def _rms(x, g):
    return x * lax.rsqrt(jnp.mean(x * x, axis=-1, keepdims=True) + RMS_EPS) * g


def _dot(a, b):
    return jnp.dot(a, b, preferred_element_type=F32)


def _dot_nt(a, b):
    return lax.dot_general(a, b, (((1,), (1,)), ((), ())), preferred_element_type=F32)


def _norm_proj_kernel(h_ref, g_ref, w_ref, o_ref, *, col_tile, scaled_cols, scale):
    n = _rms(h_ref[...], g_ref[...]).astype(BF16)
    for c in range(o_ref.shape[1] // col_tile):
        cols = slice(c * col_tile, (c + 1) * col_tile)
        y = _dot(n, w_ref[:, cols])
        if (c + 1) * col_tile <= scaled_cols:
            y = y * scale
        o_ref[:, cols] = y.astype(o_ref.dtype)


def _norm_proj(h, g, w, col_tile=512, scaled_cols=0, scale=1.0):
    m, d = h.shape
    n_out = w.shape[1]
    assert scaled_cols % col_tile == 0
    return pl.pallas_call(
        functools.partial(_norm_proj_kernel, col_tile=col_tile, scaled_cols=scaled_cols, scale=scale),
        grid=(m // ROW_TILE,),
        in_specs=[pl.BlockSpec((ROW_TILE, d), lambda i: (i, 0)),
                  _resident((1, d)),
                  _resident((d, n_out))],
        out_specs=pl.BlockSpec((ROW_TILE, n_out), lambda i: (i, 0)),
        out_shape=jax.ShapeDtypeStruct((m, n_out), BF16),
        compiler_params=_params(("parallel",)),
        name="norm_proj",
    )(h, g.reshape(1, d), w)


def _dil_proj_kernel(h_ref, g_ref, w_ref, o0_ref, o1_ref, o2_ref, stage_ref):
    n = _rms(h_ref[...], g_ref[...]).astype(BF16)
    rows = h_ref.shape[0]
    width = DIL_HEADS_PER_GROUP * HEAD_DIM
    for g, (o_ref, (_, dil)) in enumerate(zip((o0_ref, o1_ref, o2_ref), DIL_PAIRS)):
        for part in range(3):
            src = (3 * g + part) * width
            y = _dot(n, w_ref[:, src:src + width])
            if part == 0:
                y = y * HEAD_Q_SCALE
            if dil == 1:
                o_ref[:, part * width:(part + 1) * width] = y.astype(BF16)
                continue
            for t in range(width // LANES):
                stage_ref[g - 1, part, t] = y[:, t * LANES:(t + 1) * LANES]
            for r in range(dil):
                for t in range(width // LANES):
                    dst = (3 * r + part) * width + t * LANES
                    strided = stage_ref[g - 1, part, t, pl.ds(r, rows // dil, stride=dil), :]
                    o_ref[:, dst:dst + LANES] = strided.astype(BF16)


def _dil_proj(h, g, w):
    m, d = h.shape
    feat = 3 * DIL_HEADS_PER_GROUP * HEAD_DIM
    assert DIL_PAIRS[0][1] == 1
    view = lambda dil: (m // dil, dil * feat)
    view_block = lambda dil: pl.BlockSpec((ROW_TILE // dil, dil * feat), lambda i: (i, 0))
    return pl.pallas_call(
        _dil_proj_kernel,
        grid=(m // ROW_TILE,),
        in_specs=[pl.BlockSpec((ROW_TILE, d), lambda i: (i, 0)), _resident((1, d)), _resident(w.shape)],
        out_specs=[view_block(dil) for _, dil in DIL_PAIRS],
        out_shape=[jax.ShapeDtypeStruct(view(dil), BF16) for _, dil in DIL_PAIRS],
        scratch_shapes=[pltpu.VMEM((len(DIL_PAIRS) - 1, 3, feat // 3 // LANES, ROW_TILE, LANES), F32)],
        compiler_params=_params(("parallel",)),
        name="dil_proj",
    )(h, g.reshape(1, d), w)


def _mlp_tail(h1, g_ref, wup_ref, wdn_ref, gf_ref, out_ref):
    n = _rms(h1, g_ref[...]).astype(BF16)
    acc = jnp.zeros_like(h1)
    for c in range(wup_ref.shape[1] // FF_TILE):
        cols = slice(c * FF_TILE, (c + 1) * FF_TILE)
        u = jnp.square(jnp.maximum(_dot(n, wup_ref[:, cols]), 0.0)).astype(BF16)
        acc = acc + _dot(u, wdn_ref[cols, :])
    out = h1 + acc
    if gf_ref is not None:
        out = _rms(out, gf_ref[...])
    out_ref[...] = out


def _post_kernel(h_ref, a_ref, wo_ref, g_ref, wup_ref, wdn_ref, *rest):
    gf_ref, out_ref = (rest[0], rest[1]) if len(rest) == 2 else (None, rest[0])
    h1 = h_ref[...] + _dot(a_ref[...], wo_ref[...])
    _mlp_tail(h1, g_ref, wup_ref, wdn_ref, gf_ref, out_ref)


def _post_merge_kernel(h_ref, o0_ref, o1_ref, o2_ref, l0_ref, l1_ref, l2_ref,
                       wo_ref, g_ref, wup_ref, wdn_ref, out_ref, stage_ref):
    rows, width = o0_ref.shape

    def token_rows(x_ref, slot, dil):
        if dil == 1:
            return x_ref[...]
        for r in range(dil):
            for t in range(width // LANES):
                src = r * width + t * LANES
                stage_ref[slot, t, pl.ds(r, rows // dil, stride=dil), :] = x_ref[:, src:src + LANES]
        return jnp.concatenate([stage_ref[slot, t] for t in range(width // LANES)], axis=1)

    dils = [dil for _, dil in DIL_PAIRS]
    l0, l1, l2 = (token_rows(ref, slot, dil) for slot, (ref, dil) in enumerate(zip((l0_ref, l1_ref, l2_ref), dils)))
    o0, o1, o2 = (token_rows(ref, 3 + slot, dil) for slot, (ref, dil) in enumerate(zip((o0_ref, o1_ref, o2_ref), dils)))
    mx = jnp.maximum(jnp.maximum(l0, l1), l2)
    e0, e1, e2 = jnp.exp(l0 - mx), jnp.exp(l1 - mx), jnp.exp(l2 - mx)
    merged = (e0 * o0 + e1 * o1 + e2 * o2) / (e0 + e1 + e2)
    h1 = h_ref[...] + _dot(merged.astype(BF16), wo_ref[...])
    _mlp_tail(h1, g_ref, wup_ref, wdn_ref, None, out_ref)


def _post(h, attn_ins, w_o, g, w_up, w_down, g_final=None):
    m, d = h.shape
    row = lambda width: pl.BlockSpec((ROW_TILE, width), lambda i: (i, 0))
    merge = len(attn_ins) > 1
    in_specs = [row(d)] + [pl.BlockSpec((ROW_TILE * a.shape[0] // m, a.shape[1]), lambda i: (i, 0)) for a in attn_ins]
    in_specs += [_resident(w_o.shape), _resident((1, d)), _resident(w_up.shape), _resident(w_down.shape)]
    args = [h, *attn_ins, w_o, g.reshape(1, d), w_up, w_down]
    if g_final is not None:
        in_specs.append(_resident((1, d)))
        args.append(g_final.reshape(1, d))
    scratch = [pltpu.VMEM((len(attn_ins), w_o.shape[0] // LANES, ROW_TILE, LANES), F32)] if merge else []
    return pl.pallas_call(
        _post_merge_kernel if merge else _post_kernel,
        grid=(m // ROW_TILE,),
        in_specs=in_specs,
        out_specs=row(d),
        out_shape=jax.ShapeDtypeStruct((m, d), F32),
        scratch_shapes=scratch,
        compiler_params=_params(("parallel",)),
        name="post_mlp",
    )(*args)


ONES_ROWS = 16


class _SoftmaxState:
    def __init__(self, m_ref, acc_ref, dv):
        self.m_ref, self.acc_ref, self.dv = m_ref, acc_ref, dv

    @staticmethod
    def scratch(n_heads, dv, tq):
        return [pltpu.VMEM((n_heads, 1, tq), F32), pltpu.VMEM((n_heads, dv + ONES_ROWS, tq), F32)]

    @staticmethod
    def store_values_t(vt_ref, hh, cols, v_t):
        dv = v_t.shape[0]
        vt_ref[hh, :dv, cols] = v_t.astype(BF16)
        vt_ref[hh, dv:, cols] = jnp.ones((ONES_ROWS, v_t.shape[1]), BF16)

    def init(self):
        self.m_ref[...] = jnp.full(self.m_ref.shape, NEG_INF, F32)
        self.acc_ref[...] = jnp.zeros(self.acc_ref.shape, F32)

    def step(self, hh, x, v_t, shift=None):
        m = self.m_ref[hh]
        m_cur = jnp.max(x, axis=0, keepdims=True)
        if shift is not None:
            m_cur = m_cur - shift
        m_new = jnp.maximum(m, m_cur)
        alpha = jnp.exp2(m - m_new)
        p = jnp.exp2(x - (m_new if shift is None else m_new + shift))
        self.m_ref[hh] = m_new
        self.acc_ref[hh] = alpha * self.acc_ref[hh] + _dot(v_t, p.astype(BF16))

    def normalized(self, hh):
        return self.acc_ref[hh, :self.dv, :] / self.acc_ref[hh, self.dv:self.dv + 1, :]


def _sweep_blocks(n_past, n_heads, state, score_fn, values_fn, own_logits, past_logits, s_a, s_b):
    def produce(buf, block):
        for hh in range(n_heads):
            buf[hh] = score_fn(hh, block)

    def consume(buf, logits_fn, block):
        for hh in range(n_heads):
            x, shift = logits_fn(hh, buf[hh])
            state.step(hh, x, values_fn(hh, block), shift)

    produce(s_b, n_past)
    produce(s_a, 0)
    consume(s_b, own_logits, n_past)

    def two_blocks(first):
        produce(s_b, first + 1)
        consume(s_a, functools.partial(past_logits, block=first), first)
        produce(s_a, jnp.minimum(first + 2, n_past - 1))
        consume(s_b, functools.partial(past_logits, block=first + 1), first + 1)

    def quad(t, carry):
        two_blocks(4 * t)
        two_blocks(4 * t + 2)
        return carry

    def pair(t, carry):
        two_blocks(4 * (n_past // 4) + 2 * t)
        return carry

    lax.fori_loop(0, n_past // 4, quad, 0)
    lax.fori_loop(0, (n_past % 4) // 2, pair, 0)

    @pl.when(n_past % 2 == 1)
    def _():
        consume(s_a, functools.partial(past_logits, block=n_past - 1), n_past - 1)


def _transpose_bf16(x):
    return x.astype(F32).T.astype(BF16)


def _keep_head_rows(x_t, half):
    r = lax.broadcasted_iota(jnp.int32, x_t.shape, 0)
    keep = (r < HEAD_DIM) if half == 0 else (r >= HEAD_DIM)
    return jnp.where(keep, x_t, jnp.zeros_like(x_t))


def _moba_kernel(slopes_ref, q_ref, k_ref, v_ref, o_ref, kmean_ref, vt_ref, term_ref, alibi_ref, qh_ref,
                 s_a, s_b, m_ref, acc_ref, *, n_blocks):
    blk = MOBA_BLOCK
    n_heads = MOBA_HEADS_PER_STEP
    group = pl.program_id(1)
    i = pl.program_id(2)
    state = _SoftmaxState(m_ref, acc_ref, HEAD_DIM)

    @pl.when(i == 0)
    def _():
        for j in range(n_blocks):
            rows = slice(j * blk, (j + 1) * blk)
            kmean_ref[j:j + 1, :] = jnp.mean(k_ref[0, rows, :].astype(F32), axis=0, keepdims=True)
            v_t = v_ref[0, rows, :].astype(F32).T
            for hh in range(n_heads):
                state.store_values_t(vt_ref, hh, rows, v_t[hh * HEAD_DIM:(hh + 1) * HEAD_DIM, :])

    state.init()
    q_t = q_ref[0].astype(F32).T
    key = lax.broadcasted_iota(jnp.int32, (blk, blk), 0)
    qry = lax.broadcasted_iota(jnp.int32, (blk, blk), 1)
    rel = (qry - key).astype(F32)
    blk_idx = lax.broadcasted_iota(jnp.int32, (n_blocks, blk), 0)
    fully_past = blk_idx < i
    block_dist = ((i - blk_idx) * blk).astype(F32)
    lane_tile = lambda hh: slice((hh // 2) * LANES, (hh // 2 + 1) * LANES)

    for hh in range(n_heads):
        slope_s = slopes_ref[n_heads * group + hh] * LOG2E
        q_h = _keep_head_rows(q_t[lane_tile(hh), :], hh % 2)
        gate = lax.dot_general(kmean_ref[:, lane_tile(hh)], q_h, (((1,), (0,)), ((), ())),
                               precision=lax.Precision.HIGHEST, preferred_element_type=F32)
        work = jnp.where(fully_past, gate, NEG_INF * HEAD_Q_SCALE)
        chosen = jnp.zeros((n_blocks, blk), jnp.bool_)
        for _ in range(min(MOBA_TOPK, n_blocks)):
            best = jnp.max(work, axis=0, keepdims=True)
            first = jnp.min(jnp.where(work == best, blk_idx, n_blocks), axis=0, keepdims=True)
            pick = blk_idx == first
            chosen = jnp.logical_or(chosen, pick)
            work = jnp.where(pick, -jnp.inf, work)
        term_ref[hh] = jnp.where(jnp.logical_and(chosen, fully_past), slope_s * block_dist, -NEG_INF)
        qh_ref[hh] = q_h.astype(BF16)
        alibi_ref[hh] = slope_s * rel

    def rows_of(block):
        return pl.ds(pl.multiple_of(block * blk, blk), blk)

    def scores(hh, block):
        return _dot(k_ref[0, rows_of(block), lane_tile(hh)], qh_ref[hh])

    def v_t(hh, block):
        return vt_ref[hh, :, rows_of(block)]

    def own_logits(hh, s):
        return jnp.where(rel >= 0, s - alibi_ref[hh], NEG_INF), None

    def past_logits(hh, s, block):
        return s - alibi_ref[hh], term_ref[hh, pl.ds(block, 1), :]

    _sweep_blocks(i, n_heads, state, scores, v_t, own_logits, past_logits, s_a, s_b)
    o_t = jnp.concatenate([state.normalized(hh) for hh in range(n_heads)], axis=0)
    o_ref[0] = o_t.T.astype(o_ref.dtype)


def _moba_attention(qkv, b, s):
    n_blocks = s // MOBA_BLOCK
    n_heads = MOBA_HEADS_PER_STEP
    width = n_heads * HEAD_DIM
    n_groups = MOBA_HEADS // n_heads
    score_buf = pltpu.VMEM((n_heads, MOBA_BLOCK, MOBA_BLOCK), F32)
    return pl.pallas_call(
        functools.partial(_moba_kernel, n_blocks=n_blocks),
        grid=(b, n_groups, n_blocks),
        in_specs=[pl.BlockSpec(memory_space=pltpu.SMEM),
                  pl.BlockSpec((1, MOBA_BLOCK, width), lambda bi, g, i: (bi, i, g)),
                  pl.BlockSpec((1, s, width), lambda bi, g, i: (bi, 0, n_groups + g)),
                  pl.BlockSpec((1, s, width), lambda bi, g, i: (bi, 0, 2 * n_groups + g))],
        out_specs=pl.BlockSpec((1, MOBA_BLOCK, width), lambda bi, g, i: (bi, i, g)),
        out_shape=jax.ShapeDtypeStruct((b, s, MOBA_HEADS * HEAD_DIM), BF16),
        scratch_shapes=[pltpu.VMEM((n_blocks, width), F32),
                        pltpu.VMEM((n_heads, HEAD_DIM + ONES_ROWS, s), BF16),
                        pltpu.VMEM((n_heads, n_blocks, MOBA_BLOCK), F32),
                        pltpu.VMEM((n_heads, MOBA_BLOCK, MOBA_BLOCK), F32),
                        pltpu.VMEM((n_heads, LANES, MOBA_BLOCK), BF16),
                        score_buf, score_buf,
                        *_SoftmaxState.scratch(n_heads, HEAD_DIM, MOBA_BLOCK)],
        compiler_params=_params(("parallel", "parallel", "arbitrary")),
        name="moba_attention",
    )(_alibi_slopes(MOBA_HEADS), qkv, qkv, qkv)


def _lane_half(x, half):
    lane = lax.broadcasted_iota(jnp.int32, x.shape, x.ndim - 1)
    keep = (lane < HEAD_DIM) if half == 0 else (lane >= HEAD_DIM)
    return jnp.where(keep, x, jnp.zeros_like(x))


BANDS_PER_STEP = 2


def _band_attention(n_heads, q_tile, k_tile, v_tile, logits_fn, floor_fn, store_pair, s_ref, with_lse):
    chains = [(band, h) for band in range(BANDS_PER_STEP) for h in range(n_heads)]
    for c, (band, h) in enumerate(chains):
        s_ref[c] = _dot_nt(k_tile(h, band), _lane_half(q_tile(h, band), h % 2))
    even = None
    for c, (band, h) in enumerate(chains):
        x = logits_fn(h, band, s_ref[c])
        m = jnp.max(x, axis=0, keepdims=True)
        floor = floor_fn(h)
        if floor is not None:
            m = jnp.maximum(m, floor)
        p = jnp.exp2(x - m)
        den = jnp.sum(p, axis=0, keepdims=True)
        if floor is not None:
            den = den + jnp.exp2(floor - m)
        o_t = lax.dot_general(v_tile(h, band), p.astype(BF16), (((0,), (0,)), ((), ())),
                              preferred_element_type=F32)
        half = slice((h % 2) * HEAD_DIM, (h % 2 + 1) * HEAD_DIM)
        o_t = o_t[half, :] / den
        lse_t = jnp.broadcast_to((m + jnp.log2(den)) * (1.0 / LOG2E), o_t.shape) if with_lse else None
        if h % 2 == 0:
            even = (o_t, lse_t)
            continue
        store_pair(h // 2, band, jnp.concatenate([even[0], o_t], axis=0).T,
                   jnp.concatenate([even[1], lse_t], axis=0).T if with_lse else None)


def _band_rows(band, size):
    return slice(band * size, (band + 1) * size)


def _band_keys(prev_ref, cur_ref, band, size, cols):
    if band == 0:
        return jnp.concatenate([prev_ref[0, :, cols], cur_ref[0, :size, cols]], axis=0)
    return cur_ref[0, (band - 1) * size:(band + 1) * size, cols]


def _band_geometry(size, n):
    key = lax.broadcasted_iota(jnp.int32, (2 * size, size), 0)
    qry = lax.broadcasted_iota(jnp.int32, (2 * size, size), 1)
    diff = qry + size - key
    in_sequence = (n > 0) | (key >= size)
    return diff, in_sequence


def _dil_kernel(slopes_ref, q_ref, kp_ref, kc_ref, vp_ref, vc_ref, o_ref, lse_ref, s_ref, *, n_pts, dil, group):
    n = pl.program_id(2)
    diff, in_sequence = _band_geometry(DIL_BAND, n)
    window = (diff >= 0) & (diff <= n_pts)
    masks = [window & in_sequence] + [window] * (BANDS_PER_STEP - 1)
    dist = (dil * diff).astype(F32) * LOG2E
    tile = lambda h: slice((h // 2) * LANES, (h // 2 + 1) * LANES)

    def logits(h, band, s):
        slope = slopes_ref[group * DIL_HEADS_PER_GROUP + h]
        return jnp.where(masks[band], s - slope * dist, NEG_INF)

    def store_pair(pair, band, o, lse):
        rows, cols = _band_rows(band, DIL_BAND), slice(pair * LANES, (pair + 1) * LANES)
        o_ref[0, rows, cols] = o
        lse_ref[0, rows, cols] = lse

    _band_attention(
        DIL_HEADS_PER_GROUP,
        lambda h, band: q_ref[0, _band_rows(band, DIL_BAND), tile(h)],
        lambda h, band: _band_keys(kp_ref, kc_ref, band, DIL_BAND, tile(h)),
        lambda h, band: _band_keys(vp_ref, vc_ref, band, DIL_BAND, tile(h)),
        logits, lambda h: None, store_pair, s_ref, with_lse=True)


def _dilated_group(qkv, b, s, group, window, dil):
    width = DIL_HEADS_PER_GROUP * HEAD_DIM
    n_sub = s // dil
    step_rows = BANDS_PER_STEP * DIL_BAND
    assert n_sub % step_rows == 0
    view = qkv.reshape(b, n_sub, dil * 3 * width)

    def col(part):
        return lambda bi, r, n: (bi, n, r * 3 + part)

    def col_prev(part):
        return lambda bi, r, n: (bi, jnp.maximum(n * BANDS_PER_STEP - 1, 0), r * 3 + part)

    blk, prev_blk = (1, step_rows, width), (1, DIL_BAND, width)
    out_spec = pl.BlockSpec(blk, lambda bi, r, n: (bi, n, r))
    out_sds = jax.ShapeDtypeStruct((b, n_sub, dil * width), F32)
    o, lse = pl.pallas_call(
        functools.partial(_dil_kernel, n_pts=window // dil, dil=dil, group=group),
        grid=(b, dil, n_sub // step_rows),
        in_specs=[pl.BlockSpec(memory_space=pltpu.SMEM),
                  pl.BlockSpec(blk, col(0)),
                  pl.BlockSpec(prev_blk, col_prev(1)), pl.BlockSpec(blk, col(1)),
                  pl.BlockSpec(prev_blk, col_prev(2)), pl.BlockSpec(blk, col(2))],
        out_specs=[out_spec, out_spec],
        out_shape=[out_sds, out_sds],
        scratch_shapes=[pltpu.VMEM((BANDS_PER_STEP * DIL_HEADS_PER_GROUP, 2 * DIL_BAND, DIL_BAND), F32)],
        compiler_params=_params(("parallel", "parallel", "arbitrary")),
        name=f"dilated_attention_g{group}",
    )(_alibi_slopes(len(DIL_PAIRS) * DIL_HEADS_PER_GROUP), view, view, view, view, view)
    return o.reshape(b * n_sub, dil * width), lse.reshape(b * n_sub, dil * width)


def _mla_proj_kernel(h_ref, g_ref, wd_ref, gq_ref, gkv_ref, wuq_ref, wuk_ref, wuv_ref,
                     cos_ref, sin_up_ref, sin_dn_ref, q_ref, k_ref, v_ref):
    n = _rms(h_ref[...], g_ref[...]).astype(BF16)
    c = _dot(n, wd_ref[...])
    nq = _rms(c[:, :MLA_Q_RANK], gq_ref[...]).astype(BF16)
    nkv = _rms(c[:, MLA_Q_RANK:MLA_Q_RANK + MLA_KV_RANK], gkv_ref[...]).astype(BF16)
    k_rope = c[:, MLA_Q_RANK + MLA_KV_RANK:]
    half = MLA_ROPE // 2
    pair = 2

    def rope(x, width):
        cos, sin_up, sin_dn = (jnp.tile(t[...], (1, width)) for t in (cos_ref, sin_up_ref, sin_dn_ref))
        return x * cos + pltpu.roll(x, half, 1) * sin_up + pltpu.roll(x, x.shape[1] - half, 1) * sin_dn

    kr = jnp.tile(rope(k_rope, 1), (1, pair))
    for h in range(0, MLA_HEADS, pair):
        cols = slice(h * LANES, (h + pair) * LANES)
        q_ref[:, cols] = (rope(_dot(nq, wuq_ref[:, cols]), pair) * MLA_Q_SCALE).astype(BF16)
        k_ref[:, cols] = (_dot(nkv, wuk_ref[:, cols]) + kr).astype(BF16)
    v_ref[...] = _dot(nkv, wuv_ref[...]).astype(BF16)


def _mla_proj(h, g, w_dkv, q_norm, w_uq, kv_norm, w_ukv, s):
    m, d = h.shape
    qk = MLA_NOPE + MLA_ROPE
    wd = jnp.concatenate([w_dkv[:, :MLA_Q_RANK + MLA_KV_RANK],
                          jnp.zeros((d, MLA_NOPE), F32), w_dkv[:, MLA_Q_RANK + MLA_KV_RANK:],
                          jnp.zeros((d, LANES - qk), F32)], axis=1).astype(BF16)
    wuq = jnp.pad(w_uq.reshape(MLA_Q_RANK, MLA_HEADS, qk), ((0, 0), (0, 0), (0, LANES - qk)))
    wuq = wuq.reshape(MLA_Q_RANK, MLA_HEADS * LANES).astype(BF16)
    w_ukv = w_ukv.reshape(MLA_KV_RANK, MLA_HEADS, MLA_NOPE + MLA_V)
    wuk = jnp.pad(w_ukv[:, :, :MLA_NOPE], ((0, 0), (0, 0), (0, LANES - MLA_NOPE)))
    wuk = wuk.reshape(MLA_KV_RANK, MLA_HEADS * LANES).astype(BF16)
    wuv = w_ukv[:, :, MLA_NOPE:].reshape(MLA_KV_RANK, MLA_HEADS * MLA_V).astype(BF16)
    half = MLA_ROPE // 2
    inv = ROPE_THETA ** (-jnp.arange(0, MLA_ROPE, 2, dtype=F32) / MLA_ROPE)
    ang = jnp.arange(s).astype(F32)[:, None] * inv[None, :]
    cos, sin = jnp.cos(ang), jnp.sin(ang)
    zeros = lambda w: jnp.zeros((s, w), F32)
    cos_t = jnp.concatenate([jnp.ones((s, MLA_NOPE), F32), cos, cos, zeros(LANES - qk)], axis=1)
    sin_up = jnp.concatenate([zeros(MLA_NOPE + half), sin, zeros(LANES - qk)], axis=1)
    sin_dn = jnp.concatenate([zeros(MLA_NOPE), -sin, zeros(half + LANES - qk)], axis=1)

    tiles_per_seq = s // ROW_TILE
    row = lambda width: pl.BlockSpec((ROW_TILE, width), lambda i: (i, 0))
    table = pl.BlockSpec((ROW_TILE, LANES), lambda i: (i % tiles_per_seq, 0))
    return pl.pallas_call(
        _mla_proj_kernel,
        grid=(m // ROW_TILE,),
        in_specs=[row(d), _resident((1, d)), _resident(wd.shape),
                  _resident((1, MLA_Q_RANK)), _resident((1, MLA_KV_RANK)),
                  _resident(wuq.shape), _resident(wuk.shape), _resident(wuv.shape),
                  table, table, table],
        out_specs=[row(MLA_HEADS * LANES), row(MLA_HEADS * LANES), row(MLA_HEADS * MLA_V)],
        out_shape=[jax.ShapeDtypeStruct((m, MLA_HEADS * LANES), BF16),
                   jax.ShapeDtypeStruct((m, MLA_HEADS * LANES), BF16),
                   jax.ShapeDtypeStruct((m, MLA_HEADS * MLA_V), BF16)],
        compiler_params=_params(("parallel",)),
        name="mla_proj",
    )(h, g.reshape(1, d), wd, q_norm.reshape(1, -1), kv_norm.reshape(1, -1), wuq, wuk, wuv,
      cos_t, sin_up, sin_dn)


def _mla_kernel(q_ref, k_ref, v_ref, o_ref, vt_ref, qh_ref, s_a, s_b, m_ref, acc_ref, *, n_tiles):
    t = MLA_TILE
    n_heads = MLA_HEADS_PER_STEP
    i = pl.program_id(2)
    state = _SoftmaxState(m_ref, acc_ref, MLA_V)

    @pl.when(i == 0)
    def _():
        for j in range(n_tiles):
            rows = slice(j * t, (j + 1) * t)
            v_t = v_ref[0, rows, :].astype(F32).T
            for hh in range(n_heads):
                state.store_values_t(vt_ref, hh, rows, v_t[hh * MLA_V:(hh + 1) * MLA_V, :])

    state.init()
    key = lax.broadcasted_iota(jnp.int32, (t, t), 0)
    qry = lax.broadcasted_iota(jnp.int32, (t, t), 1)
    lane_tile = lambda hh: slice(hh * LANES, (hh + 1) * LANES)
    for hh in range(n_heads):
        qh_ref[hh] = _transpose_bf16(q_ref[0, :, lane_tile(hh)])

    def rows_of(block):
        return pl.ds(pl.multiple_of(block * t, t), t)

    def scores(hh, block):
        return _dot(k_ref[0, rows_of(block), lane_tile(hh)], qh_ref[hh])

    def v_t(hh, block):
        return vt_ref[hh, :, rows_of(block)]

    def own_logits(hh, s):
        return jnp.where(key <= qry, s, NEG_INF), None

    def past_logits(hh, s, block):
        return s, None

    _sweep_blocks(i, n_heads, state, scores, v_t, own_logits, past_logits, s_a, s_b)
    o_t = jnp.concatenate([state.normalized(hh) for hh in range(n_heads)], axis=0)
    o_ref[0] = o_t.T.astype(o_ref.dtype)


def _mla_attention(q, k, v, b, s):
    n_heads = MLA_HEADS_PER_STEP
    n_groups = MLA_HEADS // n_heads
    whole_seq = lambda bi, g, i: (bi, 0, g)
    tile = lambda bi, g, i: (bi, i, g)
    score_buf = pltpu.VMEM((n_heads, MLA_TILE, MLA_TILE), F32)
    return pl.pallas_call(
        functools.partial(_mla_kernel, n_tiles=s // MLA_TILE),
        grid=(b, n_groups, s // MLA_TILE),
        in_specs=[pl.BlockSpec((1, MLA_TILE, n_heads * LANES), tile),
                  pl.BlockSpec((1, s, n_heads * LANES), whole_seq),
                  pl.BlockSpec((1, s, n_heads * MLA_V), whole_seq)],
        out_specs=pl.BlockSpec((1, MLA_TILE, n_heads * MLA_V), tile),
        out_shape=jax.ShapeDtypeStruct((b, s, MLA_HEADS * MLA_V), BF16),
        scratch_shapes=[pltpu.VMEM((n_heads, MLA_V + ONES_ROWS, s), BF16),
                        pltpu.VMEM((n_heads, LANES, MLA_TILE), BF16),
                        score_buf, score_buf,
                        *_SoftmaxState.scratch(n_heads, MLA_V, MLA_TILE)],
        compiler_params=_params(("parallel", "parallel", "arbitrary")),
        name="mla_attention",
    )(q, k, v)


def _swa_kernel(slopes_ref, sinks_ref, q_ref, kp_ref, kc_ref, vp_ref, vc_ref, o_ref, s_ref):
    n = pl.program_id(1)
    group = SWA_Q_HEADS // SWA_KV_HEADS
    diff, in_sequence = _band_geometry(SWA_BLOCK, n)
    window = (diff >= 0) & (diff < SWA_WINDOW)
    masks = [window & in_sequence] + [window] * (BANDS_PER_STEP - 1)
    dist = diff.astype(F32) * LOG2E

    def both_halves(prev_ref, cur_ref, band, kv_head):
        x = _band_keys(prev_ref, cur_ref, band, SWA_BLOCK, slice(None))
        part = x[:, kv_head * HEAD_DIM:(kv_head + 1) * HEAD_DIM]
        return jnp.concatenate([part, part], axis=1)

    bands_kv = [(band, kv) for band in range(BANDS_PER_STEP) for kv in range(SWA_KV_HEADS)]
    k_tiles = {bk: both_halves(kp_ref, kc_ref, *bk) for bk in bands_kv}
    v_tiles = {bk: both_halves(vp_ref, vc_ref, *bk) for bk in bands_kv}

    def logits(h, band, s):
        return jnp.where(masks[band], s - slopes_ref[h] * dist, NEG_INF)

    def sink(h):
        return sinks_ref[h] * LOG2E

    def store_pair(pair, band, o, lse):
        o_ref[0, _band_rows(band, SWA_BLOCK), pair * LANES:(pair + 1) * LANES] = o.astype(o_ref.dtype)

    _band_attention(
        SWA_Q_HEADS,
        lambda h, band: q_ref[0, _band_rows(band, SWA_BLOCK), (h // 2) * LANES:(h // 2 + 1) * LANES],
        lambda h, band: k_tiles[band, h // group],
        lambda h, band: v_tiles[band, h // group],
        logits, sink, store_pair, s_ref, with_lse=False)


def _swa_attention(qkv, sinks, b, s):
    q_width = SWA_Q_HEADS * HEAD_DIM
    kv_width = SWA_KV_HEADS * HEAD_DIM
    k_tile = q_width // kv_width
    step_rows = BANDS_PER_STEP * SWA_BLOCK
    assert s % step_rows == 0
    cur = lambda t: (lambda bi, n: (bi, n, t))
    prev = lambda t: (lambda bi, n: (bi, jnp.maximum(n * BANDS_PER_STEP - 1, 0), t))
    kv_blk, kv_prev_blk = (1, step_rows, kv_width), (1, SWA_BLOCK, kv_width)
    return pl.pallas_call(
        _swa_kernel,
        grid=(b, s // step_rows),
        in_specs=[pl.BlockSpec(memory_space=pltpu.SMEM), pl.BlockSpec(memory_space=pltpu.SMEM),
                  pl.BlockSpec((1, step_rows, q_width), cur(0)),
                  pl.BlockSpec(kv_prev_blk, prev(k_tile)), pl.BlockSpec(kv_blk, cur(k_tile)),
                  pl.BlockSpec(kv_prev_blk, prev(k_tile + 1)), pl.BlockSpec(kv_blk, cur(k_tile + 1))],
        out_specs=pl.BlockSpec((1, step_rows, q_width), cur(0)),
        out_shape=jax.ShapeDtypeStruct((b, s, q_width), BF16),
        scratch_shapes=[pltpu.VMEM((BANDS_PER_STEP * SWA_Q_HEADS, 2 * SWA_BLOCK, SWA_BLOCK), F32)],
        compiler_params=_params(("parallel", "arbitrary")),
        name="swa_attention",
    )(_alibi_slopes(SWA_Q_HEADS), sinks.astype(F32), qkv, qkv, qkv, qkv, qkv)


def kernel(x, l0_attn_norm, l0_w_qkv, l0_w_o, l0_mlp_norm, l0_w_up, l0_w_down, l1_attn_norm, l1_w_qkv, l1_w_o, l1_mlp_norm, l1_w_up, l1_w_down, l2_attn_norm, l2_w_dkv, l2_q_norm, l2_w_uq, l2_kv_norm, l2_w_ukv, l2_w_o, l2_mlp_norm, l2_w_up, l2_w_down, l3_attn_norm, l3_w_qkv, l3_sinks, l3_w_o, l3_mlp_norm, l3_w_up, l3_w_down, final_norm):
    b, s, d = x.shape
    bf = lambda w: w.astype(BF16)
    h = x.reshape(b * s, d)

    qkv = _norm_proj(h, l0_attn_norm, bf(l0_w_qkv), scaled_cols=MOBA_HEADS * HEAD_DIM, scale=HEAD_Q_SCALE)
    a = _moba_attention(qkv.reshape(b, s, -1), b, s).reshape(b * s, -1)
    h = _post(h, (a,), bf(l0_w_o), l0_mlp_norm, bf(l0_w_up), bf(l0_w_down))

    qkvs = _dil_proj(h, l1_attn_norm, bf(l1_w_qkv))
    groups = [_dilated_group(qkvs[g], b, s, g, window, dil) for g, (window, dil) in enumerate(DIL_PAIRS)]
    h = _post(h, tuple(o for o, _ in groups) + tuple(l for _, l in groups),
              bf(l1_w_o), l1_mlp_norm, bf(l1_w_up), bf(l1_w_down))

    q, k, v = _mla_proj(h, l2_attn_norm, l2_w_dkv, l2_q_norm, l2_w_uq, l2_kv_norm, l2_w_ukv, s)
    a = _mla_attention(q.reshape(b, s, -1), k.reshape(b, s, -1), v.reshape(b, s, -1), b, s).reshape(b * s, -1)
    h = _post(h, (a,), bf(l2_w_o), l2_mlp_norm, bf(l2_w_up), bf(l2_w_down))

    qkv = _norm_proj(h, l3_attn_norm, bf(l3_w_qkv), col_tile=256, scaled_cols=SWA_Q_HEADS * HEAD_DIM,
                     scale=HEAD_Q_SCALE)
    a = _swa_attention(qkv.reshape(b, s, -1), l3_sinks, b, s).reshape(b * s, -1)
    h = _post(h, (a,), bf(l3_w_o), l3_mlp_norm, bf(l3_w_up), bf(l3_w_down), g_final=final_norm)
    return h.reshape(b, s, d)
```

```python
import functools

import jax
import jax.numpy as jnp
from jax import lax
from jax.experimental import pallas as pl
from jax.experimental.pallas import tpu as pltpu

F32 = jnp.float32
BF16 = jnp.bfloat16

D_MODEL = 1024
HEAD_DIM = 64
RMS_EPS = 1e-6
D_FF = 4 * D_MODEL
NEG_INF = -1e30

MOBA_HEADS = 16
MOBA_BLOCK = 256
MOBA_TOPK = 3

DIL_PAIRS = ((128, 1), (512, 4), (2048, 16))
DIL_HEADS_PER_GROUP = 8
DIL_BAND = 128

MLA_HEADS = 16
MLA_Q_RANK = 768
MLA_KV_RANK = 256
MLA_NOPE = 64
MLA_ROPE = 32
MLA_V = 64
ROPE_THETA = 10000.0
MLA_TILE = 256
MLA_HEADS_PER_STEP = 4
MOBA_HEADS_PER_STEP = 4

SWA_Q_HEADS = 16
SWA_KV_HEADS = 2
SWA_WINDOW = 128
SWA_BLOCK = 128

LANES = 128
ROW_TILE = 512
FF_TILE = 512
VMEM_LIMIT = 56 * 1024 * 1024
LOG2E = 1.4426950408889634
HEAD_Q_SCALE = HEAD_DIM ** -0.5 * LOG2E
MLA_Q_SCALE = (MLA_NOPE + MLA_ROPE) ** -0.5 * LOG2E


def _params(semantics):
    return pltpu.CompilerParams(dimension_semantics=semantics, vmem_limit_bytes=VMEM_LIMIT)


def _resident(shape):
    return pl.BlockSpec(shape, lambda *_: (0,) * len(shape), pipeline_mode=pl.Buffered(1))


def _alibi_slopes(n_heads):
    return 2.0 ** (-8.0 * jnp.arange(1, n_heads + 1, dtype=F32) / n_heads)


def _rms(x, g):
    return x * lax.rsqrt(jnp.mean(x * x, axis=-1, keepdims=True) + RMS_EPS) * g


def _dot(a, b):
    return jnp.dot(a, b, preferred_element_type=F32)


def _dot_nt(a, b):
    return lax.dot_general(a, b, (((1,), (1,)), ((), ())), preferred_element_type=F32)


def _norm_proj_kernel(h_ref, g_ref, w_ref, o_ref, *, col_tile, scaled_cols, scale):
    n = _rms(h_ref[...], g_ref[...]).astype(BF16)
    for c in range(o_ref.shape[1] // col_tile):
        cols = slice(c * col_tile, (c + 1) * col_tile)
        y = _dot(n, w_ref[:, cols])
        if (c + 1) * col_tile <= scaled_cols:
            y = y * scale
        o_ref[:, cols] = y.astype(o_ref.dtype)


def _norm_proj(h, g, w, col_tile=512, scaled_cols=0, scale=1.0):
    m, d = h.shape
    n_out = w.shape[1]
    assert scaled_cols % col_tile == 0
    return pl.pallas_call(
        functools.partial(_norm_proj_kernel, col_tile=col_tile, scaled_cols=scaled_cols, scale=scale),
        grid=(m // ROW_TILE,),
        in_specs=[pl.BlockSpec((ROW_TILE, d), lambda i: (i, 0)),
                  _resident((1, d)),
                  _resident((d, n_out))],
        out_specs=pl.BlockSpec((ROW_TILE, n_out), lambda i: (i, 0)),
        out_shape=jax.ShapeDtypeStruct((m, n_out), BF16),
        compiler_params=_params(("parallel",)),
        name="norm_proj",
    )(h, g.reshape(1, d), w)


def _dil_proj_kernel(h_ref, g_ref, w_ref, o0_ref, o1_ref, o2_ref, stage_ref, sorted_ref):
    n = _rms(h_ref[...], g_ref[...])
    rows, d = n.shape
    width = DIL_HEADS_PER_GROUP * HEAD_DIM
    for t in range(d // LANES):
        stage_ref[t] = n[:, t * LANES:(t + 1) * LANES]
    n = n.astype(BF16)
    for g, (o_ref, (_, dil)) in enumerate(zip((o0_ref, o1_ref, o2_ref), DIL_PAIRS)):
        per = rows // dil
        for r in range(dil if dil > 1 else 0):
            for t in range(d // LANES):
                sorted_ref[g - 1, r * per:(r + 1) * per, t * LANES:(t + 1) * LANES] = (
                    stage_ref[t, pl.ds(r, per, stride=dil), :].astype(BF16))
        lhs = n if dil == 1 else sorted_ref[g - 1]
        for part in range(3):
            src = (3 * g + part) * width
            y = _dot(lhs, w_ref[:, src:src + width])
            if part == 0:
                y = y * HEAD_Q_SCALE
            for r in range(dil):
                dst = (3 * r + part) * width
                o_ref[:, dst:dst + width] = y[r * per:(r + 1) * per, :].astype(BF16)


def _dil_proj(h, g, w):
    m, d = h.shape
    feat = 3 * DIL_HEADS_PER_GROUP * HEAD_DIM
    assert DIL_PAIRS[0][1] == 1
    view = lambda dil: (m // dil, dil * feat)
    view_block = lambda dil: pl.BlockSpec((ROW_TILE // dil, dil * feat), lambda i: (i, 0))
    return pl.pallas_call(
        _dil_proj_kernel,
        grid=(m // ROW_TILE,),
        in_specs=[pl.BlockSpec((ROW_TILE, d), lambda i: (i, 0)), _resident((1, d)), _resident(w.shape)],
        out_specs=[view_block(dil) for _, dil in DIL_PAIRS],
        out_shape=[jax.ShapeDtypeStruct(view(dil), BF16) for _, dil in DIL_PAIRS],
        scratch_shapes=[pltpu.VMEM((d // LANES, ROW_TILE, LANES), F32),
                        pltpu.VMEM((len(DIL_PAIRS) - 1, ROW_TILE, d), BF16)],
        compiler_params=_params(("parallel",)),
        name="dil_proj",
    )(h, g.reshape(1, d), w)


def _mlp_tail(h1, g_ref, wup_ref, wdn_ref, gf_ref, out_ref):
    n = _rms(h1, g_ref[...]).astype(BF16)
    acc = jnp.zeros_like(h1)
    for c in range(wup_ref.shape[1] // FF_TILE):
        cols = slice(c * FF_TILE, (c + 1) * FF_TILE)
        u = jnp.square(jnp.maximum(_dot(n, wup_ref[:, cols]), 0.0)).astype(BF16)
        acc = acc + _dot(u, wdn_ref[cols, :])
    out = h1 + acc
    if gf_ref is not None:
        out = _rms(out, gf_ref[...])
    out_ref[...] = out


def _post_kernel(h_ref, a_ref, wo_ref, g_ref, wup_ref, wdn_ref, *rest):
    gf_ref, out_ref = (rest[0], rest[1]) if len(rest) == 2 else (None, rest[0])
    h1 = h_ref[...] + _dot(a_ref[...], wo_ref[...])
    _mlp_tail(h1, g_ref, wup_ref, wdn_ref, gf_ref, out_ref)


def _post_merge_kernel(h_ref, o0_ref, o1_ref, o2_ref, l0_ref, l1_ref, l2_ref,
                       wo_ref, g_ref, wup_ref, wdn_ref, out_ref, stage_ref):
    rows, width = o0_ref.shape

    def token_rows(x_ref, slot, dil):
        if dil == 1:
            return x_ref[...]
        for r in range(dil):
            for t in range(width // LANES):
                src = r * width + t * LANES
                stage_ref[slot, t, pl.ds(r, rows // dil, stride=dil), :] = x_ref[:, src:src + LANES]
        return jnp.concatenate([stage_ref[slot, t] for t in range(width // LANES)], axis=1)

    dils = [dil for _, dil in DIL_PAIRS]
    l0, l1, l2 = (token_rows(ref, slot, dil) for slot, (ref, dil) in enumerate(zip((l0_ref, l1_ref, l2_ref), dils)))
    o0, o1, o2 = (token_rows(ref, 3 + slot, dil) for slot, (ref, dil) in enumerate(zip((o0_ref, o1_ref, o2_ref), dils)))
    mx = jnp.maximum(jnp.maximum(l0, l1), l2)
    e0, e1, e2 = jnp.exp(l0 - mx), jnp.exp(l1 - mx), jnp.exp(l2 - mx)
    merged = (e0 * o0 + e1 * o1 + e2 * o2) / (e0 + e1 + e2)
    h1 = h_ref[...] + _dot(merged.astype(BF16), wo_ref[...])
    _mlp_tail(h1, g_ref, wup_ref, wdn_ref, None, out_ref)


def _post(h, attn_ins, w_o, g, w_up, w_down, g_final=None):
    m, d = h.shape
    row = lambda width: pl.BlockSpec((ROW_TILE, width), lambda i: (i, 0))
    merge = len(attn_ins) > 1
    in_specs = [row(d)] + [pl.BlockSpec((ROW_TILE * a.shape[0] // m, a.shape[1]), lambda i: (i, 0)) for a in attn_ins]
    in_specs += [_resident(w_o.shape), _resident((1, d)), _resident(w_up.shape), _resident(w_down.shape)]
    args = [h, *attn_ins, w_o, g.reshape(1, d), w_up, w_down]
    if g_final is not None:
        in_specs.append(_resident((1, d)))
        args.append(g_final.reshape(1, d))
    scratch = [pltpu.VMEM((len(attn_ins), w_o.shape[0] // LANES, ROW_TILE, LANES), F32)] if merge else []
    return pl.pallas_call(
        _post_merge_kernel if merge else _post_kernel,
        grid=(m // ROW_TILE,),
        in_specs=in_specs,
        out_specs=row(d),
        out_shape=jax.ShapeDtypeStruct((m, d), F32),
        scratch_shapes=scratch,
        compiler_params=_params(("parallel",)),
        name="post_mlp",
    )(*args)


ONES_ROWS = 16
SKIP_SHIFT = -2.0 * NEG_INF


class _SoftmaxState:
    def __init__(self, m_ref, acc_ref, dv):
        self.m_ref, self.acc_ref, self.dv = m_ref, acc_ref, dv

    @staticmethod
    def scratch(n_heads, dv, tq):
        return [pltpu.VMEM((n_heads, 1, tq), F32), pltpu.VMEM((n_heads, dv + ONES_ROWS, tq), F32)]

    @staticmethod
    def store_values_t(vt_ref, hh, cols, v_t):
        dv = v_t.shape[0]
        vt_ref[hh, :dv, cols] = v_t.astype(BF16)
        vt_ref[hh, dv:, cols] = jnp.ones((ONES_ROWS, v_t.shape[1]), BF16)

    def init(self):
        self.m_ref[...] = jnp.full(self.m_ref.shape, NEG_INF, F32)
        self.acc_ref[...] = jnp.zeros(self.acc_ref.shape, F32)

    def step(self, hh, x, v_t, shift=None):
        m = self.m_ref[hh]
        m_cur = jnp.max(x, axis=0, keepdims=True)
        if shift is not None:
            m_cur = m_cur - shift
        m_new = jnp.maximum(m, m_cur)
        alpha = jnp.exp2(m - m_new)
        p = jnp.exp2(x - (m_new if shift is None else m_new + shift))
        self.m_ref[hh] = m_new
        self.acc_ref[hh] = alpha * self.acc_ref[hh] + _dot(v_t, p.astype(BF16))

    def normalized(self, hh):
        return self.acc_ref[hh, :self.dv, :] / self.acc_ref[hh, self.dv:self.dv + 1, :]


def _sweep_blocks(n_past, n_heads, state, score_fn, values_fn, past_logits, own_logits, s_a, s_b):
    def produce(buf, block):
        for hh in range(n_heads):
            buf[hh] = score_fn(hh, block)

    def consume(buf, block):
        for hh in range(n_heads):
            x, shift = past_logits(hh, buf[hh], block)
            state.step(hh, x, values_fn(hh, block), shift)

    def consume_own(buf):
        for hh in range(n_heads):
            x, shift = own_logits(hh, buf[hh])
            state.step(hh, x, values_fn(hh, n_past), shift)

    produce(s_a, 0)

    def two_blocks(first):
        produce(s_b, first + 1)
        consume(s_a, first)
        produce(s_a, first + 2)
        consume(s_b, first + 1)

    def quad(t, carry):
        two_blocks(4 * t)
        two_blocks(4 * t + 2)
        return carry

    def pair(t, carry):
        two_blocks(4 * (n_past // 4) + 2 * t)
        return carry

    lax.fori_loop(0, n_past // 4, quad, 0)
    lax.fori_loop(0, (n_past % 4) // 2, pair, 0)

    @pl.when(n_past % 2 == 1)
    def _():
        produce(s_b, n_past)
        consume(s_a, n_past - 1)
        consume_own(s_b)

    @pl.when(n_past % 2 == 0)
    def _():
        consume_own(s_a)


def _transpose_bf16(x):
    return x.astype(F32).T.astype(BF16)


def _keep_head_rows(x_t, half):
    r = lax.broadcasted_iota(jnp.int32, x_t.shape, 0)
    keep = (r < HEAD_DIM) if half == 0 else (r >= HEAD_DIM)
    return jnp.where(keep, x_t, jnp.zeros_like(x_t))


def _moba_kernel(slopes_ref, q_ref, k_ref, v_ref, o_ref, kmean_ref, vt_ref, term_ref, alibi_ref, qh_ref,
                 s_a, s_b, m_ref, acc_ref, *, n_blocks):
    blk = MOBA_BLOCK
    n_heads = MOBA_HEADS_PER_STEP
    group = pl.program_id(1)
    i = pl.program_id(2)
    state = _SoftmaxState(m_ref, acc_ref, HEAD_DIM)

    @pl.when(i == 0)
    def _():
        for j in range(n_blocks):
            rows = slice(j * blk, (j + 1) * blk)
            kmean_ref[j:j + 1, :] = jnp.mean(k_ref[0, rows, :].astype(F32), axis=0, keepdims=True)
            v_t = v_ref[0, rows, :].astype(F32).T
            for hh in range(n_heads):
                state.store_values_t(vt_ref, hh, rows, v_t[hh * HEAD_DIM:(hh + 1) * HEAD_DIM, :])

    state.init()
    q_t = q_ref[0].astype(F32).T
    key = lax.broadcasted_iota(jnp.int32, (blk, blk), 0)
    qry = lax.broadcasted_iota(jnp.int32, (blk, blk), 1)
    rel = (qry - key).astype(F32)
    blk_idx = lax.broadcasted_iota(jnp.int32, (n_blocks, blk), 0)
    fully_past = blk_idx < i
    block_dist = ((i - blk_idx) * blk).astype(F32)
    lane_tile = lambda hh: slice((hh // 2) * LANES, (hh // 2 + 1) * LANES)

    for hh in range(n_heads):
        slope_s = slopes_ref[n_heads * group + hh] * LOG2E
        q_h = _keep_head_rows(q_t[lane_tile(hh), :], hh % 2)
        gate = lax.dot_general(kmean_ref[:, lane_tile(hh)], q_h, (((1,), (0,)), ((), ())),
                               precision=lax.Precision.HIGHEST, preferred_element_type=F32)
        work = jnp.where(fully_past, gate, NEG_INF * HEAD_Q_SCALE)
        chosen = jnp.zeros((n_blocks, blk), jnp.bool_)
        for _ in range(min(MOBA_TOPK, n_blocks)):
            best = jnp.max(work, axis=0, keepdims=True)
            first = jnp.min(jnp.where(work == best, blk_idx, n_blocks), axis=0, keepdims=True)
            pick = blk_idx == first
            chosen = jnp.logical_or(chosen, pick)
            work = jnp.where(pick, -jnp.inf, work)
        term_ref[hh] = jnp.where(jnp.logical_and(chosen, fully_past), slope_s * block_dist, SKIP_SHIFT)
        qh_ref[hh] = q_h.astype(BF16)
        alibi_ref[hh] = slope_s * rel

    def rows_of(block):
        return pl.ds(pl.multiple_of(block * blk, blk), blk)

    def scores(hh, block):
        return _dot(k_ref[0, rows_of(block), lane_tile(hh)], qh_ref[hh])

    def v_t(hh, block):
        return vt_ref[hh, :, rows_of(block)]

    def past_logits(hh, s, block):
        return s - alibi_ref[hh], term_ref[hh, pl.ds(block, 1), :]

    def own_logits(hh, s):
        return jnp.where(rel >= 0, s - alibi_ref[hh], NEG_INF), None

    _sweep_blocks(i, n_heads, state, scores, v_t, past_logits, own_logits, s_a, s_b)
    o_t = jnp.concatenate([state.normalized(hh) for hh in range(n_heads)], axis=0)
    o_ref[0] = o_t.T.astype(o_ref.dtype)


def _moba_attention(qkv, b, s):
    n_blocks = s // MOBA_BLOCK
    n_heads = MOBA_HEADS_PER_STEP
    width = n_heads * HEAD_DIM
    n_groups = MOBA_HEADS // n_heads
    score_buf = pltpu.VMEM((n_heads, MOBA_BLOCK, MOBA_BLOCK), F32)
    return pl.pallas_call(
        functools.partial(_moba_kernel, n_blocks=n_blocks),
        grid=(b, n_groups, n_blocks),
        in_specs=[pl.BlockSpec(memory_space=pltpu.SMEM),
                  pl.BlockSpec((1, MOBA_BLOCK, width), lambda bi, g, i: (bi, i, g)),
                  pl.BlockSpec((1, s, width), lambda bi, g, i: (bi, 0, n_groups + g)),
                  pl.BlockSpec((1, s, width), lambda bi, g, i: (bi, 0, 2 * n_groups + g))],
        out_specs=pl.BlockSpec((1, MOBA_BLOCK, width), lambda bi, g, i: (bi, i, g)),
        out_shape=jax.ShapeDtypeStruct((b, s, MOBA_HEADS * HEAD_DIM), BF16),
        scratch_shapes=[pltpu.VMEM((n_blocks, width), F32),
                        pltpu.VMEM((n_heads, HEAD_DIM + ONES_ROWS, s), BF16),
                        pltpu.VMEM((n_heads, n_blocks, MOBA_BLOCK), F32),
                        pltpu.VMEM((n_heads, MOBA_BLOCK, MOBA_BLOCK), F32),
                        pltpu.VMEM((n_heads, LANES, MOBA_BLOCK), BF16),
                        score_buf, score_buf,
                        *_SoftmaxState.scratch(n_heads, HEAD_DIM, MOBA_BLOCK)],
        compiler_params=_params(("parallel", "parallel", "arbitrary")),
        name="moba_attention",
    )(_alibi_slopes(MOBA_HEADS), qkv, qkv, qkv)


def _lane_half(x, half):
    lane = lax.broadcasted_iota(jnp.int32, x.shape, x.ndim - 1)
    keep = (lane < HEAD_DIM) if half == 0 else (lane >= HEAD_DIM)
    return jnp.where(keep, x, jnp.zeros_like(x))


BANDS_PER_STEP = 2


def _band_attention(n_heads, q_tile, k_tile, v_tile, logits_fn, floor_fn, store_pair, s_ref, with_lse):
    chains = [(band, h) for band in range(BANDS_PER_STEP) for h in range(n_heads)]
    for c, (band, h) in enumerate(chains):
        s_ref[c] = _dot_nt(k_tile(h, band), _lane_half(q_tile(h, band), h % 2))
    even = None
    for c, (band, h) in enumerate(chains):
        x = logits_fn(h, band, s_ref[c])
        m = jnp.max(x, axis=0, keepdims=True)
        floor = floor_fn(h)
        if floor is not None:
            m = jnp.maximum(m, floor)
        p = jnp.exp2(x - m)
        den = jnp.sum(p, axis=0, keepdims=True)
        if floor is not None:
            den = den + jnp.exp2(floor - m)
        o_t = lax.dot_general(v_tile(h, band), p.astype(BF16), (((0,), (0,)), ((), ())),
                              preferred_element_type=F32)
        half = slice((h % 2) * HEAD_DIM, (h % 2 + 1) * HEAD_DIM)
        o_t = o_t[half, :] / den
        lse_t = jnp.broadcast_to((m + jnp.log2(den)) * (1.0 / LOG2E), o_t.shape) if with_lse else None
        if h % 2 == 0:
            even = (o_t, lse_t)
            continue
        store_pair(h // 2, band, jnp.concatenate([even[0], o_t], axis=0).T,
                   jnp.concatenate([even[1], lse_t], axis=0).T if with_lse else None)


def _band_rows(band, size):
    return slice(band * size, (band + 1) * size)


def _band_keys(prev_ref, cur_ref, band, size, cols):
    if band == 0:
        return jnp.concatenate([prev_ref[0, :, cols], cur_ref[0, :size, cols]], axis=0)
    return cur_ref[0, (band - 1) * size:(band + 1) * size, cols]


def _band_geometry(size, n):
    key = lax.broadcasted_iota(jnp.int32, (2 * size, size), 0)
    qry = lax.broadcasted_iota(jnp.int32, (2 * size, size), 1)
    diff = qry + size - key
    in_sequence = (n > 0) | (key >= size)
    return diff, in_sequence


def _dil_kernel(slopes_ref, q_ref, kp_ref, kc_ref, vp_ref, vc_ref, o_ref, lse_ref, s_ref, *, n_pts, dil, group):
    n = pl.program_id(2)
    diff, in_sequence = _band_geometry(DIL_BAND, n)
    window = (diff >= 0) & (diff <= n_pts)
    masks = [window & in_sequence] + [window] * (BANDS_PER_STEP - 1)
    dist = (dil * diff).astype(F32) * LOG2E
    tile = lambda h: slice((h // 2) * LANES, (h // 2 + 1) * LANES)

    def logits(h, band, s):
        slope = slopes_ref[group * DIL_HEADS_PER_GROUP + h]
        return jnp.where(masks[band], s - slope * dist, NEG_INF)

    def store_pair(pair, band, o, lse):
        rows, cols = _band_rows(band, DIL_BAND), slice(pair * LANES, (pair + 1) * LANES)
        o_ref[0, rows, cols] = o
        lse_ref[0, rows, cols] = lse

    _band_attention(
        DIL_HEADS_PER_GROUP,
        lambda h, band: q_ref[0, _band_rows(band, DIL_BAND), tile(h)],
        lambda h, band: _band_keys(kp_ref, kc_ref, band, DIL_BAND, tile(h)),
        lambda h, band: _band_keys(vp_ref, vc_ref, band, DIL_BAND, tile(h)),
        logits, lambda h: None, store_pair, s_ref, with_lse=True)


def _dilated_group(qkv, b, s, group, window, dil):
    width = DIL_HEADS_PER_GROUP * HEAD_DIM
    n_sub = s // dil
    step_rows = BANDS_PER_STEP * DIL_BAND
    assert n_sub % step_rows == 0
    view = qkv.reshape(b, n_sub, dil * 3 * width)

    def col(part):
        return lambda bi, r, n: (bi, n, r * 3 + part)

    def col_prev(part):
        return lambda bi, r, n: (bi, jnp.maximum(n * BANDS_PER_STEP - 1, 0), r * 3 + part)

    blk, prev_blk = (1, step_rows, width), (1, DIL_BAND, width)
    out_spec = pl.BlockSpec(blk, lambda bi, r, n: (bi, n, r))
    out_sds = jax.ShapeDtypeStruct((b, n_sub, dil * width), F32)
    o, lse = pl.pallas_call(
        functools.partial(_dil_kernel, n_pts=window // dil, dil=dil, group=group),
        grid=(b, dil, n_sub // step_rows),
        in_specs=[pl.BlockSpec(memory_space=pltpu.SMEM),
                  pl.BlockSpec(blk, col(0)),
                  pl.BlockSpec(prev_blk, col_prev(1)), pl.BlockSpec(blk, col(1)),
                  pl.BlockSpec(prev_blk, col_prev(2)), pl.BlockSpec(blk, col(2))],
        out_specs=[out_spec, out_spec],
        out_shape=[out_sds, out_sds],
        scratch_shapes=[pltpu.VMEM((BANDS_PER_STEP * DIL_HEADS_PER_GROUP, 2 * DIL_BAND, DIL_BAND), F32)],
        compiler_params=_params(("parallel", "parallel", "arbitrary")),
        name=f"dilated_attention_g{group}",
    )(_alibi_slopes(len(DIL_PAIRS) * DIL_HEADS_PER_GROUP), view, view, view, view, view)
    return o.reshape(b * n_sub, dil * width), lse.reshape(b * n_sub, dil * width)


def _mla_proj_kernel(h_ref, g_ref, wd_ref, gq_ref, gkv_ref, wuq_ref, wuk_ref, wuv_ref,
                     cos_ref, sin_up_ref, sin_dn_ref, q_ref, k_ref, v_ref):
    n = _rms(h_ref[...], g_ref[...]).astype(BF16)
    c = _dot(n, wd_ref[...])
    nq = _rms(c[:, :MLA_Q_RANK], gq_ref[...]).astype(BF16)
    nkv = _rms(c[:, MLA_Q_RANK:MLA_Q_RANK + MLA_KV_RANK], gkv_ref[...]).astype(BF16)
    k_rope = c[:, MLA_Q_RANK + MLA_KV_RANK:]
    half = MLA_ROPE // 2
    pair = 2

    def rope(x, width):
        cos, sin_up, sin_dn = (jnp.tile(t[...], (1, width)) for t in (cos_ref, sin_up_ref, sin_dn_ref))
        return x * cos + pltpu.roll(x, half, 1) * sin_up + pltpu.roll(x, x.shape[1] - half, 1) * sin_dn

    kr = jnp.tile(rope(k_rope, 1), (1, pair))
    for h in range(0, MLA_HEADS, pair):
        cols = slice(h * LANES, (h + pair) * LANES)
        q_ref[:, cols] = (rope(_dot(nq, wuq_ref[:, cols]), pair) * MLA_Q_SCALE).astype(BF16)
        k_ref[:, cols] = (_dot(nkv, wuk_ref[:, cols]) + kr).astype(BF16)
    v_ref[...] = _dot(nkv, wuv_ref[...]).astype(BF16)


def _mla_proj(h, g, w_dkv, q_norm, w_uq, kv_norm, w_ukv, s):
    m, d = h.shape
    qk = MLA_NOPE + MLA_ROPE
    wd = jnp.concatenate([w_dkv[:, :MLA_Q_RANK + MLA_KV_RANK],
                          jnp.zeros((d, MLA_NOPE), F32), w_dkv[:, MLA_Q_RANK + MLA_KV_RANK:],
                          jnp.zeros((d, LANES - qk), F32)], axis=1).astype(BF16)
    wuq = jnp.pad(w_uq.reshape(MLA_Q_RANK, MLA_HEADS, qk), ((0, 0), (0, 0), (0, LANES - qk)))
    wuq = wuq.reshape(MLA_Q_RANK, MLA_HEADS * LANES).astype(BF16)
    w_ukv = w_ukv.reshape(MLA_KV_RANK, MLA_HEADS, MLA_NOPE + MLA_V)
    wuk = jnp.pad(w_ukv[:, :, :MLA_NOPE], ((0, 0), (0, 0), (0, LANES - MLA_NOPE)))
    wuk = wuk.reshape(MLA_KV_RANK, MLA_HEADS * LANES).astype(BF16)
    wuv = w_ukv[:, :, MLA_NOPE:].reshape(MLA_KV_RANK, MLA_HEADS * MLA_V).astype(BF16)
    half = MLA_ROPE // 2
    inv = ROPE_THETA ** (-jnp.arange(0, MLA_ROPE, 2, dtype=F32) / MLA_ROPE)
    ang = jnp.arange(s).astype(F32)[:, None] * inv[None, :]
    cos, sin = jnp.cos(ang), jnp.sin(ang)
    zeros = lambda w: jnp.zeros((s, w), F32)
    cos_t = jnp.concatenate([jnp.ones((s, MLA_NOPE), F32), cos, cos, zeros(LANES - qk)], axis=1)
    sin_up = jnp.concatenate([zeros(MLA_NOPE + half), sin, zeros(LANES - qk)], axis=1)
    sin_dn = jnp.concatenate([zeros(MLA_NOPE), -sin, zeros(half + LANES - qk)], axis=1)

    tiles_per_seq = s // ROW_TILE
    row = lambda width: pl.BlockSpec((ROW_TILE, width), lambda i: (i, 0))
    table = pl.BlockSpec((ROW_TILE, LANES), lambda i: (i % tiles_per_seq, 0))
    return pl.pallas_call(
        _mla_proj_kernel,
        grid=(m // ROW_TILE,),
        in_specs=[row(d), _resident((1, d)), _resident(wd.shape),
                  _resident((1, MLA_Q_RANK)), _resident((1, MLA_KV_RANK)),
                  _resident(wuq.shape), _resident(wuk.shape), _resident(wuv.shape),
                  table, table, table],
        out_specs=[row(MLA_HEADS * LANES), row(MLA_HEADS * LANES), row(MLA_HEADS * MLA_V)],
        out_shape=[jax.ShapeDtypeStruct((m, MLA_HEADS * LANES), BF16),
                   jax.ShapeDtypeStruct((m, MLA_HEADS * LANES), BF16),
                   jax.ShapeDtypeStruct((m, MLA_HEADS * MLA_V), BF16)],
        compiler_params=_params(("parallel",)),
        name="mla_proj",
    )(h, g.reshape(1, d), wd, q_norm.reshape(1, -1), kv_norm.reshape(1, -1), wuq, wuk, wuv,
      cos_t, sin_up, sin_dn)


def _mla_kernel(q_ref, k_ref, v_ref, o_ref, vt_ref, qh_ref, s_a, s_b, m_ref, acc_ref, *, n_tiles):
    t = MLA_TILE
    n_heads = MLA_HEADS_PER_STEP
    i = pl.program_id(2)
    state = _SoftmaxState(m_ref, acc_ref, MLA_V)

    @pl.when(i == 0)
    def _():
        for j in range(n_tiles):
            rows = slice(j * t, (j + 1) * t)
            v_t = v_ref[0, rows, :].astype(F32).T
            for hh in range(n_heads):
                state.store_values_t(vt_ref, hh, rows, v_t[hh * MLA_V:(hh + 1) * MLA_V, :])

    state.init()
    key = lax.broadcasted_iota(jnp.int32, (t, t), 0)
    qry = lax.broadcasted_iota(jnp.int32, (t, t), 1)
    lane_tile = lambda hh: slice(hh * LANES, (hh + 1) * LANES)
    for hh in range(n_heads):
        qh_ref[hh] = _transpose_bf16(q_ref[0, :, lane_tile(hh)])

    def rows_of(block):
        return pl.ds(pl.multiple_of(block * t, t), t)

    def scores(hh, block):
        return _dot(k_ref[0, rows_of(block), lane_tile(hh)], qh_ref[hh])

    def v_t(hh, block):
        return vt_ref[hh, :, rows_of(block)]

    def past_logits(hh, s, block):
        return s, None

    def own_logits(hh, s):
        return jnp.where(key <= qry, s, NEG_INF), None

    _sweep_blocks(i, n_heads, state, scores, v_t, past_logits, own_logits, s_a, s_b)
    o_t = jnp.concatenate([state.normalized(hh) for hh in range(n_heads)], axis=0)
    o_ref[0] = o_t.T.astype(o_ref.dtype)


def _mla_attention(q, k, v, b, s):
    n_heads = MLA_HEADS_PER_STEP
    n_groups = MLA_HEADS // n_heads
    whole_seq = lambda bi, g, i: (bi, 0, g)
    tile = lambda bi, g, i: (bi, i, g)
    score_buf = pltpu.VMEM((n_heads, MLA_TILE, MLA_TILE), F32)
    return pl.pallas_call(
        functools.partial(_mla_kernel, n_tiles=s // MLA_TILE),
        grid=(b, n_groups, s // MLA_TILE),
        in_specs=[pl.BlockSpec((1, MLA_TILE, n_heads * LANES), tile),
                  pl.BlockSpec((1, s, n_heads * LANES), whole_seq),
                  pl.BlockSpec((1, s, n_heads * MLA_V), whole_seq)],
        out_specs=pl.BlockSpec((1, MLA_TILE, n_heads * MLA_V), tile),
        out_shape=jax.ShapeDtypeStruct((b, s, MLA_HEADS * MLA_V), BF16),
        scratch_shapes=[pltpu.VMEM((n_heads, MLA_V + ONES_ROWS, s), BF16),
                        pltpu.VMEM((n_heads, LANES, MLA_TILE), BF16),
                        score_buf, score_buf,
                        *_SoftmaxState.scratch(n_heads, MLA_V, MLA_TILE)],
        compiler_params=_params(("parallel", "parallel", "arbitrary")),
        name="mla_attention",
    )(q, k, v)


def _swa_kernel(slopes_ref, sinks_ref, q_ref, kp_ref, kc_ref, vp_ref, vc_ref, o_ref, s_ref):
    n = pl.program_id(1)
    group = SWA_Q_HEADS // SWA_KV_HEADS
    diff, in_sequence = _band_geometry(SWA_BLOCK, n)
    window = (diff >= 0) & (diff < SWA_WINDOW)
    masks = [window & in_sequence] + [window] * (BANDS_PER_STEP - 1)
    dist = diff.astype(F32) * LOG2E

    def both_halves(prev_ref, cur_ref, band, kv_head):
        x = _band_keys(prev_ref, cur_ref, band, SWA_BLOCK, slice(None))
        part = x[:, kv_head * HEAD_DIM:(kv_head + 1) * HEAD_DIM]
        return jnp.concatenate([part, part], axis=1)

    bands_kv = [(band, kv) for band in range(BANDS_PER_STEP) for kv in range(SWA_KV_HEADS)]
    k_tiles = {bk: both_halves(kp_ref, kc_ref, *bk) for bk in bands_kv}
    v_tiles = {bk: both_halves(vp_ref, vc_ref, *bk) for bk in bands_kv}

    def logits(h, band, s):
        return jnp.where(masks[band], s - slopes_ref[h] * dist, NEG_INF)

    def sink(h):
        return sinks_ref[h] * LOG2E

    def store_pair(pair, band, o, lse):
        o_ref[0, _band_rows(band, SWA_BLOCK), pair * LANES:(pair + 1) * LANES] = o.astype(o_ref.dtype)

    _band_attention(
        SWA_Q_HEADS,
        lambda h, band: q_ref[0, _band_rows(band, SWA_BLOCK), (h // 2) * LANES:(h // 2 + 1) * LANES],
        lambda h, band: k_tiles[band, h // group],
        lambda h, band: v_tiles[band, h // group],
        logits, sink, store_pair, s_ref, with_lse=False)


def _swa_attention(qkv, sinks, b, s):
    q_width = SWA_Q_HEADS * HEAD_DIM
    kv_width = SWA_KV_HEADS * HEAD_DIM
    k_tile = q_width // kv_width
    step_rows = BANDS_PER_STEP * SWA_BLOCK
    assert s % step_rows == 0
    cur = lambda t: (lambda bi, n: (bi, n, t))
    prev = lambda t: (lambda bi, n: (bi, jnp.maximum(n * BANDS_PER_STEP - 1, 0), t))
    kv_blk, kv_prev_blk = (1, step_rows, kv_width), (1, SWA_BLOCK, kv_width)
    return pl.pallas_call(
        _swa_kernel,
        grid=(b, s // step_rows),
        in_specs=[pl.BlockSpec(memory_space=pltpu.SMEM), pl.BlockSpec(memory_space=pltpu.SMEM),
                  pl.BlockSpec((1, step_rows, q_width), cur(0)),
                  pl.BlockSpec(kv_prev_blk, prev(k_tile)), pl.BlockSpec(kv_blk, cur(k_tile)),
                  pl.BlockSpec(kv_prev_blk, prev(k_tile + 1)), pl.BlockSpec(kv_blk, cur(k_tile + 1))],
        out_specs=pl.BlockSpec((1, step_rows, q_width), cur(0)),
        out_shape=jax.ShapeDtypeStruct((b, s, q_width), BF16),
        scratch_shapes=[pltpu.VMEM((BANDS_PER_STEP * SWA_Q_HEADS, 2 * SWA_BLOCK, SWA_BLOCK), F32)],
        compiler_params=_params(("parallel", "arbitrary")),
        name="swa_attention",
    )(_alibi_slopes(SWA_Q_HEADS), sinks.astype(F32), qkv, qkv, qkv, qkv, qkv)


def kernel(x, l0_attn_norm, l0_w_qkv, l0_w_o, l0_mlp_norm, l0_w_up, l0_w_down, l1_attn_norm, l1_w_qkv, l1_w_o, l1_mlp_norm, l1_w_up, l1_w_down, l2_attn_norm, l2_w_dkv, l2_q_norm, l2_w_uq, l2_kv_norm, l2_w_ukv, l2_w_o, l2_mlp_norm, l2_w_up, l2_w_down, l3_attn_norm, l3_w_qkv, l3_sinks, l3_w_o, l3_mlp_norm, l3_w_up, l3_w_down, final_norm):
    b, s, d = x.shape
    bf = lambda w: w.astype(BF16)
    h = x.reshape(b * s, d)

    qkv = _norm_proj(h, l0_attn_norm, bf(l0_w_qkv), scaled_cols=MOBA_HEADS * HEAD_DIM, scale=HEAD_Q_SCALE)
    a = _moba_attention(qkv.reshape(b, s, -1), b, s).reshape(b * s, -1)
    h = _post(h, (a,), bf(l0_w_o), l0_mlp_norm, bf(l0_w_up), bf(l0_w_down))

    qkvs = _dil_proj(h, l1_attn_norm, bf(l1_w_qkv))
    groups = [_dilated_group(qkvs[g], b, s, g, window, dil) for g, (window, dil) in enumerate(DIL_PAIRS)]
    h = _post(h, tuple(o for o, _ in groups) + tuple(l for _, l in groups),
              bf(l1_w_o), l1_mlp_norm, bf(l1_w_up), bf(l1_w_down))

    q, k, v = _mla_proj(h, l2_attn_norm, l2_w_dkv, l2_q_norm, l2_w_uq, l2_kv_norm, l2_w_ukv, s)
    a = _mla_attention(q.reshape(b, s, -1), k.reshape(b, s, -1), v.reshape(b, s, -1), b, s).reshape(b * s, -1)
    h = _post(h, (a,), bf(l2_w_o), l2_mlp_norm, bf(l2_w_up), bf(l2_w_down))

    qkv = _norm_proj(h, l3_attn_norm, bf(l3_w_qkv), col_tile=256, scaled_cols=SWA_Q_HEADS * HEAD_DIM,
                     scale=HEAD_Q_SCALE)
    a = _swa_attention(qkv.reshape(b, s, -1), l3_sinks, b, s).reshape(b * s, -1)
    h = _post(h, (a,), bf(l3_w_o), l3_mlp_norm, bf(l3_w_up), bf(l3_w_down), g_final=final_norm)
    return h.reshape(b, s, d)
```

```python
import functools

import jax
import jax.numpy as jnp
from jax import lax
from jax.experimental import pallas as pl
from jax.experimental.pallas import tpu as pltpu

F32 = jnp.float32
BF16 = jnp.bfloat16

D_MODEL = 1024
HEAD_DIM = 64
RMS_EPS = 1e-6
D_FF = 4 * D_MODEL
NEG_INF = -1e30

MOBA_HEADS = 16
MOBA_BLOCK = 256
MOBA_TOPK = 3

DIL_PAIRS = ((128, 1), (512, 4), (2048, 16))
DIL_HEADS_PER_GROUP = 8
DIL_BAND = 128

MLA_HEADS = 16
MLA_Q_RANK = 768
MLA_KV_RANK = 256
MLA_NOPE = 64
MLA_ROPE = 32
MLA_V = 64
ROPE_THETA = 10000.0
MLA_TILE = 256
MLA_HEADS_PER_STEP = 8
MOBA_HEADS_PER_STEP = 8

SWA_Q_HEADS = 16
SWA_KV_HEADS = 2
SWA_WINDOW = 128
SWA_BLOCK = 128

LANES = 128
ROW_TILE = 512
FF_TILE = 512
VMEM_LIMIT = 56 * 1024 * 1024
LOG2E = 1.4426950408889634
HEAD_Q_SCALE = HEAD_DIM ** -0.5 * LOG2E
MLA_Q_SCALE = (MLA_NOPE + MLA_ROPE) ** -0.5 * LOG2E


def _params(semantics):
    return pltpu.CompilerParams(dimension_semantics=semantics, vmem_limit_bytes=VMEM_LIMIT)


def _resident(shape):
    return pl.BlockSpec(shape, lambda *_: (0,) * len(shape), pipeline_mode=pl.Buffered(1))


def _alibi_slopes(n_heads):
    return 2.0 ** (-8.0 * jnp.arange(1, n_heads + 1, dtype=F32) / n_heads)


def _rms(x, g):
    return x * lax.rsqrt(jnp.mean(x * x, axis=-1, keepdims=True) + RMS_EPS) * g


def _dot(a, b):
    return jnp.dot(a, b, preferred_element_type=F32)


def _dot_nt(a, b):
    return lax.dot_general(a, b, (((1,), (1,)), ((), ())), preferred_element_type=F32)


def _norm_proj_kernel(h_ref, g_ref, w_ref, o_ref, *, col_tile, scaled_cols, scale):
    n = _rms(h_ref[...], g_ref[...]).astype(BF16)
    for c in range(o_ref.shape[1] // col_tile):
        cols = slice(c * col_tile, (c + 1) * col_tile)
        y = _dot(n, w_ref[:, cols])
        if (c + 1) * col_tile <= scaled_cols:
            y = y * scale
        o_ref[:, cols] = y.astype(o_ref.dtype)


def _norm_proj(h, g, w, col_tile=512, scaled_cols=0, scale=1.0):
    m, d = h.shape
    n_out = w.shape[1]
    assert scaled_cols % col_tile == 0
    return pl.pallas_call(
        functools.partial(_norm_proj_kernel, col_tile=col_tile, scaled_cols=scaled_cols, scale=scale),
        grid=(m // ROW_TILE,),
        in_specs=[pl.BlockSpec((ROW_TILE, d), lambda i: (i, 0)),
                  _resident((1, d)),
                  _resident((d, n_out))],
        out_specs=pl.BlockSpec((ROW_TILE, n_out), lambda i: (i, 0)),
        out_shape=jax.ShapeDtypeStruct((m, n_out), BF16),
        compiler_params=_params(("parallel",)),
        name="norm_proj",
    )(h, g.reshape(1, d), w)


def _dil_proj_kernel(h_ref, g_ref, w_ref, o0_ref, o1_ref, o2_ref, stage_ref, sorted_ref):
    n = _rms(h_ref[...], g_ref[...])
    rows, d = n.shape
    width = DIL_HEADS_PER_GROUP * HEAD_DIM
    for t in range(d // LANES):
        stage_ref[t] = n[:, t * LANES:(t + 1) * LANES]
    n = n.astype(BF16)
    for g, (o_ref, (_, dil)) in enumerate(zip((o0_ref, o1_ref, o2_ref), DIL_PAIRS)):
        per = rows // dil
        for r in range(dil if dil > 1 else 0):
            for t in range(d // LANES):
                sorted_ref[g - 1, r * per:(r + 1) * per, t * LANES:(t + 1) * LANES] = (
                    stage_ref[t, pl.ds(r, per, stride=dil), :].astype(BF16))
        lhs = n if dil == 1 else sorted_ref[g - 1]
        for part in range(3):
            src = (3 * g + part) * width
            y = _dot(lhs, w_ref[:, src:src + width])
            if part == 0:
                y = y * HEAD_Q_SCALE
            for r in range(dil):
                dst = (3 * r + part) * width
                o_ref[:, dst:dst + width] = y[r * per:(r + 1) * per, :].astype(BF16)


def _dil_proj(h, g, w):
    m, d = h.shape
    feat = 3 * DIL_HEADS_PER_GROUP * HEAD_DIM
    assert DIL_PAIRS[0][1] == 1
    view = lambda dil: (m // dil, dil * feat)
    view_block = lambda dil: pl.BlockSpec((ROW_TILE // dil, dil * feat), lambda i: (i, 0))
    return pl.pallas_call(
        _dil_proj_kernel,
        grid=(m // ROW_TILE,),
        in_specs=[pl.BlockSpec((ROW_TILE, d), lambda i: (i, 0)), _resident((1, d)), _resident(w.shape)],
        out_specs=[view_block(dil) for _, dil in DIL_PAIRS],
        out_shape=[jax.ShapeDtypeStruct(view(dil), BF16) for _, dil in DIL_PAIRS],
        scratch_shapes=[pltpu.VMEM((d // LANES, ROW_TILE, LANES), F32),
                        pltpu.VMEM((len(DIL_PAIRS) - 1, ROW_TILE, d), BF16)],
        compiler_params=_params(("parallel",)),
        name="dil_proj",
    )(h, g.reshape(1, d), w)


def _mlp_tail(h1, g_ref, wup_ref, wdn_ref, gf_ref, out_ref):
    n = _rms(h1, g_ref[...]).astype(BF16)
    acc = jnp.zeros_like(h1)
    for c in range(wup_ref.shape[1] // FF_TILE):
        cols = slice(c * FF_TILE, (c + 1) * FF_TILE)
        u = jnp.square(jnp.maximum(_dot(n, wup_ref[:, cols]), 0.0)).astype(BF16)
        acc = acc + _dot(u, wdn_ref[cols, :])
    out = h1 + acc
    if gf_ref is not None:
        out = _rms(out, gf_ref[...])
    out_ref[...] = out


def _post_kernel(h_ref, a_ref, wo_ref, g_ref, wup_ref, wdn_ref, *rest):
    gf_ref, out_ref = (rest[0], rest[1]) if len(rest) == 2 else (None, rest[0])
    h1 = h_ref[...] + _dot(a_ref[...], wo_ref[...])
    _mlp_tail(h1, g_ref, wup_ref, wdn_ref, gf_ref, out_ref)


def _post_merge_kernel(h_ref, o0_ref, o1_ref, o2_ref, l0_ref, l1_ref, l2_ref,
                       wo_ref, g_ref, wup_ref, wdn_ref, out_ref, stage_ref):
    rows, width = o0_ref.shape

    def token_rows(x_ref, slot, dil):
        if dil == 1:
            return x_ref[...]
        for r in range(dil):
            for t in range(width // LANES):
                src = r * width + t * LANES
                stage_ref[slot, t, pl.ds(r, rows // dil, stride=dil), :] = x_ref[:, src:src + LANES]
        return jnp.concatenate([stage_ref[slot, t] for t in range(width // LANES)], axis=1)

    dils = [dil for _, dil in DIL_PAIRS]
    l0, l1, l2 = (token_rows(ref, slot, dil) for slot, (ref, dil) in enumerate(zip((l0_ref, l1_ref, l2_ref), dils)))
    o0, o1, o2 = (token_rows(ref, 3 + slot, dil) for slot, (ref, dil) in enumerate(zip((o0_ref, o1_ref, o2_ref), dils)))
    mx = jnp.maximum(jnp.maximum(l0, l1), l2)
    e0, e1, e2 = jnp.exp(l0 - mx), jnp.exp(l1 - mx), jnp.exp(l2 - mx)
    merged = (e0 * o0 + e1 * o1 + e2 * o2) / (e0 + e1 + e2)
    h1 = h_ref[...] + _dot(merged.astype(BF16), wo_ref[...])
    _mlp_tail(h1, g_ref, wup_ref, wdn_ref, None, out_ref)


def _post(h, attn_ins, w_o, g, w_up, w_down, g_final=None):
    m, d = h.shape
    row = lambda width: pl.BlockSpec((ROW_TILE, width), lambda i: (i, 0))
    merge = len(attn_ins) > 1
    in_specs = [row(d)] + [pl.BlockSpec((ROW_TILE * a.shape[0] // m, a.shape[1]), lambda i: (i, 0)) for a in attn_ins]
    in_specs += [_resident(w_o.shape), _resident((1, d)), _resident(w_up.shape), _resident(w_down.shape)]
    args = [h, *attn_ins, w_o, g.reshape(1, d), w_up, w_down]
    if g_final is not None:
        in_specs.append(_resident((1, d)))
        args.append(g_final.reshape(1, d))
    scratch = [pltpu.VMEM((len(attn_ins), w_o.shape[0] // LANES, ROW_TILE, LANES), F32)] if merge else []
    return pl.pallas_call(
        _post_merge_kernel if merge else _post_kernel,
        grid=(m // ROW_TILE,),
        in_specs=in_specs,
        out_specs=row(d),
        out_shape=jax.ShapeDtypeStruct((m, d), F32),
        scratch_shapes=scratch,
        compiler_params=_params(("parallel",)),
        name="post_mlp",
    )(*args)


ONES_ROWS = 16
SKIP_SHIFT = -2.0 * NEG_INF


class _SoftmaxState:
    def __init__(self, m_ref, acc_ref, dv):
        self.m_ref, self.acc_ref, self.dv = m_ref, acc_ref, dv

    @staticmethod
    def scratch(n_heads, dv, tq):
        return [pltpu.VMEM((n_heads, 1, tq), F32), pltpu.VMEM((n_heads, dv + ONES_ROWS, tq), F32)]

    @staticmethod
    def store_values_t(vt_ref, hh, cols, v_t):
        dv = v_t.shape[0]
        vt_ref[hh, :dv, cols] = v_t.astype(BF16)
        vt_ref[hh, dv:, cols] = jnp.ones((ONES_ROWS, v_t.shape[1]), BF16)

    def init(self):
        self.m_ref[...] = jnp.full(self.m_ref.shape, NEG_INF, F32)
        self.acc_ref[...] = jnp.zeros(self.acc_ref.shape, F32)

    def step(self, hh, x, v_t, shift=None):
        m = self.m_ref[hh]
        m_cur = jnp.max(x, axis=0, keepdims=True)
        if shift is not None:
            m_cur = m_cur - shift
        m_new = jnp.maximum(m, m_cur)
        alpha = jnp.exp2(m - m_new)
        p = jnp.exp2(x - (m_new if shift is None else m_new + shift))
        self.m_ref[hh] = m_new
        self.acc_ref[hh] = alpha * self.acc_ref[hh] + _dot(v_t, p.astype(BF16))

    def normalized(self, hh):
        return self.acc_ref[hh, :self.dv, :] / self.acc_ref[hh, self.dv:self.dv + 1, :]


def _sweep_blocks(n_past, n_heads, state, score_fn, values_fn, past_logits, own_logits, s_a, s_b):
    def produce(buf, block):
        for hh in range(n_heads):
            buf[hh] = score_fn(hh, block)

    def consume(buf, block):
        for hh in range(n_heads):
            x, shift = past_logits(hh, buf[hh], block)
            state.step(hh, x, values_fn(hh, block), shift)

    def consume_own(buf):
        for hh in range(n_heads):
            x, shift = own_logits(hh, buf[hh])
            state.step(hh, x, values_fn(hh, n_past), shift)

    produce(s_b, n_past)
    produce(s_a, 0)
    consume_own(s_b)

    def two_blocks(first):
        produce(s_b, first + 1)
        consume(s_a, first)
        produce(s_a, jnp.minimum(first + 2, n_past - 1))
        consume(s_b, first + 1)

    def quad(t, carry):
        two_blocks(4 * t)
        two_blocks(4 * t + 2)
        return carry

    def pair(t, carry):
        two_blocks(4 * (n_past // 4) + 2 * t)
        return carry

    lax.fori_loop(0, n_past // 4, quad, 0)
    lax.fori_loop(0, (n_past % 4) // 2, pair, 0)

    @pl.when(n_past % 2 == 1)
    def _():
        consume(s_a, n_past - 1)


def _transpose_bf16(x):
    return x.astype(F32).T.astype(BF16)


def _keep_head_rows(x_t, half):
    r = lax.broadcasted_iota(jnp.int32, x_t.shape, 0)
    keep = (r < HEAD_DIM) if half == 0 else (r >= HEAD_DIM)
    return jnp.where(keep, x_t, jnp.zeros_like(x_t))


def _moba_kernel(slopes_ref, q_ref, k_ref, v_ref, o_ref, kmean_ref, vt_ref, term_ref, alibi_ref, qh_ref,
                 s_a, s_b, m_ref, acc_ref, *, n_blocks):
    blk = MOBA_BLOCK
    n_heads = MOBA_HEADS_PER_STEP
    group = pl.program_id(1)
    i = pl.program_id(2)
    state = _SoftmaxState(m_ref, acc_ref, HEAD_DIM)

    @pl.when(i == 0)
    def _():
        for j in range(n_blocks):
            rows = slice(j * blk, (j + 1) * blk)
            kmean_ref[j:j + 1, :] = jnp.mean(k_ref[0, rows, :].astype(F32), axis=0, keepdims=True)
            v_t = v_ref[0, rows, :].astype(F32).T
            for hh in range(n_heads):
                state.store_values_t(vt_ref, hh, rows, v_t[hh * HEAD_DIM:(hh + 1) * HEAD_DIM, :])

    state.init()
    q_t = q_ref[0].astype(F32).T
    key = lax.broadcasted_iota(jnp.int32, (blk, blk), 0)
    qry = lax.broadcasted_iota(jnp.int32, (blk, blk), 1)
    rel = (qry - key).astype(F32)
    blk_idx = lax.broadcasted_iota(jnp.int32, (n_blocks, blk), 0)
    fully_past = blk_idx < i
    block_dist = ((i - blk_idx) * blk).astype(F32)
    lane_tile = lambda hh: slice((hh // 2) * LANES, (hh // 2 + 1) * LANES)

    for hh in range(n_heads):
        slope_s = slopes_ref[n_heads * group + hh] * LOG2E
        q_h = _keep_head_rows(q_t[lane_tile(hh), :], hh % 2)
        gate = lax.dot_general(kmean_ref[:, lane_tile(hh)], q_h, (((1,), (0,)), ((), ())),
                               precision=lax.Precision.HIGHEST, preferred_element_type=F32)
        work = jnp.where(fully_past, gate, NEG_INF * HEAD_Q_SCALE)
        chosen = jnp.zeros((n_blocks, blk), jnp.bool_)
        for _ in range(min(MOBA_TOPK, n_blocks)):
            best = jnp.max(work, axis=0, keepdims=True)
            first = jnp.min(jnp.where(work == best, blk_idx, n_blocks), axis=0, keepdims=True)
            pick = blk_idx == first
            chosen = jnp.logical_or(chosen, pick)
            work = jnp.where(pick, -jnp.inf, work)
        term_ref[hh] = jnp.where(jnp.logical_and(chosen, fully_past), slope_s * block_dist, SKIP_SHIFT)
        qh_ref[hh] = q_h.astype(BF16)
        alibi_ref[hh] = slope_s * rel

    def rows_of(block):
        return pl.ds(pl.multiple_of(block * blk, blk), blk)

    def scores(hh, block):
        return _dot(k_ref[0, rows_of(block), lane_tile(hh)], qh_ref[hh])

    def v_t(hh, block):
        return vt_ref[hh, :, rows_of(block)]

    def past_logits(hh, s, block):
        return s - alibi_ref[hh], term_ref[hh, pl.ds(block, 1), :]

    def own_logits(hh, s):
        return jnp.where(rel >= 0, s - alibi_ref[hh], NEG_INF), None

    _sweep_blocks(i, n_heads, state, scores, v_t, past_logits, own_logits, s_a, s_b)
    o_t = jnp.concatenate([state.normalized(hh) for hh in range(n_heads)], axis=0)
    o_ref[0] = o_t.T.astype(o_ref.dtype)


def _moba_attention(qkv, b, s):
    n_blocks = s // MOBA_BLOCK
    n_heads = MOBA_HEADS_PER_STEP
    width = n_heads * HEAD_DIM
    n_groups = MOBA_HEADS // n_heads
    score_buf = pltpu.VMEM((n_heads, MOBA_BLOCK, MOBA_BLOCK), F32)
    return pl.pallas_call(
        functools.partial(_moba_kernel, n_blocks=n_blocks),
        grid=(b, n_groups, n_blocks),
        in_specs=[pl.BlockSpec(memory_space=pltpu.SMEM),
                  pl.BlockSpec((1, MOBA_BLOCK, width), lambda bi, g, i: (bi, i, g)),
                  pl.BlockSpec((1, s, width), lambda bi, g, i: (bi, 0, n_groups + g)),
                  pl.BlockSpec((1, s, width), lambda bi, g, i: (bi, 0, 2 * n_groups + g))],
        out_specs=pl.BlockSpec((1, MOBA_BLOCK, width), lambda bi, g, i: (bi, i, g)),
        out_shape=jax.ShapeDtypeStruct((b, s, MOBA_HEADS * HEAD_DIM), BF16),
        scratch_shapes=[pltpu.VMEM((n_blocks, width), F32),
                        pltpu.VMEM((n_heads, HEAD_DIM + ONES_ROWS, s), BF16),
                        pltpu.VMEM((n_heads, n_blocks, MOBA_BLOCK), F32),
                        pltpu.VMEM((n_heads, MOBA_BLOCK, MOBA_BLOCK), F32),
                        pltpu.VMEM((n_heads, LANES, MOBA_BLOCK), BF16),
                        score_buf, score_buf,
                        *_SoftmaxState.scratch(n_heads, HEAD_DIM, MOBA_BLOCK)],
        compiler_params=_params(("parallel", "parallel", "arbitrary")),
        name="moba_attention",
    )(_alibi_slopes(MOBA_HEADS), qkv, qkv, qkv)


def _lane_half(x, half):
    lane = lax.broadcasted_iota(jnp.int32, x.shape, x.ndim - 1)
    keep = (lane < HEAD_DIM) if half == 0 else (lane >= HEAD_DIM)
    return jnp.where(keep, x, jnp.zeros_like(x))


BANDS_PER_STEP = 2


def _band_attention(n_heads, q_tile, k_tile, v_tile, logits_fn, floor_fn, store_pair, s_ref, with_lse):
    chains = [(band, h) for band in range(BANDS_PER_STEP) for h in range(n_heads)]
    for c, (band, h) in enumerate(chains):
        s_ref[c] = _dot_nt(k_tile(h, band), _lane_half(q_tile(h, band), h % 2))
    even = None
    for c, (band, h) in enumerate(chains):
        x = logits_fn(h, band, s_ref[c])
        m = jnp.max(x, axis=0, keepdims=True)
        floor = floor_fn(h)
        if floor is not None:
            m = jnp.maximum(m, floor)
        p = jnp.exp2(x - m)
        den = jnp.sum(p, axis=0, keepdims=True)
        if floor is not None:
            den = den + jnp.exp2(floor - m)
        o_t = lax.dot_general(v_tile(h, band), p.astype(BF16), (((0,), (0,)), ((), ())),
                              preferred_element_type=F32)
        half = slice((h % 2) * HEAD_DIM, (h % 2 + 1) * HEAD_DIM)
        o_t = o_t[half, :] / den
        lse_t = jnp.broadcast_to((m + jnp.log2(den)) * (1.0 / LOG2E), o_t.shape) if with_lse else None
        if h % 2 == 0:
            even = (o_t, lse_t)
            continue
        store_pair(h // 2, band, jnp.concatenate([even[0], o_t], axis=0).T,
                   jnp.concatenate([even[1], lse_t], axis=0).T if with_lse else None)


def _band_rows(band, size):
    return slice(band * size, (band + 1) * size)


def _band_keys(prev_ref, cur_ref, band, size, cols):
    if band == 0:
        return jnp.concatenate([prev_ref[0, :, cols], cur_ref[0, :size, cols]], axis=0)
    return cur_ref[0, (band - 1) * size:(band + 1) * size, cols]


def _band_geometry(size, n):
    key = lax.broadcasted_iota(jnp.int32, (2 * size, size), 0)
    qry = lax.broadcasted_iota(jnp.int32, (2 * size, size), 1)
    diff = qry + size - key
    in_sequence = (n > 0) | (key >= size)
    return diff, in_sequence


def _dil_kernel(slopes_ref, q_ref, kp_ref, kc_ref, vp_ref, vc_ref, o_ref, lse_ref, s_ref, *, n_pts, dil, group):
    n = pl.program_id(2)
    diff, in_sequence = _band_geometry(DIL_BAND, n)
    window = (diff >= 0) & (diff <= n_pts)
    masks = [window & in_sequence] + [window] * (BANDS_PER_STEP - 1)
    dist = (dil * diff).astype(F32) * LOG2E
    tile = lambda h: slice((h // 2) * LANES, (h // 2 + 1) * LANES)

    def logits(h, band, s):
        slope = slopes_ref[group * DIL_HEADS_PER_GROUP + h]
        return jnp.where(masks[band], s - slope * dist, NEG_INF)

    def store_pair(pair, band, o, lse):
        rows, cols = _band_rows(band, DIL_BAND), slice(pair * LANES, (pair + 1) * LANES)
        o_ref[0, rows, cols] = o
        lse_ref[0, rows, cols] = lse

    _band_attention(
        DIL_HEADS_PER_GROUP,
        lambda h, band: q_ref[0, _band_rows(band, DIL_BAND), tile(h)],
        lambda h, band: _band_keys(kp_ref, kc_ref, band, DIL_BAND, tile(h)),
        lambda h, band: _band_keys(vp_ref, vc_ref, band, DIL_BAND, tile(h)),
        logits, lambda h: None, store_pair, s_ref, with_lse=True)


def _dilated_group(qkv, b, s, group, window, dil):
    width = DIL_HEADS_PER_GROUP * HEAD_DIM
    n_sub = s // dil
    step_rows = BANDS_PER_STEP * DIL_BAND
    assert n_sub % step_rows == 0
    view = qkv.reshape(b, n_sub, dil * 3 * width)

    def col(part):
        return lambda bi, r, n: (bi, n, r * 3 + part)

    def col_prev(part):
        return lambda bi, r, n: (bi, jnp.maximum(n * BANDS_PER_STEP - 1, 0), r * 3 + part)

    blk, prev_blk = (1, step_rows, width), (1, DIL_BAND, width)
    out_spec = pl.BlockSpec(blk, lambda bi, r, n: (bi, n, r))
    out_sds = jax.ShapeDtypeStruct((b, n_sub, dil * width), F32)
    o, lse = pl.pallas_call(
        functools.partial(_dil_kernel, n_pts=window // dil, dil=dil, group=group),
        grid=(b, dil, n_sub // step_rows),
        in_specs=[pl.BlockSpec(memory_space=pltpu.SMEM),
                  pl.BlockSpec(blk, col(0)),
                  pl.BlockSpec(prev_blk, col_prev(1)), pl.BlockSpec(blk, col(1)),
                  pl.BlockSpec(prev_blk, col_prev(2)), pl.BlockSpec(blk, col(2))],
        out_specs=[out_spec, out_spec],
        out_shape=[out_sds, out_sds],
        scratch_shapes=[pltpu.VMEM((BANDS_PER_STEP * DIL_HEADS_PER_GROUP, 2 * DIL_BAND, DIL_BAND), F32)],
        compiler_params=_params(("parallel", "parallel", "arbitrary")),
        name=f"dilated_attention_g{group}",
    )(_alibi_slopes(len(DIL_PAIRS) * DIL_HEADS_PER_GROUP), view, view, view, view, view)
    return o.reshape(b * n_sub, dil * width), lse.reshape(b * n_sub, dil * width)


def _mla_proj_kernel(h_ref, g_ref, wd_ref, gq_ref, gkv_ref, wuq_ref, wuk_ref, wuv_ref,
                     cos_ref, sin_up_ref, sin_dn_ref, q_ref, k_ref, v_ref):
    n = _rms(h_ref[...], g_ref[...]).astype(BF16)
    c = _dot(n, wd_ref[...])
    nq = _rms(c[:, :MLA_Q_RANK], gq_ref[...]).astype(BF16)
    nkv = _rms(c[:, MLA_Q_RANK:MLA_Q_RANK + MLA_KV_RANK], gkv_ref[...]).astype(BF16)
    k_rope = c[:, MLA_Q_RANK + MLA_KV_RANK:]
    half = MLA_ROPE // 2
    pair = 2

    def rope(x, width):
        cos, sin_up, sin_dn = (jnp.tile(t[...], (1, width)) for t in (cos_ref, sin_up_ref, sin_dn_ref))
        return x * cos + pltpu.roll(x, half, 1) * sin_up + pltpu.roll(x, x.shape[1] - half, 1) * sin_dn

    kr = jnp.tile(rope(k_rope, 1), (1, pair))
    for h in range(0, MLA_HEADS, pair):
        cols = slice(h * LANES, (h + pair) * LANES)
        q_ref[:, cols] = (rope(_dot(nq, wuq_ref[:, cols]), pair) * MLA_Q_SCALE).astype(BF16)
        k_ref[:, cols] = (_dot(nkv, wuk_ref[:, cols]) + kr).astype(BF16)
    v_ref[...] = _dot(nkv, wuv_ref[...]).astype(BF16)


def _mla_proj(h, g, w_dkv, q_norm, w_uq, kv_norm, w_ukv, s):
    m, d = h.shape
    qk = MLA_NOPE + MLA_ROPE
    wd = jnp.concatenate([w_dkv[:, :MLA_Q_RANK + MLA_KV_RANK],
                          jnp.zeros((d, MLA_NOPE), F32), w_dkv[:, MLA_Q_RANK + MLA_KV_RANK:],
                          jnp.zeros((d, LANES - qk), F32)], axis=1).astype(BF16)
    wuq = jnp.pad(w_uq.reshape(MLA_Q_RANK, MLA_HEADS, qk), ((0, 0), (0, 0), (0, LANES - qk)))
    wuq = wuq.reshape(MLA_Q_RANK, MLA_HEADS * LANES).astype(BF16)
    w_ukv = w_ukv.reshape(MLA_KV_RANK, MLA_HEADS, MLA_NOPE + MLA_V)
    wuk = jnp.pad(w_ukv[:, :, :MLA_NOPE], ((0, 0), (0, 0), (0, LANES - MLA_NOPE)))
    wuk = wuk.reshape(MLA_KV_RANK, MLA_HEADS * LANES).astype(BF16)
    wuv = w_ukv[:, :, MLA_NOPE:].reshape(MLA_KV_RANK, MLA_HEADS * MLA_V).astype(BF16)
    half = MLA_ROPE // 2
    inv = ROPE_THETA ** (-jnp.arange(0, MLA_ROPE, 2, dtype=F32) / MLA_ROPE)
    ang = jnp.arange(s).astype(F32)[:, None] * inv[None, :]
    cos, sin = jnp.cos(ang), jnp.sin(ang)
    zeros = lambda w: jnp.zeros((s, w), F32)
    cos_t = jnp.concatenate([jnp.ones((s, MLA_NOPE), F32), cos, cos, zeros(LANES - qk)], axis=1)
    sin_up = jnp.concatenate([zeros(MLA_NOPE + half), sin, zeros(LANES - qk)], axis=1)
    sin_dn = jnp.concatenate([zeros(MLA_NOPE), -sin, zeros(half + LANES - qk)], axis=1)

    tiles_per_seq = s // ROW_TILE
    row = lambda width: pl.BlockSpec((ROW_TILE, width), lambda i: (i, 0))
    table = pl.BlockSpec((ROW_TILE, LANES), lambda i: (i % tiles_per_seq, 0))
    return pl.pallas_call(
        _mla_proj_kernel,
        grid=(m // ROW_TILE,),
        in_specs=[row(d), _resident((1, d)), _resident(wd.shape),
                  _resident((1, MLA_Q_RANK)), _resident((1, MLA_KV_RANK)),
                  _resident(wuq.shape), _resident(wuk.shape), _resident(wuv.shape),
                  table, table, table],
        out_specs=[row(MLA_HEADS * LANES), row(MLA_HEADS * LANES), row(MLA_HEADS * MLA_V)],
        out_shape=[jax.ShapeDtypeStruct((m, MLA_HEADS * LANES), BF16),
                   jax.ShapeDtypeStruct((m, MLA_HEADS * LANES), BF16),
                   jax.ShapeDtypeStruct((m, MLA_HEADS * MLA_V), BF16)],
        compiler_params=_params(("parallel",)),
        name="mla_proj",
    )(h, g.reshape(1, d), wd, q_norm.reshape(1, -1), kv_norm.reshape(1, -1), wuq, wuk, wuv,
      cos_t, sin_up, sin_dn)


def _mla_kernel(q_ref, k_ref, v_ref, o_ref, vt_ref, qh_ref, s_a, s_b, m_ref, acc_ref, *, n_tiles):
    t = MLA_TILE
    n_heads = MLA_HEADS_PER_STEP
    i = pl.program_id(2)
    state = _SoftmaxState(m_ref, acc_ref, MLA_V)

    @pl.when(i == 0)
    def _():
        for j in range(n_tiles):
            rows = slice(j * t, (j + 1) * t)
            v_t = v_ref[0, rows, :].astype(F32).T
            for hh in range(n_heads):
                state.store_values_t(vt_ref, hh, rows, v_t[hh * MLA_V:(hh + 1) * MLA_V, :])

    state.init()
    key = lax.broadcasted_iota(jnp.int32, (t, t), 0)
    qry = lax.broadcasted_iota(jnp.int32, (t, t), 1)
    lane_tile = lambda hh: slice(hh * LANES, (hh + 1) * LANES)
    for hh in range(n_heads):
        qh_ref[hh] = _transpose_bf16(q_ref[0, :, lane_tile(hh)])

    def rows_of(block):
        return pl.ds(pl.multiple_of(block * t, t), t)

    def scores(hh, block):
        return _dot(k_ref[0, rows_of(block), lane_tile(hh)], qh_ref[hh])

    def v_t(hh, block):
        return vt_ref[hh, :, rows_of(block)]

    def past_logits(hh, s, block):
        return s, None

    def own_logits(hh, s):
        return jnp.where(key <= qry, s, NEG_INF), None

    _sweep_blocks(i, n_heads, state, scores, v_t, past_logits, own_logits, s_a, s_b)
    o_t = jnp.concatenate([state.normalized(hh) for hh in range(n_heads)], axis=0)
    o_ref[0] = o_t.T.astype(o_ref.dtype)


def _mla_attention(q, k, v, b, s):
    n_heads = MLA_HEADS_PER_STEP
    n_groups = MLA_HEADS // n_heads
    whole_seq = lambda bi, g, i: (bi, 0, g)
    tile = lambda bi, g, i: (bi, i, g)
    score_buf = pltpu.VMEM((n_heads, MLA_TILE, MLA_TILE), F32)
    return pl.pallas_call(
        functools.partial(_mla_kernel, n_tiles=s // MLA_TILE),
        grid=(b, n_groups, s // MLA_TILE),
        in_specs=[pl.BlockSpec((1, MLA_TILE, n_heads * LANES), tile),
                  pl.BlockSpec((1, s, n_heads * LANES), whole_seq),
                  pl.BlockSpec((1, s, n_heads * MLA_V), whole_seq)],
        out_specs=pl.BlockSpec((1, MLA_TILE, n_heads * MLA_V), tile),
        out_shape=jax.ShapeDtypeStruct((b, s, MLA_HEADS * MLA_V), BF16),
        scratch_shapes=[pltpu.VMEM((n_heads, MLA_V + ONES_ROWS, s), BF16),
                        pltpu.VMEM((n_heads, LANES, MLA_TILE), BF16),
                        score_buf, score_buf,
                        *_SoftmaxState.scratch(n_heads, MLA_V, MLA_TILE)],
        compiler_params=_params(("parallel", "parallel", "arbitrary")),
        name="mla_attention",
    )(q, k, v)


def _swa_kernel(slopes_ref, sinks_ref, q_ref, kp_ref, kc_ref, vp_ref, vc_ref, o_ref, s_ref):
    n = pl.program_id(1)
    group = SWA_Q_HEADS // SWA_KV_HEADS
    diff, in_sequence = _band_geometry(SWA_BLOCK, n)
    window = (diff >= 0) & (diff < SWA_WINDOW)
    masks = [window & in_sequence] + [window] * (BANDS_PER_STEP - 1)
    dist = diff.astype(F32) * LOG2E

    def both_halves(prev_ref, cur_ref, band, kv_head):
        x = _band_keys(prev_ref, cur_ref, band, SWA_BLOCK, slice(None))
        part = x[:, kv_head * HEAD_DIM:(kv_head + 1) * HEAD_DIM]
        return jnp.concatenate([part, part], axis=1)

    bands_kv = [(band, kv) for band in range(BANDS_PER_STEP) for kv in range(SWA_KV_HEADS)]
    k_tiles = {bk: both_halves(kp_ref, kc_ref, *bk) for bk in bands_kv}
    v_tiles = {bk: both_halves(vp_ref, vc_ref, *bk) for bk in bands_kv}

    def logits(h, band, s):
        return jnp.where(masks[band], s - slopes_ref[h] * dist, NEG_INF)

    def sink(h):
        return sinks_ref[h] * LOG2E

    def store_pair(pair, band, o, lse):
        o_ref[0, _band_rows(band, SWA_BLOCK), pair * LANES:(pair + 1) * LANES] = o.astype(o_ref.dtype)

    _band_attention(
        SWA_Q_HEADS,
        lambda h, band: q_ref[0, _band_rows(band, SWA_BLOCK), (h // 2) * LANES:(h // 2 + 1) * LANES],
        lambda h, band: k_tiles[band, h // group],
        lambda h, band: v_tiles[band, h // group],
        logits, sink, store_pair, s_ref, with_lse=False)


def _swa_attention(qkv, sinks, b, s):
    q_width = SWA_Q_HEADS * HEAD_DIM
    kv_width = SWA_KV_HEADS * HEAD_DIM
    k_tile = q_width // kv_width
    step_rows = BANDS_PER_STEP * SWA_BLOCK
    assert s % step_rows == 0
    cur = lambda t: (lambda bi, n: (bi, n, t))
    prev = lambda t: (lambda bi, n: (bi, jnp.maximum(n * BANDS_PER_STEP - 1, 0), t))
    kv_blk, kv_prev_blk = (1, step_rows, kv_width), (1, SWA_BLOCK, kv_width)
    return pl.pallas_call(
        _swa_kernel,
        grid=(b, s // step_rows),
        in_specs=[pl.BlockSpec(memory_space=pltpu.SMEM), pl.BlockSpec(memory_space=pltpu.SMEM),
                  pl.BlockSpec((1, step_rows, q_width), cur(0)),
                  pl.BlockSpec(kv_prev_blk, prev(k_tile)), pl.BlockSpec(kv_blk, cur(k_tile)),
                  pl.BlockSpec(kv_prev_blk, prev(k_tile + 1)), pl.BlockSpec(kv_blk, cur(k_tile + 1))],
        out_specs=pl.BlockSpec((1, step_rows, q_width), cur(0)),
        out_shape=jax.ShapeDtypeStruct((b, s, q_width), BF16),
        scratch_shapes=[pltpu.VMEM((BANDS_PER_STEP * SWA_Q_HEADS, 2 * SWA_BLOCK, SWA_BLOCK), F32)],
        compiler_params=_params(("parallel", "arbitrary")),
        name="swa_attention",
    )(_alibi_slopes(SWA_Q_HEADS), sinks.astype(F32), qkv, qkv, qkv, qkv, qkv)


def kernel(x, l0_attn_norm, l0_w_qkv, l0_w_o, l0_mlp_norm, l0_w_up, l0_w_down, l1_attn_norm, l1_w_qkv, l1_w_o, l1_mlp_norm, l1_w_up, l1_w_down, l2_attn_norm, l2_w_dkv, l2_q_norm, l2_w_uq, l2_kv_norm, l2_w_ukv, l2_w_o, l2_mlp_norm, l2_w_up, l2_w_down, l3_attn_norm, l3_w_qkv, l3_sinks, l3_w_o, l3_mlp_norm, l3_w_up, l3_w_down, final_norm):
    b, s, d = x.shape
    bf = lambda w: w.astype(BF16)
    h = x.reshape(b * s, d)

    qkv = _norm_proj(h, l0_attn_norm, bf(l0_w_qkv), scaled_cols=MOBA_HEADS * HEAD_DIM, scale=HEAD_Q_SCALE)
    a = _moba_attention(qkv.reshape(b, s, -1), b, s).reshape(b * s, -1)
    h = _post(h, (a,), bf(l0_w_o), l0_mlp_norm, bf(l0_w_up), bf(l0_w_down))

    qkvs = _dil_proj(h, l1_attn_norm, bf(l1_w_qkv))
    groups = [_dilated_group(qkvs[g], b, s, g, window, dil) for g, (window, dil) in enumerate(DIL_PAIRS)]
    h = _post(h, tuple(o for o, _ in groups) + tuple(l for _, l in groups),
              bf(l1_w_o), l1_mlp_norm, bf(l1_w_up), bf(l1_w_down))

    q, k, v = _mla_proj(h, l2_attn_norm, l2_w_dkv, l2_q_norm, l2_w_uq, l2_kv_norm, l2_w_ukv, s)
    a = _mla_attention(q.reshape(b, s, -1), k.reshape(b, s, -1), v.reshape(b, s, -1), b, s).reshape(b * s, -1)
    h = _post(h, (a,), bf(l2_w_o), l2_mlp_norm, bf(l2_w_up), bf(l2_w_down))

    qkv = _norm_proj(h, l3_attn_norm, bf(l3_w_qkv), col_tile=256, scaled_cols=SWA_Q_HEADS * HEAD_DIM,
                     scale=HEAD_Q_SCALE)
    a = _swa_attention(qkv.reshape(b, s, -1), l3_sinks, b, s).reshape(b * s, -1)
    h = _post(h, (a,), bf(l3_w_o), l3_mlp_norm, bf(l3_w_up), bf(l3_w_down), g_final=final_norm)
    return h.reshape(b, s, d)
```

```python
import functools

import jax
import jax.numpy as jnp
from jax import lax
from jax.experimental import pallas as pl
from jax.experimental.pallas import tpu as pltpu

F32 = jnp.float32
BF16 = jnp.bfloat16

D_MODEL = 1024
HEAD_DIM = 64
RMS_EPS = 1e-6
D_FF = 4 * D_MODEL
NEG_INF = -1e30

MOBA_HEADS = 16
MOBA_BLOCK = 256
MOBA_TOPK = 3

DIL_PAIRS = ((128, 1), (512, 4), (2048, 16))
DIL_HEADS_PER_GROUP = 8
DIL_BAND = 128

MLA_HEADS = 16
MLA_Q_RANK = 768
MLA_KV_RANK = 256
MLA_NOPE = 64
MLA_ROPE = 32
MLA_V = 64
ROPE_THETA = 10000.0
MLA_TILE = 256
MLA_Q_TILES = 2
MLA_HEADS_PER_STEP = 8
MOBA_HEADS_PER_STEP = 8

SWA_Q_HEADS = 16
SWA_KV_HEADS = 2
SWA_WINDOW = 128
SWA_BLOCK = 128

LANES = 128
ROW_TILE = 512
FF_TILE = 512
VMEM_LIMIT = 56 * 1024 * 1024
LOG2E = 1.4426950408889634
HEAD_Q_SCALE = HEAD_DIM ** -0.5 * LOG2E
MLA_Q_SCALE = (MLA_NOPE + MLA_ROPE) ** -0.5 * LOG2E


def _params(semantics):
    return pltpu.CompilerParams(dimension_semantics=semantics, vmem_limit_bytes=VMEM_LIMIT)


def _resident(shape):
    return pl.BlockSpec(shape, lambda *_: (0,) * len(shape), pipeline_mode=pl.Buffered(1))


def _alibi_slopes(n_heads):
    return 2.0 ** (-8.0 * jnp.arange(1, n_heads + 1, dtype=F32) / n_heads)


def _rms(x, g):
    return x * lax.rsqrt(jnp.mean(x * x, axis=-1, keepdims=True) + RMS_EPS) * g


def _dot(a, b):
    return jnp.dot(a, b, preferred_element_type=F32)


def _dot_nt(a, b):
    return lax.dot_general(a, b, (((1,), (1,)), ((), ())), preferred_element_type=F32)


def _norm_proj_kernel(h_ref, g_ref, w_ref, o_ref, *, col_tile, scaled_cols, scale):
    n = _rms(h_ref[...], g_ref[...]).astype(BF16)
    for c in range(o_ref.shape[1] // col_tile):
        cols = slice(c * col_tile, (c + 1) * col_tile)
        y = _dot(n, w_ref[:, cols])
        if (c + 1) * col_tile <= scaled_cols:
            y = y * scale
        o_ref[:, cols] = y.astype(o_ref.dtype)


def _norm_proj(h, g, w, col_tile=512, scaled_cols=0, scale=1.0):
    m, d = h.shape
    n_out = w.shape[1]
    assert scaled_cols % col_tile == 0
    return pl.pallas_call(
        functools.partial(_norm_proj_kernel, col_tile=col_tile, scaled_cols=scaled_cols, scale=scale),
        grid=(m // ROW_TILE,),
        in_specs=[pl.BlockSpec((ROW_TILE, d), lambda i: (i, 0)),
                  _resident((1, d)),
                  _resident((d, n_out))],
        out_specs=pl.BlockSpec((ROW_TILE, n_out), lambda i: (i, 0)),
        out_shape=jax.ShapeDtypeStruct((m, n_out), BF16),
        compiler_params=_params(("parallel",)),
        name="norm_proj",
    )(h, g.reshape(1, d), w)


def _dil_proj_kernel(h_ref, g_ref, w_ref, o0_ref, o1_ref, o2_ref, stage_ref, sorted_ref):
    n = _rms(h_ref[...], g_ref[...])
    rows, d = n.shape
    width = DIL_HEADS_PER_GROUP * HEAD_DIM
    for t in range(d // LANES):
        stage_ref[t] = n[:, t * LANES:(t + 1) * LANES]
    n = n.astype(BF16)
    for g, (o_ref, (_, dil)) in enumerate(zip((o0_ref, o1_ref, o2_ref), DIL_PAIRS)):
        per = rows // dil
        for r in range(dil if dil > 1 else 0):
            for t in range(d // LANES):
                sorted_ref[g - 1, r * per:(r + 1) * per, t * LANES:(t + 1) * LANES] = (
                    stage_ref[t, pl.ds(r, per, stride=dil), :].astype(BF16))
        lhs = n if dil == 1 else sorted_ref[g - 1]
        for part in range(3):
            src = (3 * g + part) * width
            y = _dot(lhs, w_ref[:, src:src + width])
            if part == 0:
                y = y * HEAD_Q_SCALE
            for r in range(dil):
                dst = (3 * r + part) * width
                o_ref[:, dst:dst + width] = y[r * per:(r + 1) * per, :].astype(BF16)


def _dil_proj(h, g, w):
    m, d = h.shape
    feat = 3 * DIL_HEADS_PER_GROUP * HEAD_DIM
    assert DIL_PAIRS[0][1] == 1
    view = lambda dil: (m // dil, dil * feat)
    view_block = lambda dil: pl.BlockSpec((ROW_TILE // dil, dil * feat), lambda i: (i, 0))
    return pl.pallas_call(
        _dil_proj_kernel,
        grid=(m // ROW_TILE,),
        in_specs=[pl.BlockSpec((ROW_TILE, d), lambda i: (i, 0)), _resident((1, d)), _resident(w.shape)],
        out_specs=[view_block(dil) for _, dil in DIL_PAIRS],
        out_shape=[jax.ShapeDtypeStruct(view(dil), BF16) for _, dil in DIL_PAIRS],
        scratch_shapes=[pltpu.VMEM((d // LANES, ROW_TILE, LANES), F32),
                        pltpu.VMEM((len(DIL_PAIRS) - 1, ROW_TILE, d), BF16)],
        compiler_params=_params(("parallel",)),
        name="dil_proj",
    )(h, g.reshape(1, d), w)


def _mlp_tail(h1, g_ref, wup_ref, wdn_ref, gf_ref, out_ref):
    n = _rms(h1, g_ref[...]).astype(BF16)
    acc = jnp.zeros_like(h1)
    for c in range(wup_ref.shape[1] // FF_TILE):
        cols = slice(c * FF_TILE, (c + 1) * FF_TILE)
        u = jnp.square(jnp.maximum(_dot(n, wup_ref[:, cols]), 0.0)).astype(BF16)
        acc = acc + _dot(u, wdn_ref[cols, :])
    out = h1 + acc
    if gf_ref is not None:
        out = _rms(out, gf_ref[...])
    out_ref[...] = out


def _post_kernel(h_ref, a_ref, wo_ref, g_ref, wup_ref, wdn_ref, *rest):
    gf_ref, out_ref = (rest[0], rest[1]) if len(rest) == 2 else (None, rest[0])
    h1 = h_ref[...] + _dot(a_ref[...], wo_ref[...])
    _mlp_tail(h1, g_ref, wup_ref, wdn_ref, gf_ref, out_ref)


def _post_merge_kernel(h_ref, o0_ref, o1_ref, o2_ref, l0_ref, l1_ref, l2_ref,
                       wo_ref, g_ref, wup_ref, wdn_ref, out_ref, stage_ref):
    rows, width = o0_ref.shape

    def token_rows(x_ref, slot, dil):
        if dil == 1:
            return x_ref[...]
        for r in range(dil):
            for t in range(width // LANES):
                src = r * width + t * LANES
                stage_ref[slot, t, pl.ds(r, rows // dil, stride=dil), :] = x_ref[:, src:src + LANES]
        return jnp.concatenate([stage_ref[slot, t] for t in range(width // LANES)], axis=1)

    dils = [dil for _, dil in DIL_PAIRS]
    l0, l1, l2 = (token_rows(ref, slot, dil) for slot, (ref, dil) in enumerate(zip((l0_ref, l1_ref, l2_ref), dils)))
    o0, o1, o2 = (token_rows(ref, 3 + slot, dil) for slot, (ref, dil) in enumerate(zip((o0_ref, o1_ref, o2_ref), dils)))
    mx = jnp.maximum(jnp.maximum(l0, l1), l2)
    e0, e1, e2 = jnp.exp(l0 - mx), jnp.exp(l1 - mx), jnp.exp(l2 - mx)
    merged = (e0 * o0 + e1 * o1 + e2 * o2) / (e0 + e1 + e2)
    h1 = h_ref[...] + _dot(merged.astype(BF16), wo_ref[...])
    _mlp_tail(h1, g_ref, wup_ref, wdn_ref, None, out_ref)


def _post(h, attn_ins, w_o, g, w_up, w_down, g_final=None):
    m, d = h.shape
    row = lambda width: pl.BlockSpec((ROW_TILE, width), lambda i: (i, 0))
    merge = len(attn_ins) > 1
    in_specs = [row(d)] + [pl.BlockSpec((ROW_TILE * a.shape[0] // m, a.shape[1]), lambda i: (i, 0)) for a in attn_ins]
    in_specs += [_resident(w_o.shape), _resident((1, d)), _resident(w_up.shape), _resident(w_down.shape)]
    args = [h, *attn_ins, w_o, g.reshape(1, d), w_up, w_down]
    if g_final is not None:
        in_specs.append(_resident((1, d)))
        args.append(g_final.reshape(1, d))
    scratch = [pltpu.VMEM((len(attn_ins), w_o.shape[0] // LANES, ROW_TILE, LANES), F32)] if merge else []
    return pl.pallas_call(
        _post_merge_kernel if merge else _post_kernel,
        grid=(m // ROW_TILE,),
        in_specs=in_specs,
        out_specs=row(d),
        out_shape=jax.ShapeDtypeStruct((m, d), F32),
        scratch_shapes=scratch,
        compiler_params=_params(("parallel",)),
        name="post_mlp",
    )(*args)


ONES_ROWS = 16
SKIP_SHIFT = -2.0 * NEG_INF


class _SoftmaxState:
    def __init__(self, m_ref, acc_ref, dv):
        self.m_ref, self.acc_ref, self.dv = m_ref, acc_ref, dv

    @staticmethod
    def scratch(n_heads, dv, tq):
        return [pltpu.VMEM((n_heads, 1, tq), F32), pltpu.VMEM((n_heads, dv + ONES_ROWS, tq), F32)]

    @staticmethod
    def store_values_t(vt_ref, hh, cols, v_t):
        dv = v_t.shape[0]
        vt_ref[hh, :dv, cols] = v_t.astype(BF16)
        vt_ref[hh, dv:, cols] = jnp.ones((ONES_ROWS, v_t.shape[1]), BF16)

    def init(self):
        self.m_ref[...] = jnp.full(self.m_ref.shape, NEG_INF, F32)
        self.acc_ref[...] = jnp.zeros(self.acc_ref.shape, F32)

    def step(self, hh, x, v_t, shift=None):
        m = self.m_ref[hh]
        m_cur = jnp.max(x, axis=0, keepdims=True)
        if shift is not None:
            m_cur = m_cur - shift
        m_new = jnp.maximum(m, m_cur)
        alpha = jnp.exp2(m - m_new)
        p = jnp.exp2(x - (m_new if shift is None else m_new + shift))
        self.m_ref[hh] = m_new
        self.acc_ref[hh] = alpha * self.acc_ref[hh] + _dot(v_t, p.astype(BF16))

    def normalized(self, hh):
        return self.acc_ref[hh, :self.dv, :] / self.acc_ref[hh, self.dv:self.dv + 1, :]


def _sweep_blocks(n_past, n_chains, state, score_fn, values_fn, past_logits, own_steps, s_a, s_b):
    def produce(buf, block, chains=range(n_chains)):
        for c in chains:
            buf[c] = score_fn(c, block)

    def consume(buf, block):
        for c in range(n_chains):
            x, shift = past_logits(c, buf[c], block)
            state.step(c, x, values_fn(c, block), shift)

    produced = set()
    for buf, block, chains, _ in own_steps:
        produce(buf, block, [c for c in chains if (id(buf), c) not in produced])
        produced.update((id(buf), c) for c in chains)
    produce(s_a, 0)
    for buf, block, chains, logits_fn in own_steps:
        for c in chains:
            x, shift = logits_fn(c, buf[c])
            state.step(c, x, values_fn(c, block), shift)

    def two_blocks(first):
        produce(s_b, first + 1)
        consume(s_a, first)
        produce(s_a, jnp.minimum(first + 2, n_past - 1))
        consume(s_b, first + 1)

    def quad(t, carry):
        two_blocks(4 * t)
        two_blocks(4 * t + 2)
        return carry

    def pair(t, carry):
        two_blocks(4 * (n_past // 4) + 2 * t)
        return carry

    lax.fori_loop(0, n_past // 4, quad, 0)
    lax.fori_loop(0, (n_past % 4) // 2, pair, 0)

    @pl.when(n_past % 2 == 1)
    def _():
        consume(s_a, n_past - 1)


def _transpose_bf16(x):
    return x.astype(F32).T.astype(BF16)


def _keep_head_rows(x_t, half):
    r = lax.broadcasted_iota(jnp.int32, x_t.shape, 0)
    keep = (r < HEAD_DIM) if half == 0 else (r >= HEAD_DIM)
    return jnp.where(keep, x_t, jnp.zeros_like(x_t))


def _moba_kernel(slopes_ref, q_ref, k_ref, v_ref, o_ref, kmean_ref, vt_ref, term_ref, alibi_ref, qh_ref,
                 s_a, s_b, m_ref, acc_ref, *, n_blocks):
    blk = MOBA_BLOCK
    n_heads = MOBA_HEADS_PER_STEP
    group = pl.program_id(1)
    i = pl.program_id(2)
    state = _SoftmaxState(m_ref, acc_ref, HEAD_DIM)

    @pl.when(i == 0)
    def _():
        for j in range(n_blocks):
            rows = slice(j * blk, (j + 1) * blk)
            kmean_ref[j:j + 1, :] = jnp.mean(k_ref[0, rows, :].astype(F32), axis=0, keepdims=True)
            v_t = v_ref[0, rows, :].astype(F32).T
            for hh in range(n_heads):
                state.store_values_t(vt_ref, hh, rows, v_t[hh * HEAD_DIM:(hh + 1) * HEAD_DIM, :])

    state.init()
    q_t = q_ref[0].astype(F32).T
    key = lax.broadcasted_iota(jnp.int32, (blk, blk), 0)
    qry = lax.broadcasted_iota(jnp.int32, (blk, blk), 1)
    rel = (qry - key).astype(F32)
    blk_idx = lax.broadcasted_iota(jnp.int32, (n_blocks, blk), 0)
    fully_past = blk_idx < i
    block_dist = ((i - blk_idx) * blk).astype(F32)
    lane_tile = lambda hh: slice((hh // 2) * LANES, (hh // 2 + 1) * LANES)

    for hh in range(n_heads):
        slope_s = slopes_ref[n_heads * group + hh] * LOG2E
        q_h = _keep_head_rows(q_t[lane_tile(hh), :], hh % 2)
        gate = lax.dot_general(kmean_ref[:, lane_tile(hh)], q_h, (((1,), (0,)), ((), ())),
                               precision=lax.Precision.HIGHEST, preferred_element_type=F32)
        work = jnp.where(fully_past, gate, NEG_INF * HEAD_Q_SCALE)
        chosen = jnp.zeros((n_blocks, blk), jnp.bool_)
        for _ in range(min(MOBA_TOPK, n_blocks)):
            best = jnp.max(work, axis=0, keepdims=True)
            first = jnp.min(jnp.where(work == best, blk_idx, n_blocks), axis=0, keepdims=True)
            pick = blk_idx == first
            chosen = jnp.logical_or(chosen, pick)
            work = jnp.where(pick, -jnp.inf, work)
        term_ref[hh] = jnp.where(jnp.logical_and(chosen, fully_past), slope_s * block_dist, SKIP_SHIFT)
        qh_ref[hh] = q_h.astype(BF16)
        alibi_ref[hh] = slope_s * rel

    def rows_of(block):
        return pl.ds(pl.multiple_of(block * blk, blk), blk)

    def scores(hh, block):
        return _dot(k_ref[0, rows_of(block), lane_tile(hh)], qh_ref[hh])

    def v_t(hh, block):
        return vt_ref[hh, :, rows_of(block)]

    def past_logits(hh, s, block):
        return s - alibi_ref[hh], term_ref[hh, pl.ds(block, 1), :]

    def own_logits(hh, s):
        return jnp.where(rel >= 0, s - alibi_ref[hh], NEG_INF), None

    _sweep_blocks(i, n_heads, state, scores, v_t, past_logits, [(s_b, i, range(n_heads), own_logits)], s_a, s_b)
    o_t = jnp.concatenate([state.normalized(hh) for hh in range(n_heads)], axis=0)
    o_ref[0] = o_t.T.astype(o_ref.dtype)


def _moba_attention(qkv, b, s):
    n_blocks = s // MOBA_BLOCK
    n_heads = MOBA_HEADS_PER_STEP
    width = n_heads * HEAD_DIM
    n_groups = MOBA_HEADS // n_heads
    score_buf = pltpu.VMEM((n_heads, MOBA_BLOCK, MOBA_BLOCK), F32)
    return pl.pallas_call(
        functools.partial(_moba_kernel, n_blocks=n_blocks),
        grid=(b, n_groups, n_blocks),
        in_specs=[pl.BlockSpec(memory_space=pltpu.SMEM),
                  pl.BlockSpec((1, MOBA_BLOCK, width), lambda bi, g, i: (bi, i, g)),
                  pl.BlockSpec((1, s, width), lambda bi, g, i: (bi, 0, n_groups + g)),
                  pl.BlockSpec((1, s, width), lambda bi, g, i: (bi, 0, 2 * n_groups + g))],
        out_specs=pl.BlockSpec((1, MOBA_BLOCK, width), lambda bi, g, i: (bi, i, g)),
        out_shape=jax.ShapeDtypeStruct((b, s, MOBA_HEADS * HEAD_DIM), BF16),
        scratch_shapes=[pltpu.VMEM((n_blocks, width), F32),
                        pltpu.VMEM((n_heads, HEAD_DIM + ONES_ROWS, s), BF16),
                        pltpu.VMEM((n_heads, n_blocks, MOBA_BLOCK), F32),
                        pltpu.VMEM((n_heads, MOBA_BLOCK, MOBA_BLOCK), F32),
                        pltpu.VMEM((n_heads, LANES, MOBA_BLOCK), BF16),
                        score_buf, score_buf,
                        *_SoftmaxState.scratch(n_heads, HEAD_DIM, MOBA_BLOCK)],
        compiler_params=_params(("parallel", "parallel", "arbitrary")),
        name="moba_attention",
    )(_alibi_slopes(MOBA_HEADS), qkv, qkv, qkv)


def _lane_half(x, half):
    lane = lax.broadcasted_iota(jnp.int32, x.shape, x.ndim - 1)
    keep = (lane < HEAD_DIM) if half == 0 else (lane >= HEAD_DIM)
    return jnp.where(keep, x, jnp.zeros_like(x))


BANDS_PER_STEP = 2


def _band_attention(n_heads, q_tile, k_tile, v_tile, logits_fn, floor_fn, store_pair, s_ref, with_lse):
    chains = [(band, h) for band in range(BANDS_PER_STEP) for h in range(n_heads)]
    for c, (band, h) in enumerate(chains):
        s_ref[c] = _dot_nt(k_tile(h, band), _lane_half(q_tile(h, band), h % 2))
    even = None
    for c, (band, h) in enumerate(chains):
        x = logits_fn(h, band, s_ref[c])
        m = jnp.max(x, axis=0, keepdims=True)
        floor = floor_fn(h)
        if floor is not None:
            m = jnp.maximum(m, floor)
        p = jnp.exp2(x - m)
        den = jnp.sum(p, axis=0, keepdims=True)
        if floor is not None:
            den = den + jnp.exp2(floor - m)
        o_t = lax.dot_general(v_tile(h, band), p.astype(BF16), (((0,), (0,)), ((), ())),
                              preferred_element_type=F32)
        half = slice((h % 2) * HEAD_DIM, (h % 2 + 1) * HEAD_DIM)
        o_t = o_t[half, :] / den
        lse_t = jnp.broadcast_to((m + jnp.log2(den)) * (1.0 / LOG2E), o_t.shape) if with_lse else None
        if h % 2 == 0:
            even = (o_t, lse_t)
            continue
        store_pair(h // 2, band, jnp.concatenate([even[0], o_t], axis=0).T,
                   jnp.concatenate([even[1], lse_t], axis=0).T if with_lse else None)


def _band_rows(band, size):
    return slice(band * size, (band + 1) * size)


def _band_keys(prev_ref, cur_ref, band, size, cols):
    if band == 0:
        return jnp.concatenate([prev_ref[0, :, cols], cur_ref[0, :size, cols]], axis=0)
    return cur_ref[0, (band - 1) * size:(band + 1) * size, cols]


def _band_geometry(size, n):
    key = lax.broadcasted_iota(jnp.int32, (2 * size, size), 0)
    qry = lax.broadcasted_iota(jnp.int32, (2 * size, size), 1)
    diff = qry + size - key
    in_sequence = (n > 0) | (key >= size)
    return diff, in_sequence


def _dil_kernel(slopes_ref, q_ref, kp_ref, kc_ref, vp_ref, vc_ref, o_ref, lse_ref, s_ref, *, n_pts, dil, group):
    n = pl.program_id(2)
    diff, in_sequence = _band_geometry(DIL_BAND, n)
    window = (diff >= 0) & (diff <= n_pts)
    masks = [window & in_sequence] + [window] * (BANDS_PER_STEP - 1)
    dist = (dil * diff).astype(F32) * LOG2E
    tile = lambda h: slice((h // 2) * LANES, (h // 2 + 1) * LANES)

    def logits(h, band, s):
        slope = slopes_ref[group * DIL_HEADS_PER_GROUP + h]
        return jnp.where(masks[band], s - slope * dist, NEG_INF)

    def store_pair(pair, band, o, lse):
        rows, cols = _band_rows(band, DIL_BAND), slice(pair * LANES, (pair + 1) * LANES)
        o_ref[0, rows, cols] = o
        lse_ref[0, rows, cols] = lse

    _band_attention(
        DIL_HEADS_PER_GROUP,
        lambda h, band: q_ref[0, _band_rows(band, DIL_BAND), tile(h)],
        lambda h, band: _band_keys(kp_ref, kc_ref, band, DIL_BAND, tile(h)),
        lambda h, band: _band_keys(vp_ref, vc_ref, band, DIL_BAND, tile(h)),
        logits, lambda h: None, store_pair, s_ref, with_lse=True)


def _dilated_group(qkv, b, s, group, window, dil):
    width = DIL_HEADS_PER_GROUP * HEAD_DIM
    n_sub = s // dil
    step_rows = BANDS_PER_STEP * DIL_BAND
    assert n_sub % step_rows == 0
    view = qkv.reshape(b, n_sub, dil * 3 * width)

    def col(part):
        return lambda bi, r, n: (bi, n, r * 3 + part)

    def col_prev(part):
        return lambda bi, r, n: (bi, jnp.maximum(n * BANDS_PER_STEP - 1, 0), r * 3 + part)

    blk, prev_blk = (1, step_rows, width), (1, DIL_BAND, width)
    out_spec = pl.BlockSpec(blk, lambda bi, r, n: (bi, n, r))
    out_sds = jax.ShapeDtypeStruct((b, n_sub, dil * width), F32)
    o, lse = pl.pallas_call(
        functools.partial(_dil_kernel, n_pts=window // dil, dil=dil, group=group),
        grid=(b, dil, n_sub // step_rows),
        in_specs=[pl.BlockSpec(memory_space=pltpu.SMEM),
                  pl.BlockSpec(blk, col(0)),
                  pl.BlockSpec(prev_blk, col_prev(1)), pl.BlockSpec(blk, col(1)),
                  pl.BlockSpec(prev_blk, col_prev(2)), pl.BlockSpec(blk, col(2))],
        out_specs=[out_spec, out_spec],
        out_shape=[out_sds, out_sds],
        scratch_shapes=[pltpu.VMEM((BANDS_PER_STEP * DIL_HEADS_PER_GROUP, 2 * DIL_BAND, DIL_BAND), F32)],
        compiler_params=_params(("parallel", "parallel", "arbitrary")),
        name=f"dilated_attention_g{group}",
    )(_alibi_slopes(len(DIL_PAIRS) * DIL_HEADS_PER_GROUP), view, view, view, view, view)
    return o.reshape(b * n_sub, dil * width), lse.reshape(b * n_sub, dil * width)


def _mla_proj_kernel(h_ref, g_ref, wd_ref, gq_ref, gkv_ref, wuq_ref, wuk_ref, wuv_ref,
                     cos_ref, sin_up_ref, sin_dn_ref, q_ref, k_ref, v_ref):
    n = _rms(h_ref[...], g_ref[...]).astype(BF16)
    c = _dot(n, wd_ref[...])
    nq = _rms(c[:, :MLA_Q_RANK], gq_ref[...]).astype(BF16)
    nkv = _rms(c[:, MLA_Q_RANK:MLA_Q_RANK + MLA_KV_RANK], gkv_ref[...]).astype(BF16)
    k_rope = c[:, MLA_Q_RANK + MLA_KV_RANK:]
    half = MLA_ROPE // 2
    pair = 2

    def rope(x, width):
        cos, sin_up, sin_dn = (jnp.tile(t[...], (1, width)) for t in (cos_ref, sin_up_ref, sin_dn_ref))
        return x * cos + pltpu.roll(x, half, 1) * sin_up + pltpu.roll(x, x.shape[1] - half, 1) * sin_dn

    kr = jnp.tile(rope(k_rope, 1), (1, pair))
    for h in range(0, MLA_HEADS, pair):
        cols = slice(h * LANES, (h + pair) * LANES)
        q_ref[:, cols] = (rope(_dot(nq, wuq_ref[:, cols]), pair) * MLA_Q_SCALE).astype(BF16)
        k_ref[:, cols] = (_dot(nkv, wuk_ref[:, cols]) + kr).astype(BF16)
    v_ref[...] = _dot(nkv, wuv_ref[...]).astype(BF16)


def _mla_proj(h, g, w_dkv, q_norm, w_uq, kv_norm, w_ukv, s):
    m, d = h.shape
    qk = MLA_NOPE + MLA_ROPE
    wd = jnp.concatenate([w_dkv[:, :MLA_Q_RANK + MLA_KV_RANK],
                          jnp.zeros((d, MLA_NOPE), F32), w_dkv[:, MLA_Q_RANK + MLA_KV_RANK:],
                          jnp.zeros((d, LANES - qk), F32)], axis=1).astype(BF16)
    wuq = jnp.pad(w_uq.reshape(MLA_Q_RANK, MLA_HEADS, qk), ((0, 0), (0, 0), (0, LANES - qk)))
    wuq = wuq.reshape(MLA_Q_RANK, MLA_HEADS * LANES).astype(BF16)
    w_ukv = w_ukv.reshape(MLA_KV_RANK, MLA_HEADS, MLA_NOPE + MLA_V)
    wuk = jnp.pad(w_ukv[:, :, :MLA_NOPE], ((0, 0), (0, 0), (0, LANES - MLA_NOPE)))
    wuk = wuk.reshape(MLA_KV_RANK, MLA_HEADS * LANES).astype(BF16)
    wuv = w_ukv[:, :, MLA_NOPE:].reshape(MLA_KV_RANK, MLA_HEADS * MLA_V).astype(BF16)
    half = MLA_ROPE // 2
    inv = ROPE_THETA ** (-jnp.arange(0, MLA_ROPE, 2, dtype=F32) / MLA_ROPE)
    ang = jnp.arange(s).astype(F32)[:, None] * inv[None, :]
    cos, sin = jnp.cos(ang), jnp.sin(ang)
    zeros = lambda w: jnp.zeros((s, w), F32)
    cos_t = jnp.concatenate([jnp.ones((s, MLA_NOPE), F32), cos, cos, zeros(LANES - qk)], axis=1)
    sin_up = jnp.concatenate([zeros(MLA_NOPE + half), sin, zeros(LANES - qk)], axis=1)
    sin_dn = jnp.concatenate([zeros(MLA_NOPE), -sin, zeros(half + LANES - qk)], axis=1)

    tiles_per_seq = s // ROW_TILE
    row = lambda width: pl.BlockSpec((ROW_TILE, width), lambda i: (i, 0))
    table = pl.BlockSpec((ROW_TILE, LANES), lambda i: (i % tiles_per_seq, 0))
    return pl.pallas_call(
        _mla_proj_kernel,
        grid=(m // ROW_TILE,),
        in_specs=[row(d), _resident((1, d)), _resident(wd.shape),
                  _resident((1, MLA_Q_RANK)), _resident((1, MLA_KV_RANK)),
                  _resident(wuq.shape), _resident(wuk.shape), _resident(wuv.shape),
                  table, table, table],
        out_specs=[row(MLA_HEADS * LANES), row(MLA_HEADS * LANES), row(MLA_HEADS * MLA_V)],
        out_shape=[jax.ShapeDtypeStruct((m, MLA_HEADS * LANES), BF16),
                   jax.ShapeDtypeStruct((m, MLA_HEADS * LANES), BF16),
                   jax.ShapeDtypeStruct((m, MLA_HEADS * MLA_V), BF16)],
        compiler_params=_params(("parallel",)),
        name="mla_proj",
    )(h, g.reshape(1, d), wd, q_norm.reshape(1, -1), kv_norm.reshape(1, -1), wuq, wuk, wuv,
      cos_t, sin_up, sin_dn)


def _mla_kernel(q_ref, k_ref, v_ref, o_ref, vt_ref, qh_ref, s_a, s_b, s_c, m_ref, acc_ref, *, n_tiles):
    t = MLA_TILE
    n_heads = MLA_HEADS_PER_STEP
    i = pl.program_id(2)
    state = _SoftmaxState(m_ref, acc_ref, MLA_V)
    chain = lambda tile, hh: tile * n_heads + hh
    tile_chains = [[chain(tile, hh) for hh in range(n_heads)] for tile in range(MLA_Q_TILES)]

    @pl.when(i == 0)
    def _():
        for j in range(n_tiles):
            rows = slice(j * t, (j + 1) * t)
            v_t = v_ref[0, rows, :].astype(F32).T
            for hh in range(n_heads):
                state.store_values_t(vt_ref, hh, rows, v_t[hh * MLA_V:(hh + 1) * MLA_V, :])

    state.init()
    key = lax.broadcasted_iota(jnp.int32, (t, t), 0)
    qry = lax.broadcasted_iota(jnp.int32, (t, t), 1)
    lane_tile = lambda hh: slice(hh * LANES, (hh + 1) * LANES)
    for tile in range(MLA_Q_TILES):
        for hh in range(n_heads):
            qh_ref[chain(tile, hh)] = _transpose_bf16(q_ref[0, tile * t:(tile + 1) * t, lane_tile(hh)])

    def rows_of(block):
        return pl.ds(pl.multiple_of(block * t, t), t)

    def scores(c, block):
        return _dot(k_ref[0, rows_of(block), lane_tile(c % n_heads)], qh_ref[c])

    def v_t(c, block):
        return vt_ref[c % n_heads, :, rows_of(block)]

    def past_logits(c, s, block):
        return s, None

    def causal(c, s):
        return jnp.where(key <= qry, s, NEG_INF), None

    def full(c, s):
        return s, None

    assert MLA_Q_TILES == 2
    first = MLA_Q_TILES * i
    own_steps = [(s_b, first, tile_chains[0], causal), (s_b, first, tile_chains[1], full),
                 (s_c, first + 1, tile_chains[1], causal)]
    _sweep_blocks(first, MLA_Q_TILES * n_heads, state, scores, v_t, past_logits, own_steps, s_a, s_b)
    for tile in range(MLA_Q_TILES):
        o_t = jnp.concatenate([state.normalized(c) for c in tile_chains[tile]], axis=0)
        o_ref[0, tile * t:(tile + 1) * t, :] = o_t.T.astype(o_ref.dtype)


def _mla_attention(q, k, v, b, s):
    n_heads = MLA_HEADS_PER_STEP
    n_groups = MLA_HEADS // n_heads
    whole_seq = lambda bi, g, i: (bi, 0, g)
    tile = lambda bi, g, i: (bi, i, g)
    n_chains = MLA_Q_TILES * n_heads
    step_rows = MLA_Q_TILES * MLA_TILE
    assert s % step_rows == 0
    score_buf = pltpu.VMEM((n_chains, MLA_TILE, MLA_TILE), F32)
    return pl.pallas_call(
        functools.partial(_mla_kernel, n_tiles=s // MLA_TILE),
        grid=(b, n_groups, s // step_rows),
        in_specs=[pl.BlockSpec((1, step_rows, n_heads * LANES), tile),
                  pl.BlockSpec((1, s, n_heads * LANES), whole_seq),
                  pl.BlockSpec((1, s, n_heads * MLA_V), whole_seq)],
        out_specs=pl.BlockSpec((1, step_rows, n_heads * MLA_V), tile),
        out_shape=jax.ShapeDtypeStruct((b, s, MLA_HEADS * MLA_V), BF16),
        scratch_shapes=[pltpu.VMEM((n_heads, MLA_V + ONES_ROWS, s), BF16),
                        pltpu.VMEM((n_chains, LANES, MLA_TILE), BF16),
                        score_buf, score_buf, score_buf,
                        *_SoftmaxState.scratch(n_chains, MLA_V, MLA_TILE)],
        compiler_params=_params(("parallel", "parallel", "arbitrary")),
        name="mla_attention",
    )(q, k, v)


def _swa_kernel(slopes_ref, sinks_ref, q_ref, kp_ref, kc_ref, vp_ref, vc_ref, o_ref, s_ref):
    n = pl.program_id(1)
    group = SWA_Q_HEADS // SWA_KV_HEADS
    diff, in_sequence = _band_geometry(SWA_BLOCK, n)
    window = (diff >= 0) & (diff < SWA_WINDOW)
    masks = [window & in_sequence] + [window] * (BANDS_PER_STEP - 1)
    dist = diff.astype(F32) * LOG2E

    def both_halves(prev_ref, cur_ref, band, kv_head):
        x = _band_keys(prev_ref, cur_ref, band, SWA_BLOCK, slice(None))
        part = x[:, kv_head * HEAD_DIM:(kv_head + 1) * HEAD_DIM]
        return jnp.concatenate([part, part], axis=1)

    bands_kv = [(band, kv) for band in range(BANDS_PER_STEP) for kv in range(SWA_KV_HEADS)]
    k_tiles = {bk: both_halves(kp_ref, kc_ref, *bk) for bk in bands_kv}
    v_tiles = {bk: both_halves(vp_ref, vc_ref, *bk) for bk in bands_kv}

    def logits(h, band, s):
        return jnp.where(masks[band], s - slopes_ref[h] * dist, NEG_INF)

    def sink(h):
        return sinks_ref[h] * LOG2E

    def store_pair(pair, band, o, lse):
        o_ref[0, _band_rows(band, SWA_BLOCK), pair * LANES:(pair + 1) * LANES] = o.astype(o_ref.dtype)

    _band_attention(
        SWA_Q_HEADS,
        lambda h, band: q_ref[0, _band_rows(band, SWA_BLOCK), (h // 2) * LANES:(h // 2 + 1) * LANES],
        lambda h, band: k_tiles[band, h // group],
        lambda h, band: v_tiles[band, h // group],
        logits, sink, store_pair, s_ref, with_lse=False)


def _swa_attention(qkv, sinks, b, s):
    q_width = SWA_Q_HEADS * HEAD_DIM
    kv_width = SWA_KV_HEADS * HEAD_DIM
    k_tile = q_width // kv_width
    step_rows = BANDS_PER_STEP * SWA_BLOCK
    assert s % step_rows == 0
    cur = lambda t: (lambda bi, n: (bi, n, t))
    prev = lambda t: (lambda bi, n: (bi, jnp.maximum(n * BANDS_PER_STEP - 1, 0), t))
    kv_blk, kv_prev_blk = (1, step_rows, kv_width), (1, SWA_BLOCK, kv_width)
    return pl.pallas_call(
        _swa_kernel,
        grid=(b, s // step_rows),
        in_specs=[pl.BlockSpec(memory_space=pltpu.SMEM), pl.BlockSpec(memory_space=pltpu.SMEM),
                  pl.BlockSpec((1, step_rows, q_width), cur(0)),
                  pl.BlockSpec(kv_prev_blk, prev(k_tile)), pl.BlockSpec(kv_blk, cur(k_tile)),
                  pl.BlockSpec(kv_prev_blk, prev(k_tile + 1)), pl.BlockSpec(kv_blk, cur(k_tile + 1))],
        out_specs=pl.BlockSpec((1, step_rows, q_width), cur(0)),
        out_shape=jax.ShapeDtypeStruct((b, s, q_width), BF16),
        scratch_shapes=[pltpu.VMEM((BANDS_PER_STEP * SWA_Q_HEADS, 2 * SWA_BLOCK, SWA_BLOCK), F32)],
        compiler_params=_params(("parallel", "arbitrary")),
        name="swa_attention",
    )(_alibi_slopes(SWA_Q_HEADS), sinks.astype(F32), qkv, qkv, qkv, qkv, qkv)


def kernel(x, l0_attn_norm, l0_w_qkv, l0_w_o, l0_mlp_norm, l0_w_up, l0_w_down, l1_attn_norm, l1_w_qkv, l1_w_o, l1_mlp_norm, l1_w_up, l1_w_down, l2_attn_norm, l2_w_dkv, l2_q_norm, l2_w_uq, l2_kv_norm, l2_w_ukv, l2_w_o, l2_mlp_norm, l2_w_up, l2_w_down, l3_attn_norm, l3_w_qkv, l3_sinks, l3_w_o, l3_mlp_norm, l3_w_up, l3_w_down, final_norm):
    b, s, d = x.shape
    bf = lambda w: w.astype(BF16)
    h = x.reshape(b * s, d)

    qkv = _norm_proj(h, l0_attn_norm, bf(l0_w_qkv), scaled_cols=MOBA_HEADS * HEAD_DIM, scale=HEAD_Q_SCALE)
    a = _moba_attention(qkv.reshape(b, s, -1), b, s).reshape(b * s, -1)
    h = _post(h, (a,), bf(l0_w_o), l0_mlp_norm, bf(l0_w_up), bf(l0_w_down))

    qkvs = _dil_proj(h, l1_attn_norm, bf(l1_w_qkv))
    groups = [_dilated_group(qkvs[g], b, s, g, window, dil) for g, (window, dil) in enumerate(DIL_PAIRS)]
    h = _post(h, tuple(o for o, _ in groups) + tuple(l for _, l in groups),
              bf(l1_w_o), l1_mlp_norm, bf(l1_w_up), bf(l1_w_down))

    q, k, v = _mla_proj(h, l2_attn_norm, l2_w_dkv, l2_q_norm, l2_w_uq, l2_kv_norm, l2_w_ukv, s)
    a = _mla_attention(q.reshape(b, s, -1), k.reshape(b, s, -1), v.reshape(b, s, -1), b, s).reshape(b * s, -1)
    h = _post(h, (a,), bf(l2_w_o), l2_mlp_norm, bf(l2_w_up), bf(l2_w_down))

    qkv = _norm_proj(h, l3_attn_norm, bf(l3_w_qkv), col_tile=256, scaled_cols=SWA_Q_HEADS * HEAD_DIM,
                     scale=HEAD_Q_SCALE)
    a = _swa_attention(qkv.reshape(b, s, -1), l3_sinks, b, s).reshape(b * s, -1)
    h = _post(h, (a,), bf(l3_w_o), l3_mlp_norm, bf(l3_w_up), bf(l3_w_down), g_final=final_norm)
    return h.reshape(b, s, d)
```

```python
import functools

import jax
import jax.numpy as jnp
from jax import lax
from jax.experimental import pallas as pl
from jax.experimental.pallas import tpu as pltpu

F32 = jnp.float32
BF16 = jnp.bfloat16

D_MODEL = 1024
HEAD_DIM = 64
RMS_EPS = 1e-6
D_FF = 4 * D_MODEL
NEG_INF = -1e30

MOBA_HEADS = 16
MOBA_BLOCK = 256
MOBA_TOPK = 3

DIL_PAIRS = ((128, 1), (512, 4), (2048, 16))
DIL_HEADS_PER_GROUP = 8
DIL_BAND = 128

MLA_HEADS = 16
MLA_Q_RANK = 768
MLA_KV_RANK = 256
MLA_NOPE = 64
MLA_ROPE = 32
MLA_V = 64
ROPE_THETA = 10000.0
MLA_TILE = 256
MLA_Q_TILES = 2
MLA_HEADS_PER_STEP = 8
MOBA_HEADS_PER_STEP = 8
MOBA_Q_TILES = 2

SWA_Q_HEADS = 16
SWA_KV_HEADS = 2
SWA_WINDOW = 128
SWA_BLOCK = 128

LANES = 128
ROW_TILE = 512
FF_TILE = 512
VMEM_LIMIT = 56 * 1024 * 1024
LOG2E = 1.4426950408889634
HEAD_Q_SCALE = HEAD_DIM ** -0.5 * LOG2E
MLA_Q_SCALE = (MLA_NOPE + MLA_ROPE) ** -0.5 * LOG2E


def _params(semantics):
    return pltpu.CompilerParams(dimension_semantics=semantics, vmem_limit_bytes=VMEM_LIMIT)


def _resident(shape):
    return pl.BlockSpec(shape, lambda *_: (0,) * len(shape), pipeline_mode=pl.Buffered(1))


def _alibi_slopes(n_heads):
    return 2.0 ** (-8.0 * jnp.arange(1, n_heads + 1, dtype=F32) / n_heads)


def _rms(x, g):
    return x * lax.rsqrt(jnp.mean(x * x, axis=-1, keepdims=True) + RMS_EPS) * g


def _dot(a, b):
    return jnp.dot(a, b, preferred_element_type=F32)


def _dot_nt(a, b):
    return lax.dot_general(a, b, (((1,), (1,)), ((), ())), preferred_element_type=F32)


def _norm_proj_kernel(h_ref, g_ref, w_ref, o_ref, *, col_tile, scaled_cols, scale):
    n = _rms(h_ref[...], g_ref[...]).astype(BF16)
    for c in range(o_ref.shape[1] // col_tile):
        cols = slice(c * col_tile, (c + 1) * col_tile)
        y = _dot(n, w_ref[:, cols])
        if (c + 1) * col_tile <= scaled_cols:
            y = y * scale
        o_ref[:, cols] = y.astype(o_ref.dtype)


def _norm_proj(h, g, w, col_tile=512, scaled_cols=0, scale=1.0):
    m, d = h.shape
    n_out = w.shape[1]
    assert scaled_cols % col_tile == 0
    return pl.pallas_call(
        functools.partial(_norm_proj_kernel, col_tile=col_tile, scaled_cols=scaled_cols, scale=scale),
        grid=(m // ROW_TILE,),
        in_specs=[pl.BlockSpec((ROW_TILE, d), lambda i: (i, 0)),
                  _resident((1, d)),
                  _resident((d, n_out))],
        out_specs=pl.BlockSpec((ROW_TILE, n_out), lambda i: (i, 0)),
        out_shape=jax.ShapeDtypeStruct((m, n_out), BF16),
        compiler_params=_params(("parallel",)),
        name="norm_proj",
    )(h, g.reshape(1, d), w)


def _dil_proj_kernel(h_ref, g_ref, w_ref, o0_ref, o1_ref, o2_ref, stage_ref, sorted_ref):
    n = _rms(h_ref[...], g_ref[...])
    rows, d = n.shape
    width = DIL_HEADS_PER_GROUP * HEAD_DIM
    for t in range(d // LANES):
        stage_ref[t] = n[:, t * LANES:(t + 1) * LANES]
    n = n.astype(BF16)
    for g, (o_ref, (_, dil)) in enumerate(zip((o0_ref, o1_ref, o2_ref), DIL_PAIRS)):
        per = rows // dil
        for r in range(dil if dil > 1 else 0):
            for t in range(d // LANES):
                sorted_ref[g - 1, r * per:(r + 1) * per, t * LANES:(t + 1) * LANES] = (
                    stage_ref[t, pl.ds(r, per, stride=dil), :].astype(BF16))
        lhs = n if dil == 1 else sorted_ref[g - 1]
        for part in range(3):
            src = (3 * g + part) * width
            y = _dot(lhs, w_ref[:, src:src + width])
            if part == 0:
                y = y * HEAD_Q_SCALE
            for r in range(dil):
                dst = (3 * r + part) * width
                o_ref[:, dst:dst + width] = y[r * per:(r + 1) * per, :].astype(BF16)


def _dil_proj(h, g, w):
    m, d = h.shape
    feat = 3 * DIL_HEADS_PER_GROUP * HEAD_DIM
    assert DIL_PAIRS[0][1] == 1
    view = lambda dil: (m // dil, dil * feat)
    view_block = lambda dil: pl.BlockSpec((ROW_TILE // dil, dil * feat), lambda i: (i, 0))
    return pl.pallas_call(
        _dil_proj_kernel,
        grid=(m // ROW_TILE,),
        in_specs=[pl.BlockSpec((ROW_TILE, d), lambda i: (i, 0)), _resident((1, d)), _resident(w.shape)],
        out_specs=[view_block(dil) for _, dil in DIL_PAIRS],
        out_shape=[jax.ShapeDtypeStruct(view(dil), BF16) for _, dil in DIL_PAIRS],
        scratch_shapes=[pltpu.VMEM((d // LANES, ROW_TILE, LANES), F32),
                        pltpu.VMEM((len(DIL_PAIRS) - 1, ROW_TILE, d), BF16)],
        compiler_params=_params(("parallel",)),
        name="dil_proj",
    )(h, g.reshape(1, d), w)


def _mlp_tail(h1, g_ref, wup_ref, wdn_ref, gf_ref, out_ref):
    n = _rms(h1, g_ref[...]).astype(BF16)
    acc = jnp.zeros_like(h1)
    for c in range(wup_ref.shape[1] // FF_TILE):
        cols = slice(c * FF_TILE, (c + 1) * FF_TILE)
        u = jnp.square(jnp.maximum(_dot(n, wup_ref[:, cols]), 0.0)).astype(BF16)
        acc = acc + _dot(u, wdn_ref[cols, :])
    out = h1 + acc
    if gf_ref is not None:
        out = _rms(out, gf_ref[...])
    out_ref[...] = out


def _post_kernel(h_ref, a_ref, wo_ref, g_ref, wup_ref, wdn_ref, *rest):
    gf_ref, out_ref = (rest[0], rest[1]) if len(rest) == 2 else (None, rest[0])
    h1 = h_ref[...] + _dot(a_ref[...], wo_ref[...])
    _mlp_tail(h1, g_ref, wup_ref, wdn_ref, gf_ref, out_ref)


def _post_merge_kernel(h_ref, o0_ref, o1_ref, o2_ref, l0_ref, l1_ref, l2_ref,
                       wo_ref, g_ref, wup_ref, wdn_ref, out_ref, stage_ref):
    rows, width = o0_ref.shape

    def token_rows(x_ref, slot, dil):
        if dil == 1:
            return x_ref[...]
        for r in range(dil):
            for t in range(width // LANES):
                src = r * width + t * LANES
                stage_ref[slot, t, pl.ds(r, rows // dil, stride=dil), :] = x_ref[:, src:src + LANES]
        return jnp.concatenate([stage_ref[slot, t] for t in range(width // LANES)], axis=1)

    dils = [dil for _, dil in DIL_PAIRS]
    l0, l1, l2 = (token_rows(ref, slot, dil) for slot, (ref, dil) in enumerate(zip((l0_ref, l1_ref, l2_ref), dils)))
    o0, o1, o2 = (token_rows(ref, 3 + slot, dil) for slot, (ref, dil) in enumerate(zip((o0_ref, o1_ref, o2_ref), dils)))
    mx = jnp.maximum(jnp.maximum(l0, l1), l2)
    e0, e1, e2 = jnp.exp(l0 - mx), jnp.exp(l1 - mx), jnp.exp(l2 - mx)
    merged = (e0 * o0 + e1 * o1 + e2 * o2) / (e0 + e1 + e2)
    h1 = h_ref[...] + _dot(merged.astype(BF16), wo_ref[...])
    _mlp_tail(h1, g_ref, wup_ref, wdn_ref, None, out_ref)


def _post(h, attn_ins, w_o, g, w_up, w_down, g_final=None):
    m, d = h.shape
    row = lambda width: pl.BlockSpec((ROW_TILE, width), lambda i: (i, 0))
    merge = len(attn_ins) > 1
    in_specs = [row(d)] + [pl.BlockSpec((ROW_TILE * a.shape[0] // m, a.shape[1]), lambda i: (i, 0)) for a in attn_ins]
    in_specs += [_resident(w_o.shape), _resident((1, d)), _resident(w_up.shape), _resident(w_down.shape)]
    args = [h, *attn_ins, w_o, g.reshape(1, d), w_up, w_down]
    if g_final is not None:
        in_specs.append(_resident((1, d)))
        args.append(g_final.reshape(1, d))
    scratch = [pltpu.VMEM((len(attn_ins), w_o.shape[0] // LANES, ROW_TILE, LANES), F32)] if merge else []
    return pl.pallas_call(
        _post_merge_kernel if merge else _post_kernel,
        grid=(m // ROW_TILE,),
        in_specs=in_specs,
        out_specs=row(d),
        out_shape=jax.ShapeDtypeStruct((m, d), F32),
        scratch_shapes=scratch,
        compiler_params=_params(("parallel",)),
        name="post_mlp",
    )(*args)


ONES_ROWS = 16
SKIP_SHIFT = -2.0 * NEG_INF


class _SoftmaxState:
    def __init__(self, m_ref, acc_ref, dv):
        self.m_ref, self.acc_ref, self.dv = m_ref, acc_ref, dv

    @staticmethod
    def scratch(n_heads, dv, tq):
        return [pltpu.VMEM((n_heads, 1, tq), F32), pltpu.VMEM((n_heads, dv + ONES_ROWS, tq), F32)]

    @staticmethod
    def store_values_t(vt_ref, hh, cols, v_t):
        dv = v_t.shape[0]
        vt_ref[hh, :dv, cols] = v_t.astype(BF16)
        vt_ref[hh, dv:, cols] = jnp.ones((ONES_ROWS, v_t.shape[1]), BF16)

    def init(self):
        self.m_ref[...] = jnp.full(self.m_ref.shape, NEG_INF, F32)
        self.acc_ref[...] = jnp.zeros(self.acc_ref.shape, F32)

    def step(self, hh, x, v_t, shift=None):
        m = self.m_ref[hh]
        m_cur = jnp.max(x, axis=0, keepdims=True)
        if shift is not None:
            m_cur = m_cur - shift
        m_new = jnp.maximum(m, m_cur)
        alpha = jnp.exp2(m - m_new)
        p = jnp.exp2(x - (m_new if shift is None else m_new + shift))
        self.m_ref[hh] = m_new
        self.acc_ref[hh] = alpha * self.acc_ref[hh] + _dot(v_t, p.astype(BF16))

    def normalized(self, hh):
        return self.acc_ref[hh, :self.dv, :] / self.acc_ref[hh, self.dv:self.dv + 1, :]


def _sweep_blocks(n_past, n_chains, state, score_fn, values_fn, past_logits, own_steps, s_a, s_b):
    def produce(buf, block, chains=range(n_chains)):
        for c in chains:
            buf[c] = score_fn(c, block)

    def consume(buf, block):
        for c in range(n_chains):
            x, shift = past_logits(c, buf[c], block)
            state.step(c, x, values_fn(c, block), shift)

    produced = set()
    for buf, block, chains, _ in own_steps:
        produce(buf, block, [c for c in chains if (id(buf), c) not in produced])
        produced.update((id(buf), c) for c in chains)
    produce(s_a, 0)
    for buf, block, chains, logits_fn in own_steps:
        for c in chains:
            x, shift = logits_fn(c, buf[c])
            state.step(c, x, values_fn(c, block), shift)

    def two_blocks(first):
        produce(s_b, first + 1)
        consume(s_a, first)
        produce(s_a, jnp.minimum(first + 2, n_past - 1))
        consume(s_b, first + 1)

    def quad(t, carry):
        two_blocks(4 * t)
        two_blocks(4 * t + 2)
        return carry

    def pair(t, carry):
        two_blocks(4 * (n_past // 4) + 2 * t)
        return carry

    lax.fori_loop(0, n_past // 4, quad, 0)
    lax.fori_loop(0, (n_past % 4) // 2, pair, 0)

    @pl.when(n_past % 2 == 1)
    def _():
        consume(s_a, n_past - 1)


def _transpose_bf16(x):
    return x.astype(F32).T.astype(BF16)


def _keep_head_rows(x_t, half):
    r = lax.broadcasted_iota(jnp.int32, x_t.shape, 0)
    keep = (r < HEAD_DIM) if half == 0 else (r >= HEAD_DIM)
    return jnp.where(keep, x_t, jnp.zeros_like(x_t))


def _moba_kernel(slopes_ref, q_ref, k_ref, v_ref, o_ref, kmean_ref, vt_ref, term_ref, alibi_ref, qh_ref,
                 s_a, s_b, s_c, m_ref, acc_ref, *, n_blocks):
    blk = MOBA_BLOCK
    n_heads = MOBA_HEADS_PER_STEP
    group = pl.program_id(1)
    i = pl.program_id(2)
    state = _SoftmaxState(m_ref, acc_ref, HEAD_DIM)
    chain = lambda tile, hh: tile * n_heads + hh
    tile_chains = [[chain(tile, hh) for hh in range(n_heads)] for tile in range(MOBA_Q_TILES)]

    @pl.when(i == 0)
    def _():
        for j in range(n_blocks):
            rows = slice(j * blk, (j + 1) * blk)
            kmean_ref[j:j + 1, :] = jnp.mean(k_ref[0, rows, :].astype(F32), axis=0, keepdims=True)
            v_t = v_ref[0, rows, :].astype(F32).T
            for hh in range(n_heads):
                state.store_values_t(vt_ref, hh, rows, v_t[hh * HEAD_DIM:(hh + 1) * HEAD_DIM, :])

    state.init()
    key = lax.broadcasted_iota(jnp.int32, (blk, blk), 0)
    qry = lax.broadcasted_iota(jnp.int32, (blk, blk), 1)
    rel = (qry - key).astype(F32)
    blk_idx = lax.broadcasted_iota(jnp.int32, (n_blocks, blk), 0)
    lane_tile = lambda hh: slice((hh // 2) * LANES, (hh // 2 + 1) * LANES)
    first_block = MOBA_Q_TILES * i

    for hh in range(n_heads):
        alibi_ref[hh] = (slopes_ref[n_heads * group + hh] * LOG2E) * rel
    for tile in range(MOBA_Q_TILES):
        own = first_block + tile
        q_t = q_ref[0, tile * blk:(tile + 1) * blk, :].astype(F32).T
        fully_past = blk_idx < own
        block_dist = ((own - blk_idx) * blk).astype(F32)
        for hh in range(n_heads):
            slope_s = slopes_ref[n_heads * group + hh] * LOG2E
            q_h = _keep_head_rows(q_t[lane_tile(hh), :], hh % 2)
            gate = lax.dot_general(kmean_ref[:, lane_tile(hh)], q_h, (((1,), (0,)), ((), ())),
                                   precision=lax.Precision.HIGHEST, preferred_element_type=F32)
            work = jnp.where(fully_past, gate, NEG_INF * HEAD_Q_SCALE)
            chosen = jnp.zeros((n_blocks, blk), jnp.bool_)
            for _ in range(min(MOBA_TOPK, n_blocks)):
                best = jnp.max(work, axis=0, keepdims=True)
                first = jnp.min(jnp.where(work == best, blk_idx, n_blocks), axis=0, keepdims=True)
                pick = blk_idx == first
                chosen = jnp.logical_or(chosen, pick)
                work = jnp.where(pick, -jnp.inf, work)
            term_ref[chain(tile, hh)] = jnp.where(jnp.logical_and(chosen, fully_past), slope_s * block_dist,
                                                  SKIP_SHIFT)
            qh_ref[chain(tile, hh)] = q_h.astype(BF16)

    def rows_of(block):
        return pl.ds(pl.multiple_of(block * blk, blk), blk)

    def scores(c, block):
        return _dot(k_ref[0, rows_of(block), lane_tile(c % n_heads)], qh_ref[c])

    def v_t(c, block):
        return vt_ref[c % n_heads, :, rows_of(block)]

    def past_logits(c, s, block):
        return s - alibi_ref[c % n_heads], term_ref[c, pl.ds(block, 1), :]

    def own_logits(c, s):
        return jnp.where(rel >= 0, s - alibi_ref[c % n_heads], NEG_INF), None

    assert MOBA_Q_TILES == 2
    own_steps = [(s_b, first_block, tile_chains[0], own_logits),
                 (s_b, first_block, tile_chains[1], functools.partial(past_logits, block=first_block)),
                 (s_c, first_block + 1, tile_chains[1], own_logits)]
    _sweep_blocks(first_block, MOBA_Q_TILES * n_heads, state, scores, v_t, past_logits, own_steps, s_a, s_b)
    for tile in range(MOBA_Q_TILES):
        o_t = jnp.concatenate([state.normalized(c) for c in tile_chains[tile]], axis=0)
        o_ref[0, tile * blk:(tile + 1) * blk, :] = o_t.T.astype(o_ref.dtype)


def _moba_attention(qkv, b, s):
    n_blocks = s // MOBA_BLOCK
    n_heads = MOBA_HEADS_PER_STEP
    width = n_heads * HEAD_DIM
    n_groups = MOBA_HEADS // n_heads
    n_chains = MOBA_Q_TILES * n_heads
    step_rows = MOBA_Q_TILES * MOBA_BLOCK
    assert s % step_rows == 0
    score_buf = pltpu.VMEM((n_chains, MOBA_BLOCK, MOBA_BLOCK), F32)
    return pl.pallas_call(
        functools.partial(_moba_kernel, n_blocks=n_blocks),
        grid=(b, n_groups, s // step_rows),
        in_specs=[pl.BlockSpec(memory_space=pltpu.SMEM),
                  pl.BlockSpec((1, step_rows, width), lambda bi, g, i: (bi, i, g)),
                  pl.BlockSpec((1, s, width), lambda bi, g, i: (bi, 0, n_groups + g)),
                  pl.BlockSpec((1, s, width), lambda bi, g, i: (bi, 0, 2 * n_groups + g))],
        out_specs=pl.BlockSpec((1, step_rows, width), lambda bi, g, i: (bi, i, g)),
        out_shape=jax.ShapeDtypeStruct((b, s, MOBA_HEADS * HEAD_DIM), BF16),
        scratch_shapes=[pltpu.VMEM((n_blocks, width), F32),
                        pltpu.VMEM((n_heads, HEAD_DIM + ONES_ROWS, s), BF16),
                        pltpu.VMEM((n_chains, n_blocks, MOBA_BLOCK), F32),
                        pltpu.VMEM((n_heads, MOBA_BLOCK, MOBA_BLOCK), F32),
                        pltpu.VMEM((n_chains, LANES, MOBA_BLOCK), BF16),
                        score_buf, score_buf, score_buf,
                        *_SoftmaxState.scratch(n_chains, HEAD_DIM, MOBA_BLOCK)],
        compiler_params=_params(("parallel", "parallel", "arbitrary")),
        name="moba_attention",
    )(_alibi_slopes(MOBA_HEADS), qkv, qkv, qkv)


def _lane_half(x, half):
    lane = lax.broadcasted_iota(jnp.int32, x.shape, x.ndim - 1)
    keep = (lane < HEAD_DIM) if half == 0 else (lane >= HEAD_DIM)
    return jnp.where(keep, x, jnp.zeros_like(x))


BANDS_PER_STEP = 2


def _band_attention(n_heads, q_tile, k_tile, v_tile, logits_fn, floor_fn, store_pair, s_ref, with_lse):
    chains = [(band, h) for band in range(BANDS_PER_STEP) for h in range(n_heads)]
    for c, (band, h) in enumerate(chains):
        s_ref[c] = _dot_nt(k_tile(h, band), _lane_half(q_tile(h, band), h % 2))
    even = None
    for c, (band, h) in enumerate(chains):
        x = logits_fn(h, band, s_ref[c])
        m = jnp.max(x, axis=0, keepdims=True)
        floor = floor_fn(h)
        if floor is not None:
            m = jnp.maximum(m, floor)
        p = jnp.exp2(x - m)
        den = jnp.sum(p, axis=0, keepdims=True)
        if floor is not None:
            den = den + jnp.exp2(floor - m)
        o_t = lax.dot_general(v_tile(h, band), p.astype(BF16), (((0,), (0,)), ((), ())),
                              preferred_element_type=F32)
        half = slice((h % 2) * HEAD_DIM, (h % 2 + 1) * HEAD_DIM)
        o_t = o_t[half, :] / den
        lse_t = jnp.broadcast_to((m + jnp.log2(den)) * (1.0 / LOG2E), o_t.shape) if with_lse else None
        if h % 2 == 0:
            even = (o_t, lse_t)
            continue
        store_pair(h // 2, band, jnp.concatenate([even[0], o_t], axis=0).T,
                   jnp.concatenate([even[1], lse_t], axis=0).T if with_lse else None)


def _band_rows(band, size):
    return slice(band * size, (band + 1) * size)


def _band_keys(prev_ref, cur_ref, band, size, cols):
    if band == 0:
        return jnp.concatenate([prev_ref[0, :, cols], cur_ref[0, :size, cols]], axis=0)
    return cur_ref[0, (band - 1) * size:(band + 1) * size, cols]


def _band_geometry(size, n):
    key = lax.broadcasted_iota(jnp.int32, (2 * size, size), 0)
    qry = lax.broadcasted_iota(jnp.int32, (2 * size, size), 1)
    diff = qry + size - key
    in_sequence = (n > 0) | (key >= size)
    return diff, in_sequence


def _dil_kernel(slopes_ref, q_ref, kp_ref, kc_ref, vp_ref, vc_ref, o_ref, lse_ref, s_ref, *, n_pts, dil, group):
    n = pl.program_id(2)
    diff, in_sequence = _band_geometry(DIL_BAND, n)
    window = (diff >= 0) & (diff <= n_pts)
    masks = [window & in_sequence] + [window] * (BANDS_PER_STEP - 1)
    dist = (dil * diff).astype(F32) * LOG2E
    tile = lambda h: slice((h // 2) * LANES, (h // 2 + 1) * LANES)

    def logits(h, band, s):
        slope = slopes_ref[group * DIL_HEADS_PER_GROUP + h]
        return jnp.where(masks[band], s - slope * dist, NEG_INF)

    def store_pair(pair, band, o, lse):
        rows, cols = _band_rows(band, DIL_BAND), slice(pair * LANES, (pair + 1) * LANES)
        o_ref[0, rows, cols] = o
        lse_ref[0, rows, cols] = lse

    _band_attention(
        DIL_HEADS_PER_GROUP,
        lambda h, band: q_ref[0, _band_rows(band, DIL_BAND), tile(h)],
        lambda h, band: _band_keys(kp_ref, kc_ref, band, DIL_BAND, tile(h)),
        lambda h, band: _band_keys(vp_ref, vc_ref, band, DIL_BAND, tile(h)),
        logits, lambda h: None, store_pair, s_ref, with_lse=True)


def _dilated_group(qkv, b, s, group, window, dil):
    width = DIL_HEADS_PER_GROUP * HEAD_DIM
    n_sub = s // dil
    step_rows = BANDS_PER_STEP * DIL_BAND
    assert n_sub % step_rows == 0
    view = qkv.reshape(b, n_sub, dil * 3 * width)

    def col(part):
        return lambda bi, r, n: (bi, n, r * 3 + part)

    def col_prev(part):
        return lambda bi, r, n: (bi, jnp.maximum(n * BANDS_PER_STEP - 1, 0), r * 3 + part)

    blk, prev_blk = (1, step_rows, width), (1, DIL_BAND, width)
    out_spec = pl.BlockSpec(blk, lambda bi, r, n: (bi, n, r))
    out_sds = jax.ShapeDtypeStruct((b, n_sub, dil * width), F32)
    o, lse = pl.pallas_call(
        functools.partial(_dil_kernel, n_pts=window // dil, dil=dil, group=group),
        grid=(b, dil, n_sub // step_rows),
        in_specs=[pl.BlockSpec(memory_space=pltpu.SMEM),
                  pl.BlockSpec(blk, col(0)),
                  pl.BlockSpec(prev_blk, col_prev(1)), pl.BlockSpec(blk, col(1)),
                  pl.BlockSpec(prev_blk, col_prev(2)), pl.BlockSpec(blk, col(2))],
        out_specs=[out_spec, out_spec],
        out_shape=[out_sds, out_sds],
        scratch_shapes=[pltpu.VMEM((BANDS_PER_STEP * DIL_HEADS_PER_GROUP, 2 * DIL_BAND, DIL_BAND), F32)],
        compiler_params=_params(("parallel", "parallel", "arbitrary")),
        name=f"dilated_attention_g{group}",
    )(_alibi_slopes(len(DIL_PAIRS) * DIL_HEADS_PER_GROUP), view, view, view, view, view)
    return o.reshape(b * n_sub, dil * width), lse.reshape(b * n_sub, dil * width)


def _mla_proj_kernel(h_ref, g_ref, wd_ref, gq_ref, gkv_ref, wuq_ref, wuk_ref, wuv_ref,
                     cos_ref, sin_up_ref, sin_dn_ref, q_ref, k_ref, v_ref):
    n = _rms(h_ref[...], g_ref[...]).astype(BF16)
    c = _dot(n, wd_ref[...])
    nq = _rms(c[:, :MLA_Q_RANK], gq_ref[...]).astype(BF16)
    nkv = _rms(c[:, MLA_Q_RANK:MLA_Q_RANK + MLA_KV_RANK], gkv_ref[...]).astype(BF16)
    k_rope = c[:, MLA_Q_RANK + MLA_KV_RANK:]
    half = MLA_ROPE // 2
    pair = 2

    def rope(x, width):
        cos, sin_up, sin_dn = (jnp.tile(t[...], (1, width)) for t in (cos_ref, sin_up_ref, sin_dn_ref))
        return x * cos + pltpu.roll(x, half, 1) * sin_up + pltpu.roll(x, x.shape[1] - half, 1) * sin_dn

    kr = jnp.tile(rope(k_rope, 1), (1, pair))
    for h in range(0, MLA_HEADS, pair):
        cols = slice(h * LANES, (h + pair) * LANES)
        q_ref[:, cols] = (rope(_dot(nq, wuq_ref[:, cols]), pair) * MLA_Q_SCALE).astype(BF16)
        k_ref[:, cols] = (_dot(nkv, wuk_ref[:, cols]) + kr).astype(BF16)
    v_ref[...] = _dot(nkv, wuv_ref[...]).astype(BF16)


def _mla_proj(h, g, w_dkv, q_norm, w_uq, kv_norm, w_ukv, s):
    m, d = h.shape
    qk = MLA_NOPE + MLA_ROPE
    wd = jnp.concatenate([w_dkv[:, :MLA_Q_RANK + MLA_KV_RANK],
                          jnp.zeros((d, MLA_NOPE), F32), w_dkv[:, MLA_Q_RANK + MLA_KV_RANK:],
                          jnp.zeros((d, LANES - qk), F32)], axis=1).astype(BF16)
    wuq = jnp.pad(w_uq.reshape(MLA_Q_RANK, MLA_HEADS, qk), ((0, 0), (0, 0), (0, LANES - qk)))
    wuq = wuq.reshape(MLA_Q_RANK, MLA_HEADS * LANES).astype(BF16)
    w_ukv = w_ukv.reshape(MLA_KV_RANK, MLA_HEADS, MLA_NOPE + MLA_V)
    wuk = jnp.pad(w_ukv[:, :, :MLA_NOPE], ((0, 0), (0, 0), (0, LANES - MLA_NOPE)))
    wuk = wuk.reshape(MLA_KV_RANK, MLA_HEADS * LANES).astype(BF16)
    wuv = w_ukv[:, :, MLA_NOPE:].reshape(MLA_KV_RANK, MLA_HEADS * MLA_V).astype(BF16)
    half = MLA_ROPE // 2
    inv = ROPE_THETA ** (-jnp.arange(0, MLA_ROPE, 2, dtype=F32) / MLA_ROPE)
    ang = jnp.arange(s).astype(F32)[:, None] * inv[None, :]
    cos, sin = jnp.cos(ang), jnp.sin(ang)
    zeros = lambda w: jnp.zeros((s, w), F32)
    cos_t = jnp.concatenate([jnp.ones((s, MLA_NOPE), F32), cos, cos, zeros(LANES - qk)], axis=1)
    sin_up = jnp.concatenate([zeros(MLA_NOPE + half), sin, zeros(LANES - qk)], axis=1)
    sin_dn = jnp.concatenate([zeros(MLA_NOPE), -sin, zeros(half + LANES - qk)], axis=1)

    tiles_per_seq = s // ROW_TILE
    row = lambda width: pl.BlockSpec((ROW_TILE, width), lambda i: (i, 0))
    table = pl.BlockSpec((ROW_TILE, LANES), lambda i: (i % tiles_per_seq, 0))
    return pl.pallas_call(
        _mla_proj_kernel,
        grid=(m // ROW_TILE,),
        in_specs=[row(d), _resident((1, d)), _resident(wd.shape),
                  _resident((1, MLA_Q_RANK)), _resident((1, MLA_KV_RANK)),
                  _resident(wuq.shape), _resident(wuk.shape), _resident(wuv.shape),
                  table, table, table],
        out_specs=[row(MLA_HEADS * LANES), row(MLA_HEADS * LANES), row(MLA_HEADS * MLA_V)],
        out_shape=[jax.ShapeDtypeStruct((m, MLA_HEADS * LANES), BF16),
                   jax.ShapeDtypeStruct((m, MLA_HEADS * LANES), BF16),
                   jax.ShapeDtypeStruct((m, MLA_HEADS * MLA_V), BF16)],
        compiler_params=_params(("parallel",)),
        name="mla_proj",
    )(h, g.reshape(1, d), wd, q_norm.reshape(1, -1), kv_norm.reshape(1, -1), wuq, wuk, wuv,
      cos_t, sin_up, sin_dn)


def _mla_kernel(q_ref, k_ref, v_ref, o_ref, vt_ref, qh_ref, s_a, s_b, s_c, m_ref, acc_ref, *, n_tiles):
    t = MLA_TILE
    n_heads = MLA_HEADS_PER_STEP
    i = pl.program_id(2)
    state = _SoftmaxState(m_ref, acc_ref, MLA_V)
    chain = lambda tile, hh: tile * n_heads + hh
    tile_chains = [[chain(tile, hh) for hh in range(n_heads)] for tile in range(MLA_Q_TILES)]

    @pl.when(i == 0)
    def _():
        for j in range(n_tiles):
            rows = slice(j * t, (j + 1) * t)
            v_t = v_ref[0, rows, :].astype(F32).T
            for hh in range(n_heads):
                state.store_values_t(vt_ref, hh, rows, v_t[hh * MLA_V:(hh + 1) * MLA_V, :])

    state.init()
    key = lax.broadcasted_iota(jnp.int32, (t, t), 0)
    qry = lax.broadcasted_iota(jnp.int32, (t, t), 1)
    lane_tile = lambda hh: slice(hh * LANES, (hh + 1) * LANES)
    for tile in range(MLA_Q_TILES):
        for hh in range(n_heads):
            qh_ref[chain(tile, hh)] = _transpose_bf16(q_ref[0, tile * t:(tile + 1) * t, lane_tile(hh)])

    def rows_of(block):
        return pl.ds(pl.multiple_of(block * t, t), t)

    def scores(c, block):
        return _dot(k_ref[0, rows_of(block), lane_tile(c % n_heads)], qh_ref[c])

    def v_t(c, block):
        return vt_ref[c % n_heads, :, rows_of(block)]

    def past_logits(c, s, block):
        return s, None

    def causal(c, s):
        return jnp.where(key <= qry, s, NEG_INF), None

    def full(c, s):
        return s, None

    assert MLA_Q_TILES == 2
    first = MLA_Q_TILES * i
    own_steps = [(s_b, first, tile_chains[0], causal), (s_b, first, tile_chains[1], full),
                 (s_c, first + 1, tile_chains[1], causal)]
    _sweep_blocks(first, MLA_Q_TILES * n_heads, state, scores, v_t, past_logits, own_steps, s_a, s_b)
    for tile in range(MLA_Q_TILES):
        o_t = jnp.concatenate([state.normalized(c) for c in tile_chains[tile]], axis=0)
        o_ref[0, tile * t:(tile + 1) * t, :] = o_t.T.astype(o_ref.dtype)


def _mla_attention(q, k, v, b, s):
    n_heads = MLA_HEADS_PER_STEP
    n_groups = MLA_HEADS // n_heads
    whole_seq = lambda bi, g, i: (bi, 0, g)
    tile = lambda bi, g, i: (bi, i, g)
    n_chains = MLA_Q_TILES * n_heads
    step_rows = MLA_Q_TILES * MLA_TILE
    assert s % step_rows == 0
    score_buf = pltpu.VMEM((n_chains, MLA_TILE, MLA_TILE), F32)
    return pl.pallas_call(
        functools.partial(_mla_kernel, n_tiles=s // MLA_TILE),
        grid=(b, n_groups, s // step_rows),
        in_specs=[pl.BlockSpec((1, step_rows, n_heads * LANES), tile),
                  pl.BlockSpec((1, s, n_heads * LANES), whole_seq),
                  pl.BlockSpec((1, s, n_heads * MLA_V), whole_seq)],
        out_specs=pl.BlockSpec((1, step_rows, n_heads * MLA_V), tile),
        out_shape=jax.ShapeDtypeStruct((b, s, MLA_HEADS * MLA_V), BF16),
        scratch_shapes=[pltpu.VMEM((n_heads, MLA_V + ONES_ROWS, s), BF16),
                        pltpu.VMEM((n_chains, LANES, MLA_TILE), BF16),
                        score_buf, score_buf, score_buf,
                        *_SoftmaxState.scratch(n_chains, MLA_V, MLA_TILE)],
        compiler_params=_params(("parallel", "parallel", "arbitrary")),
        name="mla_attention",
    )(q, k, v)


def _swa_kernel(slopes_ref, sinks_ref, q_ref, kp_ref, kc_ref, vp_ref, vc_ref, o_ref, s_ref):
    n = pl.program_id(1)
    group = SWA_Q_HEADS // SWA_KV_HEADS
    diff, in_sequence = _band_geometry(SWA_BLOCK, n)
    window = (diff >= 0) & (diff < SWA_WINDOW)
    masks = [window & in_sequence] + [window] * (BANDS_PER_STEP - 1)
    dist = diff.astype(F32) * LOG2E

    def both_halves(prev_ref, cur_ref, band, kv_head):
        x = _band_keys(prev_ref, cur_ref, band, SWA_BLOCK, slice(None))
        part = x[:, kv_head * HEAD_DIM:(kv_head + 1) * HEAD_DIM]
        return jnp.concatenate([part, part], axis=1)

    bands_kv = [(band, kv) for band in range(BANDS_PER_STEP) for kv in range(SWA_KV_HEADS)]
    k_tiles = {bk: both_halves(kp_ref, kc_ref, *bk) for bk in bands_kv}
    v_tiles = {bk: both_halves(vp_ref, vc_ref, *bk) for bk in bands_kv}

    def logits(h, band, s):
        return jnp.where(masks[band], s - slopes_ref[h] * dist, NEG_INF)

    def sink(h):
        return sinks_ref[h] * LOG2E

    def store_pair(pair, band, o, lse):
        o_ref[0, _band_rows(band, SWA_BLOCK), pair * LANES:(pair + 1) * LANES] = o.astype(o_ref.dtype)

    _band_attention(
        SWA_Q_HEADS,
        lambda h, band: q_ref[0, _band_rows(band, SWA_BLOCK), (h // 2) * LANES:(h // 2 + 1) * LANES],
        lambda h, band: k_tiles[band, h // group],
        lambda h, band: v_tiles[band, h // group],
        logits, sink, store_pair, s_ref, with_lse=False)


def _swa_attention(qkv, sinks, b, s):
    q_width = SWA_Q_HEADS * HEAD_DIM
    kv_width = SWA_KV_HEADS * HEAD_DIM
    k_tile = q_width // kv_width
    step_rows = BANDS_PER_STEP * SWA_BLOCK
    assert s % step_rows == 0
    cur = lambda t: (lambda bi, n: (bi, n, t))
    prev = lambda t: (lambda bi, n: (bi, jnp.maximum(n * BANDS_PER_STEP - 1, 0), t))
    kv_blk, kv_prev_blk = (1, step_rows, kv_width), (1, SWA_BLOCK, kv_width)
    return pl.pallas_call(
        _swa_kernel,
        grid=(b, s // step_rows),
        in_specs=[pl.BlockSpec(memory_space=pltpu.SMEM), pl.BlockSpec(memory_space=pltpu.SMEM),
                  pl.BlockSpec((1, step_rows, q_width), cur(0)),
                  pl.BlockSpec(kv_prev_blk, prev(k_tile)), pl.BlockSpec(kv_blk, cur(k_tile)),
                  pl.BlockSpec(kv_prev_blk, prev(k_tile + 1)), pl.BlockSpec(kv_blk, cur(k_tile + 1))],
        out_specs=pl.BlockSpec((1, step_rows, q_width), cur(0)),
        out_shape=jax.ShapeDtypeStruct((b, s, q_width), BF16),
        scratch_shapes=[pltpu.VMEM((BANDS_PER_STEP * SWA_Q_HEADS, 2 * SWA_BLOCK, SWA_BLOCK), F32)],
        compiler_params=_params(("parallel", "arbitrary")),
        name="swa_attention",
    )(_alibi_slopes(SWA_Q_HEADS), sinks.astype(F32), qkv, qkv, qkv, qkv, qkv)


def kernel(x, l0_attn_norm, l0_w_qkv, l0_w_o, l0_mlp_norm, l0_w_up, l0_w_down, l1_attn_norm, l1_w_qkv, l1_w_o, l1_mlp_norm, l1_w_up, l1_w_down, l2_attn_norm, l2_w_dkv, l2_q_norm, l2_w_uq, l2_kv_norm, l2_w_ukv, l2_w_o, l2_mlp_norm, l2_w_up, l2_w_down, l3_attn_norm, l3_w_qkv, l3_sinks, l3_w_o, l3_mlp_norm, l3_w_up, l3_w_down, final_norm):
    b, s, d = x.shape
    bf = lambda w: w.astype(BF16)
    h = x.reshape(b * s, d)

    qkv = _norm_proj(h, l0_attn_norm, bf(l0_w_qkv), scaled_cols=MOBA_HEADS * HEAD_DIM, scale=HEAD_Q_SCALE)
    a = _moba_attention(qkv.reshape(b, s, -1), b, s).reshape(b * s, -1)
    h = _post(h, (a,), bf(l0_w_o), l0_mlp_norm, bf(l0_w_up), bf(l0_w_down))

    qkvs = _dil_proj(h, l1_attn_norm, bf(l1_w_qkv))
    groups = [_dilated_group(qkvs[g], b, s, g, window, dil) for g, (window, dil) in enumerate(DIL_PAIRS)]
    h = _post(h, tuple(o for o, _ in groups) + tuple(l for _, l in groups),
              bf(l1_w_o), l1_mlp_norm, bf(l1_w_up), bf(l1_w_down))

    q, k, v = _mla_proj(h, l2_attn_norm, l2_w_dkv, l2_q_norm, l2_w_uq, l2_kv_norm, l2_w_ukv, s)
    a = _mla_attention(q.reshape(b, s, -1), k.reshape(b, s, -1), v.reshape(b, s, -1), b, s).reshape(b * s, -1)
    h = _post(h, (a,), bf(l2_w_o), l2_mlp_norm, bf(l2_w_up), bf(l2_w_down))

    qkv = _norm_proj(h, l3_attn_norm, bf(l3_w_qkv), col_tile=256, scaled_cols=SWA_Q_HEADS * HEAD_DIM,
                     scale=HEAD_Q_SCALE)
    a = _swa_attention(qkv.reshape(b, s, -1), l3_sinks, b, s).reshape(b * s, -1)
    h = _post(h, (a,), bf(l3_w_o), l3_mlp_norm, bf(l3_w_up), bf(l3_w_down), g_final=final_norm)
    return h.reshape(b, s, d)
```

```python
import functools

import jax
import jax.numpy as jnp
from jax import lax
from jax.experimental import pallas as pl
from jax.experimental.pallas import tpu as pltpu

F32 = jnp.float32
BF16 = jnp.bfloat16

D_MODEL = 1024
HEAD_DIM = 64
RMS_EPS = 1e-6
D_FF = 4 * D_MODEL
NEG_INF = -1e30

MOBA_HEADS = 16
MOBA_BLOCK = 256
MOBA_TOPK = 3

DIL_PAIRS = ((128, 1), (512, 4), (2048, 16))
DIL_HEADS_PER_GROUP = 8
DIL_BAND = 128

MLA_HEADS = 16
MLA_Q_RANK = 768
MLA_KV_RANK = 256
MLA_NOPE = 64
MLA_ROPE = 32
MLA_V = 64
ROPE_THETA = 10000.0
MLA_TILE = 256
MLA_Q_TILES = 2
MLA_HEADS_PER_STEP = 8
MOBA_HEADS_PER_STEP = 8
MOBA_Q_TILES = 2

SWA_Q_HEADS = 16
SWA_KV_HEADS = 2
SWA_WINDOW = 128
SWA_BLOCK = 128

LANES = 128
ROW_TILE = 512
FF_TILE = 512
VMEM_LIMIT = 56 * 1024 * 1024
LOG2E = 1.4426950408889634
HEAD_Q_SCALE = HEAD_DIM ** -0.5 * LOG2E
MLA_Q_SCALE = (MLA_NOPE + MLA_ROPE) ** -0.5 * LOG2E


def _params(semantics):
    return pltpu.CompilerParams(dimension_semantics=semantics, vmem_limit_bytes=VMEM_LIMIT)


def _resident(shape):
    return pl.BlockSpec(shape, lambda *_: (0,) * len(shape), pipeline_mode=pl.Buffered(1))


def _alibi_slopes(n_heads):
    return 2.0 ** (-8.0 * jnp.arange(1, n_heads + 1, dtype=F32) / n_heads)


def _rms(x, g):
    return x * lax.rsqrt(jnp.mean(x * x, axis=-1, keepdims=True) + RMS_EPS) * g


def _dot(a, b):
    return jnp.dot(a, b, preferred_element_type=F32)


def _dot_nt(a, b):
    return lax.dot_general(a, b, (((1,), (1,)), ((), ())), preferred_element_type=F32)


def _norm_proj_kernel(h_ref, g_ref, w_ref, o_ref, *, col_tile, scaled_cols, scale):
    n = _rms(h_ref[...], g_ref[...]).astype(BF16)
    for c in range(o_ref.shape[1] // col_tile):
        cols = slice(c * col_tile, (c + 1) * col_tile)
        y = _dot(n, w_ref[:, cols])
        if (c + 1) * col_tile <= scaled_cols:
            y = y * scale
        o_ref[:, cols] = y.astype(o_ref.dtype)


def _norm_proj(h, g, w, col_tile=512, scaled_cols=0, scale=1.0):
    m, d = h.shape
    n_out = w.shape[1]
    assert scaled_cols % col_tile == 0
    return pl.pallas_call(
        functools.partial(_norm_proj_kernel, col_tile=col_tile, scaled_cols=scaled_cols, scale=scale),
        grid=(m // ROW_TILE,),
        in_specs=[pl.BlockSpec((ROW_TILE, d), lambda i: (i, 0)),
                  _resident((1, d)),
                  _resident((d, n_out))],
        out_specs=pl.BlockSpec((ROW_TILE, n_out), lambda i: (i, 0)),
        out_shape=jax.ShapeDtypeStruct((m, n_out), BF16),
        compiler_params=_params(("parallel",)),
        name="norm_proj",
    )(h, g.reshape(1, d), w)


def _dil_proj_kernel(h_ref, g_ref, w_ref, o0_ref, o1_ref, o2_ref, stage_ref, sorted_ref):
    n = _rms(h_ref[...], g_ref[...])
    rows, d = n.shape
    width = DIL_HEADS_PER_GROUP * HEAD_DIM
    for t in range(d // LANES):
        stage_ref[t] = n[:, t * LANES:(t + 1) * LANES]
    n = n.astype(BF16)
    for g, (o_ref, (_, dil)) in enumerate(zip((o0_ref, o1_ref, o2_ref), DIL_PAIRS)):
        per = rows // dil
        for r in range(dil if dil > 1 else 0):
            for t in range(d // LANES):
                sorted_ref[g - 1, r * per:(r + 1) * per, t * LANES:(t + 1) * LANES] = (
                    stage_ref[t, pl.ds(r, per, stride=dil), :].astype(BF16))
        lhs = n if dil == 1 else sorted_ref[g - 1]
        for part in range(3):
            src = (3 * g + part) * width
            y = _dot(lhs, w_ref[:, src:src + width])
            if part == 0:
                y = y * HEAD_Q_SCALE
            for r in range(dil):
                dst = (3 * r + part) * width
                o_ref[:, dst:dst + width] = y[r * per:(r + 1) * per, :].astype(BF16)


def _dil_proj(h, g, w):
    m, d = h.shape
    feat = 3 * DIL_HEADS_PER_GROUP * HEAD_DIM
    assert DIL_PAIRS[0][1] == 1
    view = lambda dil: (m // dil, dil * feat)
    view_block = lambda dil: pl.BlockSpec((ROW_TILE // dil, dil * feat), lambda i: (i, 0))
    return pl.pallas_call(
        _dil_proj_kernel,
        grid=(m // ROW_TILE,),
        in_specs=[pl.BlockSpec((ROW_TILE, d), lambda i: (i, 0)), _resident((1, d)), _resident(w.shape)],
        out_specs=[view_block(dil) for _, dil in DIL_PAIRS],
        out_shape=[jax.ShapeDtypeStruct(view(dil), BF16) for _, dil in DIL_PAIRS],
        scratch_shapes=[pltpu.VMEM((d // LANES, ROW_TILE, LANES), F32),
                        pltpu.VMEM((len(DIL_PAIRS) - 1, ROW_TILE, d), BF16)],
        compiler_params=_params(("parallel",)),
        name="dil_proj",
    )(h, g.reshape(1, d), w)


def _mlp_tail(h1, g_ref, wup_ref, wdn_ref, gf_ref, out_ref):
    n = _rms(h1, g_ref[...]).astype(BF16)
    acc = jnp.zeros_like(h1)
    for c in range(wup_ref.shape[1] // FF_TILE):
        cols = slice(c * FF_TILE, (c + 1) * FF_TILE)
        u = jnp.square(jnp.maximum(_dot(n, wup_ref[:, cols]), 0.0)).astype(BF16)
        acc = acc + _dot(u, wdn_ref[cols, :])
    out = h1 + acc
    if gf_ref is not None:
        out = _rms(out, gf_ref[...])
    out_ref[...] = out


def _post_kernel(h_ref, a_ref, wo_ref, g_ref, wup_ref, wdn_ref, *rest):
    gf_ref, out_ref = (rest[0], rest[1]) if len(rest) == 2 else (None, rest[0])
    h1 = h_ref[...] + _dot(a_ref[...], wo_ref[...])
    _mlp_tail(h1, g_ref, wup_ref, wdn_ref, gf_ref, out_ref)


def _post_merge_kernel(h_ref, o0_ref, o1_ref, o2_ref, l0_ref, l1_ref, l2_ref,
                       wo_ref, g_ref, wup_ref, wdn_ref, out_ref, stage_ref):
    rows, width = o0_ref.shape

    def token_rows(x_ref, slot, dil):
        if dil == 1:
            return x_ref[...]
        for r in range(dil):
            for t in range(width // LANES):
                src = r * width + t * LANES
                stage_ref[slot, t, pl.ds(r, rows // dil, stride=dil), :] = x_ref[:, src:src + LANES]
        return jnp.concatenate([stage_ref[slot, t] for t in range(width // LANES)], axis=1)

    dils = [dil for _, dil in DIL_PAIRS]
    l0, l1, l2 = (token_rows(ref, slot, dil) for slot, (ref, dil) in enumerate(zip((l0_ref, l1_ref, l2_ref), dils)))
    o0, o1, o2 = (token_rows(ref, 3 + slot, dil) for slot, (ref, dil) in enumerate(zip((o0_ref, o1_ref, o2_ref), dils)))
    mx = jnp.maximum(jnp.maximum(l0, l1), l2)
    e0, e1, e2 = jnp.exp(l0 - mx), jnp.exp(l1 - mx), jnp.exp(l2 - mx)
    merged = (e0 * o0 + e1 * o1 + e2 * o2) / (e0 + e1 + e2)
    h1 = h_ref[...] + _dot(merged.astype(BF16), wo_ref[...])
    _mlp_tail(h1, g_ref, wup_ref, wdn_ref, None, out_ref)


def _post(h, attn_ins, w_o, g, w_up, w_down, g_final=None):
    m, d = h.shape
    row = lambda width: pl.BlockSpec((ROW_TILE, width), lambda i: (i, 0))
    merge = len(attn_ins) > 1
    in_specs = [row(d)] + [pl.BlockSpec((ROW_TILE * a.shape[0] // m, a.shape[1]), lambda i: (i, 0)) for a in attn_ins]
    in_specs += [_resident(w_o.shape), _resident((1, d)), _resident(w_up.shape), _resident(w_down.shape)]
    args = [h, *attn_ins, w_o, g.reshape(1, d), w_up, w_down]
    if g_final is not None:
        in_specs.append(_resident((1, d)))
        args.append(g_final.reshape(1, d))
    scratch = [pltpu.VMEM((len(attn_ins), w_o.shape[0] // LANES, ROW_TILE, LANES), F32)] if merge else []
    return pl.pallas_call(
        _post_merge_kernel if merge else _post_kernel,
        grid=(m // ROW_TILE,),
        in_specs=in_specs,
        out_specs=row(d),
        out_shape=jax.ShapeDtypeStruct((m, d), F32),
        scratch_shapes=scratch,
        compiler_params=_params(("parallel",)),
        name="post_mlp",
    )(*args)


ONES_ROWS = 16
SKIP_SHIFT = -2.0 * NEG_INF


class _SoftmaxState:
    def __init__(self, m_ref, acc_ref, dv):
        self.m_ref, self.acc_ref, self.dv = m_ref, acc_ref, dv

    @staticmethod
    def scratch(n_heads, dv, tq):
        return [pltpu.VMEM((n_heads, 1, tq), F32), pltpu.VMEM((n_heads, dv + ONES_ROWS, tq), F32)]

    @staticmethod
    def store_values_t(vt_ref, hh, cols, v_t):
        dv = v_t.shape[0]
        vt_ref[hh, :dv, cols] = v_t.astype(BF16)
        vt_ref[hh, dv:, cols] = jnp.ones((ONES_ROWS, v_t.shape[1]), BF16)

    def init(self):
        self.m_ref[...] = jnp.full(self.m_ref.shape, NEG_INF, F32)
        self.acc_ref[...] = jnp.zeros(self.acc_ref.shape, F32)

    def step(self, hh, x, v_t, shift=None):
        m = self.m_ref[hh]
        m_cur = jnp.max(x, axis=0, keepdims=True)
        if shift is not None:
            m_cur = m_cur - shift
        m_new = jnp.maximum(m, m_cur)
        alpha = jnp.exp2(m - m_new)
        p = jnp.exp2(x - (m_new if shift is None else m_new + shift))
        self.m_ref[hh] = m_new
        self.acc_ref[hh] = alpha * self.acc_ref[hh] + _dot(v_t, p.astype(BF16))

    def normalized(self, hh):
        return self.acc_ref[hh, :self.dv, :] / self.acc_ref[hh, self.dv:self.dv + 1, :]


def _sweep_blocks(n_past, n_chains, state, score_fn, values_fn, past_logits, own_steps, s_a, s_b):
    def produce(buf, block, chains=range(n_chains)):
        for c in chains:
            buf[c] = score_fn(c, block)

    def consume(buf, block):
        for c in range(n_chains):
            x, shift = past_logits(c, buf[c], block)
            state.step(c, x, values_fn(c, block), shift)

    produced = set()
    for buf, block, chains, _ in own_steps:
        produce(buf, block, [c for c in chains if (id(buf), c) not in produced])
        produced.update((id(buf), c) for c in chains)
    produce(s_a, 0)
    for buf, block, chains, logits_fn in own_steps:
        for c in chains:
            x, shift = logits_fn(c, buf[c])
            state.step(c, x, values_fn(c, block), shift)

    def two_blocks(first):
        produce(s_b, first + 1)
        consume(s_a, first)
        produce(s_a, jnp.minimum(first + 2, n_past - 1))
        consume(s_b, first + 1)

    def quad(t, carry):
        two_blocks(4 * t)
        two_blocks(4 * t + 2)
        return carry

    def pair(t, carry):
        two_blocks(4 * (n_past // 4) + 2 * t)
        return carry

    lax.fori_loop(0, n_past // 4, quad, 0)
    lax.fori_loop(0, (n_past % 4) // 2, pair, 0)

    @pl.when(n_past % 2 == 1)
    def _():
        consume(s_a, n_past - 1)


def _transpose_bf16(x):
    return x.astype(F32).T.astype(BF16)


def _keep_head_rows(x_t, half):
    r = lax.broadcasted_iota(jnp.int32, x_t.shape, 0)
    keep = (r < HEAD_DIM) if half == 0 else (r >= HEAD_DIM)
    return jnp.where(keep, x_t, jnp.zeros_like(x_t))


def _bf16_pieces(x, n):
    pieces = []
    for _ in range(n):
        pieces.append(x.astype(BF16))
        x = x - pieces[-1].astype(F32)
    return pieces


F32_PIECES = 3


def _moba_kernel(slopes_ref, q_ref, k_ref, v_ref, o_ref, kmean_ref, vt_ref, term_ref, kpos_ref, qh_ref,
                 s_a, s_b, s_c, m_ref, acc_ref, *, n_blocks):
    blk = MOBA_BLOCK
    n_heads = MOBA_HEADS_PER_STEP
    group = pl.program_id(1)
    i = pl.program_id(2)
    state = _SoftmaxState(m_ref, acc_ref, HEAD_DIM)
    chain = lambda tile, hh: tile * n_heads + hh
    tile_chains = [[chain(tile, hh) for hh in range(n_heads)] for tile in range(MOBA_Q_TILES)]

    @pl.when(i == 0)
    def _():
        means = []
        pos = lax.broadcasted_iota(jnp.int32, (blk, LANES), 0)
        feature = lax.broadcasted_iota(jnp.int32, (blk, LANES), 1)
        key_pos = jnp.where(feature < F32_PIECES, pos, 0).astype(F32).astype(BF16)
        for j in range(n_blocks):
            rows = slice(j * blk, (j + 1) * blk)
            means.append(jnp.mean(k_ref[0, rows, :].astype(F32), axis=0, keepdims=True))
            for tile in range(n_heads // 2):
                kpos_ref[tile, rows, :LANES] = k_ref[0, rows, tile * LANES:(tile + 1) * LANES]
                kpos_ref[tile, rows, LANES:] = key_pos
            v_t = v_ref[0, rows, :].astype(F32).T
            for hh in range(n_heads):
                state.store_values_t(vt_ref, hh, rows, v_t[hh * HEAD_DIM:(hh + 1) * HEAD_DIM, :])
        for piece, part in enumerate(_bf16_pieces(jnp.concatenate(means, axis=0), F32_PIECES)):
            kmean_ref[piece * n_blocks:(piece + 1) * n_blocks, :] = part

    state.init()
    key = lax.broadcasted_iota(jnp.int32, (blk, blk), 0)
    qry = lax.broadcasted_iota(jnp.int32, (blk, blk), 1)
    blk_idx = lax.broadcasted_iota(jnp.int32, (n_blocks, blk), 0)
    feature_row = lax.broadcasted_iota(jnp.int32, (LANES, blk), 0)
    lane_tile = lambda hh: slice((hh // 2) * LANES, (hh // 2 + 1) * LANES)
    first_block = MOBA_Q_TILES * i

    for tile in range(MOBA_Q_TILES):
        own = first_block + tile
        q_t = q_ref[0, tile * blk:(tile + 1) * blk, :].astype(F32).T
        fully_past = blk_idx < own
        block_dist = ((own - blk_idx) * blk).astype(F32)
        for hh in range(n_heads):
            slope_s = slopes_ref[n_heads * group + hh] * LOG2E
            q_h = _keep_head_rows(q_t[lane_tile(hh), :], hh % 2).astype(BF16)
            pieces = _dot(kmean_ref[:, lane_tile(hh)], q_h)
            gate = pieces[:n_blocks] + pieces[n_blocks:2 * n_blocks] + pieces[2 * n_blocks:]
            work = jnp.where(fully_past, gate, NEG_INF * HEAD_Q_SCALE)
            chosen = jnp.zeros((n_blocks, blk), jnp.bool_)
            for _ in range(min(MOBA_TOPK, n_blocks)):
                best = jnp.max(work, axis=0, keepdims=True)
                first = jnp.min(jnp.where(work == best, blk_idx, n_blocks), axis=0, keepdims=True)
                pick = blk_idx == first
                chosen = jnp.logical_or(chosen, pick)
                work = jnp.where(pick, -jnp.inf, work)
            term_ref[chain(tile, hh)] = jnp.where(jnp.logical_and(chosen, fully_past), slope_s * block_dist,
                                                  SKIP_SHIFT)
            qh_ref[chain(tile, hh), :LANES, :] = q_h
            slope_rows = jnp.zeros((LANES, blk), F32)
            for piece, part in enumerate(_bf16_pieces(jnp.full((1, blk), slope_s, F32), F32_PIECES)):
                slope_rows = jnp.where(feature_row == piece, part.astype(F32), slope_rows)
            qh_ref[chain(tile, hh), LANES:, :] = slope_rows.astype(BF16)

    def rows_of(block):
        return pl.ds(pl.multiple_of(block * blk, blk), blk)

    def scores(c, block):
        return _dot(kpos_ref[(c % n_heads) // 2, rows_of(block), :], qh_ref[c])

    def v_t(c, block):
        return vt_ref[c % n_heads, :, rows_of(block)]

    def past_logits(c, s, block):
        return s, term_ref[c, pl.ds(block, 1), :]

    def own_logits(c, s):
        return jnp.where(key <= qry, s, NEG_INF), None

    assert MOBA_Q_TILES == 2
    own_steps = [(s_b, first_block, tile_chains[0], own_logits),
                 (s_b, first_block, tile_chains[1], functools.partial(past_logits, block=first_block)),
                 (s_c, first_block + 1, tile_chains[1], own_logits)]
    _sweep_blocks(first_block, MOBA_Q_TILES * n_heads, state, scores, v_t, past_logits, own_steps, s_a, s_b)
    for tile in range(MOBA_Q_TILES):
        o_t = jnp.concatenate([state.normalized(c) for c in tile_chains[tile]], axis=0)
        o_ref[0, tile * blk:(tile + 1) * blk, :] = o_t.T.astype(o_ref.dtype)


def _moba_attention(qkv, b, s):
    n_blocks = s // MOBA_BLOCK
    n_heads = MOBA_HEADS_PER_STEP
    width = n_heads * HEAD_DIM
    n_groups = MOBA_HEADS // n_heads
    n_chains = MOBA_Q_TILES * n_heads
    step_rows = MOBA_Q_TILES * MOBA_BLOCK
    assert s % step_rows == 0
    score_buf = pltpu.VMEM((n_chains, MOBA_BLOCK, MOBA_BLOCK), F32)
    return pl.pallas_call(
        functools.partial(_moba_kernel, n_blocks=n_blocks),
        grid=(b, n_groups, s // step_rows),
        in_specs=[pl.BlockSpec(memory_space=pltpu.SMEM),
                  pl.BlockSpec((1, step_rows, width), lambda bi, g, i: (bi, i, g)),
                  pl.BlockSpec((1, s, width), lambda bi, g, i: (bi, 0, n_groups + g)),
                  pl.BlockSpec((1, s, width), lambda bi, g, i: (bi, 0, 2 * n_groups + g))],
        out_specs=pl.BlockSpec((1, step_rows, width), lambda bi, g, i: (bi, i, g)),
        out_shape=jax.ShapeDtypeStruct((b, s, MOBA_HEADS * HEAD_DIM), BF16),
        scratch_shapes=[pltpu.VMEM((3 * n_blocks, width), BF16),
                        pltpu.VMEM((n_heads, HEAD_DIM + ONES_ROWS, s), BF16),
                        pltpu.VMEM((n_chains, n_blocks, MOBA_BLOCK), F32),
                        pltpu.VMEM((n_heads // 2, s, 2 * LANES), BF16),
                        pltpu.VMEM((n_chains, 2 * LANES, MOBA_BLOCK), BF16),
                        score_buf, score_buf, score_buf,
                        *_SoftmaxState.scratch(n_chains, HEAD_DIM, MOBA_BLOCK)],
        compiler_params=_params(("parallel", "parallel", "arbitrary")),
        name="moba_attention",
    )(_alibi_slopes(MOBA_HEADS), qkv, qkv, qkv)


def _lane_half(x, half):
    lane = lax.broadcasted_iota(jnp.int32, x.shape, x.ndim - 1)
    keep = (lane < HEAD_DIM) if half == 0 else (lane >= HEAD_DIM)
    return jnp.where(keep, x, jnp.zeros_like(x))


BANDS_PER_STEP = 2


def _band_attention(n_heads, q_tile, k_tile, v_tile, logits_fn, floor_fn, store_pair, s_ref, with_lse):
    chains = [(band, h) for band in range(BANDS_PER_STEP) for h in range(n_heads)]
    for c, (band, h) in enumerate(chains):
        s_ref[c] = _dot_nt(k_tile(h, band), _lane_half(q_tile(h, band), h % 2))
    even = None
    for c, (band, h) in enumerate(chains):
        x = logits_fn(h, band, s_ref[c])
        m = jnp.max(x, axis=0, keepdims=True)
        floor = floor_fn(h)
        if floor is not None:
            m = jnp.maximum(m, floor)
        p = jnp.exp2(x - m)
        den = jnp.sum(p, axis=0, keepdims=True)
        if floor is not None:
            den = den + jnp.exp2(floor - m)
        o_t = lax.dot_general(v_tile(h, band), p.astype(BF16), (((0,), (0,)), ((), ())),
                              preferred_element_type=F32)
        half = slice((h % 2) * HEAD_DIM, (h % 2 + 1) * HEAD_DIM)
        o_t = o_t[half, :] / den
        lse_t = jnp.broadcast_to((m + jnp.log2(den)) * (1.0 / LOG2E), o_t.shape) if with_lse else None
        if h % 2 == 0:
            even = (o_t, lse_t)
            continue
        store_pair(h // 2, band, jnp.concatenate([even[0], o_t], axis=0).T,
                   jnp.concatenate([even[1], lse_t], axis=0).T if with_lse else None)


def _band_rows(band, size):
    return slice(band * size, (band + 1) * size)


def _band_keys(prev_ref, cur_ref, band, size, cols):
    if band == 0:
        return jnp.concatenate([prev_ref[0, :, cols], cur_ref[0, :size, cols]], axis=0)
    return cur_ref[0, (band - 1) * size:(band + 1) * size, cols]


def _band_geometry(size, n):
    key = lax.broadcasted_iota(jnp.int32, (2 * size, size), 0)
    qry = lax.broadcasted_iota(jnp.int32, (2 * size, size), 1)
    diff = qry + size - key
    in_sequence = (n > 0) | (key >= size)
    return diff, in_sequence


def _dil_kernel(slopes_ref, q_ref, kp_ref, kc_ref, vp_ref, vc_ref, o_ref, lse_ref, s_ref, *, n_pts, dil, group):
    n = pl.program_id(2)
    diff, in_sequence = _band_geometry(DIL_BAND, n)
    window = (diff >= 0) & (diff <= n_pts)
    masks = [window & in_sequence] + [window] * (BANDS_PER_STEP - 1)
    dist = (dil * diff).astype(F32) * LOG2E
    tile = lambda h: slice((h // 2) * LANES, (h // 2 + 1) * LANES)

    def logits(h, band, s):
        slope = slopes_ref[group * DIL_HEADS_PER_GROUP + h]
        return jnp.where(masks[band], s - slope * dist, NEG_INF)

    def store_pair(pair, band, o, lse):
        rows, cols = _band_rows(band, DIL_BAND), slice(pair * LANES, (pair + 1) * LANES)
        o_ref[0, rows, cols] = o
        lse_ref[0, rows, cols] = lse

    _band_attention(
        DIL_HEADS_PER_GROUP,
        lambda h, band: q_ref[0, _band_rows(band, DIL_BAND), tile(h)],
        lambda h, band: _band_keys(kp_ref, kc_ref, band, DIL_BAND, tile(h)),
        lambda h, band: _band_keys(vp_ref, vc_ref, band, DIL_BAND, tile(h)),
        logits, lambda h: None, store_pair, s_ref, with_lse=True)


def _dilated_group(qkv, b, s, group, window, dil):
    width = DIL_HEADS_PER_GROUP * HEAD_DIM
    n_sub = s // dil
    step_rows = BANDS_PER_STEP * DIL_BAND
    assert n_sub % step_rows == 0
    view = qkv.reshape(b, n_sub, dil * 3 * width)

    def col(part):
        return lambda bi, r, n: (bi, n, r * 3 + part)

    def col_prev(part):
        return lambda bi, r, n: (bi, jnp.maximum(n * BANDS_PER_STEP - 1, 0), r * 3 + part)

    blk, prev_blk = (1, step_rows, width), (1, DIL_BAND, width)
    out_spec = pl.BlockSpec(blk, lambda bi, r, n: (bi, n, r))
    out_sds = jax.ShapeDtypeStruct((b, n_sub, dil * width), F32)
    o, lse = pl.pallas_call(
        functools.partial(_dil_kernel, n_pts=window // dil, dil=dil, group=group),
        grid=(b, dil, n_sub // step_rows),
        in_specs=[pl.BlockSpec(memory_space=pltpu.SMEM),
                  pl.BlockSpec(blk, col(0)),
                  pl.BlockSpec(prev_blk, col_prev(1)), pl.BlockSpec(blk, col(1)),
                  pl.BlockSpec(prev_blk, col_prev(2)), pl.BlockSpec(blk, col(2))],
        out_specs=[out_spec, out_spec],
        out_shape=[out_sds, out_sds],
        scratch_shapes=[pltpu.VMEM((BANDS_PER_STEP * DIL_HEADS_PER_GROUP, 2 * DIL_BAND, DIL_BAND), F32)],
        compiler_params=_params(("parallel", "parallel", "arbitrary")),
        name=f"dilated_attention_g{group}",
    )(_alibi_slopes(len(DIL_PAIRS) * DIL_HEADS_PER_GROUP), view, view, view, view, view)
    return o.reshape(b * n_sub, dil * width), lse.reshape(b * n_sub, dil * width)


def _mla_proj_kernel(h_ref, g_ref, wd_ref, gq_ref, gkv_ref, wuq_ref, wuk_ref, wuv_ref,
                     cos_ref, sin_up_ref, sin_dn_ref, q_ref, k_ref, v_ref):
    n = _rms(h_ref[...], g_ref[...]).astype(BF16)
    c = _dot(n, wd_ref[...])
    nq = _rms(c[:, :MLA_Q_RANK], gq_ref[...]).astype(BF16)
    nkv = _rms(c[:, MLA_Q_RANK:MLA_Q_RANK + MLA_KV_RANK], gkv_ref[...]).astype(BF16)
    k_rope = c[:, MLA_Q_RANK + MLA_KV_RANK:]
    half = MLA_ROPE // 2
    pair = 2

    def rope(x, width):
        cos, sin_up, sin_dn = (jnp.tile(t[...], (1, width)) for t in (cos_ref, sin_up_ref, sin_dn_ref))
        return x * cos + pltpu.roll(x, half, 1) * sin_up + pltpu.roll(x, x.shape[1] - half, 1) * sin_dn

    kr = jnp.tile(rope(k_rope, 1), (1, pair))
    for h in range(0, MLA_HEADS, pair):
        cols = slice(h * LANES, (h + pair) * LANES)
        q_ref[:, cols] = (rope(_dot(nq, wuq_ref[:, cols]), pair) * MLA_Q_SCALE).astype(BF16)
        k_ref[:, cols] = (_dot(nkv, wuk_ref[:, cols]) + kr).astype(BF16)
    v_ref[...] = _dot(nkv, wuv_ref[...]).astype(BF16)


def _mla_proj(h, g, w_dkv, q_norm, w_uq, kv_norm, w_ukv, s):
    m, d = h.shape
    qk = MLA_NOPE + MLA_ROPE
    wd = jnp.concatenate([w_dkv[:, :MLA_Q_RANK + MLA_KV_RANK],
                          jnp.zeros((d, MLA_NOPE), F32), w_dkv[:, MLA_Q_RANK + MLA_KV_RANK:],
                          jnp.zeros((d, LANES - qk), F32)], axis=1).astype(BF16)
    wuq = jnp.pad(w_uq.reshape(MLA_Q_RANK, MLA_HEADS, qk), ((0, 0), (0, 0), (0, LANES - qk)))
    wuq = wuq.reshape(MLA_Q_RANK, MLA_HEADS * LANES).astype(BF16)
    w_ukv = w_ukv.reshape(MLA_KV_RANK, MLA_HEADS, MLA_NOPE + MLA_V)
    wuk = jnp.pad(w_ukv[:, :, :MLA_NOPE], ((0, 0), (0, 0), (0, LANES - MLA_NOPE)))
    wuk = wuk.reshape(MLA_KV_RANK, MLA_HEADS * LANES).astype(BF16)
    wuv = w_ukv[:, :, MLA_NOPE:].reshape(MLA_KV_RANK, MLA_HEADS * MLA_V).astype(BF16)
    half = MLA_ROPE // 2
    inv = ROPE_THETA ** (-jnp.arange(0, MLA_ROPE, 2, dtype=F32) / MLA_ROPE)
    ang = jnp.arange(s).astype(F32)[:, None] * inv[None, :]
    cos, sin = jnp.cos(ang), jnp.sin(ang)
    zeros = lambda w: jnp.zeros((s, w), F32)
    cos_t = jnp.concatenate([jnp.ones((s, MLA_NOPE), F32), cos, cos, zeros(LANES - qk)], axis=1)
    sin_up = jnp.concatenate([zeros(MLA_NOPE + half), sin, zeros(LANES - qk)], axis=1)
    sin_dn = jnp.concatenate([zeros(MLA_NOPE), -sin, zeros(half + LANES - qk)], axis=1)

    tiles_per_seq = s // ROW_TILE
    row = lambda width: pl.BlockSpec((ROW_TILE, width), lambda i: (i, 0))
    table = pl.BlockSpec((ROW_TILE, LANES), lambda i: (i % tiles_per_seq, 0))
    return pl.pallas_call(
        _mla_proj_kernel,
        grid=(m // ROW_TILE,),
        in_specs=[row(d), _resident((1, d)), _resident(wd.shape),
                  _resident((1, MLA_Q_RANK)), _resident((1, MLA_KV_RANK)),
                  _resident(wuq.shape), _resident(wuk.shape), _resident(wuv.shape),
                  table, table, table],
        out_specs=[row(MLA_HEADS * LANES), row(MLA_HEADS * LANES), row(MLA_HEADS * MLA_V)],
        out_shape=[jax.ShapeDtypeStruct((m, MLA_HEADS * LANES), BF16),
                   jax.ShapeDtypeStruct((m, MLA_HEADS * LANES), BF16),
                   jax.ShapeDtypeStruct((m, MLA_HEADS * MLA_V), BF16)],
        compiler_params=_params(("parallel",)),
        name="mla_proj",
    )(h, g.reshape(1, d), wd, q_norm.reshape(1, -1), kv_norm.reshape(1, -1), wuq, wuk, wuv,
      cos_t, sin_up, sin_dn)


def _mla_kernel(q_ref, k_ref, v_ref, o_ref, vt_ref, qh_ref, s_a, s_b, s_c, m_ref, acc_ref, *, n_tiles):
    t = MLA_TILE
    n_heads = MLA_HEADS_PER_STEP
    i = pl.program_id(2)
    state = _SoftmaxState(m_ref, acc_ref, MLA_V)
    chain = lambda tile, hh: tile * n_heads + hh
    tile_chains = [[chain(tile, hh) for hh in range(n_heads)] for tile in range(MLA_Q_TILES)]

    @pl.when(i == 0)
    def _():
        for j in range(n_tiles):
            rows = slice(j * t, (j + 1) * t)
            v_t = v_ref[0, rows, :].astype(F32).T
            for hh in range(n_heads):
                state.store_values_t(vt_ref, hh, rows, v_t[hh * MLA_V:(hh + 1) * MLA_V, :])

    state.init()
    key = lax.broadcasted_iota(jnp.int32, (t, t), 0)
    qry = lax.broadcasted_iota(jnp.int32, (t, t), 1)
    lane_tile = lambda hh: slice(hh * LANES, (hh + 1) * LANES)
    for tile in range(MLA_Q_TILES):
        for hh in range(n_heads):
            qh_ref[chain(tile, hh)] = _transpose_bf16(q_ref[0, tile * t:(tile + 1) * t, lane_tile(hh)])

    def rows_of(block):
        return pl.ds(pl.multiple_of(block * t, t), t)

    def scores(c, block):
        return _dot(k_ref[0, rows_of(block), lane_tile(c % n_heads)], qh_ref[c])

    def v_t(c, block):
        return vt_ref[c % n_heads, :, rows_of(block)]

    def past_logits(c, s, block):
        return s, None

    def causal(c, s):
        return jnp.where(key <= qry, s, NEG_INF), None

    def full(c, s):
        return s, None

    assert MLA_Q_TILES == 2
    first = MLA_Q_TILES * i
    own_steps = [(s_b, first, tile_chains[0], causal), (s_b, first, tile_chains[1], full),
                 (s_c, first + 1, tile_chains[1], causal)]
    _sweep_blocks(first, MLA_Q_TILES * n_heads, state, scores, v_t, past_logits, own_steps, s_a, s_b)
    for tile in range(MLA_Q_TILES):
        o_t = jnp.concatenate([state.normalized(c) for c in tile_chains[tile]], axis=0)
        o_ref[0, tile * t:(tile + 1) * t, :] = o_t.T.astype(o_ref.dtype)


def _mla_attention(q, k, v, b, s):
    n_heads = MLA_HEADS_PER_STEP
    n_groups = MLA_HEADS // n_heads
    whole_seq = lambda bi, g, i: (bi, 0, g)
    tile = lambda bi, g, i: (bi, i, g)
    n_chains = MLA_Q_TILES * n_heads
    step_rows = MLA_Q_TILES * MLA_TILE
    assert s % step_rows == 0
    score_buf = pltpu.VMEM((n_chains, MLA_TILE, MLA_TILE), F32)
    return pl.pallas_call(
        functools.partial(_mla_kernel, n_tiles=s // MLA_TILE),
        grid=(b, n_groups, s // step_rows),
        in_specs=[pl.BlockSpec((1, step_rows, n_heads * LANES), tile),
                  pl.BlockSpec((1, s, n_heads * LANES), whole_seq),
                  pl.BlockSpec((1, s, n_heads * MLA_V), whole_seq)],
        out_specs=pl.BlockSpec((1, step_rows, n_heads * MLA_V), tile),
        out_shape=jax.ShapeDtypeStruct((b, s, MLA_HEADS * MLA_V), BF16),
        scratch_shapes=[pltpu.VMEM((n_heads, MLA_V + ONES_ROWS, s), BF16),
                        pltpu.VMEM((n_chains, LANES, MLA_TILE), BF16),
                        score_buf, score_buf, score_buf,
                        *_SoftmaxState.scratch(n_chains, MLA_V, MLA_TILE)],
        compiler_params=_params(("parallel", "parallel", "arbitrary")),
        name="mla_attention",
    )(q, k, v)


def _swa_kernel(slopes_ref, sinks_ref, q_ref, kp_ref, kc_ref, vp_ref, vc_ref, o_ref, s_ref):
    n = pl.program_id(1)
    group = SWA_Q_HEADS // SWA_KV_HEADS
    diff, in_sequence = _band_geometry(SWA_BLOCK, n)
    window = (diff >= 0) & (diff < SWA_WINDOW)
    masks = [window & in_sequence] + [window] * (BANDS_PER_STEP - 1)
    dist = diff.astype(F32) * LOG2E

    def both_halves(prev_ref, cur_ref, band, kv_head):
        x = _band_keys(prev_ref, cur_ref, band, SWA_BLOCK, slice(None))
        part = x[:, kv_head * HEAD_DIM:(kv_head + 1) * HEAD_DIM]
        return jnp.concatenate([part, part], axis=1)

    bands_kv = [(band, kv) for band in range(BANDS_PER_STEP) for kv in range(SWA_KV_HEADS)]
    k_tiles = {bk: both_halves(kp_ref, kc_ref, *bk) for bk in bands_kv}
    v_tiles = {bk: both_halves(vp_ref, vc_ref, *bk) for bk in bands_kv}

    def logits(h, band, s):
        return jnp.where(masks[band], s - slopes_ref[h] * dist, NEG_INF)

    def sink(h):
        return sinks_ref[h] * LOG2E

    def store_pair(pair, band, o, lse):
        o_ref[0, _band_rows(band, SWA_BLOCK), pair * LANES:(pair + 1) * LANES] = o.astype(o_ref.dtype)

    _band_attention(
        SWA_Q_HEADS,
        lambda h, band: q_ref[0, _band_rows(band, SWA_BLOCK), (h // 2) * LANES:(h // 2 + 1) * LANES],
        lambda h, band: k_tiles[band, h // group],
        lambda h, band: v_tiles[band, h // group],
        logits, sink, store_pair, s_ref, with_lse=False)


def _swa_attention(qkv, sinks, b, s):
    q_width = SWA_Q_HEADS * HEAD_DIM
    kv_width = SWA_KV_HEADS * HEAD_DIM
    k_tile = q_width // kv_width
    step_rows = BANDS_PER_STEP * SWA_BLOCK
    assert s % step_rows == 0
    cur = lambda t: (lambda bi, n: (bi, n, t))
    prev = lambda t: (lambda bi, n: (bi, jnp.maximum(n * BANDS_PER_STEP - 1, 0), t))
    kv_blk, kv_prev_blk = (1, step_rows, kv_width), (1, SWA_BLOCK, kv_width)
    return pl.pallas_call(
        _swa_kernel,
        grid=(b, s // step_rows),
        in_specs=[pl.BlockSpec(memory_space=pltpu.SMEM), pl.BlockSpec(memory_space=pltpu.SMEM),
                  pl.BlockSpec((1, step_rows, q_width), cur(0)),
                  pl.BlockSpec(kv_prev_blk, prev(k_tile)), pl.BlockSpec(kv_blk, cur(k_tile)),
                  pl.BlockSpec(kv_prev_blk, prev(k_tile + 1)), pl.BlockSpec(kv_blk, cur(k_tile + 1))],
        out_specs=pl.BlockSpec((1, step_rows, q_width), cur(0)),
        out_shape=jax.ShapeDtypeStruct((b, s, q_width), BF16),
        scratch_shapes=[pltpu.VMEM((BANDS_PER_STEP * SWA_Q_HEADS, 2 * SWA_BLOCK, SWA_BLOCK), F32)],
        compiler_params=_params(("parallel", "arbitrary")),
        name="swa_attention",
    )(_alibi_slopes(SWA_Q_HEADS), sinks.astype(F32), qkv, qkv, qkv, qkv, qkv)


def kernel(x, l0_attn_norm, l0_w_qkv, l0_w_o, l0_mlp_norm, l0_w_up, l0_w_down, l1_attn_norm, l1_w_qkv, l1_w_o, l1_mlp_norm, l1_w_up, l1_w_down, l2_attn_norm, l2_w_dkv, l2_q_norm, l2_w_uq, l2_kv_norm, l2_w_ukv, l2_w_o, l2_mlp_norm, l2_w_up, l2_w_down, l3_attn_norm, l3_w_qkv, l3_sinks, l3_w_o, l3_mlp_norm, l3_w_up, l3_w_down, final_norm):
    b, s, d = x.shape
    bf = lambda w: w.astype(BF16)
    h = x.reshape(b * s, d)

    qkv = _norm_proj(h, l0_attn_norm, bf(l0_w_qkv), scaled_cols=MOBA_HEADS * HEAD_DIM, scale=HEAD_Q_SCALE)
    a = _moba_attention(qkv.reshape(b, s, -1), b, s).reshape(b * s, -1)
    h = _post(h, (a,), bf(l0_w_o), l0_mlp_norm, bf(l0_w_up), bf(l0_w_down))

    qkvs = _dil_proj(h, l1_attn_norm, bf(l1_w_qkv))
    groups = [_dilated_group(qkvs[g], b, s, g, window, dil) for g, (window, dil) in enumerate(DIL_PAIRS)]
    h = _post(h, tuple(o for o, _ in groups) + tuple(l for _, l in groups),
              bf(l1_w_o), l1_mlp_norm, bf(l1_w_up), bf(l1_w_down))

    q, k, v = _mla_proj(h, l2_attn_norm, l2_w_dkv, l2_q_norm, l2_w_uq, l2_kv_norm, l2_w_ukv, s)
    a = _mla_attention(q.reshape(b, s, -1), k.reshape(b, s, -1), v.reshape(b, s, -1), b, s).reshape(b * s, -1)
    h = _post(h, (a,), bf(l2_w_o), l2_mlp_norm, bf(l2_w_up), bf(l2_w_down))

    qkv = _norm_proj(h, l3_attn_norm, bf(l3_w_qkv), col_tile=256, scaled_cols=SWA_Q_HEADS * HEAD_DIM,
                     scale=HEAD_Q_SCALE)
    a = _swa_attention(qkv.reshape(b, s, -1), l3_sinks, b, s).reshape(b * s, -1)
    h = _post(h, (a,), bf(l3_w_o), l3_mlp_norm, bf(l3_w_up), bf(l3_w_down), g_final=final_norm)
    return h.reshape(b, s, d)
```

```python
import functools

import jax
import jax.numpy as jnp
from jax import lax
from jax.experimental import pallas as pl
from jax.experimental.pallas import tpu as pltpu

F32 = jnp.float32
BF16 = jnp.bfloat16

D_MODEL = 1024
HEAD_DIM = 64
RMS_EPS = 1e-6
D_FF = 4 * D_MODEL
NEG_INF = -1e30

MOBA_HEADS = 16
MOBA_BLOCK = 256
MOBA_TOPK = 3

DIL_PAIRS = ((128, 1), (512, 4), (2048, 16))
DIL_HEADS_PER_GROUP = 8
DIL_BAND = 128

MLA_HEADS = 16
MLA_Q_RANK = 768
MLA_KV_RANK = 256
MLA_NOPE = 64
MLA_ROPE = 32
MLA_V = 64
ROPE_THETA = 10000.0
MLA_TILE = 256
MLA_Q_TILES = 2
MLA_HEADS_PER_STEP = 8
MOBA_HEADS_PER_STEP = 8
MOBA_Q_TILES = 2

SWA_Q_HEADS = 16
SWA_KV_HEADS = 2
SWA_WINDOW = 128
SWA_BLOCK = 128

LANES = 128
ROW_TILE = 512
FF_TILE = 512
VMEM_LIMIT = 56 * 1024 * 1024
LOG2E = 1.4426950408889634
HEAD_Q_SCALE = HEAD_DIM ** -0.5 * LOG2E
MLA_Q_SCALE = (MLA_NOPE + MLA_ROPE) ** -0.5 * LOG2E


def _params(semantics):
    return pltpu.CompilerParams(dimension_semantics=semantics, vmem_limit_bytes=VMEM_LIMIT)


def _resident(shape):
    return pl.BlockSpec(shape, lambda *_: (0,) * len(shape), pipeline_mode=pl.Buffered(1))


def _alibi_slopes(n_heads):
    return 2.0 ** (-8.0 * jnp.arange(1, n_heads + 1, dtype=F32) / n_heads)


def _rms(x, g):
    return x * lax.rsqrt(jnp.mean(x * x, axis=-1, keepdims=True) + RMS_EPS) * g


def _dot(a, b):
    return jnp.dot(a, b, preferred_element_type=F32)


def _dot_nt(a, b):
    return lax.dot_general(a, b, (((1,), (1,)), ((), ())), preferred_element_type=F32)


def _norm_proj_kernel(h_ref, g_ref, w_ref, o_ref, *, col_tile, scaled_cols, scale):
    n = _rms(h_ref[...], g_ref[...]).astype(BF16)
    for c in range(o_ref.shape[1] // col_tile):
        cols = slice(c * col_tile, (c + 1) * col_tile)
        y = _dot(n, w_ref[:, cols])
        if (c + 1) * col_tile <= scaled_cols:
            y = y * scale
        o_ref[:, cols] = y.astype(o_ref.dtype)


def _norm_proj(h, g, w, col_tile=512, scaled_cols=0, scale=1.0):
    m, d = h.shape
    n_out = w.shape[1]
    assert scaled_cols % col_tile == 0
    return pl.pallas_call(
        functools.partial(_norm_proj_kernel, col_tile=col_tile, scaled_cols=scaled_cols, scale=scale),
        grid=(m // ROW_TILE,),
        in_specs=[pl.BlockSpec((ROW_TILE, d), lambda i: (i, 0)),
                  _resident((1, d)),
                  _resident((d, n_out))],
        out_specs=pl.BlockSpec((ROW_TILE, n_out), lambda i: (i, 0)),
        out_shape=jax.ShapeDtypeStruct((m, n_out), BF16),
        compiler_params=_params(("parallel",)),
        name="norm_proj",
    )(h, g.reshape(1, d), w)


def _dil_proj_kernel(h_ref, g_ref, w_ref, o0_ref, o1_ref, o2_ref, stage_ref, sorted_ref):
    n = _rms(h_ref[...], g_ref[...])
    rows, d = n.shape
    width = DIL_HEADS_PER_GROUP * HEAD_DIM
    for t in range(d // LANES):
        stage_ref[t] = n[:, t * LANES:(t + 1) * LANES]
    n = n.astype(BF16)
    for g, (o_ref, (_, dil)) in enumerate(zip((o0_ref, o1_ref, o2_ref), DIL_PAIRS)):
        per = rows // dil
        for r in range(dil if dil > 1 else 0):
            for t in range(d // LANES):
                sorted_ref[g - 1, r * per:(r + 1) * per, t * LANES:(t + 1) * LANES] = (
                    stage_ref[t, pl.ds(r, per, stride=dil), :].astype(BF16))
        lhs = n if dil == 1 else sorted_ref[g - 1]
        for part in range(3):
            src = (3 * g + part) * width
            y = _dot(lhs, w_ref[:, src:src + width])
            if part == 0:
                y = y * HEAD_Q_SCALE
            for r in range(dil):
                dst = (3 * r + part) * width
                o_ref[:, dst:dst + width] = y[r * per:(r + 1) * per, :].astype(BF16)


def _dil_proj(h, g, w):
    m, d = h.shape
    feat = 3 * DIL_HEADS_PER_GROUP * HEAD_DIM
    assert DIL_PAIRS[0][1] == 1
    view = lambda dil: (m // dil, dil * feat)
    view_block = lambda dil: pl.BlockSpec((ROW_TILE // dil, dil * feat), lambda i: (i, 0))
    return pl.pallas_call(
        _dil_proj_kernel,
        grid=(m // ROW_TILE,),
        in_specs=[pl.BlockSpec((ROW_TILE, d), lambda i: (i, 0)), _resident((1, d)), _resident(w.shape)],
        out_specs=[view_block(dil) for _, dil in DIL_PAIRS],
        out_shape=[jax.ShapeDtypeStruct(view(dil), BF16) for _, dil in DIL_PAIRS],
        scratch_shapes=[pltpu.VMEM((d // LANES, ROW_TILE, LANES), F32),
                        pltpu.VMEM((len(DIL_PAIRS) - 1, ROW_TILE, d), BF16)],
        compiler_params=_params(("parallel",)),
        name="dil_proj",
    )(h, g.reshape(1, d), w)


def _mlp_tail(h1, g_ref, wup_ref, wdn_ref, gf_ref, out_ref):
    n = _rms(h1, g_ref[...]).astype(BF16)
    acc = jnp.zeros_like(h1)
    for c in range(wup_ref.shape[1] // FF_TILE):
        cols = slice(c * FF_TILE, (c + 1) * FF_TILE)
        u = jnp.square(jnp.maximum(_dot(n, wup_ref[:, cols]), 0.0)).astype(BF16)
        acc = acc + _dot(u, wdn_ref[cols, :])
    out = h1 + acc
    if gf_ref is not None:
        out = _rms(out, gf_ref[...])
    out_ref[...] = out


def _post_kernel(h_ref, a_ref, wo_ref, g_ref, wup_ref, wdn_ref, *rest):
    gf_ref, out_ref = (rest[0], rest[1]) if len(rest) == 2 else (None, rest[0])
    h1 = h_ref[...] + _dot(a_ref[...], wo_ref[...])
    _mlp_tail(h1, g_ref, wup_ref, wdn_ref, gf_ref, out_ref)


def _post_merge_kernel(h_ref, o0_ref, o1_ref, o2_ref, l0_ref, l1_ref, l2_ref,
                       wo_ref, g_ref, wup_ref, wdn_ref, out_ref, stage_ref):
    rows, width = o0_ref.shape

    def token_rows(x_ref, slot, dil):
        if dil == 1:
            return x_ref[...]
        for r in range(dil):
            for t in range(width // LANES):
                src = r * width + t * LANES
                stage_ref[slot, t, pl.ds(r, rows // dil, stride=dil), :] = x_ref[:, src:src + LANES]
        return jnp.concatenate([stage_ref[slot, t] for t in range(width // LANES)], axis=1)

    dils = [dil for _, dil in DIL_PAIRS]
    l0, l1, l2 = (token_rows(ref, slot, dil) for slot, (ref, dil) in enumerate(zip((l0_ref, l1_ref, l2_ref), dils)))
    o0, o1, o2 = (token_rows(ref, 3 + slot, dil) for slot, (ref, dil) in enumerate(zip((o0_ref, o1_ref, o2_ref), dils)))
    mx = jnp.maximum(jnp.maximum(l0, l1), l2)
    e0, e1, e2 = jnp.exp(l0 - mx), jnp.exp(l1 - mx), jnp.exp(l2 - mx)
    merged = (e0 * o0 + e1 * o1 + e2 * o2) / (e0 + e1 + e2)
    h1 = h_ref[...] + _dot(merged.astype(BF16), wo_ref[...])
    _mlp_tail(h1, g_ref, wup_ref, wdn_ref, None, out_ref)


def _post(h, attn_ins, w_o, g, w_up, w_down, g_final=None):
    m, d = h.shape
    row = lambda width: pl.BlockSpec((ROW_TILE, width), lambda i: (i, 0))
    merge = len(attn_ins) > 1
    in_specs = [row(d)] + [pl.BlockSpec((ROW_TILE * a.shape[0] // m, a.shape[1]), lambda i: (i, 0)) for a in attn_ins]
    in_specs += [_resident(w_o.shape), _resident((1, d)), _resident(w_up.shape), _resident(w_down.shape)]
    args = [h, *attn_ins, w_o, g.reshape(1, d), w_up, w_down]
    if g_final is not None:
        in_specs.append(_resident((1, d)))
        args.append(g_final.reshape(1, d))
    scratch = [pltpu.VMEM((len(attn_ins), w_o.shape[0] // LANES, ROW_TILE, LANES), F32)] if merge else []
    return pl.pallas_call(
        _post_merge_kernel if merge else _post_kernel,
        grid=(m // ROW_TILE,),
        in_specs=in_specs,
        out_specs=row(d),
        out_shape=jax.ShapeDtypeStruct((m, d), F32),
        scratch_shapes=scratch,
        compiler_params=_params(("parallel",)),
        name="post_mlp",
    )(*args)


ONES_ROWS = 16
SKIP_SHIFT = -2.0 * NEG_INF


class _SoftmaxState:
    def __init__(self, m_ref, acc_ref, dv):
        self.m_ref, self.acc_ref, self.dv = m_ref, acc_ref, dv

    @staticmethod
    def scratch(n_heads, dv, tq):
        return [pltpu.VMEM((n_heads, 1, tq), F32), pltpu.VMEM((n_heads, dv + ONES_ROWS, tq), F32)]

    @staticmethod
    def store_values_t(vt_ref, hh, cols, v_t):
        dv = v_t.shape[0]
        vt_ref[hh, :dv, cols] = v_t.astype(BF16)
        vt_ref[hh, dv:, cols] = jnp.ones((ONES_ROWS, v_t.shape[1]), BF16)

    def init(self):
        self.m_ref[...] = jnp.full(self.m_ref.shape, NEG_INF, F32)
        self.acc_ref[...] = jnp.zeros(self.acc_ref.shape, F32)

    def step(self, hh, x, v_t, shift=None):
        m = self.m_ref[hh]
        m_cur = jnp.max(x, axis=0, keepdims=True)
        if shift is not None:
            m_cur = m_cur - shift
        m_new = jnp.maximum(m, m_cur)
        alpha = jnp.exp2(m - m_new)
        p = jnp.exp2(x - (m_new if shift is None else m_new + shift))
        self.m_ref[hh] = m_new
        self.acc_ref[hh] = alpha * self.acc_ref[hh] + _dot(v_t, p.astype(BF16))

    def normalized(self, hh):
        return self.acc_ref[hh, :self.dv, :] / self.acc_ref[hh, self.dv:self.dv + 1, :]


def _sweep_blocks(n_past, n_chains, state, score_fn, values_fn, past_logits, own_steps, s_a, s_b):
    def produce(buf, block, chains=range(n_chains)):
        for c in chains:
            buf[c] = score_fn(c, block)

    def consume(buf, block):
        for c in range(n_chains):
            x, shift = past_logits(c, buf[c], block)
            state.step(c, x, values_fn(c, block), shift)

    produced = set()
    for buf, block, chains, _ in own_steps:
        produce(buf, block, [c for c in chains if (id(buf), c) not in produced])
        produced.update((id(buf), c) for c in chains)
    for c in range(n_chains):
        s_a[c] = score_fn(c, 0)
        for buf, block, chains, logits_fn in own_steps:
            if c in chains:
                x, shift = logits_fn(c, buf[c])
                state.step(c, x, values_fn(c, block), shift)

    def advance(src, block, dst, next_block):
        for c in range(n_chains):
            dst[c] = score_fn(c, next_block)
            x, shift = past_logits(c, src[c], block)
            state.step(c, x, values_fn(c, block), shift)

    def two_blocks(first):
        advance(s_a, first, s_b, first + 1)
        advance(s_b, first + 1, s_a, jnp.minimum(first + 2, n_past - 1))

    def quad(t, carry):
        two_blocks(4 * t)
        two_blocks(4 * t + 2)
        return carry

    def pair(t, carry):
        two_blocks(4 * (n_past // 4) + 2 * t)
        return carry

    lax.fori_loop(0, n_past // 4, quad, 0)
    lax.fori_loop(0, (n_past % 4) // 2, pair, 0)

    @pl.when(n_past % 2 == 1)
    def _():
        consume(s_a, n_past - 1)


def _transpose_bf16(x):
    return x.astype(F32).T.astype(BF16)


def _keep_head_rows(x_t, half):
    r = lax.broadcasted_iota(jnp.int32, x_t.shape, 0)
    keep = (r < HEAD_DIM) if half == 0 else (r >= HEAD_DIM)
    return jnp.where(keep, x_t, jnp.zeros_like(x_t))


def _bf16_pieces(x, n):
    pieces = []
    for _ in range(n):
        pieces.append(x.astype(BF16))
        x = x - pieces[-1].astype(F32)
    return pieces


F32_PIECES = 3


def _moba_kernel(slopes_ref, q_ref, k_ref, v_ref, o_ref, kmean_ref, vt_ref, term_ref, kpos_ref, qh_ref,
                 s_a, s_b, s_c, m_ref, acc_ref, *, n_blocks):
    blk = MOBA_BLOCK
    n_heads = MOBA_HEADS_PER_STEP
    group = pl.program_id(1)
    i = pl.program_id(2)
    state = _SoftmaxState(m_ref, acc_ref, HEAD_DIM)
    chain = lambda tile, hh: tile * n_heads + hh
    tile_chains = [[chain(tile, hh) for hh in range(n_heads)] for tile in range(MOBA_Q_TILES)]

    @pl.when(i == 0)
    def _():
        means = []
        pos = lax.broadcasted_iota(jnp.int32, (blk, LANES), 0)
        feature = lax.broadcasted_iota(jnp.int32, (blk, LANES), 1)
        key_pos = jnp.where(feature < F32_PIECES, pos, 0).astype(F32).astype(BF16)
        for j in range(n_blocks):
            rows = slice(j * blk, (j + 1) * blk)
            means.append(jnp.mean(k_ref[0, rows, :].astype(F32), axis=0, keepdims=True))
            for tile in range(n_heads // 2):
                kpos_ref[tile, rows, :LANES] = k_ref[0, rows, tile * LANES:(tile + 1) * LANES]
                kpos_ref[tile, rows, LANES:] = key_pos
            v_t = v_ref[0, rows, :].astype(F32).T
            for hh in range(n_heads):
                state.store_values_t(vt_ref, hh, rows, v_t[hh * HEAD_DIM:(hh + 1) * HEAD_DIM, :])
        for piece, part in enumerate(_bf16_pieces(jnp.concatenate(means, axis=0), F32_PIECES)):
            kmean_ref[piece * n_blocks:(piece + 1) * n_blocks, :] = part

    state.init()
    key = lax.broadcasted_iota(jnp.int32, (blk, blk), 0)
    qry = lax.broadcasted_iota(jnp.int32, (blk, blk), 1)
    blk_idx = lax.broadcasted_iota(jnp.int32, (n_blocks, blk), 0)
    feature_row = lax.broadcasted_iota(jnp.int32, (LANES, blk), 0)
    lane_tile = lambda hh: slice((hh // 2) * LANES, (hh // 2 + 1) * LANES)
    first_block = MOBA_Q_TILES * i

    for tile in range(MOBA_Q_TILES):
        own = first_block + tile
        q_t = q_ref[0, tile * blk:(tile + 1) * blk, :].astype(F32).T
        fully_past = blk_idx < own
        block_dist = ((own - blk_idx) * blk).astype(F32)
        for hh in range(n_heads):
            slope_s = slopes_ref[n_heads * group + hh] * LOG2E
            q_h = _keep_head_rows(q_t[lane_tile(hh), :], hh % 2).astype(BF16)
            pieces = _dot(kmean_ref[:, lane_tile(hh)], q_h)
            gate = pieces[:n_blocks] + pieces[n_blocks:2 * n_blocks] + pieces[2 * n_blocks:]
            work = jnp.where(fully_past, gate, NEG_INF * HEAD_Q_SCALE)
            chosen = jnp.zeros((n_blocks, blk), jnp.bool_)
            for _ in range(min(MOBA_TOPK, n_blocks)):
                best = jnp.max(work, axis=0, keepdims=True)
                first = jnp.min(jnp.where(work == best, blk_idx, n_blocks), axis=0, keepdims=True)
                pick = blk_idx == first
                chosen = jnp.logical_or(chosen, pick)
                work = jnp.where(pick, -jnp.inf, work)
            term_ref[chain(tile, hh)] = jnp.where(jnp.logical_and(chosen, fully_past), slope_s * block_dist,
                                                  SKIP_SHIFT)
            qh_ref[chain(tile, hh), :LANES, :] = q_h
            slope_rows = jnp.zeros((LANES, blk), F32)
            for piece, part in enumerate(_bf16_pieces(jnp.full((1, blk), slope_s, F32), F32_PIECES)):
                slope_rows = jnp.where(feature_row == piece, part.astype(F32), slope_rows)
            qh_ref[chain(tile, hh), LANES:, :] = slope_rows.astype(BF16)

    def rows_of(block):
        return pl.ds(pl.multiple_of(block * blk, blk), blk)

    def scores(c, block):
        return _dot(kpos_ref[(c % n_heads) // 2, rows_of(block), :], qh_ref[c])

    def v_t(c, block):
        return vt_ref[c % n_heads, :, rows_of(block)]

    def past_logits(c, s, block):
        return s, term_ref[c, pl.ds(block, 1), :]

    def own_logits(c, s):
        return jnp.where(key <= qry, s, NEG_INF), None

    assert MOBA_Q_TILES == 2
    own_steps = [(s_b, first_block, tile_chains[0], own_logits),
                 (s_b, first_block, tile_chains[1], functools.partial(past_logits, block=first_block)),
                 (s_c, first_block + 1, tile_chains[1], own_logits)]
    _sweep_blocks(first_block, MOBA_Q_TILES * n_heads, state, scores, v_t, past_logits, own_steps, s_a, s_b)
    for tile in range(MOBA_Q_TILES):
        o_t = jnp.concatenate([state.normalized(c) for c in tile_chains[tile]], axis=0)
        o_ref[0, tile * blk:(tile + 1) * blk, :] = o_t.T.astype(o_ref.dtype)


def _moba_attention(qkv, b, s):
    n_blocks = s // MOBA_BLOCK
    n_heads = MOBA_HEADS_PER_STEP
    width = n_heads * HEAD_DIM
    n_groups = MOBA_HEADS // n_heads
    n_chains = MOBA_Q_TILES * n_heads
    step_rows = MOBA_Q_TILES * MOBA_BLOCK
    assert s % step_rows == 0
    score_buf = pltpu.VMEM((n_chains, MOBA_BLOCK, MOBA_BLOCK), F32)
    return pl.pallas_call(
        functools.partial(_moba_kernel, n_blocks=n_blocks),
        grid=(b, n_groups, s // step_rows),
        in_specs=[pl.BlockSpec(memory_space=pltpu.SMEM),
                  pl.BlockSpec((1, step_rows, width), lambda bi, g, i: (bi, i, g)),
                  pl.BlockSpec((1, s, width), lambda bi, g, i: (bi, 0, n_groups + g)),
                  pl.BlockSpec((1, s, width), lambda bi, g, i: (bi, 0, 2 * n_groups + g))],
        out_specs=pl.BlockSpec((1, step_rows, width), lambda bi, g, i: (bi, i, g)),
        out_shape=jax.ShapeDtypeStruct((b, s, MOBA_HEADS * HEAD_DIM), BF16),
        scratch_shapes=[pltpu.VMEM((3 * n_blocks, width), BF16),
                        pltpu.VMEM((n_heads, HEAD_DIM + ONES_ROWS, s), BF16),
                        pltpu.VMEM((n_chains, n_blocks, MOBA_BLOCK), F32),
                        pltpu.VMEM((n_heads // 2, s, 2 * LANES), BF16),
                        pltpu.VMEM((n_chains, 2 * LANES, MOBA_BLOCK), BF16),
                        score_buf, score_buf, score_buf,
                        *_SoftmaxState.scratch(n_chains, HEAD_DIM, MOBA_BLOCK)],
        compiler_params=_params(("parallel", "parallel", "arbitrary")),
        name="moba_attention",
    )(_alibi_slopes(MOBA_HEADS), qkv, qkv, qkv)


def _lane_half(x, half):
    lane = lax.broadcasted_iota(jnp.int32, x.shape, x.ndim - 1)
    keep = (lane < HEAD_DIM) if half == 0 else (lane >= HEAD_DIM)
    return jnp.where(keep, x, jnp.zeros_like(x))


BANDS_PER_STEP = 2


def _band_attention(n_heads, q_tile, k_tile, v_tile, logits_fn, floor_fn, store_pair, s_ref, with_lse):
    chains = [(band, h) for band in range(BANDS_PER_STEP) for h in range(n_heads)]

    def produce(c):
        band, h = chains[c]
        s_ref[c] = _dot_nt(k_tile(h, band), _lane_half(q_tile(h, band), h % 2))

    lookahead = 2
    for c in range(min(lookahead, len(chains))):
        produce(c)
    even = None
    for c, (band, h) in enumerate(chains):
        if c + lookahead < len(chains):
            produce(c + lookahead)
        x = logits_fn(h, band, s_ref[c])
        m = jnp.max(x, axis=0, keepdims=True)
        floor = floor_fn(h)
        if floor is not None:
            m = jnp.maximum(m, floor)
        p = jnp.exp2(x - m)
        den = jnp.sum(p, axis=0, keepdims=True)
        if floor is not None:
            den = den + jnp.exp2(floor - m)
        o_t = lax.dot_general(v_tile(h, band), p.astype(BF16), (((0,), (0,)), ((), ())),
                              preferred_element_type=F32)
        half = slice((h % 2) * HEAD_DIM, (h % 2 + 1) * HEAD_DIM)
        o_t = o_t[half, :] / den
        lse_t = jnp.broadcast_to((m + jnp.log2(den)) * (1.0 / LOG2E), o_t.shape) if with_lse else None
        if h % 2 == 0:
            even = (o_t, lse_t)
            continue
        store_pair(h // 2, band, jnp.concatenate([even[0], o_t], axis=0).T,
                   jnp.concatenate([even[1], lse_t], axis=0).T if with_lse else None)


def _band_rows(band, size):
    return slice(band * size, (band + 1) * size)


def _band_keys(prev_ref, cur_ref, band, size, cols):
    if band == 0:
        return jnp.concatenate([prev_ref[0, :, cols], cur_ref[0, :size, cols]], axis=0)
    return cur_ref[0, (band - 1) * size:(band + 1) * size, cols]


def _band_geometry(size, n):
    key = lax.broadcasted_iota(jnp.int32, (2 * size, size), 0)
    qry = lax.broadcasted_iota(jnp.int32, (2 * size, size), 1)
    diff = qry + size - key
    in_sequence = (n > 0) | (key >= size)
    return diff, in_sequence


def _dil_kernel(slopes_ref, q_ref, kp_ref, kc_ref, vp_ref, vc_ref, o_ref, lse_ref, s_ref, *, n_pts, dil, group):
    n = pl.program_id(2)
    diff, in_sequence = _band_geometry(DIL_BAND, n)
    window = (diff >= 0) & (diff <= n_pts)
    masks = [window & in_sequence] + [window] * (BANDS_PER_STEP - 1)
    dist = (dil * diff).astype(F32) * LOG2E
    tile = lambda h: slice((h // 2) * LANES, (h // 2 + 1) * LANES)

    def logits(h, band, s):
        slope = slopes_ref[group * DIL_HEADS_PER_GROUP + h]
        return jnp.where(masks[band], s - slope * dist, NEG_INF)

    def store_pair(pair, band, o, lse):
        rows, cols = _band_rows(band, DIL_BAND), slice(pair * LANES, (pair + 1) * LANES)
        o_ref[0, rows, cols] = o
        lse_ref[0, rows, cols] = lse

    _band_attention(
        DIL_HEADS_PER_GROUP,
        lambda h, band: q_ref[0, _band_rows(band, DIL_BAND), tile(h)],
        lambda h, band: _band_keys(kp_ref, kc_ref, band, DIL_BAND, tile(h)),
        lambda h, band: _band_keys(vp_ref, vc_ref, band, DIL_BAND, tile(h)),
        logits, lambda h: None, store_pair, s_ref, with_lse=True)


def _dilated_group(qkv, b, s, group, window, dil):
    width = DIL_HEADS_PER_GROUP * HEAD_DIM
    n_sub = s // dil
    step_rows = BANDS_PER_STEP * DIL_BAND
    assert n_sub % step_rows == 0
    view = qkv.reshape(b, n_sub, dil * 3 * width)

    def col(part):
        return lambda bi, r, n: (bi, n, r * 3 + part)

    def col_prev(part):
        return lambda bi, r, n: (bi, jnp.maximum(n * BANDS_PER_STEP - 1, 0), r * 3 + part)

    blk, prev_blk = (1, step_rows, width), (1, DIL_BAND, width)
    out_spec = pl.BlockSpec(blk, lambda bi, r, n: (bi, n, r))
    out_sds = jax.ShapeDtypeStruct((b, n_sub, dil * width), F32)
    o, lse = pl.pallas_call(
        functools.partial(_dil_kernel, n_pts=window // dil, dil=dil, group=group),
        grid=(b, dil, n_sub // step_rows),
        in_specs=[pl.BlockSpec(memory_space=pltpu.SMEM),
                  pl.BlockSpec(blk, col(0)),
                  pl.BlockSpec(prev_blk, col_prev(1)), pl.BlockSpec(blk, col(1)),
                  pl.BlockSpec(prev_blk, col_prev(2)), pl.BlockSpec(blk, col(2))],
        out_specs=[out_spec, out_spec],
        out_shape=[out_sds, out_sds],
        scratch_shapes=[pltpu.VMEM((BANDS_PER_STEP * DIL_HEADS_PER_GROUP, 2 * DIL_BAND, DIL_BAND), F32)],
        compiler_params=_params(("parallel", "parallel", "arbitrary")),
        name=f"dilated_attention_g{group}",
    )(_alibi_slopes(len(DIL_PAIRS) * DIL_HEADS_PER_GROUP), view, view, view, view, view)
    return o.reshape(b * n_sub, dil * width), lse.reshape(b * n_sub, dil * width)


def _mla_proj_kernel(h_ref, g_ref, wd_ref, gq_ref, gkv_ref, wuq_ref, wuk_ref, wuv_ref,
                     cos_ref, sin_up_ref, sin_dn_ref, q_ref, k_ref, v_ref):
    n = _rms(h_ref[...], g_ref[...]).astype(BF16)
    c = _dot(n, wd_ref[...])
    nq = _rms(c[:, :MLA_Q_RANK], gq_ref[...]).astype(BF16)
    nkv = _rms(c[:, MLA_Q_RANK:MLA_Q_RANK + MLA_KV_RANK], gkv_ref[...]).astype(BF16)
    k_rope = c[:, MLA_Q_RANK + MLA_KV_RANK:]
    half = MLA_ROPE // 2
    pair = 2

    def rope(x, width):
        cos, sin_up, sin_dn = (jnp.tile(t[...], (1, width)) for t in (cos_ref, sin_up_ref, sin_dn_ref))
        return x * cos + pltpu.roll(x, half, 1) * sin_up + pltpu.roll(x, x.shape[1] - half, 1) * sin_dn

    kr = jnp.tile(rope(k_rope, 1), (1, pair))
    for h in range(0, MLA_HEADS, pair):
        cols = slice(h * LANES, (h + pair) * LANES)
        q_ref[:, cols] = (rope(_dot(nq, wuq_ref[:, cols]), pair) * MLA_Q_SCALE).astype(BF16)
        k_ref[:, cols] = (_dot(nkv, wuk_ref[:, cols]) + kr).astype(BF16)
    v_ref[...] = _dot(nkv, wuv_ref[...]).astype(BF16)


def _mla_proj(h, g, w_dkv, q_norm, w_uq, kv_norm, w_ukv, s):
    m, d = h.shape
    qk = MLA_NOPE + MLA_ROPE
    wd = jnp.concatenate([w_dkv[:, :MLA_Q_RANK + MLA_KV_RANK],
                          jnp.zeros((d, MLA_NOPE), F32), w_dkv[:, MLA_Q_RANK + MLA_KV_RANK:],
                          jnp.zeros((d, LANES - qk), F32)], axis=1).astype(BF16)
    wuq = jnp.pad(w_uq.reshape(MLA_Q_RANK, MLA_HEADS, qk), ((0, 0), (0, 0), (0, LANES - qk)))
    wuq = wuq.reshape(MLA_Q_RANK, MLA_HEADS * LANES).astype(BF16)
    w_ukv = w_ukv.reshape(MLA_KV_RANK, MLA_HEADS, MLA_NOPE + MLA_V)
    wuk = jnp.pad(w_ukv[:, :, :MLA_NOPE], ((0, 0), (0, 0), (0, LANES - MLA_NOPE)))
    wuk = wuk.reshape(MLA_KV_RANK, MLA_HEADS * LANES).astype(BF16)
    wuv = w_ukv[:, :, MLA_NOPE:].reshape(MLA_KV_RANK, MLA_HEADS * MLA_V).astype(BF16)
    half = MLA_ROPE // 2
    inv = ROPE_THETA ** (-jnp.arange(0, MLA_ROPE, 2, dtype=F32) / MLA_ROPE)
    ang = jnp.arange(s).astype(F32)[:, None] * inv[None, :]
    cos, sin = jnp.cos(ang), jnp.sin(ang)
    zeros = lambda w: jnp.zeros((s, w), F32)
    cos_t = jnp.concatenate([jnp.ones((s, MLA_NOPE), F32), cos, cos, zeros(LANES - qk)], axis=1)
    sin_up = jnp.concatenate([zeros(MLA_NOPE + half), sin, zeros(LANES - qk)], axis=1)
    sin_dn = jnp.concatenate([zeros(MLA_NOPE), -sin, zeros(half + LANES - qk)], axis=1)

    tiles_per_seq = s // ROW_TILE
    row = lambda width: pl.BlockSpec((ROW_TILE, width), lambda i: (i, 0))
    table = pl.BlockSpec((ROW_TILE, LANES), lambda i: (i % tiles_per_seq, 0))
    return pl.pallas_call(
        _mla_proj_kernel,
        grid=(m // ROW_TILE,),
        in_specs=[row(d), _resident((1, d)), _resident(wd.shape),
                  _resident((1, MLA_Q_RANK)), _resident((1, MLA_KV_RANK)),
                  _resident(wuq.shape), _resident(wuk.shape), _resident(wuv.shape),
                  table, table, table],
        out_specs=[row(MLA_HEADS * LANES), row(MLA_HEADS * LANES), row(MLA_HEADS * MLA_V)],
        out_shape=[jax.ShapeDtypeStruct((m, MLA_HEADS * LANES), BF16),
                   jax.ShapeDtypeStruct((m, MLA_HEADS * LANES), BF16),
                   jax.ShapeDtypeStruct((m, MLA_HEADS * MLA_V), BF16)],
        compiler_params=_params(("parallel",)),
        name="mla_proj",
    )(h, g.reshape(1, d), wd, q_norm.reshape(1, -1), kv_norm.reshape(1, -1), wuq, wuk, wuv,
      cos_t, sin_up, sin_dn)


def _mla_kernel(q_ref, k_ref, v_ref, o_ref, vt_ref, qh_ref, s_a, s_b, s_c, m_ref, acc_ref, *, n_tiles):
    t = MLA_TILE
    n_heads = MLA_HEADS_PER_STEP
    i = pl.program_id(2)
    state = _SoftmaxState(m_ref, acc_ref, MLA_V)
    chain = lambda tile, hh: tile * n_heads + hh
    tile_chains = [[chain(tile, hh) for hh in range(n_heads)] for tile in range(MLA_Q_TILES)]

    @pl.when(i == 0)
    def _():
        for j in range(n_tiles):
            rows = slice(j * t, (j + 1) * t)
            v_t = v_ref[0, rows, :].astype(F32).T
            for hh in range(n_heads):
                state.store_values_t(vt_ref, hh, rows, v_t[hh * MLA_V:(hh + 1) * MLA_V, :])

    state.init()
    key = lax.broadcasted_iota(jnp.int32, (t, t), 0)
    qry = lax.broadcasted_iota(jnp.int32, (t, t), 1)
    lane_tile = lambda hh: slice(hh * LANES, (hh + 1) * LANES)
    for tile in range(MLA_Q_TILES):
        for hh in range(n_heads):
            qh_ref[chain(tile, hh)] = _transpose_bf16(q_ref[0, tile * t:(tile + 1) * t, lane_tile(hh)])

    def rows_of(block):
        return pl.ds(pl.multiple_of(block * t, t), t)

    def scores(c, block):
        return _dot(k_ref[0, rows_of(block), lane_tile(c % n_heads)], qh_ref[c])

    def v_t(c, block):
        return vt_ref[c % n_heads, :, rows_of(block)]

    def past_logits(c, s, block):
        return s, None

    def causal(c, s):
        return jnp.where(key <= qry, s, NEG_INF), None

    def full(c, s):
        return s, None

    assert MLA_Q_TILES == 2
    first = MLA_Q_TILES * i
    own_steps = [(s_b, first, tile_chains[0], causal), (s_b, first, tile_chains[1], full),
                 (s_c, first + 1, tile_chains[1], causal)]
    _sweep_blocks(first, MLA_Q_TILES * n_heads, state, scores, v_t, past_logits, own_steps, s_a, s_b)
    for tile in range(MLA_Q_TILES):
        o_t = jnp.concatenate([state.normalized(c) for c in tile_chains[tile]], axis=0)
        o_ref[0, tile * t:(tile + 1) * t, :] = o_t.T.astype(o_ref.dtype)


def _mla_attention(q, k, v, b, s):
    n_heads = MLA_HEADS_PER_STEP
    n_groups = MLA_HEADS // n_heads
    whole_seq = lambda bi, g, i: (bi, 0, g)
    tile = lambda bi, g, i: (bi, i, g)
    n_chains = MLA_Q_TILES * n_heads
    step_rows = MLA_Q_TILES * MLA_TILE
    assert s % step_rows == 0
    score_buf = pltpu.VMEM((n_chains, MLA_TILE, MLA_TILE), F32)
    return pl.pallas_call(
        functools.partial(_mla_kernel, n_tiles=s // MLA_TILE),
        grid=(b, n_groups, s // step_rows),
        in_specs=[pl.BlockSpec((1, step_rows, n_heads * LANES), tile),
                  pl.BlockSpec((1, s, n_heads * LANES), whole_seq),
                  pl.BlockSpec((1, s, n_heads * MLA_V), whole_seq)],
        out_specs=pl.BlockSpec((1, step_rows, n_heads * MLA_V), tile),
        out_shape=jax.ShapeDtypeStruct((b, s, MLA_HEADS * MLA_V), BF16),
        scratch_shapes=[pltpu.VMEM((n_heads, MLA_V + ONES_ROWS, s), BF16),
                        pltpu.VMEM((n_chains, LANES, MLA_TILE), BF16),
                        score_buf, score_buf, score_buf,
                        *_SoftmaxState.scratch(n_chains, MLA_V, MLA_TILE)],
        compiler_params=_params(("parallel", "parallel", "arbitrary")),
        name="mla_attention",
    )(q, k, v)


def _swa_kernel(slopes_ref, sinks_ref, q_ref, kp_ref, kc_ref, vp_ref, vc_ref, o_ref, s_ref):
    n = pl.program_id(1)
    group = SWA_Q_HEADS // SWA_KV_HEADS
    diff, in_sequence = _band_geometry(SWA_BLOCK, n)
    window = (diff >= 0) & (diff < SWA_WINDOW)
    masks = [window & in_sequence] + [window] * (BANDS_PER_STEP - 1)
    dist = diff.astype(F32) * LOG2E

    def both_halves(prev_ref, cur_ref, band, kv_head):
        x = _band_keys(prev_ref, cur_ref, band, SWA_BLOCK, slice(None))
        part = x[:, kv_head * HEAD_DIM:(kv_head + 1) * HEAD_DIM]
        return jnp.concatenate([part, part], axis=1)

    bands_kv = [(band, kv) for band in range(BANDS_PER_STEP) for kv in range(SWA_KV_HEADS)]
    k_tiles = {bk: both_halves(kp_ref, kc_ref, *bk) for bk in bands_kv}
    v_tiles = {bk: both_halves(vp_ref, vc_ref, *bk) for bk in bands_kv}

    def logits(h, band, s):
        return jnp.where(masks[band], s - slopes_ref[h] * dist, NEG_INF)

    def sink(h):
        return sinks_ref[h] * LOG2E

    def store_pair(pair, band, o, lse):
        o_ref[0, _band_rows(band, SWA_BLOCK), pair * LANES:(pair + 1) * LANES] = o.astype(o_ref.dtype)

    _band_attention(
        SWA_Q_HEADS,
        lambda h, band: q_ref[0, _band_rows(band, SWA_BLOCK), (h // 2) * LANES:(h // 2 + 1) * LANES],
        lambda h, band: k_tiles[band, h // group],
        lambda h, band: v_tiles[band, h // group],
        logits, sink, store_pair, s_ref, with_lse=False)


def _swa_attention(qkv, sinks, b, s):
    q_width = SWA_Q_HEADS * HEAD_DIM
    kv_width = SWA_KV_HEADS * HEAD_DIM
    k_tile = q_width // kv_width
    step_rows = BANDS_PER_STEP * SWA_BLOCK
    assert s % step_rows == 0
    cur = lambda t: (lambda bi, n: (bi, n, t))
    prev = lambda t: (lambda bi, n: (bi, jnp.maximum(n * BANDS_PER_STEP - 1, 0), t))
    kv_blk, kv_prev_blk = (1, step_rows, kv_width), (1, SWA_BLOCK, kv_width)
    return pl.pallas_call(
        _swa_kernel,
        grid=(b, s // step_rows),
        in_specs=[pl.BlockSpec(memory_space=pltpu.SMEM), pl.BlockSpec(memory_space=pltpu.SMEM),
                  pl.BlockSpec((1, step_rows, q_width), cur(0)),
                  pl.BlockSpec(kv_prev_blk, prev(k_tile)), pl.BlockSpec(kv_blk, cur(k_tile)),
                  pl.BlockSpec(kv_prev_blk, prev(k_tile + 1)), pl.BlockSpec(kv_blk, cur(k_tile + 1))],
        out_specs=pl.BlockSpec((1, step_rows, q_width), cur(0)),
        out_shape=jax.ShapeDtypeStruct((b, s, q_width), BF16),
        scratch_shapes=[pltpu.VMEM((BANDS_PER_STEP * SWA_Q_HEADS, 2 * SWA_BLOCK, SWA_BLOCK), F32)],
        compiler_params=_params(("parallel", "arbitrary")),
        name="swa_attention",
    )(_alibi_slopes(SWA_Q_HEADS), sinks.astype(F32), qkv, qkv, qkv, qkv, qkv)


def kernel(x, l0_attn_norm, l0_w_qkv, l0_w_o, l0_mlp_norm, l0_w_up, l0_w_down, l1_attn_norm, l1_w_qkv, l1_w_o, l1_mlp_norm, l1_w_up, l1_w_down, l2_attn_norm, l2_w_dkv, l2_q_norm, l2_w_uq, l2_kv_norm, l2_w_ukv, l2_w_o, l2_mlp_norm, l2_w_up, l2_w_down, l3_attn_norm, l3_w_qkv, l3_sinks, l3_w_o, l3_mlp_norm, l3_w_up, l3_w_down, final_norm):
    b, s, d = x.shape
    bf = lambda w: w.astype(BF16)
    h = x.reshape(b * s, d)

    qkv = _norm_proj(h, l0_attn_norm, bf(l0_w_qkv), scaled_cols=MOBA_HEADS * HEAD_DIM, scale=HEAD_Q_SCALE)
    a = _moba_attention(qkv.reshape(b, s, -1), b, s).reshape(b * s, -1)
    h = _post(h, (a,), bf(l0_w_o), l0_mlp_norm, bf(l0_w_up), bf(l0_w_down))

    qkvs = _dil_proj(h, l1_attn_norm, bf(l1_w_qkv))
    groups = [_dilated_group(qkvs[g], b, s, g, window, dil) for g, (window, dil) in enumerate(DIL_PAIRS)]
    h = _post(h, tuple(o for o, _ in groups) + tuple(l for _, l in groups),
              bf(l1_w_o), l1_mlp_norm, bf(l1_w_up), bf(l1_w_down))

    q, k, v = _mla_proj(h, l2_attn_norm, l2_w_dkv, l2_q_norm, l2_w_uq, l2_kv_norm, l2_w_ukv, s)
    a = _mla_attention(q.reshape(b, s, -1), k.reshape(b, s, -1), v.reshape(b, s, -1), b, s).reshape(b * s, -1)
    h = _post(h, (a,), bf(l2_w_o), l2_mlp_norm, bf(l2_w_up), bf(l2_w_down))

    qkv = _norm_proj(h, l3_attn_norm, bf(l3_w_qkv), col_tile=256, scaled_cols=SWA_Q_HEADS * HEAD_DIM,
                     scale=HEAD_Q_SCALE)
    a = _swa_attention(qkv.reshape(b, s, -1), l3_sinks, b, s).reshape(b * s, -1)
    h = _post(h, (a,), bf(l3_w_o), l3_mlp_norm, bf(l3_w_up), bf(l3_w_down), g_final=final_norm)
    return h.reshape(b, s, d)
```

```python
import functools

import jax
import jax.numpy as jnp
from jax import lax
from jax.experimental import pallas as pl
from jax.experimental.pallas import tpu as pltpu

F32 = jnp.float32
BF16 = jnp.bfloat16

D_MODEL = 1024
HEAD_DIM = 64
RMS_EPS = 1e-6
D_FF = 4 * D_MODEL
NEG_INF = -1e30

MOBA_HEADS = 16
MOBA_BLOCK = 256
MOBA_TOPK = 3

DIL_PAIRS = ((128, 1), (512, 4), (2048, 16))
DIL_HEADS_PER_GROUP = 8
DIL_BAND = 128

MLA_HEADS = 16
MLA_Q_RANK = 768
MLA_KV_RANK = 256
MLA_NOPE = 64
MLA_ROPE = 32
MLA_V = 64
ROPE_THETA = 10000.0
MLA_TILE = 256
MLA_Q_TILES = 2
MLA_HEADS_PER_STEP = 8
MOBA_HEADS_PER_STEP = 8
MOBA_Q_TILES = 2

SWA_Q_HEADS = 16
SWA_KV_HEADS = 2
SWA_WINDOW = 128
SWA_BLOCK = 128

LANES = 128
ROW_TILE = 512
FF_TILE = 512
VMEM_LIMIT = 56 * 1024 * 1024
LOG2E = 1.4426950408889634
HEAD_Q_SCALE = HEAD_DIM ** -0.5 * LOG2E
MLA_Q_SCALE = (MLA_NOPE + MLA_ROPE) ** -0.5 * LOG2E


def _params(semantics):
    return pltpu.CompilerParams(dimension_semantics=semantics, vmem_limit_bytes=VMEM_LIMIT)


def _resident(shape):
    return pl.BlockSpec(shape, lambda *_: (0,) * len(shape), pipeline_mode=pl.Buffered(1))


def _alibi_slopes(n_heads):
    return 2.0 ** (-8.0 * jnp.arange(1, n_heads + 1, dtype=F32) / n_heads)


def _rms(x, g):
    return x * lax.rsqrt(jnp.mean(x * x, axis=-1, keepdims=True) + RMS_EPS) * g


def _dot(a, b):
    return jnp.dot(a, b, preferred_element_type=F32)


def _dot_nt(a, b):
    return lax.dot_general(a, b, (((1,), (1,)), ((), ())), preferred_element_type=F32)


def _norm_proj_kernel(h_ref, g_ref, w_ref, o_ref, *, col_tile, scaled_cols, scale):
    n = _rms(h_ref[...], g_ref[...]).astype(BF16)
    for c in range(o_ref.shape[1] // col_tile):
        cols = slice(c * col_tile, (c + 1) * col_tile)
        y = _dot(n, w_ref[:, cols])
        if (c + 1) * col_tile <= scaled_cols:
            y = y * scale
        o_ref[:, cols] = y.astype(o_ref.dtype)


def _norm_proj(h, g, w, col_tile=512, scaled_cols=0, scale=1.0):
    m, d = h.shape
    n_out = w.shape[1]
    assert scaled_cols % col_tile == 0
    return pl.pallas_call(
        functools.partial(_norm_proj_kernel, col_tile=col_tile, scaled_cols=scaled_cols, scale=scale),
        grid=(m // ROW_TILE,),
        in_specs=[pl.BlockSpec((ROW_TILE, d), lambda i: (i, 0)),
                  _resident((1, d)),
                  _resident((d, n_out))],
        out_specs=pl.BlockSpec((ROW_TILE, n_out), lambda i: (i, 0)),
        out_shape=jax.ShapeDtypeStruct((m, n_out), BF16),
        compiler_params=_params(("parallel",)),
        name="norm_proj",
    )(h, g.reshape(1, d), w)


def _dil_proj_kernel(h_ref, g_ref, w_ref, o0_ref, o1_ref, o2_ref, stage_ref, sorted_ref):
    n = _rms(h_ref[...], g_ref[...])
    rows, d = n.shape
    width = DIL_HEADS_PER_GROUP * HEAD_DIM
    for t in range(d // LANES):
        stage_ref[t] = n[:, t * LANES:(t + 1) * LANES]
    n = n.astype(BF16)
    for g, (o_ref, (_, dil)) in enumerate(zip((o0_ref, o1_ref, o2_ref), DIL_PAIRS)):
        per = rows // dil
        for r in range(dil if dil > 1 else 0):
            for t in range(d // LANES):
                sorted_ref[g - 1, r * per:(r + 1) * per, t * LANES:(t + 1) * LANES] = (
                    stage_ref[t, pl.ds(r, per, stride=dil), :].astype(BF16))
        lhs = n if dil == 1 else sorted_ref[g - 1]
        for part in range(3):
            src = (3 * g + part) * width
            y = _dot(lhs, w_ref[:, src:src + width])
            if part == 0:
                y = y * HEAD_Q_SCALE
            for r in range(dil):
                dst = (3 * r + part) * width
                o_ref[:, dst:dst + width] = y[r * per:(r + 1) * per, :].astype(BF16)


def _dil_proj(h, g, w):
    m, d = h.shape
    feat = 3 * DIL_HEADS_PER_GROUP * HEAD_DIM
    assert DIL_PAIRS[0][1] == 1
    view = lambda dil: (m // dil, dil * feat)
    view_block = lambda dil: pl.BlockSpec((ROW_TILE // dil, dil * feat), lambda i: (i, 0))
    return pl.pallas_call(
        _dil_proj_kernel,
        grid=(m // ROW_TILE,),
        in_specs=[pl.BlockSpec((ROW_TILE, d), lambda i: (i, 0)), _resident((1, d)), _resident(w.shape)],
        out_specs=[view_block(dil) for _, dil in DIL_PAIRS],
        out_shape=[jax.ShapeDtypeStruct(view(dil), BF16) for _, dil in DIL_PAIRS],
        scratch_shapes=[pltpu.VMEM((d // LANES, ROW_TILE, LANES), F32),
                        pltpu.VMEM((len(DIL_PAIRS) - 1, ROW_TILE, d), BF16)],
        compiler_params=_params(("parallel",)),
        name="dil_proj",
    )(h, g.reshape(1, d), w)


def _mlp_tail(h1, g_ref, wup_ref, wdn_ref, gf_ref, out_ref):
    n = _rms(h1, g_ref[...]).astype(BF16)
    acc = jnp.zeros_like(h1)
    for c in range(wup_ref.shape[1] // FF_TILE):
        cols = slice(c * FF_TILE, (c + 1) * FF_TILE)
        u = jnp.square(jnp.maximum(_dot(n, wup_ref[:, cols]), 0.0)).astype(BF16)
        acc = acc + _dot(u, wdn_ref[cols, :])
    out = h1 + acc
    if gf_ref is not None:
        out = _rms(out, gf_ref[...])
    out_ref[...] = out


def _post_kernel(h_ref, a_ref, wo_ref, g_ref, wup_ref, wdn_ref, *rest):
    gf_ref, out_ref = (rest[0], rest[1]) if len(rest) == 2 else (None, rest[0])
    h1 = h_ref[...] + _dot(a_ref[...], wo_ref[...])
    _mlp_tail(h1, g_ref, wup_ref, wdn_ref, gf_ref, out_ref)


def _post_merge_kernel(h_ref, o0_ref, o1_ref, o2_ref, l0_ref, l1_ref, l2_ref,
                       wo_ref, g_ref, wup_ref, wdn_ref, out_ref, stage_ref):
    rows, width = o0_ref.shape

    def token_rows(x_ref, slot, dil):
        if dil == 1:
            return x_ref[...]
        for r in range(dil):
            for t in range(width // LANES):
                src = r * width + t * LANES
                stage_ref[slot, t, pl.ds(r, rows // dil, stride=dil), :] = x_ref[:, src:src + LANES]
        return jnp.concatenate([stage_ref[slot, t] for t in range(width // LANES)], axis=1)

    dils = [dil for _, dil in DIL_PAIRS]
    l0, l1, l2 = (token_rows(ref, slot, dil) for slot, (ref, dil) in enumerate(zip((l0_ref, l1_ref, l2_ref), dils)))
    o0, o1, o2 = (token_rows(ref, 3 + slot, dil) for slot, (ref, dil) in enumerate(zip((o0_ref, o1_ref, o2_ref), dils)))
    mx = jnp.maximum(jnp.maximum(l0, l1), l2)
    e0, e1, e2 = jnp.exp(l0 - mx), jnp.exp(l1 - mx), jnp.exp(l2 - mx)
    merged = (e0 * o0 + e1 * o1 + e2 * o2) / (e0 + e1 + e2)
    h1 = h_ref[...] + _dot(merged.astype(BF16), wo_ref[...])
    _mlp_tail(h1, g_ref, wup_ref, wdn_ref, None, out_ref)


def _post(h, attn_ins, w_o, g, w_up, w_down, g_final=None):
    m, d = h.shape
    row = lambda width: pl.BlockSpec((ROW_TILE, width), lambda i: (i, 0))
    merge = len(attn_ins) > 1
    in_specs = [row(d)] + [pl.BlockSpec((ROW_TILE * a.shape[0] // m, a.shape[1]), lambda i: (i, 0)) for a in attn_ins]
    in_specs += [_resident(w_o.shape), _resident((1, d)), _resident(w_up.shape), _resident(w_down.shape)]
    args = [h, *attn_ins, w_o, g.reshape(1, d), w_up, w_down]
    if g_final is not None:
        in_specs.append(_resident((1, d)))
        args.append(g_final.reshape(1, d))
    scratch = [pltpu.VMEM((len(attn_ins), w_o.shape[0] // LANES, ROW_TILE, LANES), F32)] if merge else []
    return pl.pallas_call(
        _post_merge_kernel if merge else _post_kernel,
        grid=(m // ROW_TILE,),
        in_specs=in_specs,
        out_specs=row(d),
        out_shape=jax.ShapeDtypeStruct((m, d), F32),
        scratch_shapes=scratch,
        compiler_params=_params(("parallel",)),
        name="post_mlp",
    )(*args)


ONES_ROWS = 16
SKIP_SHIFT = -2.0 * NEG_INF


class _SoftmaxState:
    def __init__(self, m_ref, acc_ref, dv):
        self.m_ref, self.acc_ref, self.dv = m_ref, acc_ref, dv

    @staticmethod
    def scratch(n_heads, dv, tq):
        return [pltpu.VMEM((n_heads, 1, tq), F32), pltpu.VMEM((n_heads, dv + ONES_ROWS, tq), F32)]

    @staticmethod
    def store_values_t(vt_ref, hh, cols, v_t):
        dv = v_t.shape[0]
        vt_ref[hh, :dv, cols] = v_t.astype(BF16)
        vt_ref[hh, dv:, cols] = jnp.ones((ONES_ROWS, v_t.shape[1]), BF16)

    def init(self):
        self.m_ref[...] = jnp.full(self.m_ref.shape, NEG_INF, F32)
        self.acc_ref[...] = jnp.zeros(self.acc_ref.shape, F32)

    def step(self, hh, x, v_t, shift=None):
        m = self.m_ref[hh]
        m_cur = jnp.max(x, axis=0, keepdims=True)
        if shift is not None:
            m_cur = m_cur - shift
        m_new = jnp.maximum(m, m_cur)
        alpha = jnp.exp2(m - m_new)
        p = jnp.exp2(x - (m_new if shift is None else m_new + shift))
        self.m_ref[hh] = m_new
        self.acc_ref[hh] = alpha * self.acc_ref[hh] + _dot(v_t, p.astype(BF16))

    def normalized(self, hh):
        return self.acc_ref[hh, :self.dv, :] / self.acc_ref[hh, self.dv:self.dv + 1, :]


def _sweep_blocks(n_past, n_chains, state, score_fn, values_fn, past_logits, own_steps, s_a, s_b):
    def produce(buf, block, chains=range(n_chains)):
        for c in chains:
            buf[c] = score_fn(c, block)

    def consume(buf, block):
        for c in range(n_chains):
            x, shift = past_logits(c, buf[c], block)
            state.step(c, x, values_fn(c, block), shift)

    produced = set()
    for buf, block, chains, _ in own_steps:
        produce(buf, block, [c for c in chains if (id(buf), c) not in produced])
        produced.update((id(buf), c) for c in chains)
    for c in range(n_chains):
        s_a[c] = score_fn(c, 0)
        for buf, block, chains, logits_fn in own_steps:
            if c in chains:
                x, shift = logits_fn(c, buf[c])
                state.step(c, x, values_fn(c, block), shift)

    def advance(src, block, dst, next_block):
        for c in range(n_chains):
            dst[c] = score_fn(c, next_block)
            x, shift = past_logits(c, src[c], block)
            state.step(c, x, values_fn(c, block), shift)

    def two_blocks(first):
        advance(s_a, first, s_b, first + 1)
        advance(s_b, first + 1, s_a, jnp.minimum(first + 2, n_past - 1))

    def quad(t, carry):
        two_blocks(4 * t)
        two_blocks(4 * t + 2)
        return carry

    def pair(t, carry):
        two_blocks(4 * (n_past // 4) + 2 * t)
        return carry

    lax.fori_loop(0, n_past // 4, quad, 0)
    lax.fori_loop(0, (n_past % 4) // 2, pair, 0)

    @pl.when(n_past % 2 == 1)
    def _():
        consume(s_a, n_past - 1)


def _transpose_bf16(x):
    return x.astype(F32).T.astype(BF16)


def _keep_head_rows(x_t, half):
    r = lax.broadcasted_iota(jnp.int32, x_t.shape, 0)
    keep = (r < HEAD_DIM) if half == 0 else (r >= HEAD_DIM)
    return jnp.where(keep, x_t, jnp.zeros_like(x_t))


def _bf16_pieces(x, n):
    pieces = []
    for _ in range(n):
        pieces.append(x.astype(BF16))
        x = x - pieces[-1].astype(F32)
    return pieces


F32_PIECES = 3


def _moba_kernel(slopes_ref, q_ref, k_ref, v_ref, o_ref, kmean_ref, vt_ref, term_ref, kpos_ref, qh_ref,
                 s_a, s_b, s_c, m_ref, acc_ref, *, n_blocks):
    blk = MOBA_BLOCK
    n_heads = MOBA_HEADS_PER_STEP
    group = pl.program_id(1)
    i = pl.program_id(2)
    state = _SoftmaxState(m_ref, acc_ref, HEAD_DIM)
    chain = lambda tile, hh: tile * n_heads + hh
    tile_chains = [[chain(tile, hh) for hh in range(n_heads)] for tile in range(MOBA_Q_TILES)]

    @pl.when(i == 0)
    def _():
        means = []
        pos = lax.broadcasted_iota(jnp.int32, (blk, LANES), 0)
        feature = lax.broadcasted_iota(jnp.int32, (blk, LANES), 1)
        key_pos = jnp.where(feature < F32_PIECES, pos, 0).astype(F32).astype(BF16)
        for j in range(n_blocks):
            rows = slice(j * blk, (j + 1) * blk)
            means.append(jnp.mean(k_ref[0, rows, :].astype(F32), axis=0, keepdims=True))
            for tile in range(n_heads // 2):
                kpos_ref[tile, rows, :LANES] = k_ref[0, rows, tile * LANES:(tile + 1) * LANES]
                kpos_ref[tile, rows, LANES:] = key_pos
            v_t = v_ref[0, rows, :].astype(F32).T
            for hh in range(n_heads):
                state.store_values_t(vt_ref, hh, rows, v_t[hh * HEAD_DIM:(hh + 1) * HEAD_DIM, :])
        for piece, part in enumerate(_bf16_pieces(jnp.concatenate(means, axis=0), F32_PIECES)):
            kmean_ref[piece * n_blocks:(piece + 1) * n_blocks, :] = part

    state.init()
    key = lax.broadcasted_iota(jnp.int32, (blk, blk), 0)
    qry = lax.broadcasted_iota(jnp.int32, (blk, blk), 1)
    blk_idx = lax.broadcasted_iota(jnp.int32, (n_blocks, blk), 0)
    feature_row = lax.broadcasted_iota(jnp.int32, (LANES, blk), 0)
    lane_tile = lambda hh: slice((hh // 2) * LANES, (hh // 2 + 1) * LANES)
    first_block = MOBA_Q_TILES * i

    for tile in range(MOBA_Q_TILES):
        own = first_block + tile
        q_t = q_ref[0, tile * blk:(tile + 1) * blk, :].astype(F32).T
        fully_past = blk_idx < own
        block_dist = ((own - blk_idx) * blk).astype(F32)
        for hh in range(n_heads):
            slope_s = slopes_ref[n_heads * group + hh] * LOG2E
            q_h = _keep_head_rows(q_t[lane_tile(hh), :], hh % 2).astype(BF16)
            pieces = _dot(kmean_ref[:, lane_tile(hh)], q_h)
            gate = pieces[:n_blocks] + pieces[n_blocks:2 * n_blocks] + pieces[2 * n_blocks:]
            work = jnp.where(fully_past, gate, NEG_INF * HEAD_Q_SCALE)
            chosen = jnp.zeros((n_blocks, blk), jnp.bool_)
            for _ in range(min(MOBA_TOPK, n_blocks)):
                best = jnp.max(work, axis=0, keepdims=True)
                first = jnp.min(jnp.where(work == best, blk_idx, n_blocks), axis=0, keepdims=True)
                pick = blk_idx == first
                chosen = jnp.logical_or(chosen, pick)
                work = jnp.where(pick, -jnp.inf, work)
            term_ref[chain(tile, hh)] = jnp.where(jnp.logical_and(chosen, fully_past), slope_s * block_dist,
                                                  SKIP_SHIFT)
            qh_ref[chain(tile, hh), :LANES, :] = q_h
            slope_rows = jnp.zeros((LANES, blk), F32)
            for piece, part in enumerate(_bf16_pieces(jnp.full((1, blk), slope_s, F32), F32_PIECES)):
                slope_rows = jnp.where(feature_row == piece, part.astype(F32), slope_rows)
            qh_ref[chain(tile, hh), LANES:, :] = slope_rows.astype(BF16)

    def rows_of(block):
        return pl.ds(pl.multiple_of(block * blk, blk), blk)

    def scores(c, block):
        return _dot(kpos_ref[(c % n_heads) // 2, rows_of(block), :], qh_ref[c])

    def v_t(c, block):
        return vt_ref[c % n_heads, :, rows_of(block)]

    def past_logits(c, s, block):
        return s, term_ref[c, pl.ds(block, 1), :]

    def own_logits(c, s):
        return jnp.where(key <= qry, s, NEG_INF), None

    assert MOBA_Q_TILES == 2
    own_steps = [(s_b, first_block, tile_chains[0], own_logits),
                 (s_b, first_block, tile_chains[1], functools.partial(past_logits, block=first_block)),
                 (s_c, first_block + 1, tile_chains[1], own_logits)]
    _sweep_blocks(first_block, MOBA_Q_TILES * n_heads, state, scores, v_t, past_logits, own_steps, s_a, s_b)
    for tile in range(MOBA_Q_TILES):
        o_t = jnp.concatenate([state.normalized(c) for c in tile_chains[tile]], axis=0)
        o_ref[0, tile * blk:(tile + 1) * blk, :] = o_t.T.astype(o_ref.dtype)


def _moba_attention(qkv, b, s):
    n_blocks = s // MOBA_BLOCK
    n_heads = MOBA_HEADS_PER_STEP
    width = n_heads * HEAD_DIM
    n_groups = MOBA_HEADS // n_heads
    n_chains = MOBA_Q_TILES * n_heads
    step_rows = MOBA_Q_TILES * MOBA_BLOCK
    assert s % step_rows == 0
    score_buf = pltpu.VMEM((n_chains, MOBA_BLOCK, MOBA_BLOCK), F32)
    return pl.pallas_call(
        functools.partial(_moba_kernel, n_blocks=n_blocks),
        grid=(b, n_groups, s // step_rows),
        in_specs=[pl.BlockSpec(memory_space=pltpu.SMEM),
                  pl.BlockSpec((1, step_rows, width), lambda bi, g, i: (bi, i, g)),
                  pl.BlockSpec((1, s, width), lambda bi, g, i: (bi, 0, n_groups + g)),
                  pl.BlockSpec((1, s, width), lambda bi, g, i: (bi, 0, 2 * n_groups + g))],
        out_specs=pl.BlockSpec((1, step_rows, width), lambda bi, g, i: (bi, i, g)),
        out_shape=jax.ShapeDtypeStruct((b, s, MOBA_HEADS * HEAD_DIM), BF16),
        scratch_shapes=[pltpu.VMEM((3 * n_blocks, width), BF16),
                        pltpu.VMEM((n_heads, HEAD_DIM + ONES_ROWS, s), BF16),
                        pltpu.VMEM((n_chains, n_blocks, MOBA_BLOCK), F32),
                        pltpu.VMEM((n_heads // 2, s, 2 * LANES), BF16),
                        pltpu.VMEM((n_chains, 2 * LANES, MOBA_BLOCK), BF16),
                        score_buf, score_buf, score_buf,
                        *_SoftmaxState.scratch(n_chains, HEAD_DIM, MOBA_BLOCK)],
        compiler_params=_params(("parallel", "parallel", "arbitrary")),
        name="moba_attention",
    )(_alibi_slopes(MOBA_HEADS), qkv, qkv, qkv)


def _lane_half(x, half):
    lane = lax.broadcasted_iota(jnp.int32, x.shape, x.ndim - 1)
    keep = (lane < HEAD_DIM) if half == 0 else (lane >= HEAD_DIM)
    return jnp.where(keep, x, jnp.zeros_like(x))


BANDS_PER_STEP = 2


def _band_attention(n_heads, q_tile, k_tile, v_tile, logits_fn, floor_fn, store_pair, s_ref, with_lse):
    chains = [(band, h) for band in range(BANDS_PER_STEP) for h in range(n_heads)]

    def produce(c):
        band, h = chains[c]
        s_ref[c] = _dot_nt(k_tile(h, band), _lane_half(q_tile(h, band), h % 2))

    lookahead = 2
    for c in range(min(lookahead, len(chains))):
        produce(c)
    even = None
    for c, (band, h) in enumerate(chains):
        if c + lookahead < len(chains):
            produce(c + lookahead)
        x = logits_fn(h, band, s_ref[c])
        m = jnp.max(x, axis=0, keepdims=True)
        floor = floor_fn(h)
        if floor is not None:
            m = jnp.maximum(m, floor)
        p = jnp.exp2(x - m)
        den = jnp.sum(p, axis=0, keepdims=True)
        if floor is not None:
            den = den + jnp.exp2(floor - m)
        o_t = lax.dot_general(v_tile(h, band), p.astype(BF16), (((0,), (0,)), ((), ())),
                              preferred_element_type=F32)
        half = slice((h % 2) * HEAD_DIM, (h % 2 + 1) * HEAD_DIM)
        o_t = o_t[half, :] / den
        lse_t = jnp.broadcast_to((m + jnp.log2(den)) * (1.0 / LOG2E), o_t.shape) if with_lse else None
        if h % 2 == 0:
            even = (o_t, lse_t)
            continue
        store_pair(h // 2, band, jnp.concatenate([even[0], o_t], axis=0).T,
                   jnp.concatenate([even[1], lse_t], axis=0).T if with_lse else None)


def _band_rows(band, size):
    return slice(band * size, (band + 1) * size)


def _band_keys(prev_ref, cur_ref, band, size, cols):
    if band == 0:
        return jnp.concatenate([prev_ref[0, :, cols], cur_ref[0, :size, cols]], axis=0)
    return cur_ref[0, (band - 1) * size:(band + 1) * size, cols]


def _band_geometry(size, n):
    key = lax.broadcasted_iota(jnp.int32, (2 * size, size), 0)
    qry = lax.broadcasted_iota(jnp.int32, (2 * size, size), 1)
    diff = qry + size - key
    in_sequence = (n > 0) | (key >= size)
    return diff, in_sequence


def _carry_previous_block(n, cur_refs, prev_refs, size):
    @pl.when(n == 0)
    def _():
        for prev in prev_refs:
            prev[...] = jnp.zeros(prev.shape, prev.dtype)

    def save():
        for cur, prev in zip(cur_refs, prev_refs):
            prev[0] = cur[0, cur.shape[1] - size:, :]
    return save


def _dil_kernel(slopes_ref, q_ref, kc_ref, vc_ref, o_ref, lse_ref, s_ref, kp_ref, vp_ref, *, n_pts, dil, group):
    n = pl.program_id(2)
    save_last_block = _carry_previous_block(n, (kc_ref, vc_ref), (kp_ref, vp_ref), DIL_BAND)
    diff, in_sequence = _band_geometry(DIL_BAND, n)
    window = (diff >= 0) & (diff <= n_pts)
    masks = [window & in_sequence] + [window] * (BANDS_PER_STEP - 1)
    dist = (dil * diff).astype(F32) * LOG2E
    tile = lambda h: slice((h // 2) * LANES, (h // 2 + 1) * LANES)

    def logits(h, band, s):
        slope = slopes_ref[group * DIL_HEADS_PER_GROUP + h]
        return jnp.where(masks[band], s - slope * dist, NEG_INF)

    def store_pair(pair, band, o, lse):
        rows, cols = _band_rows(band, DIL_BAND), slice(pair * LANES, (pair + 1) * LANES)
        o_ref[0, rows, cols] = o
        lse_ref[0, rows, cols] = lse

    _band_attention(
        DIL_HEADS_PER_GROUP,
        lambda h, band: q_ref[0, _band_rows(band, DIL_BAND), tile(h)],
        lambda h, band: _band_keys(kp_ref, kc_ref, band, DIL_BAND, tile(h)),
        lambda h, band: _band_keys(vp_ref, vc_ref, band, DIL_BAND, tile(h)),
        logits, lambda h: None, store_pair, s_ref, with_lse=True)
    save_last_block()


def _dilated_group(qkv, b, s, group, window, dil):
    width = DIL_HEADS_PER_GROUP * HEAD_DIM
    n_sub = s // dil
    step_rows = BANDS_PER_STEP * DIL_BAND
    assert n_sub % step_rows == 0
    view = qkv.reshape(b, n_sub, dil * 3 * width)

    def col(part):
        return lambda bi, r, n: (bi, n, r * 3 + part)

    blk = (1, step_rows, width)
    previous_block = pltpu.VMEM((1, DIL_BAND, width), BF16)
    out_spec = pl.BlockSpec(blk, lambda bi, r, n: (bi, n, r))
    out_sds = jax.ShapeDtypeStruct((b, n_sub, dil * width), F32)
    o, lse = pl.pallas_call(
        functools.partial(_dil_kernel, n_pts=window // dil, dil=dil, group=group),
        grid=(b, dil, n_sub // step_rows),
        in_specs=[pl.BlockSpec(memory_space=pltpu.SMEM),
                  pl.BlockSpec(blk, col(0)), pl.BlockSpec(blk, col(1)), pl.BlockSpec(blk, col(2))],
        out_specs=[out_spec, out_spec],
        out_shape=[out_sds, out_sds],
        scratch_shapes=[pltpu.VMEM((BANDS_PER_STEP * DIL_HEADS_PER_GROUP, 2 * DIL_BAND, DIL_BAND), F32),
                        previous_block, previous_block],
        compiler_params=_params(("parallel", "parallel", "arbitrary")),
        name=f"dilated_attention_g{group}",
    )(_alibi_slopes(len(DIL_PAIRS) * DIL_HEADS_PER_GROUP), view, view, view)
    return o.reshape(b * n_sub, dil * width), lse.reshape(b * n_sub, dil * width)


def _mla_proj_kernel(h_ref, g_ref, wd_ref, gq_ref, gkv_ref, wuq_ref, wuk_ref, wuv_ref,
                     cos_ref, sin_up_ref, sin_dn_ref, q_ref, k_ref, v_ref):
    n = _rms(h_ref[...], g_ref[...]).astype(BF16)
    c = _dot(n, wd_ref[...])
    nq = _rms(c[:, :MLA_Q_RANK], gq_ref[...]).astype(BF16)
    nkv = _rms(c[:, MLA_Q_RANK:MLA_Q_RANK + MLA_KV_RANK], gkv_ref[...]).astype(BF16)
    k_rope = c[:, MLA_Q_RANK + MLA_KV_RANK:]
    half = MLA_ROPE // 2
    pair = 2

    def rope(x, width):
        cos, sin_up, sin_dn = (jnp.tile(t[...], (1, width)) for t in (cos_ref, sin_up_ref, sin_dn_ref))
        return x * cos + pltpu.roll(x, half, 1) * sin_up + pltpu.roll(x, x.shape[1] - half, 1) * sin_dn

    kr = jnp.tile(rope(k_rope, 1), (1, pair))
    for h in range(0, MLA_HEADS, pair):
        cols = slice(h * LANES, (h + pair) * LANES)
        q_ref[:, cols] = (rope(_dot(nq, wuq_ref[:, cols]), pair) * MLA_Q_SCALE).astype(BF16)
        k_ref[:, cols] = (_dot(nkv, wuk_ref[:, cols]) + kr).astype(BF16)
    v_ref[...] = _dot(nkv, wuv_ref[...]).astype(BF16)


def _mla_proj(h, g, w_dkv, q_norm, w_uq, kv_norm, w_ukv, s):
    m, d = h.shape
    qk = MLA_NOPE + MLA_ROPE
    wd = jnp.concatenate([w_dkv[:, :MLA_Q_RANK + MLA_KV_RANK],
                          jnp.zeros((d, MLA_NOPE), F32), w_dkv[:, MLA_Q_RANK + MLA_KV_RANK:],
                          jnp.zeros((d, LANES - qk), F32)], axis=1).astype(BF16)
    wuq = jnp.pad(w_uq.reshape(MLA_Q_RANK, MLA_HEADS, qk), ((0, 0), (0, 0), (0, LANES - qk)))
    wuq = wuq.reshape(MLA_Q_RANK, MLA_HEADS * LANES).astype(BF16)
    w_ukv = w_ukv.reshape(MLA_KV_RANK, MLA_HEADS, MLA_NOPE + MLA_V)
    wuk = jnp.pad(w_ukv[:, :, :MLA_NOPE], ((0, 0), (0, 0), (0, LANES - MLA_NOPE)))
    wuk = wuk.reshape(MLA_KV_RANK, MLA_HEADS * LANES).astype(BF16)
    wuv = w_ukv[:, :, MLA_NOPE:].reshape(MLA_KV_RANK, MLA_HEADS * MLA_V).astype(BF16)
    half = MLA_ROPE // 2
    inv = ROPE_THETA ** (-jnp.arange(0, MLA_ROPE, 2, dtype=F32) / MLA_ROPE)
    ang = jnp.arange(s).astype(F32)[:, None] * inv[None, :]
    cos, sin = jnp.cos(ang), jnp.sin(ang)
    zeros = lambda w: jnp.zeros((s, w), F32)
    cos_t = jnp.concatenate([jnp.ones((s, MLA_NOPE), F32), cos, cos, zeros(LANES - qk)], axis=1)
    sin_up = jnp.concatenate([zeros(MLA_NOPE + half), sin, zeros(LANES - qk)], axis=1)
    sin_dn = jnp.concatenate([zeros(MLA_NOPE), -sin, zeros(half + LANES - qk)], axis=1)

    tiles_per_seq = s // ROW_TILE
    row = lambda width: pl.BlockSpec((ROW_TILE, width), lambda i: (i, 0))
    table = pl.BlockSpec((ROW_TILE, LANES), lambda i: (i % tiles_per_seq, 0))
    return pl.pallas_call(
        _mla_proj_kernel,
        grid=(m // ROW_TILE,),
        in_specs=[row(d), _resident((1, d)), _resident(wd.shape),
                  _resident((1, MLA_Q_RANK)), _resident((1, MLA_KV_RANK)),
                  _resident(wuq.shape), _resident(wuk.shape), _resident(wuv.shape),
                  table, table, table],
        out_specs=[row(MLA_HEADS * LANES), row(MLA_HEADS * LANES), row(MLA_HEADS * MLA_V)],
        out_shape=[jax.ShapeDtypeStruct((m, MLA_HEADS * LANES), BF16),
                   jax.ShapeDtypeStruct((m, MLA_HEADS * LANES), BF16),
                   jax.ShapeDtypeStruct((m, MLA_HEADS * MLA_V), BF16)],
        compiler_params=_params(("parallel",)),
        name="mla_proj",
    )(h, g.reshape(1, d), wd, q_norm.reshape(1, -1), kv_norm.reshape(1, -1), wuq, wuk, wuv,
      cos_t, sin_up, sin_dn)


def _mla_kernel(q_ref, k_ref, v_ref, o_ref, vt_ref, qh_ref, s_a, s_b, s_c, m_ref, acc_ref, *, n_tiles):
    t = MLA_TILE
    n_heads = MLA_HEADS_PER_STEP
    i = pl.program_id(2)
    state = _SoftmaxState(m_ref, acc_ref, MLA_V)
    chain = lambda tile, hh: tile * n_heads + hh
    tile_chains = [[chain(tile, hh) for hh in range(n_heads)] for tile in range(MLA_Q_TILES)]

    @pl.when(i == 0)
    def _():
        for j in range(n_tiles):
            rows = slice(j * t, (j + 1) * t)
            v_t = v_ref[0, rows, :].astype(F32).T
            for hh in range(n_heads):
                state.store_values_t(vt_ref, hh, rows, v_t[hh * MLA_V:(hh + 1) * MLA_V, :])

    state.init()
    key = lax.broadcasted_iota(jnp.int32, (t, t), 0)
    qry = lax.broadcasted_iota(jnp.int32, (t, t), 1)
    lane_tile = lambda hh: slice(hh * LANES, (hh + 1) * LANES)
    for tile in range(MLA_Q_TILES):
        for hh in range(n_heads):
            qh_ref[chain(tile, hh)] = _transpose_bf16(q_ref[0, tile * t:(tile + 1) * t, lane_tile(hh)])

    def rows_of(block):
        return pl.ds(pl.multiple_of(block * t, t), t)

    def scores(c, block):
        return _dot(k_ref[0, rows_of(block), lane_tile(c % n_heads)], qh_ref[c])

    def v_t(c, block):
        return vt_ref[c % n_heads, :, rows_of(block)]

    def past_logits(c, s, block):
        return s, None

    def causal(c, s):
        return jnp.where(key <= qry, s, NEG_INF), None

    def full(c, s):
        return s, None

    assert MLA_Q_TILES == 2
    first = MLA_Q_TILES * i
    own_steps = [(s_b, first, tile_chains[0], causal), (s_b, first, tile_chains[1], full),
                 (s_c, first + 1, tile_chains[1], causal)]
    _sweep_blocks(first, MLA_Q_TILES * n_heads, state, scores, v_t, past_logits, own_steps, s_a, s_b)
    for tile in range(MLA_Q_TILES):
        o_t = jnp.concatenate([state.normalized(c) for c in tile_chains[tile]], axis=0)
        o_ref[0, tile * t:(tile + 1) * t, :] = o_t.T.astype(o_ref.dtype)


def _mla_attention(q, k, v, b, s):
    n_heads = MLA_HEADS_PER_STEP
    n_groups = MLA_HEADS // n_heads
    whole_seq = lambda bi, g, i: (bi, 0, g)
    tile = lambda bi, g, i: (bi, i, g)
    n_chains = MLA_Q_TILES * n_heads
    step_rows = MLA_Q_TILES * MLA_TILE
    assert s % step_rows == 0
    score_buf = pltpu.VMEM((n_chains, MLA_TILE, MLA_TILE), F32)
    return pl.pallas_call(
        functools.partial(_mla_kernel, n_tiles=s // MLA_TILE),
        grid=(b, n_groups, s // step_rows),
        in_specs=[pl.BlockSpec((1, step_rows, n_heads * LANES), tile),
                  pl.BlockSpec((1, s, n_heads * LANES), whole_seq),
                  pl.BlockSpec((1, s, n_heads * MLA_V), whole_seq)],
        out_specs=pl.BlockSpec((1, step_rows, n_heads * MLA_V), tile),
        out_shape=jax.ShapeDtypeStruct((b, s, MLA_HEADS * MLA_V), BF16),
        scratch_shapes=[pltpu.VMEM((n_heads, MLA_V + ONES_ROWS, s), BF16),
                        pltpu.VMEM((n_chains, LANES, MLA_TILE), BF16),
                        score_buf, score_buf, score_buf,
                        *_SoftmaxState.scratch(n_chains, MLA_V, MLA_TILE)],
        compiler_params=_params(("parallel", "parallel", "arbitrary")),
        name="mla_attention",
    )(q, k, v)


def _swa_kernel(slopes_ref, sinks_ref, q_ref, kc_ref, vc_ref, o_ref, s_ref, kp_ref, vp_ref):
    n = pl.program_id(1)
    save_last_block = _carry_previous_block(n, (kc_ref, vc_ref), (kp_ref, vp_ref), SWA_BLOCK)
    group = SWA_Q_HEADS // SWA_KV_HEADS
    diff, in_sequence = _band_geometry(SWA_BLOCK, n)
    window = (diff >= 0) & (diff < SWA_WINDOW)
    masks = [window & in_sequence] + [window] * (BANDS_PER_STEP - 1)
    dist = diff.astype(F32) * LOG2E

    def both_halves(prev_ref, cur_ref, band, kv_head):
        x = _band_keys(prev_ref, cur_ref, band, SWA_BLOCK, slice(None))
        part = x[:, kv_head * HEAD_DIM:(kv_head + 1) * HEAD_DIM]
        return jnp.concatenate([part, part], axis=1)

    bands_kv = [(band, kv) for band in range(BANDS_PER_STEP) for kv in range(SWA_KV_HEADS)]
    k_tiles = {bk: both_halves(kp_ref, kc_ref, *bk) for bk in bands_kv}
    v_tiles = {bk: both_halves(vp_ref, vc_ref, *bk) for bk in bands_kv}

    def logits(h, band, s):
        return jnp.where(masks[band], s - slopes_ref[h] * dist, NEG_INF)

    def sink(h):
        return sinks_ref[h] * LOG2E

    def store_pair(pair, band, o, lse):
        o_ref[0, _band_rows(band, SWA_BLOCK), pair * LANES:(pair + 1) * LANES] = o.astype(o_ref.dtype)

    _band_attention(
        SWA_Q_HEADS,
        lambda h, band: q_ref[0, _band_rows(band, SWA_BLOCK), (h // 2) * LANES:(h // 2 + 1) * LANES],
        lambda h, band: k_tiles[band, h // group],
        lambda h, band: v_tiles[band, h // group],
        logits, sink, store_pair, s_ref, with_lse=False)
    save_last_block()


def _swa_attention(qkv, sinks, b, s):
    q_width = SWA_Q_HEADS * HEAD_DIM
    kv_width = SWA_KV_HEADS * HEAD_DIM
    k_tile = q_width // kv_width
    step_rows = BANDS_PER_STEP * SWA_BLOCK
    assert s % step_rows == 0
    cur = lambda t: (lambda bi, n: (bi, n, t))
    kv_blk = (1, step_rows, kv_width)
    previous_block = pltpu.VMEM((1, SWA_BLOCK, kv_width), BF16)
    return pl.pallas_call(
        _swa_kernel,
        grid=(b, s // step_rows),
        in_specs=[pl.BlockSpec(memory_space=pltpu.SMEM), pl.BlockSpec(memory_space=pltpu.SMEM),
                  pl.BlockSpec((1, step_rows, q_width), cur(0)),
                  pl.BlockSpec(kv_blk, cur(k_tile)), pl.BlockSpec(kv_blk, cur(k_tile + 1))],
        out_specs=pl.BlockSpec((1, step_rows, q_width), cur(0)),
        out_shape=jax.ShapeDtypeStruct((b, s, q_width), BF16),
        scratch_shapes=[pltpu.VMEM((BANDS_PER_STEP * SWA_Q_HEADS, 2 * SWA_BLOCK, SWA_BLOCK), F32),
                        previous_block, previous_block],
        compiler_params=_params(("parallel", "arbitrary")),
        name="swa_attention",
    )(_alibi_slopes(SWA_Q_HEADS), sinks.astype(F32), qkv, qkv, qkv)


def kernel(x, l0_attn_norm, l0_w_qkv, l0_w_o, l0_mlp_norm, l0_w_up, l0_w_down, l1_attn_norm, l1_w_qkv, l1_w_o, l1_mlp_norm, l1_w_up, l1_w_down, l2_attn_norm, l2_w_dkv, l2_q_norm, l2_w_uq, l2_kv_norm, l2_w_ukv, l2_w_o, l2_mlp_norm, l2_w_up, l2_w_down, l3_attn_norm, l3_w_qkv, l3_sinks, l3_w_o, l3_mlp_norm, l3_w_up, l3_w_down, final_norm):
    b, s, d = x.shape
    bf = lambda w: w.astype(BF16)
    h = x.reshape(b * s, d)

    qkv = _norm_proj(h, l0_attn_norm, bf(l0_w_qkv), scaled_cols=MOBA_HEADS * HEAD_DIM, scale=HEAD_Q_SCALE)
    a = _moba_attention(qkv.reshape(b, s, -1), b, s).reshape(b * s, -1)
    h = _post(h, (a,), bf(l0_w_o), l0_mlp_norm, bf(l0_w_up), bf(l0_w_down))

    qkvs = _dil_proj(h, l1_attn_norm, bf(l1_w_qkv))
    groups = [_dilated_group(qkvs[g], b, s, g, window, dil) for g, (window, dil) in enumerate(DIL_PAIRS)]
    h = _post(h, tuple(o for o, _ in groups) + tuple(l for _, l in groups),
              bf(l1_w_o), l1_mlp_norm, bf(l1_w_up), bf(l1_w_down))

    q, k, v = _mla_proj(h, l2_attn_norm, l2_w_dkv, l2_q_norm, l2_w_uq, l2_kv_norm, l2_w_ukv, s)
    a = _mla_attention(q.reshape(b, s, -1), k.reshape(b, s, -1), v.reshape(b, s, -1), b, s).reshape(b * s, -1)
    h = _post(h, (a,), bf(l2_w_o), l2_mlp_norm, bf(l2_w_up), bf(l2_w_down))

    qkv = _norm_proj(h, l3_attn_norm, bf(l3_w_qkv), col_tile=256, scaled_cols=SWA_Q_HEADS * HEAD_DIM,
                     scale=HEAD_Q_SCALE)
    a = _swa_attention(qkv.reshape(b, s, -1), l3_sinks, b, s).reshape(b * s, -1)
    h = _post(h, (a,), bf(l3_w_o), l3_mlp_norm, bf(l3_w_up), bf(l3_w_down), g_final=final_norm)
    return h.reshape(b, s, d)
```

```python
import functools

import jax
import jax.numpy as jnp
from jax import lax
from jax.experimental import pallas as pl
from jax.experimental.pallas import tpu as pltpu

F32 = jnp.float32
BF16 = jnp.bfloat16

D_MODEL = 1024
HEAD_DIM = 64
RMS_EPS = 1e-6
D_FF = 4 * D_MODEL
NEG_INF = -1e30

MOBA_HEADS = 16
MOBA_BLOCK = 256
MOBA_TOPK = 3

DIL_PAIRS = ((128, 1), (512, 4), (2048, 16))
DIL_HEADS_PER_GROUP = 8
DIL_BAND = 128

MLA_HEADS = 16
MLA_Q_RANK = 768
MLA_KV_RANK = 256
MLA_NOPE = 64
MLA_ROPE = 32
MLA_V = 64
ROPE_THETA = 10000.0
MLA_TILE = 256
MLA_Q_TILES = 2
MLA_HEADS_PER_STEP = 8
MOBA_HEADS_PER_STEP = 8
MOBA_Q_TILES = 2

SWA_Q_HEADS = 16
SWA_KV_HEADS = 2
SWA_WINDOW = 128
SWA_BLOCK = 128

LANES = 128
ROW_TILE = 512
FF_TILE = 512
VMEM_LIMIT = 56 * 1024 * 1024
LOG2E = 1.4426950408889634
HEAD_Q_SCALE = HEAD_DIM ** -0.5 * LOG2E
MLA_Q_SCALE = (MLA_NOPE + MLA_ROPE) ** -0.5 * LOG2E


def _params(semantics):
    return pltpu.CompilerParams(dimension_semantics=semantics, vmem_limit_bytes=VMEM_LIMIT)


def _resident(shape):
    return pl.BlockSpec(shape, lambda *_: (0,) * len(shape), pipeline_mode=pl.Buffered(1))


def _alibi_slopes(n_heads):
    return 2.0 ** (-8.0 * jnp.arange(1, n_heads + 1, dtype=F32) / n_heads)


def _rms(x, g):
    return x * lax.rsqrt(jnp.mean(x * x, axis=-1, keepdims=True) + RMS_EPS) * g


def _dot(a, b):
    return jnp.dot(a, b, preferred_element_type=F32)


def _dot_nt(a, b):
    return lax.dot_general(a, b, (((1,), (1,)), ((), ())), preferred_element_type=F32)


def _norm_proj_kernel(h_ref, g_ref, w_ref, o_ref, *, col_tile, scaled_cols, scale):
    n = _rms(h_ref[...], g_ref[...]).astype(BF16)
    for c in range(o_ref.shape[1] // col_tile):
        cols = slice(c * col_tile, (c + 1) * col_tile)
        y = _dot(n, w_ref[:, cols])
        if (c + 1) * col_tile <= scaled_cols:
            y = y * scale
        o_ref[:, cols] = y.astype(o_ref.dtype)


def _norm_proj(h, g, w, col_tile=512, scaled_cols=0, scale=1.0):
    m, d = h.shape
    n_out = w.shape[1]
    assert scaled_cols % col_tile == 0
    return pl.pallas_call(
        functools.partial(_norm_proj_kernel, col_tile=col_tile, scaled_cols=scaled_cols, scale=scale),
        grid=(m // ROW_TILE,),
        in_specs=[pl.BlockSpec((ROW_TILE, d), lambda i: (i, 0)),
                  _resident((1, d)),
                  _resident((d, n_out))],
        out_specs=pl.BlockSpec((ROW_TILE, n_out), lambda i: (i, 0)),
        out_shape=jax.ShapeDtypeStruct((m, n_out), BF16),
        compiler_params=_params(("parallel",)),
        name="norm_proj",
    )(h, g.reshape(1, d), w)


def _dil_proj_kernel(h_ref, g_ref, w_ref, o0_ref, o1_ref, o2_ref, stage_ref, sorted_ref):
    n = _rms(h_ref[...], g_ref[...])
    rows, d = n.shape
    width = DIL_HEADS_PER_GROUP * HEAD_DIM
    for t in range(d // LANES):
        stage_ref[t] = n[:, t * LANES:(t + 1) * LANES]
    n = n.astype(BF16)
    for g, (o_ref, (_, dil)) in enumerate(zip((o0_ref, o1_ref, o2_ref), DIL_PAIRS)):
        per = rows // dil
        for r in range(dil if dil > 1 else 0):
            for t in range(d // LANES):
                sorted_ref[g - 1, r * per:(r + 1) * per, t * LANES:(t + 1) * LANES] = (
                    stage_ref[t, pl.ds(r, per, stride=dil), :].astype(BF16))
        lhs = n if dil == 1 else sorted_ref[g - 1]
        for part in range(3):
            src = (3 * g + part) * width
            y = _dot(lhs, w_ref[:, src:src + width])
            if part == 0:
                y = y * HEAD_Q_SCALE
            for r in range(dil):
                dst = (3 * r + part) * width
                o_ref[:, dst:dst + width] = y[r * per:(r + 1) * per, :].astype(BF16)


def _dil_proj(h, g, w):
    m, d = h.shape
    feat = 3 * DIL_HEADS_PER_GROUP * HEAD_DIM
    assert DIL_PAIRS[0][1] == 1
    view = lambda dil: (m // dil, dil * feat)
    view_block = lambda dil: pl.BlockSpec((ROW_TILE // dil, dil * feat), lambda i: (i, 0))
    return pl.pallas_call(
        _dil_proj_kernel,
        grid=(m // ROW_TILE,),
        in_specs=[pl.BlockSpec((ROW_TILE, d), lambda i: (i, 0)), _resident((1, d)), _resident(w.shape)],
        out_specs=[view_block(dil) for _, dil in DIL_PAIRS],
        out_shape=[jax.ShapeDtypeStruct(view(dil), BF16) for _, dil in DIL_PAIRS],
        scratch_shapes=[pltpu.VMEM((d // LANES, ROW_TILE, LANES), F32),
                        pltpu.VMEM((len(DIL_PAIRS) - 1, ROW_TILE, d), BF16)],
        compiler_params=_params(("parallel",)),
        name="dil_proj",
    )(h, g.reshape(1, d), w)


def _mlp_tail(h1, g_ref, wup_ref, wdn_ref, gf_ref, out_ref):
    n = _rms(h1, g_ref[...]).astype(BF16)
    acc = jnp.zeros_like(h1)
    for c in range(wup_ref.shape[1] // FF_TILE):
        cols = slice(c * FF_TILE, (c + 1) * FF_TILE)
        u = jnp.square(jnp.maximum(_dot(n, wup_ref[:, cols]), 0.0)).astype(BF16)
        acc = acc + _dot(u, wdn_ref[cols, :])
    out = h1 + acc
    if gf_ref is not None:
        out = _rms(out, gf_ref[...])
    out_ref[...] = out


def _post_kernel(h_ref, a_ref, wo_ref, g_ref, wup_ref, wdn_ref, *rest):
    gf_ref, out_ref = (rest[0], rest[1]) if len(rest) == 2 else (None, rest[0])
    h1 = h_ref[...] + _dot(a_ref[...], wo_ref[...])
    _mlp_tail(h1, g_ref, wup_ref, wdn_ref, gf_ref, out_ref)


def _post_merge_kernel(h_ref, o0_ref, o1_ref, o2_ref, l0_ref, l1_ref, l2_ref,
                       wo_ref, g_ref, wup_ref, wdn_ref, out_ref, stage_ref):
    rows, width = o0_ref.shape

    def token_rows(x_ref, slot, dil):
        if dil == 1:
            return x_ref[...]
        for r in range(dil):
            for t in range(width // LANES):
                src = r * width + t * LANES
                stage_ref[slot, t, pl.ds(r, rows // dil, stride=dil), :] = x_ref[:, src:src + LANES]
        return jnp.concatenate([stage_ref[slot, t] for t in range(width // LANES)], axis=1)

    dils = [dil for _, dil in DIL_PAIRS]
    l0, l1, l2 = (token_rows(ref, slot, dil) for slot, (ref, dil) in enumerate(zip((l0_ref, l1_ref, l2_ref), dils)))
    o0, o1, o2 = (token_rows(ref, 3 + slot, dil) for slot, (ref, dil) in enumerate(zip((o0_ref, o1_ref, o2_ref), dils)))
    mx = jnp.maximum(jnp.maximum(l0, l1), l2)
    e0, e1, e2 = jnp.exp(l0 - mx), jnp.exp(l1 - mx), jnp.exp(l2 - mx)
    merged = (e0 * o0 + e1 * o1 + e2 * o2) / (e0 + e1 + e2)
    h1 = h_ref[...] + _dot(merged.astype(BF16), wo_ref[...])
    _mlp_tail(h1, g_ref, wup_ref, wdn_ref, None, out_ref)


def _post(h, attn_ins, w_o, g, w_up, w_down, g_final=None):
    m, d = h.shape
    row = lambda width: pl.BlockSpec((ROW_TILE, width), lambda i: (i, 0))
    merge = len(attn_ins) > 1
    in_specs = [row(d)] + [pl.BlockSpec((ROW_TILE * a.shape[0] // m, a.shape[1]), lambda i: (i, 0)) for a in attn_ins]
    in_specs += [_resident(w_o.shape), _resident((1, d)), _resident(w_up.shape), _resident(w_down.shape)]
    args = [h, *attn_ins, w_o, g.reshape(1, d), w_up, w_down]
    if g_final is not None:
        in_specs.append(_resident((1, d)))
        args.append(g_final.reshape(1, d))
    scratch = [pltpu.VMEM((len(attn_ins), w_o.shape[0] // LANES, ROW_TILE, LANES), F32)] if merge else []
    return pl.pallas_call(
        _post_merge_kernel if merge else _post_kernel,
        grid=(m // ROW_TILE,),
        in_specs=in_specs,
        out_specs=row(d),
        out_shape=jax.ShapeDtypeStruct((m, d), F32),
        scratch_shapes=scratch,
        compiler_params=_params(("parallel",)),
        name="post_mlp",
    )(*args)


ONES_ROWS = 16
SKIP_SHIFT = -2.0 * NEG_INF


class _SoftmaxState:
    def __init__(self, m_ref, acc_ref, dv):
        self.m_ref, self.acc_ref, self.dv = m_ref, acc_ref, dv

    @staticmethod
    def scratch(n_heads, dv, tq):
        return [pltpu.VMEM((n_heads, 1, tq), F32), pltpu.VMEM((n_heads, dv + ONES_ROWS, tq), F32)]

    @staticmethod
    def store_values_t(vt_ref, hh, cols, v_t):
        dv = v_t.shape[0]
        vt_ref[hh, :dv, cols] = v_t.astype(BF16)
        vt_ref[hh, dv:, cols] = jnp.ones((ONES_ROWS, v_t.shape[1]), BF16)

    def init(self):
        self.m_ref[...] = jnp.full(self.m_ref.shape, NEG_INF, F32)
        self.acc_ref[...] = jnp.zeros(self.acc_ref.shape, F32)

    def step(self, hh, x, v_t, shift=None):
        m = self.m_ref[hh]
        m_cur = jnp.max(x, axis=0, keepdims=True)
        if shift is not None:
            m_cur = m_cur - shift
        m_new = jnp.maximum(m, m_cur)
        alpha = jnp.exp2(m - m_new)
        p = jnp.exp2(x - (m_new if shift is None else m_new + shift))
        self.m_ref[hh] = m_new
        self.acc_ref[hh] = alpha * self.acc_ref[hh] + _dot(v_t, p.astype(BF16))

    def normalized(self, hh):
        return self.acc_ref[hh, :self.dv, :] / self.acc_ref[hh, self.dv:self.dv + 1, :]


def _sweep_blocks(n_past, n_chains, state, score_fn, values_fn, past_logits, own_steps, s_a, s_b):
    def produce(buf, block, chains=range(n_chains)):
        for c in chains:
            buf[c] = score_fn(c, block)

    def consume(buf, block):
        for c in range(n_chains):
            x, shift = past_logits(c, buf[c], block)
            state.step(c, x, values_fn(c, block), shift)

    produced = set()
    for buf, block, chains, _ in own_steps:
        produce(buf, block, [c for c in chains if (id(buf), c) not in produced])
        produced.update((id(buf), c) for c in chains)
    for c in range(n_chains):
        s_a[c] = score_fn(c, 0)
        for buf, block, chains, logits_fn in own_steps:
            if c in chains:
                x, shift = logits_fn(c, buf[c])
                state.step(c, x, values_fn(c, block), shift)

    def advance(src, block, dst, next_block):
        for c in range(n_chains):
            dst[c] = score_fn(c, next_block)
            x, shift = past_logits(c, src[c], block)
            state.step(c, x, values_fn(c, block), shift)

    def two_blocks(first):
        advance(s_a, first, s_b, first + 1)
        advance(s_b, first + 1, s_a, jnp.minimum(first + 2, n_past - 1))

    def quad(t, carry):
        two_blocks(4 * t)
        two_blocks(4 * t + 2)
        return carry

    def pair(t, carry):
        two_blocks(4 * (n_past // 4) + 2 * t)
        return carry

    lax.fori_loop(0, n_past // 4, quad, 0)
    lax.fori_loop(0, (n_past % 4) // 2, pair, 0)

    @pl.when(n_past % 2 == 1)
    def _():
        consume(s_a, n_past - 1)


def _transpose_bf16(x):
    return x.astype(F32).T.astype(BF16)


def _keep_head_rows(x_t, half):
    r = lax.broadcasted_iota(jnp.int32, x_t.shape, 0)
    keep = (r < HEAD_DIM) if half == 0 else (r >= HEAD_DIM)
    return jnp.where(keep, x_t, jnp.zeros_like(x_t))


def _bf16_pieces(x, n):
    pieces = []
    for _ in range(n):
        pieces.append(x.astype(BF16))
        x = x - pieces[-1].astype(F32)
    return pieces


F32_PIECES = 3


def _moba_kernel(slopes_ref, q_ref, k_ref, v_ref, o_ref, kmean_ref, vt_ref, term_ref, kpos_ref, qh_ref,
                 s_a, s_b, s_c, m_ref, acc_ref, *, n_blocks):
    blk = MOBA_BLOCK
    n_heads = MOBA_HEADS_PER_STEP
    group = pl.program_id(1)
    i = pl.program_id(2)
    state = _SoftmaxState(m_ref, acc_ref, HEAD_DIM)
    chain = lambda tile, hh: tile * n_heads + hh
    tile_chains = [[chain(tile, hh) for hh in range(n_heads)] for tile in range(MOBA_Q_TILES)]

    @pl.when(i == 0)
    def _():
        means = []
        pos = lax.broadcasted_iota(jnp.int32, (blk, LANES), 0)
        feature = lax.broadcasted_iota(jnp.int32, (blk, LANES), 1)
        key_pos = jnp.where(feature < F32_PIECES, pos, 0).astype(F32).astype(BF16)
        for j in range(n_blocks):
            rows = slice(j * blk, (j + 1) * blk)
            means.append(jnp.mean(k_ref[0, rows, :].astype(F32), axis=0, keepdims=True))
            for tile in range(n_heads // 2):
                kpos_ref[tile, rows, :LANES] = k_ref[0, rows, tile * LANES:(tile + 1) * LANES]
                kpos_ref[tile, rows, LANES:] = key_pos
            v_t = v_ref[0, rows, :].astype(F32).T
            for hh in range(n_heads):
                state.store_values_t(vt_ref, hh, rows, v_t[hh * HEAD_DIM:(hh + 1) * HEAD_DIM, :])
        for piece, part in enumerate(_bf16_pieces(jnp.concatenate(means, axis=0), F32_PIECES)):
            kmean_ref[piece * n_blocks:(piece + 1) * n_blocks, :] = part

    state.init()
    key = lax.broadcasted_iota(jnp.int32, (blk, blk), 0)
    qry = lax.broadcasted_iota(jnp.int32, (blk, blk), 1)
    blk_idx = lax.broadcasted_iota(jnp.int32, (n_blocks, blk), 0)
    feature_row = lax.broadcasted_iota(jnp.int32, (LANES, blk), 0)
    lane_tile = lambda hh: slice((hh // 2) * LANES, (hh // 2 + 1) * LANES)
    first_block = MOBA_Q_TILES * i

    for tile in range(MOBA_Q_TILES):
        own = first_block + tile
        q_t = q_ref[0, tile * blk:(tile + 1) * blk, :].astype(F32).T
        fully_past = blk_idx < own
        block_dist = ((own - blk_idx) * blk).astype(F32)
        for hh in range(n_heads):
            slope_s = slopes_ref[n_heads * group + hh] * LOG2E
            q_h = _keep_head_rows(q_t[lane_tile(hh), :], hh % 2).astype(BF16)
            pieces = _dot(kmean_ref[:, lane_tile(hh)], q_h)
            gate = pieces[:n_blocks] + pieces[n_blocks:2 * n_blocks] + pieces[2 * n_blocks:]
            work = jnp.where(fully_past, gate, NEG_INF * HEAD_Q_SCALE)
            chosen = jnp.zeros((n_blocks, blk), jnp.bool_)
            for _ in range(min(MOBA_TOPK, n_blocks)):
                best = jnp.max(work, axis=0, keepdims=True)
                first = jnp.min(jnp.where(work == best, blk_idx, n_blocks), axis=0, keepdims=True)
                pick = blk_idx == first
                chosen = jnp.logical_or(chosen, pick)
                work = jnp.where(pick, -jnp.inf, work)
            term_ref[chain(tile, hh)] = jnp.where(jnp.logical_and(chosen, fully_past), slope_s * block_dist,
                                                  SKIP_SHIFT)
            qh_ref[chain(tile, hh), :LANES, :] = q_h
            slope_rows = jnp.zeros((LANES, blk), F32)
            for piece, part in enumerate(_bf16_pieces(jnp.full((1, blk), slope_s, F32), F32_PIECES)):
                slope_rows = jnp.where(feature_row == piece, part.astype(F32), slope_rows)
            qh_ref[chain(tile, hh), LANES:, :] = slope_rows.astype(BF16)

    def rows_of(block):
        return pl.ds(pl.multiple_of(block * blk, blk), blk)

    def scores(c, block):
        return _dot(kpos_ref[(c % n_heads) // 2, rows_of(block), :], qh_ref[c])

    def v_t(c, block):
        return vt_ref[c % n_heads, :, rows_of(block)]

    def past_logits(c, s, block):
        return s, term_ref[c, pl.ds(block, 1), :]

    def own_logits(c, s):
        return jnp.where(key <= qry, s, NEG_INF), None

    assert MOBA_Q_TILES == 2
    own_steps = [(s_b, first_block, tile_chains[0], own_logits),
                 (s_b, first_block, tile_chains[1], functools.partial(past_logits, block=first_block)),
                 (s_c, first_block + 1, tile_chains[1], own_logits)]
    _sweep_blocks(first_block, MOBA_Q_TILES * n_heads, state, scores, v_t, past_logits, own_steps, s_a, s_b)
    for tile in range(MOBA_Q_TILES):
        o_t = jnp.concatenate([state.normalized(c) for c in tile_chains[tile]], axis=0)
        o_ref[0, tile * blk:(tile + 1) * blk, :] = o_t.T.astype(o_ref.dtype)


def _moba_attention(qkv, b, s):
    n_blocks = s // MOBA_BLOCK
    n_heads = MOBA_HEADS_PER_STEP
    width = n_heads * HEAD_DIM
    n_groups = MOBA_HEADS // n_heads
    n_chains = MOBA_Q_TILES * n_heads
    step_rows = MOBA_Q_TILES * MOBA_BLOCK
    assert s % step_rows == 0
    score_buf = pltpu.VMEM((n_chains, MOBA_BLOCK, MOBA_BLOCK), F32)
    return pl.pallas_call(
        functools.partial(_moba_kernel, n_blocks=n_blocks),
        grid=(b, n_groups, s // step_rows),
        in_specs=[pl.BlockSpec(memory_space=pltpu.SMEM),
                  pl.BlockSpec((1, step_rows, width), lambda bi, g, i: (bi, i, g)),
                  pl.BlockSpec((1, s, width), lambda bi, g, i: (bi, 0, n_groups + g)),
                  pl.BlockSpec((1, s, width), lambda bi, g, i: (bi, 0, 2 * n_groups + g))],
        out_specs=pl.BlockSpec((1, step_rows, width), lambda bi, g, i: (bi, i, g)),
        out_shape=jax.ShapeDtypeStruct((b, s, MOBA_HEADS * HEAD_DIM), BF16),
        scratch_shapes=[pltpu.VMEM((3 * n_blocks, width), BF16),
                        pltpu.VMEM((n_heads, HEAD_DIM + ONES_ROWS, s), BF16),
                        pltpu.VMEM((n_chains, n_blocks, MOBA_BLOCK), F32),
                        pltpu.VMEM((n_heads // 2, s, 2 * LANES), BF16),
                        pltpu.VMEM((n_chains, 2 * LANES, MOBA_BLOCK), BF16),
                        score_buf, score_buf, score_buf,
                        *_SoftmaxState.scratch(n_chains, HEAD_DIM, MOBA_BLOCK)],
        compiler_params=_params(("parallel", "parallel", "arbitrary")),
        name="moba_attention",
    )(_alibi_slopes(MOBA_HEADS), qkv, qkv, qkv)


def _lane_half(x, half):
    lane = lax.broadcasted_iota(jnp.int32, x.shape, x.ndim - 1)
    keep = (lane < HEAD_DIM) if half == 0 else (lane >= HEAD_DIM)
    return jnp.where(keep, x, jnp.zeros_like(x))


BANDS_PER_STEP = 2


def _band_attention(n_heads, q_tile, k_tile, v_tile, logits_fn, floor_fn, store_pair, s_ref, with_lse):
    chains = [(band, h) for band in range(BANDS_PER_STEP) for h in range(n_heads)]

    def produce(c):
        band, h = chains[c]
        s_ref[c] = _dot_nt(k_tile(h, band), _lane_half(q_tile(h, band), h % 2))

    lookahead = 2
    for c in range(min(lookahead, len(chains))):
        produce(c)
    even = None
    for c, (band, h) in enumerate(chains):
        if c + lookahead < len(chains):
            produce(c + lookahead)
        x = logits_fn(h, band, s_ref[c])
        m = jnp.max(x, axis=0, keepdims=True)
        floor = floor_fn(h)
        if floor is not None:
            m = jnp.maximum(m, floor)
        p = jnp.exp2(x - m)
        den = jnp.sum(p, axis=0, keepdims=True)
        if floor is not None:
            den = den + jnp.exp2(floor - m)
        o_t = lax.dot_general(v_tile(h, band), p.astype(BF16), (((0,), (0,)), ((), ())),
                              preferred_element_type=F32)
        half = slice((h % 2) * HEAD_DIM, (h % 2 + 1) * HEAD_DIM)
        o_t = o_t[half, :] / den
        lse_t = jnp.broadcast_to((m + jnp.log2(den)) * (1.0 / LOG2E), o_t.shape) if with_lse else None
        if h % 2 == 0:
            even = (o_t, lse_t)
            continue
        store_pair(h // 2, band, jnp.concatenate([even[0], o_t], axis=0).T,
                   jnp.concatenate([even[1], lse_t], axis=0).T if with_lse else None)


def _band_rows(band, size):
    return slice(band * size, (band + 1) * size)


def _band_keys(prev_ref, cur_ref, band, size, cols):
    if band == 0:
        return jnp.concatenate([prev_ref[0, :, cols], cur_ref[0, :size, cols]], axis=0)
    return cur_ref[0, (band - 1) * size:(band + 1) * size, cols]


def _carry_previous_block(n, cur_refs, prev_refs, size):
    @pl.when(n == 0)
    def _():
        for prev in prev_refs:
            prev[...] = jnp.zeros(prev.shape, prev.dtype)

    def save():
        for cur, prev in zip(cur_refs, prev_refs):
            prev[0] = cur[0, cur.shape[1] - size:, :]
    return save


def _band_bias(bias_ref, first_step, n, size, n_heads, in_window, slope_fn):
    @pl.when(first_step)
    def _():
        key = lax.broadcasted_iota(jnp.int32, (2 * size, size), 0)
        qry = lax.broadcasted_iota(jnp.int32, (2 * size, size), 1)
        diff = qry + size - key
        window = in_window(diff)
        for h in range(n_heads):
            bias = slope_fn(h) * diff.astype(F32)
            bias_ref[h, 0] = jnp.where(window, bias, -NEG_INF)
            bias_ref[h, 1] = jnp.where(window & (key >= size), bias, -NEG_INF)

    def logits(h, band, s):
        variant = (n == 0).astype(jnp.int32) if band == 0 else 0
        return s - bias_ref[h, variant]
    return logits


def _dil_kernel(slopes_ref, q_ref, kc_ref, vc_ref, o_ref, lse_ref, s_ref, kp_ref, vp_ref, bias_ref, *,
                n_pts, dil, group):
    n = pl.program_id(2)
    first_step = (pl.program_id(0) == 0) & (pl.program_id(1) == 0) & (n == 0)
    save_last_block = _carry_previous_block(n, (kc_ref, vc_ref), (kp_ref, vp_ref), DIL_BAND)
    logits = _band_bias(bias_ref, first_step, n, DIL_BAND, DIL_HEADS_PER_GROUP,
                        lambda diff: (diff >= 0) & (diff <= n_pts),
                        lambda h: slopes_ref[group * DIL_HEADS_PER_GROUP + h] * (dil * LOG2E))
    tile = lambda h: slice((h // 2) * LANES, (h // 2 + 1) * LANES)

    def store_pair(pair, band, o, lse):
        rows, cols = _band_rows(band, DIL_BAND), slice(pair * LANES, (pair + 1) * LANES)
        o_ref[0, rows, cols] = o
        lse_ref[0, rows, cols] = lse

    _band_attention(
        DIL_HEADS_PER_GROUP,
        lambda h, band: q_ref[0, _band_rows(band, DIL_BAND), tile(h)],
        lambda h, band: _band_keys(kp_ref, kc_ref, band, DIL_BAND, tile(h)),
        lambda h, band: _band_keys(vp_ref, vc_ref, band, DIL_BAND, tile(h)),
        logits, lambda h: None, store_pair, s_ref, with_lse=True)
    save_last_block()


def _dilated_group(qkv, b, s, group, window, dil):
    width = DIL_HEADS_PER_GROUP * HEAD_DIM
    n_sub = s // dil
    step_rows = BANDS_PER_STEP * DIL_BAND
    assert n_sub % step_rows == 0
    view = qkv.reshape(b, n_sub, dil * 3 * width)

    def col(part):
        return lambda bi, r, n: (bi, n, r * 3 + part)

    blk = (1, step_rows, width)
    previous_block = pltpu.VMEM((1, DIL_BAND, width), BF16)
    out_spec = pl.BlockSpec(blk, lambda bi, r, n: (bi, n, r))
    out_sds = jax.ShapeDtypeStruct((b, n_sub, dil * width), F32)
    o, lse = pl.pallas_call(
        functools.partial(_dil_kernel, n_pts=window // dil, dil=dil, group=group),
        grid=(b, dil, n_sub // step_rows),
        in_specs=[pl.BlockSpec(memory_space=pltpu.SMEM),
                  pl.BlockSpec(blk, col(0)), pl.BlockSpec(blk, col(1)), pl.BlockSpec(blk, col(2))],
        out_specs=[out_spec, out_spec],
        out_shape=[out_sds, out_sds],
        scratch_shapes=[pltpu.VMEM((BANDS_PER_STEP * DIL_HEADS_PER_GROUP, 2 * DIL_BAND, DIL_BAND), F32),
                        previous_block, previous_block,
                        pltpu.VMEM((DIL_HEADS_PER_GROUP, 2, 2 * DIL_BAND, DIL_BAND), F32)],
        compiler_params=_params(("arbitrary", "arbitrary", "arbitrary")),
        name=f"dilated_attention_g{group}",
    )(_alibi_slopes(len(DIL_PAIRS) * DIL_HEADS_PER_GROUP), view, view, view)
    return o.reshape(b * n_sub, dil * width), lse.reshape(b * n_sub, dil * width)


def _mla_proj_kernel(h_ref, g_ref, wd_ref, gq_ref, gkv_ref, wuq_ref, wuk_ref, wuv_ref,
                     cos_ref, sin_up_ref, sin_dn_ref, q_ref, k_ref, v_ref):
    n = _rms(h_ref[...], g_ref[...]).astype(BF16)
    c = _dot(n, wd_ref[...])
    nq = _rms(c[:, :MLA_Q_RANK], gq_ref[...]).astype(BF16)
    nkv = _rms(c[:, MLA_Q_RANK:MLA_Q_RANK + MLA_KV_RANK], gkv_ref[...]).astype(BF16)
    k_rope = c[:, MLA_Q_RANK + MLA_KV_RANK:]
    half = MLA_ROPE // 2
    pair = 2

    def rope(x, width):
        cos, sin_up, sin_dn = (jnp.tile(t[...], (1, width)) for t in (cos_ref, sin_up_ref, sin_dn_ref))
        return x * cos + pltpu.roll(x, half, 1) * sin_up + pltpu.roll(x, x.shape[1] - half, 1) * sin_dn

    kr = jnp.tile(rope(k_rope, 1), (1, pair))
    for h in range(0, MLA_HEADS, pair):
        cols = slice(h * LANES, (h + pair) * LANES)
        q_ref[:, cols] = (rope(_dot(nq, wuq_ref[:, cols]), pair) * MLA_Q_SCALE).astype(BF16)
        k_ref[:, cols] = (_dot(nkv, wuk_ref[:, cols]) + kr).astype(BF16)
    v_ref[...] = _dot(nkv, wuv_ref[...]).astype(BF16)


def _mla_proj(h, g, w_dkv, q_norm, w_uq, kv_norm, w_ukv, s):
    m, d = h.shape
    qk = MLA_NOPE + MLA_ROPE
    wd = jnp.concatenate([w_dkv[:, :MLA_Q_RANK + MLA_KV_RANK],
                          jnp.zeros((d, MLA_NOPE), F32), w_dkv[:, MLA_Q_RANK + MLA_KV_RANK:],
                          jnp.zeros((d, LANES - qk), F32)], axis=1).astype(BF16)
    wuq = jnp.pad(w_uq.reshape(MLA_Q_RANK, MLA_HEADS, qk), ((0, 0), (0, 0), (0, LANES - qk)))
    wuq = wuq.reshape(MLA_Q_RANK, MLA_HEADS * LANES).astype(BF16)
    w_ukv = w_ukv.reshape(MLA_KV_RANK, MLA_HEADS, MLA_NOPE + MLA_V)
    wuk = jnp.pad(w_ukv[:, :, :MLA_NOPE], ((0, 0), (0, 0), (0, LANES - MLA_NOPE)))
    wuk = wuk.reshape(MLA_KV_RANK, MLA_HEADS * LANES).astype(BF16)
    wuv = w_ukv[:, :, MLA_NOPE:].reshape(MLA_KV_RANK, MLA_HEADS * MLA_V).astype(BF16)
    half = MLA_ROPE // 2
    inv = ROPE_THETA ** (-jnp.arange(0, MLA_ROPE, 2, dtype=F32) / MLA_ROPE)
    ang = jnp.arange(s).astype(F32)[:, None] * inv[None, :]
    cos, sin = jnp.cos(ang), jnp.sin(ang)
    zeros = lambda w: jnp.zeros((s, w), F32)
    cos_t = jnp.concatenate([jnp.ones((s, MLA_NOPE), F32), cos, cos, zeros(LANES - qk)], axis=1)
    sin_up = jnp.concatenate([zeros(MLA_NOPE + half), sin, zeros(LANES - qk)], axis=1)
    sin_dn = jnp.concatenate([zeros(MLA_NOPE), -sin, zeros(half + LANES - qk)], axis=1)

    tiles_per_seq = s // ROW_TILE
    row = lambda width: pl.BlockSpec((ROW_TILE, width), lambda i: (i, 0))
    table = pl.BlockSpec((ROW_TILE, LANES), lambda i: (i % tiles_per_seq, 0))
    return pl.pallas_call(
        _mla_proj_kernel,
        grid=(m // ROW_TILE,),
        in_specs=[row(d), _resident((1, d)), _resident(wd.shape),
                  _resident((1, MLA_Q_RANK)), _resident((1, MLA_KV_RANK)),
                  _resident(wuq.shape), _resident(wuk.shape), _resident(wuv.shape),
                  table, table, table],
        out_specs=[row(MLA_HEADS * LANES), row(MLA_HEADS * LANES), row(MLA_HEADS * MLA_V)],
        out_shape=[jax.ShapeDtypeStruct((m, MLA_HEADS * LANES), BF16),
                   jax.ShapeDtypeStruct((m, MLA_HEADS * LANES), BF16),
                   jax.ShapeDtypeStruct((m, MLA_HEADS * MLA_V), BF16)],
        compiler_params=_params(("parallel",)),
        name="mla_proj",
    )(h, g.reshape(1, d), wd, q_norm.reshape(1, -1), kv_norm.reshape(1, -1), wuq, wuk, wuv,
      cos_t, sin_up, sin_dn)


def _mla_kernel(q_ref, k_ref, v_ref, o_ref, vt_ref, qh_ref, s_a, s_b, s_c, m_ref, acc_ref, *, n_tiles):
    t = MLA_TILE
    n_heads = MLA_HEADS_PER_STEP
    i = pl.program_id(2)
    state = _SoftmaxState(m_ref, acc_ref, MLA_V)
    chain = lambda tile, hh: tile * n_heads + hh
    tile_chains = [[chain(tile, hh) for hh in range(n_heads)] for tile in range(MLA_Q_TILES)]

    @pl.when(i == 0)
    def _():
        for j in range(n_tiles):
            rows = slice(j * t, (j + 1) * t)
            v_t = v_ref[0, rows, :].astype(F32).T
            for hh in range(n_heads):
                state.store_values_t(vt_ref, hh, rows, v_t[hh * MLA_V:(hh + 1) * MLA_V, :])

    state.init()
    key = lax.broadcasted_iota(jnp.int32, (t, t), 0)
    qry = lax.broadcasted_iota(jnp.int32, (t, t), 1)
    lane_tile = lambda hh: slice(hh * LANES, (hh + 1) * LANES)
    for tile in range(MLA_Q_TILES):
        for hh in range(n_heads):
            qh_ref[chain(tile, hh)] = _transpose_bf16(q_ref[0, tile * t:(tile + 1) * t, lane_tile(hh)])

    def rows_of(block):
        return pl.ds(pl.multiple_of(block * t, t), t)

    def scores(c, block):
        return _dot(k_ref[0, rows_of(block), lane_tile(c % n_heads)], qh_ref[c])

    def v_t(c, block):
        return vt_ref[c % n_heads, :, rows_of(block)]

    def past_logits(c, s, block):
        return s, None

    def causal(c, s):
        return jnp.where(key <= qry, s, NEG_INF), None

    def full(c, s):
        return s, None

    assert MLA_Q_TILES == 2
    first = MLA_Q_TILES * i
    own_steps = [(s_b, first, tile_chains[0], causal), (s_b, first, tile_chains[1], full),
                 (s_c, first + 1, tile_chains[1], causal)]
    _sweep_blocks(first, MLA_Q_TILES * n_heads, state, scores, v_t, past_logits, own_steps, s_a, s_b)
    for tile in range(MLA_Q_TILES):
        o_t = jnp.concatenate([state.normalized(c) for c in tile_chains[tile]], axis=0)
        o_ref[0, tile * t:(tile + 1) * t, :] = o_t.T.astype(o_ref.dtype)


def _mla_attention(q, k, v, b, s):
    n_heads = MLA_HEADS_PER_STEP
    n_groups = MLA_HEADS // n_heads
    whole_seq = lambda bi, g, i: (bi, 0, g)
    tile = lambda bi, g, i: (bi, i, g)
    n_chains = MLA_Q_TILES * n_heads
    step_rows = MLA_Q_TILES * MLA_TILE
    assert s % step_rows == 0
    score_buf = pltpu.VMEM((n_chains, MLA_TILE, MLA_TILE), F32)
    return pl.pallas_call(
        functools.partial(_mla_kernel, n_tiles=s // MLA_TILE),
        grid=(b, n_groups, s // step_rows),
        in_specs=[pl.BlockSpec((1, step_rows, n_heads * LANES), tile),
                  pl.BlockSpec((1, s, n_heads * LANES), whole_seq),
                  pl.BlockSpec((1, s, n_heads * MLA_V), whole_seq)],
        out_specs=pl.BlockSpec((1, step_rows, n_heads * MLA_V), tile),
        out_shape=jax.ShapeDtypeStruct((b, s, MLA_HEADS * MLA_V), BF16),
        scratch_shapes=[pltpu.VMEM((n_heads, MLA_V + ONES_ROWS, s), BF16),
                        pltpu.VMEM((n_chains, LANES, MLA_TILE), BF16),
                        score_buf, score_buf, score_buf,
                        *_SoftmaxState.scratch(n_chains, MLA_V, MLA_TILE)],
        compiler_params=_params(("parallel", "parallel", "arbitrary")),
        name="mla_attention",
    )(q, k, v)


def _swa_kernel(slopes_ref, sinks_ref, q_ref, kc_ref, vc_ref, o_ref, s_ref, kp_ref, vp_ref, bias_ref):
    n = pl.program_id(1)
    first_step = (pl.program_id(0) == 0) & (n == 0)
    save_last_block = _carry_previous_block(n, (kc_ref, vc_ref), (kp_ref, vp_ref), SWA_BLOCK)
    group = SWA_Q_HEADS // SWA_KV_HEADS
    logits = _band_bias(bias_ref, first_step, n, SWA_BLOCK, SWA_Q_HEADS,
                        lambda diff: (diff >= 0) & (diff < SWA_WINDOW), lambda h: slopes_ref[h] * LOG2E)

    def both_halves(prev_ref, cur_ref, band, kv_head):
        x = _band_keys(prev_ref, cur_ref, band, SWA_BLOCK, slice(None))
        part = x[:, kv_head * HEAD_DIM:(kv_head + 1) * HEAD_DIM]
        return jnp.concatenate([part, part], axis=1)

    bands_kv = [(band, kv) for band in range(BANDS_PER_STEP) for kv in range(SWA_KV_HEADS)]
    k_tiles = {bk: both_halves(kp_ref, kc_ref, *bk) for bk in bands_kv}
    v_tiles = {bk: both_halves(vp_ref, vc_ref, *bk) for bk in bands_kv}

    def sink(h):
        return sinks_ref[h] * LOG2E

    def store_pair(pair, band, o, lse):
        o_ref[0, _band_rows(band, SWA_BLOCK), pair * LANES:(pair + 1) * LANES] = o.astype(o_ref.dtype)

    _band_attention(
        SWA_Q_HEADS,
        lambda h, band: q_ref[0, _band_rows(band, SWA_BLOCK), (h // 2) * LANES:(h // 2 + 1) * LANES],
        lambda h, band: k_tiles[band, h // group],
        lambda h, band: v_tiles[band, h // group],
        logits, sink, store_pair, s_ref, with_lse=False)
    save_last_block()


def _swa_attention(qkv, sinks, b, s):
    q_width = SWA_Q_HEADS * HEAD_DIM
    kv_width = SWA_KV_HEADS * HEAD_DIM
    k_tile = q_width // kv_width
    step_rows = BANDS_PER_STEP * SWA_BLOCK
    assert s % step_rows == 0
    cur = lambda t: (lambda bi, n: (bi, n, t))
    kv_blk = (1, step_rows, kv_width)
    previous_block = pltpu.VMEM((1, SWA_BLOCK, kv_width), BF16)
    return pl.pallas_call(
        _swa_kernel,
        grid=(b, s // step_rows),
        in_specs=[pl.BlockSpec(memory_space=pltpu.SMEM), pl.BlockSpec(memory_space=pltpu.SMEM),
                  pl.BlockSpec((1, step_rows, q_width), cur(0)),
                  pl.BlockSpec(kv_blk, cur(k_tile)), pl.BlockSpec(kv_blk, cur(k_tile + 1))],
        out_specs=pl.BlockSpec((1, step_rows, q_width), cur(0)),
        out_shape=jax.ShapeDtypeStruct((b, s, q_width), BF16),
        scratch_shapes=[pltpu.VMEM((BANDS_PER_STEP * SWA_Q_HEADS, 2 * SWA_BLOCK, SWA_BLOCK), F32),
                        previous_block, previous_block,
                        pltpu.VMEM((SWA_Q_HEADS, 2, 2 * SWA_BLOCK, SWA_BLOCK), F32)],
        compiler_params=_params(("arbitrary", "arbitrary")),
        name="swa_attention",
    )(_alibi_slopes(SWA_Q_HEADS), sinks.astype(F32), qkv, qkv, qkv)


def kernel(x, l0_attn_norm, l0_w_qkv, l0_w_o, l0_mlp_norm, l0_w_up, l0_w_down, l1_attn_norm, l1_w_qkv, l1_w_o, l1_mlp_norm, l1_w_up, l1_w_down, l2_attn_norm, l2_w_dkv, l2_q_norm, l2_w_uq, l2_kv_norm, l2_w_ukv, l2_w_o, l2_mlp_norm, l2_w_up, l2_w_down, l3_attn_norm, l3_w_qkv, l3_sinks, l3_w_o, l3_mlp_norm, l3_w_up, l3_w_down, final_norm):
    b, s, d = x.shape
    bf = lambda w: w.astype(BF16)
    h = x.reshape(b * s, d)

    qkv = _norm_proj(h, l0_attn_norm, bf(l0_w_qkv), scaled_cols=MOBA_HEADS * HEAD_DIM, scale=HEAD_Q_SCALE)
    a = _moba_attention(qkv.reshape(b, s, -1), b, s).reshape(b * s, -1)
    h = _post(h, (a,), bf(l0_w_o), l0_mlp_norm, bf(l0_w_up), bf(l0_w_down))

    qkvs = _dil_proj(h, l1_attn_norm, bf(l1_w_qkv))
    groups = [_dilated_group(qkvs[g], b, s, g, window, dil) for g, (window, dil) in enumerate(DIL_PAIRS)]
    h = _post(h, tuple(o for o, _ in groups) + tuple(l for _, l in groups),
              bf(l1_w_o), l1_mlp_norm, bf(l1_w_up), bf(l1_w_down))

    q, k, v = _mla_proj(h, l2_attn_norm, l2_w_dkv, l2_q_norm, l2_w_uq, l2_kv_norm, l2_w_ukv, s)
    a = _mla_attention(q.reshape(b, s, -1), k.reshape(b, s, -1), v.reshape(b, s, -1), b, s).reshape(b * s, -1)
    h = _post(h, (a,), bf(l2_w_o), l2_mlp_norm, bf(l2_w_up), bf(l2_w_down))

    qkv = _norm_proj(h, l3_attn_norm, bf(l3_w_qkv), col_tile=256, scaled_cols=SWA_Q_HEADS * HEAD_DIM,
                     scale=HEAD_Q_SCALE)
    a = _swa_attention(qkv.reshape(b, s, -1), l3_sinks, b, s).reshape(b * s, -1)
    h = _post(h, (a,), bf(l3_w_o), l3_mlp_norm, bf(l3_w_up), bf(l3_w_down), g_final=final_norm)
    return h.reshape(b, s, d)
```

```python
import functools

import jax
import jax.numpy as jnp
from jax import lax
from jax.experimental import pallas as pl
from jax.experimental.pallas import tpu as pltpu

F32 = jnp.float32
BF16 = jnp.bfloat16

D_MODEL = 1024
HEAD_DIM = 64
RMS_EPS = 1e-6
D_FF = 4 * D_MODEL
NEG_INF = -1e30

MOBA_HEADS = 16
MOBA_BLOCK = 256
MOBA_TOPK = 3

DIL_PAIRS = ((128, 1), (512, 4), (2048, 16))
DIL_HEADS_PER_GROUP = 8
DIL_BAND = 128

MLA_HEADS = 16
MLA_Q_RANK = 768
MLA_KV_RANK = 256
MLA_NOPE = 64
MLA_ROPE = 32
MLA_V = 64
ROPE_THETA = 10000.0
MLA_TILE = 256
MLA_Q_TILES = 2
MLA_HEADS_PER_STEP = 8
MOBA_HEADS_PER_STEP = 8
MOBA_Q_TILES = 2

SWA_Q_HEADS = 16
SWA_KV_HEADS = 2
SWA_WINDOW = 128
SWA_BLOCK = 128

LANES = 128
ROW_TILE = 512
FF_TILE = 512
VMEM_LIMIT = 56 * 1024 * 1024
LOG2E = 1.4426950408889634
HEAD_Q_SCALE = HEAD_DIM ** -0.5 * LOG2E
MLA_Q_SCALE = (MLA_NOPE + MLA_ROPE) ** -0.5 * LOG2E


def _params(semantics):
    return pltpu.CompilerParams(dimension_semantics=semantics, vmem_limit_bytes=VMEM_LIMIT)


def _resident(shape):
    return pl.BlockSpec(shape, lambda *_: (0,) * len(shape), pipeline_mode=pl.Buffered(1))


def _alibi_slopes(n_heads):
    return 2.0 ** (-8.0 * jnp.arange(1, n_heads + 1, dtype=F32) / n_heads)


def _rms(x, g):
    return x * lax.rsqrt(jnp.mean(x * x, axis=-1, keepdims=True) + RMS_EPS) * g


def _dot(a, b):
    return jnp.dot(a, b, preferred_element_type=F32)


def _dot_nt(a, b):
    return lax.dot_general(a, b, (((1,), (1,)), ((), ())), preferred_element_type=F32)


def _norm_proj_kernel(h_ref, g_ref, w_ref, o_ref, *, col_tile, scaled_cols, scale):
    n = _rms(h_ref[...], g_ref[...]).astype(BF16)
    for c in range(o_ref.shape[1] // col_tile):
        cols = slice(c * col_tile, (c + 1) * col_tile)
        y = _dot(n, w_ref[:, cols])
        if (c + 1) * col_tile <= scaled_cols:
            y = y * scale
        o_ref[:, cols] = y.astype(o_ref.dtype)


def _norm_proj(h, g, w, col_tile=512, scaled_cols=0, scale=1.0):
    m, d = h.shape
    n_out = w.shape[1]
    assert scaled_cols % col_tile == 0
    return pl.pallas_call(
        functools.partial(_norm_proj_kernel, col_tile=col_tile, scaled_cols=scaled_cols, scale=scale),
        grid=(m // ROW_TILE,),
        in_specs=[pl.BlockSpec((ROW_TILE, d), lambda i: (i, 0)),
                  _resident((1, d)),
                  _resident((d, n_out))],
        out_specs=pl.BlockSpec((ROW_TILE, n_out), lambda i: (i, 0)),
        out_shape=jax.ShapeDtypeStruct((m, n_out), BF16),
        compiler_params=_params(("parallel",)),
        name="norm_proj",
    )(h, g.reshape(1, d), w)


def _dil_proj_kernel(h_ref, g_ref, w_ref, o0_ref, o1_ref, o2_ref, stage_ref, sorted_ref):
    n = _rms(h_ref[...], g_ref[...])
    rows, d = n.shape
    width = DIL_HEADS_PER_GROUP * HEAD_DIM
    for t in range(d // LANES):
        stage_ref[t] = n[:, t * LANES:(t + 1) * LANES]
    n = n.astype(BF16)
    for g, (o_ref, (_, dil)) in enumerate(zip((o0_ref, o1_ref, o2_ref), DIL_PAIRS)):
        per = rows // dil
        for r in range(dil if dil > 1 else 0):
            for t in range(d // LANES):
                sorted_ref[g - 1, r * per:(r + 1) * per, t * LANES:(t + 1) * LANES] = (
                    stage_ref[t, pl.ds(r, per, stride=dil), :].astype(BF16))
        lhs = n if dil == 1 else sorted_ref[g - 1]
        for part in range(3):
            src = (3 * g + part) * width
            y = _dot(lhs, w_ref[:, src:src + width])
            if part == 0:
                y = y * HEAD_Q_SCALE
            for r in range(dil):
                dst = (3 * r + part) * width
                o_ref[:, dst:dst + width] = y[r * per:(r + 1) * per, :].astype(BF16)


def _dil_proj(h, g, w):
    m, d = h.shape
    feat = 3 * DIL_HEADS_PER_GROUP * HEAD_DIM
    assert DIL_PAIRS[0][1] == 1
    view = lambda dil: (m // dil, dil * feat)
    view_block = lambda dil: pl.BlockSpec((ROW_TILE // dil, dil * feat), lambda i: (i, 0))
    return pl.pallas_call(
        _dil_proj_kernel,
        grid=(m // ROW_TILE,),
        in_specs=[pl.BlockSpec((ROW_TILE, d), lambda i: (i, 0)), _resident((1, d)), _resident(w.shape)],
        out_specs=[view_block(dil) for _, dil in DIL_PAIRS],
        out_shape=[jax.ShapeDtypeStruct(view(dil), BF16) for _, dil in DIL_PAIRS],
        scratch_shapes=[pltpu.VMEM((d // LANES, ROW_TILE, LANES), F32),
                        pltpu.VMEM((len(DIL_PAIRS) - 1, ROW_TILE, d), BF16)],
        compiler_params=_params(("parallel",)),
        name="dil_proj",
    )(h, g.reshape(1, d), w)


def _mlp_tail(h1, g_ref, wup_ref, wdn_ref, gf_ref, out_ref):
    n = _rms(h1, g_ref[...]).astype(BF16)
    acc = jnp.zeros_like(h1)
    for c in range(wup_ref.shape[1] // FF_TILE):
        cols = slice(c * FF_TILE, (c + 1) * FF_TILE)
        u = jnp.square(jnp.maximum(_dot(n, wup_ref[:, cols]), 0.0)).astype(BF16)
        acc = acc + _dot(u, wdn_ref[cols, :])
    out = h1 + acc
    if gf_ref is not None:
        out = _rms(out, gf_ref[...])
    out_ref[...] = out


def _post_kernel(h_ref, a_ref, wo_ref, g_ref, wup_ref, wdn_ref, *rest):
    gf_ref, out_ref = (rest[0], rest[1]) if len(rest) == 2 else (None, rest[0])
    h1 = h_ref[...] + _dot(a_ref[...], wo_ref[...])
    _mlp_tail(h1, g_ref, wup_ref, wdn_ref, gf_ref, out_ref)


def _post_merge_kernel(h_ref, o0_ref, o1_ref, o2_ref, l0_ref, l1_ref, l2_ref,
                       wo_ref, g_ref, wup_ref, wdn_ref, out_ref, stage_ref):
    rows, width = o0_ref.shape

    def token_rows(x_ref, slot, dil):
        if dil == 1:
            return x_ref[...]
        for r in range(dil):
            for t in range(width // LANES):
                src = r * width + t * LANES
                stage_ref[slot, t, pl.ds(r, rows // dil, stride=dil), :] = x_ref[:, src:src + LANES]
        return jnp.concatenate([stage_ref[slot, t] for t in range(width // LANES)], axis=1)

    dils = [dil for _, dil in DIL_PAIRS]
    l0, l1, l2 = (token_rows(ref, slot, dil) for slot, (ref, dil) in enumerate(zip((l0_ref, l1_ref, l2_ref), dils)))
    o0, o1, o2 = (token_rows(ref, 3 + slot, dil) for slot, (ref, dil) in enumerate(zip((o0_ref, o1_ref, o2_ref), dils)))
    mx = jnp.maximum(jnp.maximum(l0, l1), l2)
    e0, e1, e2 = jnp.exp(l0 - mx), jnp.exp(l1 - mx), jnp.exp(l2 - mx)
    merged = (e0 * o0 + e1 * o1 + e2 * o2) / (e0 + e1 + e2)
    h1 = h_ref[...] + _dot(merged.astype(BF16), wo_ref[...])
    _mlp_tail(h1, g_ref, wup_ref, wdn_ref, None, out_ref)


def _post(h, attn_ins, w_o, g, w_up, w_down, g_final=None):
    m, d = h.shape
    row = lambda width: pl.BlockSpec((ROW_TILE, width), lambda i: (i, 0))
    merge = len(attn_ins) > 1
    in_specs = [row(d)] + [pl.BlockSpec((ROW_TILE * a.shape[0] // m, a.shape[1]), lambda i: (i, 0)) for a in attn_ins]
    in_specs += [_resident(w_o.shape), _resident((1, d)), _resident(w_up.shape), _resident(w_down.shape)]
    args = [h, *attn_ins, w_o, g.reshape(1, d), w_up, w_down]
    if g_final is not None:
        in_specs.append(_resident((1, d)))
        args.append(g_final.reshape(1, d))
    scratch = [pltpu.VMEM((len(attn_ins), w_o.shape[0] // LANES, ROW_TILE, LANES), F32)] if merge else []
    return pl.pallas_call(
        _post_merge_kernel if merge else _post_kernel,
        grid=(m // ROW_TILE,),
        in_specs=in_specs,
        out_specs=row(d),
        out_shape=jax.ShapeDtypeStruct((m, d), F32),
        scratch_shapes=scratch,
        compiler_params=_params(("parallel",)),
        name="post_mlp",
    )(*args)


ONES_ROWS = 16
SKIP_SHIFT = -2.0 * NEG_INF


class _SoftmaxState:
    def __init__(self, m_ref, acc_ref, dv):
        self.m_ref, self.acc_ref, self.dv = m_ref, acc_ref, dv

    @staticmethod
    def scratch(n_heads, dv, tq):
        return [pltpu.VMEM((n_heads, 1, tq), F32), pltpu.VMEM((n_heads, dv + ONES_ROWS, tq), F32)]

    @staticmethod
    def store_values_t(vt_ref, hh, cols, v_t):
        dv = v_t.shape[0]
        vt_ref[hh, :dv, cols] = v_t.astype(BF16)
        vt_ref[hh, dv:, cols] = jnp.ones((ONES_ROWS, v_t.shape[1]), BF16)

    def init(self):
        self.m_ref[...] = jnp.full(self.m_ref.shape, NEG_INF, F32)
        self.acc_ref[...] = jnp.zeros(self.acc_ref.shape, F32)

    def step(self, hh, x, v_t, shift=None):
        m = self.m_ref[hh]
        m_cur = jnp.max(x, axis=0, keepdims=True)
        if shift is not None:
            m_cur = m_cur - shift
        m_new = jnp.maximum(m, m_cur)
        alpha = jnp.exp2(m - m_new)
        p = jnp.exp2(x - (m_new if shift is None else m_new + shift))
        self.m_ref[hh] = m_new
        self.acc_ref[hh] = alpha * self.acc_ref[hh] + _dot(v_t, p.astype(BF16))

    def normalized(self, hh):
        return self.acc_ref[hh, :self.dv, :] / self.acc_ref[hh, self.dv:self.dv + 1, :]


def _sweep_blocks(n_past, n_chains, state, score_fn, values_fn, past_logits, own_steps, s_a, s_b):
    def produce(buf, block, chains=range(n_chains)):
        for c in chains:
            buf[c] = score_fn(c, block)

    def consume(buf, block):
        for c in range(n_chains):
            x, shift = past_logits(c, buf[c], block)
            state.step(c, x, values_fn(c, block), shift)

    produced = set()
    for buf, block, chains, _ in own_steps:
        produce(buf, block, [c for c in chains if (id(buf), c) not in produced])
        produced.update((id(buf), c) for c in chains)
    for c in range(n_chains):
        s_a[c] = score_fn(c, 0)
        for buf, block, chains, logits_fn in own_steps:
            if c in chains:
                x, shift = logits_fn(c, buf[c])
                state.step(c, x, values_fn(c, block), shift)

    def advance(src, block, dst, next_block):
        for c in range(n_chains):
            dst[c] = score_fn(c, next_block)
            x, shift = past_logits(c, src[c], block)
            state.step(c, x, values_fn(c, block), shift)

    def two_blocks(first):
        advance(s_a, first, s_b, first + 1)
        advance(s_b, first + 1, s_a, jnp.minimum(first + 2, n_past - 1))

    def quad(t, carry):
        two_blocks(4 * t)
        two_blocks(4 * t + 2)
        return carry

    def pair(t, carry):
        two_blocks(4 * (n_past // 4) + 2 * t)
        return carry

    lax.fori_loop(0, n_past // 4, quad, 0)
    lax.fori_loop(0, (n_past % 4) // 2, pair, 0)

    @pl.when(n_past % 2 == 1)
    def _():
        consume(s_a, n_past - 1)


def _transpose_bf16(x):
    return x.astype(F32).T.astype(BF16)


def _keep_head_rows(x_t, half):
    r = lax.broadcasted_iota(jnp.int32, x_t.shape, 0)
    keep = (r < HEAD_DIM) if half == 0 else (r >= HEAD_DIM)
    return jnp.where(keep, x_t, jnp.zeros_like(x_t))


def _bf16_pieces(x, n):
    pieces = []
    for _ in range(n):
        pieces.append(x.astype(BF16))
        x = x - pieces[-1].astype(F32)
    return pieces


F32_PIECES = 3


def _moba_kernel(slopes_ref, q_ref, k_ref, v_ref, o_ref, kmean_ref, vt_ref, term_ref, kpos_ref, qh_ref,
                 s_a, s_b, s_c, m_ref, acc_ref, *, n_blocks):
    blk = MOBA_BLOCK
    n_heads = MOBA_HEADS_PER_STEP
    group = pl.program_id(1)
    i = pl.program_id(2)
    state = _SoftmaxState(m_ref, acc_ref, HEAD_DIM)
    chain = lambda tile, hh: tile * n_heads + hh
    tile_chains = [[chain(tile, hh) for hh in range(n_heads)] for tile in range(MOBA_Q_TILES)]

    @pl.when(i == 0)
    def _():
        means = []
        pos = lax.broadcasted_iota(jnp.int32, (blk, LANES), 0)
        feature = lax.broadcasted_iota(jnp.int32, (blk, LANES), 1)
        key_pos = jnp.where(feature < F32_PIECES, pos, 0).astype(F32).astype(BF16)
        for j in range(n_blocks):
            rows = slice(j * blk, (j + 1) * blk)
            means.append(jnp.mean(k_ref[0, rows, :].astype(F32), axis=0, keepdims=True))
            for tile in range(n_heads // 2):
                kpos_ref[tile, rows, :LANES] = k_ref[0, rows, tile * LANES:(tile + 1) * LANES]
                kpos_ref[tile, rows, LANES:] = key_pos
            v_t = v_ref[0, rows, :].astype(F32).T
            for hh in range(n_heads):
                state.store_values_t(vt_ref, hh, rows, v_t[hh * HEAD_DIM:(hh + 1) * HEAD_DIM, :])
        for piece, part in enumerate(_bf16_pieces(jnp.concatenate(means, axis=0), F32_PIECES)):
            kmean_ref[piece * n_blocks:(piece + 1) * n_blocks, :] = part

    state.init()
    key = lax.broadcasted_iota(jnp.int32, (blk, blk), 0)
    qry = lax.broadcasted_iota(jnp.int32, (blk, blk), 1)
    blk_idx = lax.broadcasted_iota(jnp.int32, (n_blocks, blk), 0)
    feature_row = lax.broadcasted_iota(jnp.int32, (LANES, blk), 0)
    lane_tile = lambda hh: slice((hh // 2) * LANES, (hh // 2 + 1) * LANES)
    first_block = MOBA_Q_TILES * i

    for tile in range(MOBA_Q_TILES):
        own = first_block + tile
        q_t = q_ref[0, tile * blk:(tile + 1) * blk, :].astype(F32).T
        fully_past = blk_idx < own
        block_dist = ((own - blk_idx) * blk).astype(F32)
        for hh in range(n_heads):
            slope_s = slopes_ref[n_heads * group + hh] * LOG2E
            q_h = _keep_head_rows(q_t[lane_tile(hh), :], hh % 2).astype(BF16)
            pieces = _dot(kmean_ref[:, lane_tile(hh)], q_h)
            gate = pieces[:n_blocks] + pieces[n_blocks:2 * n_blocks] + pieces[2 * n_blocks:]
            work = jnp.where(fully_past, gate, NEG_INF * HEAD_Q_SCALE)
            chosen = jnp.zeros((n_blocks, blk), jnp.bool_)
            for _ in range(min(MOBA_TOPK, n_blocks)):
                best = jnp.max(work, axis=0, keepdims=True)
                first = jnp.min(jnp.where(work == best, blk_idx, n_blocks), axis=0, keepdims=True)
                pick = blk_idx == first
                chosen = jnp.logical_or(chosen, pick)
                work = jnp.where(pick, -jnp.inf, work)
            term_ref[chain(tile, hh)] = jnp.where(jnp.logical_and(chosen, fully_past), slope_s * block_dist,
                                                  SKIP_SHIFT)
            qh_ref[chain(tile, hh), :LANES, :] = q_h
            slope_rows = jnp.zeros((LANES, blk), F32)
            for piece, part in enumerate(_bf16_pieces(jnp.full((1, blk), slope_s, F32), F32_PIECES)):
                slope_rows = jnp.where(feature_row == piece, part.astype(F32), slope_rows)
            qh_ref[chain(tile, hh), LANES:, :] = slope_rows.astype(BF16)

    def rows_of(block):
        return pl.ds(pl.multiple_of(block * blk, blk), blk)

    def scores(c, block):
        return _dot(kpos_ref[(c % n_heads) // 2, rows_of(block), :], qh_ref[c])

    def v_t(c, block):
        return vt_ref[c % n_heads, :, rows_of(block)]

    def past_logits(c, s, block):
        return s, term_ref[c, pl.ds(block, 1), :]

    def own_logits(c, s):
        return jnp.where(key <= qry, s, NEG_INF), None

    assert MOBA_Q_TILES == 2
    own_steps = [(s_b, first_block, tile_chains[0], own_logits),
                 (s_b, first_block, tile_chains[1], functools.partial(past_logits, block=first_block)),
                 (s_c, first_block + 1, tile_chains[1], own_logits)]
    _sweep_blocks(first_block, MOBA_Q_TILES * n_heads, state, scores, v_t, past_logits, own_steps, s_a, s_b)
    for tile in range(MOBA_Q_TILES):
        o_t = jnp.concatenate([state.normalized(c) for c in tile_chains[tile]], axis=0)
        o_ref[0, tile * blk:(tile + 1) * blk, :] = o_t.T.astype(o_ref.dtype)


def _moba_attention(qkv, b, s):
    n_blocks = s // MOBA_BLOCK
    n_heads = MOBA_HEADS_PER_STEP
    width = n_heads * HEAD_DIM
    n_groups = MOBA_HEADS // n_heads
    n_chains = MOBA_Q_TILES * n_heads
    step_rows = MOBA_Q_TILES * MOBA_BLOCK
    assert s % step_rows == 0
    score_buf = pltpu.VMEM((n_chains, MOBA_BLOCK, MOBA_BLOCK), F32)
    return pl.pallas_call(
        functools.partial(_moba_kernel, n_blocks=n_blocks),
        grid=(b, n_groups, s // step_rows),
        in_specs=[pl.BlockSpec(memory_space=pltpu.SMEM),
                  pl.BlockSpec((1, step_rows, width), lambda bi, g, i: (bi, i, g)),
                  pl.BlockSpec((1, s, width), lambda bi, g, i: (bi, 0, n_groups + g)),
                  pl.BlockSpec((1, s, width), lambda bi, g, i: (bi, 0, 2 * n_groups + g))],
        out_specs=pl.BlockSpec((1, step_rows, width), lambda bi, g, i: (bi, i, g)),
        out_shape=jax.ShapeDtypeStruct((b, s, MOBA_HEADS * HEAD_DIM), BF16),
        scratch_shapes=[pltpu.VMEM((3 * n_blocks, width), BF16),
                        pltpu.VMEM((n_heads, HEAD_DIM + ONES_ROWS, s), BF16),
                        pltpu.VMEM((n_chains, n_blocks, MOBA_BLOCK), F32),
                        pltpu.VMEM((n_heads // 2, s, 2 * LANES), BF16),
                        pltpu.VMEM((n_chains, 2 * LANES, MOBA_BLOCK), BF16),
                        score_buf, score_buf, score_buf,
                        *_SoftmaxState.scratch(n_chains, HEAD_DIM, MOBA_BLOCK)],
        compiler_params=_params(("parallel", "parallel", "arbitrary")),
        name="moba_attention",
    )(_alibi_slopes(MOBA_HEADS), qkv, qkv, qkv)


def _lane_half(x, half):
    lane = lax.broadcasted_iota(jnp.int32, x.shape, x.ndim - 1)
    keep = (lane < HEAD_DIM) if half == 0 else (lane >= HEAD_DIM)
    return jnp.where(keep, x, jnp.zeros_like(x))


BANDS_PER_STEP = 2


def _band_attention(n_heads, q_tile, k_tile, v_tile, logits_fn, floor_fn, store_pair, s_ref, with_lse):
    chains = [(band, h) for band in range(BANDS_PER_STEP) for h in range(n_heads)]

    def produce(c):
        band, h = chains[c]
        s_ref[c] = _dot_nt(k_tile(h, band), _lane_half(q_tile(h, band), h % 2))

    lookahead = 2
    for c in range(min(lookahead, len(chains))):
        produce(c)
    even = None
    for c, (band, h) in enumerate(chains):
        if c + lookahead < len(chains):
            produce(c + lookahead)
        x = logits_fn(h, band, s_ref[c])
        m = jnp.max(x, axis=0, keepdims=True)
        floor = floor_fn(h)
        if floor is not None:
            m = jnp.maximum(m, floor)
        p = jnp.exp2(x - m)
        den = jnp.sum(p, axis=0, keepdims=True)
        if floor is not None:
            den = den + jnp.exp2(floor - m)
        o_t = lax.dot_general(v_tile(h, band), p.astype(BF16), (((0,), (0,)), ((), ())),
                              preferred_element_type=F32)
        half = slice((h % 2) * HEAD_DIM, (h % 2 + 1) * HEAD_DIM)
        o_t = o_t[half, :] / den
        lse_t = jnp.broadcast_to((m + jnp.log2(den)) * (1.0 / LOG2E), o_t.shape) if with_lse else None
        if h % 2 == 0:
            even = (o_t, lse_t)
            continue
        store_pair(h // 2, band, jnp.concatenate([even[0], o_t], axis=0).T,
                   jnp.concatenate([even[1], lse_t], axis=0).T if with_lse else None)


def _band_rows(band, size):
    return slice(band * size, (band + 1) * size)


def _band_keys(prev_ref, cur_ref, band, size, cols):
    if band == 0:
        return jnp.concatenate([prev_ref[0, :, cols], cur_ref[0, :size, cols]], axis=0)
    return cur_ref[0, (band - 1) * size:(band + 1) * size, cols]


def _carry_previous_block(n, cur_refs, prev_refs, size):
    @pl.when(n == 0)
    def _():
        for prev in prev_refs:
            prev[...] = jnp.zeros(prev.shape, prev.dtype)

    def save():
        for cur, prev in zip(cur_refs, prev_refs):
            prev[0] = cur[0, cur.shape[1] - size:, :]
    return save


def _band_bias(bias_ref, first_step, n, size, n_heads, in_window, slope_fn):
    @pl.when(first_step)
    def _():
        key = lax.broadcasted_iota(jnp.int32, (2 * size, size), 0)
        qry = lax.broadcasted_iota(jnp.int32, (2 * size, size), 1)
        diff = qry + size - key
        window = in_window(diff)
        for h in range(n_heads):
            bias = slope_fn(h) * diff.astype(F32)
            bias_ref[h, 0] = jnp.where(window, bias, -NEG_INF)
            bias_ref[h, 1] = jnp.where(window & (key >= size), bias, -NEG_INF)

    def logits(h, band, s):
        variant = (n == 0).astype(jnp.int32) if band == 0 else 0
        return s - bias_ref[h, variant]
    return logits


def _dil_kernel(slopes_ref, qkv_ref, o_ref, lse_ref, s_ref, kp_ref, vp_ref, bias_ref, *, n_pts, dil, group):
    width = DIL_HEADS_PER_GROUP * HEAD_DIM
    q_ref, kc_ref, vc_ref = (qkv_ref.at[:, :, part * width:(part + 1) * width] for part in range(3))
    n = pl.program_id(2)
    first_step = (pl.program_id(0) == 0) & (pl.program_id(1) == 0) & (n == 0)
    save_last_block = _carry_previous_block(n, (kc_ref, vc_ref), (kp_ref, vp_ref), DIL_BAND)
    logits = _band_bias(bias_ref, first_step, n, DIL_BAND, DIL_HEADS_PER_GROUP,
                        lambda diff: (diff >= 0) & (diff <= n_pts),
                        lambda h: slopes_ref[group * DIL_HEADS_PER_GROUP + h] * (dil * LOG2E))
    tile = lambda h: slice((h // 2) * LANES, (h // 2 + 1) * LANES)

    def store_pair(pair, band, o, lse):
        rows, cols = _band_rows(band, DIL_BAND), slice(pair * LANES, (pair + 1) * LANES)
        o_ref[0, rows, cols] = o
        lse_ref[0, rows, cols] = lse

    _band_attention(
        DIL_HEADS_PER_GROUP,
        lambda h, band: q_ref[0, _band_rows(band, DIL_BAND), tile(h)],
        lambda h, band: _band_keys(kp_ref, kc_ref, band, DIL_BAND, tile(h)),
        lambda h, band: _band_keys(vp_ref, vc_ref, band, DIL_BAND, tile(h)),
        logits, lambda h: None, store_pair, s_ref, with_lse=True)
    save_last_block()


def _dilated_group(qkv, b, s, group, window, dil):
    width = DIL_HEADS_PER_GROUP * HEAD_DIM
    n_sub = s // dil
    step_rows = BANDS_PER_STEP * DIL_BAND
    assert n_sub % step_rows == 0
    view = qkv.reshape(b, n_sub, dil * 3 * width)

    residue = lambda bi, r, n: (bi, n, r)
    previous_block = pltpu.VMEM((1, DIL_BAND, width), BF16)
    out_spec = pl.BlockSpec((1, step_rows, width), residue)
    out_sds = jax.ShapeDtypeStruct((b, n_sub, dil * width), F32)
    o, lse = pl.pallas_call(
        functools.partial(_dil_kernel, n_pts=window // dil, dil=dil, group=group),
        grid=(b, dil, n_sub // step_rows),
        in_specs=[pl.BlockSpec(memory_space=pltpu.SMEM),
                  pl.BlockSpec((1, step_rows, 3 * width), residue)],
        out_specs=[out_spec, out_spec],
        out_shape=[out_sds, out_sds],
        scratch_shapes=[pltpu.VMEM((BANDS_PER_STEP * DIL_HEADS_PER_GROUP, 2 * DIL_BAND, DIL_BAND), F32),
                        previous_block, previous_block,
                        pltpu.VMEM((DIL_HEADS_PER_GROUP, 2, 2 * DIL_BAND, DIL_BAND), F32)],
        compiler_params=_params(("arbitrary", "arbitrary", "arbitrary")),
        name=f"dilated_attention_g{group}",
    )(_alibi_slopes(len(DIL_PAIRS) * DIL_HEADS_PER_GROUP), view)
    return o.reshape(b * n_sub, dil * width), lse.reshape(b * n_sub, dil * width)


def _mla_proj_kernel(h_ref, g_ref, wd_ref, gq_ref, gkv_ref, wuq_ref, wuk_ref, wuv_ref,
                     cos_ref, sin_up_ref, sin_dn_ref, q_ref, k_ref, v_ref):
    n = _rms(h_ref[...], g_ref[...]).astype(BF16)
    c = _dot(n, wd_ref[...])
    nq = _rms(c[:, :MLA_Q_RANK], gq_ref[...]).astype(BF16)
    nkv = _rms(c[:, MLA_Q_RANK:MLA_Q_RANK + MLA_KV_RANK], gkv_ref[...]).astype(BF16)
    k_rope = c[:, MLA_Q_RANK + MLA_KV_RANK:]
    half = MLA_ROPE // 2
    pair = 2

    def rope(x, width):
        cos, sin_up, sin_dn = (jnp.tile(t[...], (1, width)) for t in (cos_ref, sin_up_ref, sin_dn_ref))
        return x * cos + pltpu.roll(x, half, 1) * sin_up + pltpu.roll(x, x.shape[1] - half, 1) * sin_dn

    kr = jnp.tile(rope(k_rope, 1), (1, pair))
    for h in range(0, MLA_HEADS, pair):
        cols = slice(h * LANES, (h + pair) * LANES)
        q_ref[:, cols] = (rope(_dot(nq, wuq_ref[:, cols]), pair) * MLA_Q_SCALE).astype(BF16)
        k_ref[:, cols] = (_dot(nkv, wuk_ref[:, cols]) + kr).astype(BF16)
    v_ref[...] = _dot(nkv, wuv_ref[...]).astype(BF16)


def _mla_proj(h, g, w_dkv, q_norm, w_uq, kv_norm, w_ukv, s):
    m, d = h.shape
    qk = MLA_NOPE + MLA_ROPE
    wd = jnp.concatenate([w_dkv[:, :MLA_Q_RANK + MLA_KV_RANK],
                          jnp.zeros((d, MLA_NOPE), F32), w_dkv[:, MLA_Q_RANK + MLA_KV_RANK:],
                          jnp.zeros((d, LANES - qk), F32)], axis=1).astype(BF16)
    wuq = jnp.pad(w_uq.reshape(MLA_Q_RANK, MLA_HEADS, qk), ((0, 0), (0, 0), (0, LANES - qk)))
    wuq = wuq.reshape(MLA_Q_RANK, MLA_HEADS * LANES).astype(BF16)
    w_ukv = w_ukv.reshape(MLA_KV_RANK, MLA_HEADS, MLA_NOPE + MLA_V)
    wuk = jnp.pad(w_ukv[:, :, :MLA_NOPE], ((0, 0), (0, 0), (0, LANES - MLA_NOPE)))
    wuk = wuk.reshape(MLA_KV_RANK, MLA_HEADS * LANES).astype(BF16)
    wuv = w_ukv[:, :, MLA_NOPE:].reshape(MLA_KV_RANK, MLA_HEADS * MLA_V).astype(BF16)
    half = MLA_ROPE // 2
    inv = ROPE_THETA ** (-jnp.arange(0, MLA_ROPE, 2, dtype=F32) / MLA_ROPE)
    ang = jnp.arange(s).astype(F32)[:, None] * inv[None, :]
    cos, sin = jnp.cos(ang), jnp.sin(ang)
    zeros = lambda w: jnp.zeros((s, w), F32)
    cos_t = jnp.concatenate([jnp.ones((s, MLA_NOPE), F32), cos, cos, zeros(LANES - qk)], axis=1)
    sin_up = jnp.concatenate([zeros(MLA_NOPE + half), sin, zeros(LANES - qk)], axis=1)
    sin_dn = jnp.concatenate([zeros(MLA_NOPE), -sin, zeros(half + LANES - qk)], axis=1)

    tiles_per_seq = s // ROW_TILE
    row = lambda width: pl.BlockSpec((ROW_TILE, width), lambda i: (i, 0))
    table = pl.BlockSpec((ROW_TILE, LANES), lambda i: (i % tiles_per_seq, 0))
    return pl.pallas_call(
        _mla_proj_kernel,
        grid=(m // ROW_TILE,),
        in_specs=[row(d), _resident((1, d)), _resident(wd.shape),
                  _resident((1, MLA_Q_RANK)), _resident((1, MLA_KV_RANK)),
                  _resident(wuq.shape), _resident(wuk.shape), _resident(wuv.shape),
                  table, table, table],
        out_specs=[row(MLA_HEADS * LANES), row(MLA_HEADS * LANES), row(MLA_HEADS * MLA_V)],
        out_shape=[jax.ShapeDtypeStruct((m, MLA_HEADS * LANES), BF16),
                   jax.ShapeDtypeStruct((m, MLA_HEADS * LANES), BF16),
                   jax.ShapeDtypeStruct((m, MLA_HEADS * MLA_V), BF16)],
        compiler_params=_params(("parallel",)),
        name="mla_proj",
    )(h, g.reshape(1, d), wd, q_norm.reshape(1, -1), kv_norm.reshape(1, -1), wuq, wuk, wuv,
      cos_t, sin_up, sin_dn)


def _mla_kernel(q_ref, k_ref, v_ref, o_ref, vt_ref, qh_ref, s_a, s_b, s_c, m_ref, acc_ref, *, n_tiles):
    t = MLA_TILE
    n_heads = MLA_HEADS_PER_STEP
    i = pl.program_id(2)
    state = _SoftmaxState(m_ref, acc_ref, MLA_V)
    chain = lambda tile, hh: tile * n_heads + hh
    tile_chains = [[chain(tile, hh) for hh in range(n_heads)] for tile in range(MLA_Q_TILES)]

    @pl.when(i == 0)
    def _():
        for j in range(n_tiles):
            rows = slice(j * t, (j + 1) * t)
            v_t = v_ref[0, rows, :].astype(F32).T
            for hh in range(n_heads):
                state.store_values_t(vt_ref, hh, rows, v_t[hh * MLA_V:(hh + 1) * MLA_V, :])

    state.init()
    key = lax.broadcasted_iota(jnp.int32, (t, t), 0)
    qry = lax.broadcasted_iota(jnp.int32, (t, t), 1)
    lane_tile = lambda hh: slice(hh * LANES, (hh + 1) * LANES)
    for tile in range(MLA_Q_TILES):
        for hh in range(n_heads):
            qh_ref[chain(tile, hh)] = _transpose_bf16(q_ref[0, tile * t:(tile + 1) * t, lane_tile(hh)])

    def rows_of(block):
        return pl.ds(pl.multiple_of(block * t, t), t)

    def scores(c, block):
        return _dot(k_ref[0, rows_of(block), lane_tile(c % n_heads)], qh_ref[c])

    def v_t(c, block):
        return vt_ref[c % n_heads, :, rows_of(block)]

    def past_logits(c, s, block):
        return s, None

    def causal(c, s):
        return jnp.where(key <= qry, s, NEG_INF), None

    def full(c, s):
        return s, None

    assert MLA_Q_TILES == 2
    first = MLA_Q_TILES * i
    own_steps = [(s_b, first, tile_chains[0], causal), (s_b, first, tile_chains[1], full),
                 (s_c, first + 1, tile_chains[1], causal)]
    _sweep_blocks(first, MLA_Q_TILES * n_heads, state, scores, v_t, past_logits, own_steps, s_a, s_b)
    for tile in range(MLA_Q_TILES):
        o_t = jnp.concatenate([state.normalized(c) for c in tile_chains[tile]], axis=0)
        o_ref[0, tile * t:(tile + 1) * t, :] = o_t.T.astype(o_ref.dtype)


def _mla_attention(q, k, v, b, s):
    n_heads = MLA_HEADS_PER_STEP
    n_groups = MLA_HEADS // n_heads
    whole_seq = lambda bi, g, i: (bi, 0, g)
    tile = lambda bi, g, i: (bi, i, g)
    n_chains = MLA_Q_TILES * n_heads
    step_rows = MLA_Q_TILES * MLA_TILE
    assert s % step_rows == 0
    score_buf = pltpu.VMEM((n_chains, MLA_TILE, MLA_TILE), F32)
    return pl.pallas_call(
        functools.partial(_mla_kernel, n_tiles=s // MLA_TILE),
        grid=(b, n_groups, s // step_rows),
        in_specs=[pl.BlockSpec((1, step_rows, n_heads * LANES), tile),
                  pl.BlockSpec((1, s, n_heads * LANES), whole_seq),
                  pl.BlockSpec((1, s, n_heads * MLA_V), whole_seq)],
        out_specs=pl.BlockSpec((1, step_rows, n_heads * MLA_V), tile),
        out_shape=jax.ShapeDtypeStruct((b, s, MLA_HEADS * MLA_V), BF16),
        scratch_shapes=[pltpu.VMEM((n_heads, MLA_V + ONES_ROWS, s), BF16),
                        pltpu.VMEM((n_chains, LANES, MLA_TILE), BF16),
                        score_buf, score_buf, score_buf,
                        *_SoftmaxState.scratch(n_chains, MLA_V, MLA_TILE)],
        compiler_params=_params(("parallel", "parallel", "arbitrary")),
        name="mla_attention",
    )(q, k, v)


def _swa_kernel(slopes_ref, sinks_ref, qkv_ref, o_ref, s_ref, kp_ref, vp_ref, bias_ref):
    q_width, kv_width = SWA_Q_HEADS * HEAD_DIM, SWA_KV_HEADS * HEAD_DIM
    q_ref = qkv_ref.at[:, :, :q_width]
    kc_ref = qkv_ref.at[:, :, q_width:q_width + kv_width]
    vc_ref = qkv_ref.at[:, :, q_width + kv_width:]
    n = pl.program_id(1)
    first_step = (pl.program_id(0) == 0) & (n == 0)
    save_last_block = _carry_previous_block(n, (kc_ref, vc_ref), (kp_ref, vp_ref), SWA_BLOCK)
    group = SWA_Q_HEADS // SWA_KV_HEADS
    logits = _band_bias(bias_ref, first_step, n, SWA_BLOCK, SWA_Q_HEADS,
                        lambda diff: (diff >= 0) & (diff < SWA_WINDOW), lambda h: slopes_ref[h] * LOG2E)

    def both_halves(prev_ref, cur_ref, band, kv_head):
        x = _band_keys(prev_ref, cur_ref, band, SWA_BLOCK, slice(None))
        part = x[:, kv_head * HEAD_DIM:(kv_head + 1) * HEAD_DIM]
        return jnp.concatenate([part, part], axis=1)

    bands_kv = [(band, kv) for band in range(BANDS_PER_STEP) for kv in range(SWA_KV_HEADS)]
    k_tiles = {bk: both_halves(kp_ref, kc_ref, *bk) for bk in bands_kv}
    v_tiles = {bk: both_halves(vp_ref, vc_ref, *bk) for bk in bands_kv}

    def sink(h):
        return sinks_ref[h] * LOG2E

    def store_pair(pair, band, o, lse):
        o_ref[0, _band_rows(band, SWA_BLOCK), pair * LANES:(pair + 1) * LANES] = o.astype(o_ref.dtype)

    _band_attention(
        SWA_Q_HEADS,
        lambda h, band: q_ref[0, _band_rows(band, SWA_BLOCK), (h // 2) * LANES:(h // 2 + 1) * LANES],
        lambda h, band: k_tiles[band, h // group],
        lambda h, band: v_tiles[band, h // group],
        logits, sink, store_pair, s_ref, with_lse=False)
    save_last_block()


def _swa_attention(qkv, sinks, b, s):
    q_width = SWA_Q_HEADS * HEAD_DIM
    kv_width = SWA_KV_HEADS * HEAD_DIM
    step_rows = BANDS_PER_STEP * SWA_BLOCK
    assert s % step_rows == 0
    rows = lambda bi, n: (bi, n, 0)
    previous_block = pltpu.VMEM((1, SWA_BLOCK, kv_width), BF16)
    return pl.pallas_call(
        _swa_kernel,
        grid=(b, s // step_rows),
        in_specs=[pl.BlockSpec(memory_space=pltpu.SMEM), pl.BlockSpec(memory_space=pltpu.SMEM),
                  pl.BlockSpec((1, step_rows, qkv.shape[-1]), rows)],
        out_specs=pl.BlockSpec((1, step_rows, q_width), rows),
        out_shape=jax.ShapeDtypeStruct((b, s, q_width), BF16),
        scratch_shapes=[pltpu.VMEM((BANDS_PER_STEP * SWA_Q_HEADS, 2 * SWA_BLOCK, SWA_BLOCK), F32),
                        previous_block, previous_block,
                        pltpu.VMEM((SWA_Q_HEADS, 2, 2 * SWA_BLOCK, SWA_BLOCK), F32)],
        compiler_params=_params(("arbitrary", "arbitrary")),
        name="swa_attention",
    )(_alibi_slopes(SWA_Q_HEADS), sinks.astype(F32), qkv)


def kernel(x, l0_attn_norm, l0_w_qkv, l0_w_o, l0_mlp_norm, l0_w_up, l0_w_down, l1_attn_norm, l1_w_qkv, l1_w_o, l1_mlp_norm, l1_w_up, l1_w_down, l2_attn_norm, l2_w_dkv, l2_q_norm, l2_w_uq, l2_kv_norm, l2_w_ukv, l2_w_o, l2_mlp_norm, l2_w_up, l2_w_down, l3_attn_norm, l3_w_qkv, l3_sinks, l3_w_o, l3_mlp_norm, l3_w_up, l3_w_down, final_norm):
    b, s, d = x.shape
    bf = lambda w: w.astype(BF16)
    h = x.reshape(b * s, d)

    qkv = _norm_proj(h, l0_attn_norm, bf(l0_w_qkv), scaled_cols=MOBA_HEADS * HEAD_DIM, scale=HEAD_Q_SCALE)
    a = _moba_attention(qkv.reshape(b, s, -1), b, s).reshape(b * s, -1)
    h = _post(h, (a,), bf(l0_w_o), l0_mlp_norm, bf(l0_w_up), bf(l0_w_down))

    qkvs = _dil_proj(h, l1_attn_norm, bf(l1_w_qkv))
    groups = [_dilated_group(qkvs[g], b, s, g, window, dil) for g, (window, dil) in enumerate(DIL_PAIRS)]
    h = _post(h, tuple(o for o, _ in groups) + tuple(l for _, l in groups),
              bf(l1_w_o), l1_mlp_norm, bf(l1_w_up), bf(l1_w_down))

    q, k, v = _mla_proj(h, l2_attn_norm, l2_w_dkv, l2_q_norm, l2_w_uq, l2_kv_norm, l2_w_ukv, s)
    a = _mla_attention(q.reshape(b, s, -1), k.reshape(b, s, -1), v.reshape(b, s, -1), b, s).reshape(b * s, -1)
    h = _post(h, (a,), bf(l2_w_o), l2_mlp_norm, bf(l2_w_up), bf(l2_w_down))

    qkv = _norm_proj(h, l3_attn_norm, bf(l3_w_qkv), col_tile=256, scaled_cols=SWA_Q_HEADS * HEAD_DIM,
                     scale=HEAD_Q_SCALE)
    a = _swa_attention(qkv.reshape(b, s, -1), l3_sinks, b, s).reshape(b * s, -1)
    h = _post(h, (a,), bf(l3_w_o), l3_mlp_norm, bf(l3_w_up), bf(l3_w_down), g_final=final_norm)
    return h.reshape(b, s, d)
```

```python
import functools

import jax
import jax.numpy as jnp
from jax import lax
from jax.experimental import pallas as pl
from jax.experimental.pallas import tpu as pltpu

F32 = jnp.float32
BF16 = jnp.bfloat16

D_MODEL = 1024
HEAD_DIM = 64
RMS_EPS = 1e-6
D_FF = 4 * D_MODEL
NEG_INF = -1e30

MOBA_HEADS = 16
MOBA_BLOCK = 256
MOBA_TOPK = 3

DIL_PAIRS = ((128, 1), (512, 4), (2048, 16))
DIL_HEADS_PER_GROUP = 8
DIL_BAND = 128

MLA_HEADS = 16
MLA_Q_RANK = 768
MLA_KV_RANK = 256
MLA_NOPE = 64
MLA_ROPE = 32
MLA_V = 64
ROPE_THETA = 10000.0
MLA_TILE = 256
MLA_Q_TILES = 2
MLA_HEADS_PER_STEP = 8
MOBA_HEADS_PER_STEP = 8
MOBA_Q_TILES = 2

SWA_Q_HEADS = 16
SWA_KV_HEADS = 2
SWA_WINDOW = 128
SWA_BLOCK = 128

LANES = 128
ROW_TILE = 512
FF_TILE = 512
VMEM_LIMIT = 56 * 1024 * 1024
LOG2E = 1.4426950408889634
HEAD_Q_SCALE = HEAD_DIM ** -0.5 * LOG2E
MLA_Q_SCALE = (MLA_NOPE + MLA_ROPE) ** -0.5 * LOG2E


def _params(semantics):
    return pltpu.CompilerParams(dimension_semantics=semantics, vmem_limit_bytes=VMEM_LIMIT)


def _resident(shape):
    return pl.BlockSpec(shape, lambda *_: (0,) * len(shape), pipeline_mode=pl.Buffered(1))


def _alibi_slopes(n_heads):
    return 2.0 ** (-8.0 * jnp.arange(1, n_heads + 1, dtype=F32) / n_heads)


def _rms(x, g):
    return x * lax.rsqrt(jnp.mean(x * x, axis=-1, keepdims=True) + RMS_EPS) * g


def _dot(a, b):
    return jnp.dot(a, b, preferred_element_type=F32)


def _dot_nt(a, b):
    return lax.dot_general(a, b, (((1,), (1,)), ((), ())), preferred_element_type=F32)


def _norm_proj_kernel(h_ref, g_ref, w_ref, o_ref, *, col_tile, scaled_cols, scale):
    n = _rms(h_ref[...], g_ref[...]).astype(BF16)
    for c in range(o_ref.shape[1] // col_tile):
        cols = slice(c * col_tile, (c + 1) * col_tile)
        y = _dot(n, w_ref[:, cols])
        if (c + 1) * col_tile <= scaled_cols:
            y = y * scale
        o_ref[:, cols] = y.astype(o_ref.dtype)


def _norm_proj(h, g, w, col_tile=512, scaled_cols=0, scale=1.0):
    m, d = h.shape
    n_out = w.shape[1]
    assert scaled_cols % col_tile == 0
    return pl.pallas_call(
        functools.partial(_norm_proj_kernel, col_tile=col_tile, scaled_cols=scaled_cols, scale=scale),
        grid=(m // ROW_TILE,),
        in_specs=[pl.BlockSpec((ROW_TILE, d), lambda i: (i, 0)),
                  _resident((1, d)),
                  _resident((d, n_out))],
        out_specs=pl.BlockSpec((ROW_TILE, n_out), lambda i: (i, 0)),
        out_shape=jax.ShapeDtypeStruct((m, n_out), BF16),
        compiler_params=_params(("parallel",)),
        name="norm_proj",
    )(h, g.reshape(1, d), w)


def _dil_proj_kernel(h_ref, g_ref, w_ref, o0_ref, o1_ref, o2_ref, stage_ref, sorted_ref):
    n = _rms(h_ref[...], g_ref[...])
    rows, d = n.shape
    width = DIL_HEADS_PER_GROUP * HEAD_DIM
    for t in range(d // LANES):
        stage_ref[t] = n[:, t * LANES:(t + 1) * LANES]
    n = n.astype(BF16)
    for g, (o_ref, (_, dil)) in enumerate(zip((o0_ref, o1_ref, o2_ref), DIL_PAIRS)):
        per = rows // dil
        for r in range(dil if dil > 1 else 0):
            for t in range(d // LANES):
                sorted_ref[g - 1, r * per:(r + 1) * per, t * LANES:(t + 1) * LANES] = (
                    stage_ref[t, pl.ds(r, per, stride=dil), :].astype(BF16))
        lhs = n if dil == 1 else sorted_ref[g - 1]
        for part in range(3):
            src = (3 * g + part) * width
            y = _dot(lhs, w_ref[:, src:src + width])
            if part == 0:
                y = y * HEAD_Q_SCALE
            for r in range(dil):
                dst = (3 * r + part) * width
                o_ref[:, dst:dst + width] = y[r * per:(r + 1) * per, :].astype(BF16)


def _dil_proj(h, g, w):
    m, d = h.shape
    feat = 3 * DIL_HEADS_PER_GROUP * HEAD_DIM
    assert DIL_PAIRS[0][1] == 1
    view = lambda dil: (m // dil, dil * feat)
    view_block = lambda dil: pl.BlockSpec((ROW_TILE // dil, dil * feat), lambda i: (i, 0))
    return pl.pallas_call(
        _dil_proj_kernel,
        grid=(m // ROW_TILE,),
        in_specs=[pl.BlockSpec((ROW_TILE, d), lambda i: (i, 0)), _resident((1, d)), _resident(w.shape)],
        out_specs=[view_block(dil) for _, dil in DIL_PAIRS],
        out_shape=[jax.ShapeDtypeStruct(view(dil), BF16) for _, dil in DIL_PAIRS],
        scratch_shapes=[pltpu.VMEM((d // LANES, ROW_TILE, LANES), F32),
                        pltpu.VMEM((len(DIL_PAIRS) - 1, ROW_TILE, d), BF16)],
        compiler_params=_params(("parallel",)),
        name="dil_proj",
    )(h, g.reshape(1, d), w)


def _mlp_tail(h1, g_ref, wup_ref, wdn_ref, gf_ref, out_ref):
    n = _rms(h1, g_ref[...]).astype(BF16)
    acc = jnp.zeros_like(h1)
    for c in range(wup_ref.shape[1] // FF_TILE):
        cols = slice(c * FF_TILE, (c + 1) * FF_TILE)
        u = jnp.square(jnp.maximum(_dot(n, wup_ref[:, cols]), 0.0)).astype(BF16)
        acc = acc + _dot(u, wdn_ref[cols, :])
    out = h1 + acc
    if gf_ref is not None:
        out = _rms(out, gf_ref[...])
    out_ref[...] = out


def _post_kernel(h_ref, a_ref, wo_ref, g_ref, wup_ref, wdn_ref, *rest):
    gf_ref, out_ref = (rest[0], rest[1]) if len(rest) == 2 else (None, rest[0])
    h1 = h_ref[...] + _dot(a_ref[...], wo_ref[...])
    _mlp_tail(h1, g_ref, wup_ref, wdn_ref, gf_ref, out_ref)


def _post_merge_kernel(h_ref, o0_ref, o1_ref, o2_ref, l0_ref, l1_ref, l2_ref,
                       wo_ref, g_ref, wup_ref, wdn_ref, out_ref, stage_ref):
    rows, width = o0_ref.shape

    def token_rows(x_ref, slot, dil):
        if dil == 1:
            return x_ref[...]
        for r in range(dil):
            for t in range(width // LANES):
                src = r * width + t * LANES
                stage_ref[slot, t, pl.ds(r, rows // dil, stride=dil), :] = x_ref[:, src:src + LANES]
        return jnp.concatenate([stage_ref[slot, t] for t in range(width // LANES)], axis=1)

    dils = [dil for _, dil in DIL_PAIRS]
    l0, l1, l2 = (token_rows(ref, slot, dil) for slot, (ref, dil) in enumerate(zip((l0_ref, l1_ref, l2_ref), dils)))
    o0, o1, o2 = (token_rows(ref, 3 + slot, dil) for slot, (ref, dil) in enumerate(zip((o0_ref, o1_ref, o2_ref), dils)))
    mx = jnp.maximum(jnp.maximum(l0, l1), l2)
    e0, e1, e2 = jnp.exp(l0 - mx), jnp.exp(l1 - mx), jnp.exp(l2 - mx)
    merged = (e0 * o0 + e1 * o1 + e2 * o2) / (e0 + e1 + e2)
    h1 = h_ref[...] + _dot(merged.astype(BF16), wo_ref[...])
    _mlp_tail(h1, g_ref, wup_ref, wdn_ref, None, out_ref)


def _post(h, attn_ins, w_o, g, w_up, w_down, g_final=None):
    m, d = h.shape
    row = lambda width: pl.BlockSpec((ROW_TILE, width), lambda i: (i, 0))
    merge = len(attn_ins) > 1
    in_specs = [row(d)] + [pl.BlockSpec((ROW_TILE * a.shape[0] // m, a.shape[1]), lambda i: (i, 0)) for a in attn_ins]
    in_specs += [_resident(w_o.shape), _resident((1, d)), _resident(w_up.shape), _resident(w_down.shape)]
    args = [h, *attn_ins, w_o, g.reshape(1, d), w_up, w_down]
    if g_final is not None:
        in_specs.append(_resident((1, d)))
        args.append(g_final.reshape(1, d))
    scratch = [pltpu.VMEM((len(attn_ins), w_o.shape[0] // LANES, ROW_TILE, LANES), F32)] if merge else []
    return pl.pallas_call(
        _post_merge_kernel if merge else _post_kernel,
        grid=(m // ROW_TILE,),
        in_specs=in_specs,
        out_specs=row(d),
        out_shape=jax.ShapeDtypeStruct((m, d), F32),
        scratch_shapes=scratch,
        compiler_params=_params(("parallel",)),
        name="post_mlp",
    )(*args)


ONES_ROWS = 16
SKIP_SHIFT = -2.0 * NEG_INF


class _SoftmaxState:
    def __init__(self, m_ref, acc_ref, dv):
        self.m_ref, self.acc_ref, self.dv = m_ref, acc_ref, dv

    @staticmethod
    def scratch(n_heads, dv, tq):
        return [pltpu.VMEM((n_heads, 1, tq), F32), pltpu.VMEM((n_heads, dv + ONES_ROWS, tq), F32)]

    @staticmethod
    def store_values_t(vt_ref, hh, cols, v_t):
        dv = v_t.shape[0]
        vt_ref[hh, :dv, cols] = v_t.astype(BF16)
        vt_ref[hh, dv:, cols] = jnp.ones((ONES_ROWS, v_t.shape[1]), BF16)

    def init(self):
        self.m_ref[...] = jnp.full(self.m_ref.shape, NEG_INF, F32)
        self.acc_ref[...] = jnp.zeros(self.acc_ref.shape, F32)

    def step(self, hh, x, v_t, shift=None):
        m = self.m_ref[hh]
        m_cur = jnp.max(x, axis=0, keepdims=True)
        if shift is not None:
            m_cur = m_cur - shift
        m_new = jnp.maximum(m, m_cur)
        alpha = jnp.exp2(m - m_new)
        p = jnp.exp2(x - (m_new if shift is None else m_new + shift))
        self.m_ref[hh] = m_new
        self.acc_ref[hh] = alpha * self.acc_ref[hh] + _dot(v_t, p.astype(BF16))

    def normalized(self, hh):
        return self.acc_ref[hh, :self.dv, :] / self.acc_ref[hh, self.dv:self.dv + 1, :]


def _sweep_blocks(n_past, n_chains, state, score_fn, values_fn, past_logits, own_steps, s_a, s_b):
    def produce(buf, block, chains=range(n_chains)):
        for c in chains:
            buf[c] = score_fn(c, block)

    def consume(buf, block):
        for c in range(n_chains):
            x, shift = past_logits(c, buf[c], block)
            state.step(c, x, values_fn(c, block), shift)

    produced = set()
    for buf, block, chains, _ in own_steps:
        produce(buf, block, [c for c in chains if (id(buf), c) not in produced])
        produced.update((id(buf), c) for c in chains)
    for c in range(n_chains):
        s_a[c] = score_fn(c, 0)
        for buf, block, chains, logits_fn in own_steps:
            if c in chains:
                x, shift = logits_fn(c, buf[c])
                state.step(c, x, values_fn(c, block), shift)

    def advance(src, block, dst, next_block):
        for c in range(n_chains):
            dst[c] = score_fn(c, next_block)
            x, shift = past_logits(c, src[c], block)
            state.step(c, x, values_fn(c, block), shift)

    def two_blocks(first):
        advance(s_a, first, s_b, first + 1)
        advance(s_b, first + 1, s_a, jnp.minimum(first + 2, n_past - 1))

    def quad(t, carry):
        two_blocks(4 * t)
        two_blocks(4 * t + 2)
        return carry

    def pair(t, carry):
        two_blocks(4 * (n_past // 4) + 2 * t)
        return carry

    lax.fori_loop(0, n_past // 4, quad, 0)
    lax.fori_loop(0, (n_past % 4) // 2, pair, 0)

    @pl.when(n_past % 2 == 1)
    def _():
        consume(s_a, n_past - 1)


def _transpose_bf16(x):
    return x.astype(F32).T.astype(BF16)


def _keep_head_rows(x_t, half):
    r = lax.broadcasted_iota(jnp.int32, x_t.shape, 0)
    keep = (r < HEAD_DIM) if half == 0 else (r >= HEAD_DIM)
    return jnp.where(keep, x_t, jnp.zeros_like(x_t))


def _bf16_pieces(x, n):
    pieces = []
    for _ in range(n):
        pieces.append(x.astype(BF16))
        x = x - pieces[-1].astype(F32)
    return pieces


F32_PIECES = 3


def _moba_kernel(slopes_ref, q_ref, k_ref, v_ref, o_ref, kmean_ref, vt_ref, term_ref, kpos_ref, qh_ref,
                 s_a, s_b, s_c, m_ref, acc_ref, *, n_blocks):
    blk = MOBA_BLOCK
    n_heads = MOBA_HEADS_PER_STEP
    group = pl.program_id(1)
    i = pl.program_id(2)
    state = _SoftmaxState(m_ref, acc_ref, HEAD_DIM)
    chain = lambda tile, hh: tile * n_heads + hh
    tile_chains = [[chain(tile, hh) for hh in range(n_heads)] for tile in range(MOBA_Q_TILES)]

    @pl.when(i == 0)
    def _():
        means = []
        pos = lax.broadcasted_iota(jnp.int32, (blk, LANES), 0)
        feature = lax.broadcasted_iota(jnp.int32, (blk, LANES), 1)
        key_pos = jnp.where(feature < F32_PIECES, pos, 0).astype(F32).astype(BF16)
        for j in range(n_blocks):
            rows = slice(j * blk, (j + 1) * blk)
            means.append(jnp.mean(k_ref[0, rows, :].astype(F32), axis=0, keepdims=True))
            for tile in range(n_heads // 2):
                kpos_ref[tile, rows, :LANES] = k_ref[0, rows, tile * LANES:(tile + 1) * LANES]
                kpos_ref[tile, rows, LANES:] = key_pos
            v_t = v_ref[0, rows, :].astype(F32).T
            for hh in range(n_heads):
                state.store_values_t(vt_ref, hh, rows, v_t[hh * HEAD_DIM:(hh + 1) * HEAD_DIM, :])
        for piece, part in enumerate(_bf16_pieces(jnp.concatenate(means, axis=0), F32_PIECES)):
            kmean_ref[piece * n_blocks:(piece + 1) * n_blocks, :] = part

    state.init()
    key = lax.broadcasted_iota(jnp.int32, (blk, blk), 0)
    qry = lax.broadcasted_iota(jnp.int32, (blk, blk), 1)
    blk_idx = lax.broadcasted_iota(jnp.int32, (n_blocks, blk), 0)
    feature_row = lax.broadcasted_iota(jnp.int32, (LANES, blk), 0)
    lane_tile = lambda hh: slice((hh // 2) * LANES, (hh // 2 + 1) * LANES)
    first_block = MOBA_Q_TILES * i

    for tile in range(MOBA_Q_TILES):
        own = first_block + tile
        q_t = q_ref[0, tile * blk:(tile + 1) * blk, :].astype(F32).T
        fully_past = blk_idx < own
        block_dist = ((own - blk_idx) * blk).astype(F32)
        for hh in range(n_heads):
            slope_s = slopes_ref[n_heads * group + hh] * LOG2E
            q_h = _keep_head_rows(q_t[lane_tile(hh), :], hh % 2).astype(BF16)
            pieces = _dot(kmean_ref[:, lane_tile(hh)], q_h)
            gate = pieces[:n_blocks] + pieces[n_blocks:2 * n_blocks] + pieces[2 * n_blocks:]
            work = jnp.where(fully_past, gate, NEG_INF * HEAD_Q_SCALE)
            chosen = jnp.zeros((n_blocks, blk), jnp.bool_)
            for _ in range(min(MOBA_TOPK, n_blocks)):
                best = jnp.max(work, axis=0, keepdims=True)
                first = jnp.min(jnp.where(work == best, blk_idx, n_blocks), axis=0, keepdims=True)
                pick = blk_idx == first
                chosen = jnp.logical_or(chosen, pick)
                work = jnp.where(pick, -jnp.inf, work)
            term_ref[chain(tile, hh)] = jnp.where(jnp.logical_and(chosen, fully_past), slope_s * block_dist,
                                                  SKIP_SHIFT)
            qh_ref[chain(tile, hh), :LANES, :] = q_h
            slope_rows = jnp.zeros((LANES, blk), F32)
            for piece, part in enumerate(_bf16_pieces(jnp.full((1, blk), slope_s, F32), F32_PIECES)):
                slope_rows = jnp.where(feature_row == piece, part.astype(F32), slope_rows)
            qh_ref[chain(tile, hh), LANES:, :] = slope_rows.astype(BF16)

    def rows_of(block):
        return pl.ds(pl.multiple_of(block * blk, blk), blk)

    def scores(c, block):
        return _dot(kpos_ref[(c % n_heads) // 2, rows_of(block), :], qh_ref[c])

    def v_t(c, block):
        return vt_ref[c % n_heads, :, rows_of(block)]

    def past_logits(c, s, block):
        return s, term_ref[c, pl.ds(block, 1), :]

    def own_logits(c, s):
        return jnp.where(key <= qry, s, NEG_INF), None

    assert MOBA_Q_TILES == 2
    own_steps = [(s_b, first_block, tile_chains[0], own_logits),
                 (s_b, first_block, tile_chains[1], functools.partial(past_logits, block=first_block)),
                 (s_c, first_block + 1, tile_chains[1], own_logits)]
    _sweep_blocks(first_block, MOBA_Q_TILES * n_heads, state, scores, v_t, past_logits, own_steps, s_a, s_b)
    for tile in range(MOBA_Q_TILES):
        o_t = jnp.concatenate([state.normalized(c) for c in tile_chains[tile]], axis=0)
        o_ref[0, tile * blk:(tile + 1) * blk, :] = o_t.T.astype(o_ref.dtype)


def _moba_attention(qkv, b, s):
    n_blocks = s // MOBA_BLOCK
    n_heads = MOBA_HEADS_PER_STEP
    width = n_heads * HEAD_DIM
    n_groups = MOBA_HEADS // n_heads
    n_chains = MOBA_Q_TILES * n_heads
    step_rows = MOBA_Q_TILES * MOBA_BLOCK
    assert s % step_rows == 0
    score_buf = pltpu.VMEM((n_chains, MOBA_BLOCK, MOBA_BLOCK), F32)
    return pl.pallas_call(
        functools.partial(_moba_kernel, n_blocks=n_blocks),
        grid=(b, n_groups, s // step_rows),
        in_specs=[pl.BlockSpec(memory_space=pltpu.SMEM),
                  pl.BlockSpec((1, step_rows, width), lambda bi, g, i: (bi, i, g)),
                  pl.BlockSpec((1, s, width), lambda bi, g, i: (bi, 0, n_groups + g)),
                  pl.BlockSpec((1, s, width), lambda bi, g, i: (bi, 0, 2 * n_groups + g))],
        out_specs=pl.BlockSpec((1, step_rows, width), lambda bi, g, i: (bi, i, g)),
        out_shape=jax.ShapeDtypeStruct((b, s, MOBA_HEADS * HEAD_DIM), BF16),
        scratch_shapes=[pltpu.VMEM((3 * n_blocks, width), BF16),
                        pltpu.VMEM((n_heads, HEAD_DIM + ONES_ROWS, s), BF16),
                        pltpu.VMEM((n_chains, n_blocks, MOBA_BLOCK), F32),
                        pltpu.VMEM((n_heads // 2, s, 2 * LANES), BF16),
                        pltpu.VMEM((n_chains, 2 * LANES, MOBA_BLOCK), BF16),
                        score_buf, score_buf, score_buf,
                        *_SoftmaxState.scratch(n_chains, HEAD_DIM, MOBA_BLOCK)],
        compiler_params=_params(("parallel", "parallel", "arbitrary")),
        name="moba_attention",
    )(_alibi_slopes(MOBA_HEADS), qkv, qkv, qkv)


def _lane_half(x, half):
    lane = lax.broadcasted_iota(jnp.int32, x.shape, x.ndim - 1)
    keep = (lane < HEAD_DIM) if half == 0 else (lane >= HEAD_DIM)
    return jnp.where(keep, x, jnp.zeros_like(x))


MAX_BANDS_PER_STEP = 4


def _bands_per_step(n_blocks):
    bands = MAX_BANDS_PER_STEP
    while n_blocks % bands:
        bands //= 2
    return bands


def _band_attention(n_bands, n_heads, q_tile, k_tile, v_tile, logits_fn, floor_fn, store_pair, s_ref, with_lse):
    chains = [(band, h) for band in range(n_bands) for h in range(n_heads)]

    def produce(c):
        band, h = chains[c]
        s_ref[c] = _dot_nt(k_tile(h, band), _lane_half(q_tile(h, band), h % 2))

    lookahead = 2
    for c in range(min(lookahead, len(chains))):
        produce(c)
    even = None
    for c, (band, h) in enumerate(chains):
        if c + lookahead < len(chains):
            produce(c + lookahead)
        x = logits_fn(h, band, s_ref[c])
        m = jnp.max(x, axis=0, keepdims=True)
        floor = floor_fn(h)
        if floor is not None:
            m = jnp.maximum(m, floor)
        p = jnp.exp2(x - m)
        den = jnp.sum(p, axis=0, keepdims=True)
        if floor is not None:
            den = den + jnp.exp2(floor - m)
        o_t = lax.dot_general(v_tile(h, band), p.astype(BF16), (((0,), (0,)), ((), ())),
                              preferred_element_type=F32)
        half = slice((h % 2) * HEAD_DIM, (h % 2 + 1) * HEAD_DIM)
        o_t = o_t[half, :] / den
        lse_t = jnp.broadcast_to((m + jnp.log2(den)) * (1.0 / LOG2E), o_t.shape) if with_lse else None
        if h % 2 == 0:
            even = (o_t, lse_t)
            continue
        store_pair(h // 2, band, jnp.concatenate([even[0], o_t], axis=0).T,
                   jnp.concatenate([even[1], lse_t], axis=0).T if with_lse else None)


def _band_rows(band, size):
    return slice(band * size, (band + 1) * size)


def _band_keys(prev_ref, cur_ref, band, size, cols):
    if band == 0:
        return jnp.concatenate([prev_ref[0, :, cols], cur_ref[0, :size, cols]], axis=0)
    return cur_ref[0, (band - 1) * size:(band + 1) * size, cols]


def _carry_previous_block(n, cur_refs, prev_refs, size):
    @pl.when(n == 0)
    def _():
        for prev in prev_refs:
            prev[...] = jnp.zeros(prev.shape, prev.dtype)

    def save():
        for cur, prev in zip(cur_refs, prev_refs):
            prev[0] = cur[0, cur.shape[1] - size:, :]
    return save


def _band_bias(bias_ref, first_step, n, size, n_heads, in_window, slope_fn):
    @pl.when(first_step)
    def _():
        key = lax.broadcasted_iota(jnp.int32, (2 * size, size), 0)
        qry = lax.broadcasted_iota(jnp.int32, (2 * size, size), 1)
        diff = qry + size - key
        window = in_window(diff)
        for h in range(n_heads):
            bias = slope_fn(h) * diff.astype(F32)
            bias_ref[h, 0] = jnp.where(window, bias, -NEG_INF)
            bias_ref[h, 1] = jnp.where(window & (key >= size), bias, -NEG_INF)

    def logits(h, band, s):
        variant = (n == 0).astype(jnp.int32) if band == 0 else 0
        return s - bias_ref[h, variant]
    return logits


def _dil_kernel(slopes_ref, qkv_ref, o_ref, lse_ref, s_ref, kp_ref, vp_ref, bias_ref, *, n_pts, dil, group):
    width = DIL_HEADS_PER_GROUP * HEAD_DIM
    q_ref, kc_ref, vc_ref = (qkv_ref.at[:, :, part * width:(part + 1) * width] for part in range(3))
    n = pl.program_id(2)
    first_step = (pl.program_id(0) == 0) & (pl.program_id(1) == 0) & (n == 0)
    save_last_block = _carry_previous_block(n, (kc_ref, vc_ref), (kp_ref, vp_ref), DIL_BAND)
    logits = _band_bias(bias_ref, first_step, n, DIL_BAND, DIL_HEADS_PER_GROUP,
                        lambda diff: (diff >= 0) & (diff <= n_pts),
                        lambda h: slopes_ref[group * DIL_HEADS_PER_GROUP + h] * (dil * LOG2E))
    tile = lambda h: slice((h // 2) * LANES, (h // 2 + 1) * LANES)

    def store_pair(pair, band, o, lse):
        rows, cols = _band_rows(band, DIL_BAND), slice(pair * LANES, (pair + 1) * LANES)
        o_ref[0, rows, cols] = o
        lse_ref[0, rows, cols] = lse

    _band_attention(
        qkv_ref.shape[1] // DIL_BAND, DIL_HEADS_PER_GROUP,
        lambda h, band: q_ref[0, _band_rows(band, DIL_BAND), tile(h)],
        lambda h, band: _band_keys(kp_ref, kc_ref, band, DIL_BAND, tile(h)),
        lambda h, band: _band_keys(vp_ref, vc_ref, band, DIL_BAND, tile(h)),
        logits, lambda h: None, store_pair, s_ref, with_lse=True)
    save_last_block()


def _dilated_group(qkv, b, s, group, window, dil):
    width = DIL_HEADS_PER_GROUP * HEAD_DIM
    n_sub = s // dil
    assert n_sub % DIL_BAND == 0
    n_bands = _bands_per_step(n_sub // DIL_BAND)
    step_rows = n_bands * DIL_BAND
    view = qkv.reshape(b, n_sub, dil * 3 * width)

    residue = lambda bi, r, n: (bi, n, r)
    previous_block = pltpu.VMEM((1, DIL_BAND, width), BF16)
    out_spec = pl.BlockSpec((1, step_rows, width), residue)
    out_sds = jax.ShapeDtypeStruct((b, n_sub, dil * width), F32)
    o, lse = pl.pallas_call(
        functools.partial(_dil_kernel, n_pts=window // dil, dil=dil, group=group),
        grid=(b, dil, n_sub // step_rows),
        in_specs=[pl.BlockSpec(memory_space=pltpu.SMEM),
                  pl.BlockSpec((1, step_rows, 3 * width), residue)],
        out_specs=[out_spec, out_spec],
        out_shape=[out_sds, out_sds],
        scratch_shapes=[pltpu.VMEM((n_bands * DIL_HEADS_PER_GROUP, 2 * DIL_BAND, DIL_BAND), F32),
                        previous_block, previous_block,
                        pltpu.VMEM((DIL_HEADS_PER_GROUP, 2, 2 * DIL_BAND, DIL_BAND), F32)],
        compiler_params=_params(("arbitrary", "arbitrary", "arbitrary")),
        name=f"dilated_attention_g{group}",
    )(_alibi_slopes(len(DIL_PAIRS) * DIL_HEADS_PER_GROUP), view)
    return o.reshape(b * n_sub, dil * width), lse.reshape(b * n_sub, dil * width)


def _mla_proj_kernel(h_ref, g_ref, wd_ref, gq_ref, gkv_ref, wuq_ref, wuk_ref, wuv_ref,
                     cos_ref, sin_up_ref, sin_dn_ref, q_ref, k_ref, v_ref):
    n = _rms(h_ref[...], g_ref[...]).astype(BF16)
    c = _dot(n, wd_ref[...])
    nq = _rms(c[:, :MLA_Q_RANK], gq_ref[...]).astype(BF16)
    nkv = _rms(c[:, MLA_Q_RANK:MLA_Q_RANK + MLA_KV_RANK], gkv_ref[...]).astype(BF16)
    k_rope = c[:, MLA_Q_RANK + MLA_KV_RANK:]
    half = MLA_ROPE // 2
    pair = 2

    def rope(x, width):
        cos, sin_up, sin_dn = (jnp.tile(t[...], (1, width)) for t in (cos_ref, sin_up_ref, sin_dn_ref))
        return x * cos + pltpu.roll(x, half, 1) * sin_up + pltpu.roll(x, x.shape[1] - half, 1) * sin_dn

    kr = jnp.tile(rope(k_rope, 1), (1, pair))
    for h in range(0, MLA_HEADS, pair):
        cols = slice(h * LANES, (h + pair) * LANES)
        q_ref[:, cols] = (rope(_dot(nq, wuq_ref[:, cols]), pair) * MLA_Q_SCALE).astype(BF16)
        k_ref[:, cols] = (_dot(nkv, wuk_ref[:, cols]) + kr).astype(BF16)
    v_ref[...] = _dot(nkv, wuv_ref[...]).astype(BF16)


def _mla_proj(h, g, w_dkv, q_norm, w_uq, kv_norm, w_ukv, s):
    m, d = h.shape
    qk = MLA_NOPE + MLA_ROPE
    wd = jnp.concatenate([w_dkv[:, :MLA_Q_RANK + MLA_KV_RANK],
                          jnp.zeros((d, MLA_NOPE), F32), w_dkv[:, MLA_Q_RANK + MLA_KV_RANK:],
                          jnp.zeros((d, LANES - qk), F32)], axis=1).astype(BF16)
    wuq = jnp.pad(w_uq.reshape(MLA_Q_RANK, MLA_HEADS, qk), ((0, 0), (0, 0), (0, LANES - qk)))
    wuq = wuq.reshape(MLA_Q_RANK, MLA_HEADS * LANES).astype(BF16)
    w_ukv = w_ukv.reshape(MLA_KV_RANK, MLA_HEADS, MLA_NOPE + MLA_V)
    wuk = jnp.pad(w_ukv[:, :, :MLA_NOPE], ((0, 0), (0, 0), (0, LANES - MLA_NOPE)))
    wuk = wuk.reshape(MLA_KV_RANK, MLA_HEADS * LANES).astype(BF16)
    wuv = w_ukv[:, :, MLA_NOPE:].reshape(MLA_KV_RANK, MLA_HEADS * MLA_V).astype(BF16)
    half = MLA_ROPE // 2
    inv = ROPE_THETA ** (-jnp.arange(0, MLA_ROPE, 2, dtype=F32) / MLA_ROPE)
    ang = jnp.arange(s).astype(F32)[:, None] * inv[None, :]
    cos, sin = jnp.cos(ang), jnp.sin(ang)
    zeros = lambda w: jnp.zeros((s, w), F32)
    cos_t = jnp.concatenate([jnp.ones((s, MLA_NOPE), F32), cos, cos, zeros(LANES - qk)], axis=1)
    sin_up = jnp.concatenate([zeros(MLA_NOPE + half), sin, zeros(LANES - qk)], axis=1)
    sin_dn = jnp.concatenate([zeros(MLA_NOPE), -sin, zeros(half + LANES - qk)], axis=1)

    tiles_per_seq = s // ROW_TILE
    row = lambda width: pl.BlockSpec((ROW_TILE, width), lambda i: (i, 0))
    table = pl.BlockSpec((ROW_TILE, LANES), lambda i: (i % tiles_per_seq, 0))
    return pl.pallas_call(
        _mla_proj_kernel,
        grid=(m // ROW_TILE,),
        in_specs=[row(d), _resident((1, d)), _resident(wd.shape),
                  _resident((1, MLA_Q_RANK)), _resident((1, MLA_KV_RANK)),
                  _resident(wuq.shape), _resident(wuk.shape), _resident(wuv.shape),
                  table, table, table],
        out_specs=[row(MLA_HEADS * LANES), row(MLA_HEADS * LANES), row(MLA_HEADS * MLA_V)],
        out_shape=[jax.ShapeDtypeStruct((m, MLA_HEADS * LANES), BF16),
                   jax.ShapeDtypeStruct((m, MLA_HEADS * LANES), BF16),
                   jax.ShapeDtypeStruct((m, MLA_HEADS * MLA_V), BF16)],
        compiler_params=_params(("parallel",)),
        name="mla_proj",
    )(h, g.reshape(1, d), wd, q_norm.reshape(1, -1), kv_norm.reshape(1, -1), wuq, wuk, wuv,
      cos_t, sin_up, sin_dn)


def _mla_kernel(q_ref, k_ref, v_ref, o_ref, vt_ref, qh_ref, s_a, s_b, s_c, m_ref, acc_ref, *, n_tiles):
    t = MLA_TILE
    n_heads = MLA_HEADS_PER_STEP
    i = pl.program_id(2)
    state = _SoftmaxState(m_ref, acc_ref, MLA_V)
    chain = lambda tile, hh: tile * n_heads + hh
    tile_chains = [[chain(tile, hh) for hh in range(n_heads)] for tile in range(MLA_Q_TILES)]

    @pl.when(i == 0)
    def _():
        for j in range(n_tiles):
            rows = slice(j * t, (j + 1) * t)
            v_t = v_ref[0, rows, :].astype(F32).T
            for hh in range(n_heads):
                state.store_values_t(vt_ref, hh, rows, v_t[hh * MLA_V:(hh + 1) * MLA_V, :])

    state.init()
    key = lax.broadcasted_iota(jnp.int32, (t, t), 0)
    qry = lax.broadcasted_iota(jnp.int32, (t, t), 1)
    lane_tile = lambda hh: slice(hh * LANES, (hh + 1) * LANES)
    for tile in range(MLA_Q_TILES):
        for hh in range(n_heads):
            qh_ref[chain(tile, hh)] = _transpose_bf16(q_ref[0, tile * t:(tile + 1) * t, lane_tile(hh)])

    def rows_of(block):
        return pl.ds(pl.multiple_of(block * t, t), t)

    def scores(c, block):
        return _dot(k_ref[0, rows_of(block), lane_tile(c % n_heads)], qh_ref[c])

    def v_t(c, block):
        return vt_ref[c % n_heads, :, rows_of(block)]

    def past_logits(c, s, block):
        return s, None

    def causal(c, s):
        return jnp.where(key <= qry, s, NEG_INF), None

    def full(c, s):
        return s, None

    assert MLA_Q_TILES == 2
    first = MLA_Q_TILES * i
    own_steps = [(s_b, first, tile_chains[0], causal), (s_b, first, tile_chains[1], full),
                 (s_c, first + 1, tile_chains[1], causal)]
    _sweep_blocks(first, MLA_Q_TILES * n_heads, state, scores, v_t, past_logits, own_steps, s_a, s_b)
    for tile in range(MLA_Q_TILES):
        o_t = jnp.concatenate([state.normalized(c) for c in tile_chains[tile]], axis=0)
        o_ref[0, tile * t:(tile + 1) * t, :] = o_t.T.astype(o_ref.dtype)


def _mla_attention(q, k, v, b, s):
    n_heads = MLA_HEADS_PER_STEP
    n_groups = MLA_HEADS // n_heads
    whole_seq = lambda bi, g, i: (bi, 0, g)
    tile = lambda bi, g, i: (bi, i, g)
    n_chains = MLA_Q_TILES * n_heads
    step_rows = MLA_Q_TILES * MLA_TILE
    assert s % step_rows == 0
    score_buf = pltpu.VMEM((n_chains, MLA_TILE, MLA_TILE), F32)
    return pl.pallas_call(
        functools.partial(_mla_kernel, n_tiles=s // MLA_TILE),
        grid=(b, n_groups, s // step_rows),
        in_specs=[pl.BlockSpec((1, step_rows, n_heads * LANES), tile),
                  pl.BlockSpec((1, s, n_heads * LANES), whole_seq),
                  pl.BlockSpec((1, s, n_heads * MLA_V), whole_seq)],
        out_specs=pl.BlockSpec((1, step_rows, n_heads * MLA_V), tile),
        out_shape=jax.ShapeDtypeStruct((b, s, MLA_HEADS * MLA_V), BF16),
        scratch_shapes=[pltpu.VMEM((n_heads, MLA_V + ONES_ROWS, s), BF16),
                        pltpu.VMEM((n_chains, LANES, MLA_TILE), BF16),
                        score_buf, score_buf, score_buf,
                        *_SoftmaxState.scratch(n_chains, MLA_V, MLA_TILE)],
        compiler_params=_params(("parallel", "parallel", "arbitrary")),
        name="mla_attention",
    )(q, k, v)


def _swa_kernel(slopes_ref, sinks_ref, qkv_ref, o_ref, s_ref, kp_ref, vp_ref, bias_ref):
    q_width, kv_width = SWA_Q_HEADS * HEAD_DIM, SWA_KV_HEADS * HEAD_DIM
    q_ref = qkv_ref.at[:, :, :q_width]
    kc_ref = qkv_ref.at[:, :, q_width:q_width + kv_width]
    vc_ref = qkv_ref.at[:, :, q_width + kv_width:]
    n = pl.program_id(1)
    first_step = (pl.program_id(0) == 0) & (n == 0)
    save_last_block = _carry_previous_block(n, (kc_ref, vc_ref), (kp_ref, vp_ref), SWA_BLOCK)
    group = SWA_Q_HEADS // SWA_KV_HEADS
    logits = _band_bias(bias_ref, first_step, n, SWA_BLOCK, SWA_Q_HEADS,
                        lambda diff: (diff >= 0) & (diff < SWA_WINDOW), lambda h: slopes_ref[h] * LOG2E)

    def both_halves(prev_ref, cur_ref, band, kv_head):
        x = _band_keys(prev_ref, cur_ref, band, SWA_BLOCK, slice(None))
        part = x[:, kv_head * HEAD_DIM:(kv_head + 1) * HEAD_DIM]
        return jnp.concatenate([part, part], axis=1)

    n_bands = qkv_ref.shape[1] // SWA_BLOCK
    bands_kv = [(band, kv) for band in range(n_bands) for kv in range(SWA_KV_HEADS)]
    k_tiles = {bk: both_halves(kp_ref, kc_ref, *bk) for bk in bands_kv}
    v_tiles = {bk: both_halves(vp_ref, vc_ref, *bk) for bk in bands_kv}

    def sink(h):
        return sinks_ref[h] * LOG2E

    def store_pair(pair, band, o, lse):
        o_ref[0, _band_rows(band, SWA_BLOCK), pair * LANES:(pair + 1) * LANES] = o.astype(o_ref.dtype)

    _band_attention(
        n_bands, SWA_Q_HEADS,
        lambda h, band: q_ref[0, _band_rows(band, SWA_BLOCK), (h // 2) * LANES:(h // 2 + 1) * LANES],
        lambda h, band: k_tiles[band, h // group],
        lambda h, band: v_tiles[band, h // group],
        logits, sink, store_pair, s_ref, with_lse=False)
    save_last_block()


def _swa_attention(qkv, sinks, b, s):
    q_width = SWA_Q_HEADS * HEAD_DIM
    kv_width = SWA_KV_HEADS * HEAD_DIM
    assert s % SWA_BLOCK == 0
    n_bands = _bands_per_step(s // SWA_BLOCK)
    step_rows = n_bands * SWA_BLOCK
    rows = lambda bi, n: (bi, n, 0)
    previous_block = pltpu.VMEM((1, SWA_BLOCK, kv_width), BF16)
    return pl.pallas_call(
        _swa_kernel,
        grid=(b, s // step_rows),
        in_specs=[pl.BlockSpec(memory_space=pltpu.SMEM), pl.BlockSpec(memory_space=pltpu.SMEM),
                  pl.BlockSpec((1, step_rows, qkv.shape[-1]), rows)],
        out_specs=pl.BlockSpec((1, step_rows, q_width), rows),
        out_shape=jax.ShapeDtypeStruct((b, s, q_width), BF16),
        scratch_shapes=[pltpu.VMEM((n_bands * SWA_Q_HEADS, 2 * SWA_BLOCK, SWA_BLOCK), F32),
                        previous_block, previous_block,
                        pltpu.VMEM((SWA_Q_HEADS, 2, 2 * SWA_BLOCK, SWA_BLOCK), F32)],
        compiler_params=_params(("arbitrary", "arbitrary")),
        name="swa_attention",
    )(_alibi_slopes(SWA_Q_HEADS), sinks.astype(F32), qkv)


def kernel(x, l0_attn_norm, l0_w_qkv, l0_w_o, l0_mlp_norm, l0_w_up, l0_w_down, l1_attn_norm, l1_w_qkv, l1_w_o, l1_mlp_norm, l1_w_up, l1_w_down, l2_attn_norm, l2_w_dkv, l2_q_norm, l2_w_uq, l2_kv_norm, l2_w_ukv, l2_w_o, l2_mlp_norm, l2_w_up, l2_w_down, l3_attn_norm, l3_w_qkv, l3_sinks, l3_w_o, l3_mlp_norm, l3_w_up, l3_w_down, final_norm):
    b, s, d = x.shape
    bf = lambda w: w.astype(BF16)
    h = x.reshape(b * s, d)

    qkv = _norm_proj(h, l0_attn_norm, bf(l0_w_qkv), scaled_cols=MOBA_HEADS * HEAD_DIM, scale=HEAD_Q_SCALE)
    a = _moba_attention(qkv.reshape(b, s, -1), b, s).reshape(b * s, -1)
    h = _post(h, (a,), bf(l0_w_o), l0_mlp_norm, bf(l0_w_up), bf(l0_w_down))

    qkvs = _dil_proj(h, l1_attn_norm, bf(l1_w_qkv))
    groups = [_dilated_group(qkvs[g], b, s, g, window, dil) for g, (window, dil) in enumerate(DIL_PAIRS)]
    h = _post(h, tuple(o for o, _ in groups) + tuple(l for _, l in groups),
              bf(l1_w_o), l1_mlp_norm, bf(l1_w_up), bf(l1_w_down))

    q, k, v = _mla_proj(h, l2_attn_norm, l2_w_dkv, l2_q_norm, l2_w_uq, l2_kv_norm, l2_w_ukv, s)
    a = _mla_attention(q.reshape(b, s, -1), k.reshape(b, s, -1), v.reshape(b, s, -1), b, s).reshape(b * s, -1)
    h = _post(h, (a,), bf(l2_w_o), l2_mlp_norm, bf(l2_w_up), bf(l2_w_down))

    qkv = _norm_proj(h, l3_attn_norm, bf(l3_w_qkv), col_tile=256, scaled_cols=SWA_Q_HEADS * HEAD_DIM,
                     scale=HEAD_Q_SCALE)
    a = _swa_attention(qkv.reshape(b, s, -1), l3_sinks, b, s).reshape(b * s, -1)
    h = _post(h, (a,), bf(l3_w_o), l3_mlp_norm, bf(l3_w_up), bf(l3_w_down), g_final=final_norm)
    return h.reshape(b, s, d)
```

```python
import functools

import jax
import jax.numpy as jnp
from jax import lax
from jax.experimental import pallas as pl
from jax.experimental.pallas import tpu as pltpu

F32 = jnp.float32
BF16 = jnp.bfloat16

D_MODEL = 1024
HEAD_DIM = 64
RMS_EPS = 1e-6
D_FF = 4 * D_MODEL
NEG_INF = -1e30

MOBA_HEADS = 16
MOBA_BLOCK = 256
MOBA_TOPK = 3

DIL_PAIRS = ((128, 1), (512, 4), (2048, 16))
DIL_HEADS_PER_GROUP = 8
DIL_BAND = 128

MLA_HEADS = 16
MLA_Q_RANK = 768
MLA_KV_RANK = 256
MLA_NOPE = 64
MLA_ROPE = 32
MLA_V = 64
ROPE_THETA = 10000.0
MLA_TILE = 256
MLA_Q_TILES = 2
MLA_HEADS_PER_STEP = 8
MOBA_HEADS_PER_STEP = 8
MOBA_Q_TILES = 2

SWA_Q_HEADS = 16
SWA_KV_HEADS = 2
SWA_WINDOW = 128
SWA_BLOCK = 128

LANES = 128
ROW_TILE = 512
FF_TILE = 512
VMEM_LIMIT = 56 * 1024 * 1024
LOG2E = 1.4426950408889634
HEAD_Q_SCALE = HEAD_DIM ** -0.5 * LOG2E
MLA_Q_SCALE = (MLA_NOPE + MLA_ROPE) ** -0.5 * LOG2E


def _params(semantics):
    return pltpu.CompilerParams(dimension_semantics=semantics, vmem_limit_bytes=VMEM_LIMIT)


def _resident(shape):
    return pl.BlockSpec(shape, lambda *_: (0,) * len(shape), pipeline_mode=pl.Buffered(1))


def _alibi_slopes(n_heads):
    return 2.0 ** (-8.0 * jnp.arange(1, n_heads + 1, dtype=F32) / n_heads)


def _rms(x, g):
    return x * lax.rsqrt(jnp.mean(x * x, axis=-1, keepdims=True) + RMS_EPS) * g


def _dot(a, b):
    return jnp.dot(a, b, preferred_element_type=F32)


def _dot_nt(a, b):
    return lax.dot_general(a, b, (((1,), (1,)), ((), ())), preferred_element_type=F32)


def _norm_proj_kernel(h_ref, g_ref, w_ref, o_ref, *, col_tile, scaled_cols, scale):
    n = _rms(h_ref[...], g_ref[...]).astype(BF16)
    for c in range(o_ref.shape[1] // col_tile):
        cols = slice(c * col_tile, (c + 1) * col_tile)
        y = _dot(n, w_ref[:, cols])
        if (c + 1) * col_tile <= scaled_cols:
            y = y * scale
        o_ref[:, cols] = y.astype(o_ref.dtype)


def _norm_proj(h, g, w, col_tile=512, scaled_cols=0, scale=1.0):
    m, d = h.shape
    n_out = w.shape[1]
    assert scaled_cols % col_tile == 0
    return pl.pallas_call(
        functools.partial(_norm_proj_kernel, col_tile=col_tile, scaled_cols=scaled_cols, scale=scale),
        grid=(m // ROW_TILE,),
        in_specs=[pl.BlockSpec((ROW_TILE, d), lambda i: (i, 0)),
                  _resident((1, d)),
                  _resident((d, n_out))],
        out_specs=pl.BlockSpec((ROW_TILE, n_out), lambda i: (i, 0)),
        out_shape=jax.ShapeDtypeStruct((m, n_out), BF16),
        compiler_params=_params(("parallel",)),
        name="norm_proj",
    )(h, g.reshape(1, d), w)


def _dil_proj_kernel(h_ref, g_ref, w_ref, o0_ref, o1_ref, o2_ref, stage_ref, sorted_ref):
    n = _rms(h_ref[...], g_ref[...])
    rows, d = n.shape
    width = DIL_HEADS_PER_GROUP * HEAD_DIM
    for t in range(d // LANES):
        stage_ref[t] = n[:, t * LANES:(t + 1) * LANES]
    n = n.astype(BF16)
    for g, (o_ref, (_, dil)) in enumerate(zip((o0_ref, o1_ref, o2_ref), DIL_PAIRS)):
        per = rows // dil
        for r in range(dil if dil > 1 else 0):
            for t in range(d // LANES):
                sorted_ref[g - 1, r * per:(r + 1) * per, t * LANES:(t + 1) * LANES] = (
                    stage_ref[t, pl.ds(r, per, stride=dil), :].astype(BF16))
        lhs = n if dil == 1 else sorted_ref[g - 1]
        for part in range(3):
            src = (3 * g + part) * width
            y = _dot(lhs, w_ref[:, src:src + width])
            if part == 0:
                y = y * HEAD_Q_SCALE
            for r in range(dil):
                dst = (3 * r + part) * width
                o_ref[:, dst:dst + width] = y[r * per:(r + 1) * per, :].astype(BF16)


def _dil_proj(h, g, w):
    m, d = h.shape
    feat = 3 * DIL_HEADS_PER_GROUP * HEAD_DIM
    assert DIL_PAIRS[0][1] == 1
    view = lambda dil: (m // dil, dil * feat)
    view_block = lambda dil: pl.BlockSpec((ROW_TILE // dil, dil * feat), lambda i: (i, 0))
    return pl.pallas_call(
        _dil_proj_kernel,
        grid=(m // ROW_TILE,),
        in_specs=[pl.BlockSpec((ROW_TILE, d), lambda i: (i, 0)), _resident((1, d)), _resident(w.shape)],
        out_specs=[view_block(dil) for _, dil in DIL_PAIRS],
        out_shape=[jax.ShapeDtypeStruct(view(dil), BF16) for _, dil in DIL_PAIRS],
        scratch_shapes=[pltpu.VMEM((d // LANES, ROW_TILE, LANES), F32),
                        pltpu.VMEM((len(DIL_PAIRS) - 1, ROW_TILE, d), BF16)],
        compiler_params=_params(("parallel",)),
        name="dil_proj",
    )(h, g.reshape(1, d), w)


def _mlp_tail(h1, g_ref, wup_ref, wdn_ref, gf_ref, out_ref):
    n = _rms(h1, g_ref[...]).astype(BF16)
    acc = jnp.zeros_like(h1)
    for c in range(wup_ref.shape[1] // FF_TILE):
        cols = slice(c * FF_TILE, (c + 1) * FF_TILE)
        u = jnp.square(jnp.maximum(_dot(n, wup_ref[:, cols]), 0.0)).astype(BF16)
        acc = acc + _dot(u, wdn_ref[cols, :])
    out = h1 + acc
    if gf_ref is not None:
        out = _rms(out, gf_ref[...])
    out_ref[...] = out


def _post_kernel(h_ref, a_ref, wo_ref, g_ref, wup_ref, wdn_ref, *rest):
    gf_ref, out_ref = (rest[0], rest[1]) if len(rest) == 2 else (None, rest[0])
    h1 = h_ref[...] + _dot(a_ref[...], wo_ref[...])
    _mlp_tail(h1, g_ref, wup_ref, wdn_ref, gf_ref, out_ref)


def _post_merge_kernel(h_ref, o0_ref, o1_ref, o2_ref, l0_ref, l1_ref, l2_ref,
                       wo_ref, g_ref, wup_ref, wdn_ref, out_ref, stage_ref):
    rows, width = o0_ref.shape

    def token_rows(x_ref, slot, dil):
        if dil == 1:
            return x_ref[...]
        for r in range(dil):
            for t in range(width // LANES):
                src = r * width + t * LANES
                stage_ref[slot, t, pl.ds(r, rows // dil, stride=dil), :] = x_ref[:, src:src + LANES]
        return jnp.concatenate([stage_ref[slot, t] for t in range(width // LANES)], axis=1)

    dils = [dil for _, dil in DIL_PAIRS]
    l0, l1, l2 = (token_rows(ref, slot, dil) for slot, (ref, dil) in enumerate(zip((l0_ref, l1_ref, l2_ref), dils)))
    o0, o1, o2 = (token_rows(ref, 3 + slot, dil) for slot, (ref, dil) in enumerate(zip((o0_ref, o1_ref, o2_ref), dils)))
    mx = jnp.maximum(jnp.maximum(l0, l1), l2)
    e0, e1, e2 = jnp.exp(l0 - mx), jnp.exp(l1 - mx), jnp.exp(l2 - mx)
    merged = (e0 * o0 + e1 * o1 + e2 * o2) / (e0 + e1 + e2)
    h1 = h_ref[...] + _dot(merged.astype(BF16), wo_ref[...])
    _mlp_tail(h1, g_ref, wup_ref, wdn_ref, None, out_ref)


def _post(h, attn_ins, w_o, g, w_up, w_down, g_final=None):
    m, d = h.shape
    row = lambda width: pl.BlockSpec((ROW_TILE, width), lambda i: (i, 0))
    merge = len(attn_ins) > 1
    in_specs = [row(d)] + [pl.BlockSpec((ROW_TILE * a.shape[0] // m, a.shape[1]), lambda i: (i, 0)) for a in attn_ins]
    in_specs += [_resident(w_o.shape), _resident((1, d)), _resident(w_up.shape), _resident(w_down.shape)]
    args = [h, *attn_ins, w_o, g.reshape(1, d), w_up, w_down]
    if g_final is not None:
        in_specs.append(_resident((1, d)))
        args.append(g_final.reshape(1, d))
    scratch = [pltpu.VMEM((len(attn_ins), w_o.shape[0] // LANES, ROW_TILE, LANES), F32)] if merge else []
    return pl.pallas_call(
        _post_merge_kernel if merge else _post_kernel,
        grid=(m // ROW_TILE,),
        in_specs=in_specs,
        out_specs=row(d),
        out_shape=jax.ShapeDtypeStruct((m, d), F32),
        scratch_shapes=scratch,
        compiler_params=_params(("parallel",)),
        name="post_mlp",
    )(*args)


ONES_ROWS = 16
SKIP_SHIFT = -2.0 * NEG_INF


class _SoftmaxState:
    def __init__(self, m_ref, acc_ref, dv):
        self.m_ref, self.acc_ref, self.dv = m_ref, acc_ref, dv

    @staticmethod
    def scratch(n_heads, dv, tq):
        return [pltpu.VMEM((n_heads, 1, tq), F32), pltpu.VMEM((n_heads, dv + ONES_ROWS, tq), F32)]

    @staticmethod
    def store_values_t(vt_ref, hh, cols, v_t):
        dv = v_t.shape[0]
        vt_ref[hh, :dv, cols] = v_t.astype(BF16)
        vt_ref[hh, dv:, cols] = jnp.ones((ONES_ROWS, v_t.shape[1]), BF16)

    def init(self):
        self.m_ref[...] = jnp.full(self.m_ref.shape, NEG_INF, F32)
        self.acc_ref[...] = jnp.zeros(self.acc_ref.shape, F32)

    def step(self, hh, x, v_t, shift=None):
        m = self.m_ref[hh]
        m_cur = jnp.max(x, axis=0, keepdims=True)
        if shift is not None:
            m_cur = m_cur - shift
        m_new = jnp.maximum(m, m_cur)
        alpha = jnp.exp2(m - m_new)
        p = jnp.exp2(x - (m_new if shift is None else m_new + shift))
        self.m_ref[hh] = m_new
        self.acc_ref[hh] = alpha * self.acc_ref[hh] + _dot(v_t, p.astype(BF16))

    def normalized(self, hh):
        return self.acc_ref[hh, :self.dv, :] / self.acc_ref[hh, self.dv:self.dv + 1, :]


def _sweep_blocks(n_past, n_chains, state, score_fn, values_fn, past_logits, own_steps, s_a, s_b):
    def produce(buf, block, chains=range(n_chains)):
        for c in chains:
            buf[c] = score_fn(c, block)

    def consume(buf, block):
        for c in range(n_chains):
            x, shift = past_logits(c, buf[c], block)
            state.step(c, x, values_fn(c, block), shift)

    produced = set()
    for buf, block, chains, _ in own_steps:
        produce(buf, block, [c for c in chains if (id(buf), c) not in produced])
        produced.update((id(buf), c) for c in chains)
    for c in range(n_chains):
        s_a[c] = score_fn(c, 0)
        for buf, block, chains, logits_fn in own_steps:
            if c in chains:
                x, shift = logits_fn(c, buf[c])
                state.step(c, x, values_fn(c, block), shift)

    def advance(src, block, dst, next_block):
        for c in range(n_chains):
            dst[c] = score_fn(c, next_block)
            x, shift = past_logits(c, src[c], block)
            state.step(c, x, values_fn(c, block), shift)

    def two_blocks(first):
        advance(s_a, first, s_b, first + 1)
        advance(s_b, first + 1, s_a, jnp.minimum(first + 2, n_past - 1))

    def quad(t, carry):
        two_blocks(4 * t)
        two_blocks(4 * t + 2)
        return carry

    def pair(t, carry):
        two_blocks(4 * (n_past // 4) + 2 * t)
        return carry

    lax.fori_loop(0, n_past // 4, quad, 0)
    lax.fori_loop(0, (n_past % 4) // 2, pair, 0)

    @pl.when(n_past % 2 == 1)
    def _():
        consume(s_a, n_past - 1)


def _transpose_bf16(x):
    return x.astype(F32).T.astype(BF16)


def _keep_head_rows(x_t, half):
    r = lax.broadcasted_iota(jnp.int32, x_t.shape, 0)
    keep = (r < HEAD_DIM) if half == 0 else (r >= HEAD_DIM)
    return jnp.where(keep, x_t, jnp.zeros_like(x_t))


def _bf16_pieces(x, n):
    pieces = []
    for _ in range(n):
        pieces.append(x.astype(BF16))
        x = x - pieces[-1].astype(F32)
    return pieces


F32_PIECES = 3


def _moba_kernel(slopes_ref, q_ref, k_ref, v_ref, o_ref, kmean_ref, vt_ref, term_ref, kpos_ref, qh_ref,
                 s_a, s_b, s_c, m_ref, acc_ref, *, n_blocks):
    blk = MOBA_BLOCK
    n_heads = MOBA_HEADS_PER_STEP
    group = pl.program_id(1)
    i = pl.program_id(2)
    state = _SoftmaxState(m_ref, acc_ref, HEAD_DIM)
    chain = lambda tile, hh: tile * n_heads + hh
    tile_chains = [[chain(tile, hh) for hh in range(n_heads)] for tile in range(MOBA_Q_TILES)]

    @pl.when(i == 0)
    def _():
        means = []
        pos = lax.broadcasted_iota(jnp.int32, (blk, LANES), 0)
        feature = lax.broadcasted_iota(jnp.int32, (blk, LANES), 1)
        key_pos = jnp.where(feature < F32_PIECES, pos, 0).astype(F32).astype(BF16)
        for j in range(n_blocks):
            rows = slice(j * blk, (j + 1) * blk)
            means.append(jnp.mean(k_ref[0, rows, :].astype(F32), axis=0, keepdims=True))
            for tile in range(n_heads // 2):
                kpos_ref[tile, rows, :LANES] = k_ref[0, rows, tile * LANES:(tile + 1) * LANES]
                kpos_ref[tile, rows, LANES:] = key_pos
            v_t = v_ref[0, rows, :].astype(F32).T
            for hh in range(n_heads):
                state.store_values_t(vt_ref, hh, rows, v_t[hh * HEAD_DIM:(hh + 1) * HEAD_DIM, :])
        for piece, part in enumerate(_bf16_pieces(jnp.concatenate(means, axis=0), F32_PIECES)):
            kmean_ref[piece * n_blocks:(piece + 1) * n_blocks, :] = part

    state.init()
    key = lax.broadcasted_iota(jnp.int32, (blk, blk), 0)
    qry = lax.broadcasted_iota(jnp.int32, (blk, blk), 1)
    blk_idx = lax.broadcasted_iota(jnp.int32, (n_blocks, blk), 0)
    feature_row = lax.broadcasted_iota(jnp.int32, (LANES, blk), 0)
    lane_tile = lambda hh: slice((hh // 2) * LANES, (hh // 2 + 1) * LANES)
    first_block = MOBA_Q_TILES * i

    for tile in range(MOBA_Q_TILES):
        own = first_block + tile
        q_t = q_ref[0, tile * blk:(tile + 1) * blk, :].astype(F32).T
        fully_past = blk_idx < own
        block_dist = ((own - blk_idx) * blk).astype(F32)
        for hh in range(n_heads):
            slope_s = slopes_ref[n_heads * group + hh] * LOG2E
            q_h = _keep_head_rows(q_t[lane_tile(hh), :], hh % 2).astype(BF16)
            pieces = _dot(kmean_ref[:, lane_tile(hh)], q_h)
            gate = pieces[:n_blocks] + pieces[n_blocks:2 * n_blocks] + pieces[2 * n_blocks:]
            work = jnp.where(fully_past, gate, NEG_INF * HEAD_Q_SCALE)
            chosen = jnp.zeros((n_blocks, blk), jnp.bool_)
            for _ in range(min(MOBA_TOPK, n_blocks)):
                best = jnp.max(work, axis=0, keepdims=True)
                first = jnp.min(jnp.where(work == best, blk_idx, n_blocks), axis=0, keepdims=True)
                pick = blk_idx == first
                chosen = jnp.logical_or(chosen, pick)
                work = jnp.where(pick, -jnp.inf, work)
            term_ref[chain(tile, hh)] = jnp.where(jnp.logical_and(chosen, fully_past), slope_s * block_dist,
                                                  SKIP_SHIFT)
            qh_ref[chain(tile, hh), :LANES, :] = q_h
            slope_rows = jnp.zeros((LANES, blk), F32)
            for piece, part in enumerate(_bf16_pieces(jnp.full((1, blk), slope_s, F32), F32_PIECES)):
                slope_rows = jnp.where(feature_row == piece, part.astype(F32), slope_rows)
            qh_ref[chain(tile, hh), LANES:, :] = slope_rows.astype(BF16)

    def rows_of(block):
        return pl.ds(pl.multiple_of(block * blk, blk), blk)

    def scores(c, block):
        return _dot(kpos_ref[(c % n_heads) // 2, rows_of(block), :], qh_ref[c])

    def v_t(c, block):
        return vt_ref[c % n_heads, :, rows_of(block)]

    def past_logits(c, s, block):
        return s, term_ref[c, pl.ds(block, 1), :]

    def own_logits(c, s):
        return jnp.where(key <= qry, s, NEG_INF), None

    assert MOBA_Q_TILES == 2
    own_steps = [(s_b, first_block, tile_chains[0], own_logits),
                 (s_b, first_block, tile_chains[1], functools.partial(past_logits, block=first_block)),
                 (s_c, first_block + 1, tile_chains[1], own_logits)]
    _sweep_blocks(first_block, MOBA_Q_TILES * n_heads, state, scores, v_t, past_logits, own_steps, s_a, s_b)
    for tile in range(MOBA_Q_TILES):
        o_t = jnp.concatenate([state.normalized(c) for c in tile_chains[tile]], axis=0)
        o_ref[0, tile * blk:(tile + 1) * blk, :] = o_t.T.astype(o_ref.dtype)


def _moba_attention(qkv, b, s):
    n_blocks = s // MOBA_BLOCK
    n_heads = MOBA_HEADS_PER_STEP
    width = n_heads * HEAD_DIM
    n_groups = MOBA_HEADS // n_heads
    n_chains = MOBA_Q_TILES * n_heads
    step_rows = MOBA_Q_TILES * MOBA_BLOCK
    assert s % step_rows == 0
    score_buf = pltpu.VMEM((n_chains, MOBA_BLOCK, MOBA_BLOCK), F32)
    return pl.pallas_call(
        functools.partial(_moba_kernel, n_blocks=n_blocks),
        grid=(b, n_groups, s // step_rows),
        in_specs=[pl.BlockSpec(memory_space=pltpu.SMEM),
                  pl.BlockSpec((1, step_rows, width), lambda bi, g, i: (bi, i, g)),
                  pl.BlockSpec((1, s, width), lambda bi, g, i: (bi, 0, n_groups + g)),
                  pl.BlockSpec((1, s, width), lambda bi, g, i: (bi, 0, 2 * n_groups + g))],
        out_specs=pl.BlockSpec((1, step_rows, width), lambda bi, g, i: (bi, i, g)),
        out_shape=jax.ShapeDtypeStruct((b, s, MOBA_HEADS * HEAD_DIM), BF16),
        scratch_shapes=[pltpu.VMEM((3 * n_blocks, width), BF16),
                        pltpu.VMEM((n_heads, HEAD_DIM + ONES_ROWS, s), BF16),
                        pltpu.VMEM((n_chains, n_blocks, MOBA_BLOCK), F32),
                        pltpu.VMEM((n_heads // 2, s, 2 * LANES), BF16),
                        pltpu.VMEM((n_chains, 2 * LANES, MOBA_BLOCK), BF16),
                        score_buf, score_buf, score_buf,
                        *_SoftmaxState.scratch(n_chains, HEAD_DIM, MOBA_BLOCK)],
        compiler_params=_params(("parallel", "parallel", "arbitrary")),
        name="moba_attention",
    )(_alibi_slopes(MOBA_HEADS), qkv, qkv, qkv)


def _lane_half(x, half):
    lane = lax.broadcasted_iota(jnp.int32, x.shape, x.ndim - 1)
    keep = (lane < HEAD_DIM) if half == 0 else (lane >= HEAD_DIM)
    return jnp.where(keep, x, jnp.zeros_like(x))


MAX_BANDS_PER_STEP = 8


def _bands_per_step(n_blocks):
    bands = MAX_BANDS_PER_STEP
    while n_blocks % bands:
        bands //= 2
    return bands


def _band_attention(n_bands, n_heads, q_tile, k_tile, v_tile, logits_fn, floor_fn, store_pair, s_ref, with_lse):
    chains = [(band, h) for band in range(n_bands) for h in range(n_heads)]

    def produce(c):
        band, h = chains[c]
        s_ref[c] = _dot_nt(k_tile(h, band), _lane_half(q_tile(h, band), h % 2))

    lookahead = 2
    for c in range(min(lookahead, len(chains))):
        produce(c)
    even = None
    for c, (band, h) in enumerate(chains):
        if c + lookahead < len(chains):
            produce(c + lookahead)
        x = logits_fn(h, band, s_ref[c])
        m = jnp.max(x, axis=0, keepdims=True)
        floor = floor_fn(h)
        if floor is not None:
            m = jnp.maximum(m, floor)
        p = jnp.exp2(x - m)
        den = jnp.sum(p, axis=0, keepdims=True)
        if floor is not None:
            den = den + jnp.exp2(floor - m)
        o_t = lax.dot_general(v_tile(h, band), p.astype(BF16), (((0,), (0,)), ((), ())),
                              preferred_element_type=F32)
        half = slice((h % 2) * HEAD_DIM, (h % 2 + 1) * HEAD_DIM)
        o_t = o_t[half, :] / den
        lse_t = jnp.broadcast_to((m + jnp.log2(den)) * (1.0 / LOG2E), o_t.shape) if with_lse else None
        if h % 2 == 0:
            even = (o_t, lse_t)
            continue
        store_pair(h // 2, band, jnp.concatenate([even[0], o_t], axis=0).T,
                   jnp.concatenate([even[1], lse_t], axis=0).T if with_lse else None)


def _band_rows(band, size):
    return slice(band * size, (band + 1) * size)


def _band_keys(prev_ref, cur_ref, band, size, cols):
    if band == 0:
        return jnp.concatenate([prev_ref[0, :, cols], cur_ref[0, :size, cols]], axis=0)
    return cur_ref[0, (band - 1) * size:(band + 1) * size, cols]


def _carry_previous_block(n, cur_refs, prev_refs, size):
    @pl.when(n == 0)
    def _():
        for prev in prev_refs:
            prev[...] = jnp.zeros(prev.shape, prev.dtype)

    def save():
        for cur, prev in zip(cur_refs, prev_refs):
            prev[0] = cur[0, cur.shape[1] - size:, :]
    return save


def _band_bias(bias_ref, first_step, n, size, n_heads, in_window, slope_fn):
    @pl.when(first_step)
    def _():
        key = lax.broadcasted_iota(jnp.int32, (2 * size, size), 0)
        qry = lax.broadcasted_iota(jnp.int32, (2 * size, size), 1)
        diff = qry + size - key
        window = in_window(diff)
        for h in range(n_heads):
            bias = slope_fn(h) * diff.astype(F32)
            bias_ref[h, 0] = jnp.where(window, bias, -NEG_INF)
            bias_ref[h, 1] = jnp.where(window & (key >= size), bias, -NEG_INF)

    def logits(h, band, s):
        variant = (n == 0).astype(jnp.int32) if band == 0 else 0
        return s - bias_ref[h, variant]
    return logits


def _dil_kernel(slopes_ref, qkv_ref, o_ref, lse_ref, s_ref, kp_ref, vp_ref, bias_ref, *, n_pts, dil, group):
    width = DIL_HEADS_PER_GROUP * HEAD_DIM
    q_ref, kc_ref, vc_ref = (qkv_ref.at[:, :, part * width:(part + 1) * width] for part in range(3))
    n = pl.program_id(2)
    first_step = (pl.program_id(0) == 0) & (pl.program_id(1) == 0) & (n == 0)
    save_last_block = _carry_previous_block(n, (kc_ref, vc_ref), (kp_ref, vp_ref), DIL_BAND)
    logits = _band_bias(bias_ref, first_step, n, DIL_BAND, DIL_HEADS_PER_GROUP,
                        lambda diff: (diff >= 0) & (diff <= n_pts),
                        lambda h: slopes_ref[group * DIL_HEADS_PER_GROUP + h] * (dil * LOG2E))
    tile = lambda h: slice((h // 2) * LANES, (h // 2 + 1) * LANES)

    def store_pair(pair, band, o, lse):
        rows, cols = _band_rows(band, DIL_BAND), slice(pair * LANES, (pair + 1) * LANES)
        o_ref[0, rows, cols] = o
        lse_ref[0, rows, cols] = lse

    _band_attention(
        qkv_ref.shape[1] // DIL_BAND, DIL_HEADS_PER_GROUP,
        lambda h, band: q_ref[0, _band_rows(band, DIL_BAND), tile(h)],
        lambda h, band: _band_keys(kp_ref, kc_ref, band, DIL_BAND, tile(h)),
        lambda h, band: _band_keys(vp_ref, vc_ref, band, DIL_BAND, tile(h)),
        logits, lambda h: None, store_pair, s_ref, with_lse=True)
    save_last_block()


def _dilated_group(qkv, b, s, group, window, dil):
    width = DIL_HEADS_PER_GROUP * HEAD_DIM
    n_sub = s // dil
    assert n_sub % DIL_BAND == 0
    n_bands = _bands_per_step(n_sub // DIL_BAND)
    step_rows = n_bands * DIL_BAND
    view = qkv.reshape(b, n_sub, dil * 3 * width)

    residue = lambda bi, r, n: (bi, n, r)
    previous_block = pltpu.VMEM((1, DIL_BAND, width), BF16)
    out_spec = pl.BlockSpec((1, step_rows, width), residue)
    out_sds = jax.ShapeDtypeStruct((b, n_sub, dil * width), F32)
    o, lse = pl.pallas_call(
        functools.partial(_dil_kernel, n_pts=window // dil, dil=dil, group=group),
        grid=(b, dil, n_sub // step_rows),
        in_specs=[pl.BlockSpec(memory_space=pltpu.SMEM),
                  pl.BlockSpec((1, step_rows, 3 * width), residue)],
        out_specs=[out_spec, out_spec],
        out_shape=[out_sds, out_sds],
        scratch_shapes=[pltpu.VMEM((n_bands * DIL_HEADS_PER_GROUP, 2 * DIL_BAND, DIL_BAND), F32),
                        previous_block, previous_block,
                        pltpu.VMEM((DIL_HEADS_PER_GROUP, 2, 2 * DIL_BAND, DIL_BAND), F32)],
        compiler_params=_params(("arbitrary", "arbitrary", "arbitrary")),
        name=f"dilated_attention_g{group}",
    )(_alibi_slopes(len(DIL_PAIRS) * DIL_HEADS_PER_GROUP), view)
    return o.reshape(b * n_sub, dil * width), lse.reshape(b * n_sub, dil * width)


def _mla_proj_kernel(h_ref, g_ref, wd_ref, gq_ref, gkv_ref, wuq_ref, wuk_ref, wuv_ref,
                     cos_ref, sin_up_ref, sin_dn_ref, q_ref, k_ref, v_ref):
    n = _rms(h_ref[...], g_ref[...]).astype(BF16)
    c = _dot(n, wd_ref[...])
    nq = _rms(c[:, :MLA_Q_RANK], gq_ref[...]).astype(BF16)
    nkv = _rms(c[:, MLA_Q_RANK:MLA_Q_RANK + MLA_KV_RANK], gkv_ref[...]).astype(BF16)
    k_rope = c[:, MLA_Q_RANK + MLA_KV_RANK:]
    half = MLA_ROPE // 2
    pair = 2

    def rope(x, width):
        cos, sin_up, sin_dn = (jnp.tile(t[...], (1, width)) for t in (cos_ref, sin_up_ref, sin_dn_ref))
        return x * cos + pltpu.roll(x, half, 1) * sin_up + pltpu.roll(x, x.shape[1] - half, 1) * sin_dn

    kr = jnp.tile(rope(k_rope, 1), (1, pair))
    for h in range(0, MLA_HEADS, pair):
        cols = slice(h * LANES, (h + pair) * LANES)
        q_ref[:, cols] = (rope(_dot(nq, wuq_ref[:, cols]), pair) * MLA_Q_SCALE).astype(BF16)
        k_ref[:, cols] = (_dot(nkv, wuk_ref[:, cols]) + kr).astype(BF16)
    v_ref[...] = _dot(nkv, wuv_ref[...]).astype(BF16)


def _mla_proj(h, g, w_dkv, q_norm, w_uq, kv_norm, w_ukv, s):
    m, d = h.shape
    qk = MLA_NOPE + MLA_ROPE
    wd = jnp.concatenate([w_dkv[:, :MLA_Q_RANK + MLA_KV_RANK],
                          jnp.zeros((d, MLA_NOPE), F32), w_dkv[:, MLA_Q_RANK + MLA_KV_RANK:],
                          jnp.zeros((d, LANES - qk), F32)], axis=1).astype(BF16)
    wuq = jnp.pad(w_uq.reshape(MLA_Q_RANK, MLA_HEADS, qk), ((0, 0), (0, 0), (0, LANES - qk)))
    wuq = wuq.reshape(MLA_Q_RANK, MLA_HEADS * LANES).astype(BF16)
    w_ukv = w_ukv.reshape(MLA_KV_RANK, MLA_HEADS, MLA_NOPE + MLA_V)
    wuk = jnp.pad(w_ukv[:, :, :MLA_NOPE], ((0, 0), (0, 0), (0, LANES - MLA_NOPE)))
    wuk = wuk.reshape(MLA_KV_RANK, MLA_HEADS * LANES).astype(BF16)
    wuv = w_ukv[:, :, MLA_NOPE:].reshape(MLA_KV_RANK, MLA_HEADS * MLA_V).astype(BF16)
    half = MLA_ROPE // 2
    inv = ROPE_THETA ** (-jnp.arange(0, MLA_ROPE, 2, dtype=F32) / MLA_ROPE)
    ang = jnp.arange(s).astype(F32)[:, None] * inv[None, :]
    cos, sin = jnp.cos(ang), jnp.sin(ang)
    zeros = lambda w: jnp.zeros((s, w), F32)
    cos_t = jnp.concatenate([jnp.ones((s, MLA_NOPE), F32), cos, cos, zeros(LANES - qk)], axis=1)
    sin_up = jnp.concatenate([zeros(MLA_NOPE + half), sin, zeros(LANES - qk)], axis=1)
    sin_dn = jnp.concatenate([zeros(MLA_NOPE), -sin, zeros(half + LANES - qk)], axis=1)

    tiles_per_seq = s // ROW_TILE
    row = lambda width: pl.BlockSpec((ROW_TILE, width), lambda i: (i, 0))
    table = pl.BlockSpec((ROW_TILE, LANES), lambda i: (i % tiles_per_seq, 0))
    return pl.pallas_call(
        _mla_proj_kernel,
        grid=(m // ROW_TILE,),
        in_specs=[row(d), _resident((1, d)), _resident(wd.shape),
                  _resident((1, MLA_Q_RANK)), _resident((1, MLA_KV_RANK)),
                  _resident(wuq.shape), _resident(wuk.shape), _resident(wuv.shape),
                  table, table, table],
        out_specs=[row(MLA_HEADS * LANES), row(MLA_HEADS * LANES), row(MLA_HEADS * MLA_V)],
        out_shape=[jax.ShapeDtypeStruct((m, MLA_HEADS * LANES), BF16),
                   jax.ShapeDtypeStruct((m, MLA_HEADS * LANES), BF16),
                   jax.ShapeDtypeStruct((m, MLA_HEADS * MLA_V), BF16)],
        compiler_params=_params(("parallel",)),
        name="mla_proj",
    )(h, g.reshape(1, d), wd, q_norm.reshape(1, -1), kv_norm.reshape(1, -1), wuq, wuk, wuv,
      cos_t, sin_up, sin_dn)


def _mla_kernel(q_ref, k_ref, v_ref, o_ref, vt_ref, qh_ref, s_a, s_b, s_c, m_ref, acc_ref, *, n_tiles):
    t = MLA_TILE
    n_heads = MLA_HEADS_PER_STEP
    i = pl.program_id(2)
    state = _SoftmaxState(m_ref, acc_ref, MLA_V)
    chain = lambda tile, hh: tile * n_heads + hh
    tile_chains = [[chain(tile, hh) for hh in range(n_heads)] for tile in range(MLA_Q_TILES)]

    @pl.when(i == 0)
    def _():
        for j in range(n_tiles):
            rows = slice(j * t, (j + 1) * t)
            v_t = v_ref[0, rows, :].astype(F32).T
            for hh in range(n_heads):
                state.store_values_t(vt_ref, hh, rows, v_t[hh * MLA_V:(hh + 1) * MLA_V, :])

    state.init()
    key = lax.broadcasted_iota(jnp.int32, (t, t), 0)
    qry = lax.broadcasted_iota(jnp.int32, (t, t), 1)
    lane_tile = lambda hh: slice(hh * LANES, (hh + 1) * LANES)
    for tile in range(MLA_Q_TILES):
        for hh in range(n_heads):
            qh_ref[chain(tile, hh)] = _transpose_bf16(q_ref[0, tile * t:(tile + 1) * t, lane_tile(hh)])

    def rows_of(block):
        return pl.ds(pl.multiple_of(block * t, t), t)

    def scores(c, block):
        return _dot(k_ref[0, rows_of(block), lane_tile(c % n_heads)], qh_ref[c])

    def v_t(c, block):
        return vt_ref[c % n_heads, :, rows_of(block)]

    def past_logits(c, s, block):
        return s, None

    def causal(c, s):
        return jnp.where(key <= qry, s, NEG_INF), None

    def full(c, s):
        return s, None

    assert MLA_Q_TILES == 2
    first = MLA_Q_TILES * i
    own_steps = [(s_b, first, tile_chains[0], causal), (s_b, first, tile_chains[1], full),
                 (s_c, first + 1, tile_chains[1], causal)]
    _sweep_blocks(first, MLA_Q_TILES * n_heads, state, scores, v_t, past_logits, own_steps, s_a, s_b)
    for tile in range(MLA_Q_TILES):
        o_t = jnp.concatenate([state.normalized(c) for c in tile_chains[tile]], axis=0)
        o_ref[0, tile * t:(tile + 1) * t, :] = o_t.T.astype(o_ref.dtype)


def _mla_attention(q, k, v, b, s):
    n_heads = MLA_HEADS_PER_STEP
    n_groups = MLA_HEADS // n_heads
    whole_seq = lambda bi, g, i: (bi, 0, g)
    tile = lambda bi, g, i: (bi, i, g)
    n_chains = MLA_Q_TILES * n_heads
    step_rows = MLA_Q_TILES * MLA_TILE
    assert s % step_rows == 0
    score_buf = pltpu.VMEM((n_chains, MLA_TILE, MLA_TILE), F32)
    return pl.pallas_call(
        functools.partial(_mla_kernel, n_tiles=s // MLA_TILE),
        grid=(b, n_groups, s // step_rows),
        in_specs=[pl.BlockSpec((1, step_rows, n_heads * LANES), tile),
                  pl.BlockSpec((1, s, n_heads * LANES), whole_seq),
                  pl.BlockSpec((1, s, n_heads * MLA_V), whole_seq)],
        out_specs=pl.BlockSpec((1, step_rows, n_heads * MLA_V), tile),
        out_shape=jax.ShapeDtypeStruct((b, s, MLA_HEADS * MLA_V), BF16),
        scratch_shapes=[pltpu.VMEM((n_heads, MLA_V + ONES_ROWS, s), BF16),
                        pltpu.VMEM((n_chains, LANES, MLA_TILE), BF16),
                        score_buf, score_buf, score_buf,
                        *_SoftmaxState.scratch(n_chains, MLA_V, MLA_TILE)],
        compiler_params=_params(("parallel", "parallel", "arbitrary")),
        name="mla_attention",
    )(q, k, v)


def _swa_kernel(slopes_ref, sinks_ref, qkv_ref, o_ref, s_ref, kp_ref, vp_ref, bias_ref):
    q_width, kv_width = SWA_Q_HEADS * HEAD_DIM, SWA_KV_HEADS * HEAD_DIM
    q_ref = qkv_ref.at[:, :, :q_width]
    kc_ref = qkv_ref.at[:, :, q_width:q_width + kv_width]
    vc_ref = qkv_ref.at[:, :, q_width + kv_width:]
    n = pl.program_id(1)
    first_step = (pl.program_id(0) == 0) & (n == 0)
    save_last_block = _carry_previous_block(n, (kc_ref, vc_ref), (kp_ref, vp_ref), SWA_BLOCK)
    group = SWA_Q_HEADS // SWA_KV_HEADS
    logits = _band_bias(bias_ref, first_step, n, SWA_BLOCK, SWA_Q_HEADS,
                        lambda diff: (diff >= 0) & (diff < SWA_WINDOW), lambda h: slopes_ref[h] * LOG2E)

    def both_halves(prev_ref, cur_ref, band, kv_head):
        x = _band_keys(prev_ref, cur_ref, band, SWA_BLOCK, slice(None))
        part = x[:, kv_head * HEAD_DIM:(kv_head + 1) * HEAD_DIM]
        return jnp.concatenate([part, part], axis=1)

    n_bands = qkv_ref.shape[1] // SWA_BLOCK
    bands_kv = [(band, kv) for band in range(n_bands) for kv in range(SWA_KV_HEADS)]
    k_tiles = {bk: both_halves(kp_ref, kc_ref, *bk) for bk in bands_kv}
    v_tiles = {bk: both_halves(vp_ref, vc_ref, *bk) for bk in bands_kv}

    def sink(h):
        return sinks_ref[h] * LOG2E

    def store_pair(pair, band, o, lse):
        o_ref[0, _band_rows(band, SWA_BLOCK), pair * LANES:(pair + 1) * LANES] = o.astype(o_ref.dtype)

    _band_attention(
        n_bands, SWA_Q_HEADS,
        lambda h, band: q_ref[0, _band_rows(band, SWA_BLOCK), (h // 2) * LANES:(h // 2 + 1) * LANES],
        lambda h, band: k_tiles[band, h // group],
        lambda h, band: v_tiles[band, h // group],
        logits, sink, store_pair, s_ref, with_lse=False)
    save_last_block()


def _swa_attention(qkv, sinks, b, s):
    q_width = SWA_Q_HEADS * HEAD_DIM
    kv_width = SWA_KV_HEADS * HEAD_DIM
    assert s % SWA_BLOCK == 0
    n_bands = _bands_per_step(s // SWA_BLOCK)
    step_rows = n_bands * SWA_BLOCK
    rows = lambda bi, n: (bi, n, 0)
    previous_block = pltpu.VMEM((1, SWA_BLOCK, kv_width), BF16)
    return pl.pallas_call(
        _swa_kernel,
        grid=(b, s // step_rows),
        in_specs=[pl.BlockSpec(memory_space=pltpu.SMEM), pl.BlockSpec(memory_space=pltpu.SMEM),
                  pl.BlockSpec((1, step_rows, qkv.shape[-1]), rows)],
        out_specs=pl.BlockSpec((1, step_rows, q_width), rows),
        out_shape=jax.ShapeDtypeStruct((b, s, q_width), BF16),
        scratch_shapes=[pltpu.VMEM((n_bands * SWA_Q_HEADS, 2 * SWA_BLOCK, SWA_BLOCK), F32),
                        previous_block, previous_block,
                        pltpu.VMEM((SWA_Q_HEADS, 2, 2 * SWA_BLOCK, SWA_BLOCK), F32)],
        compiler_params=_params(("arbitrary", "arbitrary")),
        name="swa_attention",
    )(_alibi_slopes(SWA_Q_HEADS), sinks.astype(F32), qkv)


def kernel(x, l0_attn_norm, l0_w_qkv, l0_w_o, l0_mlp_norm, l0_w_up, l0_w_down, l1_attn_norm, l1_w_qkv, l1_w_o, l1_mlp_norm, l1_w_up, l1_w_down, l2_attn_norm, l2_w_dkv, l2_q_norm, l2_w_uq, l2_kv_norm, l2_w_ukv, l2_w_o, l2_mlp_norm, l2_w_up, l2_w_down, l3_attn_norm, l3_w_qkv, l3_sinks, l3_w_o, l3_mlp_norm, l3_w_up, l3_w_down, final_norm):
    b, s, d = x.shape
    bf = lambda w: w.astype(BF16)
    h = x.reshape(b * s, d)

    qkv = _norm_proj(h, l0_attn_norm, bf(l0_w_qkv), scaled_cols=MOBA_HEADS * HEAD_DIM, scale=HEAD_Q_SCALE)
    a = _moba_attention(qkv.reshape(b, s, -1), b, s).reshape(b * s, -1)
    h = _post(h, (a,), bf(l0_w_o), l0_mlp_norm, bf(l0_w_up), bf(l0_w_down))

    qkvs = _dil_proj(h, l1_attn_norm, bf(l1_w_qkv))
    groups = [_dilated_group(qkvs[g], b, s, g, window, dil) for g, (window, dil) in enumerate(DIL_PAIRS)]
    h = _post(h, tuple(o for o, _ in groups) + tuple(l for _, l in groups),
              bf(l1_w_o), l1_mlp_norm, bf(l1_w_up), bf(l1_w_down))

    q, k, v = _mla_proj(h, l2_attn_norm, l2_w_dkv, l2_q_norm, l2_w_uq, l2_kv_norm, l2_w_ukv, s)
    a = _mla_attention(q.reshape(b, s, -1), k.reshape(b, s, -1), v.reshape(b, s, -1), b, s).reshape(b * s, -1)
    h = _post(h, (a,), bf(l2_w_o), l2_mlp_norm, bf(l2_w_up), bf(l2_w_down))

    qkv = _norm_proj(h, l3_attn_norm, bf(l3_w_qkv), col_tile=256, scaled_cols=SWA_Q_HEADS * HEAD_DIM,
                     scale=HEAD_Q_SCALE)
    a = _swa_attention(qkv.reshape(b, s, -1), l3_sinks, b, s).reshape(b * s, -1)
    h = _post(h, (a,), bf(l3_w_o), l3_mlp_norm, bf(l3_w_up), bf(l3_w_down), g_final=final_norm)
    return h.reshape(b, s, d)
```

```python
import functools

import jax
import jax.numpy as jnp
from jax import lax
from jax.experimental import pallas as pl
from jax.experimental.pallas import tpu as pltpu

F32 = jnp.float32
BF16 = jnp.bfloat16

D_MODEL = 1024
HEAD_DIM = 64
RMS_EPS = 1e-6
D_FF = 4 * D_MODEL
NEG_INF = -1e30

MOBA_HEADS = 16
MOBA_BLOCK = 256
MOBA_TOPK = 3

DIL_PAIRS = ((128, 1), (512, 4), (2048, 16))
DIL_HEADS_PER_GROUP = 8
DIL_BAND = 128

MLA_HEADS = 16
MLA_Q_RANK = 768
MLA_KV_RANK = 256
MLA_NOPE = 64
MLA_ROPE = 32
MLA_V = 64
ROPE_THETA = 10000.0
MLA_TILE = 256
MLA_Q_TILES = 2
MLA_HEADS_PER_STEP = 8
MOBA_HEADS_PER_STEP = 8
MOBA_Q_TILES = 2

SWA_Q_HEADS = 16
SWA_KV_HEADS = 2
SWA_WINDOW = 128
SWA_BLOCK = 128

LANES = 128
ROW_TILE = 512
FF_TILE = 512
VMEM_LIMIT = 56 * 1024 * 1024
LOG2E = 1.4426950408889634
HEAD_Q_SCALE = HEAD_DIM ** -0.5 * LOG2E
MLA_Q_SCALE = (MLA_NOPE + MLA_ROPE) ** -0.5 * LOG2E


def _params(semantics):
    return pltpu.CompilerParams(dimension_semantics=semantics, vmem_limit_bytes=VMEM_LIMIT)


def _resident(shape):
    return pl.BlockSpec(shape, lambda *_: (0,) * len(shape), pipeline_mode=pl.Buffered(1))


def _alibi_slopes(n_heads):
    return 2.0 ** (-8.0 * jnp.arange(1, n_heads + 1, dtype=F32) / n_heads)


def _rms(x, g):
    return x * lax.rsqrt(jnp.mean(x * x, axis=-1, keepdims=True) + RMS_EPS) * g


def _dot(a, b):
    return jnp.dot(a, b, preferred_element_type=F32)


def _dot_nt(a, b):
    return lax.dot_general(a, b, (((1,), (1,)), ((), ())), preferred_element_type=F32)


def _norm_proj_kernel(h_ref, g_ref, w_ref, o_ref, *, col_tile, scaled_cols, scale):
    n = _rms(h_ref[...], g_ref[...]).astype(BF16)
    for c in range(o_ref.shape[1] // col_tile):
        cols = slice(c * col_tile, (c + 1) * col_tile)
        y = _dot(n, w_ref[:, cols])
        if (c + 1) * col_tile <= scaled_cols:
            y = y * scale
        o_ref[:, cols] = y.astype(o_ref.dtype)


def _norm_proj(h, g, w, col_tile=512, scaled_cols=0, scale=1.0):
    m, d = h.shape
    n_out = w.shape[1]
    assert scaled_cols % col_tile == 0
    return pl.pallas_call(
        functools.partial(_norm_proj_kernel, col_tile=col_tile, scaled_cols=scaled_cols, scale=scale),
        grid=(m // ROW_TILE,),
        in_specs=[pl.BlockSpec((ROW_TILE, d), lambda i: (i, 0)),
                  _resident((1, d)),
                  _resident((d, n_out))],
        out_specs=pl.BlockSpec((ROW_TILE, n_out), lambda i: (i, 0)),
        out_shape=jax.ShapeDtypeStruct((m, n_out), BF16),
        compiler_params=_params(("parallel",)),
        name="norm_proj",
    )(h, g.reshape(1, d), w)


def _dil_proj_kernel(h_ref, g_ref, w_ref, o0_ref, o1_ref, o2_ref, stage_ref, sorted_ref):
    n = _rms(h_ref[...], g_ref[...])
    rows, d = n.shape
    width = DIL_HEADS_PER_GROUP * HEAD_DIM
    for t in range(d // LANES):
        stage_ref[t] = n[:, t * LANES:(t + 1) * LANES]
    n = n.astype(BF16)
    for g, (o_ref, (_, dil)) in enumerate(zip((o0_ref, o1_ref, o2_ref), DIL_PAIRS)):
        per = rows // dil
        for r in range(dil if dil > 1 else 0):
            for t in range(d // LANES):
                sorted_ref[g - 1, r * per:(r + 1) * per, t * LANES:(t + 1) * LANES] = (
                    stage_ref[t, pl.ds(r, per, stride=dil), :].astype(BF16))
        lhs = n if dil == 1 else sorted_ref[g - 1]
        for part in range(3):
            src = (3 * g + part) * width
            y = _dot(lhs, w_ref[:, src:src + width])
            if part == 0:
                y = y * HEAD_Q_SCALE
            for r in range(dil):
                dst = (3 * r + part) * width
                o_ref[:, dst:dst + width] = y[r * per:(r + 1) * per, :].astype(BF16)


def _dil_proj(h, g, w):
    m, d = h.shape
    feat = 3 * DIL_HEADS_PER_GROUP * HEAD_DIM
    assert DIL_PAIRS[0][1] == 1
    view = lambda dil: (m // dil, dil * feat)
    view_block = lambda dil: pl.BlockSpec((ROW_TILE // dil, dil * feat), lambda i: (i, 0))
    return pl.pallas_call(
        _dil_proj_kernel,
        grid=(m // ROW_TILE,),
        in_specs=[pl.BlockSpec((ROW_TILE, d), lambda i: (i, 0)), _resident((1, d)), _resident(w.shape)],
        out_specs=[view_block(dil) for _, dil in DIL_PAIRS],
        out_shape=[jax.ShapeDtypeStruct(view(dil), BF16) for _, dil in DIL_PAIRS],
        scratch_shapes=[pltpu.VMEM((d // LANES, ROW_TILE, LANES), F32),
                        pltpu.VMEM((len(DIL_PAIRS) - 1, ROW_TILE, d), BF16)],
        compiler_params=_params(("parallel",)),
        name="dil_proj",
    )(h, g.reshape(1, d), w)


def _mlp_tail(h1, g_ref, wup_ref, wdn_ref, gf_ref, out_ref):
    n = _rms(h1, g_ref[...]).astype(BF16)
    acc = jnp.zeros_like(h1)
    for c in range(wup_ref.shape[1] // FF_TILE):
        cols = slice(c * FF_TILE, (c + 1) * FF_TILE)
        u = jnp.square(jnp.maximum(_dot(n, wup_ref[:, cols]), 0.0)).astype(BF16)
        acc = acc + _dot(u, wdn_ref[cols, :])
    out = h1 + acc
    if gf_ref is not None:
        out = _rms(out, gf_ref[...])
    out_ref[...] = out


def _post_kernel(h_ref, a_ref, wo_ref, g_ref, wup_ref, wdn_ref, *rest):
    gf_ref, out_ref = (rest[0], rest[1]) if len(rest) == 2 else (None, rest[0])
    h1 = h_ref[...] + _dot(a_ref[...], wo_ref[...])
    _mlp_tail(h1, g_ref, wup_ref, wdn_ref, gf_ref, out_ref)


def _post_merge_kernel(h_ref, o0_ref, o1_ref, o2_ref, l0_ref, l1_ref, l2_ref,
                       wo_ref, g_ref, wup_ref, wdn_ref, out_ref, stage_ref):
    rows, width = o0_ref.shape

    def token_rows(x_ref, slot, dil):
        if dil == 1:
            return x_ref[...]
        for r in range(dil):
            for t in range(width // LANES):
                src = r * width + t * LANES
                stage_ref[slot, t, pl.ds(r, rows // dil, stride=dil), :] = x_ref[:, src:src + LANES]
        return jnp.concatenate([stage_ref[slot, t] for t in range(width // LANES)], axis=1)

    dils = [dil for _, dil in DIL_PAIRS]
    l0, l1, l2 = (token_rows(ref, slot, dil) for slot, (ref, dil) in enumerate(zip((l0_ref, l1_ref, l2_ref), dils)))
    o0, o1, o2 = (token_rows(ref, 3 + slot, dil) for slot, (ref, dil) in enumerate(zip((o0_ref, o1_ref, o2_ref), dils)))
    mx = jnp.maximum(jnp.maximum(l0, l1), l2)
    e0, e1, e2 = jnp.exp(l0 - mx), jnp.exp(l1 - mx), jnp.exp(l2 - mx)
    merged = (e0 * o0 + e1 * o1 + e2 * o2) / (e0 + e1 + e2)
    h1 = h_ref[...] + _dot(merged.astype(BF16), wo_ref[...])
    _mlp_tail(h1, g_ref, wup_ref, wdn_ref, None, out_ref)


def _post(h, attn_ins, w_o, g, w_up, w_down, g_final=None):
    m, d = h.shape
    row = lambda width: pl.BlockSpec((ROW_TILE, width), lambda i: (i, 0))
    merge = len(attn_ins) > 1
    in_specs = [row(d)] + [pl.BlockSpec((ROW_TILE * a.shape[0] // m, a.shape[1]), lambda i: (i, 0)) for a in attn_ins]
    in_specs += [_resident(w_o.shape), _resident((1, d)), _resident(w_up.shape), _resident(w_down.shape)]
    args = [h, *attn_ins, w_o, g.reshape(1, d), w_up, w_down]
    if g_final is not None:
        in_specs.append(_resident((1, d)))
        args.append(g_final.reshape(1, d))
    scratch = [pltpu.VMEM((len(attn_ins), w_o.shape[0] // LANES, ROW_TILE, LANES), F32)] if merge else []
    return pl.pallas_call(
        _post_merge_kernel if merge else _post_kernel,
        grid=(m // ROW_TILE,),
        in_specs=in_specs,
        out_specs=row(d),
        out_shape=jax.ShapeDtypeStruct((m, d), F32),
        scratch_shapes=scratch,
        compiler_params=_params(("parallel",)),
        name="post_mlp",
    )(*args)


ONES_ROWS = 16
SKIP_SHIFT = -2.0 * NEG_INF


class _SoftmaxState:
    def __init__(self, m_ref, acc_ref, dv):
        self.m_ref, self.acc_ref, self.dv = m_ref, acc_ref, dv

    @staticmethod
    def scratch(n_heads, dv, tq):
        return [pltpu.VMEM((n_heads, 1, tq), F32), pltpu.VMEM((n_heads, dv + ONES_ROWS, tq), F32)]

    @staticmethod
    def store_values_t(vt_ref, hh, cols, v_t):
        dv = v_t.shape[0]
        vt_ref[hh, :dv, cols] = v_t.astype(BF16)
        vt_ref[hh, dv:, cols] = jnp.ones((ONES_ROWS, v_t.shape[1]), BF16)

    def init(self):
        self.m_ref[...] = jnp.full(self.m_ref.shape, NEG_INF, F32)
        self.acc_ref[...] = jnp.zeros(self.acc_ref.shape, F32)

    def step(self, hh, x, v_t, shift=None):
        m = self.m_ref[hh]
        m_cur = jnp.max(x, axis=0, keepdims=True)
        if shift is not None:
            m_cur = m_cur - shift
        m_new = jnp.maximum(m, m_cur)
        alpha = jnp.exp2(m - m_new)
        p = jnp.exp2(x - (m_new if shift is None else m_new + shift))
        self.m_ref[hh] = m_new
        self.acc_ref[hh] = alpha * self.acc_ref[hh] + _dot(v_t, p.astype(BF16))

    def normalized(self, hh):
        return self.acc_ref[hh, :self.dv, :] / self.acc_ref[hh, self.dv:self.dv + 1, :]


def _sweep_blocks(n_past, n_chains, state, score_fn, values_fn, past_logits, own_steps, s_a, s_b):
    def produce(buf, block, chains=range(n_chains)):
        for c in chains:
            buf[c] = score_fn(c, block)

    def consume(buf, block):
        for c in range(n_chains):
            x, shift = past_logits(c, buf[c], block)
            state.step(c, x, values_fn(c, block), shift)

    produced = set()
    for buf, block, chains, _ in own_steps:
        produce(buf, block, [c for c in chains if (id(buf), c) not in produced])
        produced.update((id(buf), c) for c in chains)
    for c in range(n_chains):
        s_a[c] = score_fn(c, 0)
        for buf, block, chains, logits_fn in own_steps:
            if c in chains:
                x, shift = logits_fn(c, buf[c])
                state.step(c, x, values_fn(c, block), shift)

    def advance(src, block, dst, next_block):
        for c in range(n_chains):
            dst[c] = score_fn(c, next_block)
            x, shift = past_logits(c, src[c], block)
            state.step(c, x, values_fn(c, block), shift)

    def two_blocks(first):
        advance(s_a, first, s_b, first + 1)
        advance(s_b, first + 1, s_a, jnp.minimum(first + 2, n_past - 1))

    def quad(t, carry):
        two_blocks(4 * t)
        two_blocks(4 * t + 2)
        return carry

    def pair(t, carry):
        two_blocks(4 * (n_past // 4) + 2 * t)
        return carry

    lax.fori_loop(0, n_past // 4, quad, 0)
    lax.fori_loop(0, (n_past % 4) // 2, pair, 0)

    @pl.when(n_past % 2 == 1)
    def _():
        consume(s_a, n_past - 1)


def _transpose_bf16(x):
    return x.astype(F32).T.astype(BF16)


def _keep_head_rows(x_t, half):
    r = lax.broadcasted_iota(jnp.int32, x_t.shape, 0)
    keep = (r < HEAD_DIM) if half == 0 else (r >= HEAD_DIM)
    return jnp.where(keep, x_t, jnp.zeros_like(x_t))


def _bf16_pieces(x, n):
    pieces = []
    for _ in range(n):
        pieces.append(x.astype(BF16))
        x = x - pieces[-1].astype(F32)
    return pieces


F32_PIECES = 3


def _moba_kernel(slopes_ref, q_ref, k_ref, v_ref, o_ref, kmean_ref, vt_ref, term_ref, kpos_ref, qh_ref,
                 s_a, s_b, s_c, m_ref, acc_ref, *, n_blocks):
    blk = MOBA_BLOCK
    n_heads = MOBA_HEADS_PER_STEP
    group = pl.program_id(1)
    i = pl.program_id(2)
    state = _SoftmaxState(m_ref, acc_ref, HEAD_DIM)
    chain = lambda tile, hh: tile * n_heads + hh
    tile_chains = [[chain(tile, hh) for hh in range(n_heads)] for tile in range(MOBA_Q_TILES)]

    @pl.when(i == 0)
    def _():
        means = []
        pos = lax.broadcasted_iota(jnp.int32, (blk, LANES), 0)
        feature = lax.broadcasted_iota(jnp.int32, (blk, LANES), 1)
        key_pos = jnp.where(feature < F32_PIECES, pos, 0).astype(F32).astype(BF16)
        for j in range(n_blocks):
            rows = slice(j * blk, (j + 1) * blk)
            means.append(jnp.mean(k_ref[0, rows, :].astype(F32), axis=0, keepdims=True))
            for tile in range(n_heads // 2):
                kpos_ref[tile, rows, :LANES] = k_ref[0, rows, tile * LANES:(tile + 1) * LANES]
                kpos_ref[tile, rows, LANES:] = key_pos
            v_t = v_ref[0, rows, :].astype(F32).T
            for hh in range(n_heads):
                state.store_values_t(vt_ref, hh, rows, v_t[hh * HEAD_DIM:(hh + 1) * HEAD_DIM, :])
        for piece, part in enumerate(_bf16_pieces(jnp.concatenate(means, axis=0), F32_PIECES)):
            kmean_ref[piece * n_blocks:(piece + 1) * n_blocks, :] = part

    state.init()
    key = lax.broadcasted_iota(jnp.int32, (blk, blk), 0)
    qry = lax.broadcasted_iota(jnp.int32, (blk, blk), 1)
    blk_idx = lax.broadcasted_iota(jnp.int32, (n_blocks, blk), 0)
    feature_row = lax.broadcasted_iota(jnp.int32, (LANES, blk), 0)
    lane_tile = lambda hh: slice((hh // 2) * LANES, (hh // 2 + 1) * LANES)
    first_block = MOBA_Q_TILES * i

    for tile in range(MOBA_Q_TILES):
        own = first_block + tile
        q_t = q_ref[0, tile * blk:(tile + 1) * blk, :].astype(F32).T
        fully_past = blk_idx < own
        block_dist = ((own - blk_idx) * blk).astype(F32)
        for hh in range(n_heads):
            slope_s = slopes_ref[n_heads * group + hh] * LOG2E
            q_h = _keep_head_rows(q_t[lane_tile(hh), :], hh % 2).astype(BF16)
            pieces = _dot(kmean_ref[:, lane_tile(hh)], q_h)
            gate = pieces[:n_blocks] + pieces[n_blocks:2 * n_blocks] + pieces[2 * n_blocks:]
            work = jnp.where(fully_past, gate, NEG_INF * HEAD_Q_SCALE)
            chosen = jnp.zeros((n_blocks, blk), jnp.bool_)
            for _ in range(min(MOBA_TOPK, n_blocks)):
                best = jnp.max(work, axis=0, keepdims=True)
                first = jnp.min(jnp.where(work == best, blk_idx, n_blocks), axis=0, keepdims=True)
                pick = blk_idx == first
                chosen = jnp.logical_or(chosen, pick)
                work = jnp.where(pick, -jnp.inf, work)
            term_ref[chain(tile, hh)] = jnp.where(jnp.logical_and(chosen, fully_past), slope_s * block_dist,
                                                  SKIP_SHIFT)
            qh_ref[chain(tile, hh), :LANES, :] = q_h
            slope_rows = jnp.zeros((LANES, blk), F32)
            for piece, part in enumerate(_bf16_pieces(jnp.full((1, blk), slope_s, F32), F32_PIECES)):
                slope_rows = jnp.where(feature_row == piece, part.astype(F32), slope_rows)
            qh_ref[chain(tile, hh), LANES:, :] = slope_rows.astype(BF16)

    def rows_of(block):
        return pl.ds(pl.multiple_of(block * blk, blk), blk)

    def scores(c, block):
        return _dot(kpos_ref[(c % n_heads) // 2, rows_of(block), :], qh_ref[c])

    def v_t(c, block):
        return vt_ref[c % n_heads, :, rows_of(block)]

    def past_logits(c, s, block):
        return s, term_ref[c, pl.ds(block, 1), :]

    def own_logits(c, s):
        return jnp.where(key <= qry, s, NEG_INF), None

    assert MOBA_Q_TILES == 2
    own_steps = [(s_b, first_block, tile_chains[0], own_logits),
                 (s_b, first_block, tile_chains[1], functools.partial(past_logits, block=first_block)),
                 (s_c, first_block + 1, tile_chains[1], own_logits)]
    _sweep_blocks(first_block, MOBA_Q_TILES * n_heads, state, scores, v_t, past_logits, own_steps, s_a, s_b)
    for tile in range(MOBA_Q_TILES):
        o_t = jnp.concatenate([state.normalized(c) for c in tile_chains[tile]], axis=0)
        o_ref[0, tile * blk:(tile + 1) * blk, :] = o_t.T.astype(o_ref.dtype)


def _moba_attention(qkv, b, s):
    n_blocks = s // MOBA_BLOCK
    n_heads = MOBA_HEADS_PER_STEP
    width = n_heads * HEAD_DIM
    n_groups = MOBA_HEADS // n_heads
    n_chains = MOBA_Q_TILES * n_heads
    step_rows = MOBA_Q_TILES * MOBA_BLOCK
    assert s % step_rows == 0
    score_buf = pltpu.VMEM((n_chains, MOBA_BLOCK, MOBA_BLOCK), F32)
    return pl.pallas_call(
        functools.partial(_moba_kernel, n_blocks=n_blocks),
        grid=(b, n_groups, s // step_rows),
        in_specs=[pl.BlockSpec(memory_space=pltpu.SMEM),
                  pl.BlockSpec((1, step_rows, width), lambda bi, g, i: (bi, i, g)),
                  pl.BlockSpec((1, s, width), lambda bi, g, i: (bi, 0, n_groups + g)),
                  pl.BlockSpec((1, s, width), lambda bi, g, i: (bi, 0, 2 * n_groups + g))],
        out_specs=pl.BlockSpec((1, step_rows, width), lambda bi, g, i: (bi, i, g)),
        out_shape=jax.ShapeDtypeStruct((b, s, MOBA_HEADS * HEAD_DIM), BF16),
        scratch_shapes=[pltpu.VMEM((F32_PIECES * n_blocks, width), BF16),
                        pltpu.VMEM((n_heads, HEAD_DIM + ONES_ROWS, s), BF16),
                        pltpu.VMEM((n_chains, n_blocks, MOBA_BLOCK), F32),
                        pltpu.VMEM((n_heads // 2, s, 2 * LANES), BF16),
                        pltpu.VMEM((n_chains, 2 * LANES, MOBA_BLOCK), BF16),
                        score_buf, score_buf, score_buf,
                        *_SoftmaxState.scratch(n_chains, HEAD_DIM, MOBA_BLOCK)],
        compiler_params=_params(("parallel", "parallel", "arbitrary")),
        name="moba_attention",
    )(_alibi_slopes(MOBA_HEADS), qkv, qkv, qkv)


def _lane_half(x, half):
    lane = lax.broadcasted_iota(jnp.int32, x.shape, x.ndim - 1)
    keep = (lane < HEAD_DIM) if half == 0 else (lane >= HEAD_DIM)
    return jnp.where(keep, x, jnp.zeros_like(x))


MAX_BANDS_PER_STEP = 8


def _bands_per_step(n_blocks):
    bands = MAX_BANDS_PER_STEP
    while n_blocks % bands:
        bands //= 2
    return bands


def _band_attention(n_bands, n_heads, q_tile, k_tile, v_tile, logits_fn, floor_fn, store_pair, s_ref, with_lse):
    chains = [(band, h) for band in range(n_bands) for h in range(n_heads)]

    def produce(c):
        band, h = chains[c]
        s_ref[c] = _dot_nt(k_tile(h, band), _lane_half(q_tile(h, band), h % 2))

    lookahead = 2
    for c in range(min(lookahead, len(chains))):
        produce(c)
    even = None
    for c, (band, h) in enumerate(chains):
        if c + lookahead < len(chains):
            produce(c + lookahead)
        x = logits_fn(h, band, s_ref[c])
        m = jnp.max(x, axis=0, keepdims=True)
        floor = floor_fn(h)
        if floor is not None:
            m = jnp.maximum(m, floor)
        p = jnp.exp2(x - m)
        den = jnp.sum(p, axis=0, keepdims=True)
        if floor is not None:
            den = den + jnp.exp2(floor - m)
        o_t = lax.dot_general(v_tile(h, band), p.astype(BF16), (((0,), (0,)), ((), ())),
                              preferred_element_type=F32)
        half = slice((h % 2) * HEAD_DIM, (h % 2 + 1) * HEAD_DIM)
        o_t = o_t[half, :] / den
        lse_t = jnp.broadcast_to((m + jnp.log2(den)) * (1.0 / LOG2E), o_t.shape) if with_lse else None
        if h % 2 == 0:
            even = (o_t, lse_t)
            continue
        store_pair(h // 2, band, jnp.concatenate([even[0], o_t], axis=0).T,
                   jnp.concatenate([even[1], lse_t], axis=0).T if with_lse else None)


def _band_rows(band, size):
    return slice(band * size, (band + 1) * size)


def _band_keys(prev_ref, cur_ref, band, size, cols):
    if band == 0:
        return jnp.concatenate([prev_ref[0, :, cols], cur_ref[0, :size, cols]], axis=0)
    return cur_ref[0, (band - 1) * size:(band + 1) * size, cols]


def _carry_previous_block(n, cur_refs, prev_refs, size):
    @pl.when(n == 0)
    def _():
        for prev in prev_refs:
            prev[...] = jnp.zeros(prev.shape, prev.dtype)

    def save():
        for cur, prev in zip(cur_refs, prev_refs):
            prev[0] = cur[0, cur.shape[1] - size:, :]
    return save


def _band_bias(bias_ref, first_step, n, size, n_heads, in_window, slope_fn):
    @pl.when(first_step)
    def _():
        key = lax.broadcasted_iota(jnp.int32, (2 * size, size), 0)
        qry = lax.broadcasted_iota(jnp.int32, (2 * size, size), 1)
        diff = qry + size - key
        window = in_window(diff)
        for h in range(n_heads):
            bias = slope_fn(h) * diff.astype(F32)
            bias_ref[h, 0] = jnp.where(window, bias, -NEG_INF)
            bias_ref[h, 1] = jnp.where(window & (key >= size), bias, -NEG_INF)

    def logits(h, band, s):
        variant = (n == 0).astype(jnp.int32) if band == 0 else 0
        return s - bias_ref[h, variant]
    return logits


def _dil_kernel(slopes_ref, qkv_ref, o_ref, lse_ref, s_ref, kp_ref, vp_ref, bias_ref, *, n_pts, dil, group):
    width = DIL_HEADS_PER_GROUP * HEAD_DIM
    n_bands = qkv_ref.shape[1] // DIL_BAND
    n_residues = qkv_ref.shape[2] // (3 * width)
    part_ref = lambda res, part: qkv_ref.at[:, :, (3 * res + part) * width:(3 * res + part + 1) * width]
    q_refs, k_refs, v_refs = ([part_ref(res, part) for res in range(n_residues)] for part in range(3))
    kp_refs = [kp_ref.at[res:res + 1] for res in range(n_residues)]
    vp_refs = [vp_ref.at[res:res + 1] for res in range(n_residues)]
    n = pl.program_id(2)
    first_step = (pl.program_id(0) == 0) & (pl.program_id(1) == 0) & (n == 0)
    save_last_blocks = _carry_previous_block(n, k_refs + v_refs, kp_refs + vp_refs, DIL_BAND)
    band_logits = _band_bias(bias_ref, first_step, n, DIL_BAND, DIL_HEADS_PER_GROUP,
                             lambda diff: (diff >= 0) & (diff <= n_pts),
                             lambda h: slopes_ref[group * DIL_HEADS_PER_GROUP + h] * (dil * LOG2E))
    tile = lambda h: slice((h // 2) * LANES, (h // 2 + 1) * LANES)
    unit = lambda u: divmod(u, n_bands)

    def store_pair(pair, u, o, lse):
        res, band = unit(u)
        rows, cols = _band_rows(band, DIL_BAND), slice(res * width + pair * LANES, res * width + (pair + 1) * LANES)
        o_ref[0, rows, cols] = o
        lse_ref[0, rows, cols] = lse

    _band_attention(
        n_residues * n_bands, DIL_HEADS_PER_GROUP,
        lambda h, u: q_refs[unit(u)[0]][0, _band_rows(unit(u)[1], DIL_BAND), tile(h)],
        lambda h, u: _band_keys(kp_refs[unit(u)[0]], k_refs[unit(u)[0]], unit(u)[1], DIL_BAND, tile(h)),
        lambda h, u: _band_keys(vp_refs[unit(u)[0]], v_refs[unit(u)[0]], unit(u)[1], DIL_BAND, tile(h)),
        lambda h, u, s: band_logits(h, unit(u)[1], s), lambda h: None, store_pair, s_ref, with_lse=True)
    save_last_blocks()


def _dilated_group(qkv, b, s, group, window, dil):
    width = DIL_HEADS_PER_GROUP * HEAD_DIM
    n_sub = s // dil
    assert n_sub % DIL_BAND == 0
    n_bands = _bands_per_step(n_sub // DIL_BAND)
    step_rows = n_bands * DIL_BAND
    n_residues = min(dil, MAX_BANDS_PER_STEP // n_bands)
    assert dil % n_residues == 0
    view = qkv.reshape(b, n_sub, dil * 3 * width)

    residues = lambda bi, r, n: (bi, n, r)
    previous_block = pltpu.VMEM((n_residues, DIL_BAND, width), BF16)
    out_spec = pl.BlockSpec((1, step_rows, n_residues * width), residues)
    out_sds = jax.ShapeDtypeStruct((b, n_sub, dil * width), F32)
    o, lse = pl.pallas_call(
        functools.partial(_dil_kernel, n_pts=window // dil, dil=dil, group=group),
        grid=(b, dil // n_residues, n_sub // step_rows),
        in_specs=[pl.BlockSpec(memory_space=pltpu.SMEM),
                  pl.BlockSpec((1, step_rows, n_residues * 3 * width), residues)],
        out_specs=[out_spec, out_spec],
        out_shape=[out_sds, out_sds],
        scratch_shapes=[pltpu.VMEM((n_residues * n_bands * DIL_HEADS_PER_GROUP, 2 * DIL_BAND, DIL_BAND), F32),
                        previous_block, previous_block,
                        pltpu.VMEM((DIL_HEADS_PER_GROUP, 2, 2 * DIL_BAND, DIL_BAND), F32)],
        compiler_params=_params(("arbitrary", "arbitrary", "arbitrary")),
        name=f"dilated_attention_g{group}",
    )(_alibi_slopes(len(DIL_PAIRS) * DIL_HEADS_PER_GROUP), view)
    return o.reshape(b * n_sub, dil * width), lse.reshape(b * n_sub, dil * width)


def _mla_proj_kernel(h_ref, g_ref, wd_ref, gq_ref, gkv_ref, wuq_ref, wuk_ref, wuv_ref,
                     cos_ref, sin_up_ref, sin_dn_ref, q_ref, k_ref, v_ref):
    n = _rms(h_ref[...], g_ref[...]).astype(BF16)
    c = _dot(n, wd_ref[...])
    nq = _rms(c[:, :MLA_Q_RANK], gq_ref[...]).astype(BF16)
    nkv = _rms(c[:, MLA_Q_RANK:MLA_Q_RANK + MLA_KV_RANK], gkv_ref[...]).astype(BF16)
    k_rope = c[:, MLA_Q_RANK + MLA_KV_RANK:]
    half = MLA_ROPE // 2
    pair = 2

    def rope(x, width):
        cos, sin_up, sin_dn = (jnp.tile(t[...], (1, width)) for t in (cos_ref, sin_up_ref, sin_dn_ref))
        return x * cos + pltpu.roll(x, half, 1) * sin_up + pltpu.roll(x, x.shape[1] - half, 1) * sin_dn

    kr = jnp.tile(rope(k_rope, 1), (1, pair))
    for h in range(0, MLA_HEADS, pair):
        cols = slice(h * LANES, (h + pair) * LANES)
        q_ref[:, cols] = (rope(_dot(nq, wuq_ref[:, cols]), pair) * MLA_Q_SCALE).astype(BF16)
        k_ref[:, cols] = (_dot(nkv, wuk_ref[:, cols]) + kr).astype(BF16)
    v_ref[...] = _dot(nkv, wuv_ref[...]).astype(BF16)


def _mla_proj(h, g, w_dkv, q_norm, w_uq, kv_norm, w_ukv, s):
    m, d = h.shape
    qk = MLA_NOPE + MLA_ROPE
    wd = jnp.concatenate([w_dkv[:, :MLA_Q_RANK + MLA_KV_RANK],
                          jnp.zeros((d, MLA_NOPE), F32), w_dkv[:, MLA_Q_RANK + MLA_KV_RANK:],
                          jnp.zeros((d, LANES - qk), F32)], axis=1).astype(BF16)
    wuq = jnp.pad(w_uq.reshape(MLA_Q_RANK, MLA_HEADS, qk), ((0, 0), (0, 0), (0, LANES - qk)))
    wuq = wuq.reshape(MLA_Q_RANK, MLA_HEADS * LANES).astype(BF16)
    w_ukv = w_ukv.reshape(MLA_KV_RANK, MLA_HEADS, MLA_NOPE + MLA_V)
    wuk = jnp.pad(w_ukv[:, :, :MLA_NOPE], ((0, 0), (0, 0), (0, LANES - MLA_NOPE)))
    wuk = wuk.reshape(MLA_KV_RANK, MLA_HEADS * LANES).astype(BF16)
    wuv = w_ukv[:, :, MLA_NOPE:].reshape(MLA_KV_RANK, MLA_HEADS * MLA_V).astype(BF16)
    half = MLA_ROPE // 2
    inv = ROPE_THETA ** (-jnp.arange(0, MLA_ROPE, 2, dtype=F32) / MLA_ROPE)
    ang = jnp.arange(s).astype(F32)[:, None] * inv[None, :]
    cos, sin = jnp.cos(ang), jnp.sin(ang)
    zeros = lambda w: jnp.zeros((s, w), F32)
    cos_t = jnp.concatenate([jnp.ones((s, MLA_NOPE), F32), cos, cos, zeros(LANES - qk)], axis=1)
    sin_up = jnp.concatenate([zeros(MLA_NOPE + half), sin, zeros(LANES - qk)], axis=1)
    sin_dn = jnp.concatenate([zeros(MLA_NOPE), -sin, zeros(half + LANES - qk)], axis=1)

    tiles_per_seq = s // ROW_TILE
    row = lambda width: pl.BlockSpec((ROW_TILE, width), lambda i: (i, 0))
    table = pl.BlockSpec((ROW_TILE, LANES), lambda i: (i % tiles_per_seq, 0))
    return pl.pallas_call(
        _mla_proj_kernel,
        grid=(m // ROW_TILE,),
        in_specs=[row(d), _resident((1, d)), _resident(wd.shape),
                  _resident((1, MLA_Q_RANK)), _resident((1, MLA_KV_RANK)),
                  _resident(wuq.shape), _resident(wuk.shape), _resident(wuv.shape),
                  table, table, table],
        out_specs=[row(MLA_HEADS * LANES), row(MLA_HEADS * LANES), row(MLA_HEADS * MLA_V)],
        out_shape=[jax.ShapeDtypeStruct((m, MLA_HEADS * LANES), BF16),
                   jax.ShapeDtypeStruct((m, MLA_HEADS * LANES), BF16),
                   jax.ShapeDtypeStruct((m, MLA_HEADS * MLA_V), BF16)],
        compiler_params=_params(("parallel",)),
        name="mla_proj",
    )(h, g.reshape(1, d), wd, q_norm.reshape(1, -1), kv_norm.reshape(1, -1), wuq, wuk, wuv,
      cos_t, sin_up, sin_dn)


def _mla_kernel(q_ref, k_ref, v_ref, o_ref, vt_ref, qh_ref, s_a, s_b, s_c, m_ref, acc_ref, *, n_tiles):
    t = MLA_TILE
    n_heads = MLA_HEADS_PER_STEP
    i = pl.program_id(2)
    state = _SoftmaxState(m_ref, acc_ref, MLA_V)
    chain = lambda tile, hh: tile * n_heads + hh
    tile_chains = [[chain(tile, hh) for hh in range(n_heads)] for tile in range(MLA_Q_TILES)]

    @pl.when(i == 0)
    def _():
        for j in range(n_tiles):
            rows = slice(j * t, (j + 1) * t)
            v_t = v_ref[0, rows, :].astype(F32).T
            for hh in range(n_heads):
                state.store_values_t(vt_ref, hh, rows, v_t[hh * MLA_V:(hh + 1) * MLA_V, :])

    state.init()
    key = lax.broadcasted_iota(jnp.int32, (t, t), 0)
    qry = lax.broadcasted_iota(jnp.int32, (t, t), 1)
    lane_tile = lambda hh: slice(hh * LANES, (hh + 1) * LANES)
    for tile in range(MLA_Q_TILES):
        for hh in range(n_heads):
            qh_ref[chain(tile, hh)] = _transpose_bf16(q_ref[0, tile * t:(tile + 1) * t, lane_tile(hh)])

    def rows_of(block):
        return pl.ds(pl.multiple_of(block * t, t), t)

    def scores(c, block):
        return _dot(k_ref[0, rows_of(block), lane_tile(c % n_heads)], qh_ref[c])

    def v_t(c, block):
        return vt_ref[c % n_heads, :, rows_of(block)]

    def past_logits(c, s, block):
        return s, None

    def causal(c, s):
        return jnp.where(key <= qry, s, NEG_INF), None

    def full(c, s):
        return s, None

    assert MLA_Q_TILES == 2
    first = MLA_Q_TILES * i
    own_steps = [(s_b, first, tile_chains[0], causal), (s_b, first, tile_chains[1], full),
                 (s_c, first + 1, tile_chains[1], causal)]
    _sweep_blocks(first, MLA_Q_TILES * n_heads, state, scores, v_t, past_logits, own_steps, s_a, s_b)
    for tile in range(MLA_Q_TILES):
        o_t = jnp.concatenate([state.normalized(c) for c in tile_chains[tile]], axis=0)
        o_ref[0, tile * t:(tile + 1) * t, :] = o_t.T.astype(o_ref.dtype)


def _mla_attention(q, k, v, b, s):
    n_heads = MLA_HEADS_PER_STEP
    n_groups = MLA_HEADS // n_heads
    whole_seq = lambda bi, g, i: (bi, 0, g)
    tile = lambda bi, g, i: (bi, i, g)
    n_chains = MLA_Q_TILES * n_heads
    step_rows = MLA_Q_TILES * MLA_TILE
    assert s % step_rows == 0
    score_buf = pltpu.VMEM((n_chains, MLA_TILE, MLA_TILE), F32)
    return pl.pallas_call(
        functools.partial(_mla_kernel, n_tiles=s // MLA_TILE),
        grid=(b, n_groups, s // step_rows),
        in_specs=[pl.BlockSpec((1, step_rows, n_heads * LANES), tile),
                  pl.BlockSpec((1, s, n_heads * LANES), whole_seq),
                  pl.BlockSpec((1, s, n_heads * MLA_V), whole_seq)],
        out_specs=pl.BlockSpec((1, step_rows, n_heads * MLA_V), tile),
        out_shape=jax.ShapeDtypeStruct((b, s, MLA_HEADS * MLA_V), BF16),
        scratch_shapes=[pltpu.VMEM((n_heads, MLA_V + ONES_ROWS, s), BF16),
                        pltpu.VMEM((n_chains, LANES, MLA_TILE), BF16),
                        score_buf, score_buf, score_buf,
                        *_SoftmaxState.scratch(n_chains, MLA_V, MLA_TILE)],
        compiler_params=_params(("parallel", "parallel", "arbitrary")),
        name="mla_attention",
    )(q, k, v)


def _swa_kernel(slopes_ref, sinks_ref, qkv_ref, o_ref, s_ref, kp_ref, vp_ref, bias_ref):
    q_width, kv_width = SWA_Q_HEADS * HEAD_DIM, SWA_KV_HEADS * HEAD_DIM
    q_ref = qkv_ref.at[:, :, :q_width]
    kc_ref = qkv_ref.at[:, :, q_width:q_width + kv_width]
    vc_ref = qkv_ref.at[:, :, q_width + kv_width:]
    n = pl.program_id(1)
    first_step = (pl.program_id(0) == 0) & (n == 0)
    save_last_block = _carry_previous_block(n, (kc_ref, vc_ref), (kp_ref, vp_ref), SWA_BLOCK)
    group = SWA_Q_HEADS // SWA_KV_HEADS
    logits = _band_bias(bias_ref, first_step, n, SWA_BLOCK, SWA_Q_HEADS,
                        lambda diff: (diff >= 0) & (diff < SWA_WINDOW), lambda h: slopes_ref[h] * LOG2E)

    def both_halves(prev_ref, cur_ref, band, kv_head):
        x = _band_keys(prev_ref, cur_ref, band, SWA_BLOCK, slice(None))
        part = x[:, kv_head * HEAD_DIM:(kv_head + 1) * HEAD_DIM]
        return jnp.concatenate([part, part], axis=1)

    n_bands = qkv_ref.shape[1] // SWA_BLOCK
    bands_kv = [(band, kv) for band in range(n_bands) for kv in range(SWA_KV_HEADS)]
    k_tiles = {bk: both_halves(kp_ref, kc_ref, *bk) for bk in bands_kv}
    v_tiles = {bk: both_halves(vp_ref, vc_ref, *bk) for bk in bands_kv}

    def sink(h):
        return sinks_ref[h] * LOG2E

    def store_pair(pair, band, o, lse):
        o_ref[0, _band_rows(band, SWA_BLOCK), pair * LANES:(pair + 1) * LANES] = o.astype(o_ref.dtype)

    _band_attention(
        n_bands, SWA_Q_HEADS,
        lambda h, band: q_ref[0, _band_rows(band, SWA_BLOCK), (h // 2) * LANES:(h // 2 + 1) * LANES],
        lambda h, band: k_tiles[band, h // group],
        lambda h, band: v_tiles[band, h // group],
        logits, sink, store_pair, s_ref, with_lse=False)
    save_last_block()


def _swa_attention(qkv, sinks, b, s):
    q_width = SWA_Q_HEADS * HEAD_DIM
    kv_width = SWA_KV_HEADS * HEAD_DIM
    assert s % SWA_BLOCK == 0
    n_bands = _bands_per_step(s // SWA_BLOCK)
    step_rows = n_bands * SWA_BLOCK
    rows = lambda bi, n: (bi, n, 0)
    previous_block = pltpu.VMEM((1, SWA_BLOCK, kv_width), BF16)
    return pl.pallas_call(
        _swa_kernel,
        grid=(b, s // step_rows),
        in_specs=[pl.BlockSpec(memory_space=pltpu.SMEM), pl.BlockSpec(memory_space=pltpu.SMEM),
                  pl.BlockSpec((1, step_rows, qkv.shape[-1]), rows)],
        out_specs=pl.BlockSpec((1, step_rows, q_width), rows),
        out_shape=jax.ShapeDtypeStruct((b, s, q_width), BF16),
        scratch_shapes=[pltpu.VMEM((n_bands * SWA_Q_HEADS, 2 * SWA_BLOCK, SWA_BLOCK), F32),
                        previous_block, previous_block,
                        pltpu.VMEM((SWA_Q_HEADS, 2, 2 * SWA_BLOCK, SWA_BLOCK), F32)],
        compiler_params=_params(("arbitrary", "arbitrary")),
        name="swa_attention",
    )(_alibi_slopes(SWA_Q_HEADS), sinks.astype(F32), qkv)


def kernel(x, l0_attn_norm, l0_w_qkv, l0_w_o, l0_mlp_norm, l0_w_up, l0_w_down, l1_attn_norm, l1_w_qkv, l1_w_o, l1_mlp_norm, l1_w_up, l1_w_down, l2_attn_norm, l2_w_dkv, l2_q_norm, l2_w_uq, l2_kv_norm, l2_w_ukv, l2_w_o, l2_mlp_norm, l2_w_up, l2_w_down, l3_attn_norm, l3_w_qkv, l3_sinks, l3_w_o, l3_mlp_norm, l3_w_up, l3_w_down, final_norm):
    b, s, d = x.shape
    bf = lambda w: w.astype(BF16)
    h = x.reshape(b * s, d)

    qkv = _norm_proj(h, l0_attn_norm, bf(l0_w_qkv), scaled_cols=MOBA_HEADS * HEAD_DIM, scale=HEAD_Q_SCALE)
    a = _moba_attention(qkv.reshape(b, s, -1), b, s).reshape(b * s, -1)
    h = _post(h, (a,), bf(l0_w_o), l0_mlp_norm, bf(l0_w_up), bf(l0_w_down))

    qkvs = _dil_proj(h, l1_attn_norm, bf(l1_w_qkv))
    groups = [_dilated_group(qkvs[g], b, s, g, window, dil) for g, (window, dil) in enumerate(DIL_PAIRS)]
    h = _post(h, tuple(o for o, _ in groups) + tuple(l for _, l in groups),
              bf(l1_w_o), l1_mlp_norm, bf(l1_w_up), bf(l1_w_down))

    q, k, v = _mla_proj(h, l2_attn_norm, l2_w_dkv, l2_q_norm, l2_w_uq, l2_kv_norm, l2_w_ukv, s)
    a = _mla_attention(q.reshape(b, s, -1), k.reshape(b, s, -1), v.reshape(b, s, -1), b, s).reshape(b * s, -1)
    h = _post(h, (a,), bf(l2_w_o), l2_mlp_norm, bf(l2_w_up), bf(l2_w_down))

    qkv = _norm_proj(h, l3_attn_norm, bf(l3_w_qkv), col_tile=256, scaled_cols=SWA_Q_HEADS * HEAD_DIM,
                     scale=HEAD_Q_SCALE)
    a = _swa_attention(qkv.reshape(b, s, -1), l3_sinks, b, s).reshape(b * s, -1)
    h = _post(h, (a,), bf(l3_w_o), l3_mlp_norm, bf(l3_w_up), bf(l3_w_down), g_final=final_norm)
    return h.reshape(b, s, d)
```

```python
import functools

import jax
import jax.numpy as jnp
from jax import lax
from jax.experimental import pallas as pl
from jax.experimental.pallas import tpu as pltpu

F32 = jnp.float32
BF16 = jnp.bfloat16

D_MODEL = 1024
HEAD_DIM = 64
RMS_EPS = 1e-6
D_FF = 4 * D_MODEL
NEG_INF = -1e30

MOBA_HEADS = 16
MOBA_BLOCK = 256
MOBA_TOPK = 3

DIL_PAIRS = ((128, 1), (512, 4), (2048, 16))
DIL_HEADS_PER_GROUP = 8
DIL_BAND = 128

MLA_HEADS = 16
MLA_Q_RANK = 768
MLA_KV_RANK = 256
MLA_NOPE = 64
MLA_ROPE = 32
MLA_V = 64
ROPE_THETA = 10000.0
MLA_TILE = 256
MLA_Q_TILES = 2
MLA_HEADS_PER_STEP = 8
MOBA_HEADS_PER_STEP = 8
MOBA_Q_TILES = 2

SWA_Q_HEADS = 16
SWA_KV_HEADS = 2
SWA_WINDOW = 128
SWA_BLOCK = 128

LANES = 128
ROW_TILE = 512
PROJ_ROW_TILE = 1024
FF_TILE = 512
VMEM_LIMIT = 56 * 1024 * 1024
LOG2E = 1.4426950408889634
HEAD_Q_SCALE = HEAD_DIM ** -0.5 * LOG2E
MLA_Q_SCALE = (MLA_NOPE + MLA_ROPE) ** -0.5 * LOG2E


def _params(semantics):
    return pltpu.CompilerParams(dimension_semantics=semantics, vmem_limit_bytes=VMEM_LIMIT)


def _resident(shape):
    return pl.BlockSpec(shape, lambda *_: (0,) * len(shape), pipeline_mode=pl.Buffered(1))


def _alibi_slopes(n_heads):
    return 2.0 ** (-8.0 * jnp.arange(1, n_heads + 1, dtype=F32) / n_heads)


def _rms(x, g):
    return x * lax.rsqrt(jnp.mean(x * x, axis=-1, keepdims=True) + RMS_EPS) * g


def _dot(a, b):
    return jnp.dot(a, b, preferred_element_type=F32)


def _dot_nt(a, b):
    return lax.dot_general(a, b, (((1,), (1,)), ((), ())), preferred_element_type=F32)


def _norm_proj_kernel(h_ref, g_ref, w_ref, o_ref, *, col_tile, scaled_cols, scale):
    n = _rms(h_ref[...], g_ref[...]).astype(BF16)
    for c in range(o_ref.shape[1] // col_tile):
        cols = slice(c * col_tile, (c + 1) * col_tile)
        y = _dot(n, w_ref[:, cols])
        if (c + 1) * col_tile <= scaled_cols:
            y = y * scale
        o_ref[:, cols] = y.astype(o_ref.dtype)


def _norm_proj(h, g, w, col_tile=512, scaled_cols=0, scale=1.0):
    m, d = h.shape
    n_out = w.shape[1]
    assert scaled_cols % col_tile == 0
    return pl.pallas_call(
        functools.partial(_norm_proj_kernel, col_tile=col_tile, scaled_cols=scaled_cols, scale=scale),
        grid=(m // PROJ_ROW_TILE,),
        in_specs=[pl.BlockSpec((PROJ_ROW_TILE, d), lambda i: (i, 0)),
                  _resident((1, d)),
                  _resident((d, n_out))],
        out_specs=pl.BlockSpec((PROJ_ROW_TILE, n_out), lambda i: (i, 0)),
        out_shape=jax.ShapeDtypeStruct((m, n_out), BF16),
        compiler_params=_params(("parallel",)),
        name="norm_proj",
    )(h, g.reshape(1, d), w)


def _dil_proj_kernel(h_ref, g_ref, w_ref, o0_ref, o1_ref, o2_ref, stage_ref, sorted_ref):
    n = _rms(h_ref[...], g_ref[...])
    rows, d = n.shape
    width = DIL_HEADS_PER_GROUP * HEAD_DIM
    for t in range(d // LANES):
        stage_ref[t] = n[:, t * LANES:(t + 1) * LANES]
    n = n.astype(BF16)
    for g, (o_ref, (_, dil)) in enumerate(zip((o0_ref, o1_ref, o2_ref), DIL_PAIRS)):
        per = rows // dil
        for r in range(dil if dil > 1 else 0):
            for t in range(d // LANES):
                sorted_ref[g - 1, r * per:(r + 1) * per, t * LANES:(t + 1) * LANES] = (
                    stage_ref[t, pl.ds(r, per, stride=dil), :].astype(BF16))
        lhs = n if dil == 1 else sorted_ref[g - 1]
        for part in range(3):
            src = (3 * g + part) * width
            y = _dot(lhs, w_ref[:, src:src + width])
            if part == 0:
                y = y * HEAD_Q_SCALE
            for r in range(dil):
                dst = (3 * r + part) * width
                o_ref[:, dst:dst + width] = y[r * per:(r + 1) * per, :].astype(BF16)


def _dil_proj(h, g, w):
    m, d = h.shape
    feat = 3 * DIL_HEADS_PER_GROUP * HEAD_DIM
    assert DIL_PAIRS[0][1] == 1
    view = lambda dil: (m // dil, dil * feat)
    view_block = lambda dil: pl.BlockSpec((PROJ_ROW_TILE // dil, dil * feat), lambda i: (i, 0))
    return pl.pallas_call(
        _dil_proj_kernel,
        grid=(m // PROJ_ROW_TILE,),
        in_specs=[pl.BlockSpec((PROJ_ROW_TILE, d), lambda i: (i, 0)), _resident((1, d)), _resident(w.shape)],
        out_specs=[view_block(dil) for _, dil in DIL_PAIRS],
        out_shape=[jax.ShapeDtypeStruct(view(dil), BF16) for _, dil in DIL_PAIRS],
        scratch_shapes=[pltpu.VMEM((d // LANES, PROJ_ROW_TILE, LANES), F32),
                        pltpu.VMEM((len(DIL_PAIRS) - 1, PROJ_ROW_TILE, d), BF16)],
        compiler_params=_params(("parallel",)),
        name="dil_proj",
    )(h, g.reshape(1, d), w)


def _mlp_tail(h1, g_ref, wup_ref, wdn_ref, gf_ref, out_ref):
    n = _rms(h1, g_ref[...]).astype(BF16)
    acc = jnp.zeros_like(h1)
    for c in range(wup_ref.shape[1] // FF_TILE):
        cols = slice(c * FF_TILE, (c + 1) * FF_TILE)
        u = jnp.square(jnp.maximum(_dot(n, wup_ref[:, cols]), 0.0)).astype(BF16)
        acc = acc + _dot(u, wdn_ref[cols, :])
    out = h1 + acc
    if gf_ref is not None:
        out = _rms(out, gf_ref[...])
    out_ref[...] = out


def _post_kernel(h_ref, a_ref, wo_ref, g_ref, wup_ref, wdn_ref, *rest):
    gf_ref, out_ref = (rest[0], rest[1]) if len(rest) == 2 else (None, rest[0])
    h1 = h_ref[...] + _dot(a_ref[...], wo_ref[...])
    _mlp_tail(h1, g_ref, wup_ref, wdn_ref, gf_ref, out_ref)


def _post_merge_kernel(h_ref, o0_ref, o1_ref, o2_ref, l0_ref, l1_ref, l2_ref,
                       wo_ref, g_ref, wup_ref, wdn_ref, out_ref, stage_ref):
    rows, width = o0_ref.shape

    def token_rows(x_ref, slot, dil):
        if dil == 1:
            return x_ref[...]
        for r in range(dil):
            for t in range(width // LANES):
                src = r * width + t * LANES
                stage_ref[slot, t, pl.ds(r, rows // dil, stride=dil), :] = x_ref[:, src:src + LANES]
        return jnp.concatenate([stage_ref[slot, t] for t in range(width // LANES)], axis=1)

    dils = [dil for _, dil in DIL_PAIRS]
    l0, l1, l2 = (token_rows(ref, slot, dil) for slot, (ref, dil) in enumerate(zip((l0_ref, l1_ref, l2_ref), dils)))
    o0, o1, o2 = (token_rows(ref, 3 + slot, dil) for slot, (ref, dil) in enumerate(zip((o0_ref, o1_ref, o2_ref), dils)))
    mx = jnp.maximum(jnp.maximum(l0, l1), l2)
    e0, e1, e2 = jnp.exp(l0 - mx), jnp.exp(l1 - mx), jnp.exp(l2 - mx)
    merged = (e0 * o0 + e1 * o1 + e2 * o2) / (e0 + e1 + e2)
    h1 = h_ref[...] + _dot(merged.astype(BF16), wo_ref[...])
    _mlp_tail(h1, g_ref, wup_ref, wdn_ref, None, out_ref)


def _post(h, attn_ins, w_o, g, w_up, w_down, g_final=None):
    m, d = h.shape
    row = lambda width: pl.BlockSpec((ROW_TILE, width), lambda i: (i, 0))
    merge = len(attn_ins) > 1
    in_specs = [row(d)] + [pl.BlockSpec((ROW_TILE * a.shape[0] // m, a.shape[1]), lambda i: (i, 0)) for a in attn_ins]
    in_specs += [_resident(w_o.shape), _resident((1, d)), _resident(w_up.shape), _resident(w_down.shape)]
    args = [h, *attn_ins, w_o, g.reshape(1, d), w_up, w_down]
    if g_final is not None:
        in_specs.append(_resident((1, d)))
        args.append(g_final.reshape(1, d))
    scratch = [pltpu.VMEM((len(attn_ins), w_o.shape[0] // LANES, ROW_TILE, LANES), F32)] if merge else []
    return pl.pallas_call(
        _post_merge_kernel if merge else _post_kernel,
        grid=(m // ROW_TILE,),
        in_specs=in_specs,
        out_specs=row(d),
        out_shape=jax.ShapeDtypeStruct((m, d), F32),
        scratch_shapes=scratch,
        compiler_params=_params(("parallel",)),
        name="post_mlp",
    )(*args)


ONES_ROWS = 16
SKIP_SHIFT = -2.0 * NEG_INF


class _SoftmaxState:
    def __init__(self, m_ref, acc_ref, dv):
        self.m_ref, self.acc_ref, self.dv = m_ref, acc_ref, dv

    @staticmethod
    def scratch(n_heads, dv, tq):
        return [pltpu.VMEM((n_heads, 1, tq), F32), pltpu.VMEM((n_heads, dv + ONES_ROWS, tq), F32)]

    @staticmethod
    def store_values_t(vt_ref, hh, cols, v_t):
        dv = v_t.shape[0]
        vt_ref[hh, :dv, cols] = v_t.astype(BF16)
        vt_ref[hh, dv:, cols] = jnp.ones((ONES_ROWS, v_t.shape[1]), BF16)

    def init(self):
        self.m_ref[...] = jnp.full(self.m_ref.shape, NEG_INF, F32)
        self.acc_ref[...] = jnp.zeros(self.acc_ref.shape, F32)

    def step(self, hh, x, v_t, shift=None):
        m = self.m_ref[hh]
        m_cur = jnp.max(x, axis=0, keepdims=True)
        if shift is not None:
            m_cur = m_cur - shift
        m_new = jnp.maximum(m, m_cur)
        alpha = jnp.exp2(m - m_new)
        p = jnp.exp2(x - (m_new if shift is None else m_new + shift))
        self.m_ref[hh] = m_new
        self.acc_ref[hh] = alpha * self.acc_ref[hh] + _dot(v_t, p.astype(BF16))

    def normalized(self, hh):
        return self.acc_ref[hh, :self.dv, :] / self.acc_ref[hh, self.dv:self.dv + 1, :]


def _sweep_blocks(n_past, n_chains, state, score_fn, values_fn, past_logits, own_steps, s_a, s_b):
    def produce(buf, block, chains=range(n_chains)):
        for c in chains:
            buf[c] = score_fn(c, block)

    def consume(buf, block):
        for c in range(n_chains):
            x, shift = past_logits(c, buf[c], block)
            state.step(c, x, values_fn(c, block), shift)

    produced = set()
    for buf, block, chains, _ in own_steps:
        produce(buf, block, [c for c in chains if (id(buf), c) not in produced])
        produced.update((id(buf), c) for c in chains)
    for c in range(n_chains):
        s_a[c] = score_fn(c, 0)
        for buf, block, chains, logits_fn in own_steps:
            if c in chains:
                x, shift = logits_fn(c, buf[c])
                state.step(c, x, values_fn(c, block), shift)

    def advance(src, block, dst, next_block):
        for c in range(n_chains):
            dst[c] = score_fn(c, next_block)
            x, shift = past_logits(c, src[c], block)
            state.step(c, x, values_fn(c, block), shift)

    def two_blocks(first):
        advance(s_a, first, s_b, first + 1)
        advance(s_b, first + 1, s_a, jnp.minimum(first + 2, n_past - 1))

    def quad(t, carry):
        two_blocks(4 * t)
        two_blocks(4 * t + 2)
        return carry

    def pair(t, carry):
        two_blocks(4 * (n_past // 4) + 2 * t)
        return carry

    lax.fori_loop(0, n_past // 4, quad, 0)
    lax.fori_loop(0, (n_past % 4) // 2, pair, 0)

    @pl.when(n_past % 2 == 1)
    def _():
        consume(s_a, n_past - 1)


def _transpose_bf16(x):
    return x.astype(F32).T.astype(BF16)


def _keep_head_rows(x_t, half):
    r = lax.broadcasted_iota(jnp.int32, x_t.shape, 0)
    keep = (r < HEAD_DIM) if half == 0 else (r >= HEAD_DIM)
    return jnp.where(keep, x_t, jnp.zeros_like(x_t))


def _bf16_pieces(x, n):
    pieces = []
    for _ in range(n):
        pieces.append(x.astype(BF16))
        x = x - pieces[-1].astype(F32)
    return pieces


F32_PIECES = 3


def _moba_kernel(slopes_ref, q_ref, k_ref, v_ref, o_ref, kmean_ref, vt_ref, term_ref, kpos_ref, qh_ref,
                 s_a, s_b, s_c, m_ref, acc_ref, *, n_blocks):
    blk = MOBA_BLOCK
    n_heads = MOBA_HEADS_PER_STEP
    group = pl.program_id(1)
    i = pl.program_id(2)
    state = _SoftmaxState(m_ref, acc_ref, HEAD_DIM)
    chain = lambda tile, hh: tile * n_heads + hh
    tile_chains = [[chain(tile, hh) for hh in range(n_heads)] for tile in range(MOBA_Q_TILES)]

    @pl.when(i == 0)
    def _():
        means = []
        pos = lax.broadcasted_iota(jnp.int32, (blk, LANES), 0)
        feature = lax.broadcasted_iota(jnp.int32, (blk, LANES), 1)
        key_pos = jnp.where(feature < F32_PIECES, pos, 0).astype(F32).astype(BF16)
        for j in range(n_blocks):
            rows = slice(j * blk, (j + 1) * blk)
            means.append(jnp.mean(k_ref[0, rows, :].astype(F32), axis=0, keepdims=True))
            for tile in range(n_heads // 2):
                kpos_ref[tile, rows, :LANES] = k_ref[0, rows, tile * LANES:(tile + 1) * LANES]
                kpos_ref[tile, rows, LANES:] = key_pos
            v_t = v_ref[0, rows, :].astype(F32).T
            for hh in range(n_heads):
                state.store_values_t(vt_ref, hh, rows, v_t[hh * HEAD_DIM:(hh + 1) * HEAD_DIM, :])
        for piece, part in enumerate(_bf16_pieces(jnp.concatenate(means, axis=0), F32_PIECES)):
            kmean_ref[piece * n_blocks:(piece + 1) * n_blocks, :] = part

    state.init()
    key = lax.broadcasted_iota(jnp.int32, (blk, blk), 0)
    qry = lax.broadcasted_iota(jnp.int32, (blk, blk), 1)
    blk_idx = lax.broadcasted_iota(jnp.int32, (n_blocks, blk), 0)
    feature_row = lax.broadcasted_iota(jnp.int32, (LANES, blk), 0)
    lane_tile = lambda hh: slice((hh // 2) * LANES, (hh // 2 + 1) * LANES)
    first_block = MOBA_Q_TILES * i

    for tile in range(MOBA_Q_TILES):
        own = first_block + tile
        q_t = q_ref[0, tile * blk:(tile + 1) * blk, :].astype(F32).T
        fully_past = blk_idx < own
        block_dist = ((own - blk_idx) * blk).astype(F32)
        for hh in range(n_heads):
            slope_s = slopes_ref[n_heads * group + hh] * LOG2E
            q_h = _keep_head_rows(q_t[lane_tile(hh), :], hh % 2).astype(BF16)
            pieces = _dot(kmean_ref[:, lane_tile(hh)], q_h)
            gate = pieces[:n_blocks] + pieces[n_blocks:2 * n_blocks] + pieces[2 * n_blocks:]
            work = jnp.where(fully_past, gate, NEG_INF * HEAD_Q_SCALE)
            chosen = jnp.zeros((n_blocks, blk), jnp.bool_)
            for _ in range(min(MOBA_TOPK, n_blocks)):
                best = jnp.max(work, axis=0, keepdims=True)
                first = jnp.min(jnp.where(work == best, blk_idx, n_blocks), axis=0, keepdims=True)
                pick = blk_idx == first
                chosen = jnp.logical_or(chosen, pick)
                work = jnp.where(pick, -jnp.inf, work)
            term_ref[chain(tile, hh)] = jnp.where(jnp.logical_and(chosen, fully_past), slope_s * block_dist,
                                                  SKIP_SHIFT)
            qh_ref[chain(tile, hh), :LANES, :] = q_h
            slope_rows = jnp.zeros((LANES, blk), F32)
            for piece, part in enumerate(_bf16_pieces(jnp.full((1, blk), slope_s, F32), F32_PIECES)):
                slope_rows = jnp.where(feature_row == piece, part.astype(F32), slope_rows)
            qh_ref[chain(tile, hh), LANES:, :] = slope_rows.astype(BF16)

    def rows_of(block):
        return pl.ds(pl.multiple_of(block * blk, blk), blk)

    def scores(c, block):
        return _dot(kpos_ref[(c % n_heads) // 2, rows_of(block), :], qh_ref[c])

    def v_t(c, block):
        return vt_ref[c % n_heads, :, rows_of(block)]

    def past_logits(c, s, block):
        return s, term_ref[c, pl.ds(block, 1), :]

    def own_logits(c, s):
        return jnp.where(key <= qry, s, NEG_INF), None

    assert MOBA_Q_TILES == 2
    own_steps = [(s_b, first_block, tile_chains[0], own_logits),
                 (s_b, first_block, tile_chains[1], functools.partial(past_logits, block=first_block)),
                 (s_c, first_block + 1, tile_chains[1], own_logits)]
    _sweep_blocks(first_block, MOBA_Q_TILES * n_heads, state, scores, v_t, past_logits, own_steps, s_a, s_b)
    for tile in range(MOBA_Q_TILES):
        o_t = jnp.concatenate([state.normalized(c) for c in tile_chains[tile]], axis=0)
        o_ref[0, tile * blk:(tile + 1) * blk, :] = o_t.T.astype(o_ref.dtype)


def _moba_attention(qkv, b, s):
    n_blocks = s // MOBA_BLOCK
    n_heads = MOBA_HEADS_PER_STEP
    width = n_heads * HEAD_DIM
    n_groups = MOBA_HEADS // n_heads
    n_chains = MOBA_Q_TILES * n_heads
    step_rows = MOBA_Q_TILES * MOBA_BLOCK
    assert s % step_rows == 0
    score_buf = pltpu.VMEM((n_chains, MOBA_BLOCK, MOBA_BLOCK), F32)
    return pl.pallas_call(
        functools.partial(_moba_kernel, n_blocks=n_blocks),
        grid=(b, n_groups, s // step_rows),
        in_specs=[pl.BlockSpec(memory_space=pltpu.SMEM),
                  pl.BlockSpec((1, step_rows, width), lambda bi, g, i: (bi, i, g)),
                  pl.BlockSpec((1, s, width), lambda bi, g, i: (bi, 0, n_groups + g)),
                  pl.BlockSpec((1, s, width), lambda bi, g, i: (bi, 0, 2 * n_groups + g))],
        out_specs=pl.BlockSpec((1, step_rows, width), lambda bi, g, i: (bi, i, g)),
        out_shape=jax.ShapeDtypeStruct((b, s, MOBA_HEADS * HEAD_DIM), BF16),
        scratch_shapes=[pltpu.VMEM((F32_PIECES * n_blocks, width), BF16),
                        pltpu.VMEM((n_heads, HEAD_DIM + ONES_ROWS, s), BF16),
                        pltpu.VMEM((n_chains, n_blocks, MOBA_BLOCK), F32),
                        pltpu.VMEM((n_heads // 2, s, 2 * LANES), BF16),
                        pltpu.VMEM((n_chains, 2 * LANES, MOBA_BLOCK), BF16),
                        score_buf, score_buf, score_buf,
                        *_SoftmaxState.scratch(n_chains, HEAD_DIM, MOBA_BLOCK)],
        compiler_params=_params(("parallel", "parallel", "arbitrary")),
        name="moba_attention",
    )(_alibi_slopes(MOBA_HEADS), qkv, qkv, qkv)


def _lane_half(x, half):
    lane = lax.broadcasted_iota(jnp.int32, x.shape, x.ndim - 1)
    keep = (lane < HEAD_DIM) if half == 0 else (lane >= HEAD_DIM)
    return jnp.where(keep, x, jnp.zeros_like(x))


MAX_BANDS_PER_STEP = 8


def _bands_per_step(n_blocks):
    bands = MAX_BANDS_PER_STEP
    while n_blocks % bands:
        bands //= 2
    return bands


def _band_attention(n_bands, n_heads, q_tile, k_tile, v_tile, logits_fn, floor_fn, store_pair, s_ref, with_lse):
    chains = [(band, h) for band in range(n_bands) for h in range(n_heads)]

    def produce(c):
        band, h = chains[c]
        s_ref[c] = _dot_nt(k_tile(h, band), _lane_half(q_tile(h, band), h % 2))

    lookahead = 2
    for c in range(min(lookahead, len(chains))):
        produce(c)
    even = None
    for c, (band, h) in enumerate(chains):
        if c + lookahead < len(chains):
            produce(c + lookahead)
        x = logits_fn(h, band, s_ref[c])
        m = jnp.max(x, axis=0, keepdims=True)
        floor = floor_fn(h)
        if floor is not None:
            m = jnp.maximum(m, floor)
        p = jnp.exp2(x - m)
        den = jnp.sum(p, axis=0, keepdims=True)
        if floor is not None:
            den = den + jnp.exp2(floor - m)
        o_t = lax.dot_general(v_tile(h, band), p.astype(BF16), (((0,), (0,)), ((), ())),
                              preferred_element_type=F32)
        half = slice((h % 2) * HEAD_DIM, (h % 2 + 1) * HEAD_DIM)
        o_t = o_t[half, :] / den
        lse_t = jnp.broadcast_to((m + jnp.log2(den)) * (1.0 / LOG2E), o_t.shape) if with_lse else None
        if h % 2 == 0:
            even = (o_t, lse_t)
            continue
        store_pair(h // 2, band, jnp.concatenate([even[0], o_t], axis=0).T,
                   jnp.concatenate([even[1], lse_t], axis=0).T if with_lse else None)


def _band_rows(band, size):
    return slice(band * size, (band + 1) * size)


def _band_keys(prev_ref, cur_ref, band, size, cols):
    if band == 0:
        return jnp.concatenate([prev_ref[0, :, cols], cur_ref[0, :size, cols]], axis=0)
    return cur_ref[0, (band - 1) * size:(band + 1) * size, cols]


def _carry_previous_block(n, cur_refs, prev_refs, size):
    @pl.when(n == 0)
    def _():
        for prev in prev_refs:
            prev[...] = jnp.zeros(prev.shape, prev.dtype)

    def save():
        for cur, prev in zip(cur_refs, prev_refs):
            prev[0] = cur[0, cur.shape[1] - size:, :]
    return save


def _band_bias(bias_ref, first_step, n, size, n_heads, in_window, slope_fn):
    @pl.when(first_step)
    def _():
        key = lax.broadcasted_iota(jnp.int32, (2 * size, size), 0)
        qry = lax.broadcasted_iota(jnp.int32, (2 * size, size), 1)
        diff = qry + size - key
        window = in_window(diff)
        for h in range(n_heads):
            bias = slope_fn(h) * diff.astype(F32)
            bias_ref[h, 0] = jnp.where(window, bias, -NEG_INF)
            bias_ref[h, 1] = jnp.where(window & (key >= size), bias, -NEG_INF)

    def logits(h, band, s):
        variant = (n == 0).astype(jnp.int32) if band == 0 else 0
        return s - bias_ref[h, variant]
    return logits


def _dil_kernel(slopes_ref, qkv_ref, o_ref, lse_ref, s_ref, kp_ref, vp_ref, bias_ref, *, n_pts, dil, group):
    width = DIL_HEADS_PER_GROUP * HEAD_DIM
    n_bands = qkv_ref.shape[1] // DIL_BAND
    n_residues = qkv_ref.shape[2] // (3 * width)
    part_ref = lambda res, part: qkv_ref.at[:, :, (3 * res + part) * width:(3 * res + part + 1) * width]
    q_refs, k_refs, v_refs = ([part_ref(res, part) for res in range(n_residues)] for part in range(3))
    kp_refs = [kp_ref.at[res:res + 1] for res in range(n_residues)]
    vp_refs = [vp_ref.at[res:res + 1] for res in range(n_residues)]
    n = pl.program_id(2)
    first_step = (pl.program_id(0) == 0) & (pl.program_id(1) == 0) & (n == 0)
    save_last_blocks = _carry_previous_block(n, k_refs + v_refs, kp_refs + vp_refs, DIL_BAND)
    band_logits = _band_bias(bias_ref, first_step, n, DIL_BAND, DIL_HEADS_PER_GROUP,
                             lambda diff: (diff >= 0) & (diff <= n_pts),
                             lambda h: slopes_ref[group * DIL_HEADS_PER_GROUP + h] * (dil * LOG2E))
    tile = lambda h: slice((h // 2) * LANES, (h // 2 + 1) * LANES)
    unit = lambda u: divmod(u, n_bands)

    def store_pair(pair, u, o, lse):
        res, band = unit(u)
        rows, cols = _band_rows(band, DIL_BAND), slice(res * width + pair * LANES, res * width + (pair + 1) * LANES)
        o_ref[0, rows, cols] = o
        lse_ref[0, rows, cols] = lse

    _band_attention(
        n_residues * n_bands, DIL_HEADS_PER_GROUP,
        lambda h, u: q_refs[unit(u)[0]][0, _band_rows(unit(u)[1], DIL_BAND), tile(h)],
        lambda h, u: _band_keys(kp_refs[unit(u)[0]], k_refs[unit(u)[0]], unit(u)[1], DIL_BAND, tile(h)),
        lambda h, u: _band_keys(vp_refs[unit(u)[0]], v_refs[unit(u)[0]], unit(u)[1], DIL_BAND, tile(h)),
        lambda h, u, s: band_logits(h, unit(u)[1], s), lambda h: None, store_pair, s_ref, with_lse=True)
    save_last_blocks()


def _dilated_group(qkv, b, s, group, window, dil):
    width = DIL_HEADS_PER_GROUP * HEAD_DIM
    n_sub = s // dil
    assert n_sub % DIL_BAND == 0
    n_bands = _bands_per_step(n_sub // DIL_BAND)
    step_rows = n_bands * DIL_BAND
    n_residues = min(dil, MAX_BANDS_PER_STEP // n_bands)
    assert dil % n_residues == 0
    view = qkv.reshape(b, n_sub, dil * 3 * width)

    residues = lambda bi, r, n: (bi, n, r)
    previous_block = pltpu.VMEM((n_residues, DIL_BAND, width), BF16)
    out_spec = pl.BlockSpec((1, step_rows, n_residues * width), residues)
    out_sds = jax.ShapeDtypeStruct((b, n_sub, dil * width), F32)
    o, lse = pl.pallas_call(
        functools.partial(_dil_kernel, n_pts=window // dil, dil=dil, group=group),
        grid=(b, dil // n_residues, n_sub // step_rows),
        in_specs=[pl.BlockSpec(memory_space=pltpu.SMEM),
                  pl.BlockSpec((1, step_rows, n_residues * 3 * width), residues)],
        out_specs=[out_spec, out_spec],
        out_shape=[out_sds, out_sds],
        scratch_shapes=[pltpu.VMEM((n_residues * n_bands * DIL_HEADS_PER_GROUP, 2 * DIL_BAND, DIL_BAND), F32),
                        previous_block, previous_block,
                        pltpu.VMEM((DIL_HEADS_PER_GROUP, 2, 2 * DIL_BAND, DIL_BAND), F32)],
        compiler_params=_params(("arbitrary", "arbitrary", "arbitrary")),
        name=f"dilated_attention_g{group}",
    )(_alibi_slopes(len(DIL_PAIRS) * DIL_HEADS_PER_GROUP), view)
    return o.reshape(b * n_sub, dil * width), lse.reshape(b * n_sub, dil * width)


def _mla_proj_kernel(h_ref, g_ref, wd_ref, gq_ref, gkv_ref, wuq_ref, wuk_ref, wuv_ref,
                     cos_ref, sin_up_ref, sin_dn_ref, q_ref, k_ref, v_ref):
    n = _rms(h_ref[...], g_ref[...]).astype(BF16)
    c = _dot(n, wd_ref[...])
    nq = _rms(c[:, :MLA_Q_RANK], gq_ref[...]).astype(BF16)
    nkv = _rms(c[:, MLA_Q_RANK:MLA_Q_RANK + MLA_KV_RANK], gkv_ref[...]).astype(BF16)
    k_rope = c[:, MLA_Q_RANK + MLA_KV_RANK:]
    half = MLA_ROPE // 2
    pair = 2

    def rope(x, width):
        cos, sin_up, sin_dn = (jnp.tile(t[...], (1, width)) for t in (cos_ref, sin_up_ref, sin_dn_ref))
        return x * cos + pltpu.roll(x, half, 1) * sin_up + pltpu.roll(x, x.shape[1] - half, 1) * sin_dn

    kr = jnp.tile(rope(k_rope, 1), (1, pair))
    for h in range(0, MLA_HEADS, pair):
        cols = slice(h * LANES, (h + pair) * LANES)
        q_ref[:, cols] = (rope(_dot(nq, wuq_ref[:, cols]), pair) * MLA_Q_SCALE).astype(BF16)
        k_ref[:, cols] = (_dot(nkv, wuk_ref[:, cols]) + kr).astype(BF16)
    v_ref[...] = _dot(nkv, wuv_ref[...]).astype(BF16)


def _mla_proj(h, g, w_dkv, q_norm, w_uq, kv_norm, w_ukv, s):
    m, d = h.shape
    qk = MLA_NOPE + MLA_ROPE
    wd = jnp.concatenate([w_dkv[:, :MLA_Q_RANK + MLA_KV_RANK],
                          jnp.zeros((d, MLA_NOPE), F32), w_dkv[:, MLA_Q_RANK + MLA_KV_RANK:],
                          jnp.zeros((d, LANES - qk), F32)], axis=1).astype(BF16)
    wuq = jnp.pad(w_uq.reshape(MLA_Q_RANK, MLA_HEADS, qk), ((0, 0), (0, 0), (0, LANES - qk)))
    wuq = wuq.reshape(MLA_Q_RANK, MLA_HEADS * LANES).astype(BF16)
    w_ukv = w_ukv.reshape(MLA_KV_RANK, MLA_HEADS, MLA_NOPE + MLA_V)
    wuk = jnp.pad(w_ukv[:, :, :MLA_NOPE], ((0, 0), (0, 0), (0, LANES - MLA_NOPE)))
    wuk = wuk.reshape(MLA_KV_RANK, MLA_HEADS * LANES).astype(BF16)
    wuv = w_ukv[:, :, MLA_NOPE:].reshape(MLA_KV_RANK, MLA_HEADS * MLA_V).astype(BF16)
    half = MLA_ROPE // 2
    inv = ROPE_THETA ** (-jnp.arange(0, MLA_ROPE, 2, dtype=F32) / MLA_ROPE)
    ang = jnp.arange(s).astype(F32)[:, None] * inv[None, :]
    cos, sin = jnp.cos(ang), jnp.sin(ang)
    zeros = lambda w: jnp.zeros((s, w), F32)
    cos_t = jnp.concatenate([jnp.ones((s, MLA_NOPE), F32), cos, cos, zeros(LANES - qk)], axis=1)
    sin_up = jnp.concatenate([zeros(MLA_NOPE + half), sin, zeros(LANES - qk)], axis=1)
    sin_dn = jnp.concatenate([zeros(MLA_NOPE), -sin, zeros(half + LANES - qk)], axis=1)

    tiles_per_seq = s // PROJ_ROW_TILE
    row = lambda width: pl.BlockSpec((PROJ_ROW_TILE, width), lambda i: (i, 0))
    table = pl.BlockSpec((PROJ_ROW_TILE, LANES), lambda i: (i % tiles_per_seq, 0))
    return pl.pallas_call(
        _mla_proj_kernel,
        grid=(m // PROJ_ROW_TILE,),
        in_specs=[row(d), _resident((1, d)), _resident(wd.shape),
                  _resident((1, MLA_Q_RANK)), _resident((1, MLA_KV_RANK)),
                  _resident(wuq.shape), _resident(wuk.shape), _resident(wuv.shape),
                  table, table, table],
        out_specs=[row(MLA_HEADS * LANES), row(MLA_HEADS * LANES), row(MLA_HEADS * MLA_V)],
        out_shape=[jax.ShapeDtypeStruct((m, MLA_HEADS * LANES), BF16),
                   jax.ShapeDtypeStruct((m, MLA_HEADS * LANES), BF16),
                   jax.ShapeDtypeStruct((m, MLA_HEADS * MLA_V), BF16)],
        compiler_params=_params(("parallel",)),
        name="mla_proj",
    )(h, g.reshape(1, d), wd, q_norm.reshape(1, -1), kv_norm.reshape(1, -1), wuq, wuk, wuv,
      cos_t, sin_up, sin_dn)


def _mla_kernel(q_ref, k_ref, v_ref, o_ref, vt_ref, qh_ref, s_a, s_b, s_c, m_ref, acc_ref, *, n_tiles):
    t = MLA_TILE
    n_heads = MLA_HEADS_PER_STEP
    i = pl.program_id(2)
    state = _SoftmaxState(m_ref, acc_ref, MLA_V)
    chain = lambda tile, hh: tile * n_heads + hh
    tile_chains = [[chain(tile, hh) for hh in range(n_heads)] for tile in range(MLA_Q_TILES)]

    @pl.when(i == 0)
    def _():
        for j in range(n_tiles):
            rows = slice(j * t, (j + 1) * t)
            v_t = v_ref[0, rows, :].astype(F32).T
            for hh in range(n_heads):
                state.store_values_t(vt_ref, hh, rows, v_t[hh * MLA_V:(hh + 1) * MLA_V, :])

    state.init()
    key = lax.broadcasted_iota(jnp.int32, (t, t), 0)
    qry = lax.broadcasted_iota(jnp.int32, (t, t), 1)
    lane_tile = lambda hh: slice(hh * LANES, (hh + 1) * LANES)
    for tile in range(MLA_Q_TILES):
        for hh in range(n_heads):
            qh_ref[chain(tile, hh)] = _transpose_bf16(q_ref[0, tile * t:(tile + 1) * t, lane_tile(hh)])

    def rows_of(block):
        return pl.ds(pl.multiple_of(block * t, t), t)

    def scores(c, block):
        return _dot(k_ref[0, rows_of(block), lane_tile(c % n_heads)], qh_ref[c])

    def v_t(c, block):
        return vt_ref[c % n_heads, :, rows_of(block)]

    def past_logits(c, s, block):
        return s, None

    def causal(c, s):
        return jnp.where(key <= qry, s, NEG_INF), None

    def full(c, s):
        return s, None

    assert MLA_Q_TILES == 2
    first = MLA_Q_TILES * i
    own_steps = [(s_b, first, tile_chains[0], causal), (s_b, first, tile_chains[1], full),
                 (s_c, first + 1, tile_chains[1], causal)]
    _sweep_blocks(first, MLA_Q_TILES * n_heads, state, scores, v_t, past_logits, own_steps, s_a, s_b)
    for tile in range(MLA_Q_TILES):
        o_t = jnp.concatenate([state.normalized(c) for c in tile_chains[tile]], axis=0)
        o_ref[0, tile * t:(tile + 1) * t, :] = o_t.T.astype(o_ref.dtype)


def _mla_attention(q, k, v, b, s):
    n_heads = MLA_HEADS_PER_STEP
    n_groups = MLA_HEADS // n_heads
    whole_seq = lambda bi, g, i: (bi, 0, g)
    tile = lambda bi, g, i: (bi, i, g)
    n_chains = MLA_Q_TILES * n_heads
    step_rows = MLA_Q_TILES * MLA_TILE
    assert s % step_rows == 0
    score_buf = pltpu.VMEM((n_chains, MLA_TILE, MLA_TILE), F32)
    return pl.pallas_call(
        functools.partial(_mla_kernel, n_tiles=s // MLA_TILE),
        grid=(b, n_groups, s // step_rows),
        in_specs=[pl.BlockSpec((1, step_rows, n_heads * LANES), tile),
                  pl.BlockSpec((1, s, n_heads * LANES), whole_seq),
                  pl.BlockSpec((1, s, n_heads * MLA_V), whole_seq)],
        out_specs=pl.BlockSpec((1, step_rows, n_heads * MLA_V), tile),
        out_shape=jax.ShapeDtypeStruct((b, s, MLA_HEADS * MLA_V), BF16),
        scratch_shapes=[pltpu.VMEM((n_heads, MLA_V + ONES_ROWS, s), BF16),
                        pltpu.VMEM((n_chains, LANES, MLA_TILE), BF16),
                        score_buf, score_buf, score_buf,
                        *_SoftmaxState.scratch(n_chains, MLA_V, MLA_TILE)],
        compiler_params=_params(("parallel", "parallel", "arbitrary")),
        name="mla_attention",
    )(q, k, v)


def _swa_kernel(slopes_ref, sinks_ref, qkv_ref, o_ref, s_ref, kp_ref, vp_ref, bias_ref):
    q_width, kv_width = SWA_Q_HEADS * HEAD_DIM, SWA_KV_HEADS * HEAD_DIM
    q_ref = qkv_ref.at[:, :, :q_width]
    kc_ref = qkv_ref.at[:, :, q_width:q_width + kv_width]
    vc_ref = qkv_ref.at[:, :, q_width + kv_width:]
    n = pl.program_id(1)
    first_step = (pl.program_id(0) == 0) & (n == 0)
    save_last_block = _carry_previous_block(n, (kc_ref, vc_ref), (kp_ref, vp_ref), SWA_BLOCK)
    group = SWA_Q_HEADS // SWA_KV_HEADS
    logits = _band_bias(bias_ref, first_step, n, SWA_BLOCK, SWA_Q_HEADS,
                        lambda diff: (diff >= 0) & (diff < SWA_WINDOW), lambda h: slopes_ref[h] * LOG2E)

    def both_halves(prev_ref, cur_ref, band, kv_head):
        x = _band_keys(prev_ref, cur_ref, band, SWA_BLOCK, slice(None))
        part = x[:, kv_head * HEAD_DIM:(kv_head + 1) * HEAD_DIM]
        return jnp.concatenate([part, part], axis=1)

    n_bands = qkv_ref.shape[1] // SWA_BLOCK
    bands_kv = [(band, kv) for band in range(n_bands) for kv in range(SWA_KV_HEADS)]
    k_tiles = {bk: both_halves(kp_ref, kc_ref, *bk) for bk in bands_kv}
    v_tiles = {bk: both_halves(vp_ref, vc_ref, *bk) for bk in bands_kv}

    def sink(h):
        return sinks_ref[h] * LOG2E

    def store_pair(pair, band, o, lse):
        o_ref[0, _band_rows(band, SWA_BLOCK), pair * LANES:(pair + 1) * LANES] = o.astype(o_ref.dtype)

    _band_attention(
        n_bands, SWA_Q_HEADS,
        lambda h, band: q_ref[0, _band_rows(band, SWA_BLOCK), (h // 2) * LANES:(h // 2 + 1) * LANES],
        lambda h, band: k_tiles[band, h // group],
        lambda h, band: v_tiles[band, h // group],
        logits, sink, store_pair, s_ref, with_lse=False)
    save_last_block()


def _swa_attention(qkv, sinks, b, s):
    q_width = SWA_Q_HEADS * HEAD_DIM
    kv_width = SWA_KV_HEADS * HEAD_DIM
    assert s % SWA_BLOCK == 0
    n_bands = _bands_per_step(s // SWA_BLOCK)
    step_rows = n_bands * SWA_BLOCK
    rows = lambda bi, n: (bi, n, 0)
    previous_block = pltpu.VMEM((1, SWA_BLOCK, kv_width), BF16)
    return pl.pallas_call(
        _swa_kernel,
        grid=(b, s // step_rows),
        in_specs=[pl.BlockSpec(memory_space=pltpu.SMEM), pl.BlockSpec(memory_space=pltpu.SMEM),
                  pl.BlockSpec((1, step_rows, qkv.shape[-1]), rows)],
        out_specs=pl.BlockSpec((1, step_rows, q_width), rows),
        out_shape=jax.ShapeDtypeStruct((b, s, q_width), BF16),
        scratch_shapes=[pltpu.VMEM((n_bands * SWA_Q_HEADS, 2 * SWA_BLOCK, SWA_BLOCK), F32),
                        previous_block, previous_block,
                        pltpu.VMEM((SWA_Q_HEADS, 2, 2 * SWA_BLOCK, SWA_BLOCK), F32)],
        compiler_params=_params(("arbitrary", "arbitrary")),
        name="swa_attention",
    )(_alibi_slopes(SWA_Q_HEADS), sinks.astype(F32), qkv)


def kernel(x, l0_attn_norm, l0_w_qkv, l0_w_o, l0_mlp_norm, l0_w_up, l0_w_down, l1_attn_norm, l1_w_qkv, l1_w_o, l1_mlp_norm, l1_w_up, l1_w_down, l2_attn_norm, l2_w_dkv, l2_q_norm, l2_w_uq, l2_kv_norm, l2_w_ukv, l2_w_o, l2_mlp_norm, l2_w_up, l2_w_down, l3_attn_norm, l3_w_qkv, l3_sinks, l3_w_o, l3_mlp_norm, l3_w_up, l3_w_down, final_norm):
    b, s, d = x.shape
    bf = lambda w: w.astype(BF16)
    h = x.reshape(b * s, d)

    qkv = _norm_proj(h, l0_attn_norm, bf(l0_w_qkv), scaled_cols=MOBA_HEADS * HEAD_DIM, scale=HEAD_Q_SCALE)
    a = _moba_attention(qkv.reshape(b, s, -1), b, s).reshape(b * s, -1)
    h = _post(h, (a,), bf(l0_w_o), l0_mlp_norm, bf(l0_w_up), bf(l0_w_down))

    qkvs = _dil_proj(h, l1_attn_norm, bf(l1_w_qkv))
    groups = [_dilated_group(qkvs[g], b, s, g, window, dil) for g, (window, dil) in enumerate(DIL_PAIRS)]
    h = _post(h, tuple(o for o, _ in groups) + tuple(l for _, l in groups),
              bf(l1_w_o), l1_mlp_norm, bf(l1_w_up), bf(l1_w_down))

    q, k, v = _mla_proj(h, l2_attn_norm, l2_w_dkv, l2_q_norm, l2_w_uq, l2_kv_norm, l2_w_ukv, s)
    a = _mla_attention(q.reshape(b, s, -1), k.reshape(b, s, -1), v.reshape(b, s, -1), b, s).reshape(b * s, -1)
    h = _post(h, (a,), bf(l2_w_o), l2_mlp_norm, bf(l2_w_up), bf(l2_w_down))

    qkv = _norm_proj(h, l3_attn_norm, bf(l3_w_qkv), col_tile=256, scaled_cols=SWA_Q_HEADS * HEAD_DIM,
                     scale=HEAD_Q_SCALE)
    a = _swa_attention(qkv.reshape(b, s, -1), l3_sinks, b, s).reshape(b * s, -1)
    h = _post(h, (a,), bf(l3_w_o), l3_mlp_norm, bf(l3_w_up), bf(l3_w_down), g_final=final_norm)
    return h.reshape(b, s, d)
```

```python
import functools

import jax
import jax.numpy as jnp
from jax import lax
from jax.experimental import pallas as pl
from jax.experimental.pallas import tpu as pltpu

F32 = jnp.float32
BF16 = jnp.bfloat16

D_MODEL = 1024
HEAD_DIM = 64
RMS_EPS = 1e-6
D_FF = 4 * D_MODEL
NEG_INF = -1e30

MOBA_HEADS = 16
MOBA_BLOCK = 256
MOBA_TOPK = 3

DIL_PAIRS = ((128, 1), (512, 4), (2048, 16))
DIL_HEADS_PER_GROUP = 8
DIL_BAND = 128

MLA_HEADS = 16
MLA_Q_RANK = 768
MLA_KV_RANK = 256
MLA_NOPE = 64
MLA_ROPE = 32
MLA_V = 64
ROPE_THETA = 10000.0
MLA_TILE = 256
MLA_Q_TILES = 2
MLA_HEADS_PER_STEP = 8
MOBA_HEADS_PER_STEP = 8
MOBA_Q_TILES = 2

SWA_Q_HEADS = 16
SWA_KV_HEADS = 2
SWA_WINDOW = 128
SWA_BLOCK = 128

LANES = 128
ROW_TILE = 512
PROJ_ROW_TILE = 1024
FF_TILE = 512
VMEM_LIMIT = 56 * 1024 * 1024
LOG2E = 1.4426950408889634
HEAD_Q_SCALE = HEAD_DIM ** -0.5 * LOG2E
MLA_Q_SCALE = (MLA_NOPE + MLA_ROPE) ** -0.5 * LOG2E


def _params(semantics):
    return pltpu.CompilerParams(dimension_semantics=semantics, vmem_limit_bytes=VMEM_LIMIT)


def _resident(shape):
    return pl.BlockSpec(shape, lambda *_: (0,) * len(shape), pipeline_mode=pl.Buffered(1))


def _alibi_slopes(n_heads):
    return 2.0 ** (-8.0 * jnp.arange(1, n_heads + 1, dtype=F32) / n_heads)


def _rms(x, g):
    return x * lax.rsqrt(jnp.mean(x * x, axis=-1, keepdims=True) + RMS_EPS) * g


def _dot(a, b):
    return jnp.dot(a, b, preferred_element_type=F32)


def _dot_nt(a, b):
    return lax.dot_general(a, b, (((1,), (1,)), ((), ())), preferred_element_type=F32)


def _norm_proj_kernel(h_ref, g_ref, w_ref, o_ref, *, col_tile, scaled_cols, scale):
    n = _rms(h_ref[...], g_ref[...]).astype(BF16)
    for c in range(o_ref.shape[1] // col_tile):
        cols = slice(c * col_tile, (c + 1) * col_tile)
        y = _dot(n, w_ref[:, cols])
        if (c + 1) * col_tile <= scaled_cols:
            y = y * scale
        o_ref[:, cols] = y.astype(o_ref.dtype)


def _norm_proj(h, g, w, col_tile=512, scaled_cols=0, scale=1.0):
    m, d = h.shape
    n_out = w.shape[1]
    assert scaled_cols % col_tile == 0
    return pl.pallas_call(
        functools.partial(_norm_proj_kernel, col_tile=col_tile, scaled_cols=scaled_cols, scale=scale),
        grid=(m // PROJ_ROW_TILE,),
        in_specs=[pl.BlockSpec((PROJ_ROW_TILE, d), lambda i: (i, 0)),
                  _resident((1, d)),
                  _resident((d, n_out))],
        out_specs=pl.BlockSpec((PROJ_ROW_TILE, n_out), lambda i: (i, 0)),
        out_shape=jax.ShapeDtypeStruct((m, n_out), BF16),
        compiler_params=_params(("parallel",)),
        name="norm_proj",
    )(h, g.reshape(1, d), w)


def _dil_proj_kernel(h_ref, g_ref, w_ref, o0_ref, o1_ref, o2_ref, stage_ref, sorted_ref):
    n = _rms(h_ref[...], g_ref[...])
    rows, d = n.shape
    width = DIL_HEADS_PER_GROUP * HEAD_DIM
    for t in range(d // LANES):
        stage_ref[t] = n[:, t * LANES:(t + 1) * LANES]
    n = n.astype(BF16)
    for g, (o_ref, (_, dil)) in enumerate(zip((o0_ref, o1_ref, o2_ref), DIL_PAIRS)):
        per = rows // dil
        for r in range(dil if dil > 1 else 0):
            for t in range(d // LANES):
                sorted_ref[g - 1, r * per:(r + 1) * per, t * LANES:(t + 1) * LANES] = (
                    stage_ref[t, pl.ds(r, per, stride=dil), :].astype(BF16))
        lhs = n if dil == 1 else sorted_ref[g - 1]
        for part in range(3):
            src = (3 * g + part) * width
            y = _dot(lhs, w_ref[:, src:src + width])
            if part == 0:
                y = y * HEAD_Q_SCALE
            for r in range(dil):
                dst = (3 * r + part) * width
                o_ref[:, dst:dst + width] = y[r * per:(r + 1) * per, :].astype(BF16)


def _dil_proj(h, g, w):
    m, d = h.shape
    feat = 3 * DIL_HEADS_PER_GROUP * HEAD_DIM
    assert DIL_PAIRS[0][1] == 1
    view = lambda dil: (m // dil, dil * feat)
    view_block = lambda dil: pl.BlockSpec((PROJ_ROW_TILE // dil, dil * feat), lambda i: (i, 0))
    return pl.pallas_call(
        _dil_proj_kernel,
        grid=(m // PROJ_ROW_TILE,),
        in_specs=[pl.BlockSpec((PROJ_ROW_TILE, d), lambda i: (i, 0)), _resident((1, d)), _resident(w.shape)],
        out_specs=[view_block(dil) for _, dil in DIL_PAIRS],
        out_shape=[jax.ShapeDtypeStruct(view(dil), BF16) for _, dil in DIL_PAIRS],
        scratch_shapes=[pltpu.VMEM((d // LANES, PROJ_ROW_TILE, LANES), F32),
                        pltpu.VMEM((len(DIL_PAIRS) - 1, PROJ_ROW_TILE, d), BF16)],
        compiler_params=_params(("parallel",)),
        name="dil_proj",
    )(h, g.reshape(1, d), w)


def _mlp_tail(h1, g_ref, wup_ref, wdn_ref, gf_ref, out_ref):
    n = _rms(h1, g_ref[...]).astype(BF16)
    acc = jnp.zeros_like(h1)
    for c in range(wup_ref.shape[1] // FF_TILE):
        cols = slice(c * FF_TILE, (c + 1) * FF_TILE)
        u = jnp.square(jnp.maximum(_dot(n, wup_ref[:, cols]), 0.0)).astype(BF16)
        acc = acc + _dot(u, wdn_ref[cols, :])
    out = h1 + acc
    if gf_ref is not None:
        out = _rms(out, gf_ref[...])
    out_ref[...] = out


def _post_kernel(h_ref, a_ref, wo_ref, g_ref, wup_ref, wdn_ref, *rest):
    gf_ref, out_ref = (rest[0], rest[1]) if len(rest) == 2 else (None, rest[0])
    h1 = h_ref[...] + _dot(a_ref[...], wo_ref[...])
    _mlp_tail(h1, g_ref, wup_ref, wdn_ref, gf_ref, out_ref)


def _post_merge_kernel(h_ref, o0_ref, o1_ref, o2_ref, l0_ref, l1_ref, l2_ref,
                       wo_ref, g_ref, wup_ref, wdn_ref, out_ref, stage_ref):
    rows, width = o0_ref.shape

    def token_rows(x_ref, slot, dil):
        if dil == 1:
            return x_ref[...]
        for r in range(dil):
            for t in range(width // LANES):
                src = r * width + t * LANES
                stage_ref[slot, t, pl.ds(r, rows // dil, stride=dil), :] = x_ref[:, src:src + LANES]
        return jnp.concatenate([stage_ref[slot, t] for t in range(width // LANES)], axis=1)

    dils = [dil for _, dil in DIL_PAIRS]
    l0, l1, l2 = (token_rows(ref, slot, dil) for slot, (ref, dil) in enumerate(zip((l0_ref, l1_ref, l2_ref), dils)))
    o0, o1, o2 = (token_rows(ref, 3 + slot, dil) for slot, (ref, dil) in enumerate(zip((o0_ref, o1_ref, o2_ref), dils)))
    mx = jnp.maximum(jnp.maximum(l0, l1), l2)
    e0, e1, e2 = jnp.exp(l0 - mx), jnp.exp(l1 - mx), jnp.exp(l2 - mx)
    merged = (e0 * o0 + e1 * o1 + e2 * o2) / (e0 + e1 + e2)
    h1 = h_ref[...] + _dot(merged.astype(BF16), wo_ref[...])
    _mlp_tail(h1, g_ref, wup_ref, wdn_ref, None, out_ref)


def _post(h, attn_ins, w_o, g, w_up, w_down, g_final=None):
    m, d = h.shape
    row = lambda width: pl.BlockSpec((ROW_TILE, width), lambda i: (i, 0))
    merge = len(attn_ins) > 1
    in_specs = [row(d)] + [pl.BlockSpec((ROW_TILE * a.shape[0] // m, a.shape[1]), lambda i: (i, 0)) for a in attn_ins]
    in_specs += [_resident(w_o.shape), _resident((1, d)), _resident(w_up.shape), _resident(w_down.shape)]
    args = [h, *attn_ins, w_o, g.reshape(1, d), w_up, w_down]
    if g_final is not None:
        in_specs.append(_resident((1, d)))
        args.append(g_final.reshape(1, d))
    scratch = [pltpu.VMEM((len(attn_ins), w_o.shape[0] // LANES, ROW_TILE, LANES), F32)] if merge else []
    return pl.pallas_call(
        _post_merge_kernel if merge else _post_kernel,
        grid=(m // ROW_TILE,),
        in_specs=in_specs,
        out_specs=row(d),
        out_shape=jax.ShapeDtypeStruct((m, d), F32),
        scratch_shapes=scratch,
        compiler_params=_params(("parallel",)),
        name="post_mlp",
    )(*args)


ONES_ROWS = 16
SKIP_SHIFT = -2.0 * NEG_INF


class _SoftmaxState:
    def __init__(self, m_ref, acc_ref, dv):
        self.m_ref, self.acc_ref, self.dv = m_ref, acc_ref, dv

    @staticmethod
    def scratch(n_heads, dv, tq):
        return [pltpu.VMEM((n_heads, 1, tq), F32), pltpu.VMEM((n_heads, dv + ONES_ROWS, tq), F32)]

    @staticmethod
    def store_values_t(vt_ref, hh, cols, v_t):
        dv = v_t.shape[0]
        vt_ref[hh, :dv, cols] = v_t.astype(BF16)
        vt_ref[hh, dv:, cols] = jnp.ones((ONES_ROWS, v_t.shape[1]), BF16)

    def init(self):
        self.m_ref[...] = jnp.full(self.m_ref.shape, NEG_INF, F32)
        self.acc_ref[...] = jnp.zeros(self.acc_ref.shape, F32)

    def step(self, hh, x, v_t, shift=None):
        m = self.m_ref[hh]
        m_cur = jnp.max(x, axis=0, keepdims=True)
        if shift is not None:
            m_cur = m_cur - shift
        m_new = jnp.maximum(m, m_cur)
        alpha = jnp.exp2(m - m_new)
        p = jnp.exp2(x - (m_new if shift is None else m_new + shift))
        self.m_ref[hh] = m_new
        self.acc_ref[hh] = alpha * self.acc_ref[hh] + _dot(v_t, p.astype(BF16))

    def normalized(self, hh):
        return self.acc_ref[hh, :self.dv, :] / self.acc_ref[hh, self.dv:self.dv + 1, :]


def _sweep_blocks(n_past, n_chains, state, score_fn, values_fn, past_logits, own_steps, s_a, s_b,
                  chain_lookahead=0):
    def produce(buf, block, chains=range(n_chains)):
        for c in chains:
            buf[c] = score_fn(c, block)

    def consume(buf, block):
        for c in range(n_chains):
            x, shift = past_logits(c, buf[c], block)
            state.step(c, x, values_fn(c, block), shift)

    produced = set()
    for buf, block, chains, _ in own_steps:
        produce(buf, block, [c for c in chains if (id(buf), c) not in produced])
        produced.update((id(buf), c) for c in chains)
    for c in range(n_chains):
        s_a[c] = score_fn(c, 0)
        for buf, block, chains, logits_fn in own_steps:
            if c in chains:
                x, shift = logits_fn(c, buf[c])
                state.step(c, x, values_fn(c, block), shift)

    def advance(src, block, dst, next_block):
        for c in range(min(chain_lookahead, n_chains)):
            dst[c] = score_fn(c, next_block)
        for c in range(n_chains):
            if c + chain_lookahead < n_chains:
                dst[c + chain_lookahead] = score_fn(c + chain_lookahead, next_block)
            x, shift = past_logits(c, src[c], block)
            state.step(c, x, values_fn(c, block), shift)

    def two_blocks(first):
        advance(s_a, first, s_b, first + 1)
        advance(s_b, first + 1, s_a, jnp.minimum(first + 2, n_past - 1))

    def quad(t, carry):
        two_blocks(4 * t)
        two_blocks(4 * t + 2)
        return carry

    def pair(t, carry):
        two_blocks(4 * (n_past // 4) + 2 * t)
        return carry

    lax.fori_loop(0, n_past // 4, quad, 0)
    lax.fori_loop(0, (n_past % 4) // 2, pair, 0)

    @pl.when(n_past % 2 == 1)
    def _():
        consume(s_a, n_past - 1)


def _transpose_bf16(x):
    return x.astype(F32).T.astype(BF16)


def _keep_head_rows(x_t, half):
    r = lax.broadcasted_iota(jnp.int32, x_t.shape, 0)
    keep = (r < HEAD_DIM) if half == 0 else (r >= HEAD_DIM)
    return jnp.where(keep, x_t, jnp.zeros_like(x_t))


def _bf16_pieces(x, n):
    pieces = []
    for _ in range(n):
        pieces.append(x.astype(BF16))
        x = x - pieces[-1].astype(F32)
    return pieces


F32_PIECES = 3


def _moba_kernel(slopes_ref, q_ref, k_ref, v_ref, o_ref, kmean_ref, vt_ref, term_ref, kpos_ref, qh_ref,
                 s_a, s_b, s_c, m_ref, acc_ref, *, n_blocks):
    blk = MOBA_BLOCK
    n_heads = MOBA_HEADS_PER_STEP
    group = pl.program_id(1)
    i = pl.program_id(2)
    state = _SoftmaxState(m_ref, acc_ref, HEAD_DIM)
    chain = lambda tile, hh: tile * n_heads + hh
    tile_chains = [[chain(tile, hh) for hh in range(n_heads)] for tile in range(MOBA_Q_TILES)]

    @pl.when(i == 0)
    def _():
        means = []
        pos = lax.broadcasted_iota(jnp.int32, (blk, LANES), 0)
        feature = lax.broadcasted_iota(jnp.int32, (blk, LANES), 1)
        key_pos = jnp.where(feature < F32_PIECES, pos, 0).astype(F32).astype(BF16)
        for j in range(n_blocks):
            rows = slice(j * blk, (j + 1) * blk)
            means.append(jnp.mean(k_ref[0, rows, :].astype(F32), axis=0, keepdims=True))
            for tile in range(n_heads // 2):
                kpos_ref[tile, rows, :LANES] = k_ref[0, rows, tile * LANES:(tile + 1) * LANES]
                kpos_ref[tile, rows, LANES:] = key_pos
            v_t = v_ref[0, rows, :].astype(F32).T
            for hh in range(n_heads):
                state.store_values_t(vt_ref, hh, rows, v_t[hh * HEAD_DIM:(hh + 1) * HEAD_DIM, :])
        for piece, part in enumerate(_bf16_pieces(jnp.concatenate(means, axis=0), F32_PIECES)):
            kmean_ref[piece * n_blocks:(piece + 1) * n_blocks, :] = part

    state.init()
    key = lax.broadcasted_iota(jnp.int32, (blk, blk), 0)
    qry = lax.broadcasted_iota(jnp.int32, (blk, blk), 1)
    blk_idx = lax.broadcasted_iota(jnp.int32, (n_blocks, blk), 0)
    feature_row = lax.broadcasted_iota(jnp.int32, (LANES, blk), 0)
    lane_tile = lambda hh: slice((hh // 2) * LANES, (hh // 2 + 1) * LANES)
    first_block = MOBA_Q_TILES * i

    for tile in range(MOBA_Q_TILES):
        own = first_block + tile
        q_t = q_ref[0, tile * blk:(tile + 1) * blk, :].astype(F32).T
        fully_past = blk_idx < own
        block_dist = ((own - blk_idx) * blk).astype(F32)
        for hh in range(n_heads):
            slope_s = slopes_ref[n_heads * group + hh] * LOG2E
            q_h = _keep_head_rows(q_t[lane_tile(hh), :], hh % 2).astype(BF16)
            pieces = _dot(kmean_ref[:, lane_tile(hh)], q_h)
            gate = pieces[:n_blocks] + pieces[n_blocks:2 * n_blocks] + pieces[2 * n_blocks:]
            work = jnp.where(fully_past, gate, NEG_INF * HEAD_Q_SCALE)
            chosen = jnp.zeros((n_blocks, blk), jnp.bool_)
            for _ in range(min(MOBA_TOPK, n_blocks)):
                best = jnp.max(work, axis=0, keepdims=True)
                first = jnp.min(jnp.where(work == best, blk_idx, n_blocks), axis=0, keepdims=True)
                pick = blk_idx == first
                chosen = jnp.logical_or(chosen, pick)
                work = jnp.where(pick, -jnp.inf, work)
            term_ref[chain(tile, hh)] = jnp.where(jnp.logical_and(chosen, fully_past), slope_s * block_dist,
                                                  SKIP_SHIFT)
            qh_ref[chain(tile, hh), :LANES, :] = q_h
            slope_rows = jnp.zeros((LANES, blk), F32)
            for piece, part in enumerate(_bf16_pieces(jnp.full((1, blk), slope_s, F32), F32_PIECES)):
                slope_rows = jnp.where(feature_row == piece, part.astype(F32), slope_rows)
            qh_ref[chain(tile, hh), LANES:, :] = slope_rows.astype(BF16)

    def rows_of(block):
        return pl.ds(pl.multiple_of(block * blk, blk), blk)

    def scores(c, block):
        return _dot(kpos_ref[(c % n_heads) // 2, rows_of(block), :], qh_ref[c])

    def v_t(c, block):
        return vt_ref[c % n_heads, :, rows_of(block)]

    def past_logits(c, s, block):
        return s, term_ref[c, pl.ds(block, 1), :]

    def own_logits(c, s):
        return jnp.where(key <= qry, s, NEG_INF), None

    assert MOBA_Q_TILES == 2
    own_steps = [(s_b, first_block, tile_chains[0], own_logits),
                 (s_b, first_block, tile_chains[1], functools.partial(past_logits, block=first_block)),
                 (s_c, first_block + 1, tile_chains[1], own_logits)]
    _sweep_blocks(first_block, MOBA_Q_TILES * n_heads, state, scores, v_t, past_logits, own_steps, s_a, s_b,
                  chain_lookahead=1)
    for tile in range(MOBA_Q_TILES):
        o_t = jnp.concatenate([state.normalized(c) for c in tile_chains[tile]], axis=0)
        o_ref[0, tile * blk:(tile + 1) * blk, :] = o_t.T.astype(o_ref.dtype)


def _moba_attention(qkv, b, s):
    n_blocks = s // MOBA_BLOCK
    n_heads = MOBA_HEADS_PER_STEP
    width = n_heads * HEAD_DIM
    n_groups = MOBA_HEADS // n_heads
    n_chains = MOBA_Q_TILES * n_heads
    step_rows = MOBA_Q_TILES * MOBA_BLOCK
    assert s % step_rows == 0
    score_buf = pltpu.VMEM((n_chains, MOBA_BLOCK, MOBA_BLOCK), F32)
    return pl.pallas_call(
        functools.partial(_moba_kernel, n_blocks=n_blocks),
        grid=(b, n_groups, s // step_rows),
        in_specs=[pl.BlockSpec(memory_space=pltpu.SMEM),
                  pl.BlockSpec((1, step_rows, width), lambda bi, g, i: (bi, i, g)),
                  pl.BlockSpec((1, s, width), lambda bi, g, i: (bi, 0, n_groups + g)),
                  pl.BlockSpec((1, s, width), lambda bi, g, i: (bi, 0, 2 * n_groups + g))],
        out_specs=pl.BlockSpec((1, step_rows, width), lambda bi, g, i: (bi, i, g)),
        out_shape=jax.ShapeDtypeStruct((b, s, MOBA_HEADS * HEAD_DIM), BF16),
        scratch_shapes=[pltpu.VMEM((F32_PIECES * n_blocks, width), BF16),
                        pltpu.VMEM((n_heads, HEAD_DIM + ONES_ROWS, s), BF16),
                        pltpu.VMEM((n_chains, n_blocks, MOBA_BLOCK), F32),
                        pltpu.VMEM((n_heads // 2, s, 2 * LANES), BF16),
                        pltpu.VMEM((n_chains, 2 * LANES, MOBA_BLOCK), BF16),
                        score_buf, score_buf, score_buf,
                        *_SoftmaxState.scratch(n_chains, HEAD_DIM, MOBA_BLOCK)],
        compiler_params=_params(("parallel", "parallel", "arbitrary")),
        name="moba_attention",
    )(_alibi_slopes(MOBA_HEADS), qkv, qkv, qkv)


def _lane_half(x, half):
    lane = lax.broadcasted_iota(jnp.int32, x.shape, x.ndim - 1)
    keep = (lane < HEAD_DIM) if half == 0 else (lane >= HEAD_DIM)
    return jnp.where(keep, x, jnp.zeros_like(x))


MAX_BANDS_PER_STEP = 8


def _bands_per_step(n_blocks):
    bands = MAX_BANDS_PER_STEP
    while n_blocks % bands:
        bands //= 2
    return bands


def _band_attention(n_bands, n_heads, q_tile, k_tile, v_tile, logits_fn, floor_fn, store_pair, s_ref, with_lse):
    chains = [(band, h) for band in range(n_bands) for h in range(n_heads)]

    def produce(c):
        band, h = chains[c]
        s_ref[c] = _dot_nt(k_tile(h, band), _lane_half(q_tile(h, band), h % 2))

    lookahead = 2
    for c in range(min(lookahead, len(chains))):
        produce(c)
    even = None
    for c, (band, h) in enumerate(chains):
        if c + lookahead < len(chains):
            produce(c + lookahead)
        x = logits_fn(h, band, s_ref[c])
        m = jnp.max(x, axis=0, keepdims=True)
        floor = floor_fn(h)
        if floor is not None:
            m = jnp.maximum(m, floor)
        p = jnp.exp2(x - m)
        den = jnp.sum(p, axis=0, keepdims=True)
        if floor is not None:
            den = den + jnp.exp2(floor - m)
        o_t = lax.dot_general(v_tile(h, band), p.astype(BF16), (((0,), (0,)), ((), ())),
                              preferred_element_type=F32)
        half = slice((h % 2) * HEAD_DIM, (h % 2 + 1) * HEAD_DIM)
        o_t = o_t[half, :] / den
        lse_t = jnp.broadcast_to((m + jnp.log2(den)) * (1.0 / LOG2E), o_t.shape) if with_lse else None
        if h % 2 == 0:
            even = (o_t, lse_t)
            continue
        store_pair(h // 2, band, jnp.concatenate([even[0], o_t], axis=0).T,
                   jnp.concatenate([even[1], lse_t], axis=0).T if with_lse else None)


def _band_rows(band, size):
    return slice(band * size, (band + 1) * size)


def _band_keys(prev_ref, cur_ref, band, size, cols):
    if band == 0:
        return jnp.concatenate([prev_ref[0, :, cols], cur_ref[0, :size, cols]], axis=0)
    return cur_ref[0, (band - 1) * size:(band + 1) * size, cols]


def _carry_previous_block(n, cur_refs, prev_refs, size):
    @pl.when(n == 0)
    def _():
        for prev in prev_refs:
            prev[...] = jnp.zeros(prev.shape, prev.dtype)

    def save():
        for cur, prev in zip(cur_refs, prev_refs):
            prev[0] = cur[0, cur.shape[1] - size:, :]
    return save


def _band_bias(bias_ref, first_step, n, size, n_heads, in_window, slope_fn):
    @pl.when(first_step)
    def _():
        key = lax.broadcasted_iota(jnp.int32, (2 * size, size), 0)
        qry = lax.broadcasted_iota(jnp.int32, (2 * size, size), 1)
        diff = qry + size - key
        window = in_window(diff)
        for h in range(n_heads):
            bias = slope_fn(h) * diff.astype(F32)
            bias_ref[h, 0] = jnp.where(window, bias, -NEG_INF)
            bias_ref[h, 1] = jnp.where(window & (key >= size), bias, -NEG_INF)

    def logits(h, band, s):
        variant = (n == 0).astype(jnp.int32) if band == 0 else 0
        return s - bias_ref[h, variant]
    return logits


def _dil_kernel(slopes_ref, qkv_ref, o_ref, lse_ref, s_ref, kp_ref, vp_ref, bias_ref, *, n_pts, dil, group):
    width = DIL_HEADS_PER_GROUP * HEAD_DIM
    n_bands = qkv_ref.shape[1] // DIL_BAND
    n_residues = qkv_ref.shape[2] // (3 * width)
    part_ref = lambda res, part: qkv_ref.at[:, :, (3 * res + part) * width:(3 * res + part + 1) * width]
    q_refs, k_refs, v_refs = ([part_ref(res, part) for res in range(n_residues)] for part in range(3))
    kp_refs = [kp_ref.at[res:res + 1] for res in range(n_residues)]
    vp_refs = [vp_ref.at[res:res + 1] for res in range(n_residues)]
    n = pl.program_id(2)
    first_step = (pl.program_id(0) == 0) & (pl.program_id(1) == 0) & (n == 0)
    save_last_blocks = _carry_previous_block(n, k_refs + v_refs, kp_refs + vp_refs, DIL_BAND)
    band_logits = _band_bias(bias_ref, first_step, n, DIL_BAND, DIL_HEADS_PER_GROUP,
                             lambda diff: (diff >= 0) & (diff <= n_pts),
                             lambda h: slopes_ref[group * DIL_HEADS_PER_GROUP + h] * (dil * LOG2E))
    tile = lambda h: slice((h // 2) * LANES, (h // 2 + 1) * LANES)
    unit = lambda u: divmod(u, n_bands)

    def store_pair(pair, u, o, lse):
        res, band = unit(u)
        rows, cols = _band_rows(band, DIL_BAND), slice(res * width + pair * LANES, res * width + (pair + 1) * LANES)
        o_ref[0, rows, cols] = o
        lse_ref[0, rows, cols] = lse

    _band_attention(
        n_residues * n_bands, DIL_HEADS_PER_GROUP,
        lambda h, u: q_refs[unit(u)[0]][0, _band_rows(unit(u)[1], DIL_BAND), tile(h)],
        lambda h, u: _band_keys(kp_refs[unit(u)[0]], k_refs[unit(u)[0]], unit(u)[1], DIL_BAND, tile(h)),
        lambda h, u: _band_keys(vp_refs[unit(u)[0]], v_refs[unit(u)[0]], unit(u)[1], DIL_BAND, tile(h)),
        lambda h, u, s: band_logits(h, unit(u)[1], s), lambda h: None, store_pair, s_ref, with_lse=True)
    save_last_blocks()


def _dilated_group(qkv, b, s, group, window, dil):
    width = DIL_HEADS_PER_GROUP * HEAD_DIM
    n_sub = s // dil
    assert n_sub % DIL_BAND == 0
    n_bands = _bands_per_step(n_sub // DIL_BAND)
    step_rows = n_bands * DIL_BAND
    n_residues = min(dil, MAX_BANDS_PER_STEP // n_bands)
    assert dil % n_residues == 0
    view = qkv.reshape(b, n_sub, dil * 3 * width)

    residues = lambda bi, r, n: (bi, n, r)
    previous_block = pltpu.VMEM((n_residues, DIL_BAND, width), BF16)
    out_spec = pl.BlockSpec((1, step_rows, n_residues * width), residues)
    out_sds = jax.ShapeDtypeStruct((b, n_sub, dil * width), F32)
    o, lse = pl.pallas_call(
        functools.partial(_dil_kernel, n_pts=window // dil, dil=dil, group=group),
        grid=(b, dil // n_residues, n_sub // step_rows),
        in_specs=[pl.BlockSpec(memory_space=pltpu.SMEM),
                  pl.BlockSpec((1, step_rows, n_residues * 3 * width), residues)],
        out_specs=[out_spec, out_spec],
        out_shape=[out_sds, out_sds],
        scratch_shapes=[pltpu.VMEM((n_residues * n_bands * DIL_HEADS_PER_GROUP, 2 * DIL_BAND, DIL_BAND), F32),
                        previous_block, previous_block,
                        pltpu.VMEM((DIL_HEADS_PER_GROUP, 2, 2 * DIL_BAND, DIL_BAND), F32)],
        compiler_params=_params(("arbitrary", "arbitrary", "arbitrary")),
        name=f"dilated_attention_g{group}",
    )(_alibi_slopes(len(DIL_PAIRS) * DIL_HEADS_PER_GROUP), view)
    return o.reshape(b * n_sub, dil * width), lse.reshape(b * n_sub, dil * width)


def _mla_proj_kernel(h_ref, g_ref, wd_ref, gq_ref, gkv_ref, wuq_ref, wuk_ref, wuv_ref,
                     cos_ref, sin_up_ref, sin_dn_ref, q_ref, k_ref, v_ref):
    n = _rms(h_ref[...], g_ref[...]).astype(BF16)
    c = _dot(n, wd_ref[...])
    nq = _rms(c[:, :MLA_Q_RANK], gq_ref[...]).astype(BF16)
    nkv = _rms(c[:, MLA_Q_RANK:MLA_Q_RANK + MLA_KV_RANK], gkv_ref[...]).astype(BF16)
    k_rope = c[:, MLA_Q_RANK + MLA_KV_RANK:]
    half = MLA_ROPE // 2
    pair = 2

    def rope(x, width):
        cos, sin_up, sin_dn = (jnp.tile(t[...], (1, width)) for t in (cos_ref, sin_up_ref, sin_dn_ref))
        return x * cos + pltpu.roll(x, half, 1) * sin_up + pltpu.roll(x, x.shape[1] - half, 1) * sin_dn

    kr = jnp.tile(rope(k_rope, 1), (1, pair))
    for h in range(0, MLA_HEADS, pair):
        cols = slice(h * LANES, (h + pair) * LANES)
        q_ref[:, cols] = (rope(_dot(nq, wuq_ref[:, cols]), pair) * MLA_Q_SCALE).astype(BF16)
        k_ref[:, cols] = (_dot(nkv, wuk_ref[:, cols]) + kr).astype(BF16)
    v_ref[...] = _dot(nkv, wuv_ref[...]).astype(BF16)


def _mla_proj(h, g, w_dkv, q_norm, w_uq, kv_norm, w_ukv, s):
    m, d = h.shape
    qk = MLA_NOPE + MLA_ROPE
    wd = jnp.concatenate([w_dkv[:, :MLA_Q_RANK + MLA_KV_RANK],
                          jnp.zeros((d, MLA_NOPE), F32), w_dkv[:, MLA_Q_RANK + MLA_KV_RANK:],
                          jnp.zeros((d, LANES - qk), F32)], axis=1).astype(BF16)
    wuq = jnp.pad(w_uq.reshape(MLA_Q_RANK, MLA_HEADS, qk), ((0, 0), (0, 0), (0, LANES - qk)))
    wuq = wuq.reshape(MLA_Q_RANK, MLA_HEADS * LANES).astype(BF16)
    w_ukv = w_ukv.reshape(MLA_KV_RANK, MLA_HEADS, MLA_NOPE + MLA_V)
    wuk = jnp.pad(w_ukv[:, :, :MLA_NOPE], ((0, 0), (0, 0), (0, LANES - MLA_NOPE)))
    wuk = wuk.reshape(MLA_KV_RANK, MLA_HEADS * LANES).astype(BF16)
    wuv = w_ukv[:, :, MLA_NOPE:].reshape(MLA_KV_RANK, MLA_HEADS * MLA_V).astype(BF16)
    half = MLA_ROPE // 2
    inv = ROPE_THETA ** (-jnp.arange(0, MLA_ROPE, 2, dtype=F32) / MLA_ROPE)
    ang = jnp.arange(s).astype(F32)[:, None] * inv[None, :]
    cos, sin = jnp.cos(ang), jnp.sin(ang)
    zeros = lambda w: jnp.zeros((s, w), F32)
    cos_t = jnp.concatenate([jnp.ones((s, MLA_NOPE), F32), cos, cos, zeros(LANES - qk)], axis=1)
    sin_up = jnp.concatenate([zeros(MLA_NOPE + half), sin, zeros(LANES - qk)], axis=1)
    sin_dn = jnp.concatenate([zeros(MLA_NOPE), -sin, zeros(half + LANES - qk)], axis=1)

    tiles_per_seq = s // PROJ_ROW_TILE
    row = lambda width: pl.BlockSpec((PROJ_ROW_TILE, width), lambda i: (i, 0))
    table = pl.BlockSpec((PROJ_ROW_TILE, LANES), lambda i: (i % tiles_per_seq, 0))
    return pl.pallas_call(
        _mla_proj_kernel,
        grid=(m // PROJ_ROW_TILE,),
        in_specs=[row(d), _resident((1, d)), _resident(wd.shape),
                  _resident((1, MLA_Q_RANK)), _resident((1, MLA_KV_RANK)),
                  _resident(wuq.shape), _resident(wuk.shape), _resident(wuv.shape),
                  table, table, table],
        out_specs=[row(MLA_HEADS * LANES), row(MLA_HEADS * LANES), row(MLA_HEADS * MLA_V)],
        out_shape=[jax.ShapeDtypeStruct((m, MLA_HEADS * LANES), BF16),
                   jax.ShapeDtypeStruct((m, MLA_HEADS * LANES), BF16),
                   jax.ShapeDtypeStruct((m, MLA_HEADS * MLA_V), BF16)],
        compiler_params=_params(("parallel",)),
        name="mla_proj",
    )(h, g.reshape(1, d), wd, q_norm.reshape(1, -1), kv_norm.reshape(1, -1), wuq, wuk, wuv,
      cos_t, sin_up, sin_dn)


def _mla_kernel(q_ref, k_ref, v_ref, o_ref, vt_ref, qh_ref, s_a, s_b, s_c, m_ref, acc_ref, *, n_tiles):
    t = MLA_TILE
    n_heads = MLA_HEADS_PER_STEP
    i = pl.program_id(2)
    state = _SoftmaxState(m_ref, acc_ref, MLA_V)
    chain = lambda tile, hh: tile * n_heads + hh
    tile_chains = [[chain(tile, hh) for hh in range(n_heads)] for tile in range(MLA_Q_TILES)]

    @pl.when(i == 0)
    def _():
        for j in range(n_tiles):
            rows = slice(j * t, (j + 1) * t)
            v_t = v_ref[0, rows, :].astype(F32).T
            for hh in range(n_heads):
                state.store_values_t(vt_ref, hh, rows, v_t[hh * MLA_V:(hh + 1) * MLA_V, :])

    state.init()
    key = lax.broadcasted_iota(jnp.int32, (t, t), 0)
    qry = lax.broadcasted_iota(jnp.int32, (t, t), 1)
    lane_tile = lambda hh: slice(hh * LANES, (hh + 1) * LANES)
    for tile in range(MLA_Q_TILES):
        for hh in range(n_heads):
            qh_ref[chain(tile, hh)] = _transpose_bf16(q_ref[0, tile * t:(tile + 1) * t, lane_tile(hh)])

    def rows_of(block):
        return pl.ds(pl.multiple_of(block * t, t), t)

    def scores(c, block):
        return _dot(k_ref[0, rows_of(block), lane_tile(c % n_heads)], qh_ref[c])

    def v_t(c, block):
        return vt_ref[c % n_heads, :, rows_of(block)]

    def past_logits(c, s, block):
        return s, None

    def causal(c, s):
        return jnp.where(key <= qry, s, NEG_INF), None

    def full(c, s):
        return s, None

    assert MLA_Q_TILES == 2
    first = MLA_Q_TILES * i
    own_steps = [(s_b, first, tile_chains[0], causal), (s_b, first, tile_chains[1], full),
                 (s_c, first + 1, tile_chains[1], causal)]
    _sweep_blocks(first, MLA_Q_TILES * n_heads, state, scores, v_t, past_logits, own_steps, s_a, s_b)
    for tile in range(MLA_Q_TILES):
        o_t = jnp.concatenate([state.normalized(c) for c in tile_chains[tile]], axis=0)
        o_ref[0, tile * t:(tile + 1) * t, :] = o_t.T.astype(o_ref.dtype)


def _mla_attention(q, k, v, b, s):
    n_heads = MLA_HEADS_PER_STEP
    n_groups = MLA_HEADS // n_heads
    whole_seq = lambda bi, g, i: (bi, 0, g)
    tile = lambda bi, g, i: (bi, i, g)
    n_chains = MLA_Q_TILES * n_heads
    step_rows = MLA_Q_TILES * MLA_TILE
    assert s % step_rows == 0
    score_buf = pltpu.VMEM((n_chains, MLA_TILE, MLA_TILE), F32)
    return pl.pallas_call(
        functools.partial(_mla_kernel, n_tiles=s // MLA_TILE),
        grid=(b, n_groups, s // step_rows),
        in_specs=[pl.BlockSpec((1, step_rows, n_heads * LANES), tile),
                  pl.BlockSpec((1, s, n_heads * LANES), whole_seq),
                  pl.BlockSpec((1, s, n_heads * MLA_V), whole_seq)],
        out_specs=pl.BlockSpec((1, step_rows, n_heads * MLA_V), tile),
        out_shape=jax.ShapeDtypeStruct((b, s, MLA_HEADS * MLA_V), BF16),
        scratch_shapes=[pltpu.VMEM((n_heads, MLA_V + ONES_ROWS, s), BF16),
                        pltpu.VMEM((n_chains, LANES, MLA_TILE), BF16),
                        score_buf, score_buf, score_buf,
                        *_SoftmaxState.scratch(n_chains, MLA_V, MLA_TILE)],
        compiler_params=_params(("parallel", "parallel", "arbitrary")),
        name="mla_attention",
    )(q, k, v)


def _swa_kernel(slopes_ref, sinks_ref, qkv_ref, o_ref, s_ref, kp_ref, vp_ref, bias_ref):
    q_width, kv_width = SWA_Q_HEADS * HEAD_DIM, SWA_KV_HEADS * HEAD_DIM
    q_ref = qkv_ref.at[:, :, :q_width]
    kc_ref = qkv_ref.at[:, :, q_width:q_width + kv_width]
    vc_ref = qkv_ref.at[:, :, q_width + kv_width:]
    n = pl.program_id(1)
    first_step = (pl.program_id(0) == 0) & (n == 0)
    save_last_block = _carry_previous_block(n, (kc_ref, vc_ref), (kp_ref, vp_ref), SWA_BLOCK)
    group = SWA_Q_HEADS // SWA_KV_HEADS
    logits = _band_bias(bias_ref, first_step, n, SWA_BLOCK, SWA_Q_HEADS,
                        lambda diff: (diff >= 0) & (diff < SWA_WINDOW), lambda h: slopes_ref[h] * LOG2E)

    def both_halves(prev_ref, cur_ref, band, kv_head):
        x = _band_keys(prev_ref, cur_ref, band, SWA_BLOCK, slice(None))
        part = x[:, kv_head * HEAD_DIM:(kv_head + 1) * HEAD_DIM]
        return jnp.concatenate([part, part], axis=1)

    n_bands = qkv_ref.shape[1] // SWA_BLOCK
    bands_kv = [(band, kv) for band in range(n_bands) for kv in range(SWA_KV_HEADS)]
    k_tiles = {bk: both_halves(kp_ref, kc_ref, *bk) for bk in bands_kv}
    v_tiles = {bk: both_halves(vp_ref, vc_ref, *bk) for bk in bands_kv}

    def sink(h):
        return sinks_ref[h] * LOG2E

    def store_pair(pair, band, o, lse):
        o_ref[0, _band_rows(band, SWA_BLOCK), pair * LANES:(pair + 1) * LANES] = o.astype(o_ref.dtype)

    _band_attention(
        n_bands, SWA_Q_HEADS,
        lambda h, band: q_ref[0, _band_rows(band, SWA_BLOCK), (h // 2) * LANES:(h // 2 + 1) * LANES],
        lambda h, band: k_tiles[band, h // group],
        lambda h, band: v_tiles[band, h // group],
        logits, sink, store_pair, s_ref, with_lse=False)
    save_last_block()


def _swa_attention(qkv, sinks, b, s):
    q_width = SWA_Q_HEADS * HEAD_DIM
    kv_width = SWA_KV_HEADS * HEAD_DIM
    assert s % SWA_BLOCK == 0
    n_bands = _bands_per_step(s // SWA_BLOCK)
    step_rows = n_bands * SWA_BLOCK
    rows = lambda bi, n: (bi, n, 0)
    previous_block = pltpu.VMEM((1, SWA_BLOCK, kv_width), BF16)
    return pl.pallas_call(
        _swa_kernel,
        grid=(b, s // step_rows),
        in_specs=[pl.BlockSpec(memory_space=pltpu.SMEM), pl.BlockSpec(memory_space=pltpu.SMEM),
                  pl.BlockSpec((1, step_rows, qkv.shape[-1]), rows)],
        out_specs=pl.BlockSpec((1, step_rows, q_width), rows),
        out_shape=jax.ShapeDtypeStruct((b, s, q_width), BF16),
        scratch_shapes=[pltpu.VMEM((n_bands * SWA_Q_HEADS, 2 * SWA_BLOCK, SWA_BLOCK), F32),
                        previous_block, previous_block,
                        pltpu.VMEM((SWA_Q_HEADS, 2, 2 * SWA_BLOCK, SWA_BLOCK), F32)],
        compiler_params=_params(("arbitrary", "arbitrary")),
        name="swa_attention",
    )(_alibi_slopes(SWA_Q_HEADS), sinks.astype(F32), qkv)


def kernel(x, l0_attn_norm, l0_w_qkv, l0_w_o, l0_mlp_norm, l0_w_up, l0_w_down, l1_attn_norm, l1_w_qkv, l1_w_o, l1_mlp_norm, l1_w_up, l1_w_down, l2_attn_norm, l2_w_dkv, l2_q_norm, l2_w_uq, l2_kv_norm, l2_w_ukv, l2_w_o, l2_mlp_norm, l2_w_up, l2_w_down, l3_attn_norm, l3_w_qkv, l3_sinks, l3_w_o, l3_mlp_norm, l3_w_up, l3_w_down, final_norm):
    b, s, d = x.shape
    bf = lambda w: w.astype(BF16)
    h = x.reshape(b * s, d)

    qkv = _norm_proj(h, l0_attn_norm, bf(l0_w_qkv), scaled_cols=MOBA_HEADS * HEAD_DIM, scale=HEAD_Q_SCALE)
    a = _moba_attention(qkv.reshape(b, s, -1), b, s).reshape(b * s, -1)
    h = _post(h, (a,), bf(l0_w_o), l0_mlp_norm, bf(l0_w_up), bf(l0_w_down))

    qkvs = _dil_proj(h, l1_attn_norm, bf(l1_w_qkv))
    groups = [_dilated_group(qkvs[g], b, s, g, window, dil) for g, (window, dil) in enumerate(DIL_PAIRS)]
    h = _post(h, tuple(o for o, _ in groups) + tuple(l for _, l in groups),
              bf(l1_w_o), l1_mlp_norm, bf(l1_w_up), bf(l1_w_down))

    q, k, v = _mla_proj(h, l2_attn_norm, l2_w_dkv, l2_q_norm, l2_w_uq, l2_kv_norm, l2_w_ukv, s)
    a = _mla_attention(q.reshape(b, s, -1), k.reshape(b, s, -1), v.reshape(b, s, -1), b, s).reshape(b * s, -1)
    h = _post(h, (a,), bf(l2_w_o), l2_mlp_norm, bf(l2_w_up), bf(l2_w_down))

    qkv = _norm_proj(h, l3_attn_norm, bf(l3_w_qkv), col_tile=256, scaled_cols=SWA_Q_HEADS * HEAD_DIM,
                     scale=HEAD_Q_SCALE)
    a = _swa_attention(qkv.reshape(b, s, -1), l3_sinks, b, s).reshape(b * s, -1)
    h = _post(h, (a,), bf(l3_w_o), l3_mlp_norm, bf(l3_w_up), bf(l3_w_down), g_final=final_norm)
    return h.reshape(b, s, d)
```

```python
import functools

import jax
import jax.numpy as jnp
from jax import lax
from jax.experimental import pallas as pl
from jax.experimental.pallas import tpu as pltpu

F32 = jnp.float32
BF16 = jnp.bfloat16

D_MODEL = 1024
HEAD_DIM = 64
RMS_EPS = 1e-6
D_FF = 4 * D_MODEL
NEG_INF = -1e30

MOBA_HEADS = 16
MOBA_BLOCK = 256
MOBA_TOPK = 3

DIL_PAIRS = ((128, 1), (512, 4), (2048, 16))
DIL_HEADS_PER_GROUP = 8
DIL_BAND = 128

MLA_HEADS = 16
MLA_Q_RANK = 768
MLA_KV_RANK = 256
MLA_NOPE = 64
MLA_ROPE = 32
MLA_V = 64
ROPE_THETA = 10000.0
MLA_TILE = 256
MLA_Q_TILES = 2
MLA_HEADS_PER_STEP = 8
MOBA_HEADS_PER_STEP = 8
MOBA_Q_TILES = 2

SWA_Q_HEADS = 16
SWA_KV_HEADS = 2
SWA_WINDOW = 128
SWA_BLOCK = 128

LANES = 128
ROW_TILE = 512
PROJ_ROW_TILE = 1024
POST_ROW_TILE = 1024
FF_TILE = 512
VMEM_LIMIT = 56 * 1024 * 1024
LOG2E = 1.4426950408889634
HEAD_Q_SCALE = HEAD_DIM ** -0.5 * LOG2E
MLA_Q_SCALE = (MLA_NOPE + MLA_ROPE) ** -0.5 * LOG2E


def _params(semantics):
    return pltpu.CompilerParams(dimension_semantics=semantics, vmem_limit_bytes=VMEM_LIMIT)


def _resident(shape):
    return pl.BlockSpec(shape, lambda *_: (0,) * len(shape), pipeline_mode=pl.Buffered(1))


def _alibi_slopes(n_heads):
    return 2.0 ** (-8.0 * jnp.arange(1, n_heads + 1, dtype=F32) / n_heads)


def _rms(x, g):
    return x * lax.rsqrt(jnp.mean(x * x, axis=-1, keepdims=True) + RMS_EPS) * g


def _dot(a, b):
    return jnp.dot(a, b, preferred_element_type=F32)


def _dot_nt(a, b):
    return lax.dot_general(a, b, (((1,), (1,)), ((), ())), preferred_element_type=F32)


def _norm_proj_kernel(h_ref, g_ref, w_ref, o_ref, *, col_tile, scaled_cols, scale):
    n = _rms(h_ref[...], g_ref[...]).astype(BF16)
    for c in range(o_ref.shape[1] // col_tile):
        cols = slice(c * col_tile, (c + 1) * col_tile)
        y = _dot(n, w_ref[:, cols])
        if (c + 1) * col_tile <= scaled_cols:
            y = y * scale
        o_ref[:, cols] = y.astype(o_ref.dtype)


def _norm_proj(h, g, w, col_tile=512, scaled_cols=0, scale=1.0):
    m, d = h.shape
    n_out = w.shape[1]
    assert scaled_cols % col_tile == 0
    return pl.pallas_call(
        functools.partial(_norm_proj_kernel, col_tile=col_tile, scaled_cols=scaled_cols, scale=scale),
        grid=(m // PROJ_ROW_TILE,),
        in_specs=[pl.BlockSpec((PROJ_ROW_TILE, d), lambda i: (i, 0)),
                  _resident((1, d)),
                  _resident((d, n_out))],
        out_specs=pl.BlockSpec((PROJ_ROW_TILE, n_out), lambda i: (i, 0)),
        out_shape=jax.ShapeDtypeStruct((m, n_out), BF16),
        compiler_params=_params(("parallel",)),
        name="norm_proj",
    )(h, g.reshape(1, d), w)


def _dil_proj_kernel(h_ref, g_ref, w_ref, o0_ref, o1_ref, o2_ref, stage_ref, sorted_ref):
    n = _rms(h_ref[...], g_ref[...])
    rows, d = n.shape
    width = DIL_HEADS_PER_GROUP * HEAD_DIM
    for t in range(d // LANES):
        stage_ref[t] = n[:, t * LANES:(t + 1) * LANES]
    n = n.astype(BF16)
    for g, (o_ref, (_, dil)) in enumerate(zip((o0_ref, o1_ref, o2_ref), DIL_PAIRS)):
        per = rows // dil
        for r in range(dil if dil > 1 else 0):
            for t in range(d // LANES):
                sorted_ref[g - 1, r * per:(r + 1) * per, t * LANES:(t + 1) * LANES] = (
                    stage_ref[t, pl.ds(r, per, stride=dil), :].astype(BF16))
        lhs = n if dil == 1 else sorted_ref[g - 1]
        for part in range(3):
            src = (3 * g + part) * width
            y = _dot(lhs, w_ref[:, src:src + width])
            if part == 0:
                y = y * HEAD_Q_SCALE
            for r in range(dil):
                dst = (3 * r + part) * width
                o_ref[:, dst:dst + width] = y[r * per:(r + 1) * per, :].astype(BF16)


def _dil_proj(h, g, w):
    m, d = h.shape
    feat = 3 * DIL_HEADS_PER_GROUP * HEAD_DIM
    assert DIL_PAIRS[0][1] == 1
    view = lambda dil: (m // dil, dil * feat)
    view_block = lambda dil: pl.BlockSpec((PROJ_ROW_TILE // dil, dil * feat), lambda i: (i, 0))
    return pl.pallas_call(
        _dil_proj_kernel,
        grid=(m // PROJ_ROW_TILE,),
        in_specs=[pl.BlockSpec((PROJ_ROW_TILE, d), lambda i: (i, 0)), _resident((1, d)), _resident(w.shape)],
        out_specs=[view_block(dil) for _, dil in DIL_PAIRS],
        out_shape=[jax.ShapeDtypeStruct(view(dil), BF16) for _, dil in DIL_PAIRS],
        scratch_shapes=[pltpu.VMEM((d // LANES, PROJ_ROW_TILE, LANES), F32),
                        pltpu.VMEM((len(DIL_PAIRS) - 1, PROJ_ROW_TILE, d), BF16)],
        compiler_params=_params(("parallel",)),
        name="dil_proj",
    )(h, g.reshape(1, d), w)


def _mlp_tail(h1, g_ref, wup_ref, wdn_ref, gf_ref, out_ref):
    n = _rms(h1, g_ref[...]).astype(BF16)
    acc = jnp.zeros_like(h1)
    for c in range(wup_ref.shape[1] // FF_TILE):
        cols = slice(c * FF_TILE, (c + 1) * FF_TILE)
        u = jnp.square(jnp.maximum(_dot(n, wup_ref[:, cols]), 0.0)).astype(BF16)
        acc = acc + _dot(u, wdn_ref[cols, :])
    out = h1 + acc
    if gf_ref is not None:
        out = _rms(out, gf_ref[...])
    out_ref[...] = out


def _post_kernel(h_ref, a_ref, wo_ref, g_ref, wup_ref, wdn_ref, *rest):
    gf_ref, out_ref = (rest[0], rest[1]) if len(rest) == 2 else (None, rest[0])
    h1 = h_ref[...] + _dot(a_ref[...], wo_ref[...])
    _mlp_tail(h1, g_ref, wup_ref, wdn_ref, gf_ref, out_ref)


def _post_merge_kernel(h_ref, o0_ref, o1_ref, o2_ref, l0_ref, l1_ref, l2_ref,
                       wo_ref, g_ref, wup_ref, wdn_ref, out_ref, stage_ref):
    rows, width = o0_ref.shape

    def token_rows(x_ref, slot, dil):
        if dil == 1:
            return x_ref[...]
        for r in range(dil):
            for t in range(width // LANES):
                src = r * width + t * LANES
                stage_ref[slot, t, pl.ds(r, rows // dil, stride=dil), :] = x_ref[:, src:src + LANES]
        return jnp.concatenate([stage_ref[slot, t] for t in range(width // LANES)], axis=1)

    dils = [dil for _, dil in DIL_PAIRS]
    l0, l1, l2 = (token_rows(ref, slot, dil) for slot, (ref, dil) in enumerate(zip((l0_ref, l1_ref, l2_ref), dils)))
    o0, o1, o2 = (token_rows(ref, 3 + slot, dil) for slot, (ref, dil) in enumerate(zip((o0_ref, o1_ref, o2_ref), dils)))
    mx = jnp.maximum(jnp.maximum(l0, l1), l2)
    e0, e1, e2 = jnp.exp(l0 - mx), jnp.exp(l1 - mx), jnp.exp(l2 - mx)
    merged = (e0 * o0 + e1 * o1 + e2 * o2) / (e0 + e1 + e2)
    h1 = h_ref[...] + _dot(merged.astype(BF16), wo_ref[...])
    _mlp_tail(h1, g_ref, wup_ref, wdn_ref, None, out_ref)


def _post(h, attn_ins, w_o, g, w_up, w_down, g_final=None):
    m, d = h.shape
    merge = len(attn_ins) > 1
    rows = ROW_TILE if merge else POST_ROW_TILE
    row = lambda width: pl.BlockSpec((rows, width), lambda i: (i, 0))
    in_specs = [row(d)] + [pl.BlockSpec((rows * a.shape[0] // m, a.shape[1]), lambda i: (i, 0)) for a in attn_ins]
    in_specs += [_resident(w_o.shape), _resident((1, d)), _resident(w_up.shape), _resident(w_down.shape)]
    args = [h, *attn_ins, w_o, g.reshape(1, d), w_up, w_down]
    if g_final is not None:
        in_specs.append(_resident((1, d)))
        args.append(g_final.reshape(1, d))
    scratch = [pltpu.VMEM((len(attn_ins), w_o.shape[0] // LANES, rows, LANES), F32)] if merge else []
    return pl.pallas_call(
        _post_merge_kernel if merge else _post_kernel,
        grid=(m // rows,),
        in_specs=in_specs,
        out_specs=row(d),
        out_shape=jax.ShapeDtypeStruct((m, d), F32),
        scratch_shapes=scratch,
        compiler_params=_params(("parallel",)),
        name="post_mlp",
    )(*args)


ONES_ROWS = 16
SKIP_SHIFT = -2.0 * NEG_INF


class _SoftmaxState:
    def __init__(self, m_ref, acc_ref, dv):
        self.m_ref, self.acc_ref, self.dv = m_ref, acc_ref, dv

    @staticmethod
    def scratch(n_heads, dv, tq):
        return [pltpu.VMEM((n_heads, 1, tq), F32), pltpu.VMEM((n_heads, dv + ONES_ROWS, tq), F32)]

    @staticmethod
    def store_values_t(vt_ref, hh, cols, v_t):
        dv = v_t.shape[0]
        vt_ref[hh, :dv, cols] = v_t.astype(BF16)
        vt_ref[hh, dv:, cols] = jnp.ones((ONES_ROWS, v_t.shape[1]), BF16)

    def init(self):
        self.m_ref[...] = jnp.full(self.m_ref.shape, NEG_INF, F32)
        self.acc_ref[...] = jnp.zeros(self.acc_ref.shape, F32)

    def step(self, hh, x, v_t, shift=None):
        m = self.m_ref[hh]
        m_cur = jnp.max(x, axis=0, keepdims=True)
        if shift is not None:
            m_cur = m_cur - shift
        m_new = jnp.maximum(m, m_cur)
        alpha = jnp.exp2(m - m_new)
        p = jnp.exp2(x - (m_new if shift is None else m_new + shift))
        self.m_ref[hh] = m_new
        self.acc_ref[hh] = alpha * self.acc_ref[hh] + _dot(v_t, p.astype(BF16))

    def normalized(self, hh):
        return self.acc_ref[hh, :self.dv, :] / self.acc_ref[hh, self.dv:self.dv + 1, :]


def _sweep_blocks(n_past, n_chains, state, score_fn, values_fn, past_logits, own_steps, s_a, s_b,
                  chain_lookahead=0):
    def produce(buf, block, chains=range(n_chains)):
        for c in chains:
            buf[c] = score_fn(c, block)

    def consume(buf, block):
        for c in range(n_chains):
            x, shift = past_logits(c, buf[c], block)
            state.step(c, x, values_fn(c, block), shift)

    produced = set()
    for buf, block, chains, _ in own_steps:
        produce(buf, block, [c for c in chains if (id(buf), c) not in produced])
        produced.update((id(buf), c) for c in chains)
    for c in range(n_chains):
        s_a[c] = score_fn(c, 0)
        for buf, block, chains, logits_fn in own_steps:
            if c in chains:
                x, shift = logits_fn(c, buf[c])
                state.step(c, x, values_fn(c, block), shift)

    def advance(src, block, dst, next_block):
        for c in range(min(chain_lookahead, n_chains)):
            dst[c] = score_fn(c, next_block)
        for c in range(n_chains):
            if c + chain_lookahead < n_chains:
                dst[c + chain_lookahead] = score_fn(c + chain_lookahead, next_block)
            x, shift = past_logits(c, src[c], block)
            state.step(c, x, values_fn(c, block), shift)

    def two_blocks(first):
        advance(s_a, first, s_b, first + 1)
        advance(s_b, first + 1, s_a, jnp.minimum(first + 2, n_past - 1))

    def quad(t, carry):
        two_blocks(4 * t)
        two_blocks(4 * t + 2)
        return carry

    def pair(t, carry):
        two_blocks(4 * (n_past // 4) + 2 * t)
        return carry

    lax.fori_loop(0, n_past // 4, quad, 0)
    lax.fori_loop(0, (n_past % 4) // 2, pair, 0)

    @pl.when(n_past % 2 == 1)
    def _():
        consume(s_a, n_past - 1)


def _transpose_bf16(x):
    return x.astype(F32).T.astype(BF16)


def _keep_head_rows(x_t, half):
    r = lax.broadcasted_iota(jnp.int32, x_t.shape, 0)
    keep = (r < HEAD_DIM) if half == 0 else (r >= HEAD_DIM)
    return jnp.where(keep, x_t, jnp.zeros_like(x_t))


def _bf16_pieces(x, n):
    pieces = []
    for _ in range(n):
        pieces.append(x.astype(BF16))
        x = x - pieces[-1].astype(F32)
    return pieces


F32_PIECES = 3


def _moba_kernel(slopes_ref, q_ref, k_ref, v_ref, o_ref, kmean_ref, vt_ref, term_ref, kpos_ref, qh_ref,
                 s_a, s_b, s_c, m_ref, acc_ref, *, n_blocks):
    blk = MOBA_BLOCK
    n_heads = MOBA_HEADS_PER_STEP
    group = pl.program_id(1)
    i = pl.program_id(2)
    state = _SoftmaxState(m_ref, acc_ref, HEAD_DIM)
    chain = lambda tile, hh: tile * n_heads + hh
    tile_chains = [[chain(tile, hh) for hh in range(n_heads)] for tile in range(MOBA_Q_TILES)]

    @pl.when(i == 0)
    def _():
        means = []
        pos = lax.broadcasted_iota(jnp.int32, (blk, LANES), 0)
        feature = lax.broadcasted_iota(jnp.int32, (blk, LANES), 1)
        key_pos = jnp.where(feature < F32_PIECES, pos, 0).astype(F32).astype(BF16)
        for j in range(n_blocks):
            rows = slice(j * blk, (j + 1) * blk)
            means.append(jnp.mean(k_ref[0, rows, :].astype(F32), axis=0, keepdims=True))
            for tile in range(n_heads // 2):
                kpos_ref[tile, rows, :LANES] = k_ref[0, rows, tile * LANES:(tile + 1) * LANES]
                kpos_ref[tile, rows, LANES:] = key_pos
            v_t = v_ref[0, rows, :].astype(F32).T
            for hh in range(n_heads):
                state.store_values_t(vt_ref, hh, rows, v_t[hh * HEAD_DIM:(hh + 1) * HEAD_DIM, :])
        for piece, part in enumerate(_bf16_pieces(jnp.concatenate(means, axis=0), F32_PIECES)):
            kmean_ref[piece * n_blocks:(piece + 1) * n_blocks, :] = part

    state.init()
    key = lax.broadcasted_iota(jnp.int32, (blk, blk), 0)
    qry = lax.broadcasted_iota(jnp.int32, (blk, blk), 1)
    blk_idx = lax.broadcasted_iota(jnp.int32, (n_blocks, blk), 0)
    feature_row = lax.broadcasted_iota(jnp.int32, (LANES, blk), 0)
    lane_tile = lambda hh: slice((hh // 2) * LANES, (hh // 2 + 1) * LANES)
    first_block = MOBA_Q_TILES * i

    for tile in range(MOBA_Q_TILES):
        own = first_block + tile
        q_t = q_ref[0, tile * blk:(tile + 1) * blk, :].astype(F32).T
        fully_past = blk_idx < own
        block_dist = ((own - blk_idx) * blk).astype(F32)
        for hh in range(n_heads):
            slope_s = slopes_ref[n_heads * group + hh] * LOG2E
            q_h = _keep_head_rows(q_t[lane_tile(hh), :], hh % 2).astype(BF16)
            pieces = _dot(kmean_ref[:, lane_tile(hh)], q_h)
            gate = pieces[:n_blocks] + pieces[n_blocks:2 * n_blocks] + pieces[2 * n_blocks:]
            work = jnp.where(fully_past, gate, NEG_INF * HEAD_Q_SCALE)
            chosen = jnp.zeros((n_blocks, blk), jnp.bool_)
            for _ in range(min(MOBA_TOPK, n_blocks)):
                best = jnp.max(work, axis=0, keepdims=True)
                first = jnp.min(jnp.where(work == best, blk_idx, n_blocks), axis=0, keepdims=True)
                pick = blk_idx == first
                chosen = jnp.logical_or(chosen, pick)
                work = jnp.where(pick, -jnp.inf, work)
            term_ref[chain(tile, hh)] = jnp.where(jnp.logical_and(chosen, fully_past), slope_s * block_dist,
                                                  SKIP_SHIFT)
            qh_ref[chain(tile, hh), :LANES, :] = q_h
            slope_rows = jnp.zeros((LANES, blk), F32)
            for piece, part in enumerate(_bf16_pieces(jnp.full((1, blk), slope_s, F32), F32_PIECES)):
                slope_rows = jnp.where(feature_row == piece, part.astype(F32), slope_rows)
            qh_ref[chain(tile, hh), LANES:, :] = slope_rows.astype(BF16)

    def rows_of(block):
        return pl.ds(pl.multiple_of(block * blk, blk), blk)

    def scores(c, block):
        return _dot(kpos_ref[(c % n_heads) // 2, rows_of(block), :], qh_ref[c])

    def v_t(c, block):
        return vt_ref[c % n_heads, :, rows_of(block)]

    def past_logits(c, s, block):
        return s, term_ref[c, pl.ds(block, 1), :]

    def own_logits(c, s):
        return jnp.where(key <= qry, s, NEG_INF), None

    assert MOBA_Q_TILES == 2
    own_steps = [(s_b, first_block, tile_chains[0], own_logits),
                 (s_b, first_block, tile_chains[1], functools.partial(past_logits, block=first_block)),
                 (s_c, first_block + 1, tile_chains[1], own_logits)]
    _sweep_blocks(first_block, MOBA_Q_TILES * n_heads, state, scores, v_t, past_logits, own_steps, s_a, s_b,
                  chain_lookahead=1)
    for tile in range(MOBA_Q_TILES):
        o_t = jnp.concatenate([state.normalized(c) for c in tile_chains[tile]], axis=0)
        o_ref[0, tile * blk:(tile + 1) * blk, :] = o_t.T.astype(o_ref.dtype)


def _moba_attention(qkv, b, s):
    n_blocks = s // MOBA_BLOCK
    n_heads = MOBA_HEADS_PER_STEP
    width = n_heads * HEAD_DIM
    n_groups = MOBA_HEADS // n_heads
    n_chains = MOBA_Q_TILES * n_heads
    step_rows = MOBA_Q_TILES * MOBA_BLOCK
    assert s % step_rows == 0
    score_buf = pltpu.VMEM((n_chains, MOBA_BLOCK, MOBA_BLOCK), F32)
    return pl.pallas_call(
        functools.partial(_moba_kernel, n_blocks=n_blocks),
        grid=(b, n_groups, s // step_rows),
        in_specs=[pl.BlockSpec(memory_space=pltpu.SMEM),
                  pl.BlockSpec((1, step_rows, width), lambda bi, g, i: (bi, i, g)),
                  pl.BlockSpec((1, s, width), lambda bi, g, i: (bi, 0, n_groups + g)),
                  pl.BlockSpec((1, s, width), lambda bi, g, i: (bi, 0, 2 * n_groups + g))],
        out_specs=pl.BlockSpec((1, step_rows, width), lambda bi, g, i: (bi, i, g)),
        out_shape=jax.ShapeDtypeStruct((b, s, MOBA_HEADS * HEAD_DIM), BF16),
        scratch_shapes=[pltpu.VMEM((F32_PIECES * n_blocks, width), BF16),
                        pltpu.VMEM((n_heads, HEAD_DIM + ONES_ROWS, s), BF16),
                        pltpu.VMEM((n_chains, n_blocks, MOBA_BLOCK), F32),
                        pltpu.VMEM((n_heads // 2, s, 2 * LANES), BF16),
                        pltpu.VMEM((n_chains, 2 * LANES, MOBA_BLOCK), BF16),
                        score_buf, score_buf, score_buf,
                        *_SoftmaxState.scratch(n_chains, HEAD_DIM, MOBA_BLOCK)],
        compiler_params=_params(("parallel", "parallel", "arbitrary")),
        name="moba_attention",
    )(_alibi_slopes(MOBA_HEADS), qkv, qkv, qkv)


def _lane_half(x, half):
    lane = lax.broadcasted_iota(jnp.int32, x.shape, x.ndim - 1)
    keep = (lane < HEAD_DIM) if half == 0 else (lane >= HEAD_DIM)
    return jnp.where(keep, x, jnp.zeros_like(x))


MAX_BANDS_PER_STEP = 8


def _bands_per_step(n_blocks):
    bands = MAX_BANDS_PER_STEP
    while n_blocks % bands:
        bands //= 2
    return bands


def _band_attention(n_bands, n_heads, q_tile, k_tile, v_tile, logits_fn, floor_fn, store_pair, s_ref, with_lse):
    chains = [(band, h) for band in range(n_bands) for h in range(n_heads)]

    def produce(c):
        band, h = chains[c]
        s_ref[c] = _dot_nt(k_tile(h, band), _lane_half(q_tile(h, band), h % 2))

    lookahead = 2
    for c in range(min(lookahead, len(chains))):
        produce(c)
    even = None
    for c, (band, h) in enumerate(chains):
        if c + lookahead < len(chains):
            produce(c + lookahead)
        x = logits_fn(h, band, s_ref[c])
        m = jnp.max(x, axis=0, keepdims=True)
        floor = floor_fn(h)
        if floor is not None:
            m = jnp.maximum(m, floor)
        p = jnp.exp2(x - m)
        den = jnp.sum(p, axis=0, keepdims=True)
        if floor is not None:
            den = den + jnp.exp2(floor - m)
        o_t = lax.dot_general(v_tile(h, band), p.astype(BF16), (((0,), (0,)), ((), ())),
                              preferred_element_type=F32)
        half = slice((h % 2) * HEAD_DIM, (h % 2 + 1) * HEAD_DIM)
        o_t = o_t[half, :] / den
        lse_t = jnp.broadcast_to((m + jnp.log2(den)) * (1.0 / LOG2E), o_t.shape) if with_lse else None
        if h % 2 == 0:
            even = (o_t, lse_t)
            continue
        store_pair(h // 2, band, jnp.concatenate([even[0], o_t], axis=0).T,
                   jnp.concatenate([even[1], lse_t], axis=0).T if with_lse else None)


def _band_rows(band, size):
    return slice(band * size, (band + 1) * size)


def _band_keys(prev_ref, cur_ref, band, size, cols):
    if band == 0:
        return jnp.concatenate([prev_ref[0, :, cols], cur_ref[0, :size, cols]], axis=0)
    return cur_ref[0, (band - 1) * size:(band + 1) * size, cols]


def _carry_previous_block(n, cur_refs, prev_refs, size):
    @pl.when(n == 0)
    def _():
        for prev in prev_refs:
            prev[...] = jnp.zeros(prev.shape, prev.dtype)

    def save():
        for cur, prev in zip(cur_refs, prev_refs):
            prev[0] = cur[0, cur.shape[1] - size:, :]
    return save


def _band_bias(bias_ref, first_step, n, size, n_heads, in_window, slope_fn):
    @pl.when(first_step)
    def _():
        key = lax.broadcasted_iota(jnp.int32, (2 * size, size), 0)
        qry = lax.broadcasted_iota(jnp.int32, (2 * size, size), 1)
        diff = qry + size - key
        window = in_window(diff)
        for h in range(n_heads):
            bias = slope_fn(h) * diff.astype(F32)
            bias_ref[h, 0] = jnp.where(window, bias, -NEG_INF)
            bias_ref[h, 1] = jnp.where(window & (key >= size), bias, -NEG_INF)

    def logits(h, band, s):
        variant = (n == 0).astype(jnp.int32) if band == 0 else 0
        return s - bias_ref[h, variant]
    return logits


def _dil_kernel(slopes_ref, qkv_ref, o_ref, lse_ref, s_ref, kp_ref, vp_ref, bias_ref, *, n_pts, dil, group):
    width = DIL_HEADS_PER_GROUP * HEAD_DIM
    n_bands = qkv_ref.shape[1] // DIL_BAND
    n_residues = qkv_ref.shape[2] // (3 * width)
    part_ref = lambda res, part: qkv_ref.at[:, :, (3 * res + part) * width:(3 * res + part + 1) * width]
    q_refs, k_refs, v_refs = ([part_ref(res, part) for res in range(n_residues)] for part in range(3))
    kp_refs = [kp_ref.at[res:res + 1] for res in range(n_residues)]
    vp_refs = [vp_ref.at[res:res + 1] for res in range(n_residues)]
    n = pl.program_id(2)
    first_step = (pl.program_id(0) == 0) & (pl.program_id(1) == 0) & (n == 0)
    save_last_blocks = _carry_previous_block(n, k_refs + v_refs, kp_refs + vp_refs, DIL_BAND)
    band_logits = _band_bias(bias_ref, first_step, n, DIL_BAND, DIL_HEADS_PER_GROUP,
                             lambda diff: (diff >= 0) & (diff <= n_pts),
                             lambda h: slopes_ref[group * DIL_HEADS_PER_GROUP + h] * (dil * LOG2E))
    tile = lambda h: slice((h // 2) * LANES, (h // 2 + 1) * LANES)
    unit = lambda u: divmod(u, n_bands)

    def store_pair(pair, u, o, lse):
        res, band = unit(u)
        rows, cols = _band_rows(band, DIL_BAND), slice(res * width + pair * LANES, res * width + (pair + 1) * LANES)
        o_ref[0, rows, cols] = o
        lse_ref[0, rows, cols] = lse

    _band_attention(
        n_residues * n_bands, DIL_HEADS_PER_GROUP,
        lambda h, u: q_refs[unit(u)[0]][0, _band_rows(unit(u)[1], DIL_BAND), tile(h)],
        lambda h, u: _band_keys(kp_refs[unit(u)[0]], k_refs[unit(u)[0]], unit(u)[1], DIL_BAND, tile(h)),
        lambda h, u: _band_keys(vp_refs[unit(u)[0]], v_refs[unit(u)[0]], unit(u)[1], DIL_BAND, tile(h)),
        lambda h, u, s: band_logits(h, unit(u)[1], s), lambda h: None, store_pair, s_ref, with_lse=True)
    save_last_blocks()


def _dilated_group(qkv, b, s, group, window, dil):
    width = DIL_HEADS_PER_GROUP * HEAD_DIM
    n_sub = s // dil
    assert n_sub % DIL_BAND == 0
    n_bands = _bands_per_step(n_sub // DIL_BAND)
    step_rows = n_bands * DIL_BAND
    n_residues = min(dil, MAX_BANDS_PER_STEP // n_bands)
    assert dil % n_residues == 0
    view = qkv.reshape(b, n_sub, dil * 3 * width)

    residues = lambda bi, r, n: (bi, n, r)
    previous_block = pltpu.VMEM((n_residues, DIL_BAND, width), BF16)
    out_spec = pl.BlockSpec((1, step_rows, n_residues * width), residues)
    out_sds = jax.ShapeDtypeStruct((b, n_sub, dil * width), F32)
    o, lse = pl.pallas_call(
        functools.partial(_dil_kernel, n_pts=window // dil, dil=dil, group=group),
        grid=(b, dil // n_residues, n_sub // step_rows),
        in_specs=[pl.BlockSpec(memory_space=pltpu.SMEM),
                  pl.BlockSpec((1, step_rows, n_residues * 3 * width), residues)],
        out_specs=[out_spec, out_spec],
        out_shape=[out_sds, out_sds],
        scratch_shapes=[pltpu.VMEM((n_residues * n_bands * DIL_HEADS_PER_GROUP, 2 * DIL_BAND, DIL_BAND), F32),
                        previous_block, previous_block,
                        pltpu.VMEM((DIL_HEADS_PER_GROUP, 2, 2 * DIL_BAND, DIL_BAND), F32)],
        compiler_params=_params(("arbitrary", "arbitrary", "arbitrary")),
        name=f"dilated_attention_g{group}",
    )(_alibi_slopes(len(DIL_PAIRS) * DIL_HEADS_PER_GROUP), view)
    return o.reshape(b * n_sub, dil * width), lse.reshape(b * n_sub, dil * width)


def _mla_proj_kernel(h_ref, g_ref, wd_ref, gq_ref, gkv_ref, wuq_ref, wuk_ref, wuv_ref,
                     cos_ref, sin_up_ref, sin_dn_ref, q_ref, k_ref, v_ref):
    n = _rms(h_ref[...], g_ref[...]).astype(BF16)
    c = _dot(n, wd_ref[...])
    nq = _rms(c[:, :MLA_Q_RANK], gq_ref[...]).astype(BF16)
    nkv = _rms(c[:, MLA_Q_RANK:MLA_Q_RANK + MLA_KV_RANK], gkv_ref[...]).astype(BF16)
    k_rope = c[:, MLA_Q_RANK + MLA_KV_RANK:]
    half = MLA_ROPE // 2
    pair = 2

    def rope(x, width):
        cos, sin_up, sin_dn = (jnp.tile(t[...], (1, width)) for t in (cos_ref, sin_up_ref, sin_dn_ref))
        return x * cos + pltpu.roll(x, half, 1) * sin_up + pltpu.roll(x, x.shape[1] - half, 1) * sin_dn

    kr = jnp.tile(rope(k_rope, 1), (1, pair))
    for h in range(0, MLA_HEADS, pair):
        cols = slice(h * LANES, (h + pair) * LANES)
        q_ref[:, cols] = (rope(_dot(nq, wuq_ref[:, cols]), pair) * MLA_Q_SCALE).astype(BF16)
        k_ref[:, cols] = (_dot(nkv, wuk_ref[:, cols]) + kr).astype(BF16)
    v_ref[...] = _dot(nkv, wuv_ref[...]).astype(BF16)


def _mla_proj(h, g, w_dkv, q_norm, w_uq, kv_norm, w_ukv, s):
    m, d = h.shape
    qk = MLA_NOPE + MLA_ROPE
    wd = jnp.concatenate([w_dkv[:, :MLA_Q_RANK + MLA_KV_RANK],
                          jnp.zeros((d, MLA_NOPE), F32), w_dkv[:, MLA_Q_RANK + MLA_KV_RANK:],
                          jnp.zeros((d, LANES - qk), F32)], axis=1).astype(BF16)
    wuq = jnp.pad(w_uq.reshape(MLA_Q_RANK, MLA_HEADS, qk), ((0, 0), (0, 0), (0, LANES - qk)))
    wuq = wuq.reshape(MLA_Q_RANK, MLA_HEADS * LANES).astype(BF16)
    w_ukv = w_ukv.reshape(MLA_KV_RANK, MLA_HEADS, MLA_NOPE + MLA_V)
    wuk = jnp.pad(w_ukv[:, :, :MLA_NOPE], ((0, 0), (0, 0), (0, LANES - MLA_NOPE)))
    wuk = wuk.reshape(MLA_KV_RANK, MLA_HEADS * LANES).astype(BF16)
    wuv = w_ukv[:, :, MLA_NOPE:].reshape(MLA_KV_RANK, MLA_HEADS * MLA_V).astype(BF16)
    half = MLA_ROPE // 2
    inv = ROPE_THETA ** (-jnp.arange(0, MLA_ROPE, 2, dtype=F32) / MLA_ROPE)
    ang = jnp.arange(s).astype(F32)[:, None] * inv[None, :]
    cos, sin = jnp.cos(ang), jnp.sin(ang)
    zeros = lambda w: jnp.zeros((s, w), F32)
    cos_t = jnp.concatenate([jnp.ones((s, MLA_NOPE), F32), cos, cos, zeros(LANES - qk)], axis=1)
    sin_up = jnp.concatenate([zeros(MLA_NOPE + half), sin, zeros(LANES - qk)], axis=1)
    sin_dn = jnp.concatenate([zeros(MLA_NOPE), -sin, zeros(half + LANES - qk)], axis=1)

    tiles_per_seq = s // PROJ_ROW_TILE
    row = lambda width: pl.BlockSpec((PROJ_ROW_TILE, width), lambda i: (i, 0))
    table = pl.BlockSpec((PROJ_ROW_TILE, LANES), lambda i: (i % tiles_per_seq, 0))
    return pl.pallas_call(
        _mla_proj_kernel,
        grid=(m // PROJ_ROW_TILE,),
        in_specs=[row(d), _resident((1, d)), _resident(wd.shape),
                  _resident((1, MLA_Q_RANK)), _resident((1, MLA_KV_RANK)),
                  _resident(wuq.shape), _resident(wuk.shape), _resident(wuv.shape),
                  table, table, table],
        out_specs=[row(MLA_HEADS * LANES), row(MLA_HEADS * LANES), row(MLA_HEADS * MLA_V)],
        out_shape=[jax.ShapeDtypeStruct((m, MLA_HEADS * LANES), BF16),
                   jax.ShapeDtypeStruct((m, MLA_HEADS * LANES), BF16),
                   jax.ShapeDtypeStruct((m, MLA_HEADS * MLA_V), BF16)],
        compiler_params=_params(("parallel",)),
        name="mla_proj",
    )(h, g.reshape(1, d), wd, q_norm.reshape(1, -1), kv_norm.reshape(1, -1), wuq, wuk, wuv,
      cos_t, sin_up, sin_dn)


def _mla_kernel(q_ref, k_ref, v_ref, o_ref, vt_ref, qh_ref, s_a, s_b, s_c, m_ref, acc_ref, *, n_tiles):
    t = MLA_TILE
    n_heads = MLA_HEADS_PER_STEP
    i = pl.program_id(2)
    state = _SoftmaxState(m_ref, acc_ref, MLA_V)
    chain = lambda tile, hh: tile * n_heads + hh
    tile_chains = [[chain(tile, hh) for hh in range(n_heads)] for tile in range(MLA_Q_TILES)]

    @pl.when(i == 0)
    def _():
        for j in range(n_tiles):
            rows = slice(j * t, (j + 1) * t)
            v_t = v_ref[0, rows, :].astype(F32).T
            for hh in range(n_heads):
                state.store_values_t(vt_ref, hh, rows, v_t[hh * MLA_V:(hh + 1) * MLA_V, :])

    state.init()
    key = lax.broadcasted_iota(jnp.int32, (t, t), 0)
    qry = lax.broadcasted_iota(jnp.int32, (t, t), 1)
    lane_tile = lambda hh: slice(hh * LANES, (hh + 1) * LANES)
    for tile in range(MLA_Q_TILES):
        for hh in range(n_heads):
            qh_ref[chain(tile, hh)] = _transpose_bf16(q_ref[0, tile * t:(tile + 1) * t, lane_tile(hh)])

    def rows_of(block):
        return pl.ds(pl.multiple_of(block * t, t), t)

    def scores(c, block):
        return _dot(k_ref[0, rows_of(block), lane_tile(c % n_heads)], qh_ref[c])

    def v_t(c, block):
        return vt_ref[c % n_heads, :, rows_of(block)]

    def past_logits(c, s, block):
        return s, None

    def causal(c, s):
        return jnp.where(key <= qry, s, NEG_INF), None

    def full(c, s):
        return s, None

    assert MLA_Q_TILES == 2
    first = MLA_Q_TILES * i
    own_steps = [(s_b, first, tile_chains[0], causal), (s_b, first, tile_chains[1], full),
                 (s_c, first + 1, tile_chains[1], causal)]
    _sweep_blocks(first, MLA_Q_TILES * n_heads, state, scores, v_t, past_logits, own_steps, s_a, s_b)
    for tile in range(MLA_Q_TILES):
        o_t = jnp.concatenate([state.normalized(c) for c in tile_chains[tile]], axis=0)
        o_ref[0, tile * t:(tile + 1) * t, :] = o_t.T.astype(o_ref.dtype)


def _mla_attention(q, k, v, b, s):
    n_heads = MLA_HEADS_PER_STEP
    n_groups = MLA_HEADS // n_heads
    whole_seq = lambda bi, g, i: (bi, 0, g)
    tile = lambda bi, g, i: (bi, i, g)
    n_chains = MLA_Q_TILES * n_heads
    step_rows = MLA_Q_TILES * MLA_TILE
    assert s % step_rows == 0
    score_buf = pltpu.VMEM((n_chains, MLA_TILE, MLA_TILE), F32)
    return pl.pallas_call(
        functools.partial(_mla_kernel, n_tiles=s // MLA_TILE),
        grid=(b, n_groups, s // step_rows),
        in_specs=[pl.BlockSpec((1, step_rows, n_heads * LANES), tile),
                  pl.BlockSpec((1, s, n_heads * LANES), whole_seq),
                  pl.BlockSpec((1, s, n_heads * MLA_V), whole_seq)],
        out_specs=pl.BlockSpec((1, step_rows, n_heads * MLA_V), tile),
        out_shape=jax.ShapeDtypeStruct((b, s, MLA_HEADS * MLA_V), BF16),
        scratch_shapes=[pltpu.VMEM((n_heads, MLA_V + ONES_ROWS, s), BF16),
                        pltpu.VMEM((n_chains, LANES, MLA_TILE), BF16),
                        score_buf, score_buf, score_buf,
                        *_SoftmaxState.scratch(n_chains, MLA_V, MLA_TILE)],
        compiler_params=_params(("parallel", "parallel", "arbitrary")),
        name="mla_attention",
    )(q, k, v)


def _swa_kernel(slopes_ref, sinks_ref, qkv_ref, o_ref, s_ref, kp_ref, vp_ref, bias_ref):
    q_width, kv_width = SWA_Q_HEADS * HEAD_DIM, SWA_KV_HEADS * HEAD_DIM
    q_ref = qkv_ref.at[:, :, :q_width]
    kc_ref = qkv_ref.at[:, :, q_width:q_width + kv_width]
    vc_ref = qkv_ref.at[:, :, q_width + kv_width:]
    n = pl.program_id(1)
    first_step = (pl.program_id(0) == 0) & (n == 0)
    save_last_block = _carry_previous_block(n, (kc_ref, vc_ref), (kp_ref, vp_ref), SWA_BLOCK)
    group = SWA_Q_HEADS // SWA_KV_HEADS
    logits = _band_bias(bias_ref, first_step, n, SWA_BLOCK, SWA_Q_HEADS,
                        lambda diff: (diff >= 0) & (diff < SWA_WINDOW), lambda h: slopes_ref[h] * LOG2E)

    def both_halves(prev_ref, cur_ref, band, kv_head):
        x = _band_keys(prev_ref, cur_ref, band, SWA_BLOCK, slice(None))
        part = x[:, kv_head * HEAD_DIM:(kv_head + 1) * HEAD_DIM]
        return jnp.concatenate([part, part], axis=1)

    n_bands = qkv_ref.shape[1] // SWA_BLOCK
    bands_kv = [(band, kv) for band in range(n_bands) for kv in range(SWA_KV_HEADS)]
    k_tiles = {bk: both_halves(kp_ref, kc_ref, *bk) for bk in bands_kv}
    v_tiles = {bk: both_halves(vp_ref, vc_ref, *bk) for bk in bands_kv}

    def sink(h):
        return sinks_ref[h] * LOG2E

    def store_pair(pair, band, o, lse):
        o_ref[0, _band_rows(band, SWA_BLOCK), pair * LANES:(pair + 1) * LANES] = o.astype(o_ref.dtype)

    _band_attention(
        n_bands, SWA_Q_HEADS,
        lambda h, band: q_ref[0, _band_rows(band, SWA_BLOCK), (h // 2) * LANES:(h // 2 + 1) * LANES],
        lambda h, band: k_tiles[band, h // group],
        lambda h, band: v_tiles[band, h // group],
        logits, sink, store_pair, s_ref, with_lse=False)
    save_last_block()


def _swa_attention(qkv, sinks, b, s):
    q_width = SWA_Q_HEADS * HEAD_DIM
    kv_width = SWA_KV_HEADS * HEAD_DIM
    assert s % SWA_BLOCK == 0
    n_bands = _bands_per_step(s // SWA_BLOCK)
    step_rows = n_bands * SWA_BLOCK
    rows = lambda bi, n: (bi, n, 0)
    previous_block = pltpu.VMEM((1, SWA_BLOCK, kv_width), BF16)
    return pl.pallas_call(
        _swa_kernel,
        grid=(b, s // step_rows),
        in_specs=[pl.BlockSpec(memory_space=pltpu.SMEM), pl.BlockSpec(memory_space=pltpu.SMEM),
                  pl.BlockSpec((1, step_rows, qkv.shape[-1]), rows)],
        out_specs=pl.BlockSpec((1, step_rows, q_width), rows),
        out_shape=jax.ShapeDtypeStruct((b, s, q_width), BF16),
        scratch_shapes=[pltpu.VMEM((n_bands * SWA_Q_HEADS, 2 * SWA_BLOCK, SWA_BLOCK), F32),
                        previous_block, previous_block,
                        pltpu.VMEM((SWA_Q_HEADS, 2, 2 * SWA_BLOCK, SWA_BLOCK), F32)],
        compiler_params=_params(("arbitrary", "arbitrary")),
        name="swa_attention",
    )(_alibi_slopes(SWA_Q_HEADS), sinks.astype(F32), qkv)


def kernel(x, l0_attn_norm, l0_w_qkv, l0_w_o, l0_mlp_norm, l0_w_up, l0_w_down, l1_attn_norm, l1_w_qkv, l1_w_o, l1_mlp_norm, l1_w_up, l1_w_down, l2_attn_norm, l2_w_dkv, l2_q_norm, l2_w_uq, l2_kv_norm, l2_w_ukv, l2_w_o, l2_mlp_norm, l2_w_up, l2_w_down, l3_attn_norm, l3_w_qkv, l3_sinks, l3_w_o, l3_mlp_norm, l3_w_up, l3_w_down, final_norm):
    b, s, d = x.shape
    bf = lambda w: w.astype(BF16)
    h = x.reshape(b * s, d)

    qkv = _norm_proj(h, l0_attn_norm, bf(l0_w_qkv), scaled_cols=MOBA_HEADS * HEAD_DIM, scale=HEAD_Q_SCALE)
    a = _moba_attention(qkv.reshape(b, s, -1), b, s).reshape(b * s, -1)
    h = _post(h, (a,), bf(l0_w_o), l0_mlp_norm, bf(l0_w_up), bf(l0_w_down))

    qkvs = _dil_proj(h, l1_attn_norm, bf(l1_w_qkv))
    groups = [_dilated_group(qkvs[g], b, s, g, window, dil) for g, (window, dil) in enumerate(DIL_PAIRS)]
    h = _post(h, tuple(o for o, _ in groups) + tuple(l for _, l in groups),
              bf(l1_w_o), l1_mlp_norm, bf(l1_w_up), bf(l1_w_down))

    q, k, v = _mla_proj(h, l2_attn_norm, l2_w_dkv, l2_q_norm, l2_w_uq, l2_kv_norm, l2_w_ukv, s)
    a = _mla_attention(q.reshape(b, s, -1), k.reshape(b, s, -1), v.reshape(b, s, -1), b, s).reshape(b * s, -1)
    h = _post(h, (a,), bf(l2_w_o), l2_mlp_norm, bf(l2_w_up), bf(l2_w_down))

    qkv = _norm_proj(h, l3_attn_norm, bf(l3_w_qkv), col_tile=256, scaled_cols=SWA_Q_HEADS * HEAD_DIM,
                     scale=HEAD_Q_SCALE)
    a = _swa_attention(qkv.reshape(b, s, -1), l3_sinks, b, s).reshape(b * s, -1)
    h = _post(h, (a,), bf(l3_w_o), l3_mlp_norm, bf(l3_w_up), bf(l3_w_down), g_final=final_norm)
    return h.reshape(b, s, d)
```

```python
import functools

import jax
import jax.numpy as jnp
from jax import lax
from jax.experimental import pallas as pl
from jax.experimental.pallas import tpu as pltpu

F32 = jnp.float32
BF16 = jnp.bfloat16

D_MODEL = 1024
HEAD_DIM = 64
RMS_EPS = 1e-6
D_FF = 4 * D_MODEL
NEG_INF = -1e30

MOBA_HEADS = 16
MOBA_BLOCK = 256
MOBA_TOPK = 3

DIL_PAIRS = ((128, 1), (512, 4), (2048, 16))
DIL_HEADS_PER_GROUP = 8
DIL_BAND = 128

MLA_HEADS = 16
MLA_Q_RANK = 768
MLA_KV_RANK = 256
MLA_NOPE = 64
MLA_ROPE = 32
MLA_V = 64
ROPE_THETA = 10000.0
MLA_TILE = 256
MLA_Q_TILES = 2
MLA_HEADS_PER_STEP = 8
MOBA_HEADS_PER_STEP = 8
MOBA_Q_TILES = 2

SWA_Q_HEADS = 16
SWA_KV_HEADS = 2
SWA_WINDOW = 128
SWA_BLOCK = 128

LANES = 128
ROW_TILE = 512
PROJ_ROW_TILE = 1024
POST_ROW_TILE = 1024
FF_TILE = 512
VMEM_LIMIT = 56 * 1024 * 1024
LOG2E = 1.4426950408889634
HEAD_Q_SCALE = HEAD_DIM ** -0.5 * LOG2E
MLA_Q_SCALE = (MLA_NOPE + MLA_ROPE) ** -0.5 * LOG2E


def _params(semantics):
    return pltpu.CompilerParams(dimension_semantics=semantics, vmem_limit_bytes=VMEM_LIMIT)


def _resident(shape):
    return pl.BlockSpec(shape, lambda *_: (0,) * len(shape), pipeline_mode=pl.Buffered(1))


def _alibi_slopes(n_heads):
    return 2.0 ** (-8.0 * jnp.arange(1, n_heads + 1, dtype=F32) / n_heads)


def _rms(x, g):
    return x * lax.rsqrt(jnp.mean(x * x, axis=-1, keepdims=True) + RMS_EPS) * g


def _dot(a, b):
    return jnp.dot(a, b, preferred_element_type=F32)


def _dot_nt(a, b):
    return lax.dot_general(a, b, (((1,), (1,)), ((), ())), preferred_element_type=F32)


def _norm_proj_kernel(h_ref, g_ref, w_ref, o_ref, *, col_tile, scaled_cols, scale):
    n = _rms(h_ref[...], g_ref[...]).astype(BF16)
    for c in range(o_ref.shape[1] // col_tile):
        cols = slice(c * col_tile, (c + 1) * col_tile)
        y = _dot(n, w_ref[:, cols])
        if (c + 1) * col_tile <= scaled_cols:
            y = y * scale
        o_ref[:, cols] = y.astype(o_ref.dtype)


def _norm_proj(h, g, w, col_tile=512, scaled_cols=0, scale=1.0):
    m, d = h.shape
    n_out = w.shape[1]
    assert scaled_cols % col_tile == 0
    return pl.pallas_call(
        functools.partial(_norm_proj_kernel, col_tile=col_tile, scaled_cols=scaled_cols, scale=scale),
        grid=(m // PROJ_ROW_TILE,),
        in_specs=[pl.BlockSpec((PROJ_ROW_TILE, d), lambda i: (i, 0)),
                  _resident((1, d)),
                  _resident((d, n_out))],
        out_specs=pl.BlockSpec((PROJ_ROW_TILE, n_out), lambda i: (i, 0)),
        out_shape=jax.ShapeDtypeStruct((m, n_out), BF16),
        compiler_params=_params(("parallel",)),
        name="norm_proj",
    )(h, g.reshape(1, d), w)


def _dil_proj_kernel(h_ref, g_ref, w_ref, o0_ref, o1_ref, o2_ref, stage_ref, sorted_ref):
    n = _rms(h_ref[...], g_ref[...])
    rows, d = n.shape
    width = DIL_HEADS_PER_GROUP * HEAD_DIM
    for t in range(d // LANES):
        stage_ref[t] = n[:, t * LANES:(t + 1) * LANES]
    n = n.astype(BF16)
    for g, (o_ref, (_, dil)) in enumerate(zip((o0_ref, o1_ref, o2_ref), DIL_PAIRS)):
        per = rows // dil
        for r in range(dil if dil > 1 else 0):
            for t in range(d // LANES):
                sorted_ref[g - 1, r * per:(r + 1) * per, t * LANES:(t + 1) * LANES] = (
                    stage_ref[t, pl.ds(r, per, stride=dil), :].astype(BF16))
        lhs = n if dil == 1 else sorted_ref[g - 1]
        for part in range(3):
            src = (3 * g + part) * width
            y = _dot(lhs, w_ref[:, src:src + width])
            if part == 0:
                y = y * HEAD_Q_SCALE
            for r in range(dil):
                dst = (3 * r + part) * width
                o_ref[:, dst:dst + width] = y[r * per:(r + 1) * per, :].astype(BF16)


def _dil_proj(h, g, w):
    m, d = h.shape
    feat = 3 * DIL_HEADS_PER_GROUP * HEAD_DIM
    assert DIL_PAIRS[0][1] == 1
    view = lambda dil: (m // dil, dil * feat)
    view_block = lambda dil: pl.BlockSpec((PROJ_ROW_TILE // dil, dil * feat), lambda i: (i, 0))
    return pl.pallas_call(
        _dil_proj_kernel,
        grid=(m // PROJ_ROW_TILE,),
        in_specs=[pl.BlockSpec((PROJ_ROW_TILE, d), lambda i: (i, 0)), _resident((1, d)), _resident(w.shape)],
        out_specs=[view_block(dil) for _, dil in DIL_PAIRS],
        out_shape=[jax.ShapeDtypeStruct(view(dil), BF16) for _, dil in DIL_PAIRS],
        scratch_shapes=[pltpu.VMEM((d // LANES, PROJ_ROW_TILE, LANES), F32),
                        pltpu.VMEM((len(DIL_PAIRS) - 1, PROJ_ROW_TILE, d), BF16)],
        compiler_params=_params(("parallel",)),
        name="dil_proj",
    )(h, g.reshape(1, d), w)


def _mlp_tail(h1, g_ref, wup_ref, wdn_ref, gf_ref, out_ref):
    n = _rms(h1, g_ref[...]).astype(BF16)
    acc = jnp.zeros_like(h1)
    for c in range(wup_ref.shape[1] // FF_TILE):
        cols = slice(c * FF_TILE, (c + 1) * FF_TILE)
        u = jnp.square(jnp.maximum(_dot(n, wup_ref[:, cols]), 0.0)).astype(BF16)
        acc = acc + _dot(u, wdn_ref[cols, :])
    out = h1 + acc
    if gf_ref is not None:
        out = _rms(out, gf_ref[...])
    out_ref[...] = out


def _post_kernel(h_ref, a_ref, wo_ref, g_ref, wup_ref, wdn_ref, *rest):
    gf_ref, out_ref = (rest[0], rest[1]) if len(rest) == 2 else (None, rest[0])
    h1 = h_ref[...] + _dot(a_ref[...], wo_ref[...])
    _mlp_tail(h1, g_ref, wup_ref, wdn_ref, gf_ref, out_ref)


def _post_merge_kernel(h_ref, o0_ref, o1_ref, o2_ref, l0_ref, l1_ref, l2_ref,
                       wo_ref, g_ref, wup_ref, wdn_ref, out_ref, stage_ref):
    rows, width = o0_ref.shape

    def token_rows(x_ref, slot, dil):
        if dil == 1:
            return x_ref[...]
        for r in range(dil):
            for t in range(width // LANES):
                src = r * width + t * LANES
                stage_ref[slot, t, pl.ds(r, rows // dil, stride=dil), :] = x_ref[:, src:src + LANES].astype(F32)
        return jnp.concatenate([stage_ref[slot, t] for t in range(width // LANES)], axis=1)

    dils = [dil for _, dil in DIL_PAIRS]
    l0, l1, l2 = (token_rows(ref, slot, dil) for slot, (ref, dil) in enumerate(zip((l0_ref, l1_ref, l2_ref), dils)))
    o0, o1, o2 = (token_rows(ref, 3 + slot, dil) for slot, (ref, dil) in enumerate(zip((o0_ref, o1_ref, o2_ref), dils)))
    mx = jnp.maximum(jnp.maximum(l0, l1), l2)
    e0, e1, e2 = jnp.exp(l0 - mx), jnp.exp(l1 - mx), jnp.exp(l2 - mx)
    merged = (e0 * o0 + e1 * o1 + e2 * o2) / (e0 + e1 + e2)
    h1 = h_ref[...] + _dot(merged.astype(BF16), wo_ref[...])
    _mlp_tail(h1, g_ref, wup_ref, wdn_ref, None, out_ref)


def _post(h, attn_ins, w_o, g, w_up, w_down, g_final=None):
    m, d = h.shape
    merge = len(attn_ins) > 1
    rows = ROW_TILE if merge else POST_ROW_TILE
    row = lambda width: pl.BlockSpec((rows, width), lambda i: (i, 0))
    in_specs = [row(d)] + [pl.BlockSpec((rows * a.shape[0] // m, a.shape[1]), lambda i: (i, 0)) for a in attn_ins]
    in_specs += [_resident(w_o.shape), _resident((1, d)), _resident(w_up.shape), _resident(w_down.shape)]
    args = [h, *attn_ins, w_o, g.reshape(1, d), w_up, w_down]
    if g_final is not None:
        in_specs.append(_resident((1, d)))
        args.append(g_final.reshape(1, d))
    scratch = [pltpu.VMEM((len(attn_ins), w_o.shape[0] // LANES, rows, LANES), F32)] if merge else []
    return pl.pallas_call(
        _post_merge_kernel if merge else _post_kernel,
        grid=(m // rows,),
        in_specs=in_specs,
        out_specs=row(d),
        out_shape=jax.ShapeDtypeStruct((m, d), F32),
        scratch_shapes=scratch,
        compiler_params=_params(("parallel",)),
        name="post_mlp",
    )(*args)


ONES_ROWS = 16
SKIP_SHIFT = -2.0 * NEG_INF


class _SoftmaxState:
    def __init__(self, m_ref, acc_ref, dv):
        self.m_ref, self.acc_ref, self.dv = m_ref, acc_ref, dv

    @staticmethod
    def scratch(n_heads, dv, tq):
        return [pltpu.VMEM((n_heads, 1, tq), F32), pltpu.VMEM((n_heads, dv + ONES_ROWS, tq), F32)]

    @staticmethod
    def store_values_t(vt_ref, hh, cols, v_t):
        dv = v_t.shape[0]
        vt_ref[hh, :dv, cols] = v_t.astype(BF16)
        vt_ref[hh, dv:, cols] = jnp.ones((ONES_ROWS, v_t.shape[1]), BF16)

    def init(self):
        self.m_ref[...] = jnp.full(self.m_ref.shape, NEG_INF, F32)
        self.acc_ref[...] = jnp.zeros(self.acc_ref.shape, F32)

    def step(self, hh, x, v_t, shift=None):
        m = self.m_ref[hh]
        m_cur = jnp.max(x, axis=0, keepdims=True)
        if shift is not None:
            m_cur = m_cur - shift
        m_new = jnp.maximum(m, m_cur)
        alpha = jnp.exp2(m - m_new)
        p = jnp.exp2(x - (m_new if shift is None else m_new + shift))
        self.m_ref[hh] = m_new
        self.acc_ref[hh] = alpha * self.acc_ref[hh] + _dot(v_t, p.astype(BF16))

    def normalized(self, hh):
        return self.acc_ref[hh, :self.dv, :] / self.acc_ref[hh, self.dv:self.dv + 1, :]


def _sweep_blocks(n_past, n_chains, state, score_fn, values_fn, past_logits, own_steps, s_a, s_b,
                  chain_lookahead=0):
    def produce(buf, block, chains=range(n_chains)):
        for c in chains:
            buf[c] = score_fn(c, block)

    def consume(buf, block):
        for c in range(n_chains):
            x, shift = past_logits(c, buf[c], block)
            state.step(c, x, values_fn(c, block), shift)

    produced = set()
    for buf, block, chains, _ in own_steps:
        produce(buf, block, [c for c in chains if (id(buf), c) not in produced])
        produced.update((id(buf), c) for c in chains)
    for c in range(n_chains):
        s_a[c] = score_fn(c, 0)
        for buf, block, chains, logits_fn in own_steps:
            if c in chains:
                x, shift = logits_fn(c, buf[c])
                state.step(c, x, values_fn(c, block), shift)

    def advance(src, block, dst, next_block):
        for c in range(min(chain_lookahead, n_chains)):
            dst[c] = score_fn(c, next_block)
        for c in range(n_chains):
            if c + chain_lookahead < n_chains:
                dst[c + chain_lookahead] = score_fn(c + chain_lookahead, next_block)
            x, shift = past_logits(c, src[c], block)
            state.step(c, x, values_fn(c, block), shift)

    def two_blocks(first):
        advance(s_a, first, s_b, first + 1)
        advance(s_b, first + 1, s_a, jnp.minimum(first + 2, n_past - 1))

    def quad(t, carry):
        two_blocks(4 * t)
        two_blocks(4 * t + 2)
        return carry

    def pair(t, carry):
        two_blocks(4 * (n_past // 4) + 2 * t)
        return carry

    lax.fori_loop(0, n_past // 4, quad, 0)
    lax.fori_loop(0, (n_past % 4) // 2, pair, 0)

    @pl.when(n_past % 2 == 1)
    def _():
        consume(s_a, n_past - 1)


def _transpose_bf16(x):
    return x.astype(F32).T.astype(BF16)


def _keep_head_rows(x_t, half):
    r = lax.broadcasted_iota(jnp.int32, x_t.shape, 0)
    keep = (r < HEAD_DIM) if half == 0 else (r >= HEAD_DIM)
    return jnp.where(keep, x_t, jnp.zeros_like(x_t))


def _bf16_pieces(x, n):
    pieces = []
    for _ in range(n):
        pieces.append(x.astype(BF16))
        x = x - pieces[-1].astype(F32)
    return pieces


F32_PIECES = 3


def _moba_kernel(slopes_ref, q_ref, k_ref, v_ref, o_ref, kmean_ref, vt_ref, term_ref, kpos_ref, qh_ref,
                 s_a, s_b, s_c, m_ref, acc_ref, *, n_blocks):
    blk = MOBA_BLOCK
    n_heads = MOBA_HEADS_PER_STEP
    group = pl.program_id(1)
    i = pl.program_id(2)
    state = _SoftmaxState(m_ref, acc_ref, HEAD_DIM)
    chain = lambda tile, hh: tile * n_heads + hh
    tile_chains = [[chain(tile, hh) for hh in range(n_heads)] for tile in range(MOBA_Q_TILES)]

    @pl.when(i == 0)
    def _():
        means = []
        pos = lax.broadcasted_iota(jnp.int32, (blk, LANES), 0)
        feature = lax.broadcasted_iota(jnp.int32, (blk, LANES), 1)
        key_pos = jnp.where(feature < F32_PIECES, pos, 0).astype(F32).astype(BF16)
        for j in range(n_blocks):
            rows = slice(j * blk, (j + 1) * blk)
            means.append(jnp.mean(k_ref[0, rows, :].astype(F32), axis=0, keepdims=True))
            for tile in range(n_heads // 2):
                kpos_ref[tile, rows, :LANES] = k_ref[0, rows, tile * LANES:(tile + 1) * LANES]
                kpos_ref[tile, rows, LANES:] = key_pos
            v_t = v_ref[0, rows, :].astype(F32).T
            for hh in range(n_heads):
                state.store_values_t(vt_ref, hh, rows, v_t[hh * HEAD_DIM:(hh + 1) * HEAD_DIM, :])
        for piece, part in enumerate(_bf16_pieces(jnp.concatenate(means, axis=0), F32_PIECES)):
            kmean_ref[piece * n_blocks:(piece + 1) * n_blocks, :] = part

    state.init()
    key = lax.broadcasted_iota(jnp.int32, (blk, blk), 0)
    qry = lax.broadcasted_iota(jnp.int32, (blk, blk), 1)
    blk_idx = lax.broadcasted_iota(jnp.int32, (n_blocks, blk), 0)
    feature_row = lax.broadcasted_iota(jnp.int32, (LANES, blk), 0)
    lane_tile = lambda hh: slice((hh // 2) * LANES, (hh // 2 + 1) * LANES)
    first_block = MOBA_Q_TILES * i

    for tile in range(MOBA_Q_TILES):
        own = first_block + tile
        q_t = q_ref[0, tile * blk:(tile + 1) * blk, :].astype(F32).T
        fully_past = blk_idx < own
        block_dist = ((own - blk_idx) * blk).astype(F32)
        for hh in range(n_heads):
            slope_s = slopes_ref[n_heads * group + hh] * LOG2E
            q_h = _keep_head_rows(q_t[lane_tile(hh), :], hh % 2).astype(BF16)
            pieces = _dot(kmean_ref[:, lane_tile(hh)], q_h)
            gate = pieces[:n_blocks] + pieces[n_blocks:2 * n_blocks] + pieces[2 * n_blocks:]
            work = jnp.where(fully_past, gate, NEG_INF * HEAD_Q_SCALE)
            chosen = jnp.zeros((n_blocks, blk), jnp.bool_)
            for _ in range(min(MOBA_TOPK, n_blocks)):
                best = jnp.max(work, axis=0, keepdims=True)
                first = jnp.min(jnp.where(work == best, blk_idx, n_blocks), axis=0, keepdims=True)
                pick = blk_idx == first
                chosen = jnp.logical_or(chosen, pick)
                work = jnp.where(pick, -jnp.inf, work)
            term_ref[chain(tile, hh)] = jnp.where(jnp.logical_and(chosen, fully_past), slope_s * block_dist,
                                                  SKIP_SHIFT)
            qh_ref[chain(tile, hh), :LANES, :] = q_h
            slope_rows = jnp.zeros((LANES, blk), F32)
            for piece, part in enumerate(_bf16_pieces(jnp.full((1, blk), slope_s, F32), F32_PIECES)):
                slope_rows = jnp.where(feature_row == piece, part.astype(F32), slope_rows)
            qh_ref[chain(tile, hh), LANES:, :] = slope_rows.astype(BF16)

    def rows_of(block):
        return pl.ds(pl.multiple_of(block * blk, blk), blk)

    def scores(c, block):
        return _dot(kpos_ref[(c % n_heads) // 2, rows_of(block), :], qh_ref[c])

    def v_t(c, block):
        return vt_ref[c % n_heads, :, rows_of(block)]

    def past_logits(c, s, block):
        return s, term_ref[c, pl.ds(block, 1), :]

    def own_logits(c, s):
        return jnp.where(key <= qry, s, NEG_INF), None

    assert MOBA_Q_TILES == 2
    own_steps = [(s_b, first_block, tile_chains[0], own_logits),
                 (s_b, first_block, tile_chains[1], functools.partial(past_logits, block=first_block)),
                 (s_c, first_block + 1, tile_chains[1], own_logits)]
    _sweep_blocks(first_block, MOBA_Q_TILES * n_heads, state, scores, v_t, past_logits, own_steps, s_a, s_b,
                  chain_lookahead=1)
    for tile in range(MOBA_Q_TILES):
        o_t = jnp.concatenate([state.normalized(c) for c in tile_chains[tile]], axis=0)
        o_ref[0, tile * blk:(tile + 1) * blk, :] = o_t.T.astype(o_ref.dtype)


def _moba_attention(qkv, b, s):
    n_blocks = s // MOBA_BLOCK
    n_heads = MOBA_HEADS_PER_STEP
    width = n_heads * HEAD_DIM
    n_groups = MOBA_HEADS // n_heads
    n_chains = MOBA_Q_TILES * n_heads
    step_rows = MOBA_Q_TILES * MOBA_BLOCK
    assert s % step_rows == 0
    score_buf = pltpu.VMEM((n_chains, MOBA_BLOCK, MOBA_BLOCK), F32)
    return pl.pallas_call(
        functools.partial(_moba_kernel, n_blocks=n_blocks),
        grid=(b, n_groups, s // step_rows),
        in_specs=[pl.BlockSpec(memory_space=pltpu.SMEM),
                  pl.BlockSpec((1, step_rows, width), lambda bi, g, i: (bi, i, g)),
                  pl.BlockSpec((1, s, width), lambda bi, g, i: (bi, 0, n_groups + g)),
                  pl.BlockSpec((1, s, width), lambda bi, g, i: (bi, 0, 2 * n_groups + g))],
        out_specs=pl.BlockSpec((1, step_rows, width), lambda bi, g, i: (bi, i, g)),
        out_shape=jax.ShapeDtypeStruct((b, s, MOBA_HEADS * HEAD_DIM), BF16),
        scratch_shapes=[pltpu.VMEM((F32_PIECES * n_blocks, width), BF16),
                        pltpu.VMEM((n_heads, HEAD_DIM + ONES_ROWS, s), BF16),
                        pltpu.VMEM((n_chains, n_blocks, MOBA_BLOCK), F32),
                        pltpu.VMEM((n_heads // 2, s, 2 * LANES), BF16),
                        pltpu.VMEM((n_chains, 2 * LANES, MOBA_BLOCK), BF16),
                        score_buf, score_buf, score_buf,
                        *_SoftmaxState.scratch(n_chains, HEAD_DIM, MOBA_BLOCK)],
        compiler_params=_params(("parallel", "parallel", "arbitrary")),
        name="moba_attention",
    )(_alibi_slopes(MOBA_HEADS), qkv, qkv, qkv)


def _lane_half(x, half):
    lane = lax.broadcasted_iota(jnp.int32, x.shape, x.ndim - 1)
    keep = (lane < HEAD_DIM) if half == 0 else (lane >= HEAD_DIM)
    return jnp.where(keep, x, jnp.zeros_like(x))


MAX_BANDS_PER_STEP = 8


def _bands_per_step(n_blocks):
    bands = MAX_BANDS_PER_STEP
    while n_blocks % bands:
        bands //= 2
    return bands


def _band_attention(n_bands, n_heads, q_tile, k_tile, v_tile, logits_fn, floor_fn, store_pair, s_ref, with_lse):
    chains = [(band, h) for band in range(n_bands) for h in range(n_heads)]

    def produce(c):
        band, h = chains[c]
        s_ref[c] = _dot_nt(k_tile(h, band), _lane_half(q_tile(h, band), h % 2))

    lookahead = 2
    for c in range(min(lookahead, len(chains))):
        produce(c)
    even = None
    for c, (band, h) in enumerate(chains):
        if c + lookahead < len(chains):
            produce(c + lookahead)
        x = logits_fn(h, band, s_ref[c])
        m = jnp.max(x, axis=0, keepdims=True)
        floor = floor_fn(h)
        if floor is not None:
            m = jnp.maximum(m, floor)
        p = jnp.exp2(x - m)
        den = jnp.sum(p, axis=0, keepdims=True)
        if floor is not None:
            den = den + jnp.exp2(floor - m)
        o_t = lax.dot_general(v_tile(h, band), p.astype(BF16), (((0,), (0,)), ((), ())),
                              preferred_element_type=F32)
        half = slice((h % 2) * HEAD_DIM, (h % 2 + 1) * HEAD_DIM)
        o_t = o_t[half, :] / den
        lse_t = jnp.broadcast_to((m + jnp.log2(den)) * (1.0 / LOG2E), o_t.shape) if with_lse else None
        if h % 2 == 0:
            even = (o_t, lse_t)
            continue
        store_pair(h // 2, band, jnp.concatenate([even[0], o_t], axis=0).T,
                   jnp.concatenate([even[1], lse_t], axis=0).T if with_lse else None)


def _band_rows(band, size):
    return slice(band * size, (band + 1) * size)


def _band_keys(prev_ref, cur_ref, band, size, cols):
    if band == 0:
        return jnp.concatenate([prev_ref[0, :, cols], cur_ref[0, :size, cols]], axis=0)
    return cur_ref[0, (band - 1) * size:(band + 1) * size, cols]


def _carry_previous_block(n, cur_refs, prev_refs, size):
    @pl.when(n == 0)
    def _():
        for prev in prev_refs:
            prev[...] = jnp.zeros(prev.shape, prev.dtype)

    def save():
        for cur, prev in zip(cur_refs, prev_refs):
            prev[0] = cur[0, cur.shape[1] - size:, :]
    return save


def _band_bias(bias_ref, first_step, n, size, n_heads, in_window, slope_fn):
    @pl.when(first_step)
    def _():
        key = lax.broadcasted_iota(jnp.int32, (2 * size, size), 0)
        qry = lax.broadcasted_iota(jnp.int32, (2 * size, size), 1)
        diff = qry + size - key
        window = in_window(diff)
        for h in range(n_heads):
            bias = slope_fn(h) * diff.astype(F32)
            bias_ref[h, 0] = jnp.where(window, bias, -NEG_INF)
            bias_ref[h, 1] = jnp.where(window & (key >= size), bias, -NEG_INF)

    def logits(h, band, s):
        variant = (n == 0).astype(jnp.int32) if band == 0 else 0
        return s - bias_ref[h, variant]
    return logits


def _dil_kernel(slopes_ref, qkv_ref, o_ref, lse_ref, s_ref, kp_ref, vp_ref, bias_ref, *, n_pts, dil, group):
    width = DIL_HEADS_PER_GROUP * HEAD_DIM
    n_bands = qkv_ref.shape[1] // DIL_BAND
    n_residues = qkv_ref.shape[2] // (3 * width)
    part_ref = lambda res, part: qkv_ref.at[:, :, (3 * res + part) * width:(3 * res + part + 1) * width]
    q_refs, k_refs, v_refs = ([part_ref(res, part) for res in range(n_residues)] for part in range(3))
    kp_refs = [kp_ref.at[res:res + 1] for res in range(n_residues)]
    vp_refs = [vp_ref.at[res:res + 1] for res in range(n_residues)]
    n = pl.program_id(2)
    first_step = (pl.program_id(0) == 0) & (pl.program_id(1) == 0) & (n == 0)
    save_last_blocks = _carry_previous_block(n, k_refs + v_refs, kp_refs + vp_refs, DIL_BAND)
    band_logits = _band_bias(bias_ref, first_step, n, DIL_BAND, DIL_HEADS_PER_GROUP,
                             lambda diff: (diff >= 0) & (diff <= n_pts),
                             lambda h: slopes_ref[group * DIL_HEADS_PER_GROUP + h] * (dil * LOG2E))
    tile = lambda h: slice((h // 2) * LANES, (h // 2 + 1) * LANES)
    unit = lambda u: divmod(u, n_bands)

    def store_pair(pair, u, o, lse):
        res, band = unit(u)
        rows, cols = _band_rows(band, DIL_BAND), slice(res * width + pair * LANES, res * width + (pair + 1) * LANES)
        o_ref[0, rows, cols] = o.astype(o_ref.dtype)
        lse_ref[0, rows, cols] = lse

    _band_attention(
        n_residues * n_bands, DIL_HEADS_PER_GROUP,
        lambda h, u: q_refs[unit(u)[0]][0, _band_rows(unit(u)[1], DIL_BAND), tile(h)],
        lambda h, u: _band_keys(kp_refs[unit(u)[0]], k_refs[unit(u)[0]], unit(u)[1], DIL_BAND, tile(h)),
        lambda h, u: _band_keys(vp_refs[unit(u)[0]], v_refs[unit(u)[0]], unit(u)[1], DIL_BAND, tile(h)),
        lambda h, u, s: band_logits(h, unit(u)[1], s), lambda h: None, store_pair, s_ref, with_lse=True)
    save_last_blocks()


def _dilated_group(qkv, b, s, group, window, dil):
    width = DIL_HEADS_PER_GROUP * HEAD_DIM
    n_sub = s // dil
    assert n_sub % DIL_BAND == 0
    n_bands = _bands_per_step(n_sub // DIL_BAND)
    step_rows = n_bands * DIL_BAND
    n_residues = min(dil, MAX_BANDS_PER_STEP // n_bands)
    assert dil % n_residues == 0
    view = qkv.reshape(b, n_sub, dil * 3 * width)

    residues = lambda bi, r, n: (bi, n, r)
    previous_block = pltpu.VMEM((n_residues, DIL_BAND, width), BF16)
    out_spec = pl.BlockSpec((1, step_rows, n_residues * width), residues)
    out_sds = jax.ShapeDtypeStruct((b, n_sub, dil * width), F32)
    o, lse = pl.pallas_call(
        functools.partial(_dil_kernel, n_pts=window // dil, dil=dil, group=group),
        grid=(b, dil // n_residues, n_sub // step_rows),
        in_specs=[pl.BlockSpec(memory_space=pltpu.SMEM),
                  pl.BlockSpec((1, step_rows, n_residues * 3 * width), residues)],
        out_specs=[out_spec, out_spec],
        out_shape=[jax.ShapeDtypeStruct(out_sds.shape, BF16), out_sds],
        scratch_shapes=[pltpu.VMEM((n_residues * n_bands * DIL_HEADS_PER_GROUP, 2 * DIL_BAND, DIL_BAND), F32),
                        previous_block, previous_block,
                        pltpu.VMEM((DIL_HEADS_PER_GROUP, 2, 2 * DIL_BAND, DIL_BAND), F32)],
        compiler_params=_params(("arbitrary", "arbitrary", "arbitrary")),
        name=f"dilated_attention_g{group}",
    )(_alibi_slopes(len(DIL_PAIRS) * DIL_HEADS_PER_GROUP), view)
    return o.reshape(b * n_sub, dil * width), lse.reshape(b * n_sub, dil * width)


def _mla_proj_kernel(h_ref, g_ref, wd_ref, gq_ref, gkv_ref, wuq_ref, wuk_ref, wuv_ref,
                     cos_ref, sin_up_ref, sin_dn_ref, q_ref, k_ref, v_ref):
    n = _rms(h_ref[...], g_ref[...]).astype(BF16)
    c = _dot(n, wd_ref[...])
    nq = _rms(c[:, :MLA_Q_RANK], gq_ref[...]).astype(BF16)
    nkv = _rms(c[:, MLA_Q_RANK:MLA_Q_RANK + MLA_KV_RANK], gkv_ref[...]).astype(BF16)
    k_rope = c[:, MLA_Q_RANK + MLA_KV_RANK:]
    half = MLA_ROPE // 2
    pair = 2

    def rope(x, width):
        cos, sin_up, sin_dn = (jnp.tile(t[...], (1, width)) for t in (cos_ref, sin_up_ref, sin_dn_ref))
        return x * cos + pltpu.roll(x, half, 1) * sin_up + pltpu.roll(x, x.shape[1] - half, 1) * sin_dn

    kr = jnp.tile(rope(k_rope, 1), (1, pair))
    for h in range(0, MLA_HEADS, pair):
        cols = slice(h * LANES, (h + pair) * LANES)
        q_ref[:, cols] = (rope(_dot(nq, wuq_ref[:, cols]), pair) * MLA_Q_SCALE).astype(BF16)
        k_ref[:, cols] = (_dot(nkv, wuk_ref[:, cols]) + kr).astype(BF16)
    v_ref[...] = _dot(nkv, wuv_ref[...]).astype(BF16)


def _mla_proj(h, g, w_dkv, q_norm, w_uq, kv_norm, w_ukv, s):
    m, d = h.shape
    qk = MLA_NOPE + MLA_ROPE
    wd = jnp.concatenate([w_dkv[:, :MLA_Q_RANK + MLA_KV_RANK],
                          jnp.zeros((d, MLA_NOPE), F32), w_dkv[:, MLA_Q_RANK + MLA_KV_RANK:],
                          jnp.zeros((d, LANES - qk), F32)], axis=1).astype(BF16)
    wuq = jnp.pad(w_uq.reshape(MLA_Q_RANK, MLA_HEADS, qk), ((0, 0), (0, 0), (0, LANES - qk)))
    wuq = wuq.reshape(MLA_Q_RANK, MLA_HEADS * LANES).astype(BF16)
    w_ukv = w_ukv.reshape(MLA_KV_RANK, MLA_HEADS, MLA_NOPE + MLA_V)
    wuk = jnp.pad(w_ukv[:, :, :MLA_NOPE], ((0, 0), (0, 0), (0, LANES - MLA_NOPE)))
    wuk = wuk.reshape(MLA_KV_RANK, MLA_HEADS * LANES).astype(BF16)
    wuv = w_ukv[:, :, MLA_NOPE:].reshape(MLA_KV_RANK, MLA_HEADS * MLA_V).astype(BF16)
    half = MLA_ROPE // 2
    inv = ROPE_THETA ** (-jnp.arange(0, MLA_ROPE, 2, dtype=F32) / MLA_ROPE)
    ang = jnp.arange(s).astype(F32)[:, None] * inv[None, :]
    cos, sin = jnp.cos(ang), jnp.sin(ang)
    zeros = lambda w: jnp.zeros((s, w), F32)
    cos_t = jnp.concatenate([jnp.ones((s, MLA_NOPE), F32), cos, cos, zeros(LANES - qk)], axis=1)
    sin_up = jnp.concatenate([zeros(MLA_NOPE + half), sin, zeros(LANES - qk)], axis=1)
    sin_dn = jnp.concatenate([zeros(MLA_NOPE), -sin, zeros(half + LANES - qk)], axis=1)

    tiles_per_seq = s // PROJ_ROW_TILE
    row = lambda width: pl.BlockSpec((PROJ_ROW_TILE, width), lambda i: (i, 0))
    table = pl.BlockSpec((PROJ_ROW_TILE, LANES), lambda i: (i % tiles_per_seq, 0))
    return pl.pallas_call(
        _mla_proj_kernel,
        grid=(m // PROJ_ROW_TILE,),
        in_specs=[row(d), _resident((1, d)), _resident(wd.shape),
                  _resident((1, MLA_Q_RANK)), _resident((1, MLA_KV_RANK)),
                  _resident(wuq.shape), _resident(wuk.shape), _resident(wuv.shape),
                  table, table, table],
        out_specs=[row(MLA_HEADS * LANES), row(MLA_HEADS * LANES), row(MLA_HEADS * MLA_V)],
        out_shape=[jax.ShapeDtypeStruct((m, MLA_HEADS * LANES), BF16),
                   jax.ShapeDtypeStruct((m, MLA_HEADS * LANES), BF16),
                   jax.ShapeDtypeStruct((m, MLA_HEADS * MLA_V), BF16)],
        compiler_params=_params(("parallel",)),
        name="mla_proj",
    )(h, g.reshape(1, d), wd, q_norm.reshape(1, -1), kv_norm.reshape(1, -1), wuq, wuk, wuv,
      cos_t, sin_up, sin_dn)


def _mla_kernel(q_ref, k_ref, v_ref, o_ref, vt_ref, qh_ref, s_a, s_b, s_c, m_ref, acc_ref, *, n_tiles):
    t = MLA_TILE
    n_heads = MLA_HEADS_PER_STEP
    i = pl.program_id(2)
    state = _SoftmaxState(m_ref, acc_ref, MLA_V)
    chain = lambda tile, hh: tile * n_heads + hh
    tile_chains = [[chain(tile, hh) for hh in range(n_heads)] for tile in range(MLA_Q_TILES)]

    @pl.when(i == 0)
    def _():
        for j in range(n_tiles):
            rows = slice(j * t, (j + 1) * t)
            v_t = v_ref[0, rows, :].astype(F32).T
            for hh in range(n_heads):
                state.store_values_t(vt_ref, hh, rows, v_t[hh * MLA_V:(hh + 1) * MLA_V, :])

    state.init()
    key = lax.broadcasted_iota(jnp.int32, (t, t), 0)
    qry = lax.broadcasted_iota(jnp.int32, (t, t), 1)
    lane_tile = lambda hh: slice(hh * LANES, (hh + 1) * LANES)
    for tile in range(MLA_Q_TILES):
        for hh in range(n_heads):
            qh_ref[chain(tile, hh)] = _transpose_bf16(q_ref[0, tile * t:(tile + 1) * t, lane_tile(hh)])

    def rows_of(block):
        return pl.ds(pl.multiple_of(block * t, t), t)

    def scores(c, block):
        return _dot(k_ref[0, rows_of(block), lane_tile(c % n_heads)], qh_ref[c])

    def v_t(c, block):
        return vt_ref[c % n_heads, :, rows_of(block)]

    def past_logits(c, s, block):
        return s, None

    def causal(c, s):
        return jnp.where(key <= qry, s, NEG_INF), None

    def full(c, s):
        return s, None

    assert MLA_Q_TILES == 2
    first = MLA_Q_TILES * i
    own_steps = [(s_b, first, tile_chains[0], causal), (s_b, first, tile_chains[1], full),
                 (s_c, first + 1, tile_chains[1], causal)]
    _sweep_blocks(first, MLA_Q_TILES * n_heads, state, scores, v_t, past_logits, own_steps, s_a, s_b)
    for tile in range(MLA_Q_TILES):
        o_t = jnp.concatenate([state.normalized(c) for c in tile_chains[tile]], axis=0)
        o_ref[0, tile * t:(tile + 1) * t, :] = o_t.T.astype(o_ref.dtype)


def _mla_attention(q, k, v, b, s):
    n_heads = MLA_HEADS_PER_STEP
    n_groups = MLA_HEADS // n_heads
    whole_seq = lambda bi, g, i: (bi, 0, g)
    tile = lambda bi, g, i: (bi, i, g)
    n_chains = MLA_Q_TILES * n_heads
    step_rows = MLA_Q_TILES * MLA_TILE
    assert s % step_rows == 0
    score_buf = pltpu.VMEM((n_chains, MLA_TILE, MLA_TILE), F32)
    return pl.pallas_call(
        functools.partial(_mla_kernel, n_tiles=s // MLA_TILE),
        grid=(b, n_groups, s // step_rows),
        in_specs=[pl.BlockSpec((1, step_rows, n_heads * LANES), tile),
                  pl.BlockSpec((1, s, n_heads * LANES), whole_seq),
                  pl.BlockSpec((1, s, n_heads * MLA_V), whole_seq)],
        out_specs=pl.BlockSpec((1, step_rows, n_heads * MLA_V), tile),
        out_shape=jax.ShapeDtypeStruct((b, s, MLA_HEADS * MLA_V), BF16),
        scratch_shapes=[pltpu.VMEM((n_heads, MLA_V + ONES_ROWS, s), BF16),
                        pltpu.VMEM((n_chains, LANES, MLA_TILE), BF16),
                        score_buf, score_buf, score_buf,
                        *_SoftmaxState.scratch(n_chains, MLA_V, MLA_TILE)],
        compiler_params=_params(("parallel", "parallel", "arbitrary")),
        name="mla_attention",
    )(q, k, v)


def _swa_kernel(slopes_ref, sinks_ref, qkv_ref, o_ref, s_ref, kp_ref, vp_ref, bias_ref):
    q_width, kv_width = SWA_Q_HEADS * HEAD_DIM, SWA_KV_HEADS * HEAD_DIM
    q_ref = qkv_ref.at[:, :, :q_width]
    kc_ref = qkv_ref.at[:, :, q_width:q_width + kv_width]
    vc_ref = qkv_ref.at[:, :, q_width + kv_width:]
    n = pl.program_id(1)
    first_step = (pl.program_id(0) == 0) & (n == 0)
    save_last_block = _carry_previous_block(n, (kc_ref, vc_ref), (kp_ref, vp_ref), SWA_BLOCK)
    group = SWA_Q_HEADS // SWA_KV_HEADS
    logits = _band_bias(bias_ref, first_step, n, SWA_BLOCK, SWA_Q_HEADS,
                        lambda diff: (diff >= 0) & (diff < SWA_WINDOW), lambda h: slopes_ref[h] * LOG2E)

    def both_halves(prev_ref, cur_ref, band, kv_head):
        x = _band_keys(prev_ref, cur_ref, band, SWA_BLOCK, slice(None))
        part = x[:, kv_head * HEAD_DIM:(kv_head + 1) * HEAD_DIM]
        return jnp.concatenate([part, part], axis=1)

    n_bands = qkv_ref.shape[1] // SWA_BLOCK
    bands_kv = [(band, kv) for band in range(n_bands) for kv in range(SWA_KV_HEADS)]
    k_tiles = {bk: both_halves(kp_ref, kc_ref, *bk) for bk in bands_kv}
    v_tiles = {bk: both_halves(vp_ref, vc_ref, *bk) for bk in bands_kv}

    def sink(h):
        return sinks_ref[h] * LOG2E

    def store_pair(pair, band, o, lse):
        o_ref[0, _band_rows(band, SWA_BLOCK), pair * LANES:(pair + 1) * LANES] = o.astype(o_ref.dtype)

    _band_attention(
        n_bands, SWA_Q_HEADS,
        lambda h, band: q_ref[0, _band_rows(band, SWA_BLOCK), (h // 2) * LANES:(h // 2 + 1) * LANES],
        lambda h, band: k_tiles[band, h // group],
        lambda h, band: v_tiles[band, h // group],
        logits, sink, store_pair, s_ref, with_lse=False)
    save_last_block()


def _swa_attention(qkv, sinks, b, s):
    q_width = SWA_Q_HEADS * HEAD_DIM
    kv_width = SWA_KV_HEADS * HEAD_DIM
    assert s % SWA_BLOCK == 0
    n_bands = _bands_per_step(s // SWA_BLOCK)
    step_rows = n_bands * SWA_BLOCK
    rows = lambda bi, n: (bi, n, 0)
    previous_block = pltpu.VMEM((1, SWA_BLOCK, kv_width), BF16)
    return pl.pallas_call(
        _swa_kernel,
        grid=(b, s // step_rows),
        in_specs=[pl.BlockSpec(memory_space=pltpu.SMEM), pl.BlockSpec(memory_space=pltpu.SMEM),
                  pl.BlockSpec((1, step_rows, qkv.shape[-1]), rows)],
        out_specs=pl.BlockSpec((1, step_rows, q_width), rows),
        out_shape=jax.ShapeDtypeStruct((b, s, q_width), BF16),
        scratch_shapes=[pltpu.VMEM((n_bands * SWA_Q_HEADS, 2 * SWA_BLOCK, SWA_BLOCK), F32),
                        previous_block, previous_block,
                        pltpu.VMEM((SWA_Q_HEADS, 2, 2 * SWA_BLOCK, SWA_BLOCK), F32)],
        compiler_params=_params(("arbitrary", "arbitrary")),
        name="swa_attention",
    )(_alibi_slopes(SWA_Q_HEADS), sinks.astype(F32), qkv)


def kernel(x, l0_attn_norm, l0_w_qkv, l0_w_o, l0_mlp_norm, l0_w_up, l0_w_down, l1_attn_norm, l1_w_qkv, l1_w_o, l1_mlp_norm, l1_w_up, l1_w_down, l2_attn_norm, l2_w_dkv, l2_q_norm, l2_w_uq, l2_kv_norm, l2_w_ukv, l2_w_o, l2_mlp_norm, l2_w_up, l2_w_down, l3_attn_norm, l3_w_qkv, l3_sinks, l3_w_o, l3_mlp_norm, l3_w_up, l3_w_down, final_norm):
    b, s, d = x.shape
    bf = lambda w: w.astype(BF16)
    h = x.reshape(b * s, d)

    qkv = _norm_proj(h, l0_attn_norm, bf(l0_w_qkv), scaled_cols=MOBA_HEADS * HEAD_DIM, scale=HEAD_Q_SCALE)
    a = _moba_attention(qkv.reshape(b, s, -1), b, s).reshape(b * s, -1)
    h = _post(h, (a,), bf(l0_w_o), l0_mlp_norm, bf(l0_w_up), bf(l0_w_down))

    qkvs = _dil_proj(h, l1_attn_norm, bf(l1_w_qkv))
    groups = [_dilated_group(qkvs[g], b, s, g, window, dil) for g, (window, dil) in enumerate(DIL_PAIRS)]
    h = _post(h, tuple(o for o, _ in groups) + tuple(l for _, l in groups),
              bf(l1_w_o), l1_mlp_norm, bf(l1_w_up), bf(l1_w_down))

    q, k, v = _mla_proj(h, l2_attn_norm, l2_w_dkv, l2_q_norm, l2_w_uq, l2_kv_norm, l2_w_ukv, s)
    a = _mla_attention(q.reshape(b, s, -1), k.reshape(b, s, -1), v.reshape(b, s, -1), b, s).reshape(b * s, -1)
    h = _post(h, (a,), bf(l2_w_o), l2_mlp_norm, bf(l2_w_up), bf(l2_w_down))

    qkv = _norm_proj(h, l3_attn_norm, bf(l3_w_qkv), col_tile=256, scaled_cols=SWA_Q_HEADS * HEAD_DIM,
                     scale=HEAD_Q_SCALE)
    a = _swa_attention(qkv.reshape(b, s, -1), l3_sinks, b, s).reshape(b * s, -1)
    h = _post(h, (a,), bf(l3_w_o), l3_mlp_norm, bf(l3_w_up), bf(l3_w_down), g_final=final_norm)
    return h.reshape(b, s, d)
```
